```python
import jax, jax.numpy as jnp
from jax import lax
import numpy as np

D_MODEL = 1024
BATCH = 4
SEQ = 4096
DEPTH = 1
DEC_BATCH = 8
DEC_SEQ = 8192
PAST_LEN = 128

A_WIDTH = D_MODEL
A_GROUPS = 8
A_GROUP_CH = A_WIDTH // A_GROUPS
CHUNK = 128
HEAD_DIM = 128
N_Q_HEADS = D_MODEL // HEAD_DIM
N_KV_HEADS = 2
WINDOW = 128
BLOCK = 128
ROPE_THETA = 10000.0
Q_W = N_Q_HEADS * HEAD_DIM
KV_W = N_KV_HEADS * HEAD_DIM
IN_W = 2 * A_WIDTH + Q_W + 2 * KV_W + 2 * D_MODEL
SPLITS = (A_WIDTH, 2 * A_WIDTH, 2 * A_WIDTH + Q_W, 2 * A_WIDTH + Q_W + KV_W,
          2 * A_WIDTH + Q_W + 2 * KV_W, 2 * A_WIDTH + Q_W + 2 * KV_W + D_MODEL)
N_GROUPS = 4
EXPERTS_PER_GROUP = 8
N_EXPERTS = N_GROUPS * EXPERTS_PER_GROUP
TOP_K = 2
D_EXPERT = 512
MOE_BLOCK = 128
EPS = 1e-6
NEG = -1e30

kernel_name = "hybrid_sgmlp_swa_hmoe_encoder"


def rms_norm(x, g):
    xf = x.astype(jnp.float32)
    y = xf * lax.rsqrt(jnp.mean(xf * xf, axis=-1, keepdims=True) + EPS)
    return (y * g.astype(jnp.float32)).astype(x.dtype)


def apply_rope(x, pos):
    half = HEAD_DIM // 2
    inv_freq = ROPE_THETA ** (-jnp.arange(half, dtype=jnp.float32) / half)
    ang = pos.astype(jnp.float32)[:, None] * inv_freq[None, :]
    cos = jnp.cos(ang)[None, :, None, :]
    sin = jnp.sin(ang)[None, :, None, :]
    xf = x.astype(jnp.float32)
    x1, x2 = xf[..., :half], xf[..., half:]
    return jnp.concatenate([x1 * cos - x2 * sin, x2 * cos + x1 * sin], axis=-1).astype(x.dtype)


def spatial_gating_mixer(u, v, norm_v_g, w_spatial, b_spatial):
    B, S, _ = u.shape
    u = jax.nn.gelu(u)
    v = rms_norm(jax.nn.gelu(v), norm_v_g)
    vc = v.reshape(B, S // CHUNK, CHUNK, A_GROUPS, A_GROUP_CH)
    mixed = jnp.einsum('gpq,bnqgc->bnpgc', w_spatial, vc) + b_spatial.T[None, None, :, :, None]
    return u * mixed.reshape(B, S, A_WIDTH)


def window_attention(q, k, v, q_norm_g, k_norm_g, sink):
    B, S, _ = q.shape
    rep = N_Q_HEADS // N_KV_HEADS
    pos = jnp.arange(S)
    q = apply_rope(rms_norm(q.reshape(B, S, N_Q_HEADS, HEAD_DIM), q_norm_g), pos)
    k = apply_rope(rms_norm(k.reshape(B, S, N_KV_HEADS, HEAD_DIM), k_norm_g), pos)
    v = v.reshape(B, S, N_KV_HEADS, HEAD_DIM)
    n_blk = S // BLOCK
    qb = q.reshape(B, n_blk, BLOCK, N_KV_HEADS, rep, HEAD_DIM).transpose(1, 0, 2, 3, 4, 5)
    pad = ((0, 0), (BLOCK, BLOCK), (0, 0), (0, 0))
    kp = jnp.pad(k, pad)
    vp = jnp.pad(v, pad)
    q_off = jnp.arange(BLOCK)
    k_off = jnp.arange(3 * BLOCK) - BLOCK
    band = jnp.abs(k_off[None, :] - q_off[:, None]) <= WINDOW
    scale = HEAD_DIM ** -0.5
    sink_l = jnp.broadcast_to(sink.astype(jnp.float32).reshape(1, N_KV_HEADS, rep, 1, 1),
                              (B, N_KV_HEADS, rep, BLOCK, 1))

    def one_block(args):
        qi, i = args
        ki = lax.dynamic_slice_in_dim(kp, i * BLOCK, 3 * BLOCK, axis=1)
        vi = lax.dynamic_slice_in_dim(vp, i * BLOCK, 3 * BLOCK, axis=1)
        kpos = i * BLOCK + k_off
        valid = band & ((kpos >= 0) & (kpos < S))[None, :]
        s = jnp.einsum('bqgrd,bkgd->bgrqk', qi, ki).astype(jnp.float32) * scale
        s = jnp.where(valid, s, NEG)
        p = jax.nn.softmax(jnp.concatenate([s, sink_l], axis=-1), axis=-1)[..., :-1]
        o = jnp.einsum('bgrqk,bkgd->bqgrd', p.astype(vi.dtype), vi)
        return o.reshape(B, BLOCK, Q_W)

    out = lax.map(one_block, (qb, jnp.arange(n_blk)))
    return out.transpose(1, 0, 2, 3).reshape(B, S, Q_W)


def hierarchical_moe(h, w_rg, b_rg, w_re, b_re, w_gate, w_up, w_down):
    B, S, D = h.shape
    T = B * S
    xf = h.reshape(T, D)
    g_logits = (xf @ w_rg).astype(jnp.float32) + b_rg.astype(jnp.float32)
    g_prob = jax.nn.softmax(g_logits, axis=-1)
    g_sel = jnp.argmax(g_logits, axis=-1)
    g_p = jnp.take_along_axis(g_prob, g_sel[:, None], axis=1)
    e_logits = ((xf @ w_re).astype(jnp.float32) + b_re.astype(jnp.float32)).reshape(T, N_GROUPS, EXPERTS_PER_GROUP)
    e_logits = jnp.take_along_axis(e_logits, g_sel[:, None, None], axis=1)[:, 0]
    e_prob = jax.nn.softmax(e_logits, axis=-1)
    top_p, top_e = lax.top_k(e_prob, TOP_K)
    gates = g_p * top_p / jnp.sum(top_p, axis=-1, keepdims=True)
    expert_id = (g_sel[:, None] * EXPERTS_PER_GROUP + top_e).astype(jnp.int32)

    A = T * TOP_K
    flat_e = expert_id.reshape(A)
    flat_w = gates.reshape(A)
    flat_tok = (jnp.arange(A) // TOP_K).astype(jnp.int32)
    order = jnp.argsort(flat_e)
    sorted_e = flat_e[order]
    counts = jnp.bincount(flat_e, length=N_EXPERTS).astype(jnp.int32)
    starts = jnp.cumsum(counts) - counts
    padded = (counts + MOE_BLOCK - 1) // MOE_BLOCK * MOE_BLOCK
    pad_ends = jnp.cumsum(padded)
    pad_starts = pad_ends - padded
    dest = pad_starts[sorted_e] + jnp.arange(A, dtype=jnp.int32) - starts[sorted_e]
    n_blocks = -(-A // MOE_BLOCK) + N_EXPERTS
    P = n_blocks * MOE_BLOCK
    slot_tok = jnp.full((P,), T, jnp.int32).at[dest].set(flat_tok[order])
    slot_w = jnp.zeros((P,), h.dtype).at[dest].set(flat_w[order].astype(h.dtype))
    block_start = jnp.arange(n_blocks, dtype=jnp.int32) * MOE_BLOCK
    block_e = jnp.minimum(jnp.searchsorted(pad_ends, block_start, side='right'), N_EXPERTS - 1)
    xpad = jnp.concatenate([xf, jnp.zeros((1, D), xf.dtype)], axis=0)
    xb = xpad[slot_tok].reshape(n_blocks, MOE_BLOCK, D)

    def expert_block(args):
        xblk, e = args
        hid = jax.nn.silu(xblk @ w_gate[e]) * (xblk @ w_up[e])
        return hid @ w_down[e]

    yb = lax.map(expert_block, (xb, block_e)).reshape(P, D)
    y = jnp.zeros((T + 1, D), h.dtype).at[slot_tok].add(yb * slot_w[:, None])
    return y[:T].reshape(B, S, D)


def encoder_layer(x, norm_mix_g, w_in, norm_v_g, w_spatial, b_spatial, q_norm_g, k_norm_g, sink,
                  w_proj_a, w_proj_b, w_out, norm_ffn_g, w_router_group, b_router_group,
                  w_router_expert, b_router_expert, w_gate_e, w_up_e, w_down_e):
    h = rms_norm(x, norm_mix_g)
    z = h @ w_in
    u, v, q, k, va, gate_a, gate_b = jnp.split(z, SPLITS, axis=-1)
    ya = spatial_gating_mixer(u, v, norm_v_g, w_spatial, b_spatial) @ w_proj_a
    yb = window_attention(q, k, va, q_norm_g, k_norm_g, sink) @ w_proj_b
    merged = jax.nn.sigmoid(gate_a) * ya + jax.nn.sigmoid(gate_b) * yb
    x = x + merged @ w_out
    x = x + hierarchical_moe(rms_norm(x, norm_ffn_g), w_router_group, b_router_group,
                             w_router_expert, b_router_expert, w_gate_e, w_up_e, w_down_e)
    return x


def setup_inputs(seed: int = 0) -> dict:
    key = jax.random.key(seed)
    ks = jax.random.split(key, 24)
    f32 = jnp.float32
    L = DEPTH

    def nrm(k, shape, scale):
        return jax.random.normal(k, shape, f32) * scale

    def gain(k, shape):
        return 1.0 + 0.01 * jax.random.normal(k, shape, f32)

    return {
        "x_prompt": nrm(ks[0], (BATCH, SEQ, D_MODEL), 1.0),
        "x_sample": nrm(ks[1], (DEC_BATCH, DEC_SEQ, D_MODEL), 1.0),
        "norm_mix_g": gain(ks[2], (L, D_MODEL)),
        "w_in": nrm(ks[3], (L, D_MODEL, IN_W), D_MODEL ** -0.5),
        "norm_v_g": gain(ks[4], (L, A_WIDTH)),
        "w_spatial": nrm(ks[5], (L, A_GROUPS, CHUNK, CHUNK), CHUNK ** -0.5),
        "b_spatial": 1.0 + nrm(ks[6], (L, A_GROUPS, CHUNK), 0.02),
        "q_norm_g": gain(ks[7], (L, HEAD_DIM)),
        "k_norm_g": gain(ks[8], (L, HEAD_DIM)),
        "sink": nrm(ks[9], (L, N_Q_HEADS), 0.5),
        "w_proj_a": nrm(ks[10], (L, A_WIDTH, D_MODEL), A_WIDTH ** -0.5),
        "w_proj_b": nrm(ks[11], (L, Q_W, D_MODEL), Q_W ** -0.5),
        "w_out": nrm(ks[12], (L, D_MODEL, D_MODEL), D_MODEL ** -0.5),
        "norm_ffn_g": gain(ks[13], (L, D_MODEL)),
        "w_router_group": nrm(ks[14], (L, D_MODEL, N_GROUPS), D_MODEL ** -0.5),
        "b_router_group": nrm(ks[15], (L, N_GROUPS), 0.01),
        "w_router_expert": nrm(ks[16], (L, D_MODEL, N_EXPERTS), D_MODEL ** -0.5),
        "b_router_expert": nrm(ks[17], (L, N_EXPERTS), 0.01),
        "w_gate_e": nrm(ks[18], (L, N_EXPERTS, D_MODEL, D_EXPERT), D_MODEL ** -0.5),
        "w_up_e": nrm(ks[19], (L, N_EXPERTS, D_MODEL, D_EXPERT), D_MODEL ** -0.5),
        "w_down_e": nrm(ks[20], (L, N_EXPERTS, D_EXPERT, D_MODEL), D_EXPERT ** -0.5),
    }


def reference(x_prompt, x_sample, norm_mix_g, w_in, norm_v_g, w_spatial, b_spatial, q_norm_g,
              k_norm_g, sink, w_proj_a, w_proj_b, w_out, norm_ffn_g, w_router_group,
              b_router_group, w_router_expert, b_router_expert, w_gate_e, w_up_e, w_down_e):
    def trunk(x):
        for l in range(DEPTH):
            x = encoder_layer(x, norm_mix_g[l], w_in[l], norm_v_g[l], w_spatial[l], b_spatial[l],
                              q_norm_g[l], k_norm_g[l], sink[l], w_proj_a[l], w_proj_b[l],
                              w_out[l], norm_ffn_g[l], w_router_group[l], b_router_group[l],
                              w_router_expert[l], b_router_expert[l], w_gate_e[l], w_up_e[l],
                              w_down_e[l])
        return x

    y_prompt = trunk(x_prompt)
    y_sample = trunk(x_sample)
    return (y_prompt, y_sample)
```

```python
import functools

import jax
import jax.numpy as jnp
from jax import lax
from jax.experimental import pallas as pl
from jax.experimental.pallas import tpu as pltpu

F32 = jnp.float32
BF16 = jnp.bfloat16
I32 = jnp.int32

LANES = 128
SUBLANES = 8
VMEM_BYTES_V7X = 64 * 1024 * 1024

D_MODEL = 1024
A_WIDTH = D_MODEL
A_GROUPS = 8
CHUNK = 128
HEAD_DIM = 128
N_Q_HEADS = D_MODEL // HEAD_DIM
N_KV_HEADS = 2
REP = N_Q_HEADS // N_KV_HEADS
WINDOW = 128
ROPE_THETA = 10000.0
Q_W = N_Q_HEADS * HEAD_DIM
KV_W = N_KV_HEADS * HEAD_DIM
IN_W = 2 * A_WIDTH + Q_W + 2 * KV_W + 2 * D_MODEL
COL_U = 0
COL_V = COL_U + A_WIDTH
COL_Q = COL_V + A_WIDTH
COL_K = COL_Q + Q_W
COL_VA = COL_K + KV_W
COL_GA = COL_VA + KV_W
COL_GB = COL_GA + D_MODEL
N_GROUPS = 4
EXPERTS_PER_GROUP = 8
N_EXPERTS = N_GROUPS * EXPERTS_PER_GROUP
TOP_K = 2
D_EXPERT = 512
EPS = 1e-6
NEG = -1e30

TM_IN = 256
TQ = 256
TM_ROW = 256
MOE_ROWS = 256
VMEM_LIMIT = 56 * 1024 * 1024
assert VMEM_LIMIT < VMEM_BYTES_V7X


def _rms(x, g):
    return x * lax.rsqrt(jnp.mean(x * x, axis=-1, keepdims=True) + EPS) * g


def _const_spec(shape):
    nd = len(shape)
    return pl.BlockSpec(shape, lambda *_: (0,) * nd, pipeline_mode=pl.Buffered(1))


def _inproj_kernel(x_ref, gmix_ref, win_ref, gv_ref, ws_ref, bs_ref, gq_ref, gk_ref, cos_ref, sin_ref,
                   wpa_ref, ma_ref, sgb_ref, q_ref, k_ref, v_ref, u_scr, vn_scr, a_scr):
    tm = x_ref.shape[0]
    h = _rms(x_ref[...], gmix_ref[...]).astype(BF16)

    def proj(lo, width):
        return jnp.dot(h, win_ref[:, lo:lo + width], preferred_element_type=F32)

    vn_scr[...] = _rms(jax.nn.gelu(proj(COL_V, A_WIDTH)), gv_ref[...]).astype(BF16)
    u_scr[...] = jax.nn.gelu(proj(COL_U, A_WIDTH))
    for c in range(tm // CHUNK):
        rows = slice(c * CHUNK, (c + 1) * CHUNK)
        for g in range(A_GROUPS):
            cols = slice(g * LANES, (g + 1) * LANES)
            mixed = jnp.dot(ws_ref[g], vn_scr[rows, cols], preferred_element_type=F32) + bs_ref[g]
            a_scr[rows, cols] = (u_scr[rows, cols] * mixed).astype(BF16)
    ya = jnp.dot(a_scr[...], wpa_ref[...], preferred_element_type=F32)
    ma_ref[...] = jax.nn.sigmoid(proj(COL_GA, D_MODEL)) * ya
    sgb_ref[...] = jax.nn.sigmoid(proj(COL_GB, D_MODEL))

    cos = cos_ref[...]
    sin = sin_ref[...]

    def norm_rope(z, g):
        zn = _rms(z, g)
        return zn * cos + pltpu.roll(zn, HEAD_DIM // 2, 1) * sin

    qz = proj(COL_Q, Q_W)
    for hd in range(N_Q_HEADS):
        cols = slice(hd * HEAD_DIM, (hd + 1) * HEAD_DIM)
        q_ref[:, cols] = norm_rope(qz[:, cols], gq_ref[...]).astype(BF16)
    kz = proj(COL_K, KV_W)
    for hd in range(N_KV_HEADS):
        cols = slice(hd * HEAD_DIM, (hd + 1) * HEAD_DIM)
        k_ref[:, cols] = norm_rope(kz[:, cols], gk_ref[...]).astype(BF16)
    v_ref[...] = proj(COL_VA, KV_W).astype(BF16)


def _inproj(x, p, cos, sin, seq):
    t = x.shape[0]
    tm = TM_IN
    n_pos = seq // tm
    row = lambda w: pl.BlockSpec((tm, w), lambda i: (i, 0))
    pos = pl.BlockSpec((tm, HEAD_DIM), lambda i: (i % n_pos, 0))
    return pl.pallas_call(
        _inproj_kernel,
        grid=(t // tm,),
        in_specs=[row(D_MODEL), _const_spec((1, D_MODEL)), _const_spec((D_MODEL, IN_W)),
                  _const_spec((1, A_WIDTH)), _const_spec((A_GROUPS, CHUNK, CHUNK)),
                  _const_spec((A_GROUPS, CHUNK, LANES)), _const_spec((1, HEAD_DIM)),
                  _const_spec((1, HEAD_DIM)), pos, pos, _const_spec((A_WIDTH, D_MODEL))],
        out_specs=[row(D_MODEL), row(D_MODEL), row(Q_W), row(KV_W), row(KV_W)],
        out_shape=[jax.ShapeDtypeStruct((t, D_MODEL), F32), jax.ShapeDtypeStruct((t, D_MODEL), F32),
                   jax.ShapeDtypeStruct((t, Q_W), BF16), jax.ShapeDtypeStruct((t, KV_W), BF16),
                   jax.ShapeDtypeStruct((t, KV_W), BF16)],
        scratch_shapes=[pltpu.VMEM((tm, A_WIDTH), F32), pltpu.VMEM((tm, A_WIDTH), BF16),
                        pltpu.VMEM((tm, A_WIDTH), BF16)],
        compiler_params=pltpu.CompilerParams(dimension_semantics=("parallel",),
                                             vmem_limit_bytes=VMEM_LIMIT),
        name="inproj",
    )(x, p["norm_mix_g"], p["w_in"], p["norm_v_g"], p["w_spatial"], p["b_spatial"], p["q_norm_g"],
      p["k_norm_g"], cos, sin, p["w_proj_a"])


def _attn_kernel(sink_ref, x_ref, ma_ref, sgb_ref, q_ref, kp_ref, kc_ref, kn_ref, vp_ref, vc_ref, vn_ref,
                 wpb_ref, wout_ref, gffn_ref, wr_ref, br_ref,
                 x1_ref, hn_ref, route_ref, gate_ref, counts_ref, kcat, vcat, o_scr, *, tiles_per_seq):
    tq = x_ref.shape[0]
    blk = WINDOW
    i = pl.program_id(0)
    pos_tile = i % tiles_per_seq
    has_prev = pos_tile > 0
    has_next = pos_tile < tiles_per_seq - 1

    kcat[0:blk] = kp_ref[...]
    kcat[blk:blk + tq] = kc_ref[...]
    kcat[blk + tq:] = kn_ref[...]
    vcat[0:blk] = vp_ref[...]
    vcat[blk:blk + tq] = vc_ref[...]
    vcat[blk + tq:] = vn_ref[...]

    qi = lax.broadcasted_iota(I32, (blk, 3 * blk), 0)
    kj = lax.broadcasted_iota(I32, (blk, 3 * blk), 1)
    band = jnp.abs(kj - blk - qi) <= WINDOW
    scale = HEAD_DIM ** -0.5
    n_sub = tq // blk
    for j in range(n_sub):
        valid = band
        if j == 0:
            valid = jnp.logical_and(valid, kj >= jnp.where(has_prev, 0, blk))
        if j == n_sub - 1:
            valid = jnp.logical_and(valid, kj < jnp.where(has_next, 3 * blk, 2 * blk))
        rows = slice(j * blk, (j + 1) * blk)
        for g in range(N_KV_HEADS):
            kcols = slice(g * HEAD_DIM, (g + 1) * HEAD_DIM)
            ks = kcat[j * blk:(j + 3) * blk, kcols]
            vs = vcat[j * blk:(j + 3) * blk, kcols]
            for r in range(REP):
                hd = g * REP + r
                qcols = slice(hd * HEAD_DIM, (hd + 1) * HEAD_DIM)
                s = lax.dot_general(q_ref[rows, qcols], ks, (((1,), (1,)), ((), ())),
                                    preferred_element_type=F32) * scale
                s = jnp.where(valid, s, NEG)
                sink = sink_ref[hd]
                m = jnp.maximum(jnp.max(s, axis=-1, keepdims=True), sink)
                e = jnp.exp(s - m)
                denom = jnp.sum(e, axis=-1, keepdims=True) + jnp.exp(sink - m)
                prob = (e / denom).astype(BF16)
                o_scr[rows, qcols] = jnp.dot(prob, vs, preferred_element_type=F32).astype(BF16)

    yb = jnp.dot(o_scr[...], wpb_ref[...], preferred_element_type=F32)
    merged = ma_ref[...] + sgb_ref[...] * yb
    x1 = x_ref[...] + jnp.dot(merged.astype(BF16), wout_ref[...], preferred_element_type=F32)
    x1_ref[...] = x1
    hn = _rms(x1, gffn_ref[...])
    hn_ref[...] = hn

    logits = jnp.dot(hn.astype(BF16), wr_ref[...], preferred_element_type=F32) + br_ref[...]
    lane = lax.broadcasted_iota(I32, (tq, LANES), 1).astype(F32)
    ninf = -jnp.inf
    first = float(LANES)

    def rmax(a):
        return jnp.max(a, axis=-1, keepdims=True)

    def rsum(a):
        return jnp.sum(a, axis=-1, keepdims=True)

    def first_lane(mask):
        return jnp.min(jnp.where(mask, lane, first), axis=-1, keepdims=True)

    gl = jnp.where(lane < N_GROUPS, logits, ninf)
    gmax = rmax(gl)
    g_sel = first_lane(gl == gmax)
    g_p = 1.0 / rsum(jnp.exp(gl - gmax))
    lo = N_GROUPS + EXPERTS_PER_GROUP * g_sel
    emask = jnp.logical_and(lane >= lo, lane < lo + EXPERTS_PER_GROUP)
    el = jnp.where(emask, logits, ninf)
    ee = jnp.exp(el - rmax(el))
    eprob = jnp.where(emask, ee / rsum(ee), -1.0)
    p1 = rmax(eprob)
    i1 = first_lane(eprob == p1)
    eprob2 = jnp.where(lane == i1, -1.0, eprob)
    p2 = rmax(eprob2)
    i2 = first_lane(eprob2 == p2)
    psum = p1 + p2
    w1 = g_p * p1 / psum
    w2 = g_p * p2 / psum
    e1 = i1 - N_GROUPS
    e2 = i2 - N_GROUPS

    @pl.when(i == 0)
    def _():
        counts_ref[...] = jnp.zeros_like(counts_ref)

    oh1 = lane == e1
    oh2 = lane == e2
    cnt = jnp.where(oh1, 1.0, 0.0) + jnp.where(oh2, 1.0, 0.0)
    ri = lax.broadcasted_iota(I32, (tq, tq), 0)
    ci = lax.broadcasted_iota(I32, (tq, tq), 1)
    tri = jnp.where(ri > ci, 1.0, 0.0).astype(BF16)
    base = counts_ref[0:1, :] + jnp.dot(tri, cnt.astype(BF16), preferred_element_type=F32)
    r1 = rsum(jnp.where(oh1, base, 0.0))
    r2 = rsum(jnp.where(oh2, base, 0.0))
    counts_ref[...] = counts_ref[...] + jnp.sum(cnt, axis=0, keepdims=True)

    route = jnp.where(lane == 0.0, e1, jnp.where(lane == 1.0, e2,
                      jnp.where(lane == 2.0, r1, jnp.where(lane == 3.0, r2, 0.0))))
    route_ref[...] = route.astype(I32)
    gate_ref[...] = jnp.where(lane == 0.0, w1, jnp.where(lane == 1.0, w2, 0.0))


def _attn(x, ma, sgb, q, k, v, p, seq):
    t = x.shape[0]
    tq = TQ
    sub = tq // WINDOW
    last_blk = t // WINDOW - 1
    row = lambda w: pl.BlockSpec((tq, w), lambda i: (i, 0))
    prev = pl.BlockSpec((WINDOW, KV_W), lambda i: (jnp.maximum(i * sub - 1, 0), 0))
    nxt = pl.BlockSpec((WINDOW, KV_W), lambda i: (jnp.minimum((i + 1) * sub, last_blk), 0))
    return pl.pallas_call(
        functools.partial(_attn_kernel, tiles_per_seq=seq // tq),
        grid=(t // tq,),
        in_specs=[pl.BlockSpec(memory_space=pltpu.SMEM),
                  row(D_MODEL), row(D_MODEL), row(D_MODEL), row(Q_W),
                  prev, row(KV_W), nxt, prev, row(KV_W), nxt,
                  _const_spec((Q_W, D_MODEL)), _const_spec((D_MODEL, D_MODEL)), _const_spec((1, D_MODEL)),
                  _const_spec((D_MODEL, LANES)), _const_spec((1, LANES))],
        out_specs=[row(D_MODEL), row(D_MODEL), row(LANES), row(LANES),
                   pl.BlockSpec((8, LANES), lambda i: (0, 0))],
        out_shape=[jax.ShapeDtypeStruct((t, D_MODEL), F32), jax.ShapeDtypeStruct((t, D_MODEL), F32),
                   jax.ShapeDtypeStruct((t, LANES), I32), jax.ShapeDtypeStruct((t, LANES), F32),
                   jax.ShapeDtypeStruct((8, LANES), F32)],
        scratch_shapes=[pltpu.VMEM((tq + 2 * WINDOW, KV_W), BF16), pltpu.VMEM((tq + 2 * WINDOW, KV_W), BF16),
                        pltpu.VMEM((tq, Q_W), BF16)],
        compiler_params=pltpu.CompilerParams(dimension_semantics=("arbitrary",),
                                             vmem_limit_bytes=VMEM_LIMIT),
        name="attn",
    )(p["sink"], x, ma, sgb, q, k, k, k, v, v, v, p["w_proj_b"], p["w_out"], p["norm_ffn_g"],
      p["w_router"], p["b_router"])


def _dest_kernel(route_ref, starts_ref, dest_ref):
    route = route_ref[...].astype(F32)
    lane = lax.broadcasted_iota(I32, route.shape, 1).astype(F32)
    starts = starts_ref[...]

    def slot(k):
        e = route[:, k:k + 1]
        start = jnp.sum(jnp.where(lane == e, starts, 0.0), axis=-1, keepdims=True)
        return start + route[:, TOP_K + k:TOP_K + k + 1]

    dest_ref[...] = jnp.where(lane == 0.0, slot(0), jnp.where(lane == 1.0, slot(1), 0.0)).astype(I32)


def _dest(route, pad_starts):
    t = route.shape[0]
    tm = TM_ROW
    return pl.pallas_call(
        _dest_kernel,
        grid=(t // tm,),
        in_specs=[pl.BlockSpec((tm, LANES), lambda i: (i, 0)), _const_spec((1, LANES))],
        out_specs=pl.BlockSpec((tm, LANES), lambda i: (i, 0)),
        out_shape=jax.ShapeDtypeStruct((t, LANES), I32),
        compiler_params=pltpu.CompilerParams(dimension_semantics=("parallel",)),
        name="dest",
    )(route, pad_starts)


def _row_copy_wait(src_rows, dst_rows, sem):
    pltpu.make_async_copy(src_rows, dst_rows, sem).wait()


def _dispatch_kernel(counts_ref, starts_ref, nused_ref, dest_ref, hn_ref, xs_ref, zero_scr, sem, zsem):
    tm = hn_ref.shape[0]
    i = pl.program_id(0)

    def issue(r, carry):
        for k in range(TOP_K):
            d = dest_ref[0, 0, TOP_K * r + k]
            pltpu.make_async_copy(hn_ref.at[pl.ds(r, 1)], xs_ref.at[pl.ds(d, 1)], sem).start()
        return carry

    lax.fori_loop(0, tm, issue, 0, unroll=8)

    @pl.when(i == pl.num_programs(0) - 1)
    def _():
        zero_scr[...] = jnp.zeros_like(zero_scr)

        def fill(e, carry):
            cnt = counts_ref[e]
            npad = (MOE_ROWS - cnt % MOE_ROWS) % MOE_ROWS
            row0 = starts_ref[e] + cnt
            head = npad % SUBLANES

            def one_row(r, c):
                cp = pltpu.make_async_copy(zero_scr.at[pl.ds(0, 1)], xs_ref.at[pl.ds(row0 + r, 1)], zsem)
                cp.start()
                cp.wait()
                return c

            lax.fori_loop(0, head, one_row, 0)
            cur = row0 + head
            rest = npad - head
            size = MOE_ROWS // 2
            while size >= SUBLANES:
                take = (rest & size) != 0

                @pl.when(take)
                def _(cur=cur, size=size):
                    dst = xs_ref.at[pl.ds(pl.multiple_of(cur, SUBLANES), size)]
                    cp = pltpu.make_async_copy(zero_scr.at[pl.ds(0, size)], dst, zsem)
                    cp.start()
                    cp.wait()

                cur = cur + jnp.where(take, size, 0)
                size //= 2
            return carry

        lax.fori_loop(0, N_EXPERTS, fill, 0)

        def fill_block(b, carry):
            dst = xs_ref.at[pl.ds(pl.multiple_of(b * MOE_ROWS, MOE_ROWS), MOE_ROWS)]
            cp = pltpu.make_async_copy(zero_scr, dst, zsem)
            cp.start()
            cp.wait()
            return carry

        lax.fori_loop(nused_ref[0], xs_ref.shape[0] // MOE_ROWS, fill_block, 0)

    for _ in range(TOP_K):
        _row_copy_wait(hn_ref, xs_ref.at[pl.ds(0, tm)], sem)


def _dispatch(counts, pad_starts, n_used, dest_blocks, hn, n_rows):
    t = hn.shape[0]
    tm = TM_ROW
    return pl.pallas_call(
        _dispatch_kernel,
        grid_spec=pltpu.PrefetchScalarGridSpec(
            num_scalar_prefetch=3,
            grid=(t // tm,),
            in_specs=[pl.BlockSpec((1, 1, TOP_K * tm), lambda i, *_: (i, 0, 0), memory_space=pltpu.SMEM),
                      pl.BlockSpec((tm, D_MODEL), lambda i, *_: (i, 0))],
            out_specs=pl.BlockSpec(memory_space=pl.ANY),
            scratch_shapes=[pltpu.VMEM((MOE_ROWS, D_MODEL), F32), pltpu.SemaphoreType.DMA,
                            pltpu.SemaphoreType.DMA],
        ),
        out_shape=jax.ShapeDtypeStruct((n_rows, D_MODEL), F32),
        compiler_params=pltpu.CompilerParams(dimension_semantics=("arbitrary",)),
        name="dispatch",
    )(counts, pad_starts, n_used, dest_blocks, hn)


def _moe_kernel(be_ref, nused_ref, xs_ref, wg_ref, wu_ref, wd_ref, yb_ref):
    used = pl.program_id(0) < nused_ref[0]

    @pl.when(jnp.logical_not(used))
    def _():
        yb_ref[...] = jnp.zeros_like(yb_ref)

    @pl.when(used)
    def _():
        x = xs_ref[...].astype(BF16)
        gate = jnp.dot(x, wg_ref[0], preferred_element_type=F32)
        up = jnp.dot(x, wu_ref[0], preferred_element_type=F32)
        hid = (jax.nn.silu(gate) * up).astype(BF16)
        yb_ref[...] = jnp.dot(hid, wd_ref[0], preferred_element_type=F32)


def _moe(block_e, n_used, xs, p):
    n_blocks = xs.shape[0] // MOE_ROWS

    def rows(i, be, nu):
        return (jnp.minimum(i, nu[0] - 1), 0)

    def expert(i, be, nu):
        return (be[jnp.minimum(i, nu[0] - 1)], 0, 0)

    return pl.pallas_call(
        _moe_kernel,
        grid_spec=pltpu.PrefetchScalarGridSpec(
            num_scalar_prefetch=2,
            grid=(n_blocks,),
            in_specs=[pl.BlockSpec((MOE_ROWS, D_MODEL), rows),
                      pl.BlockSpec((1, D_MODEL, D_EXPERT), expert),
                      pl.BlockSpec((1, D_MODEL, D_EXPERT), expert),
                      pl.BlockSpec((1, D_EXPERT, D_MODEL), expert)],
            out_specs=pl.BlockSpec((MOE_ROWS, D_MODEL), lambda i, be, nu: (i, 0)),
        ),
        out_shape=jax.ShapeDtypeStruct(xs.shape, F32),
        compiler_params=pltpu.CompilerParams(dimension_semantics=("arbitrary",),
                                             vmem_limit_bytes=VMEM_LIMIT),
        name="moe",
    )(block_e, n_used, xs, p["w_gate_e"], p["w_up_e"], p["w_down_e"])


def _combine_kernel(dest_ref, x1_ref, gate_ref, yb_ref, out_ref, y0_scr, y1_scr, sem):
    tm = x1_ref.shape[0]
    bufs = (y0_scr, y1_scr)

    def issue(r, carry):
        for k in range(TOP_K):
            d = dest_ref[0, 0, TOP_K * r + k]
            pltpu.make_async_copy(yb_ref.at[pl.ds(d, 1)], bufs[k].at[pl.ds(r, 1)], sem).start()
        return carry

    lax.fori_loop(0, tm, issue, 0, unroll=8)
    for k in range(TOP_K):
        _row_copy_wait(yb_ref.at[pl.ds(0, tm)], bufs[k], sem)
    gate = gate_ref[...]
    out_ref[...] = x1_ref[...] + (y0_scr[...] * gate[:, 0:1] + y1_scr[...] * gate[:, 1:2])


def _combine(dest_blocks, x1, gate, yb):
    t = x1.shape[0]
    tm = TM_ROW
    return pl.pallas_call(
        _combine_kernel,
        grid=(t // tm,),
        in_specs=[pl.BlockSpec((1, 1, TOP_K * tm), lambda i: (i, 0, 0), memory_space=pltpu.SMEM),
                  pl.BlockSpec((tm, D_MODEL), lambda i: (i, 0)),
                  pl.BlockSpec((tm, LANES), lambda i: (i, 0)),
                  pl.BlockSpec(memory_space=pl.ANY)],
        out_specs=pl.BlockSpec((tm, D_MODEL), lambda i: (i, 0)),
        out_shape=jax.ShapeDtypeStruct((t, D_MODEL), F32),
        scratch_shapes=[pltpu.VMEM((tm, D_MODEL), F32), pltpu.VMEM((tm, D_MODEL), F32),
                        pltpu.SemaphoreType.DMA],
        compiler_params=pltpu.CompilerParams(dimension_semantics=("arbitrary",)),
        name="combine",
    )(dest_blocks, x1, gate, yb)


def _rope_tables(seq):
    half = HEAD_DIM // 2
    inv_freq = ROPE_THETA ** (-jnp.arange(half, dtype=F32) / half)
    ang = jnp.arange(seq).astype(F32)[:, None] * inv_freq[None, :]
    cos = jnp.cos(ang)
    sin = jnp.sin(ang)
    return jnp.concatenate([cos, cos], axis=-1), jnp.concatenate([-sin, sin], axis=-1)


def _layer(x, p, seq):
    t = x.shape[0]
    cos, sin = _rope_tables(seq)
    ma, sgb, q, k, v = _inproj(x, p, cos, sin, seq)
    x1, hn, route, gate, counts_f = _attn(x, ma, sgb, q, k, v, p, seq)

    counts = counts_f[0, :N_EXPERTS].astype(I32)
    padded = (counts + MOE_ROWS - 1) // MOE_ROWS * MOE_ROWS
    pad_ends = jnp.cumsum(padded)
    pad_starts = pad_ends - padded
    n_blocks = (t * TOP_K) // MOE_ROWS + N_EXPERTS
    block_start = jnp.arange(n_blocks, dtype=I32) * MOE_ROWS
    block_e = jnp.minimum(jnp.searchsorted(pad_ends, block_start, side="right"), N_EXPERTS - 1).astype(I32)
    n_used = (pad_ends[-1:] // MOE_ROWS).astype(I32)
    starts_row = jnp.zeros((1, LANES), F32).at[0, :N_EXPERTS].set(pad_starts.astype(F32))

    dest = _dest(route, starts_row)
    dest_blocks = dest[:, :TOP_K].reshape(t // TM_ROW, 1, TOP_K * TM_ROW)
    xs = _dispatch(counts, pad_starts.astype(I32), n_used, dest_blocks, hn, n_blocks * MOE_ROWS)
    yb = _moe(block_e, n_used, xs, p)
    return _combine(dest_blocks, x1, gate, yb)


def kernel(x_prompt, x_sample, norm_mix_g, w_in, norm_v_g, w_spatial, b_spatial, q_norm_g, k_norm_g, sink,
           w_proj_a, w_proj_b, w_out, norm_ffn_g, w_router_group, b_router_group, w_router_expert,
           b_router_expert, w_gate_e, w_up_e, w_down_e):
    depth = w_in.shape[0]
    layers = []
    for l in range(depth):
        w_router = jnp.zeros((D_MODEL, LANES), F32)
        w_router = w_router.at[:, :N_GROUPS].set(w_router_group[l])
        w_router = w_router.at[:, N_GROUPS:N_GROUPS + N_EXPERTS].set(w_router_expert[l])
        b_router = jnp.zeros((1, LANES), F32)
        b_router = b_router.at[0, :N_GROUPS].set(b_router_group[l])
        b_router = b_router.at[0, N_GROUPS:N_GROUPS + N_EXPERTS].set(b_router_expert[l])
        layers.append(dict(
            norm_mix_g=norm_mix_g[l][None], w_in=w_in[l].astype(BF16), norm_v_g=norm_v_g[l][None],
            w_spatial=w_spatial[l].astype(BF16),
            b_spatial=jnp.broadcast_to(b_spatial[l][:, :, None], (A_GROUPS, CHUNK, LANES)),
            q_norm_g=q_norm_g[l][None], k_norm_g=k_norm_g[l][None], sink=sink[l],
            w_proj_a=w_proj_a[l].astype(BF16), w_proj_b=w_proj_b[l].astype(BF16), w_out=w_out[l].astype(BF16),
            norm_ffn_g=norm_ffn_g[l][None], w_router=w_router.astype(BF16), b_router=b_router,
            w_gate_e=w_gate_e[l].astype(BF16), w_up_e=w_up_e[l].astype(BF16), w_down_e=w_down_e[l].astype(BF16)))

    def trunk(x):
        b, s, d = x.shape
        y = x.reshape(b * s, d)
        for p in layers:
            y = _layer(y, p, s)
        return y.reshape(b, s, d)

    return trunk(x_prompt), trunk(x_sample)
```

```python
import functools

import jax
import jax.numpy as jnp
from jax import lax
from jax.experimental import pallas as pl
from jax.experimental.pallas import tpu as pltpu

F32 = jnp.float32
BF16 = jnp.bfloat16
I32 = jnp.int32

LANES = 128
SUBLANES = 8
VMEM_BYTES_V7X = 64 * 1024 * 1024

D_MODEL = 1024
A_WIDTH = D_MODEL
A_GROUPS = 8
CHUNK = 128
HEAD_DIM = 128
N_Q_HEADS = D_MODEL // HEAD_DIM
N_KV_HEADS = 2
REP = N_Q_HEADS // N_KV_HEADS
WINDOW = 128
ROPE_THETA = 10000.0
Q_W = N_Q_HEADS * HEAD_DIM
KV_W = N_KV_HEADS * HEAD_DIM
IN_W = 2 * A_WIDTH + Q_W + 2 * KV_W + 2 * D_MODEL
COL_U = 0
COL_V = COL_U + A_WIDTH
COL_Q = COL_V + A_WIDTH
COL_K = COL_Q + Q_W
COL_VA = COL_K + KV_W
COL_GA = COL_VA + KV_W
COL_GB = COL_GA + D_MODEL
N_GROUPS = 4
EXPERTS_PER_GROUP = 8
N_EXPERTS = N_GROUPS * EXPERTS_PER_GROUP
TOP_K = 2
D_EXPERT = 512
EPS = 1e-6
NEG = -1e30

TM_IN = 256
TQ = 256
TM_ROW = 256
TM_DEST = 2048
MOE_ROWS = 256
VMEM_LIMIT = 56 * 1024 * 1024
assert VMEM_LIMIT < VMEM_BYTES_V7X


def _rms(x, g):
    return x * lax.rsqrt(jnp.mean(x * x, axis=-1, keepdims=True) + EPS) * g


def _const_spec(shape):
    nd = len(shape)
    return pl.BlockSpec(shape, lambda *_: (0,) * nd, pipeline_mode=pl.Buffered(1))


def _inproj_kernel(x_ref, gmix_ref, win_ref, gv_ref, ws_ref, bs_ref, gq_ref, gk_ref, cos_ref, sin_ref,
                   wpa_ref, ma_ref, sgb_ref, q_ref, k_ref, v_ref, u_scr, vn_scr, a_scr):
    tm = x_ref.shape[0]
    h = _rms(x_ref[...], gmix_ref[...]).astype(BF16)

    def proj(lo, width):
        return jnp.dot(h, win_ref[:, lo:lo + width], preferred_element_type=F32)

    vn_scr[...] = _rms(jax.nn.gelu(proj(COL_V, A_WIDTH)), gv_ref[...]).astype(BF16)
    u_scr[...] = jax.nn.gelu(proj(COL_U, A_WIDTH))
    for c in range(tm // CHUNK):
        rows = slice(c * CHUNK, (c + 1) * CHUNK)
        for g in range(A_GROUPS):
            cols = slice(g * LANES, (g + 1) * LANES)
            mixed = jnp.dot(ws_ref[g], vn_scr[rows, cols], preferred_element_type=F32) + bs_ref[g]
            a_scr[rows, cols] = (u_scr[rows, cols] * mixed).astype(BF16)
    ya = jnp.dot(a_scr[...], wpa_ref[...], preferred_element_type=F32)
    ma_ref[...] = jax.nn.sigmoid(proj(COL_GA, D_MODEL)) * ya
    sgb_ref[...] = jax.nn.sigmoid(proj(COL_GB, D_MODEL))

    cos = cos_ref[...]
    sin = sin_ref[...]

    def norm_rope(z, g):
        zn = _rms(z, g)
        return zn * cos + pltpu.roll(zn, HEAD_DIM // 2, 1) * sin

    qz = proj(COL_Q, Q_W)
    for hd in range(N_Q_HEADS):
        cols = slice(hd * HEAD_DIM, (hd + 1) * HEAD_DIM)
        q_ref[:, cols] = norm_rope(qz[:, cols], gq_ref[...]).astype(BF16)
    kz = proj(COL_K, KV_W)
    for hd in range(N_KV_HEADS):
        cols = slice(hd * HEAD_DIM, (hd + 1) * HEAD_DIM)
        k_ref[:, cols] = norm_rope(kz[:, cols], gk_ref[...]).astype(BF16)
    v_ref[...] = proj(COL_VA, KV_W).astype(BF16)


def _inproj(x, p, cos, sin, seq):
    t = x.shape[0]
    tm = TM_IN
    n_pos = seq // tm
    row = lambda w: pl.BlockSpec((tm, w), lambda i: (i, 0))
    pos = pl.BlockSpec((tm, HEAD_DIM), lambda i: (i % n_pos, 0))
    return pl.pallas_call(
        _inproj_kernel,
        grid=(t // tm,),
        in_specs=[row(D_MODEL), _const_spec((1, D_MODEL)), _const_spec((D_MODEL, IN_W)),
                  _const_spec((1, A_WIDTH)), _const_spec((A_GROUPS, CHUNK, CHUNK)),
                  _const_spec((A_GROUPS, CHUNK, LANES)), _const_spec((1, HEAD_DIM)),
                  _const_spec((1, HEAD_DIM)), pos, pos, _const_spec((A_WIDTH, D_MODEL))],
        out_specs=[row(D_MODEL), row(D_MODEL), row(Q_W), row(KV_W), row(KV_W)],
        out_shape=[jax.ShapeDtypeStruct((t, D_MODEL), F32), jax.ShapeDtypeStruct((t, D_MODEL), F32),
                   jax.ShapeDtypeStruct((t, Q_W), BF16), jax.ShapeDtypeStruct((t, KV_W), BF16),
                   jax.ShapeDtypeStruct((t, KV_W), BF16)],
        scratch_shapes=[pltpu.VMEM((tm, A_WIDTH), F32), pltpu.VMEM((tm, A_WIDTH), BF16),
                        pltpu.VMEM((tm, A_WIDTH), BF16)],
        compiler_params=pltpu.CompilerParams(dimension_semantics=("parallel",),
                                             vmem_limit_bytes=VMEM_LIMIT),
        name="inproj",
    )(x, p["norm_mix_g"], p["w_in"], p["norm_v_g"], p["w_spatial"], p["b_spatial"], p["q_norm_g"],
      p["k_norm_g"], cos, sin, p["w_proj_a"])


def _attn_kernel(sink_ref, x_ref, ma_ref, sgb_ref, q_ref, kp_ref, kc_ref, kn_ref, vp_ref, vc_ref, vn_ref,
                 wpb_ref, wout_ref, gffn_ref, wr_ref, br_ref,
                 x1_ref, hn_ref, route_ref, gate_ref, counts_ref, kcat, vcat, o_scr, s_scr, p_scr,
                 *, tiles_per_seq):
    tq = x_ref.shape[0]
    blk = WINDOW
    i = pl.program_id(0)
    pos_tile = i % tiles_per_seq
    has_prev = pos_tile > 0
    has_next = pos_tile < tiles_per_seq - 1

    kcat[0:blk] = kp_ref[...]
    kcat[blk:blk + tq] = kc_ref[...]
    kcat[blk + tq:] = kn_ref[...]
    vcat[0:blk] = vp_ref[...]
    vcat[blk:blk + tq] = vc_ref[...]
    vcat[blk + tq:] = vn_ref[...]

    qi = lax.broadcasted_iota(I32, (blk, 3 * blk), 0)
    kj = lax.broadcasted_iota(I32, (blk, 3 * blk), 1)
    band = jnp.abs(kj - blk - qi) <= WINDOW
    scale = HEAD_DIM ** -0.5
    n_sub = tq // blk
    pairs = [(j, g) for j in range(n_sub) for g in range(N_KV_HEADS)]

    def keys(ref, j, g):
        return ref[j * blk:(j + 3) * blk, g * HEAD_DIM:(g + 1) * HEAD_DIM]

    def head_cols(g, r):
        hd = g * REP + r
        return slice(hd * HEAD_DIM, (hd + 1) * HEAD_DIM)

    for b, (j, g) in enumerate(pairs):
        rows = slice(j * blk, (j + 1) * blk)
        qs = jnp.concatenate([q_ref[rows, head_cols(g, r)] for r in range(REP)], axis=0)
        s_scr[b] = lax.dot_general(qs, keys(kcat, j, g), (((1,), (1,)), ((), ())),
                                   preferred_element_type=F32)
    for b, (j, g) in enumerate(pairs):
        valid = band
        if j == 0:
            valid = jnp.logical_and(valid, kj >= jnp.where(has_prev, 0, blk))
        if j == n_sub - 1:
            valid = jnp.logical_and(valid, kj < jnp.where(has_next, 3 * blk, 2 * blk))
        for r in range(REP):
            hrows = slice(r * blk, (r + 1) * blk)
            s = jnp.where(valid, s_scr[b, hrows, :] * scale, NEG)
            sink = sink_ref[g * REP + r]
            m = jnp.maximum(jnp.max(s, axis=-1, keepdims=True), sink)
            e = jnp.exp(s - m)
            denom = jnp.sum(e, axis=-1, keepdims=True) + jnp.exp(sink - m)
            p_scr[b, hrows, :] = (e / denom).astype(BF16)
    for b, (j, g) in enumerate(pairs):
        rows = slice(j * blk, (j + 1) * blk)
        o = jnp.dot(p_scr[b], keys(vcat, j, g), preferred_element_type=F32).astype(BF16)
        for r in range(REP):
            o_scr[rows, head_cols(g, r)] = o[r * blk:(r + 1) * blk, :]

    yb = jnp.dot(o_scr[...], wpb_ref[...], preferred_element_type=F32)
    merged = ma_ref[...] + sgb_ref[...] * yb
    x1 = x_ref[...] + jnp.dot(merged.astype(BF16), wout_ref[...], preferred_element_type=F32)
    x1_ref[...] = x1
    hn = _rms(x1, gffn_ref[...])
    hn_ref[...] = hn

    logits = jnp.dot(hn.astype(BF16), wr_ref[...], preferred_element_type=F32) + br_ref[...]
    lane = lax.broadcasted_iota(I32, (tq, LANES), 1).astype(F32)
    ninf = -jnp.inf
    first = float(LANES)

    def rmax(a):
        return jnp.max(a, axis=-1, keepdims=True)

    def rsum(a):
        return jnp.sum(a, axis=-1, keepdims=True)

    def first_lane(mask):
        return jnp.min(jnp.where(mask, lane, first), axis=-1, keepdims=True)

    gl = jnp.where(lane < N_GROUPS, logits, ninf)
    gmax = rmax(gl)
    g_sel = first_lane(gl == gmax)
    g_p = 1.0 / rsum(jnp.exp(gl - gmax))
    lo = N_GROUPS + EXPERTS_PER_GROUP * g_sel
    emask = jnp.logical_and(lane >= lo, lane < lo + EXPERTS_PER_GROUP)
    el = jnp.where(emask, logits, ninf)
    ee = jnp.exp(el - rmax(el))
    eprob = jnp.where(emask, ee / rsum(ee), -1.0)
    p1 = rmax(eprob)
    i1 = first_lane(eprob == p1)
    eprob2 = jnp.where(lane == i1, -1.0, eprob)
    p2 = rmax(eprob2)
    i2 = first_lane(eprob2 == p2)
    psum = p1 + p2
    w1 = g_p * p1 / psum
    w2 = g_p * p2 / psum
    e1 = i1 - N_GROUPS
    e2 = i2 - N_GROUPS

    @pl.when(i == 0)
    def _():
        counts_ref[...] = jnp.zeros_like(counts_ref)

    oh1 = lane == e1
    oh2 = lane == e2
    cnt = jnp.where(oh1, 1.0, 0.0) + jnp.where(oh2, 1.0, 0.0)
    ri = lax.broadcasted_iota(I32, (tq, tq), 0)
    ci = lax.broadcasted_iota(I32, (tq, tq), 1)
    tri = jnp.where(ri > ci, 1.0, 0.0).astype(BF16)
    base = counts_ref[0:1, :] + jnp.dot(tri, cnt.astype(BF16), preferred_element_type=F32)
    r1 = rsum(jnp.where(oh1, base, 0.0))
    r2 = rsum(jnp.where(oh2, base, 0.0))
    counts_ref[...] = counts_ref[...] + jnp.sum(cnt, axis=0, keepdims=True)

    route = jnp.where(lane == 0.0, e1, jnp.where(lane == 1.0, e2,
                      jnp.where(lane == 2.0, r1, jnp.where(lane == 3.0, r2, 0.0))))
    route_ref[...] = route.astype(I32)
    gate_ref[...] = jnp.where(lane == 0.0, w1, jnp.where(lane == 1.0, w2, 0.0))


def _attn(x, ma, sgb, q, k, v, p, seq):
    t = x.shape[0]
    tq = TQ
    sub = tq // WINDOW
    last_blk = t // WINDOW - 1
    row = lambda w: pl.BlockSpec((tq, w), lambda i: (i, 0))
    prev = pl.BlockSpec((WINDOW, KV_W), lambda i: (jnp.maximum(i * sub - 1, 0), 0))
    nxt = pl.BlockSpec((WINDOW, KV_W), lambda i: (jnp.minimum((i + 1) * sub, last_blk), 0))
    return pl.pallas_call(
        functools.partial(_attn_kernel, tiles_per_seq=seq // tq),
        grid=(t // tq,),
        in_specs=[pl.BlockSpec(memory_space=pltpu.SMEM),
                  row(D_MODEL), row(D_MODEL), row(D_MODEL), row(Q_W),
                  prev, row(KV_W), nxt, prev, row(KV_W), nxt,
                  _const_spec((Q_W, D_MODEL)), _const_spec((D_MODEL, D_MODEL)), _const_spec((1, D_MODEL)),
                  _const_spec((D_MODEL, LANES)), _const_spec((1, LANES))],
        out_specs=[row(D_MODEL), row(D_MODEL), row(LANES), row(LANES),
                   pl.BlockSpec((8, LANES), lambda i: (0, 0))],
        out_shape=[jax.ShapeDtypeStruct((t, D_MODEL), F32), jax.ShapeDtypeStruct((t, D_MODEL), F32),
                   jax.ShapeDtypeStruct((t, LANES), I32), jax.ShapeDtypeStruct((t, LANES), F32),
                   jax.ShapeDtypeStruct((8, LANES), F32)],
        scratch_shapes=[pltpu.VMEM((tq + 2 * WINDOW, KV_W), BF16), pltpu.VMEM((tq + 2 * WINDOW, KV_W), BF16),
                        pltpu.VMEM((tq, Q_W), BF16),
                        pltpu.VMEM((sub * N_KV_HEADS, REP * WINDOW, 3 * WINDOW), F32),
                        pltpu.VMEM((sub * N_KV_HEADS, REP * WINDOW, 3 * WINDOW), BF16)],
        compiler_params=pltpu.CompilerParams(dimension_semantics=("arbitrary",),
                                             vmem_limit_bytes=VMEM_LIMIT),
        name="attn",
    )(p["sink"], x, ma, sgb, q, k, k, k, v, v, v, p["w_proj_b"], p["w_out"], p["norm_ffn_g"],
      p["w_router"], p["b_router"])


def _dest_kernel(route_ref, starts_ref, dest_ref):
    route = route_ref[...].astype(F32)
    lane = lax.broadcasted_iota(I32, route.shape, 1).astype(F32)
    starts = starts_ref[...]

    def slot(k):
        e = route[:, k:k + 1]
        start = jnp.sum(jnp.where(lane == e, starts, 0.0), axis=-1, keepdims=True)
        return start + route[:, TOP_K + k:TOP_K + k + 1]

    dest_ref[...] = jnp.where(lane == 0.0, slot(0), jnp.where(lane == 1.0, slot(1), 0.0)).astype(I32)


def _dest(route, pad_starts):
    t = route.shape[0]
    tm = TM_DEST
    return pl.pallas_call(
        _dest_kernel,
        grid=(t // tm,),
        in_specs=[pl.BlockSpec((tm, LANES), lambda i: (i, 0)), _const_spec((1, LANES))],
        out_specs=pl.BlockSpec((tm, LANES), lambda i: (i, 0)),
        out_shape=jax.ShapeDtypeStruct((t, LANES), I32),
        compiler_params=pltpu.CompilerParams(dimension_semantics=("parallel",)),
        name="dest",
    )(route, pad_starts)


def _row_copy_wait(src_rows, dst_rows, sem):
    pltpu.make_async_copy(src_rows, dst_rows, sem).wait()


def _dispatch_kernel(counts_ref, starts_ref, nused_ref, dest_ref, hn_ref, xs_ref, zero_scr, sem, zsem):
    tm = hn_ref.shape[0]
    i = pl.program_id(0)

    def issue(r, carry):
        for k in range(TOP_K):
            d = dest_ref[0, 0, TOP_K * r + k]
            pltpu.make_async_copy(hn_ref.at[pl.ds(r, 1)], xs_ref.at[pl.ds(d, 1)], sem).start(priority=k)
        return carry

    lax.fori_loop(0, tm, issue, 0, unroll=8)

    @pl.when(i == pl.num_programs(0) - 1)
    def _():
        zero_scr[...] = jnp.zeros_like(zero_scr)

        def fill(e, carry):
            cnt = counts_ref[e]
            npad = (MOE_ROWS - cnt % MOE_ROWS) % MOE_ROWS
            row0 = starts_ref[e] + cnt
            head = npad % SUBLANES

            def one_row(r, c):
                cp = pltpu.make_async_copy(zero_scr.at[pl.ds(0, 1)], xs_ref.at[pl.ds(row0 + r, 1)], zsem)
                cp.start()
                cp.wait()
                return c

            lax.fori_loop(0, head, one_row, 0)
            cur = row0 + head
            rest = npad - head
            size = MOE_ROWS // 2
            while size >= SUBLANES:
                take = (rest & size) != 0

                @pl.when(take)
                def _(cur=cur, size=size):
                    dst = xs_ref.at[pl.ds(pl.multiple_of(cur, SUBLANES), size)]
                    cp = pltpu.make_async_copy(zero_scr.at[pl.ds(0, size)], dst, zsem)
                    cp.start()
                    cp.wait()

                cur = cur + jnp.where(take, size, 0)
                size //= 2
            return carry

        lax.fori_loop(0, N_EXPERTS, fill, 0)

        def fill_block(b, carry):
            dst = xs_ref.at[pl.ds(pl.multiple_of(b * MOE_ROWS, MOE_ROWS), MOE_ROWS)]
            cp = pltpu.make_async_copy(zero_scr, dst, zsem)
            cp.start()
            cp.wait()
            return carry

        lax.fori_loop(nused_ref[0], xs_ref.shape[0] // MOE_ROWS, fill_block, 0)

    for _ in range(TOP_K):
        _row_copy_wait(hn_ref, xs_ref.at[pl.ds(0, tm)], sem)


def _dispatch(counts, pad_starts, n_used, dest_blocks, hn, n_rows):
    t = hn.shape[0]
    tm = TM_ROW
    return pl.pallas_call(
        _dispatch_kernel,
        grid_spec=pltpu.PrefetchScalarGridSpec(
            num_scalar_prefetch=3,
            grid=(t // tm,),
            in_specs=[pl.BlockSpec((1, 1, TOP_K * tm), lambda i, *_: (i, 0, 0), memory_space=pltpu.SMEM),
                      pl.BlockSpec((tm, D_MODEL), lambda i, *_: (i, 0))],
            out_specs=pl.BlockSpec(memory_space=pl.ANY),
            scratch_shapes=[pltpu.VMEM((MOE_ROWS, D_MODEL), F32), pltpu.SemaphoreType.DMA,
                            pltpu.SemaphoreType.DMA],
        ),
        out_shape=jax.ShapeDtypeStruct((n_rows, D_MODEL), F32),
        compiler_params=pltpu.CompilerParams(dimension_semantics=("arbitrary",)),
        name="dispatch",
    )(counts, pad_starts, n_used, dest_blocks, hn)


def _moe_kernel(be_ref, nused_ref, xs_ref, wg_ref, wu_ref, wd_ref, yb_ref):
    used = pl.program_id(0) < nused_ref[0]

    @pl.when(jnp.logical_not(used))
    def _():
        yb_ref[...] = jnp.zeros_like(yb_ref)

    @pl.when(used)
    def _():
        x = xs_ref[...].astype(BF16)
        gate = jnp.dot(x, wg_ref[0], preferred_element_type=F32)
        up = jnp.dot(x, wu_ref[0], preferred_element_type=F32)
        hid = (jax.nn.silu(gate) * up).astype(BF16)
        yb_ref[...] = jnp.dot(hid, wd_ref[0], preferred_element_type=F32)


def _moe(block_e, n_used, xs, p):
    n_blocks = xs.shape[0] // MOE_ROWS

    def rows(i, be, nu):
        return (jnp.minimum(i, nu[0] - 1), 0)

    def expert(i, be, nu):
        return (be[jnp.minimum(i, nu[0] - 1)], 0, 0)

    return pl.pallas_call(
        _moe_kernel,
        grid_spec=pltpu.PrefetchScalarGridSpec(
            num_scalar_prefetch=2,
            grid=(n_blocks,),
            in_specs=[pl.BlockSpec((MOE_ROWS, D_MODEL), rows),
                      pl.BlockSpec((1, D_MODEL, D_EXPERT), expert),
                      pl.BlockSpec((1, D_MODEL, D_EXPERT), expert),
                      pl.BlockSpec((1, D_EXPERT, D_MODEL), expert)],
            out_specs=pl.BlockSpec((MOE_ROWS, D_MODEL), lambda i, be, nu: (i, 0)),
        ),
        out_shape=jax.ShapeDtypeStruct(xs.shape, F32),
        compiler_params=pltpu.CompilerParams(dimension_semantics=("arbitrary",),
                                             vmem_limit_bytes=VMEM_LIMIT),
        name="moe",
    )(block_e, n_used, xs, p["w_gate_e"], p["w_up_e"], p["w_down_e"])


def _combine_kernel(dest_ref, x1_ref, gate_ref, yb_ref, out_ref, y0_scr, y1_scr, sem):
    tm = x1_ref.shape[0]
    bufs = (y0_scr, y1_scr)

    def issue(r, carry):
        for k in range(TOP_K):
            d = dest_ref[0, 0, TOP_K * r + k]
            pltpu.make_async_copy(yb_ref.at[pl.ds(d, 1)], bufs[k].at[pl.ds(r, 1)], sem).start(priority=k)
        return carry

    lax.fori_loop(0, tm, issue, 0, unroll=8)
    for k in range(TOP_K):
        _row_copy_wait(yb_ref.at[pl.ds(0, tm)], bufs[k], sem)
    gate = gate_ref[...]
    out_ref[...] = x1_ref[...] + (y0_scr[...] * gate[:, 0:1] + y1_scr[...] * gate[:, 1:2])


def _combine(dest_blocks, x1, gate, yb):
    t = x1.shape[0]
    tm = TM_ROW
    return pl.pallas_call(
        _combine_kernel,
        grid=(t // tm,),
        in_specs=[pl.BlockSpec((1, 1, TOP_K * tm), lambda i: (i, 0, 0), memory_space=pltpu.SMEM),
                  pl.BlockSpec((tm, D_MODEL), lambda i: (i, 0)),
                  pl.BlockSpec((tm, LANES), lambda i: (i, 0)),
                  pl.BlockSpec(memory_space=pl.ANY)],
        out_specs=pl.BlockSpec((tm, D_MODEL), lambda i: (i, 0)),
        out_shape=jax.ShapeDtypeStruct((t, D_MODEL), F32),
        scratch_shapes=[pltpu.VMEM((tm, D_MODEL), F32), pltpu.VMEM((tm, D_MODEL), F32),
                        pltpu.SemaphoreType.DMA],
        compiler_params=pltpu.CompilerParams(dimension_semantics=("arbitrary",)),
        name="combine",
    )(dest_blocks, x1, gate, yb)


def _rope_tables(seq):
    half = HEAD_DIM // 2
    inv_freq = ROPE_THETA ** (-jnp.arange(half, dtype=F32) / half)
    ang = jnp.arange(seq).astype(F32)[:, None] * inv_freq[None, :]
    cos = jnp.cos(ang)
    sin = jnp.sin(ang)
    return jnp.concatenate([cos, cos], axis=-1), jnp.concatenate([-sin, sin], axis=-1)


def _layer(x, p, seq):
    t = x.shape[0]
    cos, sin = _rope_tables(seq)
    ma, sgb, q, k, v = _inproj(x, p, cos, sin, seq)
    x1, hn, route, gate, counts_f = _attn(x, ma, sgb, q, k, v, p, seq)

    counts = counts_f[0, :N_EXPERTS].astype(I32)
    padded = (counts + MOE_ROWS - 1) // MOE_ROWS * MOE_ROWS
    pad_ends = jnp.cumsum(padded)
    pad_starts = pad_ends - padded
    n_blocks = (t * TOP_K) // MOE_ROWS + N_EXPERTS
    block_start = jnp.arange(n_blocks, dtype=I32) * MOE_ROWS
    block_e = jnp.sum((block_start[:, None] >= pad_ends[None, :]).astype(I32), axis=1)
    block_e = jnp.minimum(block_e, N_EXPERTS - 1)
    n_used = (pad_ends[-1:] // MOE_ROWS).astype(I32)
    starts_row = jnp.zeros((1, LANES), F32).at[0, :N_EXPERTS].set(pad_starts.astype(F32))

    dest = _dest(route, starts_row)
    dest_blocks = dest[:, :TOP_K].reshape(t // TM_ROW, 1, TOP_K * TM_ROW)
    xs = _dispatch(counts, pad_starts.astype(I32), n_used, dest_blocks, hn, n_blocks * MOE_ROWS)
    yb = _moe(block_e, n_used, xs, p)
    return _combine(dest_blocks, x1, gate, yb)


def kernel(x_prompt, x_sample, norm_mix_g, w_in, norm_v_g, w_spatial, b_spatial, q_norm_g, k_norm_g, sink,
           w_proj_a, w_proj_b, w_out, norm_ffn_g, w_router_group, b_router_group, w_router_expert,
           b_router_expert, w_gate_e, w_up_e, w_down_e):
    depth = w_in.shape[0]
    layers = []
    for l in range(depth):
        w_router = jnp.zeros((D_MODEL, LANES), F32)
        w_router = w_router.at[:, :N_GROUPS].set(w_router_group[l])
        w_router = w_router.at[:, N_GROUPS:N_GROUPS + N_EXPERTS].set(w_router_expert[l])
        b_router = jnp.zeros((1, LANES), F32)
        b_router = b_router.at[0, :N_GROUPS].set(b_router_group[l])
        b_router = b_router.at[0, N_GROUPS:N_GROUPS + N_EXPERTS].set(b_router_expert[l])
        layers.append(dict(
            norm_mix_g=norm_mix_g[l][None], w_in=w_in[l].astype(BF16), norm_v_g=norm_v_g[l][None],
            w_spatial=w_spatial[l].astype(BF16),
            b_spatial=jnp.broadcast_to(b_spatial[l][:, :, None], (A_GROUPS, CHUNK, LANES)),
            q_norm_g=q_norm_g[l][None], k_norm_g=k_norm_g[l][None], sink=sink[l],
            w_proj_a=w_proj_a[l].astype(BF16), w_proj_b=w_proj_b[l].astype(BF16), w_out=w_out[l].astype(BF16),
            norm_ffn_g=norm_ffn_g[l][None], w_router=w_router.astype(BF16), b_router=b_router,
            w_gate_e=w_gate_e[l].astype(BF16), w_up_e=w_up_e[l].astype(BF16), w_down_e=w_down_e[l].astype(BF16)))

    def trunk(x):
        b, s, d = x.shape
        y = x.reshape(b * s, d)
        for p in layers:
            y = _layer(y, p, s)
        return y.reshape(b, s, d)

    return trunk(x_prompt), trunk(x_sample)
```

```python
import functools

import jax
import jax.numpy as jnp
from jax import lax
from jax.experimental import pallas as pl
from jax.experimental.pallas import tpu as pltpu
from jax.experimental.pallas import tpu_sc as plsc

F32 = jnp.float32
BF16 = jnp.bfloat16
I32 = jnp.int32

LANES = 128
SUBLANES = 8
VMEM_BYTES_V7X = 64 * 1024 * 1024
SC_CORES = 2
SC_SUBCORES = 16
SC_WORKERS = SC_CORES * SC_SUBCORES
SC_CHUNK = 64

D_MODEL = 1024
A_WIDTH = D_MODEL
A_GROUPS = 8
CHUNK = 128
HEAD_DIM = 128
N_Q_HEADS = D_MODEL // HEAD_DIM
N_KV_HEADS = 2
REP = N_Q_HEADS // N_KV_HEADS
WINDOW = 128
ROPE_THETA = 10000.0
Q_W = N_Q_HEADS * HEAD_DIM
KV_W = N_KV_HEADS * HEAD_DIM
IN_W = 2 * A_WIDTH + Q_W + 2 * KV_W + 2 * D_MODEL
COL_U = 0
COL_V = COL_U + A_WIDTH
COL_Q = COL_V + A_WIDTH
COL_K = COL_Q + Q_W
COL_VA = COL_K + KV_W
COL_GA = COL_VA + KV_W
COL_GB = COL_GA + D_MODEL
N_GROUPS = 4
EXPERTS_PER_GROUP = 8
N_EXPERTS = N_GROUPS * EXPERTS_PER_GROUP
TOP_K = 2
D_EXPERT = 512
EPS = 1e-6
NEG = -1e30

TM_IN = 256
TQ = 256
TM_ROW = 512
TM_DEST = 2048
MOE_ROWS = 256
VMEM_LIMIT = 56 * 1024 * 1024
assert VMEM_LIMIT < VMEM_BYTES_V7X


def _rms(x, g):
    return x * lax.rsqrt(jnp.mean(x * x, axis=-1, keepdims=True) + EPS) * g


ROW3 = (D_MODEL // LANES, LANES)
assert ROW3[0] == SUBLANES


def _as_rows3(a):
    return a.reshape((a.shape[0] // SUBLANES,) + ROW3)


def _as_lines(a):
    return a.reshape((a.shape[0] * SUBLANES, LANES))


def _store_rows3(lines_ref, val):
    rows = val.shape[0]
    for s in range(SUBLANES):
        lines_ref[pl.ds(s, rows, stride=SUBLANES), :] = val[:, s * LANES:(s + 1) * LANES]


def _load_rows3(lines_ref):
    rows = lines_ref.shape[0] // SUBLANES
    return jnp.concatenate([lines_ref[pl.ds(s, rows, stride=SUBLANES), :] for s in range(SUBLANES)], axis=1)


def _rows3_spec(rows, index_map):
    return pl.BlockSpec((rows * SUBLANES, LANES), index_map)


def _const_spec(shape):
    nd = len(shape)
    return pl.BlockSpec(shape, lambda *_: (0,) * nd, pipeline_mode=pl.Buffered(1))


def _inproj_kernel(x_ref, gmix_ref, win_ref, gv_ref, ws_ref, bs_ref, gq_ref, gk_ref, cos_ref, sin_ref,
                   wpa_ref, ma_ref, sgb_ref, q_ref, k_ref, v_ref, u_scr, vn_scr, a_scr):
    tm = x_ref.shape[0]
    h = _rms(x_ref[...], gmix_ref[...]).astype(BF16)

    def proj(lo, width):
        return jnp.dot(h, win_ref[:, lo:lo + width], preferred_element_type=F32)

    vn_scr[...] = _rms(jax.nn.gelu(proj(COL_V, A_WIDTH)), gv_ref[...]).astype(BF16)
    u_scr[...] = jax.nn.gelu(proj(COL_U, A_WIDTH))
    for c in range(tm // CHUNK):
        rows = slice(c * CHUNK, (c + 1) * CHUNK)
        for g in range(A_GROUPS):
            cols = slice(g * LANES, (g + 1) * LANES)
            mixed = jnp.dot(ws_ref[g], vn_scr[rows, cols], preferred_element_type=F32) + bs_ref[g]
            a_scr[rows, cols] = (u_scr[rows, cols] * mixed).astype(BF16)
    ya = jnp.dot(a_scr[...], wpa_ref[...], preferred_element_type=F32)
    ma_ref[...] = jax.nn.sigmoid(proj(COL_GA, D_MODEL)) * ya
    sgb_ref[...] = jax.nn.sigmoid(proj(COL_GB, D_MODEL))

    cos = cos_ref[...]
    sin = sin_ref[...]

    def norm_rope(z, g):
        zn = _rms(z, g)
        return zn * cos + pltpu.roll(zn, HEAD_DIM // 2, 1) * sin

    qz = proj(COL_Q, Q_W)
    for hd in range(N_Q_HEADS):
        cols = slice(hd * HEAD_DIM, (hd + 1) * HEAD_DIM)
        q_ref[:, cols] = norm_rope(qz[:, cols], gq_ref[...]).astype(BF16)
    kz = proj(COL_K, KV_W)
    for hd in range(N_KV_HEADS):
        cols = slice(hd * HEAD_DIM, (hd + 1) * HEAD_DIM)
        k_ref[:, cols] = norm_rope(kz[:, cols], gk_ref[...]).astype(BF16)
    v_ref[...] = proj(COL_VA, KV_W).astype(BF16)


def _inproj(x, p, cos, sin, seq):
    t = x.shape[0]
    tm = TM_IN
    n_pos = seq // tm
    row = lambda w: pl.BlockSpec((tm, w), lambda i: (i, 0))
    pos = pl.BlockSpec((tm, HEAD_DIM), lambda i: (i % n_pos, 0))
    return pl.pallas_call(
        _inproj_kernel,
        grid=(t // tm,),
        in_specs=[row(D_MODEL), _const_spec((1, D_MODEL)), _const_spec((D_MODEL, IN_W)),
                  _const_spec((1, A_WIDTH)), _const_spec((A_GROUPS, CHUNK, CHUNK)),
                  _const_spec((A_GROUPS, CHUNK, LANES)), _const_spec((1, HEAD_DIM)),
                  _const_spec((1, HEAD_DIM)), pos, pos, _const_spec((A_WIDTH, D_MODEL))],
        out_specs=[row(D_MODEL), row(D_MODEL), row(Q_W), row(KV_W), row(KV_W)],
        out_shape=[jax.ShapeDtypeStruct((t, D_MODEL), F32), jax.ShapeDtypeStruct((t, D_MODEL), F32),
                   jax.ShapeDtypeStruct((t, Q_W), BF16), jax.ShapeDtypeStruct((t, KV_W), BF16),
                   jax.ShapeDtypeStruct((t, KV_W), BF16)],
        scratch_shapes=[pltpu.VMEM((tm, A_WIDTH), F32), pltpu.VMEM((tm, A_WIDTH), BF16),
                        pltpu.VMEM((tm, A_WIDTH), BF16)],
        compiler_params=pltpu.CompilerParams(dimension_semantics=("parallel",),
                                             vmem_limit_bytes=VMEM_LIMIT),
        name="inproj",
    )(x, p["norm_mix_g"], p["w_in"], p["norm_v_g"], p["w_spatial"], p["b_spatial"], p["q_norm_g"],
      p["k_norm_g"], cos, sin, p["w_proj_a"])


def _attn_kernel(sink_ref, x_ref, ma_ref, sgb_ref, q_ref, kp_ref, kc_ref, kn_ref, vp_ref, vc_ref, vn_ref,
                 wpb_ref, wout_ref, gffn_ref, wr_ref, br_ref,
                 x1_ref, hn_ref, route_ref, gate_ref, counts_ref, kcat, vcat, o_scr, s_scr, p_scr,
                 *, tiles_per_seq):
    tq = x_ref.shape[0]
    blk = WINDOW
    i = pl.program_id(0)
    pos_tile = i % tiles_per_seq
    has_prev = pos_tile > 0
    has_next = pos_tile < tiles_per_seq - 1

    kcat[0:blk] = kp_ref[...]
    kcat[blk:blk + tq] = kc_ref[...]
    kcat[blk + tq:] = kn_ref[...]
    vcat[0:blk] = vp_ref[...]
    vcat[blk:blk + tq] = vc_ref[...]
    vcat[blk + tq:] = vn_ref[...]

    qi = lax.broadcasted_iota(I32, (blk, 3 * blk), 0)
    kj = lax.broadcasted_iota(I32, (blk, 3 * blk), 1)
    band = jnp.abs(kj - blk - qi) <= WINDOW
    scale = HEAD_DIM ** -0.5
    n_sub = tq // blk
    pairs = [(j, g) for j in range(n_sub) for g in range(N_KV_HEADS)]

    def keys(ref, j, g):
        return ref[j * blk:(j + 3) * blk, g * HEAD_DIM:(g + 1) * HEAD_DIM]

    def head_cols(g, r):
        hd = g * REP + r
        return slice(hd * HEAD_DIM, (hd + 1) * HEAD_DIM)

    for b, (j, g) in enumerate(pairs):
        rows = slice(j * blk, (j + 1) * blk)
        qs = jnp.concatenate([q_ref[rows, head_cols(g, r)] for r in range(REP)], axis=0)
        s_scr[b] = lax.dot_general(qs, keys(kcat, j, g), (((1,), (1,)), ((), ())),
                                   preferred_element_type=F32)
    for b, (j, g) in enumerate(pairs):
        valid = band
        if j == 0:
            valid = jnp.logical_and(valid, kj >= jnp.where(has_prev, 0, blk))
        if j == n_sub - 1:
            valid = jnp.logical_and(valid, kj < jnp.where(has_next, 3 * blk, 2 * blk))
        for r in range(REP):
            hrows = slice(r * blk, (r + 1) * blk)
            s = jnp.where(valid, s_scr[b, hrows, :] * scale, NEG)
            sink = sink_ref[g * REP + r]
            m = jnp.maximum(jnp.max(s, axis=-1, keepdims=True), sink)
            e = jnp.exp(s - m)
            denom = jnp.sum(e, axis=-1, keepdims=True) + jnp.exp(sink - m)
            p_scr[b, hrows, :] = (e / denom).astype(BF16)
    for b, (j, g) in enumerate(pairs):
        rows = slice(j * blk, (j + 1) * blk)
        o = jnp.dot(p_scr[b], keys(vcat, j, g), preferred_element_type=F32).astype(BF16)
        for r in range(REP):
            o_scr[rows, head_cols(g, r)] = o[r * blk:(r + 1) * blk, :]

    yb = jnp.dot(o_scr[...], wpb_ref[...], preferred_element_type=F32)
    merged = ma_ref[...] + sgb_ref[...] * yb
    x1 = x_ref[...] + jnp.dot(merged.astype(BF16), wout_ref[...], preferred_element_type=F32)
    x1_ref[...] = x1
    hn = _rms(x1, gffn_ref[...])
    _store_rows3(hn_ref, hn)

    logits = jnp.dot(hn.astype(BF16), wr_ref[...], preferred_element_type=F32) + br_ref[...]
    lane = lax.broadcasted_iota(I32, (tq, LANES), 1).astype(F32)
    ninf = -jnp.inf
    first = float(LANES)

    def rmax(a):
        return jnp.max(a, axis=-1, keepdims=True)

    def rsum(a):
        return jnp.sum(a, axis=-1, keepdims=True)

    def first_lane(mask):
        return jnp.min(jnp.where(mask, lane, first), axis=-1, keepdims=True)

    gl = jnp.where(lane < N_GROUPS, logits, ninf)
    gmax = rmax(gl)
    g_sel = first_lane(gl == gmax)
    g_p = 1.0 / rsum(jnp.exp(gl - gmax))
    lo = N_GROUPS + EXPERTS_PER_GROUP * g_sel
    emask = jnp.logical_and(lane >= lo, lane < lo + EXPERTS_PER_GROUP)
    el = jnp.where(emask, logits, ninf)
    ee = jnp.exp(el - rmax(el))
    eprob = jnp.where(emask, ee / rsum(ee), -1.0)
    p1 = rmax(eprob)
    i1 = first_lane(eprob == p1)
    eprob2 = jnp.where(lane == i1, -1.0, eprob)
    p2 = rmax(eprob2)
    i2 = first_lane(eprob2 == p2)
    psum = p1 + p2
    w1 = g_p * p1 / psum
    w2 = g_p * p2 / psum
    e1 = i1 - N_GROUPS
    e2 = i2 - N_GROUPS

    @pl.when(i == 0)
    def _():
        counts_ref[...] = jnp.zeros_like(counts_ref)

    oh1 = lane == e1
    oh2 = lane == e2
    cnt = jnp.where(oh1, 1.0, 0.0) + jnp.where(oh2, 1.0, 0.0)
    ri = lax.broadcasted_iota(I32, (tq, tq), 0)
    ci = lax.broadcasted_iota(I32, (tq, tq), 1)
    tri = jnp.where(ri > ci, 1.0, 0.0).astype(BF16)
    base = counts_ref[0:1, :] + jnp.dot(tri, cnt.astype(BF16), preferred_element_type=F32)
    r1 = rsum(jnp.where(oh1, base, 0.0))
    r2 = rsum(jnp.where(oh2, base, 0.0))
    counts_ref[...] = counts_ref[...] + jnp.sum(cnt, axis=0, keepdims=True)

    route = jnp.where(lane == 0.0, e1, jnp.where(lane == 1.0, e2,
                      jnp.where(lane == 2.0, r1, jnp.where(lane == 3.0, r2, 0.0))))
    route_ref[...] = route.astype(I32)
    gate_ref[...] = jnp.where(lane == 0.0, w1, jnp.where(lane == 1.0, w2, 0.0))


def _attn(x, ma, sgb, q, k, v, p, seq):
    t = x.shape[0]
    tq = TQ
    sub = tq // WINDOW
    last_blk = t // WINDOW - 1
    row = lambda w: pl.BlockSpec((tq, w), lambda i: (i, 0))
    prev = pl.BlockSpec((WINDOW, KV_W), lambda i: (jnp.maximum(i * sub - 1, 0), 0))
    nxt = pl.BlockSpec((WINDOW, KV_W), lambda i: (jnp.minimum((i + 1) * sub, last_blk), 0))
    return pl.pallas_call(
        functools.partial(_attn_kernel, tiles_per_seq=seq // tq),
        grid=(t // tq,),
        in_specs=[pl.BlockSpec(memory_space=pltpu.SMEM),
                  row(D_MODEL), row(D_MODEL), row(D_MODEL), row(Q_W),
                  prev, row(KV_W), nxt, prev, row(KV_W), nxt,
                  _const_spec((Q_W, D_MODEL)), _const_spec((D_MODEL, D_MODEL)), _const_spec((1, D_MODEL)),
                  _const_spec((D_MODEL, LANES)), _const_spec((1, LANES))],
        out_specs=[row(D_MODEL), _rows3_spec(tq, lambda i: (i, 0)), row(LANES), row(LANES),
                   pl.BlockSpec((SUBLANES, LANES), lambda i: (0, 0))],
        out_shape=[jax.ShapeDtypeStruct((t, D_MODEL), F32), jax.ShapeDtypeStruct((t * SUBLANES, LANES), F32),
                   jax.ShapeDtypeStruct((t, LANES), I32), jax.ShapeDtypeStruct((t, LANES), F32),
                   jax.ShapeDtypeStruct((SUBLANES, LANES), F32)],
        scratch_shapes=[pltpu.VMEM((tq + 2 * WINDOW, KV_W), BF16), pltpu.VMEM((tq + 2 * WINDOW, KV_W), BF16),
                        pltpu.VMEM((tq, Q_W), BF16),
                        pltpu.VMEM((sub * N_KV_HEADS, REP * WINDOW, 3 * WINDOW), F32),
                        pltpu.VMEM((sub * N_KV_HEADS, REP * WINDOW, 3 * WINDOW), BF16)],
        compiler_params=pltpu.CompilerParams(dimension_semantics=("arbitrary",),
                                             vmem_limit_bytes=VMEM_LIMIT),
        name="attn",
    )(p["sink"], x, ma, sgb, q, k, k, k, v, v, v, p["w_proj_b"], p["w_out"], p["norm_ffn_g"],
      p["w_router"], p["b_router"])


def _dest_kernel(route_ref, starts_ref, dest_ref):
    route = route_ref[...].astype(F32)
    lane = lax.broadcasted_iota(I32, route.shape, 1).astype(F32)
    starts = starts_ref[...]

    def slot(k):
        e = route[:, k:k + 1]
        start = jnp.sum(jnp.where(lane == e, starts, 0.0), axis=-1, keepdims=True)
        return start + route[:, TOP_K + k:TOP_K + k + 1]

    dest_ref[...] = jnp.where(lane == 0.0, slot(0), jnp.where(lane == 1.0, slot(1), 0.0)).astype(I32)


def _dest(route, pad_starts):
    t = route.shape[0]
    tm = TM_DEST
    return pl.pallas_call(
        _dest_kernel,
        grid=(t // tm,),
        in_specs=[pl.BlockSpec((tm, LANES), lambda i: (i, 0)), _const_spec((1, LANES))],
        out_specs=pl.BlockSpec((tm, LANES), lambda i: (i, 0)),
        out_shape=jax.ShapeDtypeStruct((t, LANES), I32),
        compiler_params=pltpu.CompilerParams(dimension_semantics=("parallel",)),
        name="dest",
    )(route, pad_starts)


def _sc_mesh():
    return plsc.VectorSubcoreMesh(core_axis_name="c", subcore_axis_name="s")


def _sc_worker():
    return lax.axis_index("s") * SC_CORES + lax.axis_index("c")


def _dispatch(hn, dests, n_rows):
    t = hn.shape[0]
    per_worker = t // SC_WORKERS
    n_chunks = per_worker // SC_CHUNK
    idx = pltpu.VMEM((n_chunks, SC_CHUNK), I32)

    @functools.partial(
        pl.kernel, mesh=_sc_mesh(), out_type=jax.ShapeDtypeStruct((n_rows,) + ROW3, F32),
        scratch_types=[idx, idx, pltpu.VMEM((SC_CHUNK,) + ROW3, F32), pltpu.SemaphoreType.DMA])
    def scatter_rows(hn_hbm, d0_hbm, d1_hbm, xs_hbm, i0_v, i1_v, rows_v, sem):
        w = _sc_worker()
        pltpu.sync_copy(d0_hbm.at[pl.ds(w * n_chunks, n_chunks)], i0_v)
        pltpu.sync_copy(d1_hbm.at[pl.ds(w * n_chunks, n_chunks)], i1_v)

        @pl.loop(0, n_chunks)
        def _(j):
            pltpu.sync_copy(hn_hbm.at[pl.ds(w * per_worker + j * SC_CHUNK, SC_CHUNK)], rows_v)
            copies = [pltpu.make_async_copy(rows_v, xs_hbm.at[i_v.at[j]], sem) for i_v in (i0_v, i1_v)]
            for cp in copies:
                cp.start()
            for cp in copies:
                cp.wait()

    return scatter_rows(hn, *dests)


def _padfill_kernel(counts_ref, starts_ref, nused_ref, xs_in_ref, xs_ref, zero_scr, sem):
    del xs_in_ref
    zero_scr[...] = jnp.zeros_like(zero_scr)

    def fill(row0, size):
        cp = pltpu.make_async_copy(zero_scr.at[pl.ds(0, size)], xs_ref.at[pl.ds(row0, size)], sem)
        cp.start()
        cp.wait()

    def fill_expert(e, carry):
        cnt = counts_ref[e]
        npad = (MOE_ROWS - cnt % MOE_ROWS) % MOE_ROWS
        cur = starts_ref[e] + cnt
        size = MOE_ROWS // 2
        while size >= 1:
            take = (npad & size) != 0
            pl.when(take)(functools.partial(fill, cur, size))
            cur = cur + jnp.where(take, size, 0)
            size //= 2
        return carry

    lax.fori_loop(0, N_EXPERTS, fill_expert, 0)

    def fill_block(b, carry):
        fill(b * MOE_ROWS, MOE_ROWS)
        return carry

    lax.fori_loop(nused_ref[0], xs_ref.shape[0] // MOE_ROWS, fill_block, 0)


def _padfill(counts, pad_starts, n_used, xs):
    return pl.pallas_call(
        _padfill_kernel,
        grid_spec=pltpu.PrefetchScalarGridSpec(
            num_scalar_prefetch=3,
            grid=(1,),
            in_specs=[pl.BlockSpec(memory_space=pl.ANY)],
            out_specs=pl.BlockSpec(memory_space=pl.ANY),
            scratch_shapes=[pltpu.VMEM((MOE_ROWS,) + ROW3, F32), pltpu.SemaphoreType.DMA],
        ),
        out_shape=jax.ShapeDtypeStruct(xs.shape, xs.dtype),
        input_output_aliases={3: 0},
        compiler_params=pltpu.CompilerParams(dimension_semantics=("arbitrary",)),
        name="padfill",
    )(counts, pad_starts, n_used, xs)


def _moe_kernel(be_ref, nused_ref, xs_ref, wg_ref, wu_ref, wd_ref, yb_ref):
    used = pl.program_id(0) < nused_ref[0]

    @pl.when(jnp.logical_not(used))
    def _():
        yb_ref[...] = jnp.zeros_like(yb_ref)

    @pl.when(used)
    def _():
        x = _load_rows3(xs_ref).astype(BF16)
        gate = jnp.dot(x, wg_ref[0], preferred_element_type=F32)
        up = jnp.dot(x, wu_ref[0], preferred_element_type=F32)
        hid = (jax.nn.silu(gate) * up).astype(BF16)
        _store_rows3(yb_ref, jnp.dot(hid, wd_ref[0], preferred_element_type=F32))


def _moe(block_e, n_used, xs, p):
    n_blocks = xs.shape[0] // (MOE_ROWS * SUBLANES)

    def rows(i, be, nu):
        return (jnp.minimum(i, nu[0] - 1), 0)

    def expert(i, be, nu):
        return (be[jnp.minimum(i, nu[0] - 1)], 0, 0)

    return pl.pallas_call(
        _moe_kernel,
        grid_spec=pltpu.PrefetchScalarGridSpec(
            num_scalar_prefetch=2,
            grid=(n_blocks,),
            in_specs=[_rows3_spec(MOE_ROWS, rows),
                      pl.BlockSpec((1, D_MODEL, D_EXPERT), expert),
                      pl.BlockSpec((1, D_MODEL, D_EXPERT), expert),
                      pl.BlockSpec((1, D_EXPERT, D_MODEL), expert)],
            out_specs=_rows3_spec(MOE_ROWS, lambda i, be, nu: (i, 0)),
        ),
        out_shape=jax.ShapeDtypeStruct(xs.shape, F32),
        compiler_params=pltpu.CompilerParams(dimension_semantics=("arbitrary",),
                                             vmem_limit_bytes=VMEM_LIMIT),
        name="moe",
    )(block_e, n_used, xs, p["w_gate_e"], p["w_up_e"], p["w_down_e"])


def _gather(yb, dests):
    t = dests[0].shape[0] * dests[0].shape[1]
    per_worker = t // SC_WORKERS
    n_chunks = per_worker // SC_CHUNK
    idx = pltpu.VMEM((n_chunks, SC_CHUNK), I32)
    out = jax.ShapeDtypeStruct((t,) + ROW3, F32)

    @functools.partial(
        pl.kernel, mesh=_sc_mesh(), out_type=(out, out),
        scratch_types=[idx, idx, pltpu.VMEM((SC_CHUNK,) + ROW3, F32), pltpu.SemaphoreType.DMA])
    def gather_rows(yb_hbm, d0_hbm, d1_hbm, y0_hbm, y1_hbm, i0_v, i1_v, rows_v, sem):
        w = _sc_worker()
        pltpu.sync_copy(d0_hbm.at[pl.ds(w * n_chunks, n_chunks)], i0_v)
        pltpu.sync_copy(d1_hbm.at[pl.ds(w * n_chunks, n_chunks)], i1_v)

        @pl.loop(0, n_chunks)
        def _(j):
            rows = pl.ds(w * per_worker + j * SC_CHUNK, SC_CHUNK)
            for i_v, y_hbm in ((i0_v, y0_hbm), (i1_v, y1_hbm)):
                pltpu.async_copy(yb_hbm.at[i_v.at[j]], rows_v, sem).wait()
                pltpu.sync_copy(rows_v, y_hbm.at[rows])

    return gather_rows(yb, *dests)


def _combine_kernel(x1_ref, gate_ref, y0_ref, y1_ref, out_ref):
    gate = gate_ref[...]
    out_ref[...] = x1_ref[...] + (_load_rows3(y0_ref) * gate[:, 0:1] + _load_rows3(y1_ref) * gate[:, 1:2])


def _combine(x1, gate, y0, y1):
    t = x1.shape[0]
    tm = TM_ROW
    return pl.pallas_call(
        _combine_kernel,
        grid=(t // tm,),
        in_specs=[pl.BlockSpec((tm, D_MODEL), lambda i: (i, 0)),
                  pl.BlockSpec((tm, LANES), lambda i: (i, 0)),
                  _rows3_spec(tm, lambda i: (i, 0)), _rows3_spec(tm, lambda i: (i, 0))],
        out_specs=pl.BlockSpec((tm, D_MODEL), lambda i: (i, 0)),
        out_shape=jax.ShapeDtypeStruct((t, D_MODEL), F32),
        compiler_params=pltpu.CompilerParams(dimension_semantics=("parallel",)),
        name="combine",
    )(x1, gate, y0, y1)


def _rope_tables(seq):
    half = HEAD_DIM // 2
    inv_freq = ROPE_THETA ** (-jnp.arange(half, dtype=F32) / half)
    ang = jnp.arange(seq).astype(F32)[:, None] * inv_freq[None, :]
    cos = jnp.cos(ang)
    sin = jnp.sin(ang)
    return jnp.concatenate([cos, cos], axis=-1), jnp.concatenate([-sin, sin], axis=-1)


def _layer(x, p, seq):
    t = x.shape[0]
    cos, sin = _rope_tables(seq)
    ma, sgb, q, k, v = _inproj(x, p, cos, sin, seq)
    x1, hn, route, gate, counts_f = _attn(x, ma, sgb, q, k, v, p, seq)

    counts = counts_f[0, :N_EXPERTS].astype(I32)
    padded = (counts + MOE_ROWS - 1) // MOE_ROWS * MOE_ROWS
    pad_ends = jnp.cumsum(padded)
    pad_starts = pad_ends - padded
    n_blocks = (t * TOP_K) // MOE_ROWS + N_EXPERTS
    block_start = jnp.arange(n_blocks, dtype=I32) * MOE_ROWS
    block_e = jnp.sum((block_start[:, None] >= pad_ends[None, :]).astype(I32), axis=1)
    block_e = jnp.minimum(block_e, N_EXPERTS - 1)
    n_used = (pad_ends[-1:] // MOE_ROWS).astype(I32)
    starts_row = jnp.zeros((1, LANES), F32).at[0, :N_EXPERTS].set(pad_starts.astype(F32))

    dest = _dest(route, starts_row)
    dests = [dest[:, k].reshape(t // SC_CHUNK, SC_CHUNK) for k in range(TOP_K)]
    xs = _dispatch(_as_rows3(hn), dests, n_blocks * MOE_ROWS)
    xs = _padfill(counts, pad_starts.astype(I32), n_used, xs)
    yb = _moe(block_e, n_used, _as_lines(xs), p)
    y0, y1 = _gather(_as_rows3(yb), dests)
    return _combine(x1, gate, _as_lines(y0), _as_lines(y1))


def kernel(x_prompt, x_sample, norm_mix_g, w_in, norm_v_g, w_spatial, b_spatial, q_norm_g, k_norm_g, sink,
           w_proj_a, w_proj_b, w_out, norm_ffn_g, w_router_group, b_router_group, w_router_expert,
           b_router_expert, w_gate_e, w_up_e, w_down_e):
    depth = w_in.shape[0]
    layers = []
    for l in range(depth):
        w_router = jnp.zeros((D_MODEL, LANES), F32)
        w_router = w_router.at[:, :N_GROUPS].set(w_router_group[l])
        w_router = w_router.at[:, N_GROUPS:N_GROUPS + N_EXPERTS].set(w_router_expert[l])
        b_router = jnp.zeros((1, LANES), F32)
        b_router = b_router.at[0, :N_GROUPS].set(b_router_group[l])
        b_router = b_router.at[0, N_GROUPS:N_GROUPS + N_EXPERTS].set(b_router_expert[l])
        layers.append(dict(
            norm_mix_g=norm_mix_g[l][None], w_in=w_in[l].astype(BF16), norm_v_g=norm_v_g[l][None],
            w_spatial=w_spatial[l].astype(BF16),
            b_spatial=jnp.broadcast_to(b_spatial[l][:, :, None], (A_GROUPS, CHUNK, LANES)),
            q_norm_g=q_norm_g[l][None], k_norm_g=k_norm_g[l][None], sink=sink[l],
            w_proj_a=w_proj_a[l].astype(BF16), w_proj_b=w_proj_b[l].astype(BF16), w_out=w_out[l].astype(BF16),
            norm_ffn_g=norm_ffn_g[l][None], w_router=w_router.astype(BF16), b_router=b_router,
            w_gate_e=w_gate_e[l].astype(BF16), w_up_e=w_up_e[l].astype(BF16), w_down_e=w_down_e[l].astype(BF16)))

    def trunk(x):
        b, s, d = x.shape
        y = x.reshape(b * s, d)
        for p in layers:
            y = _layer(y, p, s)
        return y.reshape(b, s, d)

    return trunk(x_prompt), trunk(x_sample)
```

```python
import functools

import jax
import jax.numpy as jnp
from jax import lax
from jax.experimental import pallas as pl
from jax.experimental.pallas import tpu as pltpu
from jax.experimental.pallas import tpu_sc as plsc

F32 = jnp.float32
BF16 = jnp.bfloat16
I32 = jnp.int32

LANES = 128
SUBLANES = 8
VMEM_BYTES_V7X = 64 * 1024 * 1024
SC_CORES = 2
SC_SUBCORES = 16
SC_WORKERS = SC_CORES * SC_SUBCORES
SC_CHUNK = 64

D_MODEL = 1024
A_WIDTH = D_MODEL
A_GROUPS = 8
CHUNK = 128
HEAD_DIM = 128
N_Q_HEADS = D_MODEL // HEAD_DIM
N_KV_HEADS = 2
REP = N_Q_HEADS // N_KV_HEADS
WINDOW = 128
ROPE_THETA = 10000.0
Q_W = N_Q_HEADS * HEAD_DIM
KV_W = N_KV_HEADS * HEAD_DIM
IN_W = 2 * A_WIDTH + Q_W + 2 * KV_W + 2 * D_MODEL
COL_U = 0
COL_V = COL_U + A_WIDTH
COL_Q = COL_V + A_WIDTH
COL_K = COL_Q + Q_W
COL_VA = COL_K + KV_W
COL_GA = COL_VA + KV_W
COL_GB = COL_GA + D_MODEL
N_GROUPS = 4
EXPERTS_PER_GROUP = 8
N_EXPERTS = N_GROUPS * EXPERTS_PER_GROUP
TOP_K = 2
D_EXPERT = 512
EPS = 1e-6
NEG = -1e30

TM_IN = 256
TQ = 256
TM_ROW = 512
TM_DEST = 2048
MOE_ROWS = 256
VMEM_LIMIT = 56 * 1024 * 1024
assert VMEM_LIMIT < VMEM_BYTES_V7X


def _rms(x, g):
    return x * lax.rsqrt(jnp.mean(x * x, axis=-1, keepdims=True) + EPS) * g


ROW3 = (D_MODEL // LANES, LANES)
assert ROW3[0] == SUBLANES


def _as_rows3(a):
    return a.reshape((a.shape[0] // SUBLANES,) + ROW3)


def _as_lines(a):
    return a.reshape((a.shape[0] * SUBLANES, LANES))


def _store_rows3(lines_ref, val):
    rows = val.shape[0]
    for s in range(SUBLANES):
        lines_ref[pl.ds(s, rows, stride=SUBLANES), :] = val[:, s * LANES:(s + 1) * LANES]


def _load_rows3(lines_ref):
    rows = lines_ref.shape[0] // SUBLANES
    return jnp.concatenate([lines_ref[pl.ds(s, rows, stride=SUBLANES), :] for s in range(SUBLANES)], axis=1)


def _rows3_spec(rows, index_map):
    return pl.BlockSpec((rows * SUBLANES, LANES), index_map)


def _const_spec(shape):
    nd = len(shape)
    return pl.BlockSpec(shape, lambda *_: (0,) * nd, pipeline_mode=pl.Buffered(1))


def _inproj_kernel(x_ref, gmix_ref, win_ref, gv_ref, ws_ref, bs_ref, gq_ref, gk_ref, cos_ref, sin_ref,
                   wpa_ref, ma_ref, sgb_ref, q_ref, k_ref, v_ref, u_scr, vn_scr, a_scr):
    tm = x_ref.shape[0]
    h = _rms(x_ref[...], gmix_ref[...]).astype(BF16)

    def proj(lo, width):
        return jnp.dot(h, win_ref[:, lo:lo + width], preferred_element_type=F32)

    vn_scr[...] = _rms(jax.nn.gelu(proj(COL_V, A_WIDTH)), gv_ref[...]).astype(BF16)
    u_scr[...] = jax.nn.gelu(proj(COL_U, A_WIDTH))
    for c in range(tm // CHUNK):
        rows = slice(c * CHUNK, (c + 1) * CHUNK)
        for g in range(A_GROUPS):
            cols = slice(g * LANES, (g + 1) * LANES)
            mixed = jnp.dot(ws_ref[g], vn_scr[rows, cols], preferred_element_type=F32) + bs_ref[g]
            a_scr[rows, cols] = (u_scr[rows, cols] * mixed).astype(BF16)
    ya = jnp.dot(a_scr[...], wpa_ref[...], preferred_element_type=F32)
    ma_ref[...] = jax.nn.sigmoid(proj(COL_GA, D_MODEL)) * ya
    sgb_ref[...] = jax.nn.sigmoid(proj(COL_GB, D_MODEL))

    cos = cos_ref[...]
    sin = sin_ref[...]

    def norm_rope(z, g):
        zn = _rms(z, g)
        return zn * cos + pltpu.roll(zn, HEAD_DIM // 2, 1) * sin

    qz = proj(COL_Q, Q_W)
    for hd in range(N_Q_HEADS):
        cols = slice(hd * HEAD_DIM, (hd + 1) * HEAD_DIM)
        q_ref[:, cols] = norm_rope(qz[:, cols], gq_ref[...]).astype(BF16)
    kz = proj(COL_K, KV_W)
    for hd in range(N_KV_HEADS):
        cols = slice(hd * HEAD_DIM, (hd + 1) * HEAD_DIM)
        k_ref[:, cols] = norm_rope(kz[:, cols], gk_ref[...]).astype(BF16)
    v_ref[...] = proj(COL_VA, KV_W).astype(BF16)


def _inproj(x, p, cos, sin, seq):
    t = x.shape[0]
    tm = TM_IN
    n_pos = seq // tm
    row = lambda w: pl.BlockSpec((tm, w), lambda i: (i, 0))
    pos = pl.BlockSpec((tm, HEAD_DIM), lambda i: (i % n_pos, 0))
    return pl.pallas_call(
        _inproj_kernel,
        grid=(t // tm,),
        in_specs=[row(D_MODEL), _const_spec((1, D_MODEL)), _const_spec((D_MODEL, IN_W)),
                  _const_spec((1, A_WIDTH)), _const_spec((A_GROUPS, CHUNK, CHUNK)),
                  _const_spec((A_GROUPS, CHUNK, LANES)), _const_spec((1, HEAD_DIM)),
                  _const_spec((1, HEAD_DIM)), pos, pos, _const_spec((A_WIDTH, D_MODEL))],
        out_specs=[row(D_MODEL), row(D_MODEL), row(Q_W), row(KV_W), row(KV_W)],
        out_shape=[jax.ShapeDtypeStruct((t, D_MODEL), F32), jax.ShapeDtypeStruct((t, D_MODEL), F32),
                   jax.ShapeDtypeStruct((t, Q_W), BF16), jax.ShapeDtypeStruct((t, KV_W), BF16),
                   jax.ShapeDtypeStruct((t, KV_W), BF16)],
        scratch_shapes=[pltpu.VMEM((tm, A_WIDTH), F32), pltpu.VMEM((tm, A_WIDTH), BF16),
                        pltpu.VMEM((tm, A_WIDTH), BF16)],
        compiler_params=pltpu.CompilerParams(dimension_semantics=("parallel",),
                                             vmem_limit_bytes=VMEM_LIMIT),
        name="inproj",
    )(x, p["norm_mix_g"], p["w_in"], p["norm_v_g"], p["w_spatial"], p["b_spatial"], p["q_norm_g"],
      p["k_norm_g"], cos, sin, p["w_proj_a"])


def _attn_kernel(sink_ref, x_ref, ma_ref, sgb_ref, q_ref, kp_ref, kc_ref, kn_ref, vp_ref, vc_ref, vn_ref,
                 wpb_ref, wout_ref, gffn_ref, wr_ref, br_ref,
                 x1_ref, hn_ref, route_ref, gate_ref, counts_ref, kcat, vcat, o_scr, s_scr, p_scr,
                 *, tiles_per_seq):
    tq = x_ref.shape[0]
    blk = WINDOW
    i = pl.program_id(0)
    pos_tile = i % tiles_per_seq
    has_prev = pos_tile > 0
    has_next = pos_tile < tiles_per_seq - 1

    kcat[0:blk] = kp_ref[...]
    kcat[blk:blk + tq] = kc_ref[...]
    kcat[blk + tq:] = kn_ref[...]
    vcat[0:blk] = vp_ref[...]
    vcat[blk:blk + tq] = vc_ref[...]
    vcat[blk + tq:] = vn_ref[...]

    qi = lax.broadcasted_iota(I32, (blk, 3 * blk), 0)
    kj = lax.broadcasted_iota(I32, (blk, 3 * blk), 1)
    band = jnp.abs(kj - blk - qi) <= WINDOW
    scale = HEAD_DIM ** -0.5
    n_sub = tq // blk
    pairs = [(j, g) for j in range(n_sub) for g in range(N_KV_HEADS)]

    def keys(ref, j, g):
        return ref[j * blk:(j + 3) * blk, g * HEAD_DIM:(g + 1) * HEAD_DIM]

    def head_cols(g, r):
        hd = g * REP + r
        return slice(hd * HEAD_DIM, (hd + 1) * HEAD_DIM)

    for b, (j, g) in enumerate(pairs):
        rows = slice(j * blk, (j + 1) * blk)
        qs = jnp.concatenate([q_ref[rows, head_cols(g, r)] for r in range(REP)], axis=0)
        s_scr[b] = lax.dot_general(qs, keys(kcat, j, g), (((1,), (1,)), ((), ())),
                                   preferred_element_type=F32)
    for b, (j, g) in enumerate(pairs):
        valid = band
        if j == 0:
            valid = jnp.logical_and(valid, kj >= jnp.where(has_prev, 0, blk))
        if j == n_sub - 1:
            valid = jnp.logical_and(valid, kj < jnp.where(has_next, 3 * blk, 2 * blk))
        for r in range(REP):
            hrows = slice(r * blk, (r + 1) * blk)
            s = jnp.where(valid, s_scr[b, hrows, :] * scale, NEG)
            sink = sink_ref[g * REP + r]
            m = jnp.maximum(jnp.max(s, axis=-1, keepdims=True), sink)
            e = jnp.exp(s - m)
            denom = jnp.sum(e, axis=-1, keepdims=True) + jnp.exp(sink - m)
            p_scr[b, hrows, :] = (e / denom).astype(BF16)
    for b, (j, g) in enumerate(pairs):
        rows = slice(j * blk, (j + 1) * blk)
        o = jnp.dot(p_scr[b], keys(vcat, j, g), preferred_element_type=F32).astype(BF16)
        for r in range(REP):
            o_scr[rows, head_cols(g, r)] = o[r * blk:(r + 1) * blk, :]

    yb = jnp.dot(o_scr[...], wpb_ref[...], preferred_element_type=F32)
    merged = ma_ref[...] + sgb_ref[...] * yb
    x1 = x_ref[...] + jnp.dot(merged.astype(BF16), wout_ref[...], preferred_element_type=F32)
    x1_ref[...] = x1
    hn = _rms(x1, gffn_ref[...])
    _store_rows3(hn_ref, hn)

    logits = jnp.dot(hn.astype(BF16), wr_ref[...], preferred_element_type=F32) + br_ref[...]
    lane = lax.broadcasted_iota(I32, (tq, LANES), 1).astype(F32)
    ninf = -jnp.inf
    first = float(LANES)

    def rmax(a):
        return jnp.max(a, axis=-1, keepdims=True)

    def rsum(a):
        return jnp.sum(a, axis=-1, keepdims=True)

    def first_lane(mask):
        return jnp.min(jnp.where(mask, lane, first), axis=-1, keepdims=True)

    gl = jnp.where(lane < N_GROUPS, logits, ninf)
    gmax = rmax(gl)
    g_sel = first_lane(gl == gmax)
    g_p = 1.0 / rsum(jnp.exp(gl - gmax))
    lo = N_GROUPS + EXPERTS_PER_GROUP * g_sel
    emask = jnp.logical_and(lane >= lo, lane < lo + EXPERTS_PER_GROUP)
    el = jnp.where(emask, logits, ninf)
    ee = jnp.exp(el - rmax(el))
    eprob = jnp.where(emask, ee / rsum(ee), -1.0)
    p1 = rmax(eprob)
    i1 = first_lane(eprob == p1)
    eprob2 = jnp.where(lane == i1, -1.0, eprob)
    p2 = rmax(eprob2)
    i2 = first_lane(eprob2 == p2)
    psum = p1 + p2
    w1 = g_p * p1 / psum
    w2 = g_p * p2 / psum
    e1 = i1 - N_GROUPS
    e2 = i2 - N_GROUPS

    @pl.when(i == 0)
    def _():
        counts_ref[...] = jnp.zeros_like(counts_ref)

    oh1 = lane == e1
    oh2 = lane == e2
    cnt = jnp.where(oh1, 1.0, 0.0) + jnp.where(oh2, 1.0, 0.0)
    ri = lax.broadcasted_iota(I32, (tq, tq), 0)
    ci = lax.broadcasted_iota(I32, (tq, tq), 1)
    tri = jnp.where(ri > ci, 1.0, 0.0).astype(BF16)
    base = counts_ref[0:1, :] + jnp.dot(tri, cnt.astype(BF16), preferred_element_type=F32)
    r1 = rsum(jnp.where(oh1, base, 0.0))
    r2 = rsum(jnp.where(oh2, base, 0.0))
    counts_ref[...] = counts_ref[...] + jnp.sum(cnt, axis=0, keepdims=True)

    route = jnp.where(lane == 0.0, e1, jnp.where(lane == 1.0, e2,
                      jnp.where(lane == 2.0, r1, jnp.where(lane == 3.0, r2, 0.0))))
    route_ref[...] = route.astype(I32)
    gate_ref[...] = jnp.where(lane == 0.0, w1, jnp.where(lane == 1.0, w2, 0.0))


def _attn(x, ma, sgb, q, k, v, p, seq):
    t = x.shape[0]
    tq = TQ
    sub = tq // WINDOW
    last_blk = t // WINDOW - 1
    row = lambda w: pl.BlockSpec((tq, w), lambda i: (i, 0))
    prev = pl.BlockSpec((WINDOW, KV_W), lambda i: (jnp.maximum(i * sub - 1, 0), 0))
    nxt = pl.BlockSpec((WINDOW, KV_W), lambda i: (jnp.minimum((i + 1) * sub, last_blk), 0))
    return pl.pallas_call(
        functools.partial(_attn_kernel, tiles_per_seq=seq // tq),
        grid=(t // tq,),
        in_specs=[pl.BlockSpec(memory_space=pltpu.SMEM),
                  row(D_MODEL), row(D_MODEL), row(D_MODEL), row(Q_W),
                  prev, row(KV_W), nxt, prev, row(KV_W), nxt,
                  _const_spec((Q_W, D_MODEL)), _const_spec((D_MODEL, D_MODEL)), _const_spec((1, D_MODEL)),
                  _const_spec((D_MODEL, LANES)), _const_spec((1, LANES))],
        out_specs=[row(D_MODEL), _rows3_spec(tq, lambda i: (i, 0)), row(LANES), row(LANES),
                   pl.BlockSpec((SUBLANES, LANES), lambda i: (0, 0))],
        out_shape=[jax.ShapeDtypeStruct((t, D_MODEL), F32), jax.ShapeDtypeStruct((t * SUBLANES, LANES), F32),
                   jax.ShapeDtypeStruct((t, LANES), I32), jax.ShapeDtypeStruct((t, LANES), F32),
                   jax.ShapeDtypeStruct((SUBLANES, LANES), F32)],
        scratch_shapes=[pltpu.VMEM((tq + 2 * WINDOW, KV_W), BF16), pltpu.VMEM((tq + 2 * WINDOW, KV_W), BF16),
                        pltpu.VMEM((tq, Q_W), BF16),
                        pltpu.VMEM((sub * N_KV_HEADS, REP * WINDOW, 3 * WINDOW), F32),
                        pltpu.VMEM((sub * N_KV_HEADS, REP * WINDOW, 3 * WINDOW), BF16)],
        compiler_params=pltpu.CompilerParams(dimension_semantics=("arbitrary",),
                                             vmem_limit_bytes=VMEM_LIMIT),
        name="attn",
    )(p["sink"], x, ma, sgb, q, k, k, k, v, v, v, p["w_proj_b"], p["w_out"], p["norm_ffn_g"],
      p["w_router"], p["b_router"])


def _dest_kernel(route_ref, starts_ref, dest_ref):
    route = route_ref[...].astype(F32)
    lane = lax.broadcasted_iota(I32, route.shape, 1).astype(F32)
    starts = starts_ref[...]

    def slot(k):
        e = route[:, k:k + 1]
        start = jnp.sum(jnp.where(lane == e, starts, 0.0), axis=-1, keepdims=True)
        return start + route[:, TOP_K + k:TOP_K + k + 1]

    dest_ref[...] = jnp.where(lane == 0.0, slot(0), jnp.where(lane == 1.0, slot(1), 0.0)).astype(I32)


def _dest(route, pad_starts):
    t = route.shape[0]
    tm = TM_DEST
    return pl.pallas_call(
        _dest_kernel,
        grid=(t // tm,),
        in_specs=[pl.BlockSpec((tm, LANES), lambda i: (i, 0)), _const_spec((1, LANES))],
        out_specs=pl.BlockSpec((tm, LANES), lambda i: (i, 0)),
        out_shape=jax.ShapeDtypeStruct((t, LANES), I32),
        compiler_params=pltpu.CompilerParams(dimension_semantics=("parallel",)),
        name="dest",
    )(route, pad_starts)


def _sc_mesh():
    return plsc.VectorSubcoreMesh(core_axis_name="c", subcore_axis_name="s")


def _sc_worker():
    return lax.axis_index("s") * SC_CORES + lax.axis_index("c")


def _dispatch(hn, dests, n_rows):
    t = hn.shape[0]
    per_worker = t // SC_WORKERS
    n_chunks = per_worker // SC_CHUNK
    idx = pltpu.VMEM((n_chunks, SC_CHUNK), I32)

    @functools.partial(
        pl.kernel, mesh=_sc_mesh(), out_type=jax.ShapeDtypeStruct((n_rows,) + ROW3, F32),
        scratch_types=[idx, idx, pltpu.VMEM((SC_CHUNK,) + ROW3, F32), pltpu.SemaphoreType.DMA])
    def scatter_rows(hn_hbm, d0_hbm, d1_hbm, xs_hbm, i0_v, i1_v, rows_v, sem):
        w = _sc_worker()
        pltpu.sync_copy(d0_hbm.at[pl.ds(w * n_chunks, n_chunks)], i0_v)
        pltpu.sync_copy(d1_hbm.at[pl.ds(w * n_chunks, n_chunks)], i1_v)

        @pl.loop(0, n_chunks)
        def _(j):
            pltpu.sync_copy(hn_hbm.at[pl.ds(w * per_worker + j * SC_CHUNK, SC_CHUNK)], rows_v)
            copies = [pltpu.make_async_copy(rows_v, xs_hbm.at[i_v.at[j]], sem) for i_v in (i0_v, i1_v)]
            for cp in copies:
                cp.start()
            for cp in copies:
                cp.wait()

    return scatter_rows(hn, *dests)


def _moe_kernel(be_ref, nvalid_ref, nused_ref, xs_ref, wg_ref, wu_ref, wd_ref, yb_ref, wg_scr, wu_scr, wd_scr):
    i = pl.program_id(0)
    used = i < nused_ref[0]

    @pl.when(jnp.logical_not(used))
    def _():
        yb_ref[...] = jnp.zeros_like(yb_ref)

    @pl.when(used)
    def _():
        @pl.when(jnp.logical_or(i == 0, be_ref[i] != be_ref[jnp.maximum(i - 1, 0)]))
        def _():
            wg_scr[...] = wg_ref[0].astype(BF16)
            wu_scr[...] = wu_ref[0].astype(BF16)
            wd_scr[...] = wd_ref[0].astype(BF16)

        row = lax.broadcasted_iota(I32, (MOE_ROWS, 1), 0)
        x = jnp.where(row < nvalid_ref[i], _load_rows3(xs_ref), 0.0).astype(BF16)
        gate = jnp.dot(x, wg_scr[...], preferred_element_type=F32)
        up = jnp.dot(x, wu_scr[...], preferred_element_type=F32)
        hid = (jax.nn.silu(gate) * up).astype(BF16)
        _store_rows3(yb_ref, jnp.dot(hid, wd_scr[...], preferred_element_type=F32))


def _moe(block_e, n_valid, n_used, xs, p):
    n_blocks = xs.shape[0] // (MOE_ROWS * SUBLANES)

    def rows(i, be, nv, nu):
        return (jnp.minimum(i, nu[0] - 1), 0)

    def expert(i, be, nv, nu):
        return (be[jnp.minimum(i, nu[0] - 1)], 0, 0)

    return pl.pallas_call(
        _moe_kernel,
        grid_spec=pltpu.PrefetchScalarGridSpec(
            num_scalar_prefetch=3,
            grid=(n_blocks,),
            in_specs=[_rows3_spec(MOE_ROWS, rows),
                      pl.BlockSpec((1, D_MODEL, D_EXPERT), expert),
                      pl.BlockSpec((1, D_MODEL, D_EXPERT), expert),
                      pl.BlockSpec((1, D_EXPERT, D_MODEL), expert)],
            out_specs=_rows3_spec(MOE_ROWS, lambda i, be, nv, nu: (i, 0)),
            scratch_shapes=[pltpu.VMEM((D_MODEL, D_EXPERT), BF16), pltpu.VMEM((D_MODEL, D_EXPERT), BF16),
                            pltpu.VMEM((D_EXPERT, D_MODEL), BF16)],
        ),
        out_shape=jax.ShapeDtypeStruct(xs.shape, F32),
        compiler_params=pltpu.CompilerParams(dimension_semantics=("arbitrary",),
                                             vmem_limit_bytes=VMEM_LIMIT),
        name="moe",
    )(block_e, n_valid, n_used, xs, p["w_gate_e"], p["w_up_e"], p["w_down_e"])


def _gather(yb, dests):
    t = dests[0].shape[0] * dests[0].shape[1]
    per_worker = t // SC_WORKERS
    n_chunks = per_worker // SC_CHUNK
    idx = pltpu.VMEM((n_chunks, SC_CHUNK), I32)
    out = jax.ShapeDtypeStruct((t,) + ROW3, F32)

    @functools.partial(
        pl.kernel, mesh=_sc_mesh(), out_type=(out, out),
        scratch_types=[idx, idx, pltpu.VMEM((SC_CHUNK,) + ROW3, F32), pltpu.SemaphoreType.DMA])
    def gather_rows(yb_hbm, d0_hbm, d1_hbm, y0_hbm, y1_hbm, i0_v, i1_v, rows_v, sem):
        w = _sc_worker()
        pltpu.sync_copy(d0_hbm.at[pl.ds(w * n_chunks, n_chunks)], i0_v)
        pltpu.sync_copy(d1_hbm.at[pl.ds(w * n_chunks, n_chunks)], i1_v)

        @pl.loop(0, n_chunks)
        def _(j):
            rows = pl.ds(w * per_worker + j * SC_CHUNK, SC_CHUNK)
            for i_v, y_hbm in ((i0_v, y0_hbm), (i1_v, y1_hbm)):
                pltpu.async_copy(yb_hbm.at[i_v.at[j]], rows_v, sem).wait()
                pltpu.sync_copy(rows_v, y_hbm.at[rows])

    return gather_rows(yb, *dests)


def _combine_kernel(x1_ref, gate_ref, y0_ref, y1_ref, out_ref):
    gate = gate_ref[...]
    out_ref[...] = x1_ref[...] + (_load_rows3(y0_ref) * gate[:, 0:1] + _load_rows3(y1_ref) * gate[:, 1:2])


def _combine(x1, gate, y0, y1):
    t = x1.shape[0]
    tm = TM_ROW
    return pl.pallas_call(
        _combine_kernel,
        grid=(t // tm,),
        in_specs=[pl.BlockSpec((tm, D_MODEL), lambda i: (i, 0)),
                  pl.BlockSpec((tm, LANES), lambda i: (i, 0)),
                  _rows3_spec(tm, lambda i: (i, 0)), _rows3_spec(tm, lambda i: (i, 0))],
        out_specs=pl.BlockSpec((tm, D_MODEL), lambda i: (i, 0)),
        out_shape=jax.ShapeDtypeStruct((t, D_MODEL), F32),
        compiler_params=pltpu.CompilerParams(dimension_semantics=("parallel",)),
        name="combine",
    )(x1, gate, y0, y1)


def _rope_tables(seq):
    half = HEAD_DIM // 2
    inv_freq = ROPE_THETA ** (-jnp.arange(half, dtype=F32) / half)
    ang = jnp.arange(seq).astype(F32)[:, None] * inv_freq[None, :]
    cos = jnp.cos(ang)
    sin = jnp.sin(ang)
    return jnp.concatenate([cos, cos], axis=-1), jnp.concatenate([-sin, sin], axis=-1)


def _layer(x, p, seq):
    t = x.shape[0]
    cos, sin = _rope_tables(seq)
    ma, sgb, q, k, v = _inproj(x, p, cos, sin, seq)
    x1, hn, route, gate, counts_f = _attn(x, ma, sgb, q, k, v, p, seq)

    counts = counts_f[0, :N_EXPERTS].astype(I32)
    padded = (counts + MOE_ROWS - 1) // MOE_ROWS * MOE_ROWS
    pad_ends = jnp.cumsum(padded)
    pad_starts = pad_ends - padded
    n_blocks = (t * TOP_K) // MOE_ROWS + N_EXPERTS
    block_start = jnp.arange(n_blocks, dtype=I32) * MOE_ROWS
    block_e = jnp.sum((block_start[:, None] >= pad_ends[None, :]).astype(I32), axis=1)
    block_e = jnp.minimum(block_e, N_EXPERTS - 1)
    n_used = (pad_ends[-1:] // MOE_ROWS).astype(I32)
    starts_row = jnp.zeros((1, LANES), F32).at[0, :N_EXPERTS].set(pad_starts.astype(F32))

    dest = _dest(route, starts_row)
    dests = [dest[:, k].reshape(t // SC_CHUNK, SC_CHUNK) for k in range(TOP_K)]
    n_valid = jnp.clip(counts[block_e] - (block_start - pad_starts[block_e]), 0, MOE_ROWS).astype(I32)
    xs = _dispatch(_as_rows3(hn), dests, n_blocks * MOE_ROWS)
    yb = _moe(block_e, n_valid, n_used, _as_lines(xs), p)
    y0, y1 = _gather(_as_rows3(yb), dests)
    return _combine(x1, gate, _as_lines(y0), _as_lines(y1))


def kernel(x_prompt, x_sample, norm_mix_g, w_in, norm_v_g, w_spatial, b_spatial, q_norm_g, k_norm_g, sink,
           w_proj_a, w_proj_b, w_out, norm_ffn_g, w_router_group, b_router_group, w_router_expert,
           b_router_expert, w_gate_e, w_up_e, w_down_e):
    depth = w_in.shape[0]
    layers = []
    for l in range(depth):
        w_router = jnp.zeros((D_MODEL, LANES), F32)
        w_router = w_router.at[:, :N_GROUPS].set(w_router_group[l])
        w_router = w_router.at[:, N_GROUPS:N_GROUPS + N_EXPERTS].set(w_router_expert[l])
        b_router = jnp.zeros((1, LANES), F32)
        b_router = b_router.at[0, :N_GROUPS].set(b_router_group[l])
        b_router = b_router.at[0, N_GROUPS:N_GROUPS + N_EXPERTS].set(b_router_expert[l])
        layers.append(dict(
            norm_mix_g=norm_mix_g[l][None], w_in=w_in[l].astype(BF16), norm_v_g=norm_v_g[l][None],
            w_spatial=w_spatial[l].astype(BF16),
            b_spatial=jnp.broadcast_to(b_spatial[l][:, :, None], (A_GROUPS, CHUNK, LANES)),
            q_norm_g=q_norm_g[l][None], k_norm_g=k_norm_g[l][None], sink=sink[l],
            w_proj_a=w_proj_a[l].astype(BF16), w_proj_b=w_proj_b[l].astype(BF16), w_out=w_out[l].astype(BF16),
            norm_ffn_g=norm_ffn_g[l][None], w_router=w_router.astype(BF16), b_router=b_router,
            w_gate_e=w_gate_e[l], w_up_e=w_up_e[l], w_down_e=w_down_e[l]))

    def trunk(x):
        b, s, d = x.shape
        y = x.reshape(b * s, d)
        for p in layers:
            y = _layer(y, p, s)
        return y.reshape(b, s, d)

    return trunk(x_prompt), trunk(x_sample)
```

```python
import functools

import jax
import jax.numpy as jnp
from jax import lax
from jax.experimental import pallas as pl
from jax.experimental.pallas import tpu as pltpu
from jax.experimental.pallas import tpu_sc as plsc

F32 = jnp.float32
BF16 = jnp.bfloat16
I32 = jnp.int32

LANES = 128
SUBLANES = 8
VMEM_BYTES_V7X = 64 * 1024 * 1024
SC_CORES = 2
SC_SUBCORES = 16
SC_WORKERS = SC_CORES * SC_SUBCORES
SC_CHUNK = 64

D_MODEL = 1024
A_WIDTH = D_MODEL
A_GROUPS = 8
CHUNK = 128
HEAD_DIM = 128
N_Q_HEADS = D_MODEL // HEAD_DIM
N_KV_HEADS = 2
REP = N_Q_HEADS // N_KV_HEADS
WINDOW = 128
ROPE_THETA = 10000.0
Q_W = N_Q_HEADS * HEAD_DIM
KV_W = N_KV_HEADS * HEAD_DIM
IN_W = 2 * A_WIDTH + Q_W + 2 * KV_W + 2 * D_MODEL
COL_U = 0
COL_V = COL_U + A_WIDTH
COL_Q = COL_V + A_WIDTH
COL_K = COL_Q + Q_W
COL_VA = COL_K + KV_W
COL_GA = COL_VA + KV_W
COL_GB = COL_GA + D_MODEL
N_GROUPS = 4
EXPERTS_PER_GROUP = 8
N_EXPERTS = N_GROUPS * EXPERTS_PER_GROUP
TOP_K = 2
D_EXPERT = 512
EPS = 1e-6
NEG = -1e30

TM_IN = 256
TQ = 256
TM_ROW = 512
TM_DEST = 2048
MOE_ROWS_MAX = 512
MOE_MIN_BLOCKS = 4
VMEM_LIMIT = 56 * 1024 * 1024
assert VMEM_LIMIT < VMEM_BYTES_V7X


def _rms(x, g):
    return x * lax.rsqrt(jnp.mean(x * x, axis=-1, keepdims=True) + EPS) * g


ROW3 = (D_MODEL // LANES, LANES)
assert ROW3[0] == SUBLANES


def _as_rows3(a):
    return a.reshape((a.shape[0] // SUBLANES,) + ROW3)


def _as_lines(a):
    return a.reshape((a.shape[0] * SUBLANES, LANES))


def _store_rows3(lines_ref, val):
    rows = val.shape[0]
    for s in range(SUBLANES):
        lines_ref[pl.ds(s, rows, stride=SUBLANES), :] = val[:, s * LANES:(s + 1) * LANES]


def _load_rows3(lines_ref):
    rows = lines_ref.shape[0] // SUBLANES
    return jnp.concatenate([lines_ref[pl.ds(s, rows, stride=SUBLANES), :] for s in range(SUBLANES)], axis=1)


def _rows3_spec(rows, index_map):
    return pl.BlockSpec((rows * SUBLANES, LANES), index_map)


def _const_spec(shape):
    nd = len(shape)
    return pl.BlockSpec(shape, lambda *_: (0,) * nd, pipeline_mode=pl.Buffered(1))


def _inproj_kernel(x_ref, gmix_ref, win_ref, gv_ref, ws_ref, bs_ref, gq_ref, gk_ref, cos_ref, sin_ref,
                   wpa_ref, ma_ref, sgb_ref, q_ref, k_ref, v_ref, u_scr, vn_scr, a_scr):
    tm = x_ref.shape[0]
    h = _rms(x_ref[...], gmix_ref[...]).astype(BF16)

    def proj(lo, width):
        return jnp.dot(h, win_ref[:, lo:lo + width], preferred_element_type=F32)

    vn_scr[...] = _rms(jax.nn.gelu(proj(COL_V, A_WIDTH)), gv_ref[...]).astype(BF16)
    u_scr[...] = jax.nn.gelu(proj(COL_U, A_WIDTH))
    for c in range(tm // CHUNK):
        rows = slice(c * CHUNK, (c + 1) * CHUNK)
        for g in range(A_GROUPS):
            cols = slice(g * LANES, (g + 1) * LANES)
            mixed = jnp.dot(ws_ref[g], vn_scr[rows, cols], preferred_element_type=F32) + bs_ref[g]
            a_scr[rows, cols] = (u_scr[rows, cols] * mixed).astype(BF16)
    ya = jnp.dot(a_scr[...], wpa_ref[...], preferred_element_type=F32)
    ma_ref[...] = jax.nn.sigmoid(proj(COL_GA, D_MODEL)) * ya
    sgb_ref[...] = jax.nn.sigmoid(proj(COL_GB, D_MODEL))

    cos = cos_ref[...]
    sin = sin_ref[...]

    def norm_rope(z, g):
        zn = _rms(z, g)
        return zn * cos + pltpu.roll(zn, HEAD_DIM // 2, 1) * sin

    qz = proj(COL_Q, Q_W)
    for hd in range(N_Q_HEADS):
        cols = slice(hd * HEAD_DIM, (hd + 1) * HEAD_DIM)
        q_ref[:, cols] = norm_rope(qz[:, cols], gq_ref[...]).astype(BF16)
    kz = proj(COL_K, KV_W)
    for hd in range(N_KV_HEADS):
        cols = slice(hd * HEAD_DIM, (hd + 1) * HEAD_DIM)
        k_ref[:, cols] = norm_rope(kz[:, cols], gk_ref[...]).astype(BF16)
    v_ref[...] = proj(COL_VA, KV_W).astype(BF16)


def _inproj(x, p, cos, sin, seq):
    t = x.shape[0]
    tm = TM_IN
    n_pos = seq // tm
    row = lambda w: pl.BlockSpec((tm, w), lambda i: (i, 0))
    pos = pl.BlockSpec((tm, HEAD_DIM), lambda i: (i % n_pos, 0))
    return pl.pallas_call(
        _inproj_kernel,
        grid=(t // tm,),
        in_specs=[row(D_MODEL), _const_spec((1, D_MODEL)), _const_spec((D_MODEL, IN_W)),
                  _const_spec((1, A_WIDTH)), _const_spec((A_GROUPS, CHUNK, CHUNK)),
                  _const_spec((A_GROUPS, CHUNK, LANES)), _const_spec((1, HEAD_DIM)),
                  _const_spec((1, HEAD_DIM)), pos, pos, _const_spec((A_WIDTH, D_MODEL))],
        out_specs=[row(D_MODEL), row(D_MODEL), row(Q_W), row(KV_W), row(KV_W)],
        out_shape=[jax.ShapeDtypeStruct((t, D_MODEL), F32), jax.ShapeDtypeStruct((t, D_MODEL), F32),
                   jax.ShapeDtypeStruct((t, Q_W), BF16), jax.ShapeDtypeStruct((t, KV_W), BF16),
                   jax.ShapeDtypeStruct((t, KV_W), BF16)],
        scratch_shapes=[pltpu.VMEM((tm, A_WIDTH), F32), pltpu.VMEM((tm, A_WIDTH), BF16),
                        pltpu.VMEM((tm, A_WIDTH), BF16)],
        compiler_params=pltpu.CompilerParams(dimension_semantics=("parallel",),
                                             vmem_limit_bytes=VMEM_LIMIT),
        name="inproj",
    )(x, p["norm_mix_g"], p["w_in"], p["norm_v_g"], p["w_spatial"], p["b_spatial"], p["q_norm_g"],
      p["k_norm_g"], cos, sin, p["w_proj_a"])


def _attn_kernel(sink_ref, x_ref, ma_ref, sgb_ref, q_ref, kp_ref, kc_ref, kn_ref, vp_ref, vc_ref, vn_ref,
                 wpb_ref, wout_ref, gffn_ref, wr_ref, br_ref,
                 x1_ref, hn_ref, route_ref, gate_ref, counts_ref, kcat, vcat, o_scr, s_scr, p_scr,
                 *, tiles_per_seq):
    tq = x_ref.shape[0]
    blk = WINDOW
    i = pl.program_id(0)
    pos_tile = i % tiles_per_seq
    has_prev = pos_tile > 0
    has_next = pos_tile < tiles_per_seq - 1

    kcat[0:blk] = kp_ref[...]
    kcat[blk:blk + tq] = kc_ref[...]
    kcat[blk + tq:] = kn_ref[...]
    vcat[0:blk] = vp_ref[...]
    vcat[blk:blk + tq] = vc_ref[...]
    vcat[blk + tq:] = vn_ref[...]

    qi = lax.broadcasted_iota(I32, (blk, 3 * blk), 0)
    kj = lax.broadcasted_iota(I32, (blk, 3 * blk), 1)
    band = jnp.abs(kj - blk - qi) <= WINDOW
    scale = HEAD_DIM ** -0.5
    n_sub = tq // blk
    pairs = [(j, g) for j in range(n_sub) for g in range(N_KV_HEADS)]

    def keys(ref, j, g):
        return ref[j * blk:(j + 3) * blk, g * HEAD_DIM:(g + 1) * HEAD_DIM]

    def head_cols(g, r):
        hd = g * REP + r
        return slice(hd * HEAD_DIM, (hd + 1) * HEAD_DIM)

    for b, (j, g) in enumerate(pairs):
        rows = slice(j * blk, (j + 1) * blk)
        qs = jnp.concatenate([q_ref[rows, head_cols(g, r)] for r in range(REP)], axis=0)
        s_scr[b] = lax.dot_general(qs, keys(kcat, j, g), (((1,), (1,)), ((), ())),
                                   preferred_element_type=F32)
    for b, (j, g) in enumerate(pairs):
        valid = band
        if j == 0:
            valid = jnp.logical_and(valid, kj >= jnp.where(has_prev, 0, blk))
        if j == n_sub - 1:
            valid = jnp.logical_and(valid, kj < jnp.where(has_next, 3 * blk, 2 * blk))
        for r in range(REP):
            hrows = slice(r * blk, (r + 1) * blk)
            s = jnp.where(valid, s_scr[b, hrows, :] * scale, NEG)
            sink = sink_ref[g * REP + r]
            m = jnp.maximum(jnp.max(s, axis=-1, keepdims=True), sink)
            e = jnp.exp(s - m)
            denom = jnp.sum(e, axis=-1, keepdims=True) + jnp.exp(sink - m)
            p_scr[b, hrows, :] = (e / denom).astype(BF16)
    for b, (j, g) in enumerate(pairs):
        rows = slice(j * blk, (j + 1) * blk)
        o = jnp.dot(p_scr[b], keys(vcat, j, g), preferred_element_type=F32).astype(BF16)
        for r in range(REP):
            o_scr[rows, head_cols(g, r)] = o[r * blk:(r + 1) * blk, :]

    yb = jnp.dot(o_scr[...], wpb_ref[...], preferred_element_type=F32)
    merged = ma_ref[...] + sgb_ref[...] * yb
    x1 = x_ref[...] + jnp.dot(merged.astype(BF16), wout_ref[...], preferred_element_type=F32)
    x1_ref[...] = x1
    hn = _rms(x1, gffn_ref[...])
    _store_rows3(hn_ref, hn)

    logits = jnp.dot(hn.astype(BF16), wr_ref[...], preferred_element_type=F32) + br_ref[...]
    lane = lax.broadcasted_iota(I32, (tq, LANES), 1).astype(F32)
    ninf = -jnp.inf
    first = float(LANES)

    def rmax(a):
        return jnp.max(a, axis=-1, keepdims=True)

    def rsum(a):
        return jnp.sum(a, axis=-1, keepdims=True)

    def first_lane(mask):
        return jnp.min(jnp.where(mask, lane, first), axis=-1, keepdims=True)

    gl = jnp.where(lane < N_GROUPS, logits, ninf)
    gmax = rmax(gl)
    g_sel = first_lane(gl == gmax)
    g_p = 1.0 / rsum(jnp.exp(gl - gmax))
    lo = N_GROUPS + EXPERTS_PER_GROUP * g_sel
    emask = jnp.logical_and(lane >= lo, lane < lo + EXPERTS_PER_GROUP)
    el = jnp.where(emask, logits, ninf)
    ee = jnp.exp(el - rmax(el))
    eprob = jnp.where(emask, ee / rsum(ee), -1.0)
    p1 = rmax(eprob)
    i1 = first_lane(eprob == p1)
    eprob2 = jnp.where(lane == i1, -1.0, eprob)
    p2 = rmax(eprob2)
    i2 = first_lane(eprob2 == p2)
    psum = p1 + p2
    w1 = g_p * p1 / psum
    w2 = g_p * p2 / psum
    e1 = i1 - N_GROUPS
    e2 = i2 - N_GROUPS

    @pl.when(i == 0)
    def _():
        counts_ref[...] = jnp.zeros_like(counts_ref)

    oh1 = lane == e1
    oh2 = lane == e2
    cnt = jnp.where(oh1, 1.0, 0.0) + jnp.where(oh2, 1.0, 0.0)
    ri = lax.broadcasted_iota(I32, (tq, tq), 0)
    ci = lax.broadcasted_iota(I32, (tq, tq), 1)
    tri = jnp.where(ri > ci, 1.0, 0.0).astype(BF16)
    base = counts_ref[0:1, :] + jnp.dot(tri, cnt.astype(BF16), preferred_element_type=F32)
    r1 = rsum(jnp.where(oh1, base, 0.0))
    r2 = rsum(jnp.where(oh2, base, 0.0))
    counts_ref[...] = counts_ref[...] + jnp.sum(cnt, axis=0, keepdims=True)

    route = jnp.where(lane == 0.0, e1, jnp.where(lane == 1.0, e2,
                      jnp.where(lane == 2.0, r1, jnp.where(lane == 3.0, r2, 0.0))))
    route_ref[...] = route.astype(I32)
    gate_ref[...] = jnp.where(lane == 0.0, w1, jnp.where(lane == 1.0, w2, 0.0))


def _attn(x, ma, sgb, q, k, v, p, seq):
    t = x.shape[0]
    tq = TQ
    sub = tq // WINDOW
    last_blk = t // WINDOW - 1
    row = lambda w: pl.BlockSpec((tq, w), lambda i: (i, 0))
    prev = pl.BlockSpec((WINDOW, KV_W), lambda i: (jnp.maximum(i * sub - 1, 0), 0))
    nxt = pl.BlockSpec((WINDOW, KV_W), lambda i: (jnp.minimum((i + 1) * sub, last_blk), 0))
    return pl.pallas_call(
        functools.partial(_attn_kernel, tiles_per_seq=seq // tq),
        grid=(t // tq,),
        in_specs=[pl.BlockSpec(memory_space=pltpu.SMEM),
                  row(D_MODEL), row(D_MODEL), row(D_MODEL), row(Q_W),
                  prev, row(KV_W), nxt, prev, row(KV_W), nxt,
                  _const_spec((Q_W, D_MODEL)), _const_spec((D_MODEL, D_MODEL)), _const_spec((1, D_MODEL)),
                  _const_spec((D_MODEL, LANES)), _const_spec((1, LANES))],
        out_specs=[row(D_MODEL), _rows3_spec(tq, lambda i: (i, 0)), row(LANES), row(LANES),
                   pl.BlockSpec((SUBLANES, LANES), lambda i: (0, 0))],
        out_shape=[jax.ShapeDtypeStruct((t, D_MODEL), F32), jax.ShapeDtypeStruct((t * SUBLANES, LANES), F32),
                   jax.ShapeDtypeStruct((t, LANES), I32), jax.ShapeDtypeStruct((t, LANES), F32),
                   jax.ShapeDtypeStruct((SUBLANES, LANES), F32)],
        scratch_shapes=[pltpu.VMEM((tq + 2 * WINDOW, KV_W), BF16), pltpu.VMEM((tq + 2 * WINDOW, KV_W), BF16),
                        pltpu.VMEM((tq, Q_W), BF16),
                        pltpu.VMEM((sub * N_KV_HEADS, REP * WINDOW, 3 * WINDOW), F32),
                        pltpu.VMEM((sub * N_KV_HEADS, REP * WINDOW, 3 * WINDOW), BF16)],
        compiler_params=pltpu.CompilerParams(dimension_semantics=("arbitrary",),
                                             vmem_limit_bytes=VMEM_LIMIT),
        name="attn",
    )(p["sink"], x, ma, sgb, q, k, k, k, v, v, v, p["w_proj_b"], p["w_out"], p["norm_ffn_g"],
      p["w_router"], p["b_router"])


def _dest_kernel(route_ref, starts_ref, dest_ref):
    route = route_ref[...].astype(F32)
    lane = lax.broadcasted_iota(I32, route.shape, 1).astype(F32)
    starts = starts_ref[...]

    def slot(k):
        e = route[:, k:k + 1]
        start = jnp.sum(jnp.where(lane == e, starts, 0.0), axis=-1, keepdims=True)
        return start + route[:, TOP_K + k:TOP_K + k + 1]

    dest_ref[...] = jnp.where(lane == 0.0, slot(0), jnp.where(lane == 1.0, slot(1), 0.0)).astype(I32)


def _dest(route, pad_starts):
    t = route.shape[0]
    tm = TM_DEST
    return pl.pallas_call(
        _dest_kernel,
        grid=(t // tm,),
        in_specs=[pl.BlockSpec((tm, LANES), lambda i: (i, 0)), _const_spec((1, LANES))],
        out_specs=pl.BlockSpec((tm, LANES), lambda i: (i, 0)),
        out_shape=jax.ShapeDtypeStruct((t, LANES), I32),
        compiler_params=pltpu.CompilerParams(dimension_semantics=("parallel",)),
        name="dest",
    )(route, pad_starts)


def _sc_mesh():
    return plsc.VectorSubcoreMesh(core_axis_name="c", subcore_axis_name="s")


def _sc_worker():
    return lax.axis_index("s") * SC_CORES + lax.axis_index("c")


def _dispatch(hn, dests, n_rows):
    t = hn.shape[0]
    per_worker = t // SC_WORKERS
    n_chunks = per_worker // SC_CHUNK
    idx = pltpu.VMEM((n_chunks, SC_CHUNK), I32)

    @functools.partial(
        pl.kernel, mesh=_sc_mesh(), out_type=jax.ShapeDtypeStruct((n_rows,) + ROW3, F32),
        scratch_types=[idx, idx, pltpu.VMEM((SC_CHUNK,) + ROW3, F32), pltpu.SemaphoreType.DMA])
    def scatter_rows(hn_hbm, d0_hbm, d1_hbm, xs_hbm, i0_v, i1_v, rows_v, sem):
        w = _sc_worker()
        pltpu.sync_copy(d0_hbm.at[pl.ds(w * n_chunks, n_chunks)], i0_v)
        pltpu.sync_copy(d1_hbm.at[pl.ds(w * n_chunks, n_chunks)], i1_v)

        @pl.loop(0, n_chunks)
        def _(j):
            pltpu.sync_copy(hn_hbm.at[pl.ds(w * per_worker + j * SC_CHUNK, SC_CHUNK)], rows_v)
            copies = [pltpu.make_async_copy(rows_v, xs_hbm.at[i_v.at[j]], sem) for i_v in (i0_v, i1_v)]
            for cp in copies:
                cp.start()
            for cp in copies:
                cp.wait()

    return scatter_rows(hn, *dests)


def _moe_block_rows(t):
    rows = MOE_ROWS_MAX
    while rows > CHUNK and (t * TOP_K) // N_EXPERTS < MOE_MIN_BLOCKS * rows:
        rows //= 2
    return rows


def _moe_kernel(be_ref, nvalid_ref, nused_ref, xs_ref, wg_ref, wu_ref, wd_ref, yb_ref, wg_scr, wu_scr, wd_scr):
    i = pl.program_id(0)
    used = i < nused_ref[0]

    @pl.when(jnp.logical_not(used))
    def _():
        yb_ref[...] = jnp.zeros_like(yb_ref)

    @pl.when(used)
    def _():
        @pl.when(jnp.logical_or(i == 0, be_ref[i] != be_ref[jnp.maximum(i - 1, 0)]))
        def _():
            wg_scr[...] = wg_ref[0].astype(BF16)
            wu_scr[...] = wu_ref[0].astype(BF16)
            wd_scr[...] = wd_ref[0].astype(BF16)

        row = lax.broadcasted_iota(I32, (xs_ref.shape[0] // SUBLANES, 1), 0)
        x = jnp.where(row < nvalid_ref[i], _load_rows3(xs_ref), 0.0).astype(BF16)
        gate = jnp.dot(x, wg_scr[...], preferred_element_type=F32)
        up = jnp.dot(x, wu_scr[...], preferred_element_type=F32)
        hid = (jax.nn.silu(gate) * up).astype(BF16)
        _store_rows3(yb_ref, jnp.dot(hid, wd_scr[...], preferred_element_type=F32))


def _moe(block_e, n_valid, n_used, xs, p, block_rows):
    n_blocks = xs.shape[0] // (block_rows * SUBLANES)

    def rows(i, be, nv, nu):
        return (jnp.minimum(i, nu[0] - 1), 0)

    def expert(i, be, nv, nu):
        return (be[jnp.minimum(i, nu[0] - 1)], 0, 0)

    return pl.pallas_call(
        _moe_kernel,
        grid_spec=pltpu.PrefetchScalarGridSpec(
            num_scalar_prefetch=3,
            grid=(n_blocks,),
            in_specs=[_rows3_spec(block_rows, rows),
                      pl.BlockSpec((1, D_MODEL, D_EXPERT), expert),
                      pl.BlockSpec((1, D_MODEL, D_EXPERT), expert),
                      pl.BlockSpec((1, D_EXPERT, D_MODEL), expert)],
            out_specs=_rows3_spec(block_rows, lambda i, be, nv, nu: (i, 0)),
            scratch_shapes=[pltpu.VMEM((D_MODEL, D_EXPERT), BF16), pltpu.VMEM((D_MODEL, D_EXPERT), BF16),
                            pltpu.VMEM((D_EXPERT, D_MODEL), BF16)],
        ),
        out_shape=jax.ShapeDtypeStruct(xs.shape, F32),
        compiler_params=pltpu.CompilerParams(dimension_semantics=("arbitrary",),
                                             vmem_limit_bytes=VMEM_LIMIT),
        name="moe",
    )(block_e, n_valid, n_used, xs, p["w_gate_e"], p["w_up_e"], p["w_down_e"])


def _gather(yb, dests):
    t = dests[0].shape[0] * dests[0].shape[1]
    per_worker = t // SC_WORKERS
    n_chunks = per_worker // SC_CHUNK
    idx = pltpu.VMEM((n_chunks, SC_CHUNK), I32)
    out = jax.ShapeDtypeStruct((t,) + ROW3, F32)

    @functools.partial(
        pl.kernel, mesh=_sc_mesh(), out_type=(out, out),
        scratch_types=[idx, idx, pltpu.VMEM((SC_CHUNK,) + ROW3, F32), pltpu.SemaphoreType.DMA])
    def gather_rows(yb_hbm, d0_hbm, d1_hbm, y0_hbm, y1_hbm, i0_v, i1_v, rows_v, sem):
        w = _sc_worker()
        pltpu.sync_copy(d0_hbm.at[pl.ds(w * n_chunks, n_chunks)], i0_v)
        pltpu.sync_copy(d1_hbm.at[pl.ds(w * n_chunks, n_chunks)], i1_v)

        @pl.loop(0, n_chunks)
        def _(j):
            rows = pl.ds(w * per_worker + j * SC_CHUNK, SC_CHUNK)
            for i_v, y_hbm in ((i0_v, y0_hbm), (i1_v, y1_hbm)):
                pltpu.async_copy(yb_hbm.at[i_v.at[j]], rows_v, sem).wait()
                pltpu.sync_copy(rows_v, y_hbm.at[rows])

    return gather_rows(yb, *dests)


def _combine_kernel(x1_ref, gate_ref, y0_ref, y1_ref, out_ref):
    gate = gate_ref[...]
    out_ref[...] = x1_ref[...] + (_load_rows3(y0_ref) * gate[:, 0:1] + _load_rows3(y1_ref) * gate[:, 1:2])


def _combine(x1, gate, y0, y1):
    t = x1.shape[0]
    tm = TM_ROW
    return pl.pallas_call(
        _combine_kernel,
        grid=(t // tm,),
        in_specs=[pl.BlockSpec((tm, D_MODEL), lambda i: (i, 0)),
                  pl.BlockSpec((tm, LANES), lambda i: (i, 0)),
                  _rows3_spec(tm, lambda i: (i, 0)), _rows3_spec(tm, lambda i: (i, 0))],
        out_specs=pl.BlockSpec((tm, D_MODEL), lambda i: (i, 0)),
        out_shape=jax.ShapeDtypeStruct((t, D_MODEL), F32),
        compiler_params=pltpu.CompilerParams(dimension_semantics=("parallel",)),
        name="combine",
    )(x1, gate, y0, y1)


def _rope_tables(seq):
    half = HEAD_DIM // 2
    inv_freq = ROPE_THETA ** (-jnp.arange(half, dtype=F32) / half)
    ang = jnp.arange(seq).astype(F32)[:, None] * inv_freq[None, :]
    cos = jnp.cos(ang)
    sin = jnp.sin(ang)
    return jnp.concatenate([cos, cos], axis=-1), jnp.concatenate([-sin, sin], axis=-1)


def _layer(x, p, seq):
    t = x.shape[0]
    cos, sin = _rope_tables(seq)
    ma, sgb, q, k, v = _inproj(x, p, cos, sin, seq)
    x1, hn, route, gate, counts_f = _attn(x, ma, sgb, q, k, v, p, seq)

    block_rows = _moe_block_rows(t)
    counts = counts_f[0, :N_EXPERTS].astype(I32)
    padded = (counts + block_rows - 1) // block_rows * block_rows
    pad_ends = jnp.cumsum(padded)
    pad_starts = pad_ends - padded
    n_blocks = (t * TOP_K) // block_rows + N_EXPERTS
    block_start = jnp.arange(n_blocks, dtype=I32) * block_rows
    in_expert = jnp.logical_and(block_start[:, None] >= pad_starts[None, :],
                                block_start[:, None] < pad_ends[None, :]).astype(I32)
    block_e = jnp.minimum(jnp.sum((block_start[:, None] >= pad_ends[None, :]).astype(I32), axis=1), N_EXPERTS - 1)
    n_valid = jnp.sum(in_expert * jnp.clip(pad_starts + counts - block_start[:, None], 0, block_rows), axis=1)
    n_used = pad_ends[-1:] // block_rows
    starts_row = jnp.zeros((1, LANES), F32).at[0, :N_EXPERTS].set(pad_starts.astype(F32))

    dest = _dest(route, starts_row)
    dests = [dest[:, k].reshape(t // SC_CHUNK, SC_CHUNK) for k in range(TOP_K)]
    xs = _dispatch(_as_rows3(hn), dests, n_blocks * block_rows)
    yb = _moe(block_e, n_valid, n_used, _as_lines(xs), p, block_rows)
    y0, y1 = _gather(_as_rows3(yb), dests)
    return _combine(x1, gate, _as_lines(y0), _as_lines(y1))


def kernel(x_prompt, x_sample, norm_mix_g, w_in, norm_v_g, w_spatial, b_spatial, q_norm_g, k_norm_g, sink,
           w_proj_a, w_proj_b, w_out, norm_ffn_g, w_router_group, b_router_group, w_router_expert,
           b_router_expert, w_gate_e, w_up_e, w_down_e):
    depth = w_in.shape[0]
    layers = []
    for l in range(depth):
        w_router = jnp.zeros((D_MODEL, LANES), F32)
        w_router = w_router.at[:, :N_GROUPS].set(w_router_group[l])
        w_router = w_router.at[:, N_GROUPS:N_GROUPS + N_EXPERTS].set(w_router_expert[l])
        b_router = jnp.zeros((1, LANES), F32)
        b_router = b_router.at[0, :N_GROUPS].set(b_router_group[l])
        b_router = b_router.at[0, N_GROUPS:N_GROUPS + N_EXPERTS].set(b_router_expert[l])
        layers.append(dict(
            norm_mix_g=norm_mix_g[l][None], w_in=w_in[l].astype(BF16), norm_v_g=norm_v_g[l][None],
            w_spatial=w_spatial[l].astype(BF16),
            b_spatial=jnp.broadcast_to(b_spatial[l][:, :, None], (A_GROUPS, CHUNK, LANES)),
            q_norm_g=q_norm_g[l][None], k_norm_g=k_norm_g[l][None], sink=sink[l],
            w_proj_a=w_proj_a[l].astype(BF16), w_proj_b=w_proj_b[l].astype(BF16), w_out=w_out[l].astype(BF16),
            norm_ffn_g=norm_ffn_g[l][None], w_router=w_router.astype(BF16), b_router=b_router,
            w_gate_e=w_gate_e[l], w_up_e=w_up_e[l], w_down_e=w_down_e[l]))

    def trunk(x):
        b, s, d = x.shape
        y = x.reshape(b * s, d)
        for p in layers:
            y = _layer(y, p, s)
        return y.reshape(b, s, d)

    return trunk(x_prompt), trunk(x_sample)
```

```python
import functools

import jax
import jax.numpy as jnp
from jax import lax
from jax.experimental import pallas as pl
from jax.experimental.pallas import tpu as pltpu
from jax.experimental.pallas import tpu_sc as plsc

F32 = jnp.float32
BF16 = jnp.bfloat16
I32 = jnp.int32
U32 = jnp.uint32

LANES = 128
SUBLANES = 8
VMEM_BYTES_V7X = 64 * 1024 * 1024
SC_CORES = 2
SC_SUBCORES = 16
SC_WORKERS = SC_CORES * SC_SUBCORES
SC_CHUNK_MAX = 128

D_MODEL = 1024
A_WIDTH = D_MODEL
A_GROUPS = 8
CHUNK = 128
HEAD_DIM = 128
N_Q_HEADS = D_MODEL // HEAD_DIM
N_KV_HEADS = 2
REP = N_Q_HEADS // N_KV_HEADS
WINDOW = 128
ROPE_THETA = 10000.0
Q_W = N_Q_HEADS * HEAD_DIM
KV_W = N_KV_HEADS * HEAD_DIM
IN_W = 2 * A_WIDTH + Q_W + 2 * KV_W + 2 * D_MODEL
COL_U = 0
COL_V = COL_U + A_WIDTH
COL_Q = COL_V + A_WIDTH
COL_K = COL_Q + Q_W
COL_VA = COL_K + KV_W
COL_GA = COL_VA + KV_W
COL_GB = COL_GA + D_MODEL
N_GROUPS = 4
EXPERTS_PER_GROUP = 8
N_EXPERTS = N_GROUPS * EXPERTS_PER_GROUP
TOP_K = 2
D_EXPERT = 512
EPS = 1e-6
NEG = -1e30

TM_IN = 256
TQ = 256
TM_ROW = 512
TM_DEST = 2048
MOE_ROWS_MAX = 512
MOE_MIN_BLOCKS = 4
VMEM_LIMIT = 56 * 1024 * 1024
assert VMEM_LIMIT < VMEM_BYTES_V7X


def _rms(x, g):
    return x * lax.rsqrt(jnp.mean(x * x, axis=-1, keepdims=True) + EPS) * g


ROW_LINES = D_MODEL // 2 // LANES
ROW3 = (ROW_LINES, LANES)
HIGH_HALF = 0xFFFF0000


def _as_rows3(a):
    return a.reshape((a.shape[0] // ROW_LINES,) + ROW3)


def _as_lines(a):
    return a.reshape((a.shape[0] * ROW_LINES, LANES))


def _store_rows3(lines_ref, val):
    rows = val.shape[0]
    bits = lax.bitcast_convert_type(val.astype(BF16).astype(F32), U32)
    half = ROW_LINES * LANES
    for s in range(ROW_LINES):
        lo = bits[:, s * LANES:(s + 1) * LANES] >> 16
        hi = bits[:, half + s * LANES:half + (s + 1) * LANES] & U32(HIGH_HALF)
        lines_ref[pl.ds(s, rows, stride=ROW_LINES), :] = lax.bitcast_convert_type(lo | hi, I32)


def _load_rows3(lines_ref):
    rows = lines_ref.shape[0] // ROW_LINES
    words = [lax.bitcast_convert_type(lines_ref[pl.ds(s, rows, stride=ROW_LINES), :], U32)
             for s in range(ROW_LINES)]
    lo = [lax.bitcast_convert_type(w << 16, F32) for w in words]
    hi = [lax.bitcast_convert_type(w & U32(HIGH_HALF), F32) for w in words]
    return jnp.concatenate(lo + hi, axis=1)


def _rows3_spec(rows, index_map):
    return pl.BlockSpec((rows * ROW_LINES, LANES), index_map)


def _const_spec(shape):
    nd = len(shape)
    return pl.BlockSpec(shape, lambda *_: (0,) * nd, pipeline_mode=pl.Buffered(1))


def _inproj_kernel(x_ref, gmix_ref, win_ref, gv_ref, ws_ref, bs_ref, gq_ref, gk_ref, cos_ref, sin_ref,
                   wpa_ref, ma_ref, sgb_ref, q_ref, k_ref, v_ref, u_scr, vn_scr, a_scr):
    tm = x_ref.shape[0]
    h = _rms(x_ref[...], gmix_ref[...]).astype(BF16)

    def proj(lo, width):
        return jnp.dot(h, win_ref[:, lo:lo + width], preferred_element_type=F32)

    vn_scr[...] = _rms(jax.nn.gelu(proj(COL_V, A_WIDTH)), gv_ref[...]).astype(BF16)
    u_scr[...] = jax.nn.gelu(proj(COL_U, A_WIDTH))
    for c in range(tm // CHUNK):
        rows = slice(c * CHUNK, (c + 1) * CHUNK)
        for g in range(A_GROUPS):
            cols = slice(g * LANES, (g + 1) * LANES)
            mixed = jnp.dot(ws_ref[g], vn_scr[rows, cols], preferred_element_type=F32) + bs_ref[g]
            a_scr[rows, cols] = (u_scr[rows, cols] * mixed).astype(BF16)
    ya = jnp.dot(a_scr[...], wpa_ref[...], preferred_element_type=F32)
    ma_ref[...] = jax.nn.sigmoid(proj(COL_GA, D_MODEL)) * ya
    sgb_ref[...] = jax.nn.sigmoid(proj(COL_GB, D_MODEL))

    cos = cos_ref[...]
    sin = sin_ref[...]

    def norm_rope(z, g):
        zn = _rms(z, g)
        return zn * cos + pltpu.roll(zn, HEAD_DIM // 2, 1) * sin

    qz = proj(COL_Q, Q_W)
    for hd in range(N_Q_HEADS):
        cols = slice(hd * HEAD_DIM, (hd + 1) * HEAD_DIM)
        q_ref[:, cols] = norm_rope(qz[:, cols], gq_ref[...]).astype(BF16)
    kz = proj(COL_K, KV_W)
    for hd in range(N_KV_HEADS):
        cols = slice(hd * HEAD_DIM, (hd + 1) * HEAD_DIM)
        k_ref[:, cols] = norm_rope(kz[:, cols], gk_ref[...]).astype(BF16)
    v_ref[...] = proj(COL_VA, KV_W).astype(BF16)


def _inproj(x, p, cos, sin, seq):
    t = x.shape[0]
    tm = TM_IN
    n_pos = seq // tm
    row = lambda w: pl.BlockSpec((tm, w), lambda i: (i, 0))
    pos = pl.BlockSpec((tm, HEAD_DIM), lambda i: (i % n_pos, 0))
    return pl.pallas_call(
        _inproj_kernel,
        grid=(t // tm,),
        in_specs=[row(D_MODEL), _const_spec((1, D_MODEL)), _const_spec((D_MODEL, IN_W)),
                  _const_spec((1, A_WIDTH)), _const_spec((A_GROUPS, CHUNK, CHUNK)),
                  _const_spec((A_GROUPS, CHUNK, LANES)), _const_spec((1, HEAD_DIM)),
                  _const_spec((1, HEAD_DIM)), pos, pos, _const_spec((A_WIDTH, D_MODEL))],
        out_specs=[row(D_MODEL), row(D_MODEL), row(Q_W), row(KV_W), row(KV_W)],
        out_shape=[jax.ShapeDtypeStruct((t, D_MODEL), F32), jax.ShapeDtypeStruct((t, D_MODEL), F32),
                   jax.ShapeDtypeStruct((t, Q_W), BF16), jax.ShapeDtypeStruct((t, KV_W), BF16),
                   jax.ShapeDtypeStruct((t, KV_W), BF16)],
        scratch_shapes=[pltpu.VMEM((tm, A_WIDTH), F32), pltpu.VMEM((tm, A_WIDTH), BF16),
                        pltpu.VMEM((tm, A_WIDTH), BF16)],
        compiler_params=pltpu.CompilerParams(dimension_semantics=("parallel",),
                                             vmem_limit_bytes=VMEM_LIMIT),
        name="inproj",
    )(x, p["norm_mix_g"], p["w_in"], p["norm_v_g"], p["w_spatial"], p["b_spatial"], p["q_norm_g"],
      p["k_norm_g"], cos, sin, p["w_proj_a"])


def _attn_kernel(sink_ref, x_ref, ma_ref, sgb_ref, q_ref, kp_ref, kc_ref, kn_ref, vp_ref, vc_ref, vn_ref,
                 wpb_ref, wout_ref, gffn_ref, wr_ref, br_ref,
                 x1_ref, hn_ref, route_ref, gate_ref, counts_ref, kcat, vcat, o_scr, s_scr, p_scr,
                 *, tiles_per_seq):
    tq = x_ref.shape[0]
    blk = WINDOW
    i = pl.program_id(0)
    pos_tile = i % tiles_per_seq
    has_prev = pos_tile > 0
    has_next = pos_tile < tiles_per_seq - 1

    kcat[0:blk] = kp_ref[...]
    kcat[blk:blk + tq] = kc_ref[...]
    kcat[blk + tq:] = kn_ref[...]
    vcat[0:blk] = vp_ref[...]
    vcat[blk:blk + tq] = vc_ref[...]
    vcat[blk + tq:] = vn_ref[...]

    qi = lax.broadcasted_iota(I32, (blk, 3 * blk), 0)
    kj = lax.broadcasted_iota(I32, (blk, 3 * blk), 1)
    band = jnp.abs(kj - blk - qi) <= WINDOW
    scale = HEAD_DIM ** -0.5
    n_sub = tq // blk
    pairs = [(j, g) for j in range(n_sub) for g in range(N_KV_HEADS)]

    def keys(ref, j, g):
        return ref[j * blk:(j + 3) * blk, g * HEAD_DIM:(g + 1) * HEAD_DIM]

    def head_cols(g, r):
        hd = g * REP + r
        return slice(hd * HEAD_DIM, (hd + 1) * HEAD_DIM)

    for b, (j, g) in enumerate(pairs):
        rows = slice(j * blk, (j + 1) * blk)
        qs = jnp.concatenate([q_ref[rows, head_cols(g, r)] for r in range(REP)], axis=0)
        s_scr[b] = lax.dot_general(qs, keys(kcat, j, g), (((1,), (1,)), ((), ())),
                                   preferred_element_type=F32)
    for b, (j, g) in enumerate(pairs):
        valid = band
        if j == 0:
            valid = jnp.logical_and(valid, kj >= jnp.where(has_prev, 0, blk))
        if j == n_sub - 1:
            valid = jnp.logical_and(valid, kj < jnp.where(has_next, 3 * blk, 2 * blk))
        for r in range(REP):
            hrows = slice(r * blk, (r + 1) * blk)
            s = jnp.where(valid, s_scr[b, hrows, :] * scale, NEG)
            sink = sink_ref[g * REP + r]
            m = jnp.maximum(jnp.max(s, axis=-1, keepdims=True), sink)
            e = jnp.exp(s - m)
            denom = jnp.sum(e, axis=-1, keepdims=True) + jnp.exp(sink - m)
            p_scr[b, hrows, :] = (e / denom).astype(BF16)
    for b, (j, g) in enumerate(pairs):
        rows = slice(j * blk, (j + 1) * blk)
        o = jnp.dot(p_scr[b], keys(vcat, j, g), preferred_element_type=F32).astype(BF16)
        for r in range(REP):
            o_scr[rows, head_cols(g, r)] = o[r * blk:(r + 1) * blk, :]

    yb = jnp.dot(o_scr[...], wpb_ref[...], preferred_element_type=F32)
    merged = ma_ref[...] + sgb_ref[...] * yb
    x1 = x_ref[...] + jnp.dot(merged.astype(BF16), wout_ref[...], preferred_element_type=F32)
    x1_ref[...] = x1
    hn = _rms(x1, gffn_ref[...])
    _store_rows3(hn_ref, hn)

    logits = jnp.dot(hn.astype(BF16), wr_ref[...], preferred_element_type=F32) + br_ref[...]
    lane = lax.broadcasted_iota(I32, (tq, LANES), 1).astype(F32)
    ninf = -jnp.inf
    first = float(LANES)

    def rmax(a):
        return jnp.max(a, axis=-1, keepdims=True)

    def rsum(a):
        return jnp.sum(a, axis=-1, keepdims=True)

    def first_lane(mask):
        return jnp.min(jnp.where(mask, lane, first), axis=-1, keepdims=True)

    gl = jnp.where(lane < N_GROUPS, logits, ninf)
    gmax = rmax(gl)
    g_sel = first_lane(gl == gmax)
    g_p = 1.0 / rsum(jnp.exp(gl - gmax))
    lo = N_GROUPS + EXPERTS_PER_GROUP * g_sel
    emask = jnp.logical_and(lane >= lo, lane < lo + EXPERTS_PER_GROUP)
    el = jnp.where(emask, logits, ninf)
    ee = jnp.exp(el - rmax(el))
    eprob = jnp.where(emask, ee / rsum(ee), -1.0)
    p1 = rmax(eprob)
    i1 = first_lane(eprob == p1)
    eprob2 = jnp.where(lane == i1, -1.0, eprob)
    p2 = rmax(eprob2)
    i2 = first_lane(eprob2 == p2)
    psum = p1 + p2
    w1 = g_p * p1 / psum
    w2 = g_p * p2 / psum
    e1 = i1 - N_GROUPS
    e2 = i2 - N_GROUPS

    @pl.when(i == 0)
    def _():
        counts_ref[...] = jnp.zeros_like(counts_ref)

    oh1 = lane == e1
    oh2 = lane == e2
    cnt = jnp.where(oh1, 1.0, 0.0) + jnp.where(oh2, 1.0, 0.0)
    ri = lax.broadcasted_iota(I32, (tq, tq), 0)
    ci = lax.broadcasted_iota(I32, (tq, tq), 1)
    tri = jnp.where(ri > ci, 1.0, 0.0).astype(BF16)
    base = counts_ref[0:1, :] + jnp.dot(tri, cnt.astype(BF16), preferred_element_type=F32)
    r1 = rsum(jnp.where(oh1, base, 0.0))
    r2 = rsum(jnp.where(oh2, base, 0.0))
    counts_ref[...] = counts_ref[...] + jnp.sum(cnt, axis=0, keepdims=True)

    route = jnp.where(lane == 0.0, e1, jnp.where(lane == 1.0, e2,
                      jnp.where(lane == 2.0, r1, jnp.where(lane == 3.0, r2, 0.0))))
    route_ref[...] = route.astype(I32)
    gate_ref[...] = jnp.where(lane == 0.0, w1, jnp.where(lane == 1.0, w2, 0.0))


def _attn(x, ma, sgb, q, k, v, p, seq):
    t = x.shape[0]
    tq = TQ
    sub = tq // WINDOW
    last_blk = t // WINDOW - 1
    row = lambda w: pl.BlockSpec((tq, w), lambda i: (i, 0))
    prev = pl.BlockSpec((WINDOW, KV_W), lambda i: (jnp.maximum(i * sub - 1, 0), 0))
    nxt = pl.BlockSpec((WINDOW, KV_W), lambda i: (jnp.minimum((i + 1) * sub, last_blk), 0))
    return pl.pallas_call(
        functools.partial(_attn_kernel, tiles_per_seq=seq // tq),
        grid=(t // tq,),
        in_specs=[pl.BlockSpec(memory_space=pltpu.SMEM),
                  row(D_MODEL), row(D_MODEL), row(D_MODEL), row(Q_W),
                  prev, row(KV_W), nxt, prev, row(KV_W), nxt,
                  _const_spec((Q_W, D_MODEL)), _const_spec((D_MODEL, D_MODEL)), _const_spec((1, D_MODEL)),
                  _const_spec((D_MODEL, LANES)), _const_spec((1, LANES))],
        out_specs=[row(D_MODEL), _rows3_spec(tq, lambda i: (i, 0)), row(LANES), row(LANES),
                   pl.BlockSpec((SUBLANES, LANES), lambda i: (0, 0))],
        out_shape=[jax.ShapeDtypeStruct((t, D_MODEL), F32), jax.ShapeDtypeStruct((t * ROW_LINES, LANES), I32),
                   jax.ShapeDtypeStruct((t, LANES), I32), jax.ShapeDtypeStruct((t, LANES), F32),
                   jax.ShapeDtypeStruct((SUBLANES, LANES), F32)],
        scratch_shapes=[pltpu.VMEM((tq + 2 * WINDOW, KV_W), BF16), pltpu.VMEM((tq + 2 * WINDOW, KV_W), BF16),
                        pltpu.VMEM((tq, Q_W), BF16),
                        pltpu.VMEM((sub * N_KV_HEADS, REP * WINDOW, 3 * WINDOW), F32),
                        pltpu.VMEM((sub * N_KV_HEADS, REP * WINDOW, 3 * WINDOW), BF16)],
        compiler_params=pltpu.CompilerParams(dimension_semantics=("arbitrary",),
                                             vmem_limit_bytes=VMEM_LIMIT),
        name="attn",
    )(p["sink"], x, ma, sgb, q, k, k, k, v, v, v, p["w_proj_b"], p["w_out"], p["norm_ffn_g"],
      p["w_router"], p["b_router"])


def _dest_kernel(route_ref, starts_ref, dest_ref):
    route = route_ref[...].astype(F32)
    lane = lax.broadcasted_iota(I32, route.shape, 1).astype(F32)
    starts = starts_ref[...]

    def slot(k):
        e = route[:, k:k + 1]
        start = jnp.sum(jnp.where(lane == e, starts, 0.0), axis=-1, keepdims=True)
        return start + route[:, TOP_K + k:TOP_K + k + 1]

    dest_ref[...] = jnp.where(lane == 0.0, slot(0), jnp.where(lane == 1.0, slot(1), 0.0)).astype(I32)


def _dest(route, pad_starts):
    t = route.shape[0]
    tm = TM_DEST
    return pl.pallas_call(
        _dest_kernel,
        grid=(t // tm,),
        in_specs=[pl.BlockSpec((tm, LANES), lambda i: (i, 0)), _const_spec((1, LANES))],
        out_specs=pl.BlockSpec((tm, LANES), lambda i: (i, 0)),
        out_shape=jax.ShapeDtypeStruct((t, LANES), I32),
        compiler_params=pltpu.CompilerParams(dimension_semantics=("parallel",)),
        name="dest",
    )(route, pad_starts)


def _sc_mesh():
    return plsc.VectorSubcoreMesh(core_axis_name="c", subcore_axis_name="s")


def _sc_worker():
    return lax.axis_index("s") * SC_CORES + lax.axis_index("c")


def _sc_chunk(t):
    return min(SC_CHUNK_MAX, t // (SC_WORKERS * SUBLANES))


def _dispatch(hn, dests, n_rows):
    t = hn.shape[0]
    chunk = dests[0].shape[1]
    per_worker = t // SC_WORKERS
    n_chunks = per_worker // chunk
    idx = pltpu.VMEM((n_chunks, chunk), I32)

    @functools.partial(
        pl.kernel, mesh=_sc_mesh(), out_type=jax.ShapeDtypeStruct((n_rows,) + ROW3, I32),
        scratch_types=[idx, idx, pltpu.VMEM((chunk,) + ROW3, I32), pltpu.SemaphoreType.DMA])
    def scatter_rows(hn_hbm, d0_hbm, d1_hbm, xs_hbm, i0_v, i1_v, rows_v, sem):
        w = _sc_worker()
        pltpu.sync_copy(d0_hbm.at[pl.ds(w * n_chunks, n_chunks)], i0_v)
        pltpu.sync_copy(d1_hbm.at[pl.ds(w * n_chunks, n_chunks)], i1_v)

        @pl.loop(0, n_chunks)
        def _(j):
            pltpu.sync_copy(hn_hbm.at[pl.ds(w * per_worker + j * chunk, chunk)], rows_v)
            copies = [pltpu.make_async_copy(rows_v, xs_hbm.at[i_v.at[j]], sem) for i_v in (i0_v, i1_v)]
            for cp in copies:
                cp.start()
            for cp in copies:
                cp.wait()

    return scatter_rows(hn, *dests)


def _moe_block_rows(t):
    rows = MOE_ROWS_MAX
    while rows > CHUNK and (t * TOP_K) // N_EXPERTS < MOE_MIN_BLOCKS * rows:
        rows //= 2
    return rows


def _moe_kernel(be_ref, nvalid_ref, nused_ref, xs_ref, wg_ref, wu_ref, wd_ref, yb_ref, wg_scr, wu_scr, wd_scr):
    i = pl.program_id(0)
    used = i < nused_ref[0]

    @pl.when(jnp.logical_not(used))
    def _():
        yb_ref[...] = jnp.zeros_like(yb_ref)

    @pl.when(used)
    def _():
        @pl.when(jnp.logical_or(i == 0, be_ref[i] != be_ref[jnp.maximum(i - 1, 0)]))
        def _():
            wg_scr[...] = wg_ref[0].astype(BF16)
            wu_scr[...] = wu_ref[0].astype(BF16)
            wd_scr[...] = wd_ref[0].astype(BF16)

        row = lax.broadcasted_iota(I32, (xs_ref.shape[0] // ROW_LINES, 1), 0)
        x = jnp.where(row < nvalid_ref[i], _load_rows3(xs_ref), 0.0).astype(BF16)
        gate = jnp.dot(x, wg_scr[...], preferred_element_type=F32)
        up = jnp.dot(x, wu_scr[...], preferred_element_type=F32)
        hid = (jax.nn.silu(gate) * up).astype(BF16)
        _store_rows3(yb_ref, jnp.dot(hid, wd_scr[...], preferred_element_type=F32))


def _moe(block_e, n_valid, n_used, xs, p, block_rows):
    n_blocks = xs.shape[0] // (block_rows * ROW_LINES)

    def rows(i, be, nv, nu):
        return (jnp.minimum(i, nu[0] - 1), 0)

    def expert(i, be, nv, nu):
        return (be[jnp.minimum(i, nu[0] - 1)], 0, 0)

    return pl.pallas_call(
        _moe_kernel,
        grid_spec=pltpu.PrefetchScalarGridSpec(
            num_scalar_prefetch=3,
            grid=(n_blocks,),
            in_specs=[_rows3_spec(block_rows, rows),
                      pl.BlockSpec((1, D_MODEL, D_EXPERT), expert),
                      pl.BlockSpec((1, D_MODEL, D_EXPERT), expert),
                      pl.BlockSpec((1, D_EXPERT, D_MODEL), expert)],
            out_specs=_rows3_spec(block_rows, lambda i, be, nv, nu: (i, 0)),
            scratch_shapes=[pltpu.VMEM((D_MODEL, D_EXPERT), BF16), pltpu.VMEM((D_MODEL, D_EXPERT), BF16),
                            pltpu.VMEM((D_EXPERT, D_MODEL), BF16)],
        ),
        out_shape=jax.ShapeDtypeStruct(xs.shape, I32),
        compiler_params=pltpu.CompilerParams(dimension_semantics=("arbitrary",),
                                             vmem_limit_bytes=VMEM_LIMIT),
        name="moe",
    )(block_e, n_valid, n_used, xs, p["w_gate_e"], p["w_up_e"], p["w_down_e"])


def _gather(yb, dests):
    chunk = dests[0].shape[1]
    t = dests[0].shape[0] * chunk
    per_worker = t // SC_WORKERS
    n_chunks = per_worker // chunk
    idx = pltpu.VMEM((n_chunks, chunk), I32)
    out = jax.ShapeDtypeStruct((t,) + ROW3, I32)

    @functools.partial(
        pl.kernel, mesh=_sc_mesh(), out_type=(out, out),
        scratch_types=[idx, idx, pltpu.VMEM((chunk,) + ROW3, I32), pltpu.SemaphoreType.DMA])
    def gather_rows(yb_hbm, d0_hbm, d1_hbm, y0_hbm, y1_hbm, i0_v, i1_v, rows_v, sem):
        w = _sc_worker()
        pltpu.sync_copy(d0_hbm.at[pl.ds(w * n_chunks, n_chunks)], i0_v)
        pltpu.sync_copy(d1_hbm.at[pl.ds(w * n_chunks, n_chunks)], i1_v)

        @pl.loop(0, n_chunks)
        def _(j):
            rows = pl.ds(w * per_worker + j * chunk, chunk)
            for i_v, y_hbm in ((i0_v, y0_hbm), (i1_v, y1_hbm)):
                pltpu.async_copy(yb_hbm.at[i_v.at[j]], rows_v, sem).wait()
                pltpu.sync_copy(rows_v, y_hbm.at[rows])

    return gather_rows(yb, *dests)


def _combine_kernel(x1_ref, gate_ref, y0_ref, y1_ref, out_ref):
    gate = gate_ref[...]
    out_ref[...] = x1_ref[...] + (_load_rows3(y0_ref) * gate[:, 0:1] + _load_rows3(y1_ref) * gate[:, 1:2])


def _combine(x1, gate, y0, y1):
    t = x1.shape[0]
    tm = TM_ROW
    return pl.pallas_call(
        _combine_kernel,
        grid=(t // tm,),
        in_specs=[pl.BlockSpec((tm, D_MODEL), lambda i: (i, 0)),
                  pl.BlockSpec((tm, LANES), lambda i: (i, 0)),
                  _rows3_spec(tm, lambda i: (i, 0)), _rows3_spec(tm, lambda i: (i, 0))],
        out_specs=pl.BlockSpec((tm, D_MODEL), lambda i: (i, 0)),
        out_shape=jax.ShapeDtypeStruct((t, D_MODEL), F32),
        compiler_params=pltpu.CompilerParams(dimension_semantics=("parallel",)),
        name="combine",
    )(x1, gate, y0, y1)


def _rope_tables(seq):
    half = HEAD_DIM // 2
    inv_freq = ROPE_THETA ** (-jnp.arange(half, dtype=F32) / half)
    ang = jnp.arange(seq).astype(F32)[:, None] * inv_freq[None, :]
    cos = jnp.cos(ang)
    sin = jnp.sin(ang)
    return jnp.concatenate([cos, cos], axis=-1), jnp.concatenate([-sin, sin], axis=-1)


def _layer(x, p, seq):
    t = x.shape[0]
    cos, sin = _rope_tables(seq)
    ma, sgb, q, k, v = _inproj(x, p, cos, sin, seq)
    x1, hn, route, gate, counts_f = _attn(x, ma, sgb, q, k, v, p, seq)

    block_rows = _moe_block_rows(t)
    counts = counts_f[0, :N_EXPERTS].astype(I32)
    padded = (counts + block_rows - 1) // block_rows * block_rows
    pad_ends = jnp.cumsum(padded)
    pad_starts = pad_ends - padded
    n_blocks = (t * TOP_K) // block_rows + N_EXPERTS
    block_start = jnp.arange(n_blocks, dtype=I32) * block_rows
    in_expert = jnp.logical_and(block_start[:, None] >= pad_starts[None, :],
                                block_start[:, None] < pad_ends[None, :]).astype(I32)
    block_e = jnp.minimum(jnp.sum((block_start[:, None] >= pad_ends[None, :]).astype(I32), axis=1), N_EXPERTS - 1)
    n_valid = jnp.sum(in_expert * jnp.clip(pad_starts + counts - block_start[:, None], 0, block_rows), axis=1)
    n_used = pad_ends[-1:] // block_rows
    starts_row = jnp.zeros((1, LANES), F32).at[0, :N_EXPERTS].set(pad_starts.astype(F32))

    dest = _dest(route, starts_row)
    dests = [dest[:, k].reshape(t // _sc_chunk(t), _sc_chunk(t)) for k in range(TOP_K)]
    xs = _dispatch(_as_rows3(hn), dests, n_blocks * block_rows)
    yb = _moe(block_e, n_valid, n_used, _as_lines(xs), p, block_rows)
    y0, y1 = _gather(_as_rows3(yb), dests)
    return _combine(x1, gate, _as_lines(y0), _as_lines(y1))


def kernel(x_prompt, x_sample, norm_mix_g, w_in, norm_v_g, w_spatial, b_spatial, q_norm_g, k_norm_g, sink,
           w_proj_a, w_proj_b, w_out, norm_ffn_g, w_router_group, b_router_group, w_router_expert,
           b_router_expert, w_gate_e, w_up_e, w_down_e):
    depth = w_in.shape[0]
    layers = []
    for l in range(depth):
        w_router = jnp.zeros((D_MODEL, LANES), F32)
        w_router = w_router.at[:, :N_GROUPS].set(w_router_group[l])
        w_router = w_router.at[:, N_GROUPS:N_GROUPS + N_EXPERTS].set(w_router_expert[l])
        b_router = jnp.zeros((1, LANES), F32)
        b_router = b_router.at[0, :N_GROUPS].set(b_router_group[l])
        b_router = b_router.at[0, N_GROUPS:N_GROUPS + N_EXPERTS].set(b_router_expert[l])
        layers.append(dict(
            norm_mix_g=norm_mix_g[l][None], w_in=w_in[l].astype(BF16), norm_v_g=norm_v_g[l][None],
            w_spatial=w_spatial[l].astype(BF16),
            b_spatial=jnp.broadcast_to(b_spatial[l][:, :, None], (A_GROUPS, CHUNK, LANES)),
            q_norm_g=q_norm_g[l][None], k_norm_g=k_norm_g[l][None], sink=sink[l],
            w_proj_a=w_proj_a[l].astype(BF16), w_proj_b=w_proj_b[l].astype(BF16), w_out=w_out[l].astype(BF16),
            norm_ffn_g=norm_ffn_g[l][None], w_router=w_router.astype(BF16), b_router=b_router,
            w_gate_e=w_gate_e[l], w_up_e=w_up_e[l], w_down_e=w_down_e[l]))

    def trunk(x):
        b, s, d = x.shape
        y = x.reshape(b * s, d)
        for p in layers:
            y = _layer(y, p, s)
        return y.reshape(b, s, d)

    return trunk(x_prompt), trunk(x_sample)
```

```python
import functools

import jax
import jax.numpy as jnp
from jax import lax
from jax.experimental import pallas as pl
from jax.experimental.pallas import tpu as pltpu
from jax.experimental.pallas import tpu_sc as plsc

F32 = jnp.float32
BF16 = jnp.bfloat16
I32 = jnp.int32
U32 = jnp.uint32

LANES = 128
SUBLANES = 8
VMEM_BYTES_V7X = 64 * 1024 * 1024
SC_CORES = 2
SC_SUBCORES = 16
SC_WORKERS = SC_CORES * SC_SUBCORES
SC_CHUNK_MAX = 128

D_MODEL = 1024
A_WIDTH = D_MODEL
A_GROUPS = 8
CHUNK = 128
HEAD_DIM = 128
N_Q_HEADS = D_MODEL // HEAD_DIM
N_KV_HEADS = 2
REP = N_Q_HEADS // N_KV_HEADS
WINDOW = 128
ROPE_THETA = 10000.0
Q_W = N_Q_HEADS * HEAD_DIM
KV_W = N_KV_HEADS * HEAD_DIM
IN_W = 2 * A_WIDTH + Q_W + 2 * KV_W + 2 * D_MODEL
COL_U = 0
COL_V = COL_U + A_WIDTH
COL_Q = COL_V + A_WIDTH
COL_K = COL_Q + Q_W
COL_VA = COL_K + KV_W
COL_GA = COL_VA + KV_W
COL_GB = COL_GA + D_MODEL
N_GROUPS = 4
EXPERTS_PER_GROUP = 8
N_EXPERTS = N_GROUPS * EXPERTS_PER_GROUP
TOP_K = 2
D_EXPERT = 512
EPS = 1e-6
NEG = -1e30

TM_IN = 256
TQ = 256
DENSE_COLS = 256
ROUTER_ROWS = 64
assert EXPERTS_PER_GROUP == SUBLANES and SUBLANES + N_EXPERTS <= ROUTER_ROWS
TM_ROW = 512
TM_DEST = 2048
MOE_ROWS_MAX = 512
MOE_MIN_BLOCKS = 4
VMEM_LIMIT = 56 * 1024 * 1024
assert VMEM_LIMIT < VMEM_BYTES_V7X


def _rms(x, g):
    return x * lax.rsqrt(jnp.mean(x * x, axis=-1, keepdims=True) + EPS) * g


ROW_LINES = D_MODEL // 2 // LANES
ROW3 = (ROW_LINES, LANES)
HIGH_HALF = 0xFFFF0000


def _as_rows3(a):
    return a.reshape((a.shape[0] // ROW_LINES,) + ROW3)


def _as_lines(a):
    return a.reshape((a.shape[0] * ROW_LINES, LANES))


def _store_rows3(lines_ref, val):
    rows = val.shape[0]
    bits = lax.bitcast_convert_type(val.astype(BF16).astype(F32), U32)
    half = ROW_LINES * LANES
    for s in range(ROW_LINES):
        lo = bits[:, s * LANES:(s + 1) * LANES] >> 16
        hi = bits[:, half + s * LANES:half + (s + 1) * LANES] & U32(HIGH_HALF)
        lines_ref[pl.ds(s, rows, stride=ROW_LINES), :] = lax.bitcast_convert_type(lo | hi, I32)


def _load_rows3(lines_ref):
    rows = lines_ref.shape[0] // ROW_LINES
    words = [lax.bitcast_convert_type(lines_ref[pl.ds(s, rows, stride=ROW_LINES), :], U32)
             for s in range(ROW_LINES)]
    lo = [lax.bitcast_convert_type(w << 16, F32) for w in words]
    hi = [lax.bitcast_convert_type(w & U32(HIGH_HALF), F32) for w in words]
    return jnp.concatenate(lo + hi, axis=1)


def _rows3_spec(rows, index_map):
    return pl.BlockSpec((rows * ROW_LINES, LANES), index_map)


def _const_spec(shape):
    nd = len(shape)
    return pl.BlockSpec(shape, lambda *_: (0,) * nd, pipeline_mode=pl.Buffered(1))


def _inproj_kernel(x_ref, gmix_ref, win_ref, gv_ref, ws_ref, bs_ref, gq_ref, gk_ref, cos_ref, sin_ref,
                   wpa_ref, ma_ref, sgb_ref, q_ref, k_ref, v_ref, u_scr, vn_scr, a_scr):
    tm = x_ref.shape[0]
    h = _rms(x_ref[...], gmix_ref[...]).astype(BF16)

    def proj(lo, width):
        return jnp.dot(h, win_ref[:, lo:lo + width], preferred_element_type=F32)

    vn_scr[...] = _rms(jax.nn.gelu(proj(COL_V, A_WIDTH)), gv_ref[...]).astype(BF16)
    u_scr[...] = jax.nn.gelu(proj(COL_U, A_WIDTH))
    for c in range(tm // CHUNK):
        rows = slice(c * CHUNK, (c + 1) * CHUNK)
        for g in range(A_GROUPS):
            cols = slice(g * LANES, (g + 1) * LANES)
            mixed = jnp.dot(ws_ref[g], vn_scr[rows, cols], preferred_element_type=F32) + bs_ref[g]
            a_scr[rows, cols] = (u_scr[rows, cols] * mixed).astype(BF16)
    ya = jnp.dot(a_scr[...], wpa_ref[...], preferred_element_type=F32)
    ma_ref[...] = jax.nn.sigmoid(proj(COL_GA, D_MODEL)) * ya
    sgb_ref[...] = jax.nn.sigmoid(proj(COL_GB, D_MODEL))

    cos = cos_ref[...]
    sin = sin_ref[...]

    def norm_rope(z, g):
        zn = _rms(z, g)
        return zn * cos + pltpu.roll(zn, HEAD_DIM // 2, 1) * sin

    qz = proj(COL_Q, Q_W)
    for hd in range(N_Q_HEADS):
        cols = slice(hd * HEAD_DIM, (hd + 1) * HEAD_DIM)
        q_ref[:, cols] = norm_rope(qz[:, cols], gq_ref[...]).astype(BF16)
    kz = proj(COL_K, KV_W)
    for hd in range(N_KV_HEADS):
        cols = slice(hd * HEAD_DIM, (hd + 1) * HEAD_DIM)
        k_ref[:, cols] = norm_rope(kz[:, cols], gk_ref[...]).astype(BF16)
    v_ref[...] = proj(COL_VA, KV_W).astype(BF16)


def _inproj(x, p, cos, sin, seq):
    t = x.shape[0]
    tm = TM_IN
    n_pos = seq // tm
    row = lambda w: pl.BlockSpec((tm, w), lambda i: (i, 0))
    pos = pl.BlockSpec((tm, HEAD_DIM), lambda i: (i % n_pos, 0))
    return pl.pallas_call(
        _inproj_kernel,
        grid=(t // tm,),
        in_specs=[row(D_MODEL), _const_spec((1, D_MODEL)), _const_spec((D_MODEL, IN_W)),
                  _const_spec((1, A_WIDTH)), _const_spec((A_GROUPS, CHUNK, CHUNK)),
                  _const_spec((A_GROUPS, CHUNK, LANES)), _const_spec((1, HEAD_DIM)),
                  _const_spec((1, HEAD_DIM)), pos, pos, _const_spec((A_WIDTH, D_MODEL))],
        out_specs=[row(D_MODEL), row(D_MODEL), row(Q_W), row(KV_W), row(KV_W)],
        out_shape=[jax.ShapeDtypeStruct((t, D_MODEL), F32), jax.ShapeDtypeStruct((t, D_MODEL), F32),
                   jax.ShapeDtypeStruct((t, Q_W), BF16), jax.ShapeDtypeStruct((t, KV_W), BF16),
                   jax.ShapeDtypeStruct((t, KV_W), BF16)],
        scratch_shapes=[pltpu.VMEM((tm, A_WIDTH), F32), pltpu.VMEM((tm, A_WIDTH), BF16),
                        pltpu.VMEM((tm, A_WIDTH), BF16)],
        compiler_params=pltpu.CompilerParams(dimension_semantics=("parallel",),
                                             vmem_limit_bytes=VMEM_LIMIT),
        name="inproj",
    )(x, p["norm_mix_g"], p["w_in"], p["norm_v_g"], p["w_spatial"], p["b_spatial"], p["q_norm_g"],
      p["k_norm_g"], cos, sin, p["w_proj_a"])


def _attn_kernel(sink_ref, x_ref, ma_ref, sgb_ref, q_ref, kp_ref, kc_ref, kn_ref, vp_ref, vc_ref, vn_ref,
                 wpb_ref, wout_ref, gffn_ref, wr_ref, br_ref,
                 x1_ref, hn_ref, route_ref, gate_ref, counts_ref, kcat, vcat, o_scr, s_scr, p_scr, m_scr,
                 *, tiles_per_seq, n_tiles):
    tq = x_ref.shape[0]
    blk = WINDOW
    i = pl.program_id(0)
    slot = i % 2

    @pl.when(i == 0)
    def _():
        o_scr[...] = jnp.zeros_like(o_scr)
        counts_ref[...] = jnp.zeros_like(counts_ref)

    pos_tile = jnp.minimum(i, n_tiles - 1) % tiles_per_seq
    has_prev = pos_tile > 0
    has_next = pos_tile < tiles_per_seq - 1

    kcat[0:blk] = kp_ref[...]
    kcat[blk:blk + tq] = kc_ref[...]
    kcat[blk + tq:] = kn_ref[...]
    vcat[0:blk] = vp_ref[...]
    vcat[blk:blk + tq] = vc_ref[...]
    vcat[blk + tq:] = vn_ref[...]

    qi = lax.broadcasted_iota(I32, (blk, 3 * blk), 0)
    kj = lax.broadcasted_iota(I32, (blk, 3 * blk), 1)
    band = jnp.abs(kj - blk - qi) <= WINDOW
    scale = HEAD_DIM ** -0.5
    n_sub = tq // blk
    pairs = [(j, g) for j in range(n_sub) for g in range(N_KV_HEADS)]

    def keys(ref, j, g):
        return ref[j * blk:(j + 3) * blk, g * HEAD_DIM:(g + 1) * HEAD_DIM]

    def head_cols(g, r):
        hd = g * REP + r
        return slice(hd * HEAD_DIM, (hd + 1) * HEAD_DIM)

    for b, (j, g) in enumerate(pairs):
        rows = slice(j * blk, (j + 1) * blk)
        qs = jnp.concatenate([q_ref[rows, head_cols(g, r)] for r in range(REP)], axis=0)
        s_scr[b] = lax.dot_general(qs, keys(kcat, j, g), (((1,), (1,)), ((), ())),
                                   preferred_element_type=F32)
    def softmax(b, r):
        j, g = pairs[b]
        valid = band
        if j == 0:
            valid = jnp.logical_and(valid, kj >= jnp.where(has_prev, 0, blk))
        if j == n_sub - 1:
            valid = jnp.logical_and(valid, kj < jnp.where(has_next, 3 * blk, 2 * blk))
        hrows = slice(r * blk, (r + 1) * blk)
        s = jnp.where(valid, s_scr[b, hrows, :] * scale, NEG)
        sink = sink_ref[g * REP + r]
        m = jnp.maximum(jnp.max(s, axis=-1, keepdims=True), sink)
        e = jnp.exp(s - m)
        denom = jnp.sum(e, axis=-1, keepdims=True) + jnp.exp(sink - m)
        p_scr[b, hrows, :] = (e / denom).astype(BF16)

    def values(b):
        j, g = pairs[b]
        rows = slice(j * blk, (j + 1) * blk)
        o = jnp.dot(p_scr[b], keys(vcat, j, g), preferred_element_type=F32).astype(BF16)
        for r in range(REP):
            o_scr[slot, rows, head_cols(g, r)] = o[r * blk:(r + 1) * blk, :]

    def merged_cols(cols):
        yb = jnp.dot(o_scr[1 - slot], wpb_ref[:, cols], preferred_element_type=F32)
        m_scr[:, cols] = (ma_ref[:, cols] + sgb_ref[:, cols] * yb).astype(BF16)

    def x1_cols(cols):
        x1_ref[:, cols] = x_ref[:, cols] + jnp.dot(m_scr[...], wout_ref[:, cols], preferred_element_type=F32)

    col_chunks = [slice(c * DENSE_COLS, (c + 1) * DENSE_COLS) for c in range(D_MODEL // DENSE_COLS)]
    dense = [functools.partial(f, cols) for f in (merged_cols, x1_cols) for cols in col_chunks]
    units = [(b, r) for b in range(len(pairs)) for r in range(REP)]
    units_per_dense = len(units) // len(dense)
    for n, (b, r) in enumerate(units):
        softmax(b, r)
        if (n + 1) % units_per_dense == 0:
            dense[(n + 1) // units_per_dense - 1]()
        if r == REP - 1:
            values(b)

    hn = _rms(x1_ref[...], gffn_ref[...])
    _store_rows3(hn_ref, hn)
    def wide(a):
        return jnp.concatenate([a] * (tq // LANES), axis=1)

    logits = lax.dot_general(wr_ref[...], hn.astype(BF16), (((1,), (1,)), ((), ())),
                             preferred_element_type=F32) + wide(br_ref[...])
    sub = lax.broadcasted_iota(I32, (SUBLANES, tq), 0).astype(F32)
    ninf = -jnp.inf

    def cmax(a):
        return jnp.max(a, axis=0, keepdims=True)

    def csum(a):
        return jnp.sum(a, axis=0, keepdims=True)

    def first_row(mask):
        return jnp.min(jnp.where(mask, sub, float(SUBLANES)), axis=0, keepdims=True)

    def group_rows(g):
        return logits[(g + 1) * SUBLANES:(g + 2) * SUBLANES]

    gl = jnp.where(sub < N_GROUPS, logits[0:SUBLANES], ninf)
    gmax = cmax(gl)
    g_sel = first_row(gl == gmax)
    g_p = 1.0 / csum(jnp.exp(gl - gmax))
    el = group_rows(0)
    for g in range(1, N_GROUPS):
        el = jnp.where(g_sel == g, group_rows(g), el)
    ee = jnp.exp(el - cmax(el))
    eprob = ee / csum(ee)
    p1 = cmax(eprob)
    i1 = first_row(eprob == p1)
    eprob2 = jnp.where(sub == i1, -1.0, eprob)
    p2 = cmax(eprob2)
    i2 = first_row(eprob2 == p2)
    psum = p1 + p2
    w1 = g_p * p1 / psum
    w2 = g_p * p2 / psum
    e1 = g_sel * EXPERTS_PER_GROUP + i1
    e2 = g_sel * EXPERTS_PER_GROUP + i2

    erow = lax.broadcasted_iota(I32, (N_EXPERTS, tq), 0).astype(F32)
    oh1 = erow == e1
    oh2 = erow == e2
    cnt = (jnp.where(oh1, 1.0, 0.0) + jnp.where(oh2, 1.0, 0.0)) * jnp.where(i > 0, 1.0, 0.0)
    ri = lax.broadcasted_iota(I32, (tq, tq), 0)
    ci = lax.broadcasted_iota(I32, (tq, tq), 1)
    earlier = jnp.where(ri < ci, 1.0, 0.0).astype(BF16)
    base = wide(counts_ref[...]) + jnp.dot(cnt.astype(BF16), earlier, preferred_element_type=F32)
    r1 = csum(jnp.where(oh1, base, 0.0))
    r2 = csum(jnp.where(oh2, base, 0.0))
    counts_ref[...] = counts_ref[...] + jnp.sum(cnt, axis=1, keepdims=True)

    route = jnp.where(sub == 0.0, e1, jnp.where(sub == 1.0, e2,
                      jnp.where(sub == 2.0, r1, jnp.where(sub == 3.0, r2, 0.0))))
    route_ref[...] = route.astype(I32)
    gates = jnp.where(sub == 0.0, w1, jnp.where(sub == 1.0, w2, 0.0))
    gate_ref[...] = jnp.concatenate([gates, jnp.zeros((LANES - SUBLANES, tq), F32)], axis=0).T


def _attn(x, ma, sgb, q, k, v, p, seq):
    t = x.shape[0]
    tq = TQ
    sub = tq // WINDOW
    last_blk = t // WINDOW - 1
    n_tiles = t // tq
    att = lambda i: jnp.minimum(i, n_tiles - 1)
    post = lambda i: jnp.maximum(i - 1, 0)
    att_row = lambda w: pl.BlockSpec((tq, w), lambda i: (att(i), 0))
    row = lambda w: pl.BlockSpec((tq, w), lambda i: (post(i), 0))
    prev = pl.BlockSpec((WINDOW, KV_W), lambda i: (jnp.maximum(att(i) * sub - 1, 0), 0))
    nxt = pl.BlockSpec((WINDOW, KV_W), lambda i: (jnp.minimum((att(i) + 1) * sub, last_blk), 0))
    return pl.pallas_call(
        functools.partial(_attn_kernel, tiles_per_seq=seq // tq, n_tiles=n_tiles),
        grid=(n_tiles + 1,),
        in_specs=[pl.BlockSpec(memory_space=pltpu.SMEM),
                  row(D_MODEL), row(D_MODEL), row(D_MODEL), att_row(Q_W),
                  prev, att_row(KV_W), nxt, prev, att_row(KV_W), nxt,
                  _const_spec((Q_W, D_MODEL)), _const_spec((D_MODEL, D_MODEL)), _const_spec((1, D_MODEL)),
                  _const_spec((ROUTER_ROWS, D_MODEL)), _const_spec((ROUTER_ROWS, LANES))],
        out_specs=[row(D_MODEL), _rows3_spec(tq, lambda i: (post(i), 0)),
                   pl.BlockSpec((SUBLANES, tq), lambda i: (0, post(i))), row(LANES),
                   pl.BlockSpec((N_EXPERTS, LANES), lambda i: (0, 0))],
        out_shape=[jax.ShapeDtypeStruct((t, D_MODEL), F32), jax.ShapeDtypeStruct((t * ROW_LINES, LANES), I32),
                   jax.ShapeDtypeStruct((SUBLANES, t), I32), jax.ShapeDtypeStruct((t, LANES), F32),
                   jax.ShapeDtypeStruct((N_EXPERTS, LANES), F32)],
        scratch_shapes=[pltpu.VMEM((tq + 2 * WINDOW, KV_W), BF16), pltpu.VMEM((tq + 2 * WINDOW, KV_W), BF16),
                        pltpu.VMEM((2, tq, Q_W), BF16),
                        pltpu.VMEM((sub * N_KV_HEADS, REP * WINDOW, 3 * WINDOW), F32),
                        pltpu.VMEM((sub * N_KV_HEADS, REP * WINDOW, 3 * WINDOW), BF16),
                        pltpu.VMEM((tq, D_MODEL), BF16)],
        compiler_params=pltpu.CompilerParams(dimension_semantics=("arbitrary",),
                                             vmem_limit_bytes=VMEM_LIMIT),
        name="attn",
    )(p["sink"], x, ma, sgb, q, k, k, k, v, v, v, p["w_proj_b"], p["w_out"], p["norm_ffn_g"],
      p["w_router"], p["b_router"])


def _dest_kernel(route_ref, starts_ref, dest_ref):
    route = route_ref[...].astype(F32)
    td = route.shape[1]
    sub = lax.broadcasted_iota(I32, route.shape, 0)
    erow = lax.broadcasted_iota(I32, (N_EXPERTS, td), 0).astype(F32)
    starts = jnp.broadcast_to(starts_ref[:, 0:1], (N_EXPERTS, td))

    def slot(k):
        start = jnp.sum(jnp.where(erow == route[k:k + 1], starts, 0.0), axis=0, keepdims=True)
        return start + route[TOP_K + k:TOP_K + k + 1]

    dest_ref[...] = jnp.where(sub == 0, slot(0), jnp.where(sub == 1, slot(1), 0.0)).astype(I32)


def _dest(route, pad_starts):
    t = route.shape[1]
    td = min(TM_DEST, t)
    return pl.pallas_call(
        _dest_kernel,
        grid=(t // td,),
        in_specs=[pl.BlockSpec((SUBLANES, td), lambda i: (0, i)), _const_spec((N_EXPERTS, LANES))],
        out_specs=pl.BlockSpec((SUBLANES, td), lambda i: (0, i)),
        out_shape=jax.ShapeDtypeStruct((SUBLANES, t), I32),
        compiler_params=pltpu.CompilerParams(dimension_semantics=("parallel",)),
        name="dest",
    )(route, pad_starts)


def _sc_mesh():
    return plsc.VectorSubcoreMesh(core_axis_name="c", subcore_axis_name="s")


def _sc_worker():
    return lax.axis_index("s") * SC_CORES + lax.axis_index("c")


def _sc_chunk(t):
    return min(SC_CHUNK_MAX, t // (SC_WORKERS * SUBLANES))


def _dispatch(hn, dests, n_rows):
    t = hn.shape[0]
    chunk = dests[0].shape[1]
    per_worker = t // SC_WORKERS
    n_chunks = per_worker // chunk
    idx = pltpu.VMEM((n_chunks, chunk), I32)

    @functools.partial(
        pl.kernel, mesh=_sc_mesh(), out_type=jax.ShapeDtypeStruct((n_rows,) + ROW3, I32),
        scratch_types=[idx, idx, pltpu.VMEM((chunk,) + ROW3, I32), pltpu.SemaphoreType.DMA])
    def scatter_rows(hn_hbm, d0_hbm, d1_hbm, xs_hbm, i0_v, i1_v, rows_v, sem):
        w = _sc_worker()
        pltpu.sync_copy(d0_hbm.at[pl.ds(w * n_chunks, n_chunks)], i0_v)
        pltpu.sync_copy(d1_hbm.at[pl.ds(w * n_chunks, n_chunks)], i1_v)

        @pl.loop(0, n_chunks)
        def _(j):
            pltpu.sync_copy(hn_hbm.at[pl.ds(w * per_worker + j * chunk, chunk)], rows_v)
            copies = [pltpu.make_async_copy(rows_v, xs_hbm.at[i_v.at[j]], sem) for i_v in (i0_v, i1_v)]
            for cp in copies:
                cp.start()
            for cp in copies:
                cp.wait()

    return scatter_rows(hn, *dests)


def _moe_block_rows(t):
    rows = MOE_ROWS_MAX
    while rows > CHUNK and (t * TOP_K) // N_EXPERTS < MOE_MIN_BLOCKS * rows:
        rows //= 2
    return rows


def _moe_kernel(be_ref, nvalid_ref, nused_ref, xs_ref, wg_ref, wu_ref, wd_ref, yb_ref, wg_scr, wu_scr, wd_scr):
    i = pl.program_id(0)
    used = i < nused_ref[0]

    @pl.when(jnp.logical_not(used))
    def _():
        yb_ref[...] = jnp.zeros_like(yb_ref)

    @pl.when(used)
    def _():
        @pl.when(jnp.logical_or(i == 0, be_ref[i] != be_ref[jnp.maximum(i - 1, 0)]))
        def _():
            wg_scr[...] = wg_ref[0].astype(BF16)
            wu_scr[...] = wu_ref[0].astype(BF16)
            wd_scr[...] = wd_ref[0].astype(BF16)

        row = lax.broadcasted_iota(I32, (xs_ref.shape[0] // ROW_LINES, 1), 0)
        x = jnp.where(row < nvalid_ref[i], _load_rows3(xs_ref), 0.0).astype(BF16)
        gate = jnp.dot(x, wg_scr[...], preferred_element_type=F32)
        up = jnp.dot(x, wu_scr[...], preferred_element_type=F32)
        hid = (jax.nn.silu(gate) * up).astype(BF16)
        _store_rows3(yb_ref, jnp.dot(hid, wd_scr[...], preferred_element_type=F32))


def _moe(block_e, n_valid, n_used, xs, p, block_rows):
    n_blocks = xs.shape[0] // (block_rows * ROW_LINES)

    def rows(i, be, nv, nu):
        return (jnp.minimum(i, nu[0] - 1), 0)

    def expert(i, be, nv, nu):
        return (be[jnp.minimum(i, nu[0] - 1)], 0, 0)

    return pl.pallas_call(
        _moe_kernel,
        grid_spec=pltpu.PrefetchScalarGridSpec(
            num_scalar_prefetch=3,
            grid=(n_blocks,),
            in_specs=[_rows3_spec(block_rows, rows),
                      pl.BlockSpec((1, D_MODEL, D_EXPERT), expert),
                      pl.BlockSpec((1, D_MODEL, D_EXPERT), expert),
                      pl.BlockSpec((1, D_EXPERT, D_MODEL), expert)],
            out_specs=_rows3_spec(block_rows, lambda i, be, nv, nu: (i, 0)),
            scratch_shapes=[pltpu.VMEM((D_MODEL, D_EXPERT), BF16), pltpu.VMEM((D_MODEL, D_EXPERT), BF16),
                            pltpu.VMEM((D_EXPERT, D_MODEL), BF16)],
        ),
        out_shape=jax.ShapeDtypeStruct(xs.shape, I32),
        compiler_params=pltpu.CompilerParams(dimension_semantics=("arbitrary",),
                                             vmem_limit_bytes=VMEM_LIMIT),
        name="moe",
    )(block_e, n_valid, n_used, xs, p["w_gate_e"], p["w_up_e"], p["w_down_e"])


def _gather(yb, dests):
    chunk = dests[0].shape[1]
    t = dests[0].shape[0] * chunk
    per_worker = t // SC_WORKERS
    n_chunks = per_worker // chunk
    idx = pltpu.VMEM((n_chunks, chunk), I32)
    out = jax.ShapeDtypeStruct((t,) + ROW3, I32)

    @functools.partial(
        pl.kernel, mesh=_sc_mesh(), out_type=(out, out),
        scratch_types=[idx, idx, pltpu.VMEM((chunk,) + ROW3, I32), pltpu.SemaphoreType.DMA])
    def gather_rows(yb_hbm, d0_hbm, d1_hbm, y0_hbm, y1_hbm, i0_v, i1_v, rows_v, sem):
        w = _sc_worker()
        pltpu.sync_copy(d0_hbm.at[pl.ds(w * n_chunks, n_chunks)], i0_v)
        pltpu.sync_copy(d1_hbm.at[pl.ds(w * n_chunks, n_chunks)], i1_v)

        @pl.loop(0, n_chunks)
        def _(j):
            rows = pl.ds(w * per_worker + j * chunk, chunk)
            for i_v, y_hbm in ((i0_v, y0_hbm), (i1_v, y1_hbm)):
                pltpu.async_copy(yb_hbm.at[i_v.at[j]], rows_v, sem).wait()
                pltpu.sync_copy(rows_v, y_hbm.at[rows])

    return gather_rows(yb, *dests)


def _combine_kernel(x1_ref, gate_ref, y0_ref, y1_ref, out_ref):
    gate = gate_ref[...]
    out_ref[...] = x1_ref[...] + (_load_rows3(y0_ref) * gate[:, 0:1] + _load_rows3(y1_ref) * gate[:, 1:2])


def _combine(x1, gate, y0, y1):
    t = x1.shape[0]
    tm = TM_ROW
    return pl.pallas_call(
        _combine_kernel,
        grid=(t // tm,),
        in_specs=[pl.BlockSpec((tm, D_MODEL), lambda i: (i, 0)),
                  pl.BlockSpec((tm, LANES), lambda i: (i, 0)),
                  _rows3_spec(tm, lambda i: (i, 0)), _rows3_spec(tm, lambda i: (i, 0))],
        out_specs=pl.BlockSpec((tm, D_MODEL), lambda i: (i, 0)),
        out_shape=jax.ShapeDtypeStruct((t, D_MODEL), F32),
        compiler_params=pltpu.CompilerParams(dimension_semantics=("parallel",)),
        name="combine",
    )(x1, gate, y0, y1)


def _rope_tables(seq):
    half = HEAD_DIM // 2
    inv_freq = ROPE_THETA ** (-jnp.arange(half, dtype=F32) / half)
    ang = jnp.arange(seq).astype(F32)[:, None] * inv_freq[None, :]
    cos = jnp.cos(ang)
    sin = jnp.sin(ang)
    return jnp.concatenate([cos, cos], axis=-1), jnp.concatenate([-sin, sin], axis=-1)


def _layer(x, p, seq):
    t = x.shape[0]
    cos, sin = _rope_tables(seq)
    ma, sgb, q, k, v = _inproj(x, p, cos, sin, seq)
    x1, hn, route, gate, counts_f = _attn(x, ma, sgb, q, k, v, p, seq)

    block_rows = _moe_block_rows(t)
    counts = counts_f[:, 0].astype(I32)
    padded = (counts + block_rows - 1) // block_rows * block_rows
    pad_ends = jnp.cumsum(padded)
    pad_starts = pad_ends - padded
    n_blocks = (t * TOP_K) // block_rows + N_EXPERTS
    block_start = jnp.arange(n_blocks, dtype=I32) * block_rows
    in_expert = jnp.logical_and(block_start[:, None] >= pad_starts[None, :],
                                block_start[:, None] < pad_ends[None, :]).astype(I32)
    block_e = jnp.minimum(jnp.sum((block_start[:, None] >= pad_ends[None, :]).astype(I32), axis=1), N_EXPERTS - 1)
    n_valid = jnp.sum(in_expert * jnp.clip(pad_starts + counts - block_start[:, None], 0, block_rows), axis=1)
    n_used = pad_ends[-1:] // block_rows
    starts_col = jnp.broadcast_to(pad_starts.astype(F32)[:, None], (N_EXPERTS, LANES))

    dest = _dest(route, starts_col)
    dests = [dest[k].reshape(t // _sc_chunk(t), _sc_chunk(t)) for k in range(TOP_K)]
    xs = _dispatch(_as_rows3(hn), dests, n_blocks * block_rows)
    yb = _moe(block_e, n_valid, n_used, _as_lines(xs), p, block_rows)
    y0, y1 = _gather(_as_rows3(yb), dests)
    return _combine(x1, gate, _as_lines(y0), _as_lines(y1))


def kernel(x_prompt, x_sample, norm_mix_g, w_in, norm_v_g, w_spatial, b_spatial, q_norm_g, k_norm_g, sink,
           w_proj_a, w_proj_b, w_out, norm_ffn_g, w_router_group, b_router_group, w_router_expert,
           b_router_expert, w_gate_e, w_up_e, w_down_e):
    depth = w_in.shape[0]
    layers = []
    for l in range(depth):
        w_router = jnp.zeros((ROUTER_ROWS, D_MODEL), F32)
        w_router = w_router.at[:N_GROUPS].set(w_router_group[l].T)
        w_router = w_router.at[SUBLANES:SUBLANES + N_EXPERTS].set(w_router_expert[l].T)
        b_router = jnp.zeros((ROUTER_ROWS,), F32)
        b_router = b_router.at[:N_GROUPS].set(b_router_group[l])
        b_router = b_router.at[SUBLANES:SUBLANES + N_EXPERTS].set(b_router_expert[l])
        b_router = jnp.broadcast_to(b_router[:, None], (ROUTER_ROWS, LANES))
        layers.append(dict(
            norm_mix_g=norm_mix_g[l][None], w_in=w_in[l].astype(BF16), norm_v_g=norm_v_g[l][None],
            w_spatial=w_spatial[l].astype(BF16),
            b_spatial=jnp.broadcast_to(b_spatial[l][:, :, None], (A_GROUPS, CHUNK, LANES)),
            q_norm_g=q_norm_g[l][None], k_norm_g=k_norm_g[l][None], sink=sink[l],
            w_proj_a=w_proj_a[l].astype(BF16), w_proj_b=w_proj_b[l].astype(BF16), w_out=w_out[l].astype(BF16),
            norm_ffn_g=norm_ffn_g[l][None], w_router=w_router.astype(BF16), b_router=b_router,
            w_gate_e=w_gate_e[l], w_up_e=w_up_e[l], w_down_e=w_down_e[l]))

    def trunk(x):
        b, s, d = x.shape
        y = x.reshape(b * s, d)
        for p in layers:
            y = _layer(y, p, s)
        return y.reshape(b, s, d)

    return trunk(x_prompt), trunk(x_sample)
```

```python
import functools

import jax
import jax.numpy as jnp
from jax import lax
from jax.experimental import pallas as pl
from jax.experimental.pallas import tpu as pltpu
from jax.experimental.pallas import tpu_sc as plsc

F32 = jnp.float32
BF16 = jnp.bfloat16
I32 = jnp.int32
U32 = jnp.uint32

LANES = 128
SUBLANES = 8
VMEM_BYTES_V7X = 64 * 1024 * 1024
SC_CORES = 2
SC_SUBCORES = 16
SC_WORKERS = SC_CORES * SC_SUBCORES
SC_CHUNK_MAX = 128

D_MODEL = 1024
A_WIDTH = D_MODEL
A_GROUPS = 8
CHUNK = 128
HEAD_DIM = 128
N_Q_HEADS = D_MODEL // HEAD_DIM
N_KV_HEADS = 2
REP = N_Q_HEADS // N_KV_HEADS
WINDOW = 128
ROPE_THETA = 10000.0
Q_W = N_Q_HEADS * HEAD_DIM
KV_W = N_KV_HEADS * HEAD_DIM
IN_W = 2 * A_WIDTH + Q_W + 2 * KV_W + 2 * D_MODEL
COL_U = 0
COL_V = COL_U + A_WIDTH
COL_Q = COL_V + A_WIDTH
COL_K = COL_Q + Q_W
COL_VA = COL_K + KV_W
COL_GA = COL_VA + KV_W
COL_GB = COL_GA + D_MODEL
N_GROUPS = 4
EXPERTS_PER_GROUP = 8
N_EXPERTS = N_GROUPS * EXPERTS_PER_GROUP
TOP_K = 2
D_EXPERT = 512
EPS = 1e-6
NEG = -1e30

TM_IN = 256
TQ = 256
DENSE_COLS = 256
ROUTER_ROWS = 64
assert EXPERTS_PER_GROUP == SUBLANES and SUBLANES + N_EXPERTS <= ROUTER_ROWS
TM_ROW = 512
TM_DEST = 2048
MOE_ROWS_MAX = 512
MOE_MIN_BLOCKS = 4
VMEM_LIMIT = 56 * 1024 * 1024
assert VMEM_LIMIT < VMEM_BYTES_V7X


def _rms(x, g):
    return x * lax.rsqrt(jnp.mean(x * x, axis=-1, keepdims=True) + EPS) * g


ROW_LINES = D_MODEL // 2 // LANES
ROW3 = (ROW_LINES, LANES)
HIGH_HALF = 0xFFFF0000


def _as_rows3(a):
    return a.reshape((a.shape[0] // ROW_LINES,) + ROW3)


def _as_lines(a):
    return a.reshape((a.shape[0] * ROW_LINES, LANES))


def _store_rows3(lines_ref, val):
    rows = val.shape[0]
    bits = lax.bitcast_convert_type(val.astype(BF16).astype(F32), U32)
    half = ROW_LINES * LANES
    for s in range(ROW_LINES):
        lo = bits[:, s * LANES:(s + 1) * LANES] >> 16
        hi = bits[:, half + s * LANES:half + (s + 1) * LANES] & U32(HIGH_HALF)
        lines_ref[pl.ds(s, rows, stride=ROW_LINES), :] = lax.bitcast_convert_type(lo | hi, I32)


def _load_rows3(lines_ref):
    rows = lines_ref.shape[0] // ROW_LINES
    words = [lax.bitcast_convert_type(lines_ref[pl.ds(s, rows, stride=ROW_LINES), :], U32)
             for s in range(ROW_LINES)]
    lo = [lax.bitcast_convert_type(w << 16, F32) for w in words]
    hi = [lax.bitcast_convert_type(w & U32(HIGH_HALF), F32) for w in words]
    return jnp.concatenate(lo + hi, axis=1)


def _rows3_spec(rows, index_map):
    return pl.BlockSpec((rows * ROW_LINES, LANES), index_map)


def _const_spec(shape):
    nd = len(shape)
    return pl.BlockSpec(shape, lambda *_: (0,) * nd, pipeline_mode=pl.Buffered(1))


def _inproj_kernel(x_ref, gmix_ref, win_ref, gv_ref, ws_ref, bs_ref, gq_ref, gk_ref, cos_ref, sin_ref,
                   wpa_ref, ma_ref, sgb_ref, q_ref, k_ref, v_ref, u_scr, vn_scr, a_scr):
    tm = x_ref.shape[0]
    h = _rms(x_ref[...], gmix_ref[...]).astype(BF16)

    def proj(lo, width):
        return jnp.dot(h, win_ref[:, lo:lo + width], preferred_element_type=F32)

    vn_scr[...] = _rms(jax.nn.gelu(proj(COL_V, A_WIDTH)), gv_ref[...]).astype(BF16)
    u_scr[...] = jax.nn.gelu(proj(COL_U, A_WIDTH))
    for c in range(tm // CHUNK):
        rows = slice(c * CHUNK, (c + 1) * CHUNK)
        for g in range(A_GROUPS):
            cols = slice(g * LANES, (g + 1) * LANES)
            mixed = jnp.dot(ws_ref[g], vn_scr[rows, cols], preferred_element_type=F32) + bs_ref[g]
            a_scr[rows, cols] = (u_scr[rows, cols] * mixed).astype(BF16)
    ya = jnp.dot(a_scr[...], wpa_ref[...], preferred_element_type=F32)
    ma_ref[...] = jax.nn.sigmoid(proj(COL_GA, D_MODEL)) * ya
    sgb_ref[...] = jax.nn.sigmoid(proj(COL_GB, D_MODEL))

    cos = cos_ref[...]
    sin = sin_ref[...]

    def norm_rope(z, g):
        zn = _rms(z, g)
        return zn * cos + pltpu.roll(zn, HEAD_DIM // 2, 1) * sin

    qz = proj(COL_Q, Q_W)
    for hd in range(N_Q_HEADS):
        cols = slice(hd * HEAD_DIM, (hd + 1) * HEAD_DIM)
        q_ref[:, cols] = norm_rope(qz[:, cols], gq_ref[...]).astype(BF16)
    kz = proj(COL_K, KV_W)
    for hd in range(N_KV_HEADS):
        cols = slice(hd * HEAD_DIM, (hd + 1) * HEAD_DIM)
        k_ref[:, cols] = norm_rope(kz[:, cols], gk_ref[...]).astype(BF16)
    v_ref[...] = proj(COL_VA, KV_W).astype(BF16)


def _inproj(x, p, cos, sin, seq):
    t = x.shape[0]
    tm = TM_IN
    n_pos = seq // tm
    row = lambda w: pl.BlockSpec((tm, w), lambda i: (i, 0))
    pos = pl.BlockSpec((tm, HEAD_DIM), lambda i: (i % n_pos, 0))
    return pl.pallas_call(
        _inproj_kernel,
        grid=(t // tm,),
        in_specs=[row(D_MODEL), _const_spec((1, D_MODEL)), _const_spec((D_MODEL, IN_W)),
                  _const_spec((1, A_WIDTH)), _const_spec((A_GROUPS, CHUNK, CHUNK)),
                  _const_spec((A_GROUPS, CHUNK, LANES)), _const_spec((1, HEAD_DIM)),
                  _const_spec((1, HEAD_DIM)), pos, pos, _const_spec((A_WIDTH, D_MODEL))],
        out_specs=[row(D_MODEL), row(D_MODEL), row(Q_W), row(KV_W), row(KV_W)],
        out_shape=[jax.ShapeDtypeStruct((t, D_MODEL), F32), jax.ShapeDtypeStruct((t, D_MODEL), F32),
                   jax.ShapeDtypeStruct((t, Q_W), BF16), jax.ShapeDtypeStruct((t, KV_W), BF16),
                   jax.ShapeDtypeStruct((t, KV_W), BF16)],
        scratch_shapes=[pltpu.VMEM((tm, A_WIDTH), F32), pltpu.VMEM((tm, A_WIDTH), BF16),
                        pltpu.VMEM((tm, A_WIDTH), BF16)],
        compiler_params=pltpu.CompilerParams(dimension_semantics=("parallel",),
                                             vmem_limit_bytes=VMEM_LIMIT),
        name="inproj",
    )(x, p["norm_mix_g"], p["w_in"], p["norm_v_g"], p["w_spatial"], p["b_spatial"], p["q_norm_g"],
      p["k_norm_g"], cos, sin, p["w_proj_a"])


def _attn_kernel(sink_ref, x_ref, ma_ref, sgb_ref, q_ref, kp_ref, kc_ref, kn_ref, vp_ref, vc_ref, vn_ref,
                 wpb_ref, wout_ref, gffn_ref, wr_ref, br_ref,
                 x1_ref, hn_ref, route_ref, gate_ref, counts_ref, kcat, vcat, o_scr, s_scr, p_scr, sink_scr, m_scr,
                 *, tiles_per_seq, n_tiles):
    tq = x_ref.shape[0]
    blk = WINDOW
    i = pl.program_id(0)
    slot = i % 2

    @pl.when(i == 0)
    def _():
        o_scr[...] = jnp.zeros_like(o_scr)
        counts_ref[...] = jnp.zeros_like(counts_ref)

    pos_tile = jnp.minimum(i, n_tiles - 1) % tiles_per_seq
    has_prev = pos_tile > 0
    has_next = pos_tile < tiles_per_seq - 1

    kcat[0:blk] = kp_ref[...]
    kcat[blk:blk + tq] = kc_ref[...]
    kcat[blk + tq:] = kn_ref[...]
    vcat[0:blk] = vp_ref[...]
    vcat[blk:blk + tq] = vc_ref[...]
    vcat[blk + tq:] = vn_ref[...]

    qr = lax.broadcasted_iota(I32, (blk, blk), 0)
    kc = lax.broadcasted_iota(I32, (blk, blk), 1)
    scale = HEAD_DIM ** -0.5
    n_sub = tq // blk
    pairs = [(j, g) for j in range(n_sub) for g in range(N_KV_HEADS)]

    def keys(ref, j, g):
        return ref[j * blk:(j + 3) * blk, g * HEAD_DIM:(g + 1) * HEAD_DIM]

    def head_cols(g, r):
        hd = g * REP + r
        return slice(hd * HEAD_DIM, (hd + 1) * HEAD_DIM)

    for b, (j, g) in enumerate(pairs):
        rows = slice(j * blk, (j + 1) * blk)
        qs = jnp.concatenate([q_ref[rows, head_cols(g, r)] for r in range(REP)], axis=0)
        s_scr[b] = lax.dot_general(qs, keys(kcat, j, g), (((1,), (1,)), ((), ())),
                                   preferred_element_type=F32)
    log2e = 1.4426950408889634

    def softmax(b, r):
        j, g = pairs[b]
        hrows = slice(r * blk, (r + 1) * blk)
        z = s_scr[b, hrows, :] * (scale * log2e)
        lo_ok = kc >= (qr + jnp.where(has_prev, 0, blk) if j == 0 else qr)
        hi_ok = kc <= (qr - jnp.where(has_next, 0, blk) if j == n_sub - 1 else qr)
        z = jnp.concatenate([jnp.where(lo_ok, z[:, :blk], NEG), z[:, blk:2 * blk],
                             jnp.where(hi_ok, z[:, 2 * blk:], NEG)], axis=1)
        sink = sink_ref[g * REP + r] * log2e
        m = jnp.maximum(jnp.max(z, axis=-1, keepdims=True), sink)
        p_scr[b, hrows, :] = jnp.exp2(z - m).astype(BF16)
        sink_scr[b, hrows, :] = jnp.broadcast_to(jnp.exp2(sink - m), (blk, LANES))

    def values(b):
        j, g = pairs[b]
        rows = slice(j * blk, (j + 1) * blk)
        v_ext = jnp.concatenate([keys(vcat, j, g), jnp.ones((3 * blk, HEAD_DIM), BF16)], axis=1)
        acc = jnp.dot(p_scr[b], v_ext, preferred_element_type=F32)
        o = (acc[:, :HEAD_DIM] / (acc[:, HEAD_DIM:] + sink_scr[b])).astype(BF16)
        for r in range(REP):
            o_scr[slot, rows, head_cols(g, r)] = o[r * blk:(r + 1) * blk, :]

    def merged_cols(cols):
        yb = jnp.dot(o_scr[1 - slot], wpb_ref[:, cols], preferred_element_type=F32)
        m_scr[:, cols] = (ma_ref[:, cols] + sgb_ref[:, cols] * yb).astype(BF16)

    def x1_cols(cols):
        x1_ref[:, cols] = x_ref[:, cols] + jnp.dot(m_scr[...], wout_ref[:, cols], preferred_element_type=F32)

    col_chunks = [slice(c * DENSE_COLS, (c + 1) * DENSE_COLS) for c in range(D_MODEL // DENSE_COLS)]
    dense = [functools.partial(f, cols) for f in (merged_cols, x1_cols) for cols in col_chunks]
    units = [(b, r) for b in range(len(pairs)) for r in range(REP)]
    units_per_dense = len(units) // len(dense)
    for n, (b, r) in enumerate(units):
        softmax(b, r)
        if (n + 1) % units_per_dense == 0:
            dense[(n + 1) // units_per_dense - 1]()
        if r == REP - 1:
            values(b)

    hn = _rms(x1_ref[...], gffn_ref[...])
    _store_rows3(hn_ref, hn)
    def wide(a):
        return jnp.concatenate([a] * (tq // LANES), axis=1)

    logits = lax.dot_general(wr_ref[...], hn.astype(BF16), (((1,), (1,)), ((), ())),
                             preferred_element_type=F32) + wide(br_ref[...])
    sub = lax.broadcasted_iota(I32, (SUBLANES, tq), 0).astype(F32)
    ninf = -jnp.inf

    def cmax(a):
        return jnp.max(a, axis=0, keepdims=True)

    def csum(a):
        return jnp.sum(a, axis=0, keepdims=True)

    def first_row(mask):
        return jnp.min(jnp.where(mask, sub, float(SUBLANES)), axis=0, keepdims=True)

    def group_rows(g):
        return logits[(g + 1) * SUBLANES:(g + 2) * SUBLANES]

    gl = jnp.where(sub < N_GROUPS, logits[0:SUBLANES], ninf)
    gmax = cmax(gl)
    g_sel = first_row(gl == gmax)
    g_p = 1.0 / csum(jnp.exp(gl - gmax))
    el = group_rows(0)
    for g in range(1, N_GROUPS):
        el = jnp.where(g_sel == g, group_rows(g), el)
    ee = jnp.exp(el - cmax(el))
    eprob = ee / csum(ee)
    p1 = cmax(eprob)
    i1 = first_row(eprob == p1)
    eprob2 = jnp.where(sub == i1, -1.0, eprob)
    p2 = cmax(eprob2)
    i2 = first_row(eprob2 == p2)
    psum = p1 + p2
    w1 = g_p * p1 / psum
    w2 = g_p * p2 / psum
    e1 = g_sel * EXPERTS_PER_GROUP + i1
    e2 = g_sel * EXPERTS_PER_GROUP + i2

    erow = lax.broadcasted_iota(I32, (N_EXPERTS, tq), 0).astype(F32)
    oh1 = erow == e1
    oh2 = erow == e2
    cnt = (jnp.where(oh1, 1.0, 0.0) + jnp.where(oh2, 1.0, 0.0)) * jnp.where(i > 0, 1.0, 0.0)
    ri = lax.broadcasted_iota(I32, (tq, tq), 0)
    ci = lax.broadcasted_iota(I32, (tq, tq), 1)
    earlier = jnp.where(ri < ci, 1.0, 0.0).astype(BF16)
    base = wide(counts_ref[...]) + jnp.dot(cnt.astype(BF16), earlier, preferred_element_type=F32)
    r1 = csum(jnp.where(oh1, base, 0.0))
    r2 = csum(jnp.where(oh2, base, 0.0))
    counts_ref[...] = counts_ref[...] + jnp.sum(cnt, axis=1, keepdims=True)

    route = jnp.where(sub == 0.0, e1, jnp.where(sub == 1.0, e2,
                      jnp.where(sub == 2.0, r1, jnp.where(sub == 3.0, r2, 0.0))))
    route_ref[...] = route.astype(I32)
    gates = jnp.where(sub == 0.0, w1, jnp.where(sub == 1.0, w2, 0.0))
    gate_ref[...] = jnp.concatenate([gates, jnp.zeros((LANES - SUBLANES, tq), F32)], axis=0).T


def _attn(x, ma, sgb, q, k, v, p, seq):
    t = x.shape[0]
    tq = TQ
    sub = tq // WINDOW
    last_blk = t // WINDOW - 1
    n_tiles = t // tq
    att = lambda i: jnp.minimum(i, n_tiles - 1)
    post = lambda i: jnp.maximum(i - 1, 0)
    att_row = lambda w: pl.BlockSpec((tq, w), lambda i: (att(i), 0))
    row = lambda w: pl.BlockSpec((tq, w), lambda i: (post(i), 0))
    prev = pl.BlockSpec((WINDOW, KV_W), lambda i: (jnp.maximum(att(i) * sub - 1, 0), 0))
    nxt = pl.BlockSpec((WINDOW, KV_W), lambda i: (jnp.minimum((att(i) + 1) * sub, last_blk), 0))
    return pl.pallas_call(
        functools.partial(_attn_kernel, tiles_per_seq=seq // tq, n_tiles=n_tiles),
        grid=(n_tiles + 1,),
        in_specs=[pl.BlockSpec(memory_space=pltpu.SMEM),
                  row(D_MODEL), row(D_MODEL), row(D_MODEL), att_row(Q_W),
                  prev, att_row(KV_W), nxt, prev, att_row(KV_W), nxt,
                  _const_spec((Q_W, D_MODEL)), _const_spec((D_MODEL, D_MODEL)), _const_spec((1, D_MODEL)),
                  _const_spec((ROUTER_ROWS, D_MODEL)), _const_spec((ROUTER_ROWS, LANES))],
        out_specs=[row(D_MODEL), _rows3_spec(tq, lambda i: (post(i), 0)),
                   pl.BlockSpec((SUBLANES, tq), lambda i: (0, post(i))), row(LANES),
                   pl.BlockSpec((N_EXPERTS, LANES), lambda i: (0, 0))],
        out_shape=[jax.ShapeDtypeStruct((t, D_MODEL), F32), jax.ShapeDtypeStruct((t * ROW_LINES, LANES), I32),
                   jax.ShapeDtypeStruct((SUBLANES, t), I32), jax.ShapeDtypeStruct((t, LANES), F32),
                   jax.ShapeDtypeStruct((N_EXPERTS, LANES), F32)],
        scratch_shapes=[pltpu.VMEM((tq + 2 * WINDOW, KV_W), BF16), pltpu.VMEM((tq + 2 * WINDOW, KV_W), BF16),
                        pltpu.VMEM((2, tq, Q_W), BF16),
                        pltpu.VMEM((sub * N_KV_HEADS, REP * WINDOW, 3 * WINDOW), F32),
                        pltpu.VMEM((sub * N_KV_HEADS, REP * WINDOW, 3 * WINDOW), BF16),
                        pltpu.VMEM((sub * N_KV_HEADS, REP * WINDOW, LANES), F32),
                        pltpu.VMEM((tq, D_MODEL), BF16)],
        compiler_params=pltpu.CompilerParams(dimension_semantics=("arbitrary",),
                                             vmem_limit_bytes=VMEM_LIMIT),
        name="attn",
    )(p["sink"], x, ma, sgb, q, k, k, k, v, v, v, p["w_proj_b"], p["w_out"], p["norm_ffn_g"],
      p["w_router"], p["b_router"])


def _dest_kernel(route_ref, starts_ref, dest_ref):
    route = route_ref[...].astype(F32)
    td = route.shape[1]
    sub = lax.broadcasted_iota(I32, route.shape, 0)
    erow = lax.broadcasted_iota(I32, (N_EXPERTS, td), 0).astype(F32)
    starts = jnp.broadcast_to(starts_ref[:, 0:1], (N_EXPERTS, td))

    def slot(k):
        start = jnp.sum(jnp.where(erow == route[k:k + 1], starts, 0.0), axis=0, keepdims=True)
        return start + route[TOP_K + k:TOP_K + k + 1]

    dest_ref[...] = jnp.where(sub == 0, slot(0), jnp.where(sub == 1, slot(1), 0.0)).astype(I32)


def _dest(route, pad_starts):
    t = route.shape[1]
    td = min(TM_DEST, t)
    return pl.pallas_call(
        _dest_kernel,
        grid=(t // td,),
        in_specs=[pl.BlockSpec((SUBLANES, td), lambda i: (0, i)), _const_spec((N_EXPERTS, LANES))],
        out_specs=pl.BlockSpec((SUBLANES, td), lambda i: (0, i)),
        out_shape=jax.ShapeDtypeStruct((SUBLANES, t), I32),
        compiler_params=pltpu.CompilerParams(dimension_semantics=("parallel",)),
        name="dest",
    )(route, pad_starts)


def _sc_mesh():
    return plsc.VectorSubcoreMesh(core_axis_name="c", subcore_axis_name="s")


def _sc_worker():
    return lax.axis_index("s") * SC_CORES + lax.axis_index("c")


def _sc_chunk(t):
    return min(SC_CHUNK_MAX, t // (SC_WORKERS * SUBLANES))


def _dispatch(hn, dests, n_rows):
    t = hn.shape[0]
    chunk = dests[0].shape[1]
    per_worker = t // SC_WORKERS
    n_chunks = per_worker // chunk
    idx = pltpu.VMEM((n_chunks, chunk), I32)

    @functools.partial(
        pl.kernel, mesh=_sc_mesh(), out_type=jax.ShapeDtypeStruct((n_rows,) + ROW3, I32),
        scratch_types=[idx, idx, pltpu.VMEM((chunk,) + ROW3, I32), pltpu.SemaphoreType.DMA])
    def scatter_rows(hn_hbm, d0_hbm, d1_hbm, xs_hbm, i0_v, i1_v, rows_v, sem):
        w = _sc_worker()
        pltpu.sync_copy(d0_hbm.at[pl.ds(w * n_chunks, n_chunks)], i0_v)
        pltpu.sync_copy(d1_hbm.at[pl.ds(w * n_chunks, n_chunks)], i1_v)

        @pl.loop(0, n_chunks)
        def _(j):
            pltpu.sync_copy(hn_hbm.at[pl.ds(w * per_worker + j * chunk, chunk)], rows_v)
            copies = [pltpu.make_async_copy(rows_v, xs_hbm.at[i_v.at[j]], sem) for i_v in (i0_v, i1_v)]
            for cp in copies:
                cp.start()
            for cp in copies:
                cp.wait()

    return scatter_rows(hn, *dests)


def _moe_block_rows(t):
    rows = MOE_ROWS_MAX
    while rows > CHUNK and (t * TOP_K) // N_EXPERTS < MOE_MIN_BLOCKS * rows:
        rows //= 2
    return rows


def _moe_kernel(be_ref, nvalid_ref, nused_ref, xs_ref, wg_ref, wu_ref, wd_ref, yb_ref, wg_scr, wu_scr, wd_scr):
    i = pl.program_id(0)
    used = i < nused_ref[0]

    @pl.when(jnp.logical_not(used))
    def _():
        yb_ref[...] = jnp.zeros_like(yb_ref)

    @pl.when(used)
    def _():
        @pl.when(jnp.logical_or(i == 0, be_ref[i] != be_ref[jnp.maximum(i - 1, 0)]))
        def _():
            wg_scr[...] = wg_ref[0].astype(BF16)
            wu_scr[...] = wu_ref[0].astype(BF16)
            wd_scr[...] = wd_ref[0].astype(BF16)

        row = lax.broadcasted_iota(I32, (xs_ref.shape[0] // ROW_LINES, 1), 0)
        x = jnp.where(row < nvalid_ref[i], _load_rows3(xs_ref), 0.0).astype(BF16)
        gate = jnp.dot(x, wg_scr[...], preferred_element_type=F32)
        up = jnp.dot(x, wu_scr[...], preferred_element_type=F32)
        hid = (jax.nn.silu(gate) * up).astype(BF16)
        _store_rows3(yb_ref, jnp.dot(hid, wd_scr[...], preferred_element_type=F32))


def _moe(block_e, n_valid, n_used, xs, p, block_rows):
    n_blocks = xs.shape[0] // (block_rows * ROW_LINES)

    def rows(i, be, nv, nu):
        return (jnp.minimum(i, nu[0] - 1), 0)

    def expert(i, be, nv, nu):
        return (be[jnp.minimum(i, nu[0] - 1)], 0, 0)

    return pl.pallas_call(
        _moe_kernel,
        grid_spec=pltpu.PrefetchScalarGridSpec(
            num_scalar_prefetch=3,
            grid=(n_blocks,),
            in_specs=[_rows3_spec(block_rows, rows),
                      pl.BlockSpec((1, D_MODEL, D_EXPERT), expert),
                      pl.BlockSpec((1, D_MODEL, D_EXPERT), expert),
                      pl.BlockSpec((1, D_EXPERT, D_MODEL), expert)],
            out_specs=_rows3_spec(block_rows, lambda i, be, nv, nu: (i, 0)),
            scratch_shapes=[pltpu.VMEM((D_MODEL, D_EXPERT), BF16), pltpu.VMEM((D_MODEL, D_EXPERT), BF16),
                            pltpu.VMEM((D_EXPERT, D_MODEL), BF16)],
        ),
        out_shape=jax.ShapeDtypeStruct(xs.shape, I32),
        compiler_params=pltpu.CompilerParams(dimension_semantics=("arbitrary",),
                                             vmem_limit_bytes=VMEM_LIMIT),
        name="moe",
    )(block_e, n_valid, n_used, xs, p["w_gate_e"], p["w_up_e"], p["w_down_e"])


def _gather(yb, dests):
    chunk = dests[0].shape[1]
    t = dests[0].shape[0] * chunk
    per_worker = t // SC_WORKERS
    n_chunks = per_worker // chunk
    idx = pltpu.VMEM((n_chunks, chunk), I32)
    out = jax.ShapeDtypeStruct((t,) + ROW3, I32)

    @functools.partial(
        pl.kernel, mesh=_sc_mesh(), out_type=(out, out),
        scratch_types=[idx, idx, pltpu.VMEM((chunk,) + ROW3, I32), pltpu.SemaphoreType.DMA])
    def gather_rows(yb_hbm, d0_hbm, d1_hbm, y0_hbm, y1_hbm, i0_v, i1_v, rows_v, sem):
        w = _sc_worker()
        pltpu.sync_copy(d0_hbm.at[pl.ds(w * n_chunks, n_chunks)], i0_v)
        pltpu.sync_copy(d1_hbm.at[pl.ds(w * n_chunks, n_chunks)], i1_v)

        @pl.loop(0, n_chunks)
        def _(j):
            rows = pl.ds(w * per_worker + j * chunk, chunk)
            for i_v, y_hbm in ((i0_v, y0_hbm), (i1_v, y1_hbm)):
                pltpu.async_copy(yb_hbm.at[i_v.at[j]], rows_v, sem).wait()
                pltpu.sync_copy(rows_v, y_hbm.at[rows])

    return gather_rows(yb, *dests)


def _combine_kernel(x1_ref, gate_ref, y0_ref, y1_ref, out_ref):
    gate = gate_ref[...]
    out_ref[...] = x1_ref[...] + (_load_rows3(y0_ref) * gate[:, 0:1] + _load_rows3(y1_ref) * gate[:, 1:2])


def _combine(x1, gate, y0, y1):
    t = x1.shape[0]
    tm = TM_ROW
    return pl.pallas_call(
        _combine_kernel,
        grid=(t // tm,),
        in_specs=[pl.BlockSpec((tm, D_MODEL), lambda i: (i, 0)),
                  pl.BlockSpec((tm, LANES), lambda i: (i, 0)),
                  _rows3_spec(tm, lambda i: (i, 0)), _rows3_spec(tm, lambda i: (i, 0))],
        out_specs=pl.BlockSpec((tm, D_MODEL), lambda i: (i, 0)),
        out_shape=jax.ShapeDtypeStruct((t, D_MODEL), F32),
        compiler_params=pltpu.CompilerParams(dimension_semantics=("parallel",)),
        name="combine",
    )(x1, gate, y0, y1)


def _rope_tables(seq):
    half = HEAD_DIM // 2
    inv_freq = ROPE_THETA ** (-jnp.arange(half, dtype=F32) / half)
    ang = jnp.arange(seq).astype(F32)[:, None] * inv_freq[None, :]
    cos = jnp.cos(ang)
    sin = jnp.sin(ang)
    return jnp.concatenate([cos, cos], axis=-1), jnp.concatenate([-sin, sin], axis=-1)


def _layer(x, p, seq):
    t = x.shape[0]
    cos, sin = _rope_tables(seq)
    ma, sgb, q, k, v = _inproj(x, p, cos, sin, seq)
    x1, hn, route, gate, counts_f = _attn(x, ma, sgb, q, k, v, p, seq)

    block_rows = _moe_block_rows(t)
    counts = counts_f[:, 0].astype(I32)
    padded = (counts + block_rows - 1) // block_rows * block_rows
    pad_ends = jnp.cumsum(padded)
    pad_starts = pad_ends - padded
    n_blocks = (t * TOP_K) // block_rows + N_EXPERTS
    block_start = jnp.arange(n_blocks, dtype=I32) * block_rows
    in_expert = jnp.logical_and(block_start[:, None] >= pad_starts[None, :],
                                block_start[:, None] < pad_ends[None, :]).astype(I32)
    block_e = jnp.minimum(jnp.sum((block_start[:, None] >= pad_ends[None, :]).astype(I32), axis=1), N_EXPERTS - 1)
    n_valid = jnp.sum(in_expert * jnp.clip(pad_starts + counts - block_start[:, None], 0, block_rows), axis=1)
    n_used = pad_ends[-1:] // block_rows
    starts_col = jnp.broadcast_to(pad_starts.astype(F32)[:, None], (N_EXPERTS, LANES))

    dest = _dest(route, starts_col)
    dests = [dest[k].reshape(t // _sc_chunk(t), _sc_chunk(t)) for k in range(TOP_K)]
    xs = _dispatch(_as_rows3(hn), dests, n_blocks * block_rows)
    yb = _moe(block_e, n_valid, n_used, _as_lines(xs), p, block_rows)
    y0, y1 = _gather(_as_rows3(yb), dests)
    return _combine(x1, gate, _as_lines(y0), _as_lines(y1))


def kernel(x_prompt, x_sample, norm_mix_g, w_in, norm_v_g, w_spatial, b_spatial, q_norm_g, k_norm_g, sink,
           w_proj_a, w_proj_b, w_out, norm_ffn_g, w_router_group, b_router_group, w_router_expert,
           b_router_expert, w_gate_e, w_up_e, w_down_e):
    depth = w_in.shape[0]
    layers = []
    for l in range(depth):
        w_router = jnp.zeros((ROUTER_ROWS, D_MODEL), F32)
        w_router = w_router.at[:N_GROUPS].set(w_router_group[l].T)
        w_router = w_router.at[SUBLANES:SUBLANES + N_EXPERTS].set(w_router_expert[l].T)
        b_router = jnp.zeros((ROUTER_ROWS,), F32)
        b_router = b_router.at[:N_GROUPS].set(b_router_group[l])
        b_router = b_router.at[SUBLANES:SUBLANES + N_EXPERTS].set(b_router_expert[l])
        b_router = jnp.broadcast_to(b_router[:, None], (ROUTER_ROWS, LANES))
        layers.append(dict(
            norm_mix_g=norm_mix_g[l][None], w_in=w_in[l].astype(BF16), norm_v_g=norm_v_g[l][None],
            w_spatial=w_spatial[l].astype(BF16),
            b_spatial=jnp.broadcast_to(b_spatial[l][:, :, None], (A_GROUPS, CHUNK, LANES)),
            q_norm_g=q_norm_g[l][None], k_norm_g=k_norm_g[l][None], sink=sink[l],
            w_proj_a=w_proj_a[l].astype(BF16), w_proj_b=w_proj_b[l].astype(BF16), w_out=w_out[l].astype(BF16),
            norm_ffn_g=norm_ffn_g[l][None], w_router=w_router.astype(BF16), b_router=b_router,
            w_gate_e=w_gate_e[l], w_up_e=w_up_e[l], w_down_e=w_down_e[l]))

    def trunk(x):
        b, s, d = x.shape
        y = x.reshape(b * s, d)
        for p in layers:
            y = _layer(y, p, s)
        return y.reshape(b, s, d)

    return trunk(x_prompt), trunk(x_sample)
```

```python
import functools

import jax
import jax.numpy as jnp
from jax import lax
from jax.experimental import pallas as pl
from jax.experimental.pallas import tpu as pltpu
from jax.experimental.pallas import tpu_sc as plsc

F32 = jnp.float32
BF16 = jnp.bfloat16
I32 = jnp.int32
U32 = jnp.uint32

LANES = 128
SUBLANES = 8
VMEM_BYTES_V7X = 64 * 1024 * 1024
SC_CORES = 2
SC_SUBCORES = 16
SC_WORKERS = SC_CORES * SC_SUBCORES
SC_CHUNK_MAX = 128

D_MODEL = 1024
A_WIDTH = D_MODEL
A_GROUPS = 8
CHUNK = 128
HEAD_DIM = 128
N_Q_HEADS = D_MODEL // HEAD_DIM
N_KV_HEADS = 2
REP = N_Q_HEADS // N_KV_HEADS
WINDOW = 128
ROPE_THETA = 10000.0
Q_W = N_Q_HEADS * HEAD_DIM
KV_W = N_KV_HEADS * HEAD_DIM
IN_W = 2 * A_WIDTH + Q_W + 2 * KV_W + 2 * D_MODEL
COL_U = 0
COL_V = COL_U + A_WIDTH
COL_Q = COL_V + A_WIDTH
COL_K = COL_Q + Q_W
COL_VA = COL_K + KV_W
COL_GA = COL_VA + KV_W
COL_GB = COL_GA + D_MODEL
N_GROUPS = 4
EXPERTS_PER_GROUP = 8
N_EXPERTS = N_GROUPS * EXPERTS_PER_GROUP
TOP_K = 2
D_EXPERT = 512
EPS = 1e-6
NEG = -1e30

TM_IN = 256
TQ = 512
DENSE_COLS = 256
ROUTER_ROWS = 64
assert EXPERTS_PER_GROUP == SUBLANES and SUBLANES + N_EXPERTS <= ROUTER_ROWS
TM_ROW = 512
TM_DEST = 2048
MOE_ROWS_MAX = 1024
MOE_MIN_BLOCKS = 4
VMEM_LIMIT = 56 * 1024 * 1024
assert VMEM_LIMIT < VMEM_BYTES_V7X


def _rms(x, g):
    return x * lax.rsqrt(jnp.mean(x * x, axis=-1, keepdims=True) + EPS) * g


ROW_LINES = D_MODEL // 2 // LANES
ROW3 = (ROW_LINES, LANES)
HIGH_HALF = 0xFFFF0000


def _as_rows3(a):
    return a.reshape((a.shape[0] // ROW_LINES,) + ROW3)


def _as_lines(a):
    return a.reshape((a.shape[0] * ROW_LINES, LANES))


def _store_rows3(lines_ref, val):
    rows = val.shape[0]
    bits = lax.bitcast_convert_type(val.astype(BF16).astype(F32), U32)
    half = ROW_LINES * LANES
    for s in range(ROW_LINES):
        lo = bits[:, s * LANES:(s + 1) * LANES] >> 16
        hi = bits[:, half + s * LANES:half + (s + 1) * LANES] & U32(HIGH_HALF)
        lines_ref[pl.ds(s, rows, stride=ROW_LINES), :] = lax.bitcast_convert_type(lo | hi, I32)


def _load_rows3(lines_ref):
    rows = lines_ref.shape[0] // ROW_LINES
    words = [lax.bitcast_convert_type(lines_ref[pl.ds(s, rows, stride=ROW_LINES), :], U32)
             for s in range(ROW_LINES)]
    lo = [lax.bitcast_convert_type(w << 16, F32) for w in words]
    hi = [lax.bitcast_convert_type(w & U32(HIGH_HALF), F32) for w in words]
    return jnp.concatenate(lo + hi, axis=1)


def _rows3_spec(rows, index_map):
    return pl.BlockSpec((rows * ROW_LINES, LANES), index_map)


def _const_spec(shape):
    nd = len(shape)
    return pl.BlockSpec(shape, lambda *_: (0,) * nd, pipeline_mode=pl.Buffered(1))


def _inproj_kernel(x_ref, gmix_ref, win_ref, gv_ref, ws_ref, bs_ref, gq_ref, gk_ref, cos_ref, sin_ref,
                   wpa_ref, ma_ref, sgb_ref, q_ref, k_ref, v_ref, u_scr, vn_scr, a_scr):
    tm = x_ref.shape[0]
    h = _rms(x_ref[...], gmix_ref[...]).astype(BF16)

    def proj(lo, width):
        return jnp.dot(h, win_ref[:, lo:lo + width], preferred_element_type=F32)

    vn_scr[...] = _rms(jax.nn.gelu(proj(COL_V, A_WIDTH)), gv_ref[...]).astype(BF16)
    u_scr[...] = jax.nn.gelu(proj(COL_U, A_WIDTH))
    for c in range(tm // CHUNK):
        rows = slice(c * CHUNK, (c + 1) * CHUNK)
        for g in range(A_GROUPS):
            cols = slice(g * LANES, (g + 1) * LANES)
            mixed = jnp.dot(ws_ref[g], vn_scr[rows, cols], preferred_element_type=F32) + bs_ref[g]
            a_scr[rows, cols] = (u_scr[rows, cols] * mixed).astype(BF16)
    ya = jnp.dot(a_scr[...], wpa_ref[...], preferred_element_type=F32)
    ma_ref[...] = jax.nn.sigmoid(proj(COL_GA, D_MODEL)) * ya
    sgb_ref[...] = jax.nn.sigmoid(proj(COL_GB, D_MODEL))

    cos = cos_ref[...]
    sin = sin_ref[...]

    def norm_rope(z, g):
        zn = _rms(z, g)
        return zn * cos + pltpu.roll(zn, HEAD_DIM // 2, 1) * sin

    qz = proj(COL_Q, Q_W)
    for hd in range(N_Q_HEADS):
        cols = slice(hd * HEAD_DIM, (hd + 1) * HEAD_DIM)
        q_ref[:, cols] = norm_rope(qz[:, cols], gq_ref[...]).astype(BF16)
    kz = proj(COL_K, KV_W)
    for hd in range(N_KV_HEADS):
        cols = slice(hd * HEAD_DIM, (hd + 1) * HEAD_DIM)
        k_ref[:, cols] = norm_rope(kz[:, cols], gk_ref[...]).astype(BF16)
    v_ref[...] = proj(COL_VA, KV_W).astype(BF16)


def _inproj(x, p, cos, sin, seq):
    t = x.shape[0]
    tm = TM_IN
    n_pos = seq // tm
    row = lambda w: pl.BlockSpec((tm, w), lambda i: (i, 0))
    pos = pl.BlockSpec((tm, HEAD_DIM), lambda i: (i % n_pos, 0))
    return pl.pallas_call(
        _inproj_kernel,
        grid=(t // tm,),
        in_specs=[row(D_MODEL), _const_spec((1, D_MODEL)), _const_spec((D_MODEL, IN_W)),
                  _const_spec((1, A_WIDTH)), _const_spec((A_GROUPS, CHUNK, CHUNK)),
                  _const_spec((A_GROUPS, CHUNK, LANES)), _const_spec((1, HEAD_DIM)),
                  _const_spec((1, HEAD_DIM)), pos, pos, _const_spec((A_WIDTH, D_MODEL))],
        out_specs=[row(D_MODEL), row(D_MODEL), row(Q_W), row(KV_W), row(KV_W)],
        out_shape=[jax.ShapeDtypeStruct((t, D_MODEL), F32), jax.ShapeDtypeStruct((t, D_MODEL), F32),
                   jax.ShapeDtypeStruct((t, Q_W), BF16), jax.ShapeDtypeStruct((t, KV_W), BF16),
                   jax.ShapeDtypeStruct((t, KV_W), BF16)],
        scratch_shapes=[pltpu.VMEM((tm, A_WIDTH), F32), pltpu.VMEM((tm, A_WIDTH), BF16),
                        pltpu.VMEM((tm, A_WIDTH), BF16)],
        compiler_params=pltpu.CompilerParams(dimension_semantics=("parallel",),
                                             vmem_limit_bytes=VMEM_LIMIT),
        name="inproj",
    )(x, p["norm_mix_g"], p["w_in"], p["norm_v_g"], p["w_spatial"], p["b_spatial"], p["q_norm_g"],
      p["k_norm_g"], cos, sin, p["w_proj_a"])


def _attn_kernel(sink_ref, x_ref, ma_ref, sgb_ref, q_ref, kp_ref, kc_ref, kn_ref, vp_ref, vc_ref, vn_ref,
                 wpb_ref, wout_ref, gffn_ref, wr_ref, br_ref,
                 x1_ref, hn_ref, route_ref, gate_ref, counts_ref, kcat, vcat, o_scr, s_scr, p_scr, sink_scr, m_scr,
                 *, tiles_per_seq, n_tiles):
    tq = x_ref.shape[0]
    blk = WINDOW
    i = pl.program_id(0)
    slot = i % 2

    @pl.when(i == 0)
    def _():
        o_scr[...] = jnp.zeros_like(o_scr)
        counts_ref[...] = jnp.zeros_like(counts_ref)

    pos_tile = jnp.minimum(i, n_tiles - 1) % tiles_per_seq
    has_prev = pos_tile > 0
    has_next = pos_tile < tiles_per_seq - 1

    kcat[0:blk] = kp_ref[...]
    kcat[blk:blk + tq] = kc_ref[...]
    kcat[blk + tq:] = kn_ref[...]
    vcat[0:blk] = vp_ref[...]
    vcat[blk:blk + tq] = vc_ref[...]
    vcat[blk + tq:] = vn_ref[...]

    qr = lax.broadcasted_iota(I32, (blk, blk), 0)
    kc = lax.broadcasted_iota(I32, (blk, blk), 1)
    scale = HEAD_DIM ** -0.5
    n_sub = tq // blk
    pairs = [(j, g) for j in range(n_sub) for g in range(N_KV_HEADS)]

    def keys(ref, j, g):
        return ref[j * blk:(j + 3) * blk, g * HEAD_DIM:(g + 1) * HEAD_DIM]

    def head_cols(g, r):
        hd = g * REP + r
        return slice(hd * HEAD_DIM, (hd + 1) * HEAD_DIM)

    for b, (j, g) in enumerate(pairs):
        rows = slice(j * blk, (j + 1) * blk)
        qs = jnp.concatenate([q_ref[rows, head_cols(g, r)] for r in range(REP)], axis=0)
        s_scr[b] = lax.dot_general(qs, keys(kcat, j, g), (((1,), (1,)), ((), ())),
                                   preferred_element_type=F32)
    log2e = 1.4426950408889634

    def softmax(b, r):
        j, g = pairs[b]
        hrows = slice(r * blk, (r + 1) * blk)
        z = s_scr[b, hrows, :] * (scale * log2e)
        lo_ok = kc >= (qr + jnp.where(has_prev, 0, blk) if j == 0 else qr)
        hi_ok = kc <= (qr - jnp.where(has_next, 0, blk) if j == n_sub - 1 else qr)
        z = jnp.concatenate([jnp.where(lo_ok, z[:, :blk], NEG), z[:, blk:2 * blk],
                             jnp.where(hi_ok, z[:, 2 * blk:], NEG)], axis=1)
        sink = sink_ref[g * REP + r] * log2e
        m = jnp.maximum(jnp.max(z, axis=-1, keepdims=True), sink)
        p_scr[b, hrows, :] = jnp.exp2(z - m).astype(BF16)
        sink_scr[b, hrows, :] = jnp.broadcast_to(jnp.exp2(sink - m), (blk, LANES))

    def values(b):
        j, g = pairs[b]
        rows = slice(j * blk, (j + 1) * blk)
        v_ext = jnp.concatenate([keys(vcat, j, g), jnp.ones((3 * blk, HEAD_DIM), BF16)], axis=1)
        acc = jnp.dot(p_scr[b], v_ext, preferred_element_type=F32)
        o = (acc[:, :HEAD_DIM] / (acc[:, HEAD_DIM:] + sink_scr[b])).astype(BF16)
        for r in range(REP):
            o_scr[slot, rows, head_cols(g, r)] = o[r * blk:(r + 1) * blk, :]

    def merged_cols(cols):
        yb = jnp.dot(o_scr[1 - slot], wpb_ref[:, cols], preferred_element_type=F32)
        m_scr[:, cols] = (ma_ref[:, cols] + sgb_ref[:, cols] * yb).astype(BF16)

    def x1_cols(cols):
        x1_ref[:, cols] = x_ref[:, cols] + jnp.dot(m_scr[...], wout_ref[:, cols], preferred_element_type=F32)

    col_chunks = [slice(c * DENSE_COLS, (c + 1) * DENSE_COLS) for c in range(D_MODEL // DENSE_COLS)]
    dense = [functools.partial(f, cols) for f in (merged_cols, x1_cols) for cols in col_chunks]
    units = [(b, r) for b in range(len(pairs)) for r in range(REP)]
    units_per_dense = len(units) // len(dense)
    for n, (b, r) in enumerate(units):
        softmax(b, r)
        if (n + 1) % units_per_dense == 0:
            dense[(n + 1) // units_per_dense - 1]()
        if r == REP - 1:
            values(b)

    hn = _rms(x1_ref[...], gffn_ref[...])
    _store_rows3(hn_ref, hn)
    def wide(a):
        return jnp.concatenate([a] * (tq // LANES), axis=1)

    logits = lax.dot_general(wr_ref[...], hn.astype(BF16), (((1,), (1,)), ((), ())),
                             preferred_element_type=F32) + wide(br_ref[...])
    sub = lax.broadcasted_iota(I32, (SUBLANES, tq), 0).astype(F32)
    ninf = -jnp.inf

    def cmax(a):
        return jnp.max(a, axis=0, keepdims=True)

    def csum(a):
        return jnp.sum(a, axis=0, keepdims=True)

    def first_row(mask):
        return jnp.min(jnp.where(mask, sub, float(SUBLANES)), axis=0, keepdims=True)

    def group_rows(g):
        return logits[(g + 1) * SUBLANES:(g + 2) * SUBLANES]

    gl = jnp.where(sub < N_GROUPS, logits[0:SUBLANES], ninf)
    gmax = cmax(gl)
    g_sel = first_row(gl == gmax)
    g_p = 1.0 / csum(jnp.exp(gl - gmax))
    el = group_rows(0)
    for g in range(1, N_GROUPS):
        el = jnp.where(g_sel == g, group_rows(g), el)
    ee = jnp.exp(el - cmax(el))
    eprob = ee / csum(ee)
    p1 = cmax(eprob)
    i1 = first_row(eprob == p1)
    eprob2 = jnp.where(sub == i1, -1.0, eprob)
    p2 = cmax(eprob2)
    i2 = first_row(eprob2 == p2)
    psum = p1 + p2
    w1 = g_p * p1 / psum
    w2 = g_p * p2 / psum
    e1 = g_sel * EXPERTS_PER_GROUP + i1
    e2 = g_sel * EXPERTS_PER_GROUP + i2

    erow = lax.broadcasted_iota(I32, (N_EXPERTS, tq), 0).astype(F32)
    oh1 = erow == e1
    oh2 = erow == e2
    cnt = (jnp.where(oh1, 1.0, 0.0) + jnp.where(oh2, 1.0, 0.0)) * jnp.where(i > 0, 1.0, 0.0)
    ri = lax.broadcasted_iota(I32, (tq, tq), 0)
    ci = lax.broadcasted_iota(I32, (tq, tq), 1)
    earlier = jnp.where(ri < ci, 1.0, 0.0).astype(BF16)
    base = wide(counts_ref[...]) + jnp.dot(cnt.astype(BF16), earlier, preferred_element_type=F32)
    r1 = csum(jnp.where(oh1, base, 0.0))
    r2 = csum(jnp.where(oh2, base, 0.0))
    counts_ref[...] = counts_ref[...] + jnp.sum(cnt, axis=1, keepdims=True)

    route = jnp.where(sub == 0.0, e1, jnp.where(sub == 1.0, e2,
                      jnp.where(sub == 2.0, r1, jnp.where(sub == 3.0, r2, 0.0))))
    route_ref[...] = route.astype(I32)
    gates = jnp.where(sub == 0.0, w1, jnp.where(sub == 1.0, w2, 0.0))
    gate_ref[...] = jnp.concatenate([gates, jnp.zeros((LANES - SUBLANES, tq), F32)], axis=0).T


def _attn(x, ma, sgb, q, k, v, p, seq):
    t = x.shape[0]
    tq = TQ
    sub = tq // WINDOW
    last_blk = t // WINDOW - 1
    n_tiles = t // tq
    att = lambda i: jnp.minimum(i, n_tiles - 1)
    post = lambda i: jnp.maximum(i - 1, 0)
    att_row = lambda w: pl.BlockSpec((tq, w), lambda i: (att(i), 0))
    row = lambda w: pl.BlockSpec((tq, w), lambda i: (post(i), 0))
    prev = pl.BlockSpec((WINDOW, KV_W), lambda i: (jnp.maximum(att(i) * sub - 1, 0), 0))
    nxt = pl.BlockSpec((WINDOW, KV_W), lambda i: (jnp.minimum((att(i) + 1) * sub, last_blk), 0))
    return pl.pallas_call(
        functools.partial(_attn_kernel, tiles_per_seq=seq // tq, n_tiles=n_tiles),
        grid=(n_tiles + 1,),
        in_specs=[pl.BlockSpec(memory_space=pltpu.SMEM),
                  row(D_MODEL), row(D_MODEL), row(D_MODEL), att_row(Q_W),
                  prev, att_row(KV_W), nxt, prev, att_row(KV_W), nxt,
                  _const_spec((Q_W, D_MODEL)), _const_spec((D_MODEL, D_MODEL)), _const_spec((1, D_MODEL)),
                  _const_spec((ROUTER_ROWS, D_MODEL)), _const_spec((ROUTER_ROWS, LANES))],
        out_specs=[row(D_MODEL), _rows3_spec(tq, lambda i: (post(i), 0)),
                   pl.BlockSpec((SUBLANES, tq), lambda i: (0, post(i))), row(LANES),
                   pl.BlockSpec((N_EXPERTS, LANES), lambda i: (0, 0))],
        out_shape=[jax.ShapeDtypeStruct((t, D_MODEL), F32), jax.ShapeDtypeStruct((t * ROW_LINES, LANES), I32),
                   jax.ShapeDtypeStruct((SUBLANES, t), I32), jax.ShapeDtypeStruct((t, LANES), F32),
                   jax.ShapeDtypeStruct((N_EXPERTS, LANES), F32)],
        scratch_shapes=[pltpu.VMEM((tq + 2 * WINDOW, KV_W), BF16), pltpu.VMEM((tq + 2 * WINDOW, KV_W), BF16),
                        pltpu.VMEM((2, tq, Q_W), BF16),
                        pltpu.VMEM((sub * N_KV_HEADS, REP * WINDOW, 3 * WINDOW), F32),
                        pltpu.VMEM((sub * N_KV_HEADS, REP * WINDOW, 3 * WINDOW), BF16),
                        pltpu.VMEM((sub * N_KV_HEADS, REP * WINDOW, LANES), F32),
                        pltpu.VMEM((tq, D_MODEL), BF16)],
        compiler_params=pltpu.CompilerParams(dimension_semantics=("arbitrary",),
                                             vmem_limit_bytes=VMEM_LIMIT),
        name="attn",
    )(p["sink"], x, ma, sgb, q, k, k, k, v, v, v, p["w_proj_b"], p["w_out"], p["norm_ffn_g"],
      p["w_router"], p["b_router"])


def _dest_kernel(route_ref, starts_ref, dest_ref):
    route = route_ref[...].astype(F32)
    td = route.shape[1]
    sub = lax.broadcasted_iota(I32, route.shape, 0)
    erow = lax.broadcasted_iota(I32, (N_EXPERTS, td), 0).astype(F32)
    starts = jnp.broadcast_to(starts_ref[:, 0:1], (N_EXPERTS, td))

    def slot(k):
        start = jnp.sum(jnp.where(erow == route[k:k + 1], starts, 0.0), axis=0, keepdims=True)
        return start + route[TOP_K + k:TOP_K + k + 1]

    dest_ref[...] = jnp.where(sub == 0, slot(0), jnp.where(sub == 1, slot(1), 0.0)).astype(I32)


def _dest(route, pad_starts):
    t = route.shape[1]
    td = min(TM_DEST, t)
    return pl.pallas_call(
        _dest_kernel,
        grid=(t // td,),
        in_specs=[pl.BlockSpec((SUBLANES, td), lambda i: (0, i)), _const_spec((N_EXPERTS, LANES))],
        out_specs=pl.BlockSpec((SUBLANES, td), lambda i: (0, i)),
        out_shape=jax.ShapeDtypeStruct((SUBLANES, t), I32),
        compiler_params=pltpu.CompilerParams(dimension_semantics=("parallel",)),
        name="dest",
    )(route, pad_starts)


def _sc_mesh():
    return plsc.VectorSubcoreMesh(core_axis_name="c", subcore_axis_name="s")


def _sc_worker():
    return lax.axis_index("s") * SC_CORES + lax.axis_index("c")


def _sc_chunk(t):
    return min(SC_CHUNK_MAX, t // (SC_WORKERS * SUBLANES))


def _dispatch(hn, dests, n_rows):
    t = hn.shape[0]
    chunk = dests[0].shape[1]
    per_worker = t // SC_WORKERS
    n_chunks = per_worker // chunk
    idx = pltpu.VMEM((n_chunks, chunk), I32)

    @functools.partial(
        pl.kernel, mesh=_sc_mesh(), out_type=jax.ShapeDtypeStruct((n_rows,) + ROW3, I32),
        scratch_types=[idx, idx, pltpu.VMEM((chunk,) + ROW3, I32), pltpu.SemaphoreType.DMA])
    def scatter_rows(hn_hbm, d0_hbm, d1_hbm, xs_hbm, i0_v, i1_v, rows_v, sem):
        w = _sc_worker()
        pltpu.sync_copy(d0_hbm.at[pl.ds(w * n_chunks, n_chunks)], i0_v)
        pltpu.sync_copy(d1_hbm.at[pl.ds(w * n_chunks, n_chunks)], i1_v)

        @pl.loop(0, n_chunks)
        def _(j):
            pltpu.sync_copy(hn_hbm.at[pl.ds(w * per_worker + j * chunk, chunk)], rows_v)
            copies = [pltpu.make_async_copy(rows_v, xs_hbm.at[i_v.at[j]], sem) for i_v in (i0_v, i1_v)]
            for cp in copies:
                cp.start()
            for cp in copies:
                cp.wait()

    return scatter_rows(hn, *dests)


def _moe_block_rows(t):
    rows = MOE_ROWS_MAX
    while rows > CHUNK and (t * TOP_K) // N_EXPERTS < MOE_MIN_BLOCKS * rows:
        rows //= 2
    return rows


def _moe_kernel(be_ref, nvalid_ref, nused_ref, xs_ref, wg_ref, wu_ref, wd_ref, yb_ref, wg_scr, wu_scr, wd_scr):
    i = pl.program_id(0)
    used = i < nused_ref[0]

    @pl.when(jnp.logical_not(used))
    def _():
        yb_ref[...] = jnp.zeros_like(yb_ref)

    @pl.when(used)
    def _():
        @pl.when(jnp.logical_or(i == 0, be_ref[i] != be_ref[jnp.maximum(i - 1, 0)]))
        def _():
            wg_scr[...] = wg_ref[0].astype(BF16)
            wu_scr[...] = wu_ref[0].astype(BF16)
            wd_scr[...] = wd_ref[0].astype(BF16)

        row = lax.broadcasted_iota(I32, (xs_ref.shape[0] // ROW_LINES, 1), 0)
        x = jnp.where(row < nvalid_ref[i], _load_rows3(xs_ref), 0.0).astype(BF16)
        gate = jnp.dot(x, wg_scr[...], preferred_element_type=F32)
        up = jnp.dot(x, wu_scr[...], preferred_element_type=F32)
        hid = (jax.nn.silu(gate) * up).astype(BF16)
        _store_rows3(yb_ref, jnp.dot(hid, wd_scr[...], preferred_element_type=F32))


def _moe(block_e, n_valid, n_used, xs, p, block_rows):
    n_blocks = xs.shape[0] // (block_rows * ROW_LINES)

    def rows(i, be, nv, nu):
        return (jnp.minimum(i, nu[0] - 1), 0)

    def expert(i, be, nv, nu):
        return (be[jnp.minimum(i, nu[0] - 1)], 0, 0)

    return pl.pallas_call(
        _moe_kernel,
        grid_spec=pltpu.PrefetchScalarGridSpec(
            num_scalar_prefetch=3,
            grid=(n_blocks,),
            in_specs=[_rows3_spec(block_rows, rows),
                      pl.BlockSpec((1, D_MODEL, D_EXPERT), expert),
                      pl.BlockSpec((1, D_MODEL, D_EXPERT), expert),
                      pl.BlockSpec((1, D_EXPERT, D_MODEL), expert)],
            out_specs=_rows3_spec(block_rows, lambda i, be, nv, nu: (i, 0)),
            scratch_shapes=[pltpu.VMEM((D_MODEL, D_EXPERT), BF16), pltpu.VMEM((D_MODEL, D_EXPERT), BF16),
                            pltpu.VMEM((D_EXPERT, D_MODEL), BF16)],
        ),
        out_shape=jax.ShapeDtypeStruct(xs.shape, I32),
        compiler_params=pltpu.CompilerParams(dimension_semantics=("arbitrary",),
                                             vmem_limit_bytes=VMEM_LIMIT),
        name="moe",
    )(block_e, n_valid, n_used, xs, p["w_gate_e"], p["w_up_e"], p["w_down_e"])


def _gather(yb, dests):
    chunk = dests[0].shape[1]
    t = dests[0].shape[0] * chunk
    per_worker = t // SC_WORKERS
    n_chunks = per_worker // chunk
    idx = pltpu.VMEM((n_chunks, chunk), I32)
    out = jax.ShapeDtypeStruct((t,) + ROW3, I32)

    @functools.partial(
        pl.kernel, mesh=_sc_mesh(), out_type=(out, out),
        scratch_types=[idx, idx, pltpu.VMEM((chunk,) + ROW3, I32), pltpu.SemaphoreType.DMA])
    def gather_rows(yb_hbm, d0_hbm, d1_hbm, y0_hbm, y1_hbm, i0_v, i1_v, rows_v, sem):
        w = _sc_worker()
        pltpu.sync_copy(d0_hbm.at[pl.ds(w * n_chunks, n_chunks)], i0_v)
        pltpu.sync_copy(d1_hbm.at[pl.ds(w * n_chunks, n_chunks)], i1_v)

        @pl.loop(0, n_chunks)
        def _(j):
            rows = pl.ds(w * per_worker + j * chunk, chunk)
            for i_v, y_hbm in ((i0_v, y0_hbm), (i1_v, y1_hbm)):
                pltpu.async_copy(yb_hbm.at[i_v.at[j]], rows_v, sem).wait()
                pltpu.sync_copy(rows_v, y_hbm.at[rows])

    return gather_rows(yb, *dests)


def _combine_kernel(x1_ref, gate_ref, y0_ref, y1_ref, out_ref):
    gate = gate_ref[...]
    out_ref[...] = x1_ref[...] + (_load_rows3(y0_ref) * gate[:, 0:1] + _load_rows3(y1_ref) * gate[:, 1:2])


def _combine(x1, gate, y0, y1):
    t = x1.shape[0]
    tm = TM_ROW
    return pl.pallas_call(
        _combine_kernel,
        grid=(t // tm,),
        in_specs=[pl.BlockSpec((tm, D_MODEL), lambda i: (i, 0)),
                  pl.BlockSpec((tm, LANES), lambda i: (i, 0)),
                  _rows3_spec(tm, lambda i: (i, 0)), _rows3_spec(tm, lambda i: (i, 0))],
        out_specs=pl.BlockSpec((tm, D_MODEL), lambda i: (i, 0)),
        out_shape=jax.ShapeDtypeStruct((t, D_MODEL), F32),
        compiler_params=pltpu.CompilerParams(dimension_semantics=("parallel",)),
        name="combine",
    )(x1, gate, y0, y1)


def _rope_tables(seq):
    half = HEAD_DIM // 2
    inv_freq = ROPE_THETA ** (-jnp.arange(half, dtype=F32) / half)
    ang = jnp.arange(seq).astype(F32)[:, None] * inv_freq[None, :]
    cos = jnp.cos(ang)
    sin = jnp.sin(ang)
    return jnp.concatenate([cos, cos], axis=-1), jnp.concatenate([-sin, sin], axis=-1)


def _layer(x, p, seq):
    t = x.shape[0]
    cos, sin = _rope_tables(seq)
    ma, sgb, q, k, v = _inproj(x, p, cos, sin, seq)
    x1, hn, route, gate, counts_f = _attn(x, ma, sgb, q, k, v, p, seq)

    block_rows = _moe_block_rows(t)
    counts = counts_f[:, 0].astype(I32)
    padded = (counts + block_rows - 1) // block_rows * block_rows
    pad_ends = jnp.cumsum(padded)
    pad_starts = pad_ends - padded
    n_blocks = (t * TOP_K) // block_rows + N_EXPERTS
    block_start = jnp.arange(n_blocks, dtype=I32) * block_rows
    in_expert = jnp.logical_and(block_start[:, None] >= pad_starts[None, :],
                                block_start[:, None] < pad_ends[None, :]).astype(I32)
    block_e = jnp.minimum(jnp.sum((block_start[:, None] >= pad_ends[None, :]).astype(I32), axis=1), N_EXPERTS - 1)
    n_valid = jnp.sum(in_expert * jnp.clip(pad_starts + counts - block_start[:, None], 0, block_rows), axis=1)
    n_used = pad_ends[-1:] // block_rows
    starts_col = jnp.broadcast_to(pad_starts.astype(F32)[:, None], (N_EXPERTS, LANES))

    dest = _dest(route, starts_col)
    dests = [dest[k].reshape(t // _sc_chunk(t), _sc_chunk(t)) for k in range(TOP_K)]
    xs = _dispatch(_as_rows3(hn), dests, n_blocks * block_rows)
    yb = _moe(block_e, n_valid, n_used, _as_lines(xs), p, block_rows)
    y0, y1 = _gather(_as_rows3(yb), dests)
    return _combine(x1, gate, _as_lines(y0), _as_lines(y1))


def kernel(x_prompt, x_sample, norm_mix_g, w_in, norm_v_g, w_spatial, b_spatial, q_norm_g, k_norm_g, sink,
           w_proj_a, w_proj_b, w_out, norm_ffn_g, w_router_group, b_router_group, w_router_expert,
           b_router_expert, w_gate_e, w_up_e, w_down_e):
    depth = w_in.shape[0]
    layers = []
    for l in range(depth):
        w_router = jnp.zeros((ROUTER_ROWS, D_MODEL), F32)
        w_router = w_router.at[:N_GROUPS].set(w_router_group[l].T)
        w_router = w_router.at[SUBLANES:SUBLANES + N_EXPERTS].set(w_router_expert[l].T)
        b_router = jnp.zeros((ROUTER_ROWS,), F32)
        b_router = b_router.at[:N_GROUPS].set(b_router_group[l])
        b_router = b_router.at[SUBLANES:SUBLANES + N_EXPERTS].set(b_router_expert[l])
        b_router = jnp.broadcast_to(b_router[:, None], (ROUTER_ROWS, LANES))
        layers.append(dict(
            norm_mix_g=norm_mix_g[l][None], w_in=w_in[l].astype(BF16), norm_v_g=norm_v_g[l][None],
            w_spatial=w_spatial[l].astype(BF16),
            b_spatial=jnp.broadcast_to(b_spatial[l][:, :, None], (A_GROUPS, CHUNK, LANES)),
            q_norm_g=q_norm_g[l][None], k_norm_g=k_norm_g[l][None], sink=sink[l],
            w_proj_a=w_proj_a[l].astype(BF16), w_proj_b=w_proj_b[l].astype(BF16), w_out=w_out[l].astype(BF16),
            norm_ffn_g=norm_ffn_g[l][None], w_router=w_router.astype(BF16), b_router=b_router,
            w_gate_e=w_gate_e[l], w_up_e=w_up_e[l], w_down_e=w_down_e[l]))

    def trunk(x):
        b, s, d = x.shape
        y = x.reshape(b * s, d)
        for p in layers:
            y = _layer(y, p, s)
        return y.reshape(b, s, d)

    return trunk(x_prompt), trunk(x_sample)
```

```python
import functools

import jax
import jax.numpy as jnp
from jax import lax
from jax.experimental import pallas as pl
from jax.experimental.pallas import tpu as pltpu
from jax.experimental.pallas import tpu_sc as plsc

F32 = jnp.float32
BF16 = jnp.bfloat16
I32 = jnp.int32
U32 = jnp.uint32

LANES = 128
SUBLANES = 8
VMEM_BYTES_V7X = 64 * 1024 * 1024
SC_CORES = 2
SC_SUBCORES = 16
SC_WORKERS = SC_CORES * SC_SUBCORES
SC_CHUNK_MAX = 128

D_MODEL = 1024
A_WIDTH = D_MODEL
A_GROUPS = 8
CHUNK = 128
HEAD_DIM = 128
N_Q_HEADS = D_MODEL // HEAD_DIM
N_KV_HEADS = 2
REP = N_Q_HEADS // N_KV_HEADS
WINDOW = 128
ROPE_THETA = 10000.0
Q_W = N_Q_HEADS * HEAD_DIM
KV_W = N_KV_HEADS * HEAD_DIM
IN_W = 2 * A_WIDTH + Q_W + 2 * KV_W + 2 * D_MODEL
COL_U = 0
COL_V = COL_U + A_WIDTH
COL_Q = COL_V + A_WIDTH
COL_K = COL_Q + Q_W
COL_VA = COL_K + KV_W
COL_GA = COL_VA + KV_W
COL_GB = COL_GA + D_MODEL
N_GROUPS = 4
EXPERTS_PER_GROUP = 8
N_EXPERTS = N_GROUPS * EXPERTS_PER_GROUP
TOP_K = 2
D_EXPERT = 512
EPS = 1e-6
NEG = -1e30

TM_IN = 256
TQ = 512
DENSE_COLS = 256
ROUTER_ROWS = 64
assert EXPERTS_PER_GROUP == SUBLANES and SUBLANES + N_EXPERTS <= ROUTER_ROWS
TM_ROW = 512
TM_DEST = 2048
MOE_ROWS_MAX = 1024
MOE_MIN_BLOCKS = 4
VMEM_LIMIT = 56 * 1024 * 1024
assert VMEM_LIMIT < VMEM_BYTES_V7X


def _rms(x, g):
    return x * lax.rsqrt(jnp.mean(x * x, axis=-1, keepdims=True) + EPS) * g


ROW_LINES = D_MODEL // 2 // LANES
ROW3 = (ROW_LINES, LANES)
HIGH_HALF = 0xFFFF0000


def _as_rows3(a):
    return a.reshape((a.shape[0] // ROW_LINES,) + ROW3)


def _as_lines(a):
    return a.reshape((a.shape[0] * ROW_LINES, LANES))


def _store_rows3(lines_ref, val):
    rows = val.shape[0]
    bits = lax.bitcast_convert_type(val.astype(BF16).astype(F32), U32)
    half = ROW_LINES * LANES
    for s in range(ROW_LINES):
        lo = bits[:, s * LANES:(s + 1) * LANES] >> 16
        hi = bits[:, half + s * LANES:half + (s + 1) * LANES] & U32(HIGH_HALF)
        lines_ref[pl.ds(s, rows, stride=ROW_LINES), :] = lax.bitcast_convert_type(lo | hi, I32)


def _load_rows3(lines_ref):
    rows = lines_ref.shape[0] // ROW_LINES
    words = [lax.bitcast_convert_type(lines_ref[pl.ds(s, rows, stride=ROW_LINES), :], U32)
             for s in range(ROW_LINES)]
    lo = [lax.bitcast_convert_type(w << 16, F32) for w in words]
    hi = [lax.bitcast_convert_type(w & U32(HIGH_HALF), F32) for w in words]
    return jnp.concatenate(lo + hi, axis=1)


def _rows3_spec(rows, index_map):
    return pl.BlockSpec((rows * ROW_LINES, LANES), index_map)


def _const_spec(shape):
    nd = len(shape)
    return pl.BlockSpec(shape, lambda *_: (0,) * nd, pipeline_mode=pl.Buffered(1))


def _inproj_kernel(x_ref, gmix_ref, win_ref, gv_ref, ws_ref, bs_ref, gq_ref, gk_ref, cos_ref, sin_ref,
                   wpa_ref, ma_ref, sgb_ref, q_ref, k_ref, v_ref, u_scr, vn_scr, a_scr):
    tm = x_ref.shape[0]
    h = _rms(x_ref[...], gmix_ref[...]).astype(BF16)

    def proj(lo, width):
        return jnp.dot(h, win_ref[:, lo:lo + width], preferred_element_type=F32)

    vn_scr[...] = _rms(jax.nn.gelu(proj(COL_V, A_WIDTH)), gv_ref[...]).astype(BF16)
    u_scr[...] = jax.nn.gelu(proj(COL_U, A_WIDTH))
    for c in range(tm // CHUNK):
        rows = slice(c * CHUNK, (c + 1) * CHUNK)
        for g in range(A_GROUPS):
            cols = slice(g * LANES, (g + 1) * LANES)
            mixed = jnp.dot(ws_ref[g], vn_scr[rows, cols], preferred_element_type=F32) + bs_ref[g]
            a_scr[rows, cols] = (u_scr[rows, cols] * mixed).astype(BF16)
    ya = jnp.dot(a_scr[...], wpa_ref[...], preferred_element_type=F32)
    ma_ref[...] = jax.nn.sigmoid(proj(COL_GA, D_MODEL)) * ya
    sgb_ref[...] = jax.nn.sigmoid(proj(COL_GB, D_MODEL))

    cos = cos_ref[...]
    sin = sin_ref[...]

    def norm_rope(z, g):
        zn = _rms(z, g)
        return zn * cos + pltpu.roll(zn, HEAD_DIM // 2, 1) * sin

    qz = proj(COL_Q, Q_W)
    for hd in range(N_Q_HEADS):
        cols = slice(hd * HEAD_DIM, (hd + 1) * HEAD_DIM)
        q_ref[:, cols] = norm_rope(qz[:, cols], gq_ref[...]).astype(BF16)
    kz = proj(COL_K, KV_W)
    for hd in range(N_KV_HEADS):
        cols = slice(hd * HEAD_DIM, (hd + 1) * HEAD_DIM)
        k_ref[:, cols] = norm_rope(kz[:, cols], gk_ref[...]).astype(BF16)
    v_ref[...] = proj(COL_VA, KV_W).astype(BF16)


def _inproj(x, p, cos, sin, seq):
    t = x.shape[0]
    tm = TM_IN
    n_pos = seq // tm
    row = lambda w: pl.BlockSpec((tm, w), lambda i: (i, 0))
    pos = pl.BlockSpec((tm, HEAD_DIM), lambda i: (i % n_pos, 0))
    return pl.pallas_call(
        _inproj_kernel,
        grid=(t // tm,),
        in_specs=[row(D_MODEL), _const_spec((1, D_MODEL)), _const_spec((D_MODEL, IN_W)),
                  _const_spec((1, A_WIDTH)), _const_spec((A_GROUPS, CHUNK, CHUNK)),
                  _const_spec((A_GROUPS, CHUNK, LANES)), _const_spec((1, HEAD_DIM)),
                  _const_spec((1, HEAD_DIM)), pos, pos, _const_spec((A_WIDTH, D_MODEL))],
        out_specs=[row(D_MODEL), row(D_MODEL), row(Q_W), row(KV_W), row(KV_W)],
        out_shape=[jax.ShapeDtypeStruct((t, D_MODEL), F32), jax.ShapeDtypeStruct((t, D_MODEL), F32),
                   jax.ShapeDtypeStruct((t, Q_W), BF16), jax.ShapeDtypeStruct((t, KV_W), BF16),
                   jax.ShapeDtypeStruct((t, KV_W), BF16)],
        scratch_shapes=[pltpu.VMEM((tm, A_WIDTH), F32), pltpu.VMEM((tm, A_WIDTH), BF16),
                        pltpu.VMEM((tm, A_WIDTH), BF16)],
        compiler_params=pltpu.CompilerParams(dimension_semantics=("parallel",),
                                             vmem_limit_bytes=VMEM_LIMIT),
        name="inproj",
    )(x, p["norm_mix_g"], p["w_in"], p["norm_v_g"], p["w_spatial"], p["b_spatial"], p["q_norm_g"],
      p["k_norm_g"], cos, sin, p["w_proj_a"])


def _attn_kernel(sink_ref, x_ref, ma_ref, sgb_ref, q_ref, kp_ref, kc_ref, kn_ref, vp_ref, vc_ref, vn_ref,
                 wpb_ref, wout_ref, gffn_ref, wr_ref, br_ref,
                 x1_ref, hn_ref, route_ref, gate_ref, counts_ref, kcat, vcat, o_scr, s_scr, p_scr, sink_scr, m_scr,
                 *, tiles_per_seq, n_tiles):
    tq = x_ref.shape[0]
    blk = WINDOW
    i = pl.program_id(0)
    slot = i % 2

    @pl.when(i == 0)
    def _():
        o_scr[...] = jnp.zeros_like(o_scr)
        counts_ref[...] = jnp.zeros_like(counts_ref)

    pos_tile = jnp.minimum(i, n_tiles - 1) % tiles_per_seq
    has_prev = pos_tile > 0
    has_next = pos_tile < tiles_per_seq - 1

    kcat[0:blk] = kp_ref[...]
    kcat[blk:blk + tq] = kc_ref[...]
    kcat[blk + tq:] = kn_ref[...]
    vcat[0:blk] = vp_ref[...]
    vcat[blk:blk + tq] = vc_ref[...]
    vcat[blk + tq:] = vn_ref[...]

    qr = lax.broadcasted_iota(I32, (blk, blk), 0)
    kc = lax.broadcasted_iota(I32, (blk, blk), 1)
    scale = HEAD_DIM ** -0.5
    n_sub = tq // blk
    pairs = [(j, g) for j in range(n_sub) for g in range(N_KV_HEADS)]

    def keys(ref, j, g):
        return ref[j * blk:(j + 3) * blk, g * HEAD_DIM:(g + 1) * HEAD_DIM]

    def head_cols(g, r):
        hd = g * REP + r
        return slice(hd * HEAD_DIM, (hd + 1) * HEAD_DIM)

    for b, (j, g) in enumerate(pairs):
        rows = slice(j * blk, (j + 1) * blk)
        qs = jnp.concatenate([q_ref[rows, head_cols(g, r)] for r in range(REP)], axis=0)
        s_scr[b] = lax.dot_general(qs, keys(kcat, j, g), (((1,), (1,)), ((), ())),
                                   preferred_element_type=F32)
    log2e = 1.4426950408889634

    def softmax(b, r):
        j, g = pairs[b]
        hrows = slice(r * blk, (r + 1) * blk)
        z = s_scr[b, hrows, :] * (scale * log2e)
        lo_ok = kc >= (qr + jnp.where(has_prev, 0, blk) if j == 0 else qr)
        hi_ok = kc <= (qr - jnp.where(has_next, 0, blk) if j == n_sub - 1 else qr)
        z = jnp.concatenate([jnp.where(lo_ok, z[:, :blk], NEG), z[:, blk:2 * blk],
                             jnp.where(hi_ok, z[:, 2 * blk:], NEG)], axis=1)
        sink = sink_ref[g * REP + r] * log2e
        m = jnp.maximum(jnp.max(z, axis=-1, keepdims=True), sink)
        p_scr[b, hrows, :] = jnp.exp2(z - m).astype(BF16)
        sink_scr[b, hrows, :] = jnp.broadcast_to(jnp.exp2(sink - m), (blk, LANES))

    def values(b):
        j, g = pairs[b]
        rows = slice(j * blk, (j + 1) * blk)
        v_ext = jnp.concatenate([keys(vcat, j, g), jnp.ones((3 * blk, HEAD_DIM), BF16)], axis=1)
        acc = jnp.dot(p_scr[b], v_ext, preferred_element_type=F32)
        o = (acc[:, :HEAD_DIM] / (acc[:, HEAD_DIM:] + sink_scr[b])).astype(BF16)
        for r in range(REP):
            o_scr[slot, rows, head_cols(g, r)] = o[r * blk:(r + 1) * blk, :]

    def merged_cols(cols):
        yb = jnp.dot(o_scr[1 - slot], wpb_ref[:, cols], preferred_element_type=F32)
        m_scr[:, cols] = (ma_ref[:, cols] + sgb_ref[:, cols] * yb).astype(BF16)

    def x1_cols(cols):
        x1_ref[:, cols] = x_ref[:, cols] + jnp.dot(m_scr[...], wout_ref[:, cols], preferred_element_type=F32)

    col_chunks = [slice(c * DENSE_COLS, (c + 1) * DENSE_COLS) for c in range(D_MODEL // DENSE_COLS)]
    dense = [functools.partial(f, cols) for f in (merged_cols, x1_cols) for cols in col_chunks]
    units = [(b, r) for b in range(len(pairs)) for r in range(REP)]
    units_per_dense = len(units) // len(dense)
    for n, (b, r) in enumerate(units):
        softmax(b, r)
        if (n + 1) % units_per_dense == 0:
            dense[(n + 1) // units_per_dense - 1]()
        if r == REP - 1:
            values(b)

    hn = _rms(x1_ref[...], gffn_ref[...])
    _store_rows3(hn_ref, hn)
    def wide(a):
        return jnp.concatenate([a] * (tq // LANES), axis=1)

    logits = lax.dot_general(wr_ref[...], hn.astype(BF16), (((1,), (1,)), ((), ())),
                             preferred_element_type=F32) + wide(br_ref[...])
    sub = lax.broadcasted_iota(I32, (SUBLANES, tq), 0).astype(F32)
    ninf = -jnp.inf

    def cmax(a):
        return jnp.max(a, axis=0, keepdims=True)

    def csum(a):
        return jnp.sum(a, axis=0, keepdims=True)

    def first_row(mask):
        return jnp.min(jnp.where(mask, sub, float(SUBLANES)), axis=0, keepdims=True)

    def group_rows(g):
        return logits[(g + 1) * SUBLANES:(g + 2) * SUBLANES]

    gl = jnp.where(sub < N_GROUPS, logits[0:SUBLANES], ninf)
    gmax = cmax(gl)
    g_sel = first_row(gl == gmax)
    g_p = 1.0 / csum(jnp.exp(gl - gmax))
    el = group_rows(0)
    for g in range(1, N_GROUPS):
        el = jnp.where(g_sel == g, group_rows(g), el)
    ee = jnp.exp(el - cmax(el))
    eprob = ee / csum(ee)
    p1 = cmax(eprob)
    i1 = first_row(eprob == p1)
    eprob2 = jnp.where(sub == i1, -1.0, eprob)
    p2 = cmax(eprob2)
    i2 = first_row(eprob2 == p2)
    psum = p1 + p2
    w1 = g_p * p1 / psum
    w2 = g_p * p2 / psum
    e1 = g_sel * EXPERTS_PER_GROUP + i1
    e2 = g_sel * EXPERTS_PER_GROUP + i2

    erow = lax.broadcasted_iota(I32, (N_EXPERTS, tq), 0).astype(F32)
    oh1 = erow == e1
    oh2 = erow == e2
    cnt = (jnp.where(oh1, 1.0, 0.0) + jnp.where(oh2, 1.0, 0.0)) * jnp.where(i > 0, 1.0, 0.0)
    ri = lax.broadcasted_iota(I32, (tq, tq), 0)
    ci = lax.broadcasted_iota(I32, (tq, tq), 1)
    earlier = jnp.where(ri < ci, 1.0, 0.0).astype(BF16)
    base = wide(counts_ref[...]) + jnp.dot(cnt.astype(BF16), earlier, preferred_element_type=F32)
    r1 = csum(jnp.where(oh1, base, 0.0))
    r2 = csum(jnp.where(oh2, base, 0.0))
    counts_ref[...] = counts_ref[...] + jnp.sum(cnt, axis=1, keepdims=True)

    route = jnp.where(sub == 0.0, e1, jnp.where(sub == 1.0, e2,
                      jnp.where(sub == 2.0, r1, jnp.where(sub == 3.0, r2, 0.0))))
    route_ref[...] = route.astype(I32)
    gates = jnp.where(sub == 0.0, w1, jnp.where(sub == 1.0, w2, 0.0))
    gate_ref[...] = jnp.concatenate([gates, jnp.zeros((LANES - SUBLANES, tq), F32)], axis=0).T


def _attn(x, ma, sgb, q, k, v, p, seq):
    t = x.shape[0]
    tq = TQ
    sub = tq // WINDOW
    last_blk = t // WINDOW - 1
    n_tiles = t // tq
    att = lambda i: jnp.minimum(i, n_tiles - 1)
    post = lambda i: jnp.maximum(i - 1, 0)
    att_row = lambda w: pl.BlockSpec((tq, w), lambda i: (att(i), 0))
    row = lambda w: pl.BlockSpec((tq, w), lambda i: (post(i), 0))
    prev = pl.BlockSpec((WINDOW, KV_W), lambda i: (jnp.maximum(att(i) * sub - 1, 0), 0))
    nxt = pl.BlockSpec((WINDOW, KV_W), lambda i: (jnp.minimum((att(i) + 1) * sub, last_blk), 0))
    return pl.pallas_call(
        functools.partial(_attn_kernel, tiles_per_seq=seq // tq, n_tiles=n_tiles),
        grid=(n_tiles + 1,),
        in_specs=[pl.BlockSpec(memory_space=pltpu.SMEM),
                  row(D_MODEL), row(D_MODEL), row(D_MODEL), att_row(Q_W),
                  prev, att_row(KV_W), nxt, prev, att_row(KV_W), nxt,
                  _const_spec((Q_W, D_MODEL)), _const_spec((D_MODEL, D_MODEL)), _const_spec((1, D_MODEL)),
                  _const_spec((ROUTER_ROWS, D_MODEL)), _const_spec((ROUTER_ROWS, LANES))],
        out_specs=[row(D_MODEL), _rows3_spec(tq, lambda i: (post(i), 0)),
                   pl.BlockSpec((SUBLANES, tq), lambda i: (0, post(i))), row(LANES),
                   pl.BlockSpec((N_EXPERTS, LANES), lambda i: (0, 0))],
        out_shape=[jax.ShapeDtypeStruct((t, D_MODEL), F32), jax.ShapeDtypeStruct((t * ROW_LINES, LANES), I32),
                   jax.ShapeDtypeStruct((SUBLANES, t), I32), jax.ShapeDtypeStruct((t, LANES), F32),
                   jax.ShapeDtypeStruct((N_EXPERTS, LANES), F32)],
        scratch_shapes=[pltpu.VMEM((tq + 2 * WINDOW, KV_W), BF16), pltpu.VMEM((tq + 2 * WINDOW, KV_W), BF16),
                        pltpu.VMEM((2, tq, Q_W), BF16),
                        pltpu.VMEM((sub * N_KV_HEADS, REP * WINDOW, 3 * WINDOW), F32),
                        pltpu.VMEM((sub * N_KV_HEADS, REP * WINDOW, 3 * WINDOW), BF16),
                        pltpu.VMEM((sub * N_KV_HEADS, REP * WINDOW, LANES), F32),
                        pltpu.VMEM((tq, D_MODEL), BF16)],
        compiler_params=pltpu.CompilerParams(dimension_semantics=("arbitrary",),
                                             vmem_limit_bytes=VMEM_LIMIT),
        name="attn",
    )(p["sink"], x, ma, sgb, q, k, k, k, v, v, v, p["w_proj_b"], p["w_out"], p["norm_ffn_g"],
      p["w_router"], p["b_router"])


def _dest_kernel(route_ref, starts_ref, dest_ref):
    route = route_ref[...].astype(F32)
    td = route.shape[1]
    sub = lax.broadcasted_iota(I32, route.shape, 0)
    erow = lax.broadcasted_iota(I32, (N_EXPERTS, td), 0).astype(F32)
    starts = jnp.broadcast_to(starts_ref[:, 0:1], (N_EXPERTS, td))

    def slot(k):
        start = jnp.sum(jnp.where(erow == route[k:k + 1], starts, 0.0), axis=0, keepdims=True)
        return start + route[TOP_K + k:TOP_K + k + 1]

    dest_ref[...] = jnp.where(sub == 0, slot(0), jnp.where(sub == 1, slot(1), 0.0)).astype(I32)


def _dest(route, pad_starts):
    t = route.shape[1]
    td = min(TM_DEST, t)
    return pl.pallas_call(
        _dest_kernel,
        grid=(t // td,),
        in_specs=[pl.BlockSpec((SUBLANES, td), lambda i: (0, i)), _const_spec((N_EXPERTS, LANES))],
        out_specs=pl.BlockSpec((SUBLANES, td), lambda i: (0, i)),
        out_shape=jax.ShapeDtypeStruct((SUBLANES, t), I32),
        compiler_params=pltpu.CompilerParams(dimension_semantics=("parallel",)),
        name="dest",
    )(route, pad_starts)


def _sc_mesh():
    return plsc.VectorSubcoreMesh(core_axis_name="c", subcore_axis_name="s")


def _sc_worker():
    return lax.axis_index("s") * SC_CORES + lax.axis_index("c")


def _sc_chunk(t):
    return min(SC_CHUNK_MAX, t // (SC_WORKERS * SUBLANES))


def _dispatch(hn, dests, n_rows):
    t = hn.shape[0]
    chunk = dests[0].shape[1]
    per_worker = t // SC_WORKERS
    n_chunks = per_worker // chunk
    idx = pltpu.VMEM((n_chunks, chunk), I32)

    @functools.partial(
        pl.kernel, mesh=_sc_mesh(), out_type=jax.ShapeDtypeStruct((n_rows,) + ROW3, I32),
        scratch_types=[idx, idx, pltpu.VMEM((chunk,) + ROW3, I32), pltpu.SemaphoreType.DMA])
    def scatter_rows(hn_hbm, d0_hbm, d1_hbm, xs_hbm, i0_v, i1_v, rows_v, sem):
        w = _sc_worker()
        pltpu.sync_copy(d0_hbm.at[pl.ds(w * n_chunks, n_chunks)], i0_v)
        pltpu.sync_copy(d1_hbm.at[pl.ds(w * n_chunks, n_chunks)], i1_v)

        @pl.loop(0, n_chunks)
        def _(j):
            pltpu.sync_copy(hn_hbm.at[pl.ds(w * per_worker + j * chunk, chunk)], rows_v)
            copies = [pltpu.make_async_copy(rows_v, xs_hbm.at[i_v.at[j]], sem) for i_v in (i0_v, i1_v)]
            for cp in copies:
                cp.start()
            for cp in copies:
                cp.wait()

    return scatter_rows(hn, *dests)


def _moe_block_rows(t):
    rows = MOE_ROWS_MAX
    while rows > CHUNK and (t * TOP_K) // N_EXPERTS < MOE_MIN_BLOCKS * rows:
        rows //= 2
    return rows


def _moe_kernel(be_ref, nvalid_ref, nused_ref, xs_ref, wg_ref, wu_ref, wd_ref, yb_ref, *w16_refs):
    i = pl.program_id(0)
    used = i < nused_ref[0]
    weights = w16_refs if w16_refs else (wg_ref, wu_ref, wd_ref)

    @pl.when(jnp.logical_not(used))
    def _():
        yb_ref[...] = jnp.zeros_like(yb_ref)

    @pl.when(used)
    def _():
        if w16_refs:
            @pl.when(jnp.logical_or(i == 0, be_ref[i] != be_ref[jnp.maximum(i - 1, 0)]))
            def _():
                for src, dst in zip((wg_ref, wu_ref, wd_ref), w16_refs):
                    dst[0] = src[0].astype(BF16)

        row = lax.broadcasted_iota(I32, (xs_ref.shape[0] // ROW_LINES, 1), 0)
        x = jnp.where(row < nvalid_ref[i], _load_rows3(xs_ref), 0.0).astype(BF16)
        gate = jnp.dot(x, weights[0][0], preferred_element_type=F32)
        up = jnp.dot(x, weights[1][0], preferred_element_type=F32)
        hid = (jax.nn.silu(gate) * up).astype(BF16)
        _store_rows3(yb_ref, jnp.dot(hid, weights[2][0], preferred_element_type=F32))


def _moe(block_e, n_valid, n_used, xs, experts, block_rows):
    n_blocks = xs.shape[0] // (block_rows * ROW_LINES)
    cast = experts[0].dtype != BF16

    def rows(i, be, nv, nu):
        return (jnp.minimum(i, nu[0] - 1), 0)

    def expert(i, be, nv, nu):
        return (be[jnp.minimum(i, nu[0] - 1)], 0, 0)

    w_specs = [pl.BlockSpec((1,) + w.shape[1:], expert) for w in experts]
    out_specs = [_rows3_spec(block_rows, lambda i, be, nv, nu: (i, 0))]
    out_shape = [jax.ShapeDtypeStruct(xs.shape, I32)]
    if cast:
        out_specs += w_specs
        out_shape += [jax.ShapeDtypeStruct(w.shape, BF16) for w in experts]
    yb, *w16 = pl.pallas_call(
        _moe_kernel,
        grid_spec=pltpu.PrefetchScalarGridSpec(
            num_scalar_prefetch=3,
            grid=(n_blocks,),
            in_specs=[_rows3_spec(block_rows, rows)] + w_specs,
            out_specs=out_specs,
        ),
        out_shape=out_shape,
        compiler_params=pltpu.CompilerParams(dimension_semantics=("arbitrary",),
                                             vmem_limit_bytes=VMEM_LIMIT),
        name="moe",
    )(block_e, n_valid, n_used, xs, *experts)
    return yb, (tuple(w16) if cast else experts)


def _gather(yb, dests):
    chunk = dests[0].shape[1]
    t = dests[0].shape[0] * chunk
    per_worker = t // SC_WORKERS
    n_chunks = per_worker // chunk
    idx = pltpu.VMEM((n_chunks, chunk), I32)
    out = jax.ShapeDtypeStruct((t,) + ROW3, I32)

    @functools.partial(
        pl.kernel, mesh=_sc_mesh(), out_type=(out, out),
        scratch_types=[idx, idx, pltpu.VMEM((chunk,) + ROW3, I32), pltpu.SemaphoreType.DMA])
    def gather_rows(yb_hbm, d0_hbm, d1_hbm, y0_hbm, y1_hbm, i0_v, i1_v, rows_v, sem):
        w = _sc_worker()
        pltpu.sync_copy(d0_hbm.at[pl.ds(w * n_chunks, n_chunks)], i0_v)
        pltpu.sync_copy(d1_hbm.at[pl.ds(w * n_chunks, n_chunks)], i1_v)

        @pl.loop(0, n_chunks)
        def _(j):
            rows = pl.ds(w * per_worker + j * chunk, chunk)
            for i_v, y_hbm in ((i0_v, y0_hbm), (i1_v, y1_hbm)):
                pltpu.async_copy(yb_hbm.at[i_v.at[j]], rows_v, sem).wait()
                pltpu.sync_copy(rows_v, y_hbm.at[rows])

    return gather_rows(yb, *dests)


def _combine_kernel(x1_ref, gate_ref, y0_ref, y1_ref, out_ref):
    gate = gate_ref[...]
    out_ref[...] = x1_ref[...] + (_load_rows3(y0_ref) * gate[:, 0:1] + _load_rows3(y1_ref) * gate[:, 1:2])


def _combine(x1, gate, y0, y1):
    t = x1.shape[0]
    tm = TM_ROW
    return pl.pallas_call(
        _combine_kernel,
        grid=(t // tm,),
        in_specs=[pl.BlockSpec((tm, D_MODEL), lambda i: (i, 0)),
                  pl.BlockSpec((tm, LANES), lambda i: (i, 0)),
                  _rows3_spec(tm, lambda i: (i, 0)), _rows3_spec(tm, lambda i: (i, 0))],
        out_specs=pl.BlockSpec((tm, D_MODEL), lambda i: (i, 0)),
        out_shape=jax.ShapeDtypeStruct((t, D_MODEL), F32),
        compiler_params=pltpu.CompilerParams(dimension_semantics=("parallel",)),
        name="combine",
    )(x1, gate, y0, y1)


def _rope_tables(seq):
    half = HEAD_DIM // 2
    inv_freq = ROPE_THETA ** (-jnp.arange(half, dtype=F32) / half)
    ang = jnp.arange(seq).astype(F32)[:, None] * inv_freq[None, :]
    cos = jnp.cos(ang)
    sin = jnp.sin(ang)
    return jnp.concatenate([cos, cos], axis=-1), jnp.concatenate([-sin, sin], axis=-1)


def _layer(x, p, seq, experts):
    t = x.shape[0]
    cos, sin = _rope_tables(seq)
    ma, sgb, q, k, v = _inproj(x, p, cos, sin, seq)
    x1, hn, route, gate, counts_f = _attn(x, ma, sgb, q, k, v, p, seq)

    block_rows = _moe_block_rows(t)
    counts = counts_f[:, 0].astype(I32)
    min_blocks = 1 if experts[0].dtype != BF16 else 0
    padded = jnp.maximum((counts + block_rows - 1) // block_rows, min_blocks) * block_rows
    pad_ends = jnp.cumsum(padded)
    pad_starts = pad_ends - padded
    n_blocks = (t * TOP_K) // block_rows + N_EXPERTS
    block_start = jnp.arange(n_blocks, dtype=I32) * block_rows
    in_expert = jnp.logical_and(block_start[:, None] >= pad_starts[None, :],
                                block_start[:, None] < pad_ends[None, :]).astype(I32)
    block_e = jnp.minimum(jnp.sum((block_start[:, None] >= pad_ends[None, :]).astype(I32), axis=1), N_EXPERTS - 1)
    n_valid = jnp.sum(in_expert * jnp.clip(pad_starts + counts - block_start[:, None], 0, block_rows), axis=1)
    n_used = pad_ends[-1:] // block_rows
    starts_col = jnp.broadcast_to(pad_starts.astype(F32)[:, None], (N_EXPERTS, LANES))

    dest = _dest(route, starts_col)
    dests = [dest[k].reshape(t // _sc_chunk(t), _sc_chunk(t)) for k in range(TOP_K)]
    xs = _dispatch(_as_rows3(hn), dests, n_blocks * block_rows)
    yb, experts = _moe(block_e, n_valid, n_used, _as_lines(xs), experts, block_rows)
    y0, y1 = _gather(_as_rows3(yb), dests)
    return _combine(x1, gate, _as_lines(y0), _as_lines(y1)), experts


def kernel(x_prompt, x_sample, norm_mix_g, w_in, norm_v_g, w_spatial, b_spatial, q_norm_g, k_norm_g, sink,
           w_proj_a, w_proj_b, w_out, norm_ffn_g, w_router_group, b_router_group, w_router_expert,
           b_router_expert, w_gate_e, w_up_e, w_down_e):
    depth = w_in.shape[0]
    layers = []
    for l in range(depth):
        w_router = jnp.zeros((ROUTER_ROWS, D_MODEL), F32)
        w_router = w_router.at[:N_GROUPS].set(w_router_group[l].T)
        w_router = w_router.at[SUBLANES:SUBLANES + N_EXPERTS].set(w_router_expert[l].T)
        b_router = jnp.zeros((ROUTER_ROWS,), F32)
        b_router = b_router.at[:N_GROUPS].set(b_router_group[l])
        b_router = b_router.at[SUBLANES:SUBLANES + N_EXPERTS].set(b_router_expert[l])
        b_router = jnp.broadcast_to(b_router[:, None], (ROUTER_ROWS, LANES))
        layers.append(dict(
            norm_mix_g=norm_mix_g[l][None], w_in=w_in[l].astype(BF16), norm_v_g=norm_v_g[l][None],
            w_spatial=w_spatial[l].astype(BF16),
            b_spatial=jnp.broadcast_to(b_spatial[l][:, :, None], (A_GROUPS, CHUNK, LANES)),
            q_norm_g=q_norm_g[l][None], k_norm_g=k_norm_g[l][None], sink=sink[l],
            w_proj_a=w_proj_a[l].astype(BF16), w_proj_b=w_proj_b[l].astype(BF16), w_out=w_out[l].astype(BF16),
            norm_ffn_g=norm_ffn_g[l][None], w_router=w_router.astype(BF16), b_router=b_router,
            experts=(w_gate_e[l], w_up_e[l], w_down_e[l])))

    trunks = [x_prompt, x_sample]
    rows = [x.reshape(-1, D_MODEL) for x in trunks]
    order = sorted(range(len(trunks)), key=lambda n: -rows[n].shape[0])
    for p in layers:
        experts = p["experts"]
        for n in order:
            rows[n], experts = _layer(rows[n], p, trunks[n].shape[1], experts)
    return tuple(r.reshape(x.shape) for r, x in zip(rows, trunks))
```

```python
import functools

import jax
import jax.numpy as jnp
from jax import lax
from jax.experimental import pallas as pl
from jax.experimental.pallas import tpu as pltpu
from jax.experimental.pallas import tpu_sc as plsc

F32 = jnp.float32
BF16 = jnp.bfloat16
I32 = jnp.int32
U32 = jnp.uint32

LANES = 128
SUBLANES = 8
VMEM_BYTES_V7X = 64 * 1024 * 1024
SC_CORES = 2
SC_SUBCORES = 16
SC_WORKERS = SC_CORES * SC_SUBCORES
SC_CHUNK_MAX = 128

D_MODEL = 1024
A_WIDTH = D_MODEL
A_GROUPS = 8
CHUNK = 128
HEAD_DIM = 128
N_Q_HEADS = D_MODEL // HEAD_DIM
N_KV_HEADS = 2
REP = N_Q_HEADS // N_KV_HEADS
WINDOW = 128
ROPE_THETA = 10000.0
Q_W = N_Q_HEADS * HEAD_DIM
KV_W = N_KV_HEADS * HEAD_DIM
IN_W = 2 * A_WIDTH + Q_W + 2 * KV_W + 2 * D_MODEL
COL_U = 0
COL_V = COL_U + A_WIDTH
COL_Q = COL_V + A_WIDTH
COL_K = COL_Q + Q_W
COL_VA = COL_K + KV_W
COL_GA = COL_VA + KV_W
COL_GB = COL_GA + D_MODEL
N_GROUPS = 4
EXPERTS_PER_GROUP = 8
N_EXPERTS = N_GROUPS * EXPERTS_PER_GROUP
TOP_K = 2
D_EXPERT = 512
EPS = 1e-6
NEG = -1e30

TM_IN = 512
TM_IN_SUB = 256
TQ = 512
DENSE_COLS = 256
ROUTER_ROWS = 64
assert EXPERTS_PER_GROUP == SUBLANES and SUBLANES + N_EXPERTS <= ROUTER_ROWS
TM_ROW = 512
TM_DEST = 2048
MOE_ROWS_MAX = 1024
MOE_MIN_BLOCKS = 4
VMEM_LIMIT = 56 * 1024 * 1024
assert VMEM_LIMIT < VMEM_BYTES_V7X


def _rms(x, g):
    return x * lax.rsqrt(jnp.mean(x * x, axis=-1, keepdims=True) + EPS) * g


ROW_LINES = D_MODEL // 2 // LANES
ROW3 = (ROW_LINES, LANES)
HIGH_HALF = 0xFFFF0000


def _as_rows3(a):
    return a.reshape((a.shape[0] // ROW_LINES,) + ROW3)


def _as_lines(a):
    return a.reshape((a.shape[0] * ROW_LINES, LANES))


def _store_rows3(lines_ref, val):
    rows = val.shape[0]
    bits = lax.bitcast_convert_type(val.astype(BF16).astype(F32), U32)
    half = ROW_LINES * LANES
    for s in range(ROW_LINES):
        lo = bits[:, s * LANES:(s + 1) * LANES] >> 16
        hi = bits[:, half + s * LANES:half + (s + 1) * LANES] & U32(HIGH_HALF)
        lines_ref[pl.ds(s, rows, stride=ROW_LINES), :] = lax.bitcast_convert_type(lo | hi, I32)


def _load_rows3(lines_ref):
    rows = lines_ref.shape[0] // ROW_LINES
    words = [lax.bitcast_convert_type(lines_ref[pl.ds(s, rows, stride=ROW_LINES), :], U32)
             for s in range(ROW_LINES)]
    lo = [lax.bitcast_convert_type(w << 16, F32) for w in words]
    hi = [lax.bitcast_convert_type(w & U32(HIGH_HALF), F32) for w in words]
    return jnp.concatenate(lo + hi, axis=1)


def _rows3_spec(rows, index_map):
    return pl.BlockSpec((rows * ROW_LINES, LANES), index_map)


def _const_spec(shape):
    nd = len(shape)
    return pl.BlockSpec(shape, lambda *_: (0,) * nd, pipeline_mode=pl.Buffered(1))


def _inproj_kernel(x_ref, gmix_ref, win_ref, gv_ref, ws_ref, bs_ref, gq_ref, gk_ref, cos_ref, sin_ref,
                   wpa_ref, ma_ref, sgb_ref, q_ref, k_ref, v_ref, h_scr, u_scr, vn_scr, a_scr):
    tm = x_ref.shape[0]

    def stages(rows):
        def proj(lo, width):
            return jnp.dot(h_scr[rows], win_ref[:, lo:lo + width], preferred_element_type=F32)

        def norm_rope(z, g):
            zn = _rms(z, g)
            return zn * cos_ref[rows] + pltpu.roll(zn, HEAD_DIM // 2, 1) * sin_ref[rows]

        def norm():
            h_scr[rows] = _rms(x_ref[rows], gmix_ref[...]).astype(BF16)

        def mix_v():
            vn_scr[rows] = _rms(jax.nn.gelu(proj(COL_V, A_WIDTH)), gv_ref[...]).astype(BF16)

        def mix_u():
            u_scr[rows] = jax.nn.gelu(proj(COL_U, A_WIDTH))

        def spatial():
            for c in range(rows.start, rows.stop, CHUNK):
                chunk = slice(c, c + CHUNK)
                for g in range(A_GROUPS):
                    cols = slice(g * LANES, (g + 1) * LANES)
                    mixed = jnp.dot(ws_ref[g], vn_scr[chunk, cols], preferred_element_type=F32) + bs_ref[g]
                    a_scr[chunk, cols] = (u_scr[chunk, cols] * mixed).astype(BF16)

        def gate_a():
            ya = jnp.dot(a_scr[rows], wpa_ref[...], preferred_element_type=F32)
            ma_ref[rows] = jax.nn.sigmoid(proj(COL_GA, D_MODEL)) * ya

        def gate_b():
            sgb_ref[rows] = jax.nn.sigmoid(proj(COL_GB, D_MODEL))

        def queries():
            qz = proj(COL_Q, Q_W)
            for hd in range(N_Q_HEADS):
                cols = slice(hd * HEAD_DIM, (hd + 1) * HEAD_DIM)
                q_ref[rows, cols] = norm_rope(qz[:, cols], gq_ref[...]).astype(BF16)

        def keys_values():
            kz = proj(COL_K, KV_W)
            for hd in range(N_KV_HEADS):
                cols = slice(hd * HEAD_DIM, (hd + 1) * HEAD_DIM)
                k_ref[rows, cols] = norm_rope(kz[:, cols], gk_ref[...]).astype(BF16)
            v_ref[rows] = proj(COL_VA, KV_W).astype(BF16)

        return [norm, mix_v, mix_u, spatial, gate_a, gate_b, queries, keys_values]

    subs = [stages(slice(r, r + TM_IN_SUB)) for r in range(0, tm, TM_IN_SUB)]
    n_stage = len(subs[0])
    for step in range(n_stage + len(subs) - 1):
        for n, sub in enumerate(subs):
            if 0 <= step - n < n_stage:
                sub[step - n]()


def _inproj(x, p, cos, sin, seq):
    t = x.shape[0]
    tm = TM_IN
    n_pos = seq // tm
    row = lambda w: pl.BlockSpec((tm, w), lambda i: (i, 0))
    pos = pl.BlockSpec((tm, HEAD_DIM), lambda i: (i % n_pos, 0))
    return pl.pallas_call(
        _inproj_kernel,
        grid=(t // tm,),
        in_specs=[row(D_MODEL), _const_spec((1, D_MODEL)), _const_spec((D_MODEL, IN_W)),
                  _const_spec((1, A_WIDTH)), _const_spec((A_GROUPS, CHUNK, CHUNK)),
                  _const_spec((A_GROUPS, CHUNK, LANES)), _const_spec((1, HEAD_DIM)),
                  _const_spec((1, HEAD_DIM)), pos, pos, _const_spec((A_WIDTH, D_MODEL))],
        out_specs=[row(D_MODEL), row(D_MODEL), row(Q_W), row(KV_W), row(KV_W)],
        out_shape=[jax.ShapeDtypeStruct((t, D_MODEL), F32), jax.ShapeDtypeStruct((t, D_MODEL), F32),
                   jax.ShapeDtypeStruct((t, Q_W), BF16), jax.ShapeDtypeStruct((t, KV_W), BF16),
                   jax.ShapeDtypeStruct((t, KV_W), BF16)],
        scratch_shapes=[pltpu.VMEM((tm, D_MODEL), BF16), pltpu.VMEM((tm, A_WIDTH), F32),
                        pltpu.VMEM((tm, A_WIDTH), BF16), pltpu.VMEM((tm, A_WIDTH), BF16)],
        compiler_params=pltpu.CompilerParams(dimension_semantics=("parallel",),
                                             vmem_limit_bytes=VMEM_LIMIT),
        name="inproj",
    )(x, p["norm_mix_g"], p["w_in"], p["norm_v_g"], p["w_spatial"], p["b_spatial"], p["q_norm_g"],
      p["k_norm_g"], cos, sin, p["w_proj_a"])


def _attn_kernel(sink_ref, x_ref, ma_ref, sgb_ref, q_ref, kp_ref, kc_ref, kn_ref, vp_ref, vc_ref, vn_ref,
                 wpb_ref, wout_ref, gffn_ref, wr_ref, br_ref,
                 x1_ref, hn_ref, route_ref, gate_ref, counts_ref, kcat, vcat, o_scr, s_scr, p_scr, sink_scr, m_scr,
                 *, tiles_per_seq, n_tiles):
    tq = x_ref.shape[0]
    blk = WINDOW
    i = pl.program_id(0)
    slot = i % 2

    @pl.when(i == 0)
    def _():
        o_scr[...] = jnp.zeros_like(o_scr)
        counts_ref[...] = jnp.zeros_like(counts_ref)

    pos_tile = jnp.minimum(i, n_tiles - 1) % tiles_per_seq
    has_prev = pos_tile > 0
    has_next = pos_tile < tiles_per_seq - 1

    kcat[0:blk] = kp_ref[...]
    kcat[blk:blk + tq] = kc_ref[...]
    kcat[blk + tq:] = kn_ref[...]
    vcat[0:blk] = vp_ref[...]
    vcat[blk:blk + tq] = vc_ref[...]
    vcat[blk + tq:] = vn_ref[...]

    qr = lax.broadcasted_iota(I32, (blk, blk), 0)
    kc = lax.broadcasted_iota(I32, (blk, blk), 1)
    scale = HEAD_DIM ** -0.5
    n_sub = tq // blk
    pairs = [(j, g) for j in range(n_sub) for g in range(N_KV_HEADS)]

    def keys(ref, j, g):
        return ref[j * blk:(j + 3) * blk, g * HEAD_DIM:(g + 1) * HEAD_DIM]

    def head_cols(g, r):
        hd = g * REP + r
        return slice(hd * HEAD_DIM, (hd + 1) * HEAD_DIM)

    for b, (j, g) in enumerate(pairs):
        rows = slice(j * blk, (j + 1) * blk)
        qs = jnp.concatenate([q_ref[rows, head_cols(g, r)] for r in range(REP)], axis=0)
        s_scr[b] = lax.dot_general(qs, keys(kcat, j, g), (((1,), (1,)), ((), ())),
                                   preferred_element_type=F32)
    log2e = 1.4426950408889634

    def softmax(b, r):
        j, g = pairs[b]
        hrows = slice(r * blk, (r + 1) * blk)
        z = s_scr[b, hrows, :] * (scale * log2e)
        lo_ok = kc >= (qr + jnp.where(has_prev, 0, blk) if j == 0 else qr)
        hi_ok = kc <= (qr - jnp.where(has_next, 0, blk) if j == n_sub - 1 else qr)
        z = jnp.concatenate([jnp.where(lo_ok, z[:, :blk], NEG), z[:, blk:2 * blk],
                             jnp.where(hi_ok, z[:, 2 * blk:], NEG)], axis=1)
        sink = sink_ref[g * REP + r] * log2e
        m = jnp.maximum(jnp.max(z, axis=-1, keepdims=True), sink)
        p_scr[b, hrows, :] = jnp.exp2(z - m).astype(BF16)
        sink_scr[b, hrows, :] = jnp.broadcast_to(jnp.exp2(sink - m), (blk, LANES))

    def values(b):
        j, g = pairs[b]
        rows = slice(j * blk, (j + 1) * blk)
        v_ext = jnp.concatenate([keys(vcat, j, g), jnp.ones((3 * blk, HEAD_DIM), BF16)], axis=1)
        acc = jnp.dot(p_scr[b], v_ext, preferred_element_type=F32)
        o = (acc[:, :HEAD_DIM] / (acc[:, HEAD_DIM:] + sink_scr[b])).astype(BF16)
        for r in range(REP):
            o_scr[slot, rows, head_cols(g, r)] = o[r * blk:(r + 1) * blk, :]

    def merged_cols(cols):
        yb = jnp.dot(o_scr[1 - slot], wpb_ref[:, cols], preferred_element_type=F32)
        m_scr[:, cols] = (ma_ref[:, cols] + sgb_ref[:, cols] * yb).astype(BF16)

    def x1_cols(cols):
        x1_ref[:, cols] = x_ref[:, cols] + jnp.dot(m_scr[...], wout_ref[:, cols], preferred_element_type=F32)

    col_chunks = [slice(c * DENSE_COLS, (c + 1) * DENSE_COLS) for c in range(D_MODEL // DENSE_COLS)]
    dense = [functools.partial(f, cols) for f in (merged_cols, x1_cols) for cols in col_chunks]
    units = [(b, r) for b in range(len(pairs)) for r in range(REP)]
    units_per_dense = len(units) // len(dense)
    for n, (b, r) in enumerate(units):
        softmax(b, r)
        if (n + 1) % units_per_dense == 0:
            dense[(n + 1) // units_per_dense - 1]()
        if r == REP - 1:
            values(b)

    hn = _rms(x1_ref[...], gffn_ref[...])
    _store_rows3(hn_ref, hn)
    def wide(a):
        return jnp.concatenate([a] * (tq // LANES), axis=1)

    logits = lax.dot_general(wr_ref[...], hn.astype(BF16), (((1,), (1,)), ((), ())),
                             preferred_element_type=F32) + wide(br_ref[...])
    sub = lax.broadcasted_iota(I32, (SUBLANES, tq), 0).astype(F32)
    ninf = -jnp.inf

    def cmax(a):
        return jnp.max(a, axis=0, keepdims=True)

    def csum(a):
        return jnp.sum(a, axis=0, keepdims=True)

    def first_row(mask):
        return jnp.min(jnp.where(mask, sub, float(SUBLANES)), axis=0, keepdims=True)

    def group_rows(g):
        return logits[(g + 1) * SUBLANES:(g + 2) * SUBLANES]

    gl = jnp.where(sub < N_GROUPS, logits[0:SUBLANES], ninf)
    gmax = cmax(gl)
    g_sel = first_row(gl == gmax)
    g_p = 1.0 / csum(jnp.exp(gl - gmax))
    el = group_rows(0)
    for g in range(1, N_GROUPS):
        el = jnp.where(g_sel == g, group_rows(g), el)
    ee = jnp.exp(el - cmax(el))
    eprob = ee / csum(ee)
    p1 = cmax(eprob)
    i1 = first_row(eprob == p1)
    eprob2 = jnp.where(sub == i1, -1.0, eprob)
    p2 = cmax(eprob2)
    i2 = first_row(eprob2 == p2)
    psum = p1 + p2
    w1 = g_p * p1 / psum
    w2 = g_p * p2 / psum
    e1 = g_sel * EXPERTS_PER_GROUP + i1
    e2 = g_sel * EXPERTS_PER_GROUP + i2

    erow = lax.broadcasted_iota(I32, (N_EXPERTS, tq), 0).astype(F32)
    oh1 = erow == e1
    oh2 = erow == e2
    cnt = (jnp.where(oh1, 1.0, 0.0) + jnp.where(oh2, 1.0, 0.0)) * jnp.where(i > 0, 1.0, 0.0)
    ri = lax.broadcasted_iota(I32, (tq, tq), 0)
    ci = lax.broadcasted_iota(I32, (tq, tq), 1)
    earlier = jnp.where(ri < ci, 1.0, 0.0).astype(BF16)
    base = wide(counts_ref[...]) + jnp.dot(cnt.astype(BF16), earlier, preferred_element_type=F32)
    r1 = csum(jnp.where(oh1, base, 0.0))
    r2 = csum(jnp.where(oh2, base, 0.0))
    counts_ref[...] = counts_ref[...] + jnp.sum(cnt, axis=1, keepdims=True)

    route = jnp.where(sub == 0.0, e1, jnp.where(sub == 1.0, e2,
                      jnp.where(sub == 2.0, r1, jnp.where(sub == 3.0, r2, 0.0))))
    route_ref[...] = route.astype(I32)
    gates = jnp.where(sub == 0.0, w1, jnp.where(sub == 1.0, w2, 0.0))
    gate_ref[...] = jnp.concatenate([gates, jnp.zeros((LANES - SUBLANES, tq), F32)], axis=0).T


def _attn(x, ma, sgb, q, k, v, p, seq):
    t = x.shape[0]
    tq = TQ
    sub = tq // WINDOW
    last_blk = t // WINDOW - 1
    n_tiles = t // tq
    att = lambda i: jnp.minimum(i, n_tiles - 1)
    post = lambda i: jnp.maximum(i - 1, 0)
    att_row = lambda w: pl.BlockSpec((tq, w), lambda i: (att(i), 0))
    row = lambda w: pl.BlockSpec((tq, w), lambda i: (post(i), 0))
    prev = pl.BlockSpec((WINDOW, KV_W), lambda i: (jnp.maximum(att(i) * sub - 1, 0), 0))
    nxt = pl.BlockSpec((WINDOW, KV_W), lambda i: (jnp.minimum((att(i) + 1) * sub, last_blk), 0))
    return pl.pallas_call(
        functools.partial(_attn_kernel, tiles_per_seq=seq // tq, n_tiles=n_tiles),
        grid=(n_tiles + 1,),
        in_specs=[pl.BlockSpec(memory_space=pltpu.SMEM),
                  row(D_MODEL), row(D_MODEL), row(D_MODEL), att_row(Q_W),
                  prev, att_row(KV_W), nxt, prev, att_row(KV_W), nxt,
                  _const_spec((Q_W, D_MODEL)), _const_spec((D_MODEL, D_MODEL)), _const_spec((1, D_MODEL)),
                  _const_spec((ROUTER_ROWS, D_MODEL)), _const_spec((ROUTER_ROWS, LANES))],
        out_specs=[row(D_MODEL), _rows3_spec(tq, lambda i: (post(i), 0)),
                   pl.BlockSpec((SUBLANES, tq), lambda i: (0, post(i))), row(LANES),
                   pl.BlockSpec((N_EXPERTS, LANES), lambda i: (0, 0))],
        out_shape=[jax.ShapeDtypeStruct((t, D_MODEL), F32), jax.ShapeDtypeStruct((t * ROW_LINES, LANES), I32),
                   jax.ShapeDtypeStruct((SUBLANES, t), I32), jax.ShapeDtypeStruct((t, LANES), F32),
                   jax.ShapeDtypeStruct((N_EXPERTS, LANES), F32)],
        scratch_shapes=[pltpu.VMEM((tq + 2 * WINDOW, KV_W), BF16), pltpu.VMEM((tq + 2 * WINDOW, KV_W), BF16),
                        pltpu.VMEM((2, tq, Q_W), BF16),
                        pltpu.VMEM((sub * N_KV_HEADS, REP * WINDOW, 3 * WINDOW), F32),
                        pltpu.VMEM((sub * N_KV_HEADS, REP * WINDOW, 3 * WINDOW), BF16),
                        pltpu.VMEM((sub * N_KV_HEADS, REP * WINDOW, LANES), F32),
                        pltpu.VMEM((tq, D_MODEL), BF16)],
        compiler_params=pltpu.CompilerParams(dimension_semantics=("arbitrary",),
                                             vmem_limit_bytes=VMEM_LIMIT),
        name="attn",
    )(p["sink"], x, ma, sgb, q, k, k, k, v, v, v, p["w_proj_b"], p["w_out"], p["norm_ffn_g"],
      p["w_router"], p["b_router"])


def _dest_kernel(route_ref, starts_ref, dest_ref):
    route = route_ref[...].astype(F32)
    td = route.shape[1]
    sub = lax.broadcasted_iota(I32, route.shape, 0)
    erow = lax.broadcasted_iota(I32, (N_EXPERTS, td), 0).astype(F32)
    starts = jnp.broadcast_to(starts_ref[:, 0:1], (N_EXPERTS, td))

    def slot(k):
        start = jnp.sum(jnp.where(erow == route[k:k + 1], starts, 0.0), axis=0, keepdims=True)
        return start + route[TOP_K + k:TOP_K + k + 1]

    dest_ref[...] = jnp.where(sub == 0, slot(0), jnp.where(sub == 1, slot(1), 0.0)).astype(I32)


def _dest(route, pad_starts):
    t = route.shape[1]
    td = min(TM_DEST, t)
    return pl.pallas_call(
        _dest_kernel,
        grid=(t // td,),
        in_specs=[pl.BlockSpec((SUBLANES, td), lambda i: (0, i)), _const_spec((N_EXPERTS, LANES))],
        out_specs=pl.BlockSpec((SUBLANES, td), lambda i: (0, i)),
        out_shape=jax.ShapeDtypeStruct((SUBLANES, t), I32),
        compiler_params=pltpu.CompilerParams(dimension_semantics=("parallel",)),
        name="dest",
    )(route, pad_starts)


def _sc_mesh():
    return plsc.VectorSubcoreMesh(core_axis_name="c", subcore_axis_name="s")


def _sc_worker():
    return lax.axis_index("s") * SC_CORES + lax.axis_index("c")


def _sc_chunk(t):
    return min(SC_CHUNK_MAX, t // (SC_WORKERS * SUBLANES))


def _dispatch(hn, dests, n_rows):
    t = hn.shape[0]
    chunk = dests[0].shape[1]
    per_worker = t // SC_WORKERS
    n_chunks = per_worker // chunk
    idx = pltpu.VMEM((n_chunks, chunk), I32)

    @functools.partial(
        pl.kernel, mesh=_sc_mesh(), out_type=jax.ShapeDtypeStruct((n_rows,) + ROW3, I32),
        scratch_types=[idx, idx, pltpu.VMEM((chunk,) + ROW3, I32), pltpu.SemaphoreType.DMA])
    def scatter_rows(hn_hbm, d0_hbm, d1_hbm, xs_hbm, i0_v, i1_v, rows_v, sem):
        w = _sc_worker()
        pltpu.sync_copy(d0_hbm.at[pl.ds(w * n_chunks, n_chunks)], i0_v)
        pltpu.sync_copy(d1_hbm.at[pl.ds(w * n_chunks, n_chunks)], i1_v)

        @pl.loop(0, n_chunks)
        def _(j):
            pltpu.sync_copy(hn_hbm.at[pl.ds(w * per_worker + j * chunk, chunk)], rows_v)
            copies = [pltpu.make_async_copy(rows_v, xs_hbm.at[i_v.at[j]], sem) for i_v in (i0_v, i1_v)]
            for cp in copies:
                cp.start()
            for cp in copies:
                cp.wait()

    return scatter_rows(hn, *dests)


def _moe_block_rows(t):
    rows = MOE_ROWS_MAX
    while rows > CHUNK and (t * TOP_K) // N_EXPERTS < MOE_MIN_BLOCKS * rows:
        rows //= 2
    return rows


def _moe_kernel(be_ref, nvalid_ref, nused_ref, xs_ref, wg_ref, wu_ref, wd_ref, yb_ref, *w16_refs):
    i = pl.program_id(0)
    used = i < nused_ref[0]
    weights = w16_refs if w16_refs else (wg_ref, wu_ref, wd_ref)

    @pl.when(jnp.logical_not(used))
    def _():
        yb_ref[...] = jnp.zeros_like(yb_ref)

    @pl.when(used)
    def _():
        if w16_refs:
            @pl.when(jnp.logical_or(i == 0, be_ref[i] != be_ref[jnp.maximum(i - 1, 0)]))
            def _():
                for src, dst in zip((wg_ref, wu_ref, wd_ref), w16_refs):
                    dst[0] = src[0].astype(BF16)

        row = lax.broadcasted_iota(I32, (xs_ref.shape[0] // ROW_LINES, 1), 0)
        x = jnp.where(row < nvalid_ref[i], _load_rows3(xs_ref), 0.0).astype(BF16)
        gate = jnp.dot(x, weights[0][0], preferred_element_type=F32)
        up = jnp.dot(x, weights[1][0], preferred_element_type=F32)
        hid = (jax.nn.silu(gate) * up).astype(BF16)
        _store_rows3(yb_ref, jnp.dot(hid, weights[2][0], preferred_element_type=F32))


def _moe(block_e, n_valid, n_used, xs, experts, block_rows):
    n_blocks = xs.shape[0] // (block_rows * ROW_LINES)
    cast = experts[0].dtype != BF16

    def rows(i, be, nv, nu):
        return (jnp.minimum(i, nu[0] - 1), 0)

    def expert(i, be, nv, nu):
        return (be[jnp.minimum(i, nu[0] - 1)], 0, 0)

    w_specs = [pl.BlockSpec((1,) + w.shape[1:], expert) for w in experts]
    out_specs = [_rows3_spec(block_rows, lambda i, be, nv, nu: (i, 0))]
    out_shape = [jax.ShapeDtypeStruct(xs.shape, I32)]
    if cast:
        out_specs += w_specs
        out_shape += [jax.ShapeDtypeStruct(w.shape, BF16) for w in experts]
    yb, *w16 = pl.pallas_call(
        _moe_kernel,
        grid_spec=pltpu.PrefetchScalarGridSpec(
            num_scalar_prefetch=3,
            grid=(n_blocks,),
            in_specs=[_rows3_spec(block_rows, rows)] + w_specs,
            out_specs=out_specs,
        ),
        out_shape=out_shape,
        compiler_params=pltpu.CompilerParams(dimension_semantics=("arbitrary",),
                                             vmem_limit_bytes=VMEM_LIMIT),
        name="moe",
    )(block_e, n_valid, n_used, xs, *experts)
    return yb, (tuple(w16) if cast else experts)


def _gather(yb, dests):
    chunk = dests[0].shape[1]
    t = dests[0].shape[0] * chunk
    per_worker = t // SC_WORKERS
    n_chunks = per_worker // chunk
    idx = pltpu.VMEM((n_chunks, chunk), I32)
    out = jax.ShapeDtypeStruct((t,) + ROW3, I32)

    @functools.partial(
        pl.kernel, mesh=_sc_mesh(), out_type=(out, out),
        scratch_types=[idx, idx, pltpu.VMEM((chunk,) + ROW3, I32), pltpu.SemaphoreType.DMA])
    def gather_rows(yb_hbm, d0_hbm, d1_hbm, y0_hbm, y1_hbm, i0_v, i1_v, rows_v, sem):
        w = _sc_worker()
        pltpu.sync_copy(d0_hbm.at[pl.ds(w * n_chunks, n_chunks)], i0_v)
        pltpu.sync_copy(d1_hbm.at[pl.ds(w * n_chunks, n_chunks)], i1_v)

        @pl.loop(0, n_chunks)
        def _(j):
            rows = pl.ds(w * per_worker + j * chunk, chunk)
            for i_v, y_hbm in ((i0_v, y0_hbm), (i1_v, y1_hbm)):
                pltpu.async_copy(yb_hbm.at[i_v.at[j]], rows_v, sem).wait()
                pltpu.sync_copy(rows_v, y_hbm.at[rows])

    return gather_rows(yb, *dests)


def _combine_kernel(x1_ref, gate_ref, y0_ref, y1_ref, out_ref):
    gate = gate_ref[...]
    out_ref[...] = x1_ref[...] + (_load_rows3(y0_ref) * gate[:, 0:1] + _load_rows3(y1_ref) * gate[:, 1:2])


def _combine(x1, gate, y0, y1):
    t = x1.shape[0]
    tm = TM_ROW
    return pl.pallas_call(
        _combine_kernel,
        grid=(t // tm,),
        in_specs=[pl.BlockSpec((tm, D_MODEL), lambda i: (i, 0)),
                  pl.BlockSpec((tm, LANES), lambda i: (i, 0)),
                  _rows3_spec(tm, lambda i: (i, 0)), _rows3_spec(tm, lambda i: (i, 0))],
        out_specs=pl.BlockSpec((tm, D_MODEL), lambda i: (i, 0)),
        out_shape=jax.ShapeDtypeStruct((t, D_MODEL), F32),
        compiler_params=pltpu.CompilerParams(dimension_semantics=("parallel",)),
        name="combine",
    )(x1, gate, y0, y1)


def _rope_tables(seq):
    half = HEAD_DIM // 2
    inv_freq = ROPE_THETA ** (-jnp.arange(half, dtype=F32) / half)
    ang = jnp.arange(seq).astype(F32)[:, None] * inv_freq[None, :]
    cos = jnp.cos(ang)
    sin = jnp.sin(ang)
    return jnp.concatenate([cos, cos], axis=-1), jnp.concatenate([-sin, sin], axis=-1)


def _layer(x, p, seq, experts):
    t = x.shape[0]
    cos, sin = _rope_tables(seq)
    ma, sgb, q, k, v = _inproj(x, p, cos, sin, seq)
    x1, hn, route, gate, counts_f = _attn(x, ma, sgb, q, k, v, p, seq)

    block_rows = _moe_block_rows(t)
    counts = counts_f[:, 0].astype(I32)
    min_blocks = 1 if experts[0].dtype != BF16 else 0
    padded = jnp.maximum((counts + block_rows - 1) // block_rows, min_blocks) * block_rows
    pad_ends = jnp.cumsum(padded)
    pad_starts = pad_ends - padded
    n_blocks = (t * TOP_K) // block_rows + N_EXPERTS
    block_start = jnp.arange(n_blocks, dtype=I32) * block_rows
    in_expert = jnp.logical_and(block_start[:, None] >= pad_starts[None, :],
                                block_start[:, None] < pad_ends[None, :]).astype(I32)
    block_e = jnp.minimum(jnp.sum((block_start[:, None] >= pad_ends[None, :]).astype(I32), axis=1), N_EXPERTS - 1)
    n_valid = jnp.sum(in_expert * jnp.clip(pad_starts + counts - block_start[:, None], 0, block_rows), axis=1)
    n_used = pad_ends[-1:] // block_rows
    starts_col = jnp.broadcast_to(pad_starts.astype(F32)[:, None], (N_EXPERTS, LANES))

    dest = _dest(route, starts_col)
    dests = [dest[k].reshape(t // _sc_chunk(t), _sc_chunk(t)) for k in range(TOP_K)]
    xs = _dispatch(_as_rows3(hn), dests, n_blocks * block_rows)
    yb, experts = _moe(block_e, n_valid, n_used, _as_lines(xs), experts, block_rows)
    y0, y1 = _gather(_as_rows3(yb), dests)
    return _combine(x1, gate, _as_lines(y0), _as_lines(y1)), experts


def kernel(x_prompt, x_sample, norm_mix_g, w_in, norm_v_g, w_spatial, b_spatial, q_norm_g, k_norm_g, sink,
           w_proj_a, w_proj_b, w_out, norm_ffn_g, w_router_group, b_router_group, w_router_expert,
           b_router_expert, w_gate_e, w_up_e, w_down_e):
    depth = w_in.shape[0]
    layers = []
    for l in range(depth):
        w_router = jnp.zeros((ROUTER_ROWS, D_MODEL), F32)
        w_router = w_router.at[:N_GROUPS].set(w_router_group[l].T)
        w_router = w_router.at[SUBLANES:SUBLANES + N_EXPERTS].set(w_router_expert[l].T)
        b_router = jnp.zeros((ROUTER_ROWS,), F32)
        b_router = b_router.at[:N_GROUPS].set(b_router_group[l])
        b_router = b_router.at[SUBLANES:SUBLANES + N_EXPERTS].set(b_router_expert[l])
        b_router = jnp.broadcast_to(b_router[:, None], (ROUTER_ROWS, LANES))
        layers.append(dict(
            norm_mix_g=norm_mix_g[l][None], w_in=w_in[l].astype(BF16), norm_v_g=norm_v_g[l][None],
            w_spatial=w_spatial[l].astype(BF16),
            b_spatial=jnp.broadcast_to(b_spatial[l][:, :, None], (A_GROUPS, CHUNK, LANES)),
            q_norm_g=q_norm_g[l][None], k_norm_g=k_norm_g[l][None], sink=sink[l],
            w_proj_a=w_proj_a[l].astype(BF16), w_proj_b=w_proj_b[l].astype(BF16), w_out=w_out[l].astype(BF16),
            norm_ffn_g=norm_ffn_g[l][None], w_router=w_router.astype(BF16), b_router=b_router,
            experts=(w_gate_e[l], w_up_e[l], w_down_e[l])))

    trunks = [x_prompt, x_sample]
    rows = [x.reshape(-1, D_MODEL) for x in trunks]
    order = sorted(range(len(trunks)), key=lambda n: -rows[n].shape[0])
    for p in layers:
        experts = p["experts"]
        for n in order:
            rows[n], experts = _layer(rows[n], p, trunks[n].shape[1], experts)
    return tuple(r.reshape(x.shape) for r, x in zip(rows, trunks))
```

```python
import functools

import jax
import jax.numpy as jnp
from jax import lax
from jax.experimental import pallas as pl
from jax.experimental.pallas import tpu as pltpu
from jax.experimental.pallas import tpu_sc as plsc

F32 = jnp.float32
BF16 = jnp.bfloat16
I32 = jnp.int32
U32 = jnp.uint32

LANES = 128
SUBLANES = 8
VMEM_BYTES_V7X = 64 * 1024 * 1024
SC_CORES = 2
SC_SUBCORES = 16
SC_WORKERS = SC_CORES * SC_SUBCORES
SC_CHUNK_MAX = 128

D_MODEL = 1024
A_WIDTH = D_MODEL
A_GROUPS = 8
CHUNK = 128
HEAD_DIM = 128
N_Q_HEADS = D_MODEL // HEAD_DIM
N_KV_HEADS = 2
REP = N_Q_HEADS // N_KV_HEADS
WINDOW = 128
ROPE_THETA = 10000.0
Q_W = N_Q_HEADS * HEAD_DIM
KV_W = N_KV_HEADS * HEAD_DIM
IN_W = 2 * A_WIDTH + Q_W + 2 * KV_W + 2 * D_MODEL
COL_U = 0
COL_V = COL_U + A_WIDTH
COL_Q = COL_V + A_WIDTH
COL_K = COL_Q + Q_W
COL_VA = COL_K + KV_W
COL_GA = COL_VA + KV_W
COL_GB = COL_GA + D_MODEL
N_GROUPS = 4
EXPERTS_PER_GROUP = 8
N_EXPERTS = N_GROUPS * EXPERTS_PER_GROUP
TOP_K = 2
D_EXPERT = 512
EPS = 1e-6
NEG = -1e30

TM_IN = 512
TM_IN_SUB = 256
TQ = 512
DENSE_COLS = 256
ROUTER_ROWS = 64
assert EXPERTS_PER_GROUP == SUBLANES and SUBLANES + N_EXPERTS <= ROUTER_ROWS
TM_ROW = 1024
TM_DEST = 2048
MOE_ROWS_MAX = 1024
MOE_MIN_BLOCKS = 4
VMEM_LIMIT = 56 * 1024 * 1024
assert VMEM_LIMIT < VMEM_BYTES_V7X


def _rms(x, g):
    return x * lax.rsqrt(jnp.mean(x * x, axis=-1, keepdims=True) + EPS) * g


ROW_LINES = D_MODEL // 2 // LANES
ROW3 = (ROW_LINES, LANES)
HIGH_HALF = 0xFFFF0000


def _as_rows3(a):
    return a.reshape((a.shape[0] // ROW_LINES,) + ROW3)


def _as_lines(a):
    return a.reshape((a.shape[0] * ROW_LINES, LANES))


def _store_rows3(lines_ref, val):
    rows = val.shape[0]
    bits = lax.bitcast_convert_type(val.astype(BF16).astype(F32), U32)
    half = ROW_LINES * LANES
    for s in range(ROW_LINES):
        lo = bits[:, s * LANES:(s + 1) * LANES] >> 16
        hi = bits[:, half + s * LANES:half + (s + 1) * LANES] & U32(HIGH_HALF)
        lines_ref[pl.ds(s, rows, stride=ROW_LINES), :] = lax.bitcast_convert_type(lo | hi, I32)


def _load_rows3(lines_ref):
    rows = lines_ref.shape[0] // ROW_LINES
    words = [lax.bitcast_convert_type(lines_ref[pl.ds(s, rows, stride=ROW_LINES), :], U32)
             for s in range(ROW_LINES)]
    lo = [lax.bitcast_convert_type(w << 16, F32) for w in words]
    hi = [lax.bitcast_convert_type(w & U32(HIGH_HALF), F32) for w in words]
    return jnp.concatenate(lo + hi, axis=1)


def _rows3_spec(rows, index_map):
    return pl.BlockSpec((rows * ROW_LINES, LANES), index_map)


def _const_spec(shape):
    nd = len(shape)
    return pl.BlockSpec(shape, lambda *_: (0,) * nd, pipeline_mode=pl.Buffered(1))


def _inproj_kernel(x_ref, gmix_ref, win_ref, gv_ref, ws_ref, bs_ref, gq_ref, gk_ref, cos_ref, sin_ref,
                   wpa_ref, ma_ref, sgb_ref, q_ref, k_ref, v_ref, h_scr, u_scr, vn_scr, a_scr):
    tm = x_ref.shape[0]

    def stages(rows):
        def proj(lo, width):
            return jnp.dot(h_scr[rows], win_ref[:, lo:lo + width], preferred_element_type=F32)

        def norm_rope(z, g):
            zn = _rms(z, g)
            return zn * cos_ref[rows] + pltpu.roll(zn, HEAD_DIM // 2, 1) * sin_ref[rows]

        def norm():
            h_scr[rows] = _rms(x_ref[rows], gmix_ref[...]).astype(BF16)

        def mix_v():
            vn_scr[rows] = _rms(jax.nn.gelu(proj(COL_V, A_WIDTH)), gv_ref[...]).astype(BF16)

        def mix_u():
            u_scr[rows] = jax.nn.gelu(proj(COL_U, A_WIDTH))

        def spatial():
            for c in range(rows.start, rows.stop, CHUNK):
                chunk = slice(c, c + CHUNK)
                for g in range(A_GROUPS):
                    cols = slice(g * LANES, (g + 1) * LANES)
                    mixed = jnp.dot(ws_ref[g], vn_scr[chunk, cols], preferred_element_type=F32) + bs_ref[g]
                    a_scr[chunk, cols] = (u_scr[chunk, cols] * mixed).astype(BF16)

        def gate_a():
            ya = jnp.dot(a_scr[rows], wpa_ref[...], preferred_element_type=F32)
            ma_ref[rows] = jax.nn.sigmoid(proj(COL_GA, D_MODEL)) * ya

        def gate_b():
            sgb_ref[rows] = jax.nn.sigmoid(proj(COL_GB, D_MODEL))

        def queries():
            qz = proj(COL_Q, Q_W)
            for hd in range(N_Q_HEADS):
                cols = slice(hd * HEAD_DIM, (hd + 1) * HEAD_DIM)
                q_ref[rows, cols] = norm_rope(qz[:, cols], gq_ref[...]).astype(BF16)

        def keys_values():
            kz = proj(COL_K, KV_W)
            for hd in range(N_KV_HEADS):
                cols = slice(hd * HEAD_DIM, (hd + 1) * HEAD_DIM)
                k_ref[rows, cols] = norm_rope(kz[:, cols], gk_ref[...]).astype(BF16)
            v_ref[rows] = proj(COL_VA, KV_W).astype(BF16)

        return [norm, mix_v, mix_u, spatial, gate_a, gate_b, queries, keys_values]

    subs = [stages(slice(r, r + TM_IN_SUB)) for r in range(0, tm, TM_IN_SUB)]
    n_stage = len(subs[0])
    for step in range(n_stage + len(subs) - 1):
        for n, sub in enumerate(subs):
            if 0 <= step - n < n_stage:
                sub[step - n]()


def _inproj(x, p, cos, sin, seq):
    t = x.shape[0]
    tm = TM_IN
    n_pos = seq // tm
    row = lambda w: pl.BlockSpec((tm, w), lambda i: (i, 0))
    pos = pl.BlockSpec((tm, HEAD_DIM), lambda i: (i % n_pos, 0))
    return pl.pallas_call(
        _inproj_kernel,
        grid=(t // tm,),
        in_specs=[row(D_MODEL), _const_spec((1, D_MODEL)), _const_spec((D_MODEL, IN_W)),
                  _const_spec((1, A_WIDTH)), _const_spec((A_GROUPS, CHUNK, CHUNK)),
                  _const_spec((A_GROUPS, CHUNK, LANES)), _const_spec((1, HEAD_DIM)),
                  _const_spec((1, HEAD_DIM)), pos, pos, _const_spec((A_WIDTH, D_MODEL))],
        out_specs=[row(D_MODEL), row(D_MODEL), row(Q_W), row(KV_W), row(KV_W)],
        out_shape=[jax.ShapeDtypeStruct((t, D_MODEL), F32), jax.ShapeDtypeStruct((t, D_MODEL), F32),
                   jax.ShapeDtypeStruct((t, Q_W), BF16), jax.ShapeDtypeStruct((t, KV_W), BF16),
                   jax.ShapeDtypeStruct((t, KV_W), BF16)],
        scratch_shapes=[pltpu.VMEM((tm, D_MODEL), BF16), pltpu.VMEM((tm, A_WIDTH), F32),
                        pltpu.VMEM((tm, A_WIDTH), BF16), pltpu.VMEM((tm, A_WIDTH), BF16)],
        compiler_params=pltpu.CompilerParams(dimension_semantics=("parallel",),
                                             vmem_limit_bytes=VMEM_LIMIT),
        name="inproj",
    )(x, p["norm_mix_g"], p["w_in"], p["norm_v_g"], p["w_spatial"], p["b_spatial"], p["q_norm_g"],
      p["k_norm_g"], cos, sin, p["w_proj_a"])


def _attn_kernel(sink_ref, x_ref, ma_ref, sgb_ref, q_ref, kp_ref, kc_ref, kn_ref, vp_ref, vc_ref, vn_ref,
                 wpb_ref, wout_ref, gffn_ref, wr_ref, br_ref,
                 x1_ref, hn_ref, route_ref, gate_ref, counts_ref, kcat, vcat, o_scr, s_scr, p_scr, sink_scr, m_scr,
                 *, tiles_per_seq, n_tiles):
    tq = x_ref.shape[0]
    blk = WINDOW
    i = pl.program_id(0)
    slot = i % 2

    @pl.when(i == 0)
    def _():
        o_scr[...] = jnp.zeros_like(o_scr)
        counts_ref[...] = jnp.zeros_like(counts_ref)

    pos_tile = jnp.minimum(i, n_tiles - 1) % tiles_per_seq
    has_prev = pos_tile > 0
    has_next = pos_tile < tiles_per_seq - 1

    kcat[0:blk] = kp_ref[...]
    kcat[blk:blk + tq] = kc_ref[...]
    kcat[blk + tq:] = kn_ref[...]
    vcat[0:blk] = vp_ref[...]
    vcat[blk:blk + tq] = vc_ref[...]
    vcat[blk + tq:] = vn_ref[...]

    qr = lax.broadcasted_iota(I32, (blk, blk), 0)
    kc = lax.broadcasted_iota(I32, (blk, blk), 1)
    scale = HEAD_DIM ** -0.5
    n_sub = tq // blk
    pairs = [(j, g) for j in range(n_sub) for g in range(N_KV_HEADS)]

    def keys(ref, j, g):
        return ref[j * blk:(j + 3) * blk, g * HEAD_DIM:(g + 1) * HEAD_DIM]

    def head_cols(g, r):
        hd = g * REP + r
        return slice(hd * HEAD_DIM, (hd + 1) * HEAD_DIM)

    for b, (j, g) in enumerate(pairs):
        rows = slice(j * blk, (j + 1) * blk)
        qs = jnp.concatenate([q_ref[rows, head_cols(g, r)] for r in range(REP)], axis=0)
        s_scr[b] = lax.dot_general(qs, keys(kcat, j, g), (((1,), (1,)), ((), ())),
                                   preferred_element_type=F32)
    log2e = 1.4426950408889634

    def softmax(b, r):
        j, g = pairs[b]
        hrows = slice(r * blk, (r + 1) * blk)
        z = s_scr[b, hrows, :] * (scale * log2e)
        lo_ok = kc >= (qr + jnp.where(has_prev, 0, blk) if j == 0 else qr)
        hi_ok = kc <= (qr - jnp.where(has_next, 0, blk) if j == n_sub - 1 else qr)
        z = jnp.concatenate([jnp.where(lo_ok, z[:, :blk], NEG), z[:, blk:2 * blk],
                             jnp.where(hi_ok, z[:, 2 * blk:], NEG)], axis=1)
        sink = sink_ref[g * REP + r] * log2e
        m = jnp.maximum(jnp.max(z, axis=-1, keepdims=True), sink)
        p_scr[b, hrows, :] = jnp.exp2(z - m).astype(BF16)
        sink_scr[b, hrows, :] = jnp.broadcast_to(jnp.exp2(sink - m), (blk, LANES))

    def values(b):
        j, g = pairs[b]
        rows = slice(j * blk, (j + 1) * blk)
        v_ext = jnp.concatenate([keys(vcat, j, g), jnp.ones((3 * blk, HEAD_DIM), BF16)], axis=1)
        acc = jnp.dot(p_scr[b], v_ext, preferred_element_type=F32)
        o = (acc[:, :HEAD_DIM] / (acc[:, HEAD_DIM:] + sink_scr[b])).astype(BF16)
        for r in range(REP):
            o_scr[slot, rows, head_cols(g, r)] = o[r * blk:(r + 1) * blk, :]

    def merged_cols(cols):
        yb = jnp.dot(o_scr[1 - slot], wpb_ref[:, cols], preferred_element_type=F32)
        m_scr[:, cols] = (ma_ref[:, cols] + sgb_ref[:, cols] * yb).astype(BF16)

    def x1_cols(cols):
        x1_ref[:, cols] = x_ref[:, cols] + jnp.dot(m_scr[...], wout_ref[:, cols], preferred_element_type=F32)

    col_chunks = [slice(c * DENSE_COLS, (c + 1) * DENSE_COLS) for c in range(D_MODEL // DENSE_COLS)]
    dense = [functools.partial(f, cols) for f in (merged_cols, x1_cols) for cols in col_chunks]
    units = [(b, r) for b in range(len(pairs)) for r in range(REP)]
    units_per_dense = len(units) // len(dense)
    for n, (b, r) in enumerate(units):
        softmax(b, r)
        if (n + 1) % units_per_dense == 0:
            dense[(n + 1) // units_per_dense - 1]()
        if r == REP - 1:
            values(b)

    hn = _rms(x1_ref[...], gffn_ref[...])
    _store_rows3(hn_ref, hn)
    def wide(a):
        return jnp.concatenate([a] * (tq // LANES), axis=1)

    logits = lax.dot_general(wr_ref[...], hn.astype(BF16), (((1,), (1,)), ((), ())),
                             preferred_element_type=F32) + wide(br_ref[...])
    sub = lax.broadcasted_iota(I32, (SUBLANES, tq), 0).astype(F32)
    ninf = -jnp.inf

    def cmax(a):
        return jnp.max(a, axis=0, keepdims=True)

    def csum(a):
        return jnp.sum(a, axis=0, keepdims=True)

    def first_row(mask):
        return jnp.min(jnp.where(mask, sub, float(SUBLANES)), axis=0, keepdims=True)

    def group_rows(g):
        return logits[(g + 1) * SUBLANES:(g + 2) * SUBLANES]

    gl = jnp.where(sub < N_GROUPS, logits[0:SUBLANES], ninf)
    gmax = cmax(gl)
    g_sel = first_row(gl == gmax)
    g_p = 1.0 / csum(jnp.exp(gl - gmax))
    el = group_rows(0)
    for g in range(1, N_GROUPS):
        el = jnp.where(g_sel == g, group_rows(g), el)
    ee = jnp.exp(el - cmax(el))
    eprob = ee / csum(ee)
    p1 = cmax(eprob)
    i1 = first_row(eprob == p1)
    eprob2 = jnp.where(sub == i1, -1.0, eprob)
    p2 = cmax(eprob2)
    i2 = first_row(eprob2 == p2)
    psum = p1 + p2
    w1 = g_p * p1 / psum
    w2 = g_p * p2 / psum
    e1 = g_sel * EXPERTS_PER_GROUP + i1
    e2 = g_sel * EXPERTS_PER_GROUP + i2

    erow = lax.broadcasted_iota(I32, (N_EXPERTS, tq), 0).astype(F32)
    oh1 = erow == e1
    oh2 = erow == e2
    cnt = (jnp.where(oh1, 1.0, 0.0) + jnp.where(oh2, 1.0, 0.0)) * jnp.where(i > 0, 1.0, 0.0)
    ri = lax.broadcasted_iota(I32, (tq, tq), 0)
    ci = lax.broadcasted_iota(I32, (tq, tq), 1)
    earlier = jnp.where(ri < ci, 1.0, 0.0).astype(BF16)
    base = wide(counts_ref[...]) + jnp.dot(cnt.astype(BF16), earlier, preferred_element_type=F32)
    r1 = csum(jnp.where(oh1, base, 0.0))
    r2 = csum(jnp.where(oh2, base, 0.0))
    counts_ref[...] = counts_ref[...] + jnp.sum(cnt, axis=1, keepdims=True)

    route = jnp.where(sub == 0.0, e1, jnp.where(sub == 1.0, e2,
                      jnp.where(sub == 2.0, r1, jnp.where(sub == 3.0, r2, 0.0))))
    route_ref[...] = route.astype(I32)
    gates = jnp.where(sub == 0.0, w1, jnp.where(sub == 1.0, w2, 0.0))
    gate_ref[...] = jnp.concatenate([gates, jnp.zeros((LANES - SUBLANES, tq), F32)], axis=0).T


def _attn(x, ma, sgb, q, k, v, p, seq):
    t = x.shape[0]
    tq = TQ
    sub = tq // WINDOW
    last_blk = t // WINDOW - 1
    n_tiles = t // tq
    att = lambda i: jnp.minimum(i, n_tiles - 1)
    post = lambda i: jnp.maximum(i - 1, 0)
    att_row = lambda w: pl.BlockSpec((tq, w), lambda i: (att(i), 0))
    row = lambda w: pl.BlockSpec((tq, w), lambda i: (post(i), 0))
    prev = pl.BlockSpec((WINDOW, KV_W), lambda i: (jnp.maximum(att(i) * sub - 1, 0), 0))
    nxt = pl.BlockSpec((WINDOW, KV_W), lambda i: (jnp.minimum((att(i) + 1) * sub, last_blk), 0))
    return pl.pallas_call(
        functools.partial(_attn_kernel, tiles_per_seq=seq // tq, n_tiles=n_tiles),
        grid=(n_tiles + 1,),
        in_specs=[pl.BlockSpec(memory_space=pltpu.SMEM),
                  row(D_MODEL), row(D_MODEL), row(D_MODEL), att_row(Q_W),
                  prev, att_row(KV_W), nxt, prev, att_row(KV_W), nxt,
                  _const_spec((Q_W, D_MODEL)), _const_spec((D_MODEL, D_MODEL)), _const_spec((1, D_MODEL)),
                  _const_spec((ROUTER_ROWS, D_MODEL)), _const_spec((ROUTER_ROWS, LANES))],
        out_specs=[row(D_MODEL), _rows3_spec(tq, lambda i: (post(i), 0)),
                   pl.BlockSpec((SUBLANES, tq), lambda i: (0, post(i))), row(LANES),
                   pl.BlockSpec((N_EXPERTS, LANES), lambda i: (0, 0))],
        out_shape=[jax.ShapeDtypeStruct((t, D_MODEL), F32), jax.ShapeDtypeStruct((t * ROW_LINES, LANES), I32),
                   jax.ShapeDtypeStruct((SUBLANES, t), I32), jax.ShapeDtypeStruct((t, LANES), F32),
                   jax.ShapeDtypeStruct((N_EXPERTS, LANES), F32)],
        scratch_shapes=[pltpu.VMEM((tq + 2 * WINDOW, KV_W), BF16), pltpu.VMEM((tq + 2 * WINDOW, KV_W), BF16),
                        pltpu.VMEM((2, tq, Q_W), BF16),
                        pltpu.VMEM((sub * N_KV_HEADS, REP * WINDOW, 3 * WINDOW), F32),
                        pltpu.VMEM((sub * N_KV_HEADS, REP * WINDOW, 3 * WINDOW), BF16),
                        pltpu.VMEM((sub * N_KV_HEADS, REP * WINDOW, LANES), F32),
                        pltpu.VMEM((tq, D_MODEL), BF16)],
        compiler_params=pltpu.CompilerParams(dimension_semantics=("arbitrary",),
                                             vmem_limit_bytes=VMEM_LIMIT),
        name="attn",
    )(p["sink"], x, ma, sgb, q, k, k, k, v, v, v, p["w_proj_b"], p["w_out"], p["norm_ffn_g"],
      p["w_router"], p["b_router"])


def _dest_kernel(route_ref, starts_ref, dest_ref):
    route = route_ref[...].astype(F32)
    td = route.shape[1]
    sub = lax.broadcasted_iota(I32, route.shape, 0)
    erow = lax.broadcasted_iota(I32, (N_EXPERTS, td), 0).astype(F32)
    starts = jnp.broadcast_to(starts_ref[:, 0:1], (N_EXPERTS, td))

    def slot(k):
        start = jnp.sum(jnp.where(erow == route[k:k + 1], starts, 0.0), axis=0, keepdims=True)
        return start + route[TOP_K + k:TOP_K + k + 1]

    dest_ref[...] = jnp.where(sub == 0, slot(0), jnp.where(sub == 1, slot(1), 0.0)).astype(I32)


def _dest(route, pad_starts):
    t = route.shape[1]
    td = min(TM_DEST, t)
    return pl.pallas_call(
        _dest_kernel,
        grid=(t // td,),
        in_specs=[pl.BlockSpec((SUBLANES, td), lambda i: (0, i)), _const_spec((N_EXPERTS, LANES))],
        out_specs=pl.BlockSpec((SUBLANES, td), lambda i: (0, i)),
        out_shape=jax.ShapeDtypeStruct((SUBLANES, t), I32),
        compiler_params=pltpu.CompilerParams(dimension_semantics=("parallel",)),
        name="dest",
    )(route, pad_starts)


def _sc_mesh():
    return plsc.VectorSubcoreMesh(core_axis_name="c", subcore_axis_name="s")


def _sc_worker():
    return lax.axis_index("s") * SC_CORES + lax.axis_index("c")


def _sc_chunk(t):
    return min(SC_CHUNK_MAX, t // (SC_WORKERS * SUBLANES))


def _dispatch(hn, dests, n_rows):
    t = hn.shape[0]
    chunk = dests[0].shape[1]
    per_worker = t // SC_WORKERS
    n_chunks = per_worker // chunk
    idx = pltpu.VMEM((n_chunks, chunk), I32)

    @functools.partial(
        pl.kernel, mesh=_sc_mesh(), out_type=jax.ShapeDtypeStruct((n_rows,) + ROW3, I32),
        scratch_types=[idx, idx, pltpu.VMEM((chunk,) + ROW3, I32), pltpu.SemaphoreType.DMA])
    def scatter_rows(hn_hbm, d0_hbm, d1_hbm, xs_hbm, i0_v, i1_v, rows_v, sem):
        w = _sc_worker()
        pltpu.sync_copy(d0_hbm.at[pl.ds(w * n_chunks, n_chunks)], i0_v)
        pltpu.sync_copy(d1_hbm.at[pl.ds(w * n_chunks, n_chunks)], i1_v)

        @pl.loop(0, n_chunks)
        def _(j):
            pltpu.sync_copy(hn_hbm.at[pl.ds(w * per_worker + j * chunk, chunk)], rows_v)
            copies = [pltpu.make_async_copy(rows_v, xs_hbm.at[i_v.at[j]], sem) for i_v in (i0_v, i1_v)]
            for cp in copies:
                cp.start()
            for cp in copies:
                cp.wait()

    return scatter_rows(hn, *dests)


def _moe_block_rows(t):
    rows = MOE_ROWS_MAX
    while rows > CHUNK and (t * TOP_K) // N_EXPERTS < MOE_MIN_BLOCKS * rows:
        rows //= 2
    return rows


def _moe_kernel(be_ref, nvalid_ref, nused_ref, xs_ref, wg_ref, wu_ref, wd_ref, yb_ref, *w16_refs):
    i = pl.program_id(0)
    used = i < nused_ref[0]
    weights = w16_refs if w16_refs else (wg_ref, wu_ref, wd_ref)

    @pl.when(jnp.logical_not(used))
    def _():
        yb_ref[...] = jnp.zeros_like(yb_ref)

    @pl.when(used)
    def _():
        if w16_refs:
            @pl.when(jnp.logical_or(i == 0, be_ref[i] != be_ref[jnp.maximum(i - 1, 0)]))
            def _():
                for src, dst in zip((wg_ref, wu_ref, wd_ref), w16_refs):
                    dst[0] = src[0].astype(BF16)

        row = lax.broadcasted_iota(I32, (xs_ref.shape[0] // ROW_LINES, 1), 0)
        x = jnp.where(row < nvalid_ref[i], _load_rows3(xs_ref), 0.0).astype(BF16)
        gate = jnp.dot(x, weights[0][0], preferred_element_type=F32)
        up = jnp.dot(x, weights[1][0], preferred_element_type=F32)
        hid = (jax.nn.silu(gate) * up).astype(BF16)
        _store_rows3(yb_ref, jnp.dot(hid, weights[2][0], preferred_element_type=F32))


def _moe(block_e, n_valid, n_used, xs, experts, block_rows):
    n_blocks = xs.shape[0] // (block_rows * ROW_LINES)
    cast = experts[0].dtype != BF16

    def rows(i, be, nv, nu):
        return (jnp.minimum(i, nu[0] - 1), 0)

    def expert(i, be, nv, nu):
        return (be[jnp.minimum(i, nu[0] - 1)], 0, 0)

    w_specs = [pl.BlockSpec((1,) + w.shape[1:], expert) for w in experts]
    out_specs = [_rows3_spec(block_rows, lambda i, be, nv, nu: (i, 0))]
    out_shape = [jax.ShapeDtypeStruct(xs.shape, I32)]
    if cast:
        out_specs += w_specs
        out_shape += [jax.ShapeDtypeStruct(w.shape, BF16) for w in experts]
    yb, *w16 = pl.pallas_call(
        _moe_kernel,
        grid_spec=pltpu.PrefetchScalarGridSpec(
            num_scalar_prefetch=3,
            grid=(n_blocks,),
            in_specs=[_rows3_spec(block_rows, rows)] + w_specs,
            out_specs=out_specs,
        ),
        out_shape=out_shape,
        compiler_params=pltpu.CompilerParams(dimension_semantics=("arbitrary",),
                                             vmem_limit_bytes=VMEM_LIMIT),
        name="moe",
    )(block_e, n_valid, n_used, xs, *experts)
    return yb, (tuple(w16) if cast else experts)


def _gather(yb, dests):
    chunk = dests[0].shape[1]
    t = dests[0].shape[0] * chunk
    per_worker = t // SC_WORKERS
    n_chunks = per_worker // chunk
    idx = pltpu.VMEM((n_chunks, chunk), I32)
    out = jax.ShapeDtypeStruct((t,) + ROW3, I32)

    @functools.partial(
        pl.kernel, mesh=_sc_mesh(), out_type=(out, out),
        scratch_types=[idx, idx, pltpu.VMEM((chunk,) + ROW3, I32), pltpu.SemaphoreType.DMA])
    def gather_rows(yb_hbm, d0_hbm, d1_hbm, y0_hbm, y1_hbm, i0_v, i1_v, rows_v, sem):
        w = _sc_worker()
        pltpu.sync_copy(d0_hbm.at[pl.ds(w * n_chunks, n_chunks)], i0_v)
        pltpu.sync_copy(d1_hbm.at[pl.ds(w * n_chunks, n_chunks)], i1_v)

        @pl.loop(0, n_chunks)
        def _(j):
            rows = pl.ds(w * per_worker + j * chunk, chunk)
            for i_v, y_hbm in ((i0_v, y0_hbm), (i1_v, y1_hbm)):
                pltpu.async_copy(yb_hbm.at[i_v.at[j]], rows_v, sem).wait()
                pltpu.sync_copy(rows_v, y_hbm.at[rows])

    return gather_rows(yb, *dests)


def _combine_kernel(x1_ref, gate_ref, y0_ref, y1_ref, out_ref):
    gate = gate_ref[...]
    out_ref[...] = x1_ref[...] + (_load_rows3(y0_ref) * gate[:, 0:1] + _load_rows3(y1_ref) * gate[:, 1:2])


def _combine(x1, gate, y0, y1):
    t = x1.shape[0]
    tm = TM_ROW
    return pl.pallas_call(
        _combine_kernel,
        grid=(t // tm,),
        in_specs=[pl.BlockSpec((tm, D_MODEL), lambda i: (i, 0)),
                  pl.BlockSpec((tm, LANES), lambda i: (i, 0)),
                  _rows3_spec(tm, lambda i: (i, 0)), _rows3_spec(tm, lambda i: (i, 0))],
        out_specs=pl.BlockSpec((tm, D_MODEL), lambda i: (i, 0)),
        out_shape=jax.ShapeDtypeStruct((t, D_MODEL), F32),
        compiler_params=pltpu.CompilerParams(dimension_semantics=("parallel",)),
        name="combine",
    )(x1, gate, y0, y1)


def _rope_tables(seq):
    half = HEAD_DIM // 2
    inv_freq = ROPE_THETA ** (-jnp.arange(half, dtype=F32) / half)
    ang = jnp.arange(seq).astype(F32)[:, None] * inv_freq[None, :]
    cos = jnp.cos(ang)
    sin = jnp.sin(ang)
    return jnp.concatenate([cos, cos], axis=-1), jnp.concatenate([-sin, sin], axis=-1)


def _layer(x, p, seq, rope, experts):
    t = x.shape[0]
    ma, sgb, q, k, v = _inproj(x, p, *rope, seq)
    x1, hn, route, gate, counts_f = _attn(x, ma, sgb, q, k, v, p, seq)

    block_rows = _moe_block_rows(t)
    counts = counts_f[:, 0].astype(I32)
    min_blocks = 1 if experts[0].dtype != BF16 else 0
    padded = jnp.maximum((counts + block_rows - 1) // block_rows, min_blocks) * block_rows
    pad_ends = jnp.cumsum(padded)
    pad_starts = pad_ends - padded
    n_blocks = (t * TOP_K) // block_rows + N_EXPERTS
    block_start = jnp.arange(n_blocks, dtype=I32) * block_rows
    in_expert = jnp.logical_and(block_start[:, None] >= pad_starts[None, :],
                                block_start[:, None] < pad_ends[None, :]).astype(I32)
    block_e = jnp.minimum(jnp.sum((block_start[:, None] >= pad_ends[None, :]).astype(I32), axis=1), N_EXPERTS - 1)
    n_valid = jnp.sum(in_expert * jnp.clip(pad_starts + counts - block_start[:, None], 0, block_rows), axis=1)
    n_used = pad_ends[-1:] // block_rows
    starts_col = jnp.broadcast_to(pad_starts.astype(F32)[:, None], (N_EXPERTS, LANES))

    dest = _dest(route, starts_col)
    dests = [dest[k].reshape(t // _sc_chunk(t), _sc_chunk(t)) for k in range(TOP_K)]
    xs = _dispatch(_as_rows3(hn), dests, n_blocks * block_rows)
    yb, experts = _moe(block_e, n_valid, n_used, _as_lines(xs), experts, block_rows)
    y0, y1 = _gather(_as_rows3(yb), dests)
    return _combine(x1, gate, _as_lines(y0), _as_lines(y1)), experts


def kernel(x_prompt, x_sample, norm_mix_g, w_in, norm_v_g, w_spatial, b_spatial, q_norm_g, k_norm_g, sink,
           w_proj_a, w_proj_b, w_out, norm_ffn_g, w_router_group, b_router_group, w_router_expert,
           b_router_expert, w_gate_e, w_up_e, w_down_e):
    depth = w_in.shape[0]
    layers = []
    for l in range(depth):
        w_router = jnp.zeros((ROUTER_ROWS, D_MODEL), F32)
        w_router = w_router.at[:N_GROUPS].set(w_router_group[l].T)
        w_router = w_router.at[SUBLANES:SUBLANES + N_EXPERTS].set(w_router_expert[l].T)
        b_router = jnp.zeros((ROUTER_ROWS,), F32)
        b_router = b_router.at[:N_GROUPS].set(b_router_group[l])
        b_router = b_router.at[SUBLANES:SUBLANES + N_EXPERTS].set(b_router_expert[l])
        b_router = jnp.broadcast_to(b_router[:, None], (ROUTER_ROWS, LANES))
        layers.append(dict(
            norm_mix_g=norm_mix_g[l][None], w_in=w_in[l].astype(BF16), norm_v_g=norm_v_g[l][None],
            w_spatial=w_spatial[l].astype(BF16),
            b_spatial=jnp.broadcast_to(b_spatial[l][:, :, None], (A_GROUPS, CHUNK, LANES)),
            q_norm_g=q_norm_g[l][None], k_norm_g=k_norm_g[l][None], sink=sink[l],
            w_proj_a=w_proj_a[l].astype(BF16), w_proj_b=w_proj_b[l].astype(BF16), w_out=w_out[l].astype(BF16),
            norm_ffn_g=norm_ffn_g[l][None], w_router=w_router.astype(BF16), b_router=b_router,
            experts=(w_gate_e[l], w_up_e[l], w_down_e[l])))

    trunks = [x_prompt, x_sample]
    rows = [x.reshape(-1, D_MODEL) for x in trunks]
    order = sorted(range(len(trunks)), key=lambda n: -rows[n].shape[0])
    rope = _rope_tables(max(x.shape[1] for x in trunks))
    for p in layers:
        experts = p["experts"]
        for n in order:
            rows[n], experts = _layer(rows[n], p, trunks[n].shape[1], rope, experts)
    return tuple(r.reshape(x.shape) for r, x in zip(rows, trunks))
```

```python
import functools

import jax
import jax.numpy as jnp
from jax import lax
from jax.experimental import pallas as pl
from jax.experimental.pallas import tpu as pltpu
from jax.experimental.pallas import tpu_sc as plsc

F32 = jnp.float32
BF16 = jnp.bfloat16
I32 = jnp.int32
U32 = jnp.uint32

LANES = 128
SUBLANES = 8
VMEM_BYTES_V7X = 64 * 1024 * 1024
SC_CORES = 2
SC_SUBCORES = 16
SC_WORKERS = SC_CORES * SC_SUBCORES
SC_CHUNK_MAX = 128

D_MODEL = 1024
A_WIDTH = D_MODEL
A_GROUPS = 8
CHUNK = 128
HEAD_DIM = 128
N_Q_HEADS = D_MODEL // HEAD_DIM
N_KV_HEADS = 2
REP = N_Q_HEADS // N_KV_HEADS
WINDOW = 128
ROPE_THETA = 10000.0
ROPE_SPLIT = 64
Q_W = N_Q_HEADS * HEAD_DIM
KV_W = N_KV_HEADS * HEAD_DIM
IN_W = 2 * A_WIDTH + Q_W + 2 * KV_W + 2 * D_MODEL
COL_U = 0
COL_V = COL_U + A_WIDTH
COL_Q = COL_V + A_WIDTH
COL_K = COL_Q + Q_W
COL_VA = COL_K + KV_W
COL_GA = COL_VA + KV_W
COL_GB = COL_GA + D_MODEL
N_GROUPS = 4
EXPERTS_PER_GROUP = 8
N_EXPERTS = N_GROUPS * EXPERTS_PER_GROUP
TOP_K = 2
D_EXPERT = 512
EPS = 1e-6
NEG = -1e30

TM_IN = 512
TM_IN_SUB = 256
TQ = 512
DENSE_COLS = 256
ROUTER_ROWS = 64
assert EXPERTS_PER_GROUP == SUBLANES and SUBLANES + N_EXPERTS <= ROUTER_ROWS
TM_ROW = 1024
TM_DEST = 2048
MOE_ROWS_MAX = 1024
MOE_MIN_BLOCKS = 4
VMEM_LIMIT = 56 * 1024 * 1024
assert VMEM_LIMIT < VMEM_BYTES_V7X


def _rms(x, g):
    return x * lax.rsqrt(jnp.mean(x * x, axis=-1, keepdims=True) + EPS) * g


ROW_LINES = D_MODEL // 2 // LANES
ROW3 = (ROW_LINES, LANES)
HIGH_HALF = 0xFFFF0000


def _as_rows3(a):
    return a.reshape((a.shape[0] // ROW_LINES,) + ROW3)


def _as_lines(a):
    return a.reshape((a.shape[0] * ROW_LINES, LANES))


def _store_rows3(lines_ref, val):
    rows = val.shape[0]
    bits = lax.bitcast_convert_type(val.astype(BF16).astype(F32), U32)
    half = ROW_LINES * LANES
    for s in range(ROW_LINES):
        lo = bits[:, s * LANES:(s + 1) * LANES] >> 16
        hi = bits[:, half + s * LANES:half + (s + 1) * LANES] & U32(HIGH_HALF)
        lines_ref[pl.ds(s, rows, stride=ROW_LINES), :] = lax.bitcast_convert_type(lo | hi, I32)


def _load_rows3(lines_ref):
    rows = lines_ref.shape[0] // ROW_LINES
    words = [lax.bitcast_convert_type(lines_ref[pl.ds(s, rows, stride=ROW_LINES), :], U32)
             for s in range(ROW_LINES)]
    lo = [lax.bitcast_convert_type(w << 16, F32) for w in words]
    hi = [lax.bitcast_convert_type(w & U32(HIGH_HALF), F32) for w in words]
    return jnp.concatenate(lo + hi, axis=1)


def _rows3_spec(rows, index_map):
    return pl.BlockSpec((rows * ROW_LINES, LANES), index_map)


def _const_spec(shape):
    nd = len(shape)
    return pl.BlockSpec(shape, lambda *_: (0,) * nd, pipeline_mode=pl.Buffered(1))


def _inproj_kernel(x_ref, gmix_ref, win_ref, gv_ref, ws_ref, bs_ref, gq_ref, gk_ref, cos_ref, sin_ref,
                   wpa_ref, ma_ref, sgb_ref, q_ref, k_ref, v_ref, h_scr, u_scr, vn_scr, a_scr):
    tm = x_ref.shape[0]

    def stages(rows):
        def proj(lo, width):
            return jnp.dot(h_scr[rows], win_ref[:, lo:lo + width], preferred_element_type=F32)

        def norm_rope(z, g):
            zn = _rms(z, g)
            return zn * cos_ref[rows] + pltpu.roll(zn, HEAD_DIM // 2, 1) * sin_ref[rows]

        def norm():
            h_scr[rows] = _rms(x_ref[rows], gmix_ref[...]).astype(BF16)

        def mix_v():
            vn_scr[rows] = _rms(jax.nn.gelu(proj(COL_V, A_WIDTH)), gv_ref[...]).astype(BF16)

        def mix_u():
            u_scr[rows] = jax.nn.gelu(proj(COL_U, A_WIDTH))

        def spatial():
            for c in range(rows.start, rows.stop, CHUNK):
                chunk = slice(c, c + CHUNK)
                for g in range(A_GROUPS):
                    cols = slice(g * LANES, (g + 1) * LANES)
                    mixed = jnp.dot(ws_ref[g], vn_scr[chunk, cols], preferred_element_type=F32) + bs_ref[g]
                    a_scr[chunk, cols] = (u_scr[chunk, cols] * mixed).astype(BF16)

        def gate_a():
            ya = jnp.dot(a_scr[rows], wpa_ref[...], preferred_element_type=F32)
            ma_ref[rows] = jax.nn.sigmoid(proj(COL_GA, D_MODEL)) * ya

        def gate_b():
            sgb_ref[rows] = jax.nn.sigmoid(proj(COL_GB, D_MODEL))

        def queries():
            qz = proj(COL_Q, Q_W)
            for hd in range(N_Q_HEADS):
                cols = slice(hd * HEAD_DIM, (hd + 1) * HEAD_DIM)
                q_ref[rows, cols] = norm_rope(qz[:, cols], gq_ref[...]).astype(BF16)

        def keys_values():
            kz = proj(COL_K, KV_W)
            for hd in range(N_KV_HEADS):
                cols = slice(hd * HEAD_DIM, (hd + 1) * HEAD_DIM)
                k_ref[rows, cols] = norm_rope(kz[:, cols], gk_ref[...]).astype(BF16)
            v_ref[rows] = proj(COL_VA, KV_W).astype(BF16)

        return [norm, mix_v, mix_u, spatial, gate_a, gate_b, queries, keys_values]

    subs = [stages(slice(r, r + TM_IN_SUB)) for r in range(0, tm, TM_IN_SUB)]
    n_stage = len(subs[0])
    for step in range(n_stage + len(subs) - 1):
        for n, sub in enumerate(subs):
            if 0 <= step - n < n_stage:
                sub[step - n]()


def _inproj(x, p, cos, sin, seq):
    t = x.shape[0]
    tm = TM_IN
    n_pos = seq // tm
    row = lambda w: pl.BlockSpec((tm, w), lambda i: (i, 0))
    pos = pl.BlockSpec((tm, HEAD_DIM), lambda i: (i % n_pos, 0))
    return pl.pallas_call(
        _inproj_kernel,
        grid=(t // tm,),
        in_specs=[row(D_MODEL), _const_spec((1, D_MODEL)), _const_spec((D_MODEL, IN_W)),
                  _const_spec((1, A_WIDTH)), _const_spec((A_GROUPS, CHUNK, CHUNK)),
                  _const_spec((A_GROUPS, CHUNK, LANES)), _const_spec((1, HEAD_DIM)),
                  _const_spec((1, HEAD_DIM)), pos, pos, _const_spec((A_WIDTH, D_MODEL))],
        out_specs=[row(D_MODEL), row(D_MODEL), row(Q_W), row(KV_W), row(KV_W)],
        out_shape=[jax.ShapeDtypeStruct((t, D_MODEL), F32), jax.ShapeDtypeStruct((t, D_MODEL), F32),
                   jax.ShapeDtypeStruct((t, Q_W), BF16), jax.ShapeDtypeStruct((t, KV_W), BF16),
                   jax.ShapeDtypeStruct((t, KV_W), BF16)],
        scratch_shapes=[pltpu.VMEM((tm, D_MODEL), BF16), pltpu.VMEM((tm, A_WIDTH), F32),
                        pltpu.VMEM((tm, A_WIDTH), BF16), pltpu.VMEM((tm, A_WIDTH), BF16)],
        compiler_params=pltpu.CompilerParams(dimension_semantics=("parallel",),
                                             vmem_limit_bytes=VMEM_LIMIT),
        name="inproj",
    )(x, p["norm_mix_g"], p["w_in"], p["norm_v_g"], p["w_spatial"], p["b_spatial"], p["q_norm_g"],
      p["k_norm_g"], cos, sin, p["w_proj_a"])


def _attn_kernel(sink_ref, x_ref, ma_ref, sgb_ref, q_ref, kp_ref, kc_ref, kn_ref, vp_ref, vc_ref, vn_ref,
                 wpb_ref, wout_ref, gffn_ref, wr_ref, br_ref,
                 x1_ref, hn_ref, route_ref, gate_ref, counts_ref, kcat, vcat, o_scr, s_scr, p_scr, sink_scr, m_scr,
                 *, tiles_per_seq, n_tiles):
    tq = x_ref.shape[0]
    blk = WINDOW
    i = pl.program_id(0)
    slot = i % 2

    @pl.when(i == 0)
    def _():
        o_scr[...] = jnp.zeros_like(o_scr)
        counts_ref[...] = jnp.zeros_like(counts_ref)

    pos_tile = jnp.minimum(i, n_tiles - 1) % tiles_per_seq
    has_prev = pos_tile > 0
    has_next = pos_tile < tiles_per_seq - 1

    kcat[0:blk] = kp_ref[...]
    kcat[blk:blk + tq] = kc_ref[...]
    kcat[blk + tq:] = kn_ref[...]
    vcat[0:blk] = vp_ref[...]
    vcat[blk:blk + tq] = vc_ref[...]
    vcat[blk + tq:] = vn_ref[...]

    qr = lax.broadcasted_iota(I32, (blk, blk), 0)
    kc = lax.broadcasted_iota(I32, (blk, blk), 1)
    scale = HEAD_DIM ** -0.5
    n_sub = tq // blk
    pairs = [(j, g) for j in range(n_sub) for g in range(N_KV_HEADS)]

    def keys(ref, j, g):
        return ref[j * blk:(j + 3) * blk, g * HEAD_DIM:(g + 1) * HEAD_DIM]

    def head_cols(g, r):
        hd = g * REP + r
        return slice(hd * HEAD_DIM, (hd + 1) * HEAD_DIM)

    for b, (j, g) in enumerate(pairs):
        rows = slice(j * blk, (j + 1) * blk)
        qs = jnp.concatenate([q_ref[rows, head_cols(g, r)] for r in range(REP)], axis=0)
        s_scr[b] = lax.dot_general(qs, keys(kcat, j, g), (((1,), (1,)), ((), ())),
                                   preferred_element_type=F32)
    log2e = 1.4426950408889634

    def softmax(b, r):
        j, g = pairs[b]
        hrows = slice(r * blk, (r + 1) * blk)
        z = s_scr[b, hrows, :] * (scale * log2e)
        lo_ok = kc >= (qr + jnp.where(has_prev, 0, blk) if j == 0 else qr)
        hi_ok = kc <= (qr - jnp.where(has_next, 0, blk) if j == n_sub - 1 else qr)
        z = jnp.concatenate([jnp.where(lo_ok, z[:, :blk], NEG), z[:, blk:2 * blk],
                             jnp.where(hi_ok, z[:, 2 * blk:], NEG)], axis=1)
        sink = sink_ref[g * REP + r] * log2e
        m = jnp.maximum(jnp.max(z, axis=-1, keepdims=True), sink)
        p_scr[b, hrows, :] = jnp.exp2(z - m).astype(BF16)
        sink_scr[b, hrows, :] = jnp.broadcast_to(jnp.exp2(sink - m), (blk, LANES))

    def values(b):
        j, g = pairs[b]
        rows = slice(j * blk, (j + 1) * blk)
        v_ext = jnp.concatenate([keys(vcat, j, g), jnp.ones((3 * blk, HEAD_DIM), BF16)], axis=1)
        acc = jnp.dot(p_scr[b], v_ext, preferred_element_type=F32)
        o = (acc[:, :HEAD_DIM] / (acc[:, HEAD_DIM:] + sink_scr[b])).astype(BF16)
        for r in range(REP):
            o_scr[slot, rows, head_cols(g, r)] = o[r * blk:(r + 1) * blk, :]

    def merged_cols(cols):
        yb = jnp.dot(o_scr[1 - slot], wpb_ref[:, cols], preferred_element_type=F32)
        m_scr[:, cols] = (ma_ref[:, cols] + sgb_ref[:, cols] * yb).astype(BF16)

    def x1_cols(cols):
        x1_ref[:, cols] = x_ref[:, cols] + jnp.dot(m_scr[...], wout_ref[:, cols], preferred_element_type=F32)

    col_chunks = [slice(c * DENSE_COLS, (c + 1) * DENSE_COLS) for c in range(D_MODEL // DENSE_COLS)]
    dense = [functools.partial(f, cols) for f in (merged_cols, x1_cols) for cols in col_chunks]
    units = [(b, r) for b in range(len(pairs)) for r in range(REP)]
    units_per_dense = len(units) // len(dense)
    for n, (b, r) in enumerate(units):
        softmax(b, r)
        if (n + 1) % units_per_dense == 0:
            dense[(n + 1) // units_per_dense - 1]()
        if r == REP - 1:
            values(b)

    hn = _rms(x1_ref[...], gffn_ref[...])
    _store_rows3(hn_ref, hn)
    def wide(a):
        return jnp.concatenate([a] * (tq // LANES), axis=1)

    logits = lax.dot_general(wr_ref[...], hn.astype(BF16), (((1,), (1,)), ((), ())),
                             preferred_element_type=F32) + wide(br_ref[...])
    sub = lax.broadcasted_iota(I32, (SUBLANES, tq), 0).astype(F32)
    ninf = -jnp.inf

    def cmax(a):
        return jnp.max(a, axis=0, keepdims=True)

    def csum(a):
        return jnp.sum(a, axis=0, keepdims=True)

    def first_row(mask):
        return jnp.min(jnp.where(mask, sub, float(SUBLANES)), axis=0, keepdims=True)

    def group_rows(g):
        return logits[(g + 1) * SUBLANES:(g + 2) * SUBLANES]

    gl = jnp.where(sub < N_GROUPS, logits[0:SUBLANES], ninf)
    gmax = cmax(gl)
    g_sel = first_row(gl == gmax)
    g_p = 1.0 / csum(jnp.exp(gl - gmax))
    el = group_rows(0)
    for g in range(1, N_GROUPS):
        el = jnp.where(g_sel == g, group_rows(g), el)
    ee = jnp.exp(el - cmax(el))
    eprob = ee / csum(ee)
    p1 = cmax(eprob)
    i1 = first_row(eprob == p1)
    eprob2 = jnp.where(sub == i1, -1.0, eprob)
    p2 = cmax(eprob2)
    i2 = first_row(eprob2 == p2)
    psum = p1 + p2
    w1 = g_p * p1 / psum
    w2 = g_p * p2 / psum
    e1 = g_sel * EXPERTS_PER_GROUP + i1
    e2 = g_sel * EXPERTS_PER_GROUP + i2

    erow = lax.broadcasted_iota(I32, (N_EXPERTS, tq), 0).astype(F32)
    oh1 = erow == e1
    oh2 = erow == e2
    cnt = (jnp.where(oh1, 1.0, 0.0) + jnp.where(oh2, 1.0, 0.0)) * jnp.where(i > 0, 1.0, 0.0)
    ri = lax.broadcasted_iota(I32, (tq, tq), 0)
    ci = lax.broadcasted_iota(I32, (tq, tq), 1)
    earlier = jnp.where(ri < ci, 1.0, 0.0).astype(BF16)
    base = wide(counts_ref[...]) + jnp.dot(cnt.astype(BF16), earlier, preferred_element_type=F32)
    r1 = csum(jnp.where(oh1, base, 0.0))
    r2 = csum(jnp.where(oh2, base, 0.0))
    counts_ref[...] = counts_ref[...] + jnp.sum(cnt, axis=1, keepdims=True)

    route = jnp.where(sub == 0.0, e1, jnp.where(sub == 1.0, e2,
                      jnp.where(sub == 2.0, r1, jnp.where(sub == 3.0, r2, 0.0))))
    route_ref[...] = route.astype(I32)
    gates = jnp.where(sub == 0.0, w1, jnp.where(sub == 1.0, w2, 0.0))
    gate_ref[...] = jnp.concatenate([gates, jnp.zeros((LANES - SUBLANES, tq), F32)], axis=0).T


def _attn(x, ma, sgb, q, k, v, p, seq):
    t = x.shape[0]
    tq = TQ
    sub = tq // WINDOW
    last_blk = t // WINDOW - 1
    n_tiles = t // tq
    att = lambda i: jnp.minimum(i, n_tiles - 1)
    post = lambda i: jnp.maximum(i - 1, 0)
    att_row = lambda w: pl.BlockSpec((tq, w), lambda i: (att(i), 0))
    row = lambda w: pl.BlockSpec((tq, w), lambda i: (post(i), 0))
    prev = pl.BlockSpec((WINDOW, KV_W), lambda i: (jnp.maximum(att(i) * sub - 1, 0), 0))
    nxt = pl.BlockSpec((WINDOW, KV_W), lambda i: (jnp.minimum((att(i) + 1) * sub, last_blk), 0))
    return pl.pallas_call(
        functools.partial(_attn_kernel, tiles_per_seq=seq // tq, n_tiles=n_tiles),
        grid=(n_tiles + 1,),
        in_specs=[pl.BlockSpec(memory_space=pltpu.SMEM),
                  row(D_MODEL), row(D_MODEL), row(D_MODEL), att_row(Q_W),
                  prev, att_row(KV_W), nxt, prev, att_row(KV_W), nxt,
                  _const_spec((Q_W, D_MODEL)), _const_spec((D_MODEL, D_MODEL)), _const_spec((1, D_MODEL)),
                  _const_spec((ROUTER_ROWS, D_MODEL)), _const_spec((ROUTER_ROWS, LANES))],
        out_specs=[row(D_MODEL), _rows3_spec(tq, lambda i: (post(i), 0)),
                   pl.BlockSpec((SUBLANES, tq), lambda i: (0, post(i))), row(LANES),
                   pl.BlockSpec((N_EXPERTS, LANES), lambda i: (0, 0))],
        out_shape=[jax.ShapeDtypeStruct((t, D_MODEL), F32), jax.ShapeDtypeStruct((t * ROW_LINES, LANES), I32),
                   jax.ShapeDtypeStruct((SUBLANES, t), I32), jax.ShapeDtypeStruct((t, LANES), F32),
                   jax.ShapeDtypeStruct((N_EXPERTS, LANES), F32)],
        scratch_shapes=[pltpu.VMEM((tq + 2 * WINDOW, KV_W), BF16), pltpu.VMEM((tq + 2 * WINDOW, KV_W), BF16),
                        pltpu.VMEM((2, tq, Q_W), BF16),
                        pltpu.VMEM((sub * N_KV_HEADS, REP * WINDOW, 3 * WINDOW), F32),
                        pltpu.VMEM((sub * N_KV_HEADS, REP * WINDOW, 3 * WINDOW), BF16),
                        pltpu.VMEM((sub * N_KV_HEADS, REP * WINDOW, LANES), F32),
                        pltpu.VMEM((tq, D_MODEL), BF16)],
        compiler_params=pltpu.CompilerParams(dimension_semantics=("arbitrary",),
                                             vmem_limit_bytes=VMEM_LIMIT),
        name="attn",
    )(p["sink"], x, ma, sgb, q, k, k, k, v, v, v, p["w_proj_b"], p["w_out"], p["norm_ffn_g"],
      p["w_router"], p["b_router"])


def _dest_kernel(route_ref, starts_ref, dest_ref):
    route = route_ref[...].astype(F32)
    td = route.shape[1]
    sub = lax.broadcasted_iota(I32, route.shape, 0)
    erow = lax.broadcasted_iota(I32, (N_EXPERTS, td), 0).astype(F32)
    starts = jnp.broadcast_to(starts_ref[:, 0:1], (N_EXPERTS, td))

    def slot(k):
        start = jnp.sum(jnp.where(erow == route[k:k + 1], starts, 0.0), axis=0, keepdims=True)
        return start + route[TOP_K + k:TOP_K + k + 1]

    dest_ref[...] = jnp.where(sub == 0, slot(0), jnp.where(sub == 1, slot(1), 0.0)).astype(I32)


def _dest(route, pad_starts):
    t = route.shape[1]
    td = min(TM_DEST, t)
    return pl.pallas_call(
        _dest_kernel,
        grid=(t // td,),
        in_specs=[pl.BlockSpec((SUBLANES, td), lambda i: (0, i)), _const_spec((N_EXPERTS, LANES))],
        out_specs=pl.BlockSpec((SUBLANES, td), lambda i: (0, i)),
        out_shape=jax.ShapeDtypeStruct((SUBLANES, t), I32),
        compiler_params=pltpu.CompilerParams(dimension_semantics=("parallel",)),
        name="dest",
    )(route, pad_starts)


def _sc_mesh():
    return plsc.VectorSubcoreMesh(core_axis_name="c", subcore_axis_name="s")


def _sc_worker():
    return lax.axis_index("s") * SC_CORES + lax.axis_index("c")


def _sc_chunk(t):
    return min(SC_CHUNK_MAX, t // (SC_WORKERS * SUBLANES))


def _dispatch(hn, dests, n_rows):
    t = hn.shape[0]
    chunk = dests[0].shape[1]
    per_worker = t // SC_WORKERS
    n_chunks = per_worker // chunk
    idx = pltpu.VMEM((n_chunks, chunk), I32)

    @functools.partial(
        pl.kernel, mesh=_sc_mesh(), out_type=jax.ShapeDtypeStruct((n_rows,) + ROW3, I32),
        scratch_types=[idx, idx, pltpu.VMEM((chunk,) + ROW3, I32), pltpu.SemaphoreType.DMA])
    def scatter_rows(hn_hbm, d0_hbm, d1_hbm, xs_hbm, i0_v, i1_v, rows_v, sem):
        w = _sc_worker()
        pltpu.sync_copy(d0_hbm.at[pl.ds(w * n_chunks, n_chunks)], i0_v)
        pltpu.sync_copy(d1_hbm.at[pl.ds(w * n_chunks, n_chunks)], i1_v)

        @pl.loop(0, n_chunks)
        def _(j):
            pltpu.sync_copy(hn_hbm.at[pl.ds(w * per_worker + j * chunk, chunk)], rows_v)
            copies = [pltpu.make_async_copy(rows_v, xs_hbm.at[i_v.at[j]], sem) for i_v in (i0_v, i1_v)]
            for cp in copies:
                cp.start()
            for cp in copies:
                cp.wait()

    return scatter_rows(hn, *dests)


def _moe_block_rows(t):
    rows = MOE_ROWS_MAX
    while rows > CHUNK and (t * TOP_K) // N_EXPERTS < MOE_MIN_BLOCKS * rows:
        rows //= 2
    return rows


def _moe_kernel(be_ref, nvalid_ref, nused_ref, xs_ref, wg_ref, wu_ref, wd_ref, yb_ref, *w16_refs):
    i = pl.program_id(0)
    used = i < nused_ref[0]
    weights = w16_refs if w16_refs else (wg_ref, wu_ref, wd_ref)

    @pl.when(jnp.logical_not(used))
    def _():
        yb_ref[...] = jnp.zeros_like(yb_ref)

    @pl.when(used)
    def _():
        if w16_refs:
            @pl.when(jnp.logical_or(i == 0, be_ref[i] != be_ref[jnp.maximum(i - 1, 0)]))
            def _():
                for src, dst in zip((wg_ref, wu_ref, wd_ref), w16_refs):
                    dst[0] = src[0].astype(BF16)

        row = lax.broadcasted_iota(I32, (xs_ref.shape[0] // ROW_LINES, 1), 0)
        x = jnp.where(row < nvalid_ref[i], _load_rows3(xs_ref), 0.0).astype(BF16)
        gate = jnp.dot(x, weights[0][0], preferred_element_type=F32)
        up = jnp.dot(x, weights[1][0], preferred_element_type=F32)
        hid = (jax.nn.silu(gate) * up).astype(BF16)
        _store_rows3(yb_ref, jnp.dot(hid, weights[2][0], preferred_element_type=F32))


def _moe(block_e, n_valid, n_used, xs, experts, block_rows):
    n_blocks = xs.shape[0] // (block_rows * ROW_LINES)
    cast = experts[0].dtype != BF16

    def rows(i, be, nv, nu):
        return (jnp.minimum(i, nu[0] - 1), 0)

    def expert(i, be, nv, nu):
        return (be[jnp.minimum(i, nu[0] - 1)], 0, 0)

    w_specs = [pl.BlockSpec((1,) + w.shape[1:], expert) for w in experts]
    out_specs = [_rows3_spec(block_rows, lambda i, be, nv, nu: (i, 0))]
    out_shape = [jax.ShapeDtypeStruct(xs.shape, I32)]
    if cast:
        out_specs += w_specs
        out_shape += [jax.ShapeDtypeStruct(w.shape, BF16) for w in experts]
    yb, *w16 = pl.pallas_call(
        _moe_kernel,
        grid_spec=pltpu.PrefetchScalarGridSpec(
            num_scalar_prefetch=3,
            grid=(n_blocks,),
            in_specs=[_rows3_spec(block_rows, rows)] + w_specs,
            out_specs=out_specs,
        ),
        out_shape=out_shape,
        compiler_params=pltpu.CompilerParams(dimension_semantics=("arbitrary",),
                                             vmem_limit_bytes=VMEM_LIMIT),
        name="moe",
    )(block_e, n_valid, n_used, xs, *experts)
    return yb, (tuple(w16) if cast else experts)


def _gather(yb, dests):
    chunk = dests[0].shape[1]
    t = dests[0].shape[0] * chunk
    per_worker = t // SC_WORKERS
    n_chunks = per_worker // chunk
    idx = pltpu.VMEM((n_chunks, chunk), I32)
    out = jax.ShapeDtypeStruct((t,) + ROW3, I32)

    @functools.partial(
        pl.kernel, mesh=_sc_mesh(), out_type=(out, out),
        scratch_types=[idx, idx, pltpu.VMEM((chunk,) + ROW3, I32), pltpu.SemaphoreType.DMA])
    def gather_rows(yb_hbm, d0_hbm, d1_hbm, y0_hbm, y1_hbm, i0_v, i1_v, rows_v, sem):
        w = _sc_worker()
        pltpu.sync_copy(d0_hbm.at[pl.ds(w * n_chunks, n_chunks)], i0_v)
        pltpu.sync_copy(d1_hbm.at[pl.ds(w * n_chunks, n_chunks)], i1_v)

        @pl.loop(0, n_chunks)
        def _(j):
            rows = pl.ds(w * per_worker + j * chunk, chunk)
            for i_v, y_hbm in ((i0_v, y0_hbm), (i1_v, y1_hbm)):
                pltpu.async_copy(yb_hbm.at[i_v.at[j]], rows_v, sem).wait()
                pltpu.sync_copy(rows_v, y_hbm.at[rows])

    return gather_rows(yb, *dests)


def _combine_kernel(x1_ref, gate_ref, y0_ref, y1_ref, out_ref):
    gate = gate_ref[...]
    out_ref[...] = x1_ref[...] + (_load_rows3(y0_ref) * gate[:, 0:1] + _load_rows3(y1_ref) * gate[:, 1:2])


def _combine(x1, gate, y0, y1):
    t = x1.shape[0]
    tm = TM_ROW
    return pl.pallas_call(
        _combine_kernel,
        grid=(t // tm,),
        in_specs=[pl.BlockSpec((tm, D_MODEL), lambda i: (i, 0)),
                  pl.BlockSpec((tm, LANES), lambda i: (i, 0)),
                  _rows3_spec(tm, lambda i: (i, 0)), _rows3_spec(tm, lambda i: (i, 0))],
        out_specs=pl.BlockSpec((tm, D_MODEL), lambda i: (i, 0)),
        out_shape=jax.ShapeDtypeStruct((t, D_MODEL), F32),
        compiler_params=pltpu.CompilerParams(dimension_semantics=("parallel",)),
        name="combine",
    )(x1, gate, y0, y1)


def _rope_tables(seq):
    half = HEAD_DIM // 2
    inv_freq = ROPE_THETA ** (-jnp.arange(half, dtype=F32) / half)
    coarse = (jnp.arange(seq // ROPE_SPLIT) * ROPE_SPLIT).astype(F32)[:, None, None] * inv_freq
    fine = jnp.arange(ROPE_SPLIT).astype(F32)[None, :, None] * inv_freq
    cos = (jnp.cos(coarse) * jnp.cos(fine) - jnp.sin(coarse) * jnp.sin(fine)).reshape(seq, half)
    sin = (jnp.sin(coarse) * jnp.cos(fine) + jnp.cos(coarse) * jnp.sin(fine)).reshape(seq, half)
    return jnp.concatenate([cos, cos], axis=-1), jnp.concatenate([-sin, sin], axis=-1)


def _layer(x, p, seq, rope, experts):
    t = x.shape[0]
    ma, sgb, q, k, v = _inproj(x, p, *rope, seq)
    x1, hn, route, gate, counts_f = _attn(x, ma, sgb, q, k, v, p, seq)

    block_rows = _moe_block_rows(t)
    counts = counts_f[:, 0].astype(I32)
    min_blocks = 1 if experts[0].dtype != BF16 else 0
    padded = jnp.maximum((counts + block_rows - 1) // block_rows, min_blocks) * block_rows
    pad_ends = jnp.cumsum(padded)
    pad_starts = pad_ends - padded
    n_blocks = (t * TOP_K) // block_rows + N_EXPERTS
    block_start = jnp.arange(n_blocks, dtype=I32) * block_rows
    in_expert = jnp.logical_and(block_start[:, None] >= pad_starts[None, :],
                                block_start[:, None] < pad_ends[None, :]).astype(I32)
    block_e = jnp.minimum(jnp.sum((block_start[:, None] >= pad_ends[None, :]).astype(I32), axis=1), N_EXPERTS - 1)
    n_valid = jnp.sum(in_expert * jnp.clip(pad_starts + counts - block_start[:, None], 0, block_rows), axis=1)
    n_used = pad_ends[-1:] // block_rows
    starts_col = jnp.broadcast_to(pad_starts.astype(F32)[:, None], (N_EXPERTS, LANES))

    dest = _dest(route, starts_col)
    dests = [dest[k].reshape(t // _sc_chunk(t), _sc_chunk(t)) for k in range(TOP_K)]
    xs = _dispatch(_as_rows3(hn), dests, n_blocks * block_rows)
    yb, experts = _moe(block_e, n_valid, n_used, _as_lines(xs), experts, block_rows)
    y0, y1 = _gather(_as_rows3(yb), dests)
    return _combine(x1, gate, _as_lines(y0), _as_lines(y1)), experts


def kernel(x_prompt, x_sample, norm_mix_g, w_in, norm_v_g, w_spatial, b_spatial, q_norm_g, k_norm_g, sink,
           w_proj_a, w_proj_b, w_out, norm_ffn_g, w_router_group, b_router_group, w_router_expert,
           b_router_expert, w_gate_e, w_up_e, w_down_e):
    depth = w_in.shape[0]
    layers = []
    for l in range(depth):
        w_router = jnp.zeros((ROUTER_ROWS, D_MODEL), F32)
        w_router = w_router.at[:N_GROUPS].set(w_router_group[l].T)
        w_router = w_router.at[SUBLANES:SUBLANES + N_EXPERTS].set(w_router_expert[l].T)
        b_router = jnp.zeros((ROUTER_ROWS,), F32)
        b_router = b_router.at[:N_GROUPS].set(b_router_group[l])
        b_router = b_router.at[SUBLANES:SUBLANES + N_EXPERTS].set(b_router_expert[l])
        b_router = jnp.broadcast_to(b_router[:, None], (ROUTER_ROWS, LANES))
        layers.append(dict(
            norm_mix_g=norm_mix_g[l][None], w_in=w_in[l].astype(BF16), norm_v_g=norm_v_g[l][None],
            w_spatial=w_spatial[l].astype(BF16),
            b_spatial=jnp.broadcast_to(b_spatial[l][:, :, None], (A_GROUPS, CHUNK, LANES)),
            q_norm_g=q_norm_g[l][None], k_norm_g=k_norm_g[l][None], sink=sink[l],
            w_proj_a=w_proj_a[l].astype(BF16), w_proj_b=w_proj_b[l].astype(BF16), w_out=w_out[l].astype(BF16),
            norm_ffn_g=norm_ffn_g[l][None], w_router=w_router.astype(BF16), b_router=b_router,
            experts=(w_gate_e[l], w_up_e[l], w_down_e[l])))

    trunks = [x_prompt, x_sample]
    rows = [x.reshape(-1, D_MODEL) for x in trunks]
    order = sorted(range(len(trunks)), key=lambda n: -rows[n].shape[0])
    rope = _rope_tables(max(x.shape[1] for x in trunks))
    for p in layers:
        experts = p["experts"]
        for n in order:
            rows[n], experts = _layer(rows[n], p, trunks[n].shape[1], rope, experts)
    return tuple(r.reshape(x.shape) for r, x in zip(rows, trunks))
```

```python
import functools

import jax
import jax.numpy as jnp
from jax import lax
from jax.experimental import pallas as pl
from jax.experimental.pallas import tpu as pltpu
from jax.experimental.pallas import tpu_sc as plsc

F32 = jnp.float32
BF16 = jnp.bfloat16
I32 = jnp.int32
U32 = jnp.uint32

LANES = 128
SUBLANES = 8
VMEM_BYTES_V7X = 64 * 1024 * 1024
SC_CORES = 2
SC_SUBCORES = 16
SC_WORKERS = SC_CORES * SC_SUBCORES
SC_CHUNK_MAX = 128

D_MODEL = 1024
A_WIDTH = D_MODEL
A_GROUPS = 8
CHUNK = 128
HEAD_DIM = 128
N_Q_HEADS = D_MODEL // HEAD_DIM
N_KV_HEADS = 2
REP = N_Q_HEADS // N_KV_HEADS
WINDOW = 128
ROPE_THETA = 10000.0
ROPE_SPLIT = 64
Q_W = N_Q_HEADS * HEAD_DIM
KV_W = N_KV_HEADS * HEAD_DIM
IN_W = 2 * A_WIDTH + Q_W + 2 * KV_W + 2 * D_MODEL
COL_U = 0
COL_V = COL_U + A_WIDTH
COL_Q = COL_V + A_WIDTH
COL_K = COL_Q + Q_W
COL_VA = COL_K + KV_W
COL_GA = COL_VA + KV_W
COL_GB = COL_GA + D_MODEL
N_GROUPS = 4
EXPERTS_PER_GROUP = 8
N_EXPERTS = N_GROUPS * EXPERTS_PER_GROUP
TOP_K = 2
D_EXPERT = 512
EPS = 1e-6
NEG = -1e30

TM_IN = 512
TM_IN_SUB = 256
TQ = 512
DENSE_COLS = 256
ROUTER_ROWS = 64
assert EXPERTS_PER_GROUP == SUBLANES and SUBLANES + N_EXPERTS <= ROUTER_ROWS
TM_ROW = 1024
TM_DEST = 2048
MOE_ROWS_MAX = 1024
MOE_MIN_BLOCKS = 4
VMEM_RESERVE = 8 * 1024 * 1024
VMEM_LIMIT = VMEM_BYTES_V7X - VMEM_RESERVE


def _rms(x, g):
    return x * lax.rsqrt(jnp.mean(x * x, axis=-1, keepdims=True) + EPS) * g


ROW_LINES = D_MODEL // 2 // LANES
ROW3 = (ROW_LINES, LANES)
HIGH_HALF = 0xFFFF0000


def _as_rows3(a):
    return a.reshape((a.shape[0] // ROW_LINES,) + ROW3)


def _as_lines(a):
    return a.reshape((a.shape[0] * ROW_LINES, LANES))


def _store_rows3(lines_ref, val):
    rows = val.shape[0]
    bits = lax.bitcast_convert_type(val.astype(BF16).astype(F32), U32)
    half = ROW_LINES * LANES
    for s in range(ROW_LINES):
        lo = bits[:, s * LANES:(s + 1) * LANES] >> 16
        hi = bits[:, half + s * LANES:half + (s + 1) * LANES] & U32(HIGH_HALF)
        lines_ref[pl.ds(s, rows, stride=ROW_LINES), :] = lax.bitcast_convert_type(lo | hi, I32)


def _load_rows3(lines_ref):
    rows = lines_ref.shape[0] // ROW_LINES
    words = [lax.bitcast_convert_type(lines_ref[pl.ds(s, rows, stride=ROW_LINES), :], U32)
             for s in range(ROW_LINES)]
    lo = [lax.bitcast_convert_type(w << 16, F32) for w in words]
    hi = [lax.bitcast_convert_type(w & U32(HIGH_HALF), F32) for w in words]
    return jnp.concatenate(lo + hi, axis=1)


def _rows3_spec(rows, index_map):
    return pl.BlockSpec((rows * ROW_LINES, LANES), index_map)


def _const_spec(shape):
    nd = len(shape)
    return pl.BlockSpec(shape, lambda *_: (0,) * nd, pipeline_mode=pl.Buffered(1))


def _inproj_kernel(x_ref, gmix_ref, win_ref, gv_ref, ws_ref, bs_ref, gq_ref, gk_ref, cos_ref, sin_ref,
                   wpa_ref, ma_ref, sgb_ref, q_ref, k_ref, v_ref, h_scr, u_scr, vn_scr, a_scr):
    tm = x_ref.shape[0]

    def stages(rows):
        def proj(lo, width):
            return jnp.dot(h_scr[rows], win_ref[:, lo:lo + width], preferred_element_type=F32)

        def norm_rope(z, g):
            zn = _rms(z, g)
            return zn * cos_ref[rows] + pltpu.roll(zn, HEAD_DIM // 2, 1) * sin_ref[rows]

        def norm():
            h_scr[rows] = _rms(x_ref[rows], gmix_ref[...]).astype(BF16)

        def mix_v():
            vn_scr[rows] = _rms(jax.nn.gelu(proj(COL_V, A_WIDTH)), gv_ref[...]).astype(BF16)

        def mix_u():
            u_scr[rows] = jax.nn.gelu(proj(COL_U, A_WIDTH))

        def spatial():
            for c in range(rows.start, rows.stop, CHUNK):
                chunk = slice(c, c + CHUNK)
                for g in range(A_GROUPS):
                    cols = slice(g * LANES, (g + 1) * LANES)
                    mixed = jnp.dot(ws_ref[g], vn_scr[chunk, cols], preferred_element_type=F32) + bs_ref[g]
                    a_scr[chunk, cols] = (u_scr[chunk, cols] * mixed).astype(BF16)

        def gate_a():
            ya = jnp.dot(a_scr[rows], wpa_ref[...], preferred_element_type=F32)
            ma_ref[rows] = jax.nn.sigmoid(proj(COL_GA, D_MODEL)) * ya

        def gate_b():
            sgb_ref[rows] = jax.nn.sigmoid(proj(COL_GB, D_MODEL))

        def queries():
            qz = proj(COL_Q, Q_W)
            for hd in range(N_Q_HEADS):
                cols = slice(hd * HEAD_DIM, (hd + 1) * HEAD_DIM)
                q_ref[rows, cols] = norm_rope(qz[:, cols], gq_ref[...]).astype(BF16)

        def keys_values():
            kz = proj(COL_K, KV_W)
            for hd in range(N_KV_HEADS):
                cols = slice(hd * HEAD_DIM, (hd + 1) * HEAD_DIM)
                k_ref[rows, cols] = norm_rope(kz[:, cols], gk_ref[...]).astype(BF16)
            v_ref[rows] = proj(COL_VA, KV_W).astype(BF16)

        return [norm, mix_v, mix_u, spatial, gate_a, gate_b, queries, keys_values]

    subs = [stages(slice(r, r + TM_IN_SUB)) for r in range(0, tm, TM_IN_SUB)]
    n_stage = len(subs[0])
    for step in range(n_stage + len(subs) - 1):
        for n, sub in enumerate(subs):
            if 0 <= step - n < n_stage:
                sub[step - n]()


def _inproj(x, p, cos, sin, seq):
    t = x.shape[0]
    tm = TM_IN
    n_pos = seq // tm
    row = lambda w: pl.BlockSpec((tm, w), lambda i: (i, 0))
    pos = pl.BlockSpec((tm, HEAD_DIM), lambda i: (i % n_pos, 0))
    return pl.pallas_call(
        _inproj_kernel,
        grid=(t // tm,),
        in_specs=[row(D_MODEL), _const_spec((1, D_MODEL)), _const_spec((D_MODEL, IN_W)),
                  _const_spec((1, A_WIDTH)), _const_spec((A_GROUPS, CHUNK, CHUNK)),
                  _const_spec((A_GROUPS, CHUNK, LANES)), _const_spec((1, HEAD_DIM)),
                  _const_spec((1, HEAD_DIM)), pos, pos, _const_spec((A_WIDTH, D_MODEL))],
        out_specs=[row(D_MODEL), row(D_MODEL), row(Q_W), row(KV_W), row(KV_W)],
        out_shape=[jax.ShapeDtypeStruct((t, D_MODEL), F32), jax.ShapeDtypeStruct((t, D_MODEL), F32),
                   jax.ShapeDtypeStruct((t, Q_W), BF16), jax.ShapeDtypeStruct((t, KV_W), BF16),
                   jax.ShapeDtypeStruct((t, KV_W), BF16)],
        scratch_shapes=[pltpu.VMEM((tm, D_MODEL), BF16), pltpu.VMEM((tm, A_WIDTH), F32),
                        pltpu.VMEM((tm, A_WIDTH), BF16), pltpu.VMEM((tm, A_WIDTH), BF16)],
        compiler_params=pltpu.CompilerParams(dimension_semantics=("parallel",),
                                             vmem_limit_bytes=VMEM_LIMIT),
        name="inproj",
    )(x, p["norm_mix_g"], p["w_in"], p["norm_v_g"], p["w_spatial"], p["b_spatial"], p["q_norm_g"],
      p["k_norm_g"], cos, sin, p["w_proj_a"])


def _attn_kernel(sink_ref, x_ref, ma_ref, sgb_ref, q_ref, kp_ref, kc_ref, kn_ref, vp_ref, vc_ref, vn_ref,
                 wpb_ref, wout_ref, gffn_ref, wr_ref, br_ref,
                 x1_ref, hn_ref, route_ref, gate_ref, counts_ref, kcat, vcat, o_scr, s_scr, p_scr, sink_scr, m_scr,
                 *, tiles_per_seq, n_tiles):
    tq = x_ref.shape[0]
    blk = WINDOW
    i = pl.program_id(0)
    slot = i % 2

    @pl.when(i == 0)
    def _():
        o_scr[...] = jnp.zeros_like(o_scr)
        counts_ref[...] = jnp.zeros_like(counts_ref)

    pos_tile = jnp.minimum(i, n_tiles - 1) % tiles_per_seq
    has_prev = pos_tile > 0
    has_next = pos_tile < tiles_per_seq - 1

    kcat[0:blk] = kp_ref[...]
    kcat[blk:blk + tq] = kc_ref[...]
    kcat[blk + tq:] = kn_ref[...]
    vcat[0:blk] = vp_ref[...]
    vcat[blk:blk + tq] = vc_ref[...]
    vcat[blk + tq:] = vn_ref[...]

    qr = lax.broadcasted_iota(I32, (blk, blk), 0)
    kc = lax.broadcasted_iota(I32, (blk, blk), 1)
    scale = HEAD_DIM ** -0.5
    n_sub = tq // blk
    pairs =[(j, g) for j in range(n_sub) for g in range(N_KV_HEADS)]

    def keys(ref, j, g):
        return ref[j * blk:(j + 3) * blk, g * HEAD_DIM:(g + 1) * HEAD_DIM]

    def head_cols(g, r):
        hd = g * REP + r
        return slice(hd * HEAD_DIM, (hd + 1) * HEAD_DIM)

    for b, (j, g) in enumerate(pairs):
        rows = slice(j * blk, (j + 1) * blk)
        qs = jnp.concatenate([q_ref[rows, head_cols(g, r)] for r in range(REP)], axis=0)
        s_scr[b] = lax.dot_general(qs, keys(kcat, j, g), (((1,), (1,)), ((), ())),
                                   preferred_element_type=F32)
    log2e = 1.4426950408889634

    def softmax(b, r):
        j, g = pairs[b]
        hrows = slice(r * blk, (r + 1) * blk)
        z = s_scr[b, hrows, :] * (scale * log2e)
        lo_ok = kc >= (qr + jnp.where(has_prev, 0, blk) if j == 0 else qr)
        hi_ok = kc <= (qr - jnp.where(has_next, 0, blk) if j == n_sub - 1 else qr)
        z = jnp.concatenate([jnp.where(lo_ok, z[:, :blk], NEG), z[:, blk:2 * blk],
                             jnp.where(hi_ok, z[:, 2 * blk:], NEG)], axis=1)
        sink = sink_ref[g * REP + r] * log2e
        m = jnp.maximum(jnp.max(z, axis=-1, keepdims=True), sink)
        p_scr[b, hrows, :] = jnp.exp2(z - m).astype(BF16)
        sink_scr[b, hrows, :] = jnp.broadcast_to(jnp.exp2(sink - m), (blk, LANES))

    def values(b):
        j, g = pairs[b]
        rows = slice(j * blk, (j + 1) * blk)
        v_ext = jnp.concatenate([keys(vcat, j, g), jnp.ones((3 * blk, HEAD_DIM), BF16)], axis=1)
        acc = jnp.dot(p_scr[b], v_ext, preferred_element_type=F32)
        o = (acc[:, :HEAD_DIM] / (acc[:, HEAD_DIM:] + sink_scr[b])).astype(BF16)
        for r in range(REP):
            o_scr[slot, rows, head_cols(g, r)] = o[r * blk:(r + 1) * blk, :]

    def merged_cols(cols):
        yb = jnp.dot(o_scr[1 - slot], wpb_ref[:, cols], preferred_element_type=F32)
        m_scr[:, cols] = (ma_ref[:, cols] + sgb_ref[:, cols] * yb).astype(BF16)

    def x1_cols(cols):
        x1_ref[:, cols] = x_ref[:, cols] + jnp.dot(m_scr[...], wout_ref[:, cols], preferred_element_type=F32)

    col_chunks = [slice(c * DENSE_COLS, (c + 1) * DENSE_COLS) for c in range(D_MODEL // DENSE_COLS)]
    dense = [functools.partial(f, cols) for f in (merged_cols, x1_cols) for cols in col_chunks]
    units = [(b, r) for b in range(len(pairs)) for r in range(REP)]
    units_per_dense = len(units) // len(dense)
    for n, (b, r) in enumerate(units):
        softmax(b, r)
        if (n + 1) % units_per_dense == 0:
            dense[(n + 1) // units_per_dense - 1]()
        if r == REP - 1:
            values(b)

    hn = _rms(x1_ref[...], gffn_ref[...])
    _store_rows3(hn_ref, hn)
    def wide(a):
        return jnp.concatenate([a] * (tq // LANES), axis=1)

    logits = lax.dot_general(wr_ref[...], hn.astype(BF16), (((1,), (1,)), ((), ())),
                             preferred_element_type=F32) + wide(br_ref[...])
    sub = lax.broadcasted_iota(I32, (SUBLANES, tq), 0).astype(F32)
    ninf = -jnp.inf

    def cmax(a):
        return jnp.max(a, axis=0, keepdims=True)

    def csum(a):
        return jnp.sum(a, axis=0, keepdims=True)

    def first_row(mask):
        return jnp.min(jnp.where(mask, sub, float(SUBLANES)), axis=0, keepdims=True)

    def group_rows(g):
        return logits[(g + 1) * SUBLANES:(g + 2) * SUBLANES]

    gl = jnp.where(sub < N_GROUPS, logits[0:SUBLANES], ninf)
    gmax = cmax(gl)
    g_sel = first_row(gl == gmax)
    g_p = 1.0 / csum(jnp.exp(gl - gmax))
    el = group_rows(0)
    for g in range(1, N_GROUPS):
        el = jnp.where(g_sel == g, group_rows(g), el)
    ee = jnp.exp(el - cmax(el))
    eprob = ee / csum(ee)
    p1 = cmax(eprob)
    i1 = first_row(eprob == p1)
    eprob2 = jnp.where(sub == i1, -1.0, eprob)
    p2 = cmax(eprob2)
    i2 = first_row(eprob2 == p2)
    psum = p1 + p2
    w1 = g_p * p1 / psum
    w2 = g_p * p2 / psum
    e1 = g_sel * EXPERTS_PER_GROUP + i1
    e2 = g_sel * EXPERTS_PER_GROUP + i2

    erow = lax.broadcasted_iota(I32, (N_EXPERTS, tq), 0).astype(F32)
    oh1 = erow == e1
    oh2 = erow == e2
    cnt = (jnp.where(oh1, 1.0, 0.0) + jnp.where(oh2, 1.0, 0.0)) * jnp.where(i > 0, 1.0, 0.0)
    ri = lax.broadcasted_iota(I32, (tq, tq), 0)
    ci = lax.broadcasted_iota(I32, (tq, tq), 1)
    earlier = jnp.where(ri < ci, 1.0, 0.0).astype(BF16)
    base = wide(counts_ref[...]) + jnp.dot(cnt.astype(BF16), earlier, preferred_element_type=F32)
    r1 = csum(jnp.where(oh1, base, 0.0))
    r2 = csum(jnp.where(oh2, base, 0.0))
    counts_ref[...] = counts_ref[...] + jnp.sum(cnt, axis=1, keepdims=True)

    route = jnp.where(sub == 0.0, e1, jnp.where(sub == 1.0, e2,
                      jnp.where(sub == 2.0, r1, jnp.where(sub == 3.0, r2, 0.0))))
    route_ref[...] = route.astype(I32)
    gates = jnp.where(sub == 0.0, w1, jnp.where(sub == 1.0, w2, 0.0))
    gate_ref[...] = jnp.concatenate([gates, jnp.zeros((LANES - SUBLANES, tq), F32)], axis=0).T


def _attn(x, ma, sgb, q, k, v, p, seq):
    t = x.shape[0]
    tq = TQ
    sub = tq // WINDOW
    last_blk = t // WINDOW - 1
    n_tiles = t // tq
    att = lambda i: jnp.minimum(i, n_tiles - 1)
    post = lambda i: jnp.maximum(i - 1, 0)
    att_row = lambda w: pl.BlockSpec((tq, w), lambda i: (att(i), 0))
    row = lambda w: pl.BlockSpec((tq, w), lambda i: (post(i), 0))
    prev = pl.BlockSpec((WINDOW, KV_W), lambda i: (jnp.maximum(att(i) * sub - 1, 0), 0))
    nxt = pl.BlockSpec((WINDOW, KV_W), lambda i: (jnp.minimum((att(i) + 1) * sub, last_blk), 0))
    return pl.pallas_call(
        functools.partial(_attn_kernel, tiles_per_seq=seq // tq, n_tiles=n_tiles),
        grid=(n_tiles + 1,),
        in_specs=[pl.BlockSpec(memory_space=pltpu.SMEM),
                  row(D_MODEL), row(D_MODEL), row(D_MODEL), att_row(Q_W),
                  prev, att_row(KV_W), nxt, prev, att_row(KV_W), nxt,
                  _const_spec((Q_W, D_MODEL)), _const_spec((D_MODEL, D_MODEL)), _const_spec((1, D_MODEL)),
                  _const_spec((ROUTER_ROWS, D_MODEL)), _const_spec((ROUTER_ROWS, LANES))],
        out_specs=[row(D_MODEL), _rows3_spec(tq, lambda i: (post(i), 0)),
                   pl.BlockSpec((SUBLANES, tq), lambda i: (0, post(i))), row(LANES),
                   pl.BlockSpec((N_EXPERTS, LANES), lambda i: (0, 0))],
        out_shape=[jax.ShapeDtypeStruct((t, D_MODEL), F32), jax.ShapeDtypeStruct((t * ROW_LINES, LANES), I32),
                   jax.ShapeDtypeStruct((SUBLANES, t), I32), jax.ShapeDtypeStruct((t, LANES), F32),
                   jax.ShapeDtypeStruct((N_EXPERTS, LANES), F32)],
        scratch_shapes=[pltpu.VMEM((tq + 2 * WINDOW, KV_W), BF16), pltpu.VMEM((tq + 2 * WINDOW, KV_W), BF16),
                        pltpu.VMEM((2, tq, Q_W), BF16),
                        pltpu.VMEM((sub * N_KV_HEADS, REP * WINDOW, 3 * WINDOW), F32),
                        pltpu.VMEM((sub * N_KV_HEADS, REP * WINDOW, 3 * WINDOW), BF16),
                        pltpu.VMEM((sub * N_KV_HEADS, REP * WINDOW, LANES), F32),
                        pltpu.VMEM((tq, D_MODEL), BF16)],
        compiler_params=pltpu.CompilerParams(dimension_semantics=("arbitrary",),
                                             vmem_limit_bytes=VMEM_LIMIT),
        name="attn",
    )(p["sink"], x, ma, sgb, q, k, k, k, v, v, v, p["w_proj_b"], p["w_out"], p["norm_ffn_g"],
      p["w_router"], p["b_router"])


def _dest_kernel(route_ref, starts_ref, dest_ref):
    route = route_ref[...].astype(F32)
    td = route.shape[1]
    sub = lax.broadcasted_iota(I32, route.shape, 0)
    erow = lax.broadcasted_iota(I32, (N_EXPERTS, td), 0).astype(F32)
    starts = jnp.broadcast_to(starts_ref[:, 0:1], (N_EXPERTS, td))

    def slot(k):
        start = jnp.sum(jnp.where(erow == route[k:k + 1], starts, 0.0), axis=0, keepdims=True)
        return start + route[TOP_K + k:TOP_K + k + 1]

    dest_ref[...] = jnp.where(sub == 0, slot(0), jnp.where(sub == 1, slot(1), 0.0)).astype(I32)


def _dest(route, pad_starts):
    t = route.shape[1]
    td = min(TM_DEST, t)
    return pl.pallas_call(
        _dest_kernel,
        grid=(t // td,),
        in_specs=[pl.BlockSpec((SUBLANES, td), lambda i: (0, i)), _const_spec((N_EXPERTS, LANES))],
        out_specs=pl.BlockSpec((SUBLANES, td), lambda i: (0, i)),
        out_shape=jax.ShapeDtypeStruct((SUBLANES, t), I32),
        compiler_params=pltpu.CompilerParams(dimension_semantics=("parallel",)),
        name="dest",
    )(route, pad_starts)


def _sc_mesh():
    return plsc.VectorSubcoreMesh(core_axis_name="c", subcore_axis_name="s")


def _sc_worker():
    return lax.axis_index("s") * SC_CORES + lax.axis_index("c")


def _sc_chunk(t):
    return min(SC_CHUNK_MAX, t // (SC_WORKERS * SUBLANES))


def _dispatch(hn, dests, n_rows):
    t = hn.shape[0]
    chunk = dests[0].shape[1]
    per_worker = t // SC_WORKERS
    n_chunks = per_worker // chunk
    idx = pltpu.VMEM((n_chunks, chunk), I32)

    @functools.partial(
        pl.kernel, mesh=_sc_mesh(), out_type=jax.ShapeDtypeStruct((n_rows,) + ROW3, I32),
        scratch_types=[idx, idx, pltpu.VMEM((chunk,) + ROW3, I32), pltpu.SemaphoreType.DMA])
    def scatter_rows(hn_hbm, d0_hbm, d1_hbm, xs_hbm, i0_v, i1_v, rows_v, sem):
        w = _sc_worker()
        pltpu.sync_copy(d0_hbm.at[pl.ds(w * n_chunks, n_chunks)], i0_v)
        pltpu.sync_copy(d1_hbm.at[pl.ds(w * n_chunks, n_chunks)], i1_v)

        @pl.loop(0, n_chunks)
        def _(j):
            pltpu.sync_copy(hn_hbm.at[pl.ds(w * per_worker + j * chunk, chunk)], rows_v)
            copies = [pltpu.make_async_copy(rows_v, xs_hbm.at[i_v.at[j]], sem) for i_v in (i0_v, i1_v)]
            for cp in copies:
                cp.start()
            for cp in copies:
                cp.wait()

    return scatter_rows(hn, *dests)


def _moe_block_rows(t):
    rows = MOE_ROWS_MAX
    while rows > CHUNK and (t * TOP_K) // N_EXPERTS < MOE_MIN_BLOCKS * rows:
        rows //= 2
    return rows


def _moe_kernel(be_ref, nvalid_ref, nused_ref, xs_ref, wg_ref, wu_ref, wd_ref, yb_ref, *w16_refs):
    i = pl.program_id(0)
    used = i < nused_ref[0]
    weights = w16_refs if w16_refs else (wg_ref, wu_ref, wd_ref)

    @pl.when(jnp.logical_not(used))
    def _():
        yb_ref[...] = jnp.zeros_like(yb_ref)

    @pl.when(used)
    def _():
        if w16_refs:
            @pl.when(jnp.logical_or(i == 0, be_ref[i] != be_ref[jnp.maximum(i - 1, 0)]))
            def _():
                for src, dst in zip((wg_ref, wu_ref, wd_ref), w16_refs):
                    dst[0] = src[0].astype(BF16)

        row = lax.broadcasted_iota(I32, (xs_ref.shape[0] // ROW_LINES, 1), 0)
        x = jnp.where(row < nvalid_ref[i], _load_rows3(xs_ref), 0.0).astype(BF16)
        gate = jnp.dot(x, weights[0][0], preferred_element_type=F32)
        up = jnp.dot(x, weights[1][0], preferred_element_type=F32)
        hid = (jax.nn.silu(gate) * up).astype(BF16)
        _store_rows3(yb_ref, jnp.dot(hid, weights[2][0], preferred_element_type=F32))


def _moe(block_e, n_valid, n_used, xs, experts, block_rows):
    n_blocks = xs.shape[0] // (block_rows * ROW_LINES)
    cast = experts[0].dtype != BF16

    def rows(i, be, nv, nu):
        return (jnp.minimum(i, nu[0] - 1), 0)

    def expert(i, be, nv, nu):
        return (be[jnp.minimum(i, nu[0] - 1)], 0, 0)

    w_specs = [pl.BlockSpec((1,) + w.shape[1:], expert) for w in experts]
    out_specs = [_rows3_spec(block_rows, lambda i, be, nv, nu: (i, 0))]
    out_shape = [jax.ShapeDtypeStruct(xs.shape, I32)]
    if cast:
        out_specs += w_specs
        out_shape += [jax.ShapeDtypeStruct(w.shape, BF16) for w in experts]
    yb, *w16 = pl.pallas_call(
        _moe_kernel,
        grid_spec=pltpu.PrefetchScalarGridSpec(
            num_scalar_prefetch=3,
            grid=(n_blocks,),
            in_specs=[_rows3_spec(block_rows, rows)] + w_specs,
            out_specs=out_specs,
        ),
        out_shape=out_shape,
        compiler_params=pltpu.CompilerParams(dimension_semantics=("arbitrary",),
                                             vmem_limit_bytes=VMEM_LIMIT),
        name="moe",
    )(block_e, n_valid, n_used, xs, *experts)
    return yb, (tuple(w16) if cast else experts)


def _gather(yb, dests):
    chunk = dests[0].shape[1]
    t = dests[0].shape[0] * chunk
    per_worker = t // SC_WORKERS
    n_chunks = per_worker // chunk
    idx = pltpu.VMEM((n_chunks, chunk), I32)
    out = jax.ShapeDtypeStruct((t,) + ROW3, I32)

    @functools.partial(
        pl.kernel, mesh=_sc_mesh(), out_type=(out, out),
        scratch_types=[idx, idx, pltpu.VMEM((chunk,) + ROW3, I32), pltpu.SemaphoreType.DMA])
    def gather_rows(yb_hbm, d0_hbm, d1_hbm, y0_hbm, y1_hbm, i0_v, i1_v, rows_v, sem):
        w = _sc_worker()
        pltpu.sync_copy(d0_hbm.at[pl.ds(w * n_chunks, n_chunks)], i0_v)
        pltpu.sync_copy(d1_hbm.at[pl.ds(w * n_chunks, n_chunks)], i1_v)

        @pl.loop(0, n_chunks)
        def _(j):
            rows = pl.ds(w * per_worker + j * chunk, chunk)
            for i_v, y_hbm in ((i0_v, y0_hbm), (i1_v, y1_hbm)):
                pltpu.async_copy(yb_hbm.at[i_v.at[j]], rows_v, sem).wait()
                pltpu.sync_copy(rows_v, y_hbm.at[rows])

    return gather_rows(yb, *dests)


def _combine_kernel(x1_ref, gate_ref, y0_ref, y1_ref, out_ref):
    gate = gate_ref[...]
    out_ref[...] = x1_ref[...] + (_load_rows3(y0_ref) * gate[:, 0:1] + _load_rows3(y1_ref) * gate[:, 1:2])


def _combine(x1, gate, y0, y1):
    t = x1.shape[0]
    tm = TM_ROW
    return pl.pallas_call(
        _combine_kernel,
        grid=(t // tm,),
        in_specs=[pl.BlockSpec((tm, D_MODEL), lambda i: (i, 0)),
                  pl.BlockSpec((tm, LANES), lambda i: (i, 0)),
                  _rows3_spec(tm, lambda i: (i, 0)), _rows3_spec(tm, lambda i: (i, 0))],
        out_specs=pl.BlockSpec((tm, D_MODEL), lambda i: (i, 0)),
        out_shape=jax.ShapeDtypeStruct((t, D_MODEL), F32),
        compiler_params=pltpu.CompilerParams(dimension_semantics=("parallel",)),
        name="combine",
    )(x1, gate, y0, y1)


def _rope_tables(seq):
    half = HEAD_DIM // 2
    inv_freq = ROPE_THETA ** (-jnp.arange(half, dtype=F32) / half)
    coarse = (jnp.arange(seq // ROPE_SPLIT) * ROPE_SPLIT).astype(F32)[:, None, None] * inv_freq
    fine = jnp.arange(ROPE_SPLIT).astype(F32)[None, :, None] * inv_freq
    cos = (jnp.cos(coarse) * jnp.cos(fine) - jnp.sin(coarse) * jnp.sin(fine)).reshape(seq, half)
    sin = (jnp.sin(coarse) * jnp.cos(fine) + jnp.cos(coarse) * jnp.sin(fine)).reshape(seq, half)
    return jnp.concatenate([cos, cos], axis=-1), jnp.concatenate([-sin, sin], axis=-1)


def _layer(x, p, seq, rope, experts):
    t = x.shape[0]
    ma, sgb, q, k, v = _inproj(x, p, *rope, seq)
    x1, hn, route, gate, counts_f = _attn(x, ma, sgb, q, k, v, p, seq)

    block_rows = _moe_block_rows(t)
    counts = counts_f[:, 0].astype(I32)
    min_blocks = 1 if experts[0].dtype != BF16 else 0
    padded = jnp.maximum((counts + block_rows - 1) // block_rows, min_blocks) * block_rows
    pad_ends = jnp.cumsum(padded)
    pad_starts = pad_ends - padded
    n_blocks = (t * TOP_K) // block_rows + N_EXPERTS
    block_start = jnp.arange(n_blocks, dtype=I32) * block_rows
    in_expert = jnp.logical_and(block_start[:, None] >= pad_starts[None, :],
                                block_start[:, None] < pad_ends[None, :]).astype(I32)
    block_e = jnp.minimum(jnp.sum((block_start[:, None] >= pad_ends[None, :]).astype(I32), axis=1), N_EXPERTS - 1)
    n_valid = jnp.sum(in_expert * jnp.clip(pad_starts + counts - block_start[:, None], 0, block_rows), axis=1)
    n_used = pad_ends[-1:] // block_rows
    starts_col = jnp.broadcast_to(pad_starts.astype(F32)[:, None], (N_EXPERTS, LANES))

    dest = _dest(route, starts_col)
    dests = [dest[k].reshape(t // _sc_chunk(t), _sc_chunk(t)) for k in range(TOP_K)]
    xs = _dispatch(_as_rows3(hn), dests, n_blocks * block_rows)
    yb, experts = _moe(block_e, n_valid, n_used, _as_lines(xs), experts, block_rows)
    y0, y1 = _gather(_as_rows3(yb), dests)
    return _combine(x1, gate, _as_lines(y0), _as_lines(y1)), experts


def kernel(x_prompt, x_sample, norm_mix_g, w_in, norm_v_g, w_spatial, b_spatial, q_norm_g, k_norm_g, sink,
           w_proj_a, w_proj_b, w_out, norm_ffn_g, w_router_group, b_router_group, w_router_expert,
           b_router_expert, w_gate_e, w_up_e, w_down_e):
    depth = w_in.shape[0]
    layers = []
    for l in range(depth):
        w_router = jnp.zeros((ROUTER_ROWS, D_MODEL), F32)
        w_router = w_router.at[:N_GROUPS].set(w_router_group[l].T)
        w_router = w_router.at[SUBLANES:SUBLANES + N_EXPERTS].set(w_router_expert[l].T)
        b_router = jnp.zeros((ROUTER_ROWS,), F32)
        b_router = b_router.at[:N_GROUPS].set(b_router_group[l])
        b_router = b_router.at[SUBLANES:SUBLANES + N_EXPERTS].set(b_router_expert[l])
        b_router = jnp.broadcast_to(b_router[:, None], (ROUTER_ROWS, LANES))
        layers.append(dict(
            norm_mix_g=norm_mix_g[l][None], w_in=w_in[l].astype(BF16), norm_v_g=norm_v_g[l][None],
            w_spatial=w_spatial[l].astype(BF16),
            b_spatial=jnp.broadcast_to(b_spatial[l][:, :, None], (A_GROUPS, CHUNK, LANES)),
            q_norm_g=q_norm_g[l][None], k_norm_g=k_norm_g[l][None], sink=sink[l],
            w_proj_a=w_proj_a[l].astype(BF16), w_proj_b=w_proj_b[l].astype(BF16), w_out=w_out[l].astype(BF16),
            norm_ffn_g=norm_ffn_g[l][None], w_router=w_router.astype(BF16), b_router=b_router,
            experts=(w_gate_e[l], w_up_e[l], w_down_e[l])))

    trunks = [x_prompt, x_sample]
    rows = [x.reshape(-1, D_MODEL) for x in trunks]
    order = sorted(range(len(trunks)), key=lambda n: -rows[n].shape[0])
    rope = _rope_tables(max(x.shape[1] for x in trunks))
    for p in layers:
        experts = p["experts"]
        for n in order:
            rows[n], experts = _layer(rows[n], p, trunks[n].shape[1], rope, experts)
    return tuple(r.reshape(x.shape) for r, x in zip(rows, trunks))
```

```python
import functools

import jax
import jax.numpy as jnp
from jax import lax
from jax.experimental import pallas as pl
from jax.experimental.pallas import tpu as pltpu
from jax.experimental.pallas import tpu_sc as plsc

F32 = jnp.float32
BF16 = jnp.bfloat16
I32 = jnp.int32
U32 = jnp.uint32

LANES = 128
SUBLANES = 8
VMEM_BYTES_V7X = 64 * 1024 * 1024
SC_CORES = 2
SC_SUBCORES = 16
SC_WORKERS = SC_CORES * SC_SUBCORES
SC_CHUNK_MAX = 128

D_MODEL = 1024
A_WIDTH = D_MODEL
A_GROUPS = 8
CHUNK = 128
HEAD_DIM = 128
N_Q_HEADS = D_MODEL // HEAD_DIM
N_KV_HEADS = 2
REP = N_Q_HEADS // N_KV_HEADS
WINDOW = 128
ROPE_THETA = 10000.0
ROPE_SPLIT = 64
Q_W = N_Q_HEADS * HEAD_DIM
KV_W = N_KV_HEADS * HEAD_DIM
IN_W = 2 * A_WIDTH + Q_W + 2 * KV_W + 2 * D_MODEL
COL_U = 0
COL_V = COL_U + A_WIDTH
COL_Q = COL_V + A_WIDTH
COL_K = COL_Q + Q_W
COL_VA = COL_K + KV_W
COL_GA = COL_VA + KV_W
COL_GB = COL_GA + D_MODEL
N_GROUPS = 4
EXPERTS_PER_GROUP = 8
N_EXPERTS = N_GROUPS * EXPERTS_PER_GROUP
TOP_K = 2
D_EXPERT = 512
EPS = 1e-6
NEG = -1e30

TM_IN = 512
TM_IN_SUB = 256
TQ = 512
DENSE_COLS = 256
ROUTER_ROWS = 64
assert EXPERTS_PER_GROUP == SUBLANES and SUBLANES + N_EXPERTS <= ROUTER_ROWS
TM_ROW = 1024
TM_DEST = 2048
MOE_ROWS_MAX = 1024
MOE_MIN_BLOCKS = 2
VMEM_RESERVE = 8 * 1024 * 1024
VMEM_LIMIT = VMEM_BYTES_V7X - VMEM_RESERVE


def _rms(x, g):
    return x * lax.rsqrt(jnp.mean(x * x, axis=-1, keepdims=True) + EPS) * g


ROW_LINES = D_MODEL // 2 // LANES
ROW3 = (ROW_LINES, LANES)
HIGH_HALF = 0xFFFF0000


def _as_rows3(a):
    return a.reshape((a.shape[0] // ROW_LINES,) + ROW3)


def _as_lines(a):
    return a.reshape((a.shape[0] * ROW_LINES, LANES))


def _store_rows3(lines_ref, val):
    rows = val.shape[0]
    bits = lax.bitcast_convert_type(val.astype(BF16).astype(F32), U32)
    half = ROW_LINES * LANES
    for s in range(ROW_LINES):
        lo = bits[:, s * LANES:(s + 1) * LANES] >> 16
        hi = bits[:, half + s * LANES:half + (s + 1) * LANES] & U32(HIGH_HALF)
        lines_ref[pl.ds(s, rows, stride=ROW_LINES), :] = lax.bitcast_convert_type(lo | hi, I32)


def _load_rows3(lines_ref):
    rows = lines_ref.shape[0] // ROW_LINES
    words = [lax.bitcast_convert_type(lines_ref[pl.ds(s, rows, stride=ROW_LINES), :], U32)
             for s in range(ROW_LINES)]
    lo = [lax.bitcast_convert_type(w << 16, F32) for w in words]
    hi = [lax.bitcast_convert_type(w & U32(HIGH_HALF), F32) for w in words]
    return jnp.concatenate(lo + hi, axis=1)


def _rows3_spec(rows, index_map):
    return pl.BlockSpec((rows * ROW_LINES, LANES), index_map)


def _const_spec(shape):
    nd = len(shape)
    return pl.BlockSpec(shape, lambda *_: (0,) * nd, pipeline_mode=pl.Buffered(1))


def _inproj_kernel(x_ref, gmix_ref, win_ref, gv_ref, ws_ref, bs_ref, gq_ref, gk_ref, cos_ref, sin_ref,
                   wpa_ref, ma_ref, sgb_ref, q_ref, k_ref, v_ref, h_scr, u_scr, vn_scr, a_scr):
    tm = x_ref.shape[0]

    def stages(rows):
        def proj(lo, width):
            return jnp.dot(h_scr[rows], win_ref[:, lo:lo + width], preferred_element_type=F32)

        def norm_rope(z, g):
            zn = _rms(z, g)
            return zn * cos_ref[rows] + pltpu.roll(zn, HEAD_DIM // 2, 1) * sin_ref[rows]

        def norm():
            h_scr[rows] = _rms(x_ref[rows], gmix_ref[...]).astype(BF16)

        def mix_v():
            vn_scr[rows] = _rms(jax.nn.gelu(proj(COL_V, A_WIDTH)), gv_ref[...]).astype(BF16)

        def mix_u():
            u_scr[rows] = jax.nn.gelu(proj(COL_U, A_WIDTH))

        def spatial():
            for c in range(rows.start, rows.stop, CHUNK):
                chunk = slice(c, c + CHUNK)
                for g in range(A_GROUPS):
                    cols = slice(g * LANES, (g + 1) * LANES)
                    mixed = jnp.dot(ws_ref[g], vn_scr[chunk, cols], preferred_element_type=F32) + bs_ref[g]
                    a_scr[chunk, cols] = (u_scr[chunk, cols] * mixed).astype(BF16)

        def gate_a():
            ya = jnp.dot(a_scr[rows], wpa_ref[...], preferred_element_type=F32)
            ma_ref[rows] = jax.nn.sigmoid(proj(COL_GA, D_MODEL)) * ya

        def gate_b():
            sgb_ref[rows] = jax.nn.sigmoid(proj(COL_GB, D_MODEL))

        def queries():
            qz = proj(COL_Q, Q_W)
            for hd in range(N_Q_HEADS):
                cols = slice(hd * HEAD_DIM, (hd + 1) * HEAD_DIM)
                q_ref[rows, cols] = norm_rope(qz[:, cols], gq_ref[...]).astype(BF16)

        def keys_values():
            kz = proj(COL_K, KV_W)
            for hd in range(N_KV_HEADS):
                cols = slice(hd * HEAD_DIM, (hd + 1) * HEAD_DIM)
                k_ref[rows, cols] = norm_rope(kz[:, cols], gk_ref[...]).astype(BF16)
            v_ref[rows] = proj(COL_VA, KV_W).astype(BF16)

        return [norm, mix_v, mix_u, spatial, gate_a, gate_b, queries, keys_values]

    subs = [stages(slice(r, r + TM_IN_SUB)) for r in range(0, tm, TM_IN_SUB)]
    n_stage = len(subs[0])
    for step in range(n_stage + len(subs) - 1):
        for n, sub in enumerate(subs):
            if 0 <= step - n < n_stage:
                sub[step - n]()


def _inproj(x, p, cos, sin, seq):
    t = x.shape[0]
    tm = TM_IN
    n_pos = seq // tm
    row = lambda w: pl.BlockSpec((tm, w), lambda i: (i, 0))
    pos = pl.BlockSpec((tm, HEAD_DIM), lambda i: (i % n_pos, 0))
    return pl.pallas_call(
        _inproj_kernel,
        grid=(t // tm,),
        in_specs=[row(D_MODEL), _const_spec((1, D_MODEL)), _const_spec((D_MODEL, IN_W)),
                  _const_spec((1, A_WIDTH)), _const_spec((A_GROUPS, CHUNK, CHUNK)),
                  _const_spec((A_GROUPS, CHUNK, LANES)), _const_spec((1, HEAD_DIM)),
                  _const_spec((1, HEAD_DIM)), pos, pos, _const_spec((A_WIDTH, D_MODEL))],
        out_specs=[row(D_MODEL), row(D_MODEL), row(Q_W), row(KV_W), row(KV_W)],
        out_shape=[jax.ShapeDtypeStruct((t, D_MODEL), F32), jax.ShapeDtypeStruct((t, D_MODEL), F32),
                   jax.ShapeDtypeStruct((t, Q_W), BF16), jax.ShapeDtypeStruct((t, KV_W), BF16),
                   jax.ShapeDtypeStruct((t, KV_W), BF16)],
        scratch_shapes=[pltpu.VMEM((tm, D_MODEL), BF16), pltpu.VMEM((tm, A_WIDTH), F32),
                        pltpu.VMEM((tm, A_WIDTH), BF16), pltpu.VMEM((tm, A_WIDTH), BF16)],
        compiler_params=pltpu.CompilerParams(dimension_semantics=("parallel",),
                                             vmem_limit_bytes=VMEM_LIMIT),
        name="inproj",
    )(x, p["norm_mix_g"], p["w_in"], p["norm_v_g"], p["w_spatial"], p["b_spatial"], p["q_norm_g"],
      p["k_norm_g"], cos, sin, p["w_proj_a"])


def _attn_kernel(sink_ref, x_ref, ma_ref, sgb_ref, q_ref, kp_ref, kc_ref, kn_ref, vp_ref, vc_ref, vn_ref,
                 wpb_ref, wout_ref, gffn_ref, wr_ref, br_ref,
                 x1_ref, hn_ref, route_ref, gate_ref, counts_ref, kcat, vcat, o_scr, s_scr, p_scr, sink_scr, m_scr,
                 *, tiles_per_seq, n_tiles):
    tq = x_ref.shape[0]
    blk = WINDOW
    i = pl.program_id(0)
    slot = i % 2

    @pl.when(i == 0)
    def _():
        o_scr[...] = jnp.zeros_like(o_scr)
        counts_ref[...] = jnp.zeros_like(counts_ref)

    pos_tile = jnp.minimum(i, n_tiles - 1) % tiles_per_seq
    has_prev = pos_tile > 0
    has_next = pos_tile < tiles_per_seq - 1

    kcat[0:blk] = kp_ref[...]
    kcat[blk:blk + tq] = kc_ref[...]
    kcat[blk + tq:] = kn_ref[...]
    vcat[0:blk] = vp_ref[...]
    vcat[blk:blk + tq] = vc_ref[...]
    vcat[blk + tq:] = vn_ref[...]

    qr = lax.broadcasted_iota(I32, (blk, blk), 0)
    kc = lax.broadcasted_iota(I32, (blk, blk), 1)
    scale = HEAD_DIM ** -0.5
    n_sub = tq // blk
    pairs =[(j, g) for j in range(n_sub) for g in range(N_KV_HEADS)]

    def keys(ref, j, g):
        return ref[j * blk:(j + 3) * blk, g * HEAD_DIM:(g + 1) * HEAD_DIM]

    def head_cols(g, r):
        hd = g * REP + r
        return slice(hd * HEAD_DIM, (hd + 1) * HEAD_DIM)

    for b, (j, g) in enumerate(pairs):
        rows = slice(j * blk, (j + 1) * blk)
        qs = jnp.concatenate([q_ref[rows, head_cols(g, r)] for r in range(REP)], axis=0)
        s_scr[b] = lax.dot_general(qs, keys(kcat, j, g), (((1,), (1,)), ((), ())),
                                   preferred_element_type=F32)
    log2e = 1.4426950408889634

    def softmax(b, r):
        j, g = pairs[b]
        hrows = slice(r * blk, (r + 1) * blk)
        z = s_scr[b, hrows, :] * (scale * log2e)
        lo_ok = kc >= (qr + jnp.where(has_prev, 0, blk) if j == 0 else qr)
        hi_ok = kc <= (qr - jnp.where(has_next, 0, blk) if j == n_sub - 1 else qr)
        z = jnp.concatenate([jnp.where(lo_ok, z[:, :blk], NEG), z[:, blk:2 * blk],
                             jnp.where(hi_ok, z[:, 2 * blk:], NEG)], axis=1)
        sink = sink_ref[g * REP + r] * log2e
        m = jnp.maximum(jnp.max(z, axis=-1, keepdims=True), sink)
        p_scr[b, hrows, :] = jnp.exp2(z - m).astype(BF16)
        sink_scr[b, hrows, :] = jnp.broadcast_to(jnp.exp2(sink - m), (blk, LANES))

    def values(b):
        j, g = pairs[b]
        rows = slice(j * blk, (j + 1) * blk)
        v_ext = jnp.concatenate([keys(vcat, j, g), jnp.ones((3 * blk, HEAD_DIM), BF16)], axis=1)
        acc = jnp.dot(p_scr[b], v_ext, preferred_element_type=F32)
        o = (acc[:, :HEAD_DIM] / (acc[:, HEAD_DIM:] + sink_scr[b])).astype(BF16)
        for r in range(REP):
            o_scr[slot, rows, head_cols(g, r)] = o[r * blk:(r + 1) * blk, :]

    def merged_cols(cols):
        yb = jnp.dot(o_scr[1 - slot], wpb_ref[:, cols], preferred_element_type=F32)
        m_scr[:, cols] = (ma_ref[:, cols] + sgb_ref[:, cols] * yb).astype(BF16)

    def x1_cols(cols):
        x1_ref[:, cols] = x_ref[:, cols] + jnp.dot(m_scr[...], wout_ref[:, cols], preferred_element_type=F32)

    col_chunks = [slice(c * DENSE_COLS, (c + 1) * DENSE_COLS) for c in range(D_MODEL // DENSE_COLS)]
    dense = [functools.partial(f, cols) for f in (merged_cols, x1_cols) for cols in col_chunks]
    units = [(b, r) for b in range(len(pairs)) for r in range(REP)]
    units_per_dense = len(units) // len(dense)
    for n, (b, r) in enumerate(units):
        softmax(b, r)
        if (n + 1) % units_per_dense == 0:
            dense[(n + 1) // units_per_dense - 1]()
        if r == REP - 1:
            values(b)

    hn = _rms(x1_ref[...], gffn_ref[...])
    _store_rows3(hn_ref, hn)
    def wide(a):
        return jnp.concatenate([a] * (tq // LANES), axis=1)

    logits = lax.dot_general(wr_ref[...], hn.astype(BF16), (((1,), (1,)), ((), ())),
                             preferred_element_type=F32) + wide(br_ref[...])
    sub = lax.broadcasted_iota(I32, (SUBLANES, tq), 0).astype(F32)
    ninf = -jnp.inf

    def cmax(a):
        return jnp.max(a, axis=0, keepdims=True)

    def csum(a):
        return jnp.sum(a, axis=0, keepdims=True)

    def first_row(mask):
        return jnp.min(jnp.where(mask, sub, float(SUBLANES)), axis=0, keepdims=True)

    def group_rows(g):
        return logits[(g + 1) * SUBLANES:(g + 2) * SUBLANES]

    gl = jnp.where(sub < N_GROUPS, logits[0:SUBLANES], ninf)
    gmax = cmax(gl)
    g_sel = first_row(gl == gmax)
    g_p = 1.0 / csum(jnp.exp(gl - gmax))
    el = group_rows(0)
    for g in range(1, N_GROUPS):
        el = jnp.where(g_sel == g, group_rows(g), el)
    ee = jnp.exp(el - cmax(el))
    eprob = ee / csum(ee)
    p1 = cmax(eprob)
    i1 = first_row(eprob == p1)
    eprob2 = jnp.where(sub == i1, -1.0, eprob)
    p2 = cmax(eprob2)
    i2 = first_row(eprob2 == p2)
    psum = p1 + p2
    w1 = g_p * p1 / psum
    w2 = g_p * p2 / psum
    e1 = g_sel * EXPERTS_PER_GROUP + i1
    e2 = g_sel * EXPERTS_PER_GROUP + i2

    erow = lax.broadcasted_iota(I32, (N_EXPERTS, tq), 0).astype(F32)
    oh1 = erow == e1
    oh2 = erow == e2
    cnt = (jnp.where(oh1, 1.0, 0.0) + jnp.where(oh2, 1.0, 0.0)) * jnp.where(i > 0, 1.0, 0.0)
    ri = lax.broadcasted_iota(I32, (tq, tq), 0)
    ci = lax.broadcasted_iota(I32, (tq, tq), 1)
    earlier = jnp.where(ri < ci, 1.0, 0.0).astype(BF16)
    base = wide(counts_ref[...]) + jnp.dot(cnt.astype(BF16), earlier, preferred_element_type=F32)
    r1 = csum(jnp.where(oh1, base, 0.0))
    r2 = csum(jnp.where(oh2, base, 0.0))
    counts_ref[...] = counts_ref[...] + jnp.sum(cnt, axis=1, keepdims=True)

    route = jnp.where(sub == 0.0, e1, jnp.where(sub == 1.0, e2,
                      jnp.where(sub == 2.0, r1, jnp.where(sub == 3.0, r2, 0.0))))
    route_ref[...] = route.astype(I32)
    gates = jnp.where(sub == 0.0, w1, jnp.where(sub == 1.0, w2, 0.0))
    gate_ref[...] = jnp.concatenate([gates, jnp.zeros((LANES - SUBLANES, tq), F32)], axis=0).T


def _attn(x, ma, sgb, q, k, v, p, seq):
    t = x.shape[0]
    tq = TQ
    sub = tq // WINDOW
    last_blk = t // WINDOW - 1
    n_tiles = t // tq
    att = lambda i: jnp.minimum(i, n_tiles - 1)
    post = lambda i: jnp.maximum(i - 1, 0)
    att_row = lambda w: pl.BlockSpec((tq, w), lambda i: (att(i), 0))
    row = lambda w: pl.BlockSpec((tq, w), lambda i: (post(i), 0))
    prev = pl.BlockSpec((WINDOW, KV_W), lambda i: (jnp.maximum(att(i) * sub - 1, 0), 0))
    nxt = pl.BlockSpec((WINDOW, KV_W), lambda i: (jnp.minimum((att(i) + 1) * sub, last_blk), 0))
    return pl.pallas_call(
        functools.partial(_attn_kernel, tiles_per_seq=seq // tq, n_tiles=n_tiles),
        grid=(n_tiles + 1,),
        in_specs=[pl.BlockSpec(memory_space=pltpu.SMEM),
                  row(D_MODEL), row(D_MODEL), row(D_MODEL), att_row(Q_W),
                  prev, att_row(KV_W), nxt, prev, att_row(KV_W), nxt,
                  _const_spec((Q_W, D_MODEL)), _const_spec((D_MODEL, D_MODEL)), _const_spec((1, D_MODEL)),
                  _const_spec((ROUTER_ROWS, D_MODEL)), _const_spec((ROUTER_ROWS, LANES))],
        out_specs=[row(D_MODEL), _rows3_spec(tq, lambda i: (post(i), 0)),
                   pl.BlockSpec((SUBLANES, tq), lambda i: (0, post(i))), row(LANES),
                   pl.BlockSpec((N_EXPERTS, LANES), lambda i: (0, 0))],
        out_shape=[jax.ShapeDtypeStruct((t, D_MODEL), F32), jax.ShapeDtypeStruct((t * ROW_LINES, LANES), I32),
                   jax.ShapeDtypeStruct((SUBLANES, t), I32), jax.ShapeDtypeStruct((t, LANES), F32),
                   jax.ShapeDtypeStruct((N_EXPERTS, LANES), F32)],
        scratch_shapes=[pltpu.VMEM((tq + 2 * WINDOW, KV_W), BF16), pltpu.VMEM((tq + 2 * WINDOW, KV_W), BF16),
                        pltpu.VMEM((2, tq, Q_W), BF16),
                        pltpu.VMEM((sub * N_KV_HEADS, REP * WINDOW, 3 * WINDOW), F32),
                        pltpu.VMEM((sub * N_KV_HEADS, REP * WINDOW, 3 * WINDOW), BF16),
                        pltpu.VMEM((sub * N_KV_HEADS, REP * WINDOW, LANES), F32),
                        pltpu.VMEM((tq, D_MODEL), BF16)],
        compiler_params=pltpu.CompilerParams(dimension_semantics=("arbitrary",),
                                             vmem_limit_bytes=VMEM_LIMIT),
        name="attn",
    )(p["sink"], x, ma, sgb, q, k, k, k, v, v, v, p["w_proj_b"], p["w_out"], p["norm_ffn_g"],
      p["w_router"], p["b_router"])


def _dest_kernel(route_ref, starts_ref, dest_ref):
    route = route_ref[...].astype(F32)
    td = route.shape[1]
    sub = lax.broadcasted_iota(I32, route.shape, 0)
    erow = lax.broadcasted_iota(I32, (N_EXPERTS, td), 0).astype(F32)
    starts = jnp.broadcast_to(starts_ref[:, 0:1], (N_EXPERTS, td))

    def slot(k):
        start = jnp.sum(jnp.where(erow == route[k:k + 1], starts, 0.0), axis=0, keepdims=True)
        return start + route[TOP_K + k:TOP_K + k + 1]

    dest_ref[...] = jnp.where(sub == 0, slot(0), jnp.where(sub == 1, slot(1), 0.0)).astype(I32)


def _dest(route, pad_starts):
    t = route.shape[1]
    td = min(TM_DEST, t)
    return pl.pallas_call(
        _dest_kernel,
        grid=(t // td,),
        in_specs=[pl.BlockSpec((SUBLANES, td), lambda i: (0, i)), _const_spec((N_EXPERTS, LANES))],
        out_specs=pl.BlockSpec((SUBLANES, td), lambda i: (0, i)),
        out_shape=jax.ShapeDtypeStruct((SUBLANES, t), I32),
        compiler_params=pltpu.CompilerParams(dimension_semantics=("parallel",)),
        name="dest",
    )(route, pad_starts)


def _sc_mesh():
    return plsc.VectorSubcoreMesh(core_axis_name="c", subcore_axis_name="s")


def _sc_worker():
    return lax.axis_index("s") * SC_CORES + lax.axis_index("c")


def _sc_chunk(t):
    return min(SC_CHUNK_MAX, t // (SC_WORKERS * SUBLANES))


def _dispatch(hn, dests, n_rows):
    t = hn.shape[0]
    chunk = dests[0].shape[1]
    per_worker = t // SC_WORKERS
    n_chunks = per_worker // chunk
    idx = pltpu.VMEM((n_chunks, chunk), I32)

    @functools.partial(
        pl.kernel, mesh=_sc_mesh(), out_type=jax.ShapeDtypeStruct((n_rows,) + ROW3, I32),
        scratch_types=[idx, idx, pltpu.VMEM((chunk,) + ROW3, I32), pltpu.SemaphoreType.DMA])
    def scatter_rows(hn_hbm, d0_hbm, d1_hbm, xs_hbm, i0_v, i1_v, rows_v, sem):
        w = _sc_worker()
        pltpu.sync_copy(d0_hbm.at[pl.ds(w * n_chunks, n_chunks)], i0_v)
        pltpu.sync_copy(d1_hbm.at[pl.ds(w * n_chunks, n_chunks)], i1_v)

        @pl.loop(0, n_chunks)
        def _(j):
            pltpu.sync_copy(hn_hbm.at[pl.ds(w * per_worker + j * chunk, chunk)], rows_v)
            copies = [pltpu.make_async_copy(rows_v, xs_hbm.at[i_v.at[j]], sem) for i_v in (i0_v, i1_v)]
            for cp in copies:
                cp.start()
            for cp in copies:
                cp.wait()

    return scatter_rows(hn, *dests)


def _moe_block_rows(t):
    rows = MOE_ROWS_MAX
    while rows > CHUNK and (t * TOP_K) // N_EXPERTS < MOE_MIN_BLOCKS * rows:
        rows //= 2
    return rows


def _moe_kernel(be_ref, nvalid_ref, nused_ref, xs_ref, wg_ref, wu_ref, wd_ref, yb_ref, *w16_refs):
    i = pl.program_id(0)
    used = i < nused_ref[0]
    weights = w16_refs if w16_refs else (wg_ref, wu_ref, wd_ref)

    @pl.when(jnp.logical_not(used))
    def _():
        yb_ref[...] = jnp.zeros_like(yb_ref)

    @pl.when(used)
    def _():
        if w16_refs:
            @pl.when(jnp.logical_or(i == 0, be_ref[i] != be_ref[jnp.maximum(i - 1, 0)]))
            def _():
                for src, dst in zip((wg_ref, wu_ref, wd_ref), w16_refs):
                    dst[0] = src[0].astype(BF16)

        row = lax.broadcasted_iota(I32, (xs_ref.shape[0] // ROW_LINES, 1), 0)
        x = jnp.where(row < nvalid_ref[i], _load_rows3(xs_ref), 0.0).astype(BF16)
        gate = jnp.dot(x, weights[0][0], preferred_element_type=F32)
        up = jnp.dot(x, weights[1][0], preferred_element_type=F32)
        hid = (jax.nn.silu(gate) * up).astype(BF16)
        _store_rows3(yb_ref, jnp.dot(hid, weights[2][0], preferred_element_type=F32))


def _moe(block_e, n_valid, n_used, xs, experts, block_rows):
    n_blocks = xs.shape[0] // (block_rows * ROW_LINES)
    cast = experts[0].dtype != BF16

    def rows(i, be, nv, nu):
        return (jnp.minimum(i, nu[0] - 1), 0)

    def expert(i, be, nv, nu):
        return (be[jnp.minimum(i, nu[0] - 1)], 0, 0)

    w_specs = [pl.BlockSpec((1,) + w.shape[1:], expert) for w in experts]
    out_specs = [_rows3_spec(block_rows, lambda i, be, nv, nu: (i, 0))]
    out_shape = [jax.ShapeDtypeStruct(xs.shape, I32)]
    if cast:
        out_specs += w_specs
        out_shape += [jax.ShapeDtypeStruct(w.shape, BF16) for w in experts]
    yb, *w16 = pl.pallas_call(
        _moe_kernel,
        grid_spec=pltpu.PrefetchScalarGridSpec(
            num_scalar_prefetch=3,
            grid=(n_blocks,),
            in_specs=[_rows3_spec(block_rows, rows)] + w_specs,
            out_specs=out_specs,
        ),
        out_shape=out_shape,
        compiler_params=pltpu.CompilerParams(dimension_semantics=("arbitrary",),
                                             vmem_limit_bytes=VMEM_LIMIT),
        name="moe",
    )(block_e, n_valid, n_used, xs, *experts)
    return yb, (tuple(w16) if cast else experts)


def _gather(yb, dests):
    chunk = dests[0].shape[1]
    t = dests[0].shape[0] * chunk
    per_worker = t // SC_WORKERS
    n_chunks = per_worker // chunk
    idx = pltpu.VMEM((n_chunks, chunk), I32)
    out = jax.ShapeDtypeStruct((t,) + ROW3, I32)

    @functools.partial(
        pl.kernel, mesh=_sc_mesh(), out_type=(out, out),
        scratch_types=[idx, idx, pltpu.VMEM((chunk,) + ROW3, I32), pltpu.SemaphoreType.DMA])
    def gather_rows(yb_hbm, d0_hbm, d1_hbm, y0_hbm, y1_hbm, i0_v, i1_v, rows_v, sem):
        w = _sc_worker()
        pltpu.sync_copy(d0_hbm.at[pl.ds(w * n_chunks, n_chunks)], i0_v)
        pltpu.sync_copy(d1_hbm.at[pl.ds(w * n_chunks, n_chunks)], i1_v)

        @pl.loop(0, n_chunks)
        def _(j):
            rows = pl.ds(w * per_worker + j * chunk, chunk)
            for i_v, y_hbm in ((i0_v, y0_hbm), (i1_v, y1_hbm)):
                pltpu.async_copy(yb_hbm.at[i_v.at[j]], rows_v, sem).wait()
                pltpu.sync_copy(rows_v, y_hbm.at[rows])

    return gather_rows(yb, *dests)


def _combine_kernel(x1_ref, gate_ref, y0_ref, y1_ref, out_ref):
    gate = gate_ref[...]
    out_ref[...] = x1_ref[...] + (_load_rows3(y0_ref) * gate[:, 0:1] + _load_rows3(y1_ref) * gate[:, 1:2])


def _combine(x1, gate, y0, y1):
    t = x1.shape[0]
    tm = TM_ROW
    return pl.pallas_call(
        _combine_kernel,
        grid=(t // tm,),
        in_specs=[pl.BlockSpec((tm, D_MODEL), lambda i: (i, 0)),
                  pl.BlockSpec((tm, LANES), lambda i: (i, 0)),
                  _rows3_spec(tm, lambda i: (i, 0)), _rows3_spec(tm, lambda i: (i, 0))],
        out_specs=pl.BlockSpec((tm, D_MODEL), lambda i: (i, 0)),
        out_shape=jax.ShapeDtypeStruct((t, D_MODEL), F32),
        compiler_params=pltpu.CompilerParams(dimension_semantics=("parallel",)),
        name="combine",
    )(x1, gate, y0, y1)


def _rope_tables(seq):
    half = HEAD_DIM // 2
    inv_freq = ROPE_THETA ** (-jnp.arange(half, dtype=F32) / half)
    coarse = (jnp.arange(seq // ROPE_SPLIT) * ROPE_SPLIT).astype(F32)[:, None, None] * inv_freq
    fine = jnp.arange(ROPE_SPLIT).astype(F32)[None, :, None] * inv_freq
    cos = (jnp.cos(coarse) * jnp.cos(fine) - jnp.sin(coarse) * jnp.sin(fine)).reshape(seq, half)
    sin = (jnp.sin(coarse) * jnp.cos(fine) + jnp.cos(coarse) * jnp.sin(fine)).reshape(seq, half)
    return jnp.concatenate([cos, cos], axis=-1), jnp.concatenate([-sin, sin], axis=-1)


def _layer(x, p, seq, rope, experts):
    t = x.shape[0]
    ma, sgb, q, k, v = _inproj(x, p, *rope, seq)
    x1, hn, route, gate, counts_f = _attn(x, ma, sgb, q, k, v, p, seq)

    block_rows = _moe_block_rows(t)
    counts = counts_f[:, 0].astype(I32)
    min_blocks = 1 if experts[0].dtype != BF16 else 0
    padded = jnp.maximum((counts + block_rows - 1) // block_rows, min_blocks) * block_rows
    pad_ends = jnp.cumsum(padded)
    pad_starts = pad_ends - padded
    n_blocks = (t * TOP_K) // block_rows + N_EXPERTS
    block_start = jnp.arange(n_blocks, dtype=I32) * block_rows
    in_expert = jnp.logical_and(block_start[:, None] >= pad_starts[None, :],
                                block_start[:, None] < pad_ends[None, :]).astype(I32)
    block_e = jnp.minimum(jnp.sum((block_start[:, None] >= pad_ends[None, :]).astype(I32), axis=1), N_EXPERTS - 1)
    n_valid = jnp.sum(in_expert * jnp.clip(pad_starts + counts - block_start[:, None], 0, block_rows), axis=1)
    n_used = pad_ends[-1:] // block_rows
    starts_col = jnp.broadcast_to(pad_starts.astype(F32)[:, None], (N_EXPERTS, LANES))

    dest = _dest(route, starts_col)
    dests = [dest[k].reshape(t // _sc_chunk(t), _sc_chunk(t)) for k in range(TOP_K)]
    xs = _dispatch(_as_rows3(hn), dests, n_blocks * block_rows)
    yb, experts = _moe(block_e, n_valid, n_used, _as_lines(xs), experts, block_rows)
    y0, y1 = _gather(_as_rows3(yb), dests)
    return _combine(x1, gate, _as_lines(y0), _as_lines(y1)), experts


def kernel(x_prompt, x_sample, norm_mix_g, w_in, norm_v_g, w_spatial, b_spatial, q_norm_g, k_norm_g, sink,
           w_proj_a, w_proj_b, w_out, norm_ffn_g, w_router_group, b_router_group, w_router_expert,
           b_router_expert, w_gate_e, w_up_e, w_down_e):
    depth = w_in.shape[0]
    layers = []
    for l in range(depth):
        w_router = jnp.zeros((ROUTER_ROWS, D_MODEL), F32)
        w_router = w_router.at[:N_GROUPS].set(w_router_group[l].T)
        w_router = w_router.at[SUBLANES:SUBLANES + N_EXPERTS].set(w_router_expert[l].T)
        b_router = jnp.zeros((ROUTER_ROWS,), F32)
        b_router = b_router.at[:N_GROUPS].set(b_router_group[l])
        b_router = b_router.at[SUBLANES:SUBLANES + N_EXPERTS].set(b_router_expert[l])
        b_router = jnp.broadcast_to(b_router[:, None], (ROUTER_ROWS, LANES))
        layers.append(dict(
            norm_mix_g=norm_mix_g[l][None], w_in=w_in[l].astype(BF16), norm_v_g=norm_v_g[l][None],
            w_spatial=w_spatial[l].astype(BF16),
            b_spatial=jnp.broadcast_to(b_spatial[l][:, :, None], (A_GROUPS, CHUNK, LANES)),
            q_norm_g=q_norm_g[l][None], k_norm_g=k_norm_g[l][None], sink=sink[l],
            w_proj_a=w_proj_a[l].astype(BF16), w_proj_b=w_proj_b[l].astype(BF16), w_out=w_out[l].astype(BF16),
            norm_ffn_g=norm_ffn_g[l][None], w_router=w_router.astype(BF16), b_router=b_router,
            experts=(w_gate_e[l], w_up_e[l], w_down_e[l])))

    trunks = [x_prompt, x_sample]
    rows = [x.reshape(-1, D_MODEL) for x in trunks]
    order = sorted(range(len(trunks)), key=lambda n: -rows[n].shape[0])
    rope = _rope_tables(max(x.shape[1] for x in trunks))
    for p in layers:
        experts = p["experts"]
        for n in order:
            rows[n], experts = _layer(rows[n], p, trunks[n].shape[1], rope, experts)
    return tuple(r.reshape(x.shape) for r, x in zip(rows, trunks))
```

```python
import functools

import jax
import jax.numpy as jnp
from jax import lax
from jax.experimental import pallas as pl
from jax.experimental.pallas import tpu as pltpu
from jax.experimental.pallas import tpu_sc as plsc

F32 = jnp.float32
BF16 = jnp.bfloat16
I32 = jnp.int32
U32 = jnp.uint32

LANES = 128
SUBLANES = 8
VMEM_BYTES_V7X = 64 * 1024 * 1024
SC_CORES = 2
SC_SUBCORES = 16
SC_WORKERS = SC_CORES * SC_SUBCORES
SC_CHUNK_MAX = 128

D_MODEL = 1024
A_WIDTH = D_MODEL
A_GROUPS = 8
CHUNK = 128
HEAD_DIM = 128
N_Q_HEADS = D_MODEL // HEAD_DIM
N_KV_HEADS = 2
REP = N_Q_HEADS // N_KV_HEADS
WINDOW = 128
ROPE_THETA = 10000.0
ROPE_SPLIT = 64
Q_W = N_Q_HEADS * HEAD_DIM
KV_W = N_KV_HEADS * HEAD_DIM
IN_W = 2 * A_WIDTH + Q_W + 2 * KV_W + 2 * D_MODEL
COL_U = 0
COL_V = COL_U + A_WIDTH
COL_Q = COL_V + A_WIDTH
COL_K = COL_Q + Q_W
COL_VA = COL_K + KV_W
COL_GA = COL_VA + KV_W
COL_GB = COL_GA + D_MODEL
N_GROUPS = 4
EXPERTS_PER_GROUP = 8
N_EXPERTS = N_GROUPS * EXPERTS_PER_GROUP
TOP_K = 2
D_EXPERT = 512
EPS = 1e-6
NEG = -1e30

TM_IN = 512
TM_IN_SUB = 256
TQ = 512
DENSE_COLS = 256
ROUTER_ROWS = 64
assert EXPERTS_PER_GROUP == SUBLANES and SUBLANES + N_EXPERTS <= ROUTER_ROWS
TM_ROW = 1024
TM_DEST = 2048
MOE_ROWS_MAX = 1024
MOE_MIN_BLOCKS = 1
VMEM_RESERVE = 8 * 1024 * 1024
VMEM_LIMIT = VMEM_BYTES_V7X - VMEM_RESERVE


def _rms(x, g):
    return x * lax.rsqrt(jnp.mean(x * x, axis=-1, keepdims=True) + EPS) * g


ROW_LINES = D_MODEL // 2 // LANES
ROW3 = (ROW_LINES, LANES)
HIGH_HALF = 0xFFFF0000


def _as_rows3(a):
    return a.reshape((a.shape[0] // ROW_LINES,) + ROW3)


def _as_lines(a):
    return a.reshape((a.shape[0] * ROW_LINES, LANES))


def _store_rows3(lines_ref, val):
    rows = val.shape[0]
    bits = lax.bitcast_convert_type(val.astype(BF16).astype(F32), U32)
    half = ROW_LINES * LANES
    for s in range(ROW_LINES):
        lo = bits[:, s * LANES:(s + 1) * LANES] >> 16
        hi = bits[:, half + s * LANES:half + (s + 1) * LANES] & U32(HIGH_HALF)
        lines_ref[pl.ds(s, rows, stride=ROW_LINES), :] = lax.bitcast_convert_type(lo | hi, I32)


def _load_rows3(lines_ref):
    rows = lines_ref.shape[0] // ROW_LINES
    words = [lax.bitcast_convert_type(lines_ref[pl.ds(s, rows, stride=ROW_LINES), :], U32)
             for s in range(ROW_LINES)]
    lo = [lax.bitcast_convert_type(w << 16, F32) for w in words]
    hi = [lax.bitcast_convert_type(w & U32(HIGH_HALF), F32) for w in words]
    return jnp.concatenate(lo + hi, axis=1)


def _rows3_spec(rows, index_map):
    return pl.BlockSpec((rows * ROW_LINES, LANES), index_map)


def _const_spec(shape):
    nd = len(shape)
    return pl.BlockSpec(shape, lambda *_: (0,) * nd, pipeline_mode=pl.Buffered(1))


def _inproj_kernel(x_ref, gmix_ref, win_ref, gv_ref, ws_ref, bs_ref, gq_ref, gk_ref, cos_ref, sin_ref,
                   wpa_ref, ma_ref, sgb_ref, q_ref, k_ref, v_ref, h_scr, u_scr, vn_scr, a_scr):
    tm = x_ref.shape[0]

    def stages(rows):
        def proj(lo, width):
            return jnp.dot(h_scr[rows], win_ref[:, lo:lo + width], preferred_element_type=F32)

        def norm_rope(z, g):
            zn = _rms(z, g)
            return zn * cos_ref[rows] + pltpu.roll(zn, HEAD_DIM // 2, 1) * sin_ref[rows]

        def norm():
            h_scr[rows] = _rms(x_ref[rows], gmix_ref[...]).astype(BF16)

        def mix_v():
            vn_scr[rows] = _rms(jax.nn.gelu(proj(COL_V, A_WIDTH)), gv_ref[...]).astype(BF16)

        def mix_u():
            u_scr[rows] = jax.nn.gelu(proj(COL_U, A_WIDTH))

        def spatial():
            for c in range(rows.start, rows.stop, CHUNK):
                chunk = slice(c, c + CHUNK)
                for g in range(A_GROUPS):
                    cols = slice(g * LANES, (g + 1) * LANES)
                    mixed = jnp.dot(ws_ref[g], vn_scr[chunk, cols], preferred_element_type=F32) + bs_ref[g]
                    a_scr[chunk, cols] = (u_scr[chunk, cols] * mixed).astype(BF16)

        def gate_a():
            ya = jnp.dot(a_scr[rows], wpa_ref[...], preferred_element_type=F32)
            ma_ref[rows] = jax.nn.sigmoid(proj(COL_GA, D_MODEL)) * ya

        def gate_b():
            sgb_ref[rows] = jax.nn.sigmoid(proj(COL_GB, D_MODEL))

        def queries():
            qz = proj(COL_Q, Q_W)
            for hd in range(N_Q_HEADS):
                cols = slice(hd * HEAD_DIM, (hd + 1) * HEAD_DIM)
                q_ref[rows, cols] = norm_rope(qz[:, cols], gq_ref[...]).astype(BF16)

        def keys_values():
            kz = proj(COL_K, KV_W)
            for hd in range(N_KV_HEADS):
                cols = slice(hd * HEAD_DIM, (hd + 1) * HEAD_DIM)
                k_ref[rows, cols] = norm_rope(kz[:, cols], gk_ref[...]).astype(BF16)
            v_ref[rows] = proj(COL_VA, KV_W).astype(BF16)

        return [norm, mix_v, mix_u, spatial, gate_a, gate_b, queries, keys_values]

    subs = [stages(slice(r, r + TM_IN_SUB)) for r in range(0, tm, TM_IN_SUB)]
    n_stage = len(subs[0])
    for step in range(n_stage + len(subs) - 1):
        for n, sub in enumerate(subs):
            if 0 <= step - n < n_stage:
                sub[step - n]()


def _inproj(x, p, cos, sin, seq):
    t = x.shape[0]
    tm = TM_IN
    n_pos = seq // tm
    row = lambda w: pl.BlockSpec((tm, w), lambda i: (i, 0))
    pos = pl.BlockSpec((tm, HEAD_DIM), lambda i: (i % n_pos, 0))
    return pl.pallas_call(
        _inproj_kernel,
        grid=(t // tm,),
        in_specs=[row(D_MODEL), _const_spec((1, D_MODEL)), _const_spec((D_MODEL, IN_W)),
                  _const_spec((1, A_WIDTH)), _const_spec((A_GROUPS, CHUNK, CHUNK)),
                  _const_spec((A_GROUPS, CHUNK, LANES)), _const_spec((1, HEAD_DIM)),
                  _const_spec((1, HEAD_DIM)), pos, pos, _const_spec((A_WIDTH, D_MODEL))],
        out_specs=[row(D_MODEL), row(D_MODEL), row(Q_W), row(KV_W), row(KV_W)],
        out_shape=[jax.ShapeDtypeStruct((t, D_MODEL), F32), jax.ShapeDtypeStruct((t, D_MODEL), F32),
                   jax.ShapeDtypeStruct((t, Q_W), BF16), jax.ShapeDtypeStruct((t, KV_W), BF16),
                   jax.ShapeDtypeStruct((t, KV_W), BF16)],
        scratch_shapes=[pltpu.VMEM((tm, D_MODEL), BF16), pltpu.VMEM((tm, A_WIDTH), F32),
                        pltpu.VMEM((tm, A_WIDTH), BF16), pltpu.VMEM((tm, A_WIDTH), BF16)],
        compiler_params=pltpu.CompilerParams(dimension_semantics=("parallel",),
                                             vmem_limit_bytes=VMEM_LIMIT),
        name="inproj",
    )(x, p["norm_mix_g"], p["w_in"], p["norm_v_g"], p["w_spatial"], p["b_spatial"], p["q_norm_g"],
      p["k_norm_g"], cos, sin, p["w_proj_a"])


def _attn_kernel(sink_ref, x_ref, ma_ref, sgb_ref, q_ref, kp_ref, kc_ref, kn_ref, vp_ref, vc_ref, vn_ref,
                 wpb_ref, wout_ref, gffn_ref, wr_ref, br_ref,
                 x1_ref, hn_ref, route_ref, gate_ref, counts_ref, kcat, vcat, o_scr, s_scr, p_scr, sink_scr, m_scr,
                 *, tiles_per_seq, n_tiles):
    tq = x_ref.shape[0]
    blk = WINDOW
    i = pl.program_id(0)
    slot = i % 2

    @pl.when(i == 0)
    def _():
        o_scr[...] = jnp.zeros_like(o_scr)
        counts_ref[...] = jnp.zeros_like(counts_ref)

    pos_tile = jnp.minimum(i, n_tiles - 1) % tiles_per_seq
    has_prev = pos_tile > 0
    has_next = pos_tile < tiles_per_seq - 1

    kcat[0:blk] = kp_ref[...]
    kcat[blk:blk + tq] = kc_ref[...]
    kcat[blk + tq:] = kn_ref[...]
    vcat[0:blk] = vp_ref[...]
    vcat[blk:blk + tq] = vc_ref[...]
    vcat[blk + tq:] = vn_ref[...]

    qr = lax.broadcasted_iota(I32, (blk, blk), 0)
    kc = lax.broadcasted_iota(I32, (blk, blk), 1)
    scale = HEAD_DIM ** -0.5
    n_sub = tq // blk
    pairs =[(j, g) for j in range(n_sub) for g in range(N_KV_HEADS)]

    def keys(ref, j, g):
        return ref[j * blk:(j + 3) * blk, g * HEAD_DIM:(g + 1) * HEAD_DIM]

    def head_cols(g, r):
        hd = g * REP + r
        return slice(hd * HEAD_DIM, (hd + 1) * HEAD_DIM)

    for b, (j, g) in enumerate(pairs):
        rows = slice(j * blk, (j + 1) * blk)
        qs = jnp.concatenate([q_ref[rows, head_cols(g, r)] for r in range(REP)], axis=0)
        s_scr[b] = lax.dot_general(qs, keys(kcat, j, g), (((1,), (1,)), ((), ())),
                                   preferred_element_type=F32)
    log2e = 1.4426950408889634

    def softmax(b, r):
        j, g = pairs[b]
        hrows = slice(r * blk, (r + 1) * blk)
        z = s_scr[b, hrows, :] * (scale * log2e)
        lo_ok = kc >= (qr + jnp.where(has_prev, 0, blk) if j == 0 else qr)
        hi_ok = kc <= (qr - jnp.where(has_next, 0, blk) if j == n_sub - 1 else qr)
        z = jnp.concatenate([jnp.where(lo_ok, z[:, :blk], NEG), z[:, blk:2 * blk],
                             jnp.where(hi_ok, z[:, 2 * blk:], NEG)], axis=1)
        sink = sink_ref[g * REP + r] * log2e
        m = jnp.maximum(jnp.max(z, axis=-1, keepdims=True), sink)
        p_scr[b, hrows, :] = jnp.exp2(z - m).astype(BF16)
        sink_scr[b, hrows, :] = jnp.broadcast_to(jnp.exp2(sink - m), (blk, LANES))

    def values(b):
        j, g = pairs[b]
        rows = slice(j * blk, (j + 1) * blk)
        v_ext = jnp.concatenate([keys(vcat, j, g), jnp.ones((3 * blk, HEAD_DIM), BF16)], axis=1)
        acc = jnp.dot(p_scr[b], v_ext, preferred_element_type=F32)
        o = (acc[:, :HEAD_DIM] / (acc[:, HEAD_DIM:] + sink_scr[b])).astype(BF16)
        for r in range(REP):
            o_scr[slot, rows, head_cols(g, r)] = o[r * blk:(r + 1) * blk, :]

    def merged_cols(cols):
        yb = jnp.dot(o_scr[1 - slot], wpb_ref[:, cols], preferred_element_type=F32)
        m_scr[:, cols] = (ma_ref[:, cols] + sgb_ref[:, cols] * yb).astype(BF16)

    def x1_cols(cols):
        x1_ref[:, cols] = x_ref[:, cols] + jnp.dot(m_scr[...], wout_ref[:, cols], preferred_element_type=F32)

    col_chunks = [slice(c * DENSE_COLS, (c + 1) * DENSE_COLS) for c in range(D_MODEL // DENSE_COLS)]
    dense = [functools.partial(f, cols) for f in (merged_cols, x1_cols) for cols in col_chunks]
    units = [(b, r) for b in range(len(pairs)) for r in range(REP)]
    units_per_dense = len(units) // len(dense)
    for n, (b, r) in enumerate(units):
        softmax(b, r)
        if (n + 1) % units_per_dense == 0:
            dense[(n + 1) // units_per_dense - 1]()
        if r == REP - 1:
            values(b)

    hn = _rms(x1_ref[...], gffn_ref[...])
    _store_rows3(hn_ref, hn)
    def wide(a):
        return jnp.concatenate([a] * (tq // LANES), axis=1)

    logits = lax.dot_general(wr_ref[...], hn.astype(BF16), (((1,), (1,)), ((), ())),
                             preferred_element_type=F32) + wide(br_ref[...])
    sub = lax.broadcasted_iota(I32, (SUBLANES, tq), 0).astype(F32)
    ninf = -jnp.inf

    def cmax(a):
        return jnp.max(a, axis=0, keepdims=True)

    def csum(a):
        return jnp.sum(a, axis=0, keepdims=True)

    def first_row(mask):
        return jnp.min(jnp.where(mask, sub, float(SUBLANES)), axis=0, keepdims=True)

    def group_rows(g):
        return logits[(g + 1) * SUBLANES:(g + 2) * SUBLANES]

    gl = jnp.where(sub < N_GROUPS, logits[0:SUBLANES], ninf)
    gmax = cmax(gl)
    g_sel = first_row(gl == gmax)
    g_p = 1.0 / csum(jnp.exp(gl - gmax))
    el = group_rows(0)
    for g in range(1, N_GROUPS):
        el = jnp.where(g_sel == g, group_rows(g), el)
    ee = jnp.exp(el - cmax(el))
    eprob = ee / csum(ee)
    p1 = cmax(eprob)
    i1 = first_row(eprob == p1)
    eprob2 = jnp.where(sub == i1, -1.0, eprob)
    p2 = cmax(eprob2)
    i2 = first_row(eprob2 == p2)
    psum = p1 + p2
    w1 = g_p * p1 / psum
    w2 = g_p * p2 / psum
    e1 = g_sel * EXPERTS_PER_GROUP + i1
    e2 = g_sel * EXPERTS_PER_GROUP + i2

    erow = lax.broadcasted_iota(I32, (N_EXPERTS, tq), 0).astype(F32)
    oh1 = erow == e1
    oh2 = erow == e2
    cnt = (jnp.where(oh1, 1.0, 0.0) + jnp.where(oh2, 1.0, 0.0)) * jnp.where(i > 0, 1.0, 0.0)
    ri = lax.broadcasted_iota(I32, (tq, tq), 0)
    ci = lax.broadcasted_iota(I32, (tq, tq), 1)
    earlier = jnp.where(ri < ci, 1.0, 0.0).astype(BF16)
    base = wide(counts_ref[...]) + jnp.dot(cnt.astype(BF16), earlier, preferred_element_type=F32)
    r1 = csum(jnp.where(oh1, base, 0.0))
    r2 = csum(jnp.where(oh2, base, 0.0))
    counts_ref[...] = counts_ref[...] + jnp.sum(cnt, axis=1, keepdims=True)

    route = jnp.where(sub == 0.0, e1, jnp.where(sub == 1.0, e2,
                      jnp.where(sub == 2.0, r1, jnp.where(sub == 3.0, r2, 0.0))))
    route_ref[...] = route.astype(I32)
    gates = jnp.where(sub == 0.0, w1, jnp.where(sub == 1.0, w2, 0.0))
    gate_ref[...] = jnp.concatenate([gates, jnp.zeros((LANES - SUBLANES, tq), F32)], axis=0).T


def _attn(x, ma, sgb, q, k, v, p, seq):
    t = x.shape[0]
    tq = TQ
    sub = tq // WINDOW
    last_blk = t // WINDOW - 1
    n_tiles = t // tq
    att = lambda i: jnp.minimum(i, n_tiles - 1)
    post = lambda i: jnp.maximum(i - 1, 0)
    att_row = lambda w: pl.BlockSpec((tq, w), lambda i: (att(i), 0))
    row = lambda w: pl.BlockSpec((tq, w), lambda i: (post(i), 0))
    prev = pl.BlockSpec((WINDOW, KV_W), lambda i: (jnp.maximum(att(i) * sub - 1, 0), 0))
    nxt = pl.BlockSpec((WINDOW, KV_W), lambda i: (jnp.minimum((att(i) + 1) * sub, last_blk), 0))
    return pl.pallas_call(
        functools.partial(_attn_kernel, tiles_per_seq=seq // tq, n_tiles=n_tiles),
        grid=(n_tiles + 1,),
        in_specs=[pl.BlockSpec(memory_space=pltpu.SMEM),
                  row(D_MODEL), row(D_MODEL), row(D_MODEL), att_row(Q_W),
                  prev, att_row(KV_W), nxt, prev, att_row(KV_W), nxt,
                  _const_spec((Q_W, D_MODEL)), _const_spec((D_MODEL, D_MODEL)), _const_spec((1, D_MODEL)),
                  _const_spec((ROUTER_ROWS, D_MODEL)), _const_spec((ROUTER_ROWS, LANES))],
        out_specs=[row(D_MODEL), _rows3_spec(tq, lambda i: (post(i), 0)),
                   pl.BlockSpec((SUBLANES, tq), lambda i: (0, post(i))), row(LANES),
                   pl.BlockSpec((N_EXPERTS, LANES), lambda i: (0, 0))],
        out_shape=[jax.ShapeDtypeStruct((t, D_MODEL), F32), jax.ShapeDtypeStruct((t * ROW_LINES, LANES), I32),
                   jax.ShapeDtypeStruct((SUBLANES, t), I32), jax.ShapeDtypeStruct((t, LANES), F32),
                   jax.ShapeDtypeStruct((N_EXPERTS, LANES), F32)],
        scratch_shapes=[pltpu.VMEM((tq + 2 * WINDOW, KV_W), BF16), pltpu.VMEM((tq + 2 * WINDOW, KV_W), BF16),
                        pltpu.VMEM((2, tq, Q_W), BF16),
                        pltpu.VMEM((sub * N_KV_HEADS, REP * WINDOW, 3 * WINDOW), F32),
                        pltpu.VMEM((sub * N_KV_HEADS, REP * WINDOW, 3 * WINDOW), BF16),
                        pltpu.VMEM((sub * N_KV_HEADS, REP * WINDOW, LANES), F32),
                        pltpu.VMEM((tq, D_MODEL), BF16)],
        compiler_params=pltpu.CompilerParams(dimension_semantics=("arbitrary",),
                                             vmem_limit_bytes=VMEM_LIMIT),
        name="attn",
    )(p["sink"], x, ma, sgb, q, k, k, k, v, v, v, p["w_proj_b"], p["w_out"], p["norm_ffn_g"],
      p["w_router"], p["b_router"])


def _dest_kernel(route_ref, starts_ref, dest_ref):
    route = route_ref[...].astype(F32)
    td = route.shape[1]
    sub = lax.broadcasted_iota(I32, route.shape, 0)
    erow = lax.broadcasted_iota(I32, (N_EXPERTS, td), 0).astype(F32)
    starts = jnp.broadcast_to(starts_ref[:, 0:1], (N_EXPERTS, td))

    def slot(k):
        start = jnp.sum(jnp.where(erow == route[k:k + 1], starts, 0.0), axis=0, keepdims=True)
        return start + route[TOP_K + k:TOP_K + k + 1]

    dest_ref[...] = jnp.where(sub == 0, slot(0), jnp.where(sub == 1, slot(1), 0.0)).astype(I32)


def _dest(route, pad_starts):
    t = route.shape[1]
    td = min(TM_DEST, t)
    return pl.pallas_call(
        _dest_kernel,
        grid=(t // td,),
        in_specs=[pl.BlockSpec((SUBLANES, td), lambda i: (0, i)), _const_spec((N_EXPERTS, LANES))],
        out_specs=pl.BlockSpec((SUBLANES, td), lambda i: (0, i)),
        out_shape=jax.ShapeDtypeStruct((SUBLANES, t), I32),
        compiler_params=pltpu.CompilerParams(dimension_semantics=("parallel",)),
        name="dest",
    )(route, pad_starts)


def _sc_mesh():
    return plsc.VectorSubcoreMesh(core_axis_name="c", subcore_axis_name="s")


def _sc_worker():
    return lax.axis_index("s") * SC_CORES + lax.axis_index("c")


def _sc_chunk(t):
    return min(SC_CHUNK_MAX, t // (SC_WORKERS * SUBLANES))


def _dispatch(hn, dests, n_rows):
    t = hn.shape[0]
    chunk = dests[0].shape[1]
    per_worker = t // SC_WORKERS
    n_chunks = per_worker // chunk
    idx = pltpu.VMEM((n_chunks, chunk), I32)

    @functools.partial(
        pl.kernel, mesh=_sc_mesh(), out_type=jax.ShapeDtypeStruct((n_rows,) + ROW3, I32),
        scratch_types=[idx, idx, pltpu.VMEM((chunk,) + ROW3, I32), pltpu.SemaphoreType.DMA])
    def scatter_rows(hn_hbm, d0_hbm, d1_hbm, xs_hbm, i0_v, i1_v, rows_v, sem):
        w = _sc_worker()
        pltpu.sync_copy(d0_hbm.at[pl.ds(w * n_chunks, n_chunks)], i0_v)
        pltpu.sync_copy(d1_hbm.at[pl.ds(w * n_chunks, n_chunks)], i1_v)

        @pl.loop(0, n_chunks)
        def _(j):
            pltpu.sync_copy(hn_hbm.at[pl.ds(w * per_worker + j * chunk, chunk)], rows_v)
            copies = [pltpu.make_async_copy(rows_v, xs_hbm.at[i_v.at[j]], sem) for i_v in (i0_v, i1_v)]
            for cp in copies:
                cp.start()
            for cp in copies:
                cp.wait()

    return scatter_rows(hn, *dests)


def _moe_block_rows(t):
    rows = MOE_ROWS_MAX
    while rows > CHUNK and (t * TOP_K) // N_EXPERTS < MOE_MIN_BLOCKS * rows:
        rows //= 2
    return rows


def _moe_kernel(be_ref, nvalid_ref, nused_ref, xs_ref, wg_ref, wu_ref, wd_ref, yb_ref, *w16_refs):
    i = pl.program_id(0)
    used = i < nused_ref[0]
    weights = w16_refs if w16_refs else (wg_ref, wu_ref, wd_ref)

    @pl.when(jnp.logical_not(used))
    def _():
        yb_ref[...] = jnp.zeros_like(yb_ref)

    @pl.when(used)
    def _():
        if w16_refs:
            @pl.when(jnp.logical_or(i == 0, be_ref[i] != be_ref[jnp.maximum(i - 1, 0)]))
            def _():
                for src, dst in zip((wg_ref, wu_ref, wd_ref), w16_refs):
                    dst[0] = src[0].astype(BF16)

        row = lax.broadcasted_iota(I32, (xs_ref.shape[0] // ROW_LINES, 1), 0)
        x = jnp.where(row < nvalid_ref[i], _load_rows3(xs_ref), 0.0).astype(BF16)
        gate = jnp.dot(x, weights[0][0], preferred_element_type=F32)
        up = jnp.dot(x, weights[1][0], preferred_element_type=F32)
        hid = (jax.nn.silu(gate) * up).astype(BF16)
        _store_rows3(yb_ref, jnp.dot(hid, weights[2][0], preferred_element_type=F32))


def _moe(block_e, n_valid, n_used, xs, experts, block_rows):
    n_blocks = xs.shape[0] // (block_rows * ROW_LINES)
    cast = experts[0].dtype != BF16

    def rows(i, be, nv, nu):
        return (jnp.minimum(i, nu[0] - 1), 0)

    def expert(i, be, nv, nu):
        return (be[jnp.minimum(i, nu[0] - 1)], 0, 0)

    w_specs = [pl.BlockSpec((1,) + w.shape[1:], expert) for w in experts]
    out_specs = [_rows3_spec(block_rows, lambda i, be, nv, nu: (i, 0))]
    out_shape = [jax.ShapeDtypeStruct(xs.shape, I32)]
    if cast:
        out_specs += w_specs
        out_shape += [jax.ShapeDtypeStruct(w.shape, BF16) for w in experts]
    yb, *w16 = pl.pallas_call(
        _moe_kernel,
        grid_spec=pltpu.PrefetchScalarGridSpec(
            num_scalar_prefetch=3,
            grid=(n_blocks,),
            in_specs=[_rows3_spec(block_rows, rows)] + w_specs,
            out_specs=out_specs,
        ),
        out_shape=out_shape,
        compiler_params=pltpu.CompilerParams(dimension_semantics=("arbitrary",),
                                             vmem_limit_bytes=VMEM_LIMIT),
        name="moe",
    )(block_e, n_valid, n_used, xs, *experts)
    return yb, (tuple(w16) if cast else experts)


def _gather(yb, dests):
    chunk = dests[0].shape[1]
    t = dests[0].shape[0] * chunk
    per_worker = t // SC_WORKERS
    n_chunks = per_worker // chunk
    idx = pltpu.VMEM((n_chunks, chunk), I32)
    out = jax.ShapeDtypeStruct((t,) + ROW3, I32)

    @functools.partial(
        pl.kernel, mesh=_sc_mesh(), out_type=(out, out),
        scratch_types=[idx, idx, pltpu.VMEM((chunk,) + ROW3, I32), pltpu.SemaphoreType.DMA])
    def gather_rows(yb_hbm, d0_hbm, d1_hbm, y0_hbm, y1_hbm, i0_v, i1_v, rows_v, sem):
        w = _sc_worker()
        pltpu.sync_copy(d0_hbm.at[pl.ds(w * n_chunks, n_chunks)], i0_v)
        pltpu.sync_copy(d1_hbm.at[pl.ds(w * n_chunks, n_chunks)], i1_v)

        @pl.loop(0, n_chunks)
        def _(j):
            rows = pl.ds(w * per_worker + j * chunk, chunk)
            for i_v, y_hbm in ((i0_v, y0_hbm), (i1_v, y1_hbm)):
                pltpu.async_copy(yb_hbm.at[i_v.at[j]], rows_v, sem).wait()
                pltpu.sync_copy(rows_v, y_hbm.at[rows])

    return gather_rows(yb, *dests)


def _combine_kernel(x1_ref, gate_ref, y0_ref, y1_ref, out_ref):
    gate = gate_ref[...]
    out_ref[...] = x1_ref[...] + (_load_rows3(y0_ref) * gate[:, 0:1] + _load_rows3(y1_ref) * gate[:, 1:2])


def _combine(x1, gate, y0, y1):
    t = x1.shape[0]
    tm = TM_ROW
    return pl.pallas_call(
        _combine_kernel,
        grid=(t // tm,),
        in_specs=[pl.BlockSpec((tm, D_MODEL), lambda i: (i, 0)),
                  pl.BlockSpec((tm, LANES), lambda i: (i, 0)),
                  _rows3_spec(tm, lambda i: (i, 0)), _rows3_spec(tm, lambda i: (i, 0))],
        out_specs=pl.BlockSpec((tm, D_MODEL), lambda i: (i, 0)),
        out_shape=jax.ShapeDtypeStruct((t, D_MODEL), F32),
        compiler_params=pltpu.CompilerParams(dimension_semantics=("parallel",)),
        name="combine",
    )(x1, gate, y0, y1)


def _rope_tables(seq):
    half = HEAD_DIM // 2
    inv_freq = ROPE_THETA ** (-jnp.arange(half, dtype=F32) / half)
    coarse = (jnp.arange(seq // ROPE_SPLIT) * ROPE_SPLIT).astype(F32)[:, None, None] * inv_freq
    fine = jnp.arange(ROPE_SPLIT).astype(F32)[None, :, None] * inv_freq
    cos = (jnp.cos(coarse) * jnp.cos(fine) - jnp.sin(coarse) * jnp.sin(fine)).reshape(seq, half)
    sin = (jnp.sin(coarse) * jnp.cos(fine) + jnp.cos(coarse) * jnp.sin(fine)).reshape(seq, half)
    return jnp.concatenate([cos, cos], axis=-1), jnp.concatenate([-sin, sin], axis=-1)


def _layer(x, p, seq, rope, experts):
    t = x.shape[0]
    ma, sgb, q, k, v = _inproj(x, p, *rope, seq)
    x1, hn, route, gate, counts_f = _attn(x, ma, sgb, q, k, v, p, seq)

    block_rows = _moe_block_rows(t)
    counts = counts_f[:, 0].astype(I32)
    min_blocks = 1 if experts[0].dtype != BF16 else 0
    padded = jnp.maximum((counts + block_rows - 1) // block_rows, min_blocks) * block_rows
    pad_ends = jnp.cumsum(padded)
    pad_starts = pad_ends - padded
    n_blocks = (t * TOP_K) // block_rows + N_EXPERTS
    block_start = jnp.arange(n_blocks, dtype=I32) * block_rows
    in_expert = jnp.logical_and(block_start[:, None] >= pad_starts[None, :],
                                block_start[:, None] < pad_ends[None, :]).astype(I32)
    block_e = jnp.minimum(jnp.sum((block_start[:, None] >= pad_ends[None, :]).astype(I32), axis=1), N_EXPERTS - 1)
    n_valid = jnp.sum(in_expert * jnp.clip(pad_starts + counts - block_start[:, None], 0, block_rows), axis=1)
    n_used = pad_ends[-1:] // block_rows
    starts_col = jnp.broadcast_to(pad_starts.astype(F32)[:, None], (N_EXPERTS, LANES))

    dest = _dest(route, starts_col)
    dests = [dest[k].reshape(t // _sc_chunk(t), _sc_chunk(t)) for k in range(TOP_K)]
    xs = _dispatch(_as_rows3(hn), dests, n_blocks * block_rows)
    yb, experts = _moe(block_e, n_valid, n_used, _as_lines(xs), experts, block_rows)
    y0, y1 = _gather(_as_rows3(yb), dests)
    return _combine(x1, gate, _as_lines(y0), _as_lines(y1)), experts


def kernel(x_prompt, x_sample, norm_mix_g, w_in, norm_v_g, w_spatial, b_spatial, q_norm_g, k_norm_g, sink,
           w_proj_a, w_proj_b, w_out, norm_ffn_g, w_router_group, b_router_group, w_router_expert,
           b_router_expert, w_gate_e, w_up_e, w_down_e):
    depth = w_in.shape[0]
    layers = []
    for l in range(depth):
        w_router = jnp.zeros((ROUTER_ROWS, D_MODEL), F32)
        w_router = w_router.at[:N_GROUPS].set(w_router_group[l].T)
        w_router = w_router.at[SUBLANES:SUBLANES + N_EXPERTS].set(w_router_expert[l].T)
        b_router = jnp.zeros((ROUTER_ROWS,), F32)
        b_router = b_router.at[:N_GROUPS].set(b_router_group[l])
        b_router = b_router.at[SUBLANES:SUBLANES + N_EXPERTS].set(b_router_expert[l])
        b_router = jnp.broadcast_to(b_router[:, None], (ROUTER_ROWS, LANES))
        layers.append(dict(
            norm_mix_g=norm_mix_g[l][None], w_in=w_in[l].astype(BF16), norm_v_g=norm_v_g[l][None],
            w_spatial=w_spatial[l].astype(BF16),
            b_spatial=jnp.broadcast_to(b_spatial[l][:, :, None], (A_GROUPS, CHUNK, LANES)),
            q_norm_g=q_norm_g[l][None], k_norm_g=k_norm_g[l][None], sink=sink[l],
            w_proj_a=w_proj_a[l].astype(BF16), w_proj_b=w_proj_b[l].astype(BF16), w_out=w_out[l].astype(BF16),
            norm_ffn_g=norm_ffn_g[l][None], w_router=w_router.astype(BF16), b_router=b_router,
            experts=(w_gate_e[l], w_up_e[l], w_down_e[l])))

    trunks = [x_prompt, x_sample]
    rows = [x.reshape(-1, D_MODEL) for x in trunks]
    order = sorted(range(len(trunks)), key=lambda n: -rows[n].shape[0])
    rope = _rope_tables(max(x.shape[1] for x in trunks))
    for p in layers:
        experts = p["experts"]
        for n in order:
            rows[n], experts = _layer(rows[n], p, trunks[n].shape[1], rope, experts)
    return tuple(r.reshape(x.shape) for r, x in zip(rows, trunks))
```

```python
import functools

import jax
import jax.numpy as jnp
from jax import lax
from jax.experimental import pallas as pl
from jax.experimental.pallas import tpu as pltpu
from jax.experimental.pallas import tpu_sc as plsc

F32 = jnp.float32
BF16 = jnp.bfloat16
I32 = jnp.int32
U32 = jnp.uint32

LANES = 128
SUBLANES = 8
VMEM_BYTES_V7X = 64 * 1024 * 1024
SC_CORES = 2
SC_SUBCORES = 16
SC_WORKERS = SC_CORES * SC_SUBCORES
SC_CHUNK_MAX = 64

D_MODEL = 1024
A_WIDTH = D_MODEL
A_GROUPS = 8
CHUNK = 128
HEAD_DIM = 128
N_Q_HEADS = D_MODEL // HEAD_DIM
N_KV_HEADS = 2
REP = N_Q_HEADS // N_KV_HEADS
WINDOW = 128
ROPE_THETA = 10000.0
ROPE_SPLIT = 64
Q_W = N_Q_HEADS * HEAD_DIM
KV_W = N_KV_HEADS * HEAD_DIM
IN_W = 2 * A_WIDTH + Q_W + 2 * KV_W + 2 * D_MODEL
COL_U = 0
COL_V = COL_U + A_WIDTH
COL_Q = COL_V + A_WIDTH
COL_K = COL_Q + Q_W
COL_VA = COL_K + KV_W
COL_GA = COL_VA + KV_W
COL_GB = COL_GA + D_MODEL
N_GROUPS = 4
EXPERTS_PER_GROUP = 8
N_EXPERTS = N_GROUPS * EXPERTS_PER_GROUP
TOP_K = 2
D_EXPERT = 512
EPS = 1e-6
NEG = -1e30

TM_IN = 512
TM_IN_SUB = 256
TQ = 512
DENSE_COLS = 256
ROUTER_ROWS = 64
assert EXPERTS_PER_GROUP == SUBLANES and SUBLANES + N_EXPERTS <= ROUTER_ROWS
TM_ROW = 1024
TM_DEST = 2048
MOE_ROWS_MAX = 1024
MOE_MIN_BLOCKS = 1
VMEM_RESERVE = 8 * 1024 * 1024
VMEM_LIMIT = VMEM_BYTES_V7X - VMEM_RESERVE


def _rms(x, g):
    return x * lax.rsqrt(jnp.mean(x * x, axis=-1, keepdims=True) + EPS) * g


ROW_LINES = D_MODEL // 2 // LANES
ROW3 = (ROW_LINES, LANES)
HIGH_HALF = 0xFFFF0000


def _as_rows3(a):
    return a.reshape((a.shape[0] // ROW_LINES,) + ROW3)


def _as_lines(a):
    return a.reshape((a.shape[0] * ROW_LINES, LANES))


def _store_rows3(lines_ref, val):
    rows = val.shape[0]
    bits = lax.bitcast_convert_type(val.astype(BF16).astype(F32), U32)
    half = ROW_LINES * LANES
    for s in range(ROW_LINES):
        lo = bits[:, s * LANES:(s + 1) * LANES] >> 16
        hi = bits[:, half + s * LANES:half + (s + 1) * LANES] & U32(HIGH_HALF)
        lines_ref[pl.ds(s, rows, stride=ROW_LINES), :] = lax.bitcast_convert_type(lo | hi, I32)


def _load_rows3(lines_ref):
    rows = lines_ref.shape[0] // ROW_LINES
    words = [lax.bitcast_convert_type(lines_ref[pl.ds(s, rows, stride=ROW_LINES), :], U32)
             for s in range(ROW_LINES)]
    lo = [lax.bitcast_convert_type(w << 16, F32) for w in words]
    hi = [lax.bitcast_convert_type(w & U32(HIGH_HALF), F32) for w in words]
    return jnp.concatenate(lo + hi, axis=1)


def _rows3_spec(rows, index_map):
    return pl.BlockSpec((rows * ROW_LINES, LANES), index_map)


def _const_spec(shape):
    nd = len(shape)
    return pl.BlockSpec(shape, lambda *_: (0,) * nd, pipeline_mode=pl.Buffered(1))


def _inproj_kernel(x_ref, gmix_ref, win_ref, gv_ref, ws_ref, bs_ref, gq_ref, gk_ref, cos_ref, sin_ref,
                   wpa_ref, ma_ref, sgb_ref, q_ref, k_ref, v_ref, h_scr, u_scr, vn_scr, a_scr):
    tm = x_ref.shape[0]

    def stages(rows):
        def proj(lo, width):
            return jnp.dot(h_scr[rows], win_ref[:, lo:lo + width], preferred_element_type=F32)

        def norm_rope(z, g):
            zn = _rms(z, g)
            return zn * cos_ref[rows] + pltpu.roll(zn, HEAD_DIM // 2, 1) * sin_ref[rows]

        def norm():
            h_scr[rows] = _rms(x_ref[rows], gmix_ref[...]).astype(BF16)

        def mix_v():
            vn_scr[rows] = _rms(jax.nn.gelu(proj(COL_V, A_WIDTH)), gv_ref[...]).astype(BF16)

        def mix_u():
            u_scr[rows] = jax.nn.gelu(proj(COL_U, A_WIDTH))

        def spatial():
            for c in range(rows.start, rows.stop, CHUNK):
                chunk = slice(c, c + CHUNK)
                for g in range(A_GROUPS):
                    cols = slice(g * LANES, (g + 1) * LANES)
                    mixed = jnp.dot(ws_ref[g], vn_scr[chunk, cols], preferred_element_type=F32) + bs_ref[g]
                    a_scr[chunk, cols] = (u_scr[chunk, cols] * mixed).astype(BF16)

        def gate_a():
            ya = jnp.dot(a_scr[rows], wpa_ref[...], preferred_element_type=F32)
            ma_ref[rows] = jax.nn.sigmoid(proj(COL_GA, D_MODEL)) * ya

        def gate_b():
            sgb_ref[rows] = jax.nn.sigmoid(proj(COL_GB, D_MODEL))

        def queries():
            qz = proj(COL_Q, Q_W)
            for hd in range(N_Q_HEADS):
                cols = slice(hd * HEAD_DIM, (hd + 1) * HEAD_DIM)
                q_ref[rows, cols] = norm_rope(qz[:, cols], gq_ref[...]).astype(BF16)

        def keys_values():
            kz = proj(COL_K, KV_W)
            for hd in range(N_KV_HEADS):
                cols = slice(hd * HEAD_DIM, (hd + 1) * HEAD_DIM)
                k_ref[rows, cols] = norm_rope(kz[:, cols], gk_ref[...]).astype(BF16)
            v_ref[rows] = proj(COL_VA, KV_W).astype(BF16)

        return [norm, mix_v, mix_u, spatial, gate_a, gate_b, queries, keys_values]

    subs = [stages(slice(r, r + TM_IN_SUB)) for r in range(0, tm, TM_IN_SUB)]
    n_stage = len(subs[0])
    for step in range(n_stage + len(subs) - 1):
        for n, sub in enumerate(subs):
            if 0 <= step - n < n_stage:
                sub[step - n]()


def _inproj(x, p, cos, sin, seq):
    t = x.shape[0]
    tm = TM_IN
    n_pos = seq // tm
    row = lambda w: pl.BlockSpec((tm, w), lambda i: (i, 0))
    pos = pl.BlockSpec((tm, HEAD_DIM), lambda i: (i % n_pos, 0))
    return pl.pallas_call(
        _inproj_kernel,
        grid=(t // tm,),
        in_specs=[row(D_MODEL), _const_spec((1, D_MODEL)), _const_spec((D_MODEL, IN_W)),
                  _const_spec((1, A_WIDTH)), _const_spec((A_GROUPS, CHUNK, CHUNK)),
                  _const_spec((A_GROUPS, CHUNK, LANES)), _const_spec((1, HEAD_DIM)),
                  _const_spec((1, HEAD_DIM)), pos, pos, _const_spec((A_WIDTH, D_MODEL))],
        out_specs=[row(D_MODEL), row(D_MODEL), row(Q_W), row(KV_W), row(KV_W)],
        out_shape=[jax.ShapeDtypeStruct((t, D_MODEL), F32), jax.ShapeDtypeStruct((t, D_MODEL), F32),
                   jax.ShapeDtypeStruct((t, Q_W), BF16), jax.ShapeDtypeStruct((t, KV_W), BF16),
                   jax.ShapeDtypeStruct((t, KV_W), BF16)],
        scratch_shapes=[pltpu.VMEM((tm, D_MODEL), BF16), pltpu.VMEM((tm, A_WIDTH), F32),
                        pltpu.VMEM((tm, A_WIDTH), BF16), pltpu.VMEM((tm, A_WIDTH), BF16)],
        compiler_params=pltpu.CompilerParams(dimension_semantics=("parallel",),
                                             vmem_limit_bytes=VMEM_LIMIT),
        name="inproj",
    )(x, p["norm_mix_g"], p["w_in"], p["norm_v_g"], p["w_spatial"], p["b_spatial"], p["q_norm_g"],
      p["k_norm_g"], cos, sin, p["w_proj_a"])


def _attn_kernel(sink_ref, x_ref, ma_ref, sgb_ref, q_ref, kp_ref, kc_ref, kn_ref, vp_ref, vc_ref, vn_ref,
                 wpb_ref, wout_ref, gffn_ref, wr_ref, br_ref,
                 x1_ref, hn_ref, route_ref, gate_ref, counts_ref, kcat, vcat, o_scr, s_scr, p_scr, sink_scr, m_scr,
                 *, tiles_per_seq, n_tiles):
    tq = x_ref.shape[0]
    blk = WINDOW
    i = pl.program_id(0)
    slot = i % 2

    @pl.when(i == 0)
    def _():
        o_scr[...] = jnp.zeros_like(o_scr)
        counts_ref[...] = jnp.zeros_like(counts_ref)

    pos_tile = jnp.minimum(i, n_tiles - 1) % tiles_per_seq
    has_prev = pos_tile > 0
    has_next = pos_tile < tiles_per_seq - 1

    kcat[0:blk] = kp_ref[...]
    kcat[blk:blk + tq] = kc_ref[...]
    kcat[blk + tq:] = kn_ref[...]
    vcat[0:blk] = vp_ref[...]
    vcat[blk:blk + tq] = vc_ref[...]
    vcat[blk + tq:] = vn_ref[...]

    qr = lax.broadcasted_iota(I32, (blk, blk), 0)
    kc = lax.broadcasted_iota(I32, (blk, blk), 1)
    scale = HEAD_DIM ** -0.5
    n_sub = tq // blk
    pairs =[(j, g) for j in range(n_sub) for g in range(N_KV_HEADS)]

    def keys(ref, j, g):
        return ref[j * blk:(j + 3) * blk, g * HEAD_DIM:(g + 1) * HEAD_DIM]

    def head_cols(g, r):
        hd = g * REP + r
        return slice(hd * HEAD_DIM, (hd + 1) * HEAD_DIM)

    for b, (j, g) in enumerate(pairs):
        rows = slice(j * blk, (j + 1) * blk)
        qs = jnp.concatenate([q_ref[rows, head_cols(g, r)] for r in range(REP)], axis=0)
        s_scr[b] = lax.dot_general(qs, keys(kcat, j, g), (((1,), (1,)), ((), ())),
                                   preferred_element_type=F32)
    log2e = 1.4426950408889634

    def softmax(b, r):
        j, g = pairs[b]
        hrows = slice(r * blk, (r + 1) * blk)
        z = s_scr[b, hrows, :] * (scale * log2e)
        lo_ok = kc >= (qr + jnp.where(has_prev, 0, blk) if j == 0 else qr)
        hi_ok = kc <= (qr - jnp.where(has_next, 0, blk) if j == n_sub - 1 else qr)
        z = jnp.concatenate([jnp.where(lo_ok, z[:, :blk], NEG), z[:, blk:2 * blk],
                             jnp.where(hi_ok, z[:, 2 * blk:], NEG)], axis=1)
        sink = sink_ref[g * REP + r] * log2e
        m = jnp.maximum(jnp.max(z, axis=-1, keepdims=True), sink)
        p_scr[b, hrows, :] = jnp.exp2(z - m).astype(BF16)
        sink_scr[b, hrows, :] = jnp.broadcast_to(jnp.exp2(sink - m), (blk, LANES))

    def values(b):
        j, g = pairs[b]
        rows = slice(j * blk, (j + 1) * blk)
        v_ext = jnp.concatenate([keys(vcat, j, g), jnp.ones((3 * blk, HEAD_DIM), BF16)], axis=1)
        acc = jnp.dot(p_scr[b], v_ext, preferred_element_type=F32)
        o = (acc[:, :HEAD_DIM] / (acc[:, HEAD_DIM:] + sink_scr[b])).astype(BF16)
        for r in range(REP):
            o_scr[slot, rows, head_cols(g, r)] = o[r * blk:(r + 1) * blk, :]

    def merged_cols(cols):
        yb = jnp.dot(o_scr[1 - slot], wpb_ref[:, cols], preferred_element_type=F32)
        m_scr[:, cols] = (ma_ref[:, cols] + sgb_ref[:, cols] * yb).astype(BF16)

    def x1_cols(cols):
        x1_ref[:, cols] = x_ref[:, cols] + jnp.dot(m_scr[...], wout_ref[:, cols], preferred_element_type=F32)

    col_chunks = [slice(c * DENSE_COLS, (c + 1) * DENSE_COLS) for c in range(D_MODEL // DENSE_COLS)]
    dense = [functools.partial(f, cols) for f in (merged_cols, x1_cols) for cols in col_chunks]
    units = [(b, r) for b in range(len(pairs)) for r in range(REP)]
    units_per_dense = len(units) // len(dense)
    for n, (b, r) in enumerate(units):
        softmax(b, r)
        if (n + 1) % units_per_dense == 0:
            dense[(n + 1) // units_per_dense - 1]()
        if r == REP - 1:
            values(b)

    hn = _rms(x1_ref[...], gffn_ref[...])
    _store_rows3(hn_ref, hn)
    def wide(a):
        return jnp.concatenate([a] * (tq // LANES), axis=1)

    logits = lax.dot_general(wr_ref[...], hn.astype(BF16), (((1,), (1,)), ((), ())),
                             preferred_element_type=F32) + wide(br_ref[...])
    sub = lax.broadcasted_iota(I32, (SUBLANES, tq), 0).astype(F32)
    ninf = -jnp.inf

    def cmax(a):
        return jnp.max(a, axis=0, keepdims=True)

    def csum(a):
        return jnp.sum(a, axis=0, keepdims=True)

    def first_row(mask):
        return jnp.min(jnp.where(mask, sub, float(SUBLANES)), axis=0, keepdims=True)

    def group_rows(g):
        return logits[(g + 1) * SUBLANES:(g + 2) * SUBLANES]

    gl = jnp.where(sub < N_GROUPS, logits[0:SUBLANES], ninf)
    gmax = cmax(gl)
    g_sel = first_row(gl == gmax)
    g_p = 1.0 / csum(jnp.exp(gl - gmax))
    el = group_rows(0)
    for g in range(1, N_GROUPS):
        el = jnp.where(g_sel == g, group_rows(g), el)
    ee = jnp.exp(el - cmax(el))
    eprob = ee / csum(ee)
    p1 = cmax(eprob)
    i1 = first_row(eprob == p1)
    eprob2 = jnp.where(sub == i1, -1.0, eprob)
    p2 = cmax(eprob2)
    i2 = first_row(eprob2 == p2)
    psum = p1 + p2
    w1 = g_p * p1 / psum
    w2 = g_p * p2 / psum
    e1 = g_sel * EXPERTS_PER_GROUP + i1
    e2 = g_sel * EXPERTS_PER_GROUP + i2

    erow = lax.broadcasted_iota(I32, (N_EXPERTS, tq), 0).astype(F32)
    oh1 = erow == e1
    oh2 = erow == e2
    cnt = (jnp.where(oh1, 1.0, 0.0) + jnp.where(oh2, 1.0, 0.0)) * jnp.where(i > 0, 1.0, 0.0)
    ri = lax.broadcasted_iota(I32, (tq, tq), 0)
    ci = lax.broadcasted_iota(I32, (tq, tq), 1)
    earlier = jnp.where(ri < ci, 1.0, 0.0).astype(BF16)
    base = wide(counts_ref[...]) + jnp.dot(cnt.astype(BF16), earlier, preferred_element_type=F32)
    r1 = csum(jnp.where(oh1, base, 0.0))
    r2 = csum(jnp.where(oh2, base, 0.0))
    counts_ref[...] = counts_ref[...] + jnp.sum(cnt, axis=1, keepdims=True)

    route = jnp.where(sub == 0.0, e1, jnp.where(sub == 1.0, e2,
                      jnp.where(sub == 2.0, r1, jnp.where(sub == 3.0, r2, 0.0))))
    route_ref[...] = route.astype(I32)
    gates = jnp.where(sub == 0.0, w1, jnp.where(sub == 1.0, w2, 0.0))
    gate_ref[...] = jnp.concatenate([gates, jnp.zeros((LANES - SUBLANES, tq), F32)], axis=0).T


def _attn(x, ma, sgb, q, k, v, p, seq):
    t = x.shape[0]
    tq = TQ
    sub = tq // WINDOW
    last_blk = t // WINDOW - 1
    n_tiles = t // tq
    att = lambda i: jnp.minimum(i, n_tiles - 1)
    post = lambda i: jnp.maximum(i - 1, 0)
    att_row = lambda w: pl.BlockSpec((tq, w), lambda i: (att(i), 0))
    row = lambda w: pl.BlockSpec((tq, w), lambda i: (post(i), 0))
    prev = pl.BlockSpec((WINDOW, KV_W), lambda i: (jnp.maximum(att(i) * sub - 1, 0), 0))
    nxt = pl.BlockSpec((WINDOW, KV_W), lambda i: (jnp.minimum((att(i) + 1) * sub, last_blk), 0))
    return pl.pallas_call(
        functools.partial(_attn_kernel, tiles_per_seq=seq // tq, n_tiles=n_tiles),
        grid=(n_tiles + 1,),
        in_specs=[pl.BlockSpec(memory_space=pltpu.SMEM),
                  row(D_MODEL), row(D_MODEL), row(D_MODEL), att_row(Q_W),
                  prev, att_row(KV_W), nxt, prev, att_row(KV_W), nxt,
                  _const_spec((Q_W, D_MODEL)), _const_spec((D_MODEL, D_MODEL)), _const_spec((1, D_MODEL)),
                  _const_spec((ROUTER_ROWS, D_MODEL)), _const_spec((ROUTER_ROWS, LANES))],
        out_specs=[row(D_MODEL), _rows3_spec(tq, lambda i: (post(i), 0)),
                   pl.BlockSpec((SUBLANES, tq), lambda i: (0, post(i))), row(LANES),
                   pl.BlockSpec((N_EXPERTS, LANES), lambda i: (0, 0))],
        out_shape=[jax.ShapeDtypeStruct((t, D_MODEL), F32), jax.ShapeDtypeStruct((t * ROW_LINES, LANES), I32),
                   jax.ShapeDtypeStruct((SUBLANES, t), I32), jax.ShapeDtypeStruct((t, LANES), F32),
                   jax.ShapeDtypeStruct((N_EXPERTS, LANES), F32)],
        scratch_shapes=[pltpu.VMEM((tq + 2 * WINDOW, KV_W), BF16), pltpu.VMEM((tq + 2 * WINDOW, KV_W), BF16),
                        pltpu.VMEM((2, tq, Q_W), BF16),
                        pltpu.VMEM((sub * N_KV_HEADS, REP * WINDOW, 3 * WINDOW), F32),
                        pltpu.VMEM((sub * N_KV_HEADS, REP * WINDOW, 3 * WINDOW), BF16),
                        pltpu.VMEM((sub * N_KV_HEADS, REP * WINDOW, LANES), F32),
                        pltpu.VMEM((tq, D_MODEL), BF16)],
        compiler_params=pltpu.CompilerParams(dimension_semantics=("arbitrary",),
                                             vmem_limit_bytes=VMEM_LIMIT),
        name="attn",
    )(p["sink"], x, ma, sgb, q, k, k, k, v, v, v, p["w_proj_b"], p["w_out"], p["norm_ffn_g"],
      p["w_router"], p["b_router"])


def _dest_kernel(route_ref, starts_ref, dest_ref):
    route = route_ref[...].astype(F32)
    td = route.shape[1]
    sub = lax.broadcasted_iota(I32, route.shape, 0)
    erow = lax.broadcasted_iota(I32, (N_EXPERTS, td), 0).astype(F32)
    starts = jnp.broadcast_to(starts_ref[:, 0:1], (N_EXPERTS, td))

    def slot(k):
        start = jnp.sum(jnp.where(erow == route[k:k + 1], starts, 0.0), axis=0, keepdims=True)
        return start + route[TOP_K + k:TOP_K + k + 1]

    dest_ref[...] = jnp.where(sub == 0, slot(0), jnp.where(sub == 1, slot(1), 0.0)).astype(I32)


def _dest(route, pad_starts):
    t = route.shape[1]
    td = min(TM_DEST, t)
    return pl.pallas_call(
        _dest_kernel,
        grid=(t // td,),
        in_specs=[pl.BlockSpec((SUBLANES, td), lambda i: (0, i)), _const_spec((N_EXPERTS, LANES))],
        out_specs=pl.BlockSpec((SUBLANES, td), lambda i: (0, i)),
        out_shape=jax.ShapeDtypeStruct((SUBLANES, t), I32),
        compiler_params=pltpu.CompilerParams(dimension_semantics=("parallel",)),
        name="dest",
    )(route, pad_starts)


def _sc_mesh():
    return plsc.VectorSubcoreMesh(core_axis_name="c", subcore_axis_name="s")


def _sc_worker():
    return lax.axis_index("s") * SC_CORES + lax.axis_index("c")


def _sc_chunk(t):
    return min(SC_CHUNK_MAX, t // (SC_WORKERS * SUBLANES))


def _dispatch(hn, dests, n_rows):
    t = hn.shape[0]
    chunk = dests[0].shape[1]
    per_worker = t // SC_WORKERS
    n_chunks = per_worker // chunk
    idx = pltpu.VMEM((n_chunks, chunk), I32)

    @functools.partial(
        pl.kernel, mesh=_sc_mesh(), out_type=jax.ShapeDtypeStruct((n_rows,) + ROW3, I32),
        scratch_types=[idx, idx, pltpu.VMEM((chunk,) + ROW3, I32), pltpu.SemaphoreType.DMA])
    def scatter_rows(hn_hbm, d0_hbm, d1_hbm, xs_hbm, i0_v, i1_v, rows_v, sem):
        w = _sc_worker()
        pltpu.sync_copy(d0_hbm.at[pl.ds(w * n_chunks, n_chunks)], i0_v)
        pltpu.sync_copy(d1_hbm.at[pl.ds(w * n_chunks, n_chunks)], i1_v)

        @pl.loop(0, n_chunks)
        def _(j):
            pltpu.sync_copy(hn_hbm.at[pl.ds(w * per_worker + j * chunk, chunk)], rows_v)
            copies = [pltpu.make_async_copy(rows_v, xs_hbm.at[i_v.at[j]], sem) for i_v in (i0_v, i1_v)]
            for cp in copies:
                cp.start()
            for cp in copies:
                cp.wait()

    return scatter_rows(hn, *dests)


def _moe_block_rows(t):
    rows = MOE_ROWS_MAX
    while rows > CHUNK and (t * TOP_K) // N_EXPERTS < MOE_MIN_BLOCKS * rows:
        rows //= 2
    return rows


def _moe_kernel(be_ref, nvalid_ref, nused_ref, xs_ref, wg_ref, wu_ref, wd_ref, yb_ref, *w16_refs):
    i = pl.program_id(0)
    used = i < nused_ref[0]
    weights = w16_refs if w16_refs else (wg_ref, wu_ref, wd_ref)

    @pl.when(jnp.logical_not(used))
    def _():
        yb_ref[...] = jnp.zeros_like(yb_ref)

    @pl.when(used)
    def _():
        if w16_refs:
            @pl.when(jnp.logical_or(i == 0, be_ref[i] != be_ref[jnp.maximum(i - 1, 0)]))
            def _():
                for src, dst in zip((wg_ref, wu_ref, wd_ref), w16_refs):
                    dst[0] = src[0].astype(BF16)

        row = lax.broadcasted_iota(I32, (xs_ref.shape[0] // ROW_LINES, 1), 0)
        x = jnp.where(row < nvalid_ref[i], _load_rows3(xs_ref), 0.0).astype(BF16)
        gate = jnp.dot(x, weights[0][0], preferred_element_type=F32)
        up = jnp.dot(x, weights[1][0], preferred_element_type=F32)
        hid = (jax.nn.silu(gate) * up).astype(BF16)
        _store_rows3(yb_ref, jnp.dot(hid, weights[2][0], preferred_element_type=F32))


def _moe(block_e, n_valid, n_used, xs, experts, block_rows):
    n_blocks = xs.shape[0] // (block_rows * ROW_LINES)
    cast = experts[0].dtype != BF16

    def rows(i, be, nv, nu):
        return (jnp.minimum(i, nu[0] - 1), 0)

    def expert(i, be, nv, nu):
        return (be[jnp.minimum(i, nu[0] - 1)], 0, 0)

    w_specs = [pl.BlockSpec((1,) + w.shape[1:], expert) for w in experts]
    out_specs = [_rows3_spec(block_rows, lambda i, be, nv, nu: (i, 0))]
    out_shape = [jax.ShapeDtypeStruct(xs.shape, I32)]
    if cast:
        out_specs += w_specs
        out_shape += [jax.ShapeDtypeStruct(w.shape, BF16) for w in experts]
    yb, *w16 = pl.pallas_call(
        _moe_kernel,
        grid_spec=pltpu.PrefetchScalarGridSpec(
            num_scalar_prefetch=3,
            grid=(n_blocks,),
            in_specs=[_rows3_spec(block_rows, rows)] + w_specs,
            out_specs=out_specs,
        ),
        out_shape=out_shape,
        compiler_params=pltpu.CompilerParams(dimension_semantics=("arbitrary",),
                                             vmem_limit_bytes=VMEM_LIMIT),
        name="moe",
    )(block_e, n_valid, n_used, xs, *experts)
    return yb, (tuple(w16) if cast else experts)


def _gather(yb, dests):
    chunk = dests[0].shape[1]
    t = dests[0].shape[0] * chunk
    per_worker = t // SC_WORKERS
    n_chunks = per_worker // chunk
    idx = pltpu.VMEM((n_chunks, chunk), I32)
    buf = pltpu.VMEM((chunk,) + ROW3, I32)
    out = jax.ShapeDtypeStruct((t,) + ROW3, I32)

    @functools.partial(
        pl.kernel, mesh=_sc_mesh(), out_type=(out, out),
        scratch_types=[idx, idx, buf, buf, pltpu.SemaphoreType.DMA, pltpu.SemaphoreType.DMA])
    def gather_rows(yb_hbm, d0_hbm, d1_hbm, y0_hbm, y1_hbm, i0_v, i1_v, rows0_v, rows1_v, sem0, sem1):
        w = _sc_worker()
        pltpu.sync_copy(d0_hbm.at[pl.ds(w * n_chunks, n_chunks)], i0_v)
        pltpu.sync_copy(d1_hbm.at[pl.ds(w * n_chunks, n_chunks)], i1_v)

        @pl.loop(0, n_chunks)
        def _(j):
            rows = pl.ds(w * per_worker + j * chunk, chunk)
            g0 = pltpu.make_async_copy(yb_hbm.at[i0_v.at[j]], rows0_v, sem0)
            g1 = pltpu.make_async_copy(yb_hbm.at[i1_v.at[j]], rows1_v, sem1)
            g0.start()
            g1.start()
            g0.wait()
            pltpu.sync_copy(rows0_v, y0_hbm.at[rows])
            g1.wait()
            pltpu.sync_copy(rows1_v, y1_hbm.at[rows])

    return gather_rows(yb, *dests)


def _combine_kernel(x1_ref, gate_ref, y0_ref, y1_ref, out_ref):
    gate = gate_ref[...]
    out_ref[...] = x1_ref[...] + (_load_rows3(y0_ref) * gate[:, 0:1] + _load_rows3(y1_ref) * gate[:, 1:2])


def _combine(x1, gate, y0, y1):
    t = x1.shape[0]
    tm = TM_ROW
    return pl.pallas_call(
        _combine_kernel,
        grid=(t // tm,),
        in_specs=[pl.BlockSpec((tm, D_MODEL), lambda i: (i, 0)),
                  pl.BlockSpec((tm, LANES), lambda i: (i, 0)),
                  _rows3_spec(tm, lambda i: (i, 0)), _rows3_spec(tm, lambda i: (i, 0))],
        out_specs=pl.BlockSpec((tm, D_MODEL), lambda i: (i, 0)),
        out_shape=jax.ShapeDtypeStruct((t, D_MODEL), F32),
        compiler_params=pltpu.CompilerParams(dimension_semantics=("parallel",)),
        name="combine",
    )(x1, gate, y0, y1)


def _rope_tables(seq):
    half = HEAD_DIM // 2
    inv_freq = ROPE_THETA ** (-jnp.arange(half, dtype=F32) / half)
    coarse = (jnp.arange(seq // ROPE_SPLIT) * ROPE_SPLIT).astype(F32)[:, None, None] * inv_freq
    fine = jnp.arange(ROPE_SPLIT).astype(F32)[None, :, None] * inv_freq
    cos = (jnp.cos(coarse) * jnp.cos(fine) - jnp.sin(coarse) * jnp.sin(fine)).reshape(seq, half)
    sin = (jnp.sin(coarse) * jnp.cos(fine) + jnp.cos(coarse) * jnp.sin(fine)).reshape(seq, half)
    return jnp.concatenate([cos, cos], axis=-1), jnp.concatenate([-sin, sin], axis=-1)


def _layer(x, p, seq, rope, experts):
    t = x.shape[0]
    ma, sgb, q, k, v = _inproj(x, p, *rope, seq)
    x1, hn, route, gate, counts_f = _attn(x, ma, sgb, q, k, v, p, seq)

    block_rows = _moe_block_rows(t)
    counts = counts_f[:, 0].astype(I32)
    min_blocks = 1 if experts[0].dtype != BF16 else 0
    padded = jnp.maximum((counts + block_rows - 1) // block_rows, min_blocks) * block_rows
    pad_ends = jnp.cumsum(padded)
    pad_starts = pad_ends - padded
    n_blocks = (t * TOP_K) // block_rows + N_EXPERTS
    block_start = jnp.arange(n_blocks, dtype=I32) * block_rows
    in_expert = jnp.logical_and(block_start[:, None] >= pad_starts[None, :],
                                block_start[:, None] < pad_ends[None, :]).astype(I32)
    block_e = jnp.minimum(jnp.sum((block_start[:, None] >= pad_ends[None, :]).astype(I32), axis=1), N_EXPERTS - 1)
    n_valid = jnp.sum(in_expert * jnp.clip(pad_starts + counts - block_start[:, None], 0, block_rows), axis=1)
    n_used = pad_ends[-1:] // block_rows
    starts_col = jnp.broadcast_to(pad_starts.astype(F32)[:, None], (N_EXPERTS, LANES))

    dest = _dest(route, starts_col)
    dests = [dest[k].reshape(t // _sc_chunk(t), _sc_chunk(t)) for k in range(TOP_K)]
    xs = _dispatch(_as_rows3(hn), dests, n_blocks * block_rows)
    yb, experts = _moe(block_e, n_valid, n_used, _as_lines(xs), experts, block_rows)
    y0, y1 = _gather(_as_rows3(yb), dests)
    return _combine(x1, gate, _as_lines(y0), _as_lines(y1)), experts


def kernel(x_prompt, x_sample, norm_mix_g, w_in, norm_v_g, w_spatial, b_spatial, q_norm_g, k_norm_g, sink,
           w_proj_a, w_proj_b, w_out, norm_ffn_g, w_router_group, b_router_group, w_router_expert,
           b_router_expert, w_gate_e, w_up_e, w_down_e):
    depth = w_in.shape[0]
    layers = []
    for l in range(depth):
        w_router = jnp.zeros((ROUTER_ROWS, D_MODEL), F32)
        w_router = w_router.at[:N_GROUPS].set(w_router_group[l].T)
        w_router = w_router.at[SUBLANES:SUBLANES + N_EXPERTS].set(w_router_expert[l].T)
        b_router = jnp.zeros((ROUTER_ROWS,), F32)
        b_router = b_router.at[:N_GROUPS].set(b_router_group[l])
        b_router = b_router.at[SUBLANES:SUBLANES + N_EXPERTS].set(b_router_expert[l])
        b_router = jnp.broadcast_to(b_router[:, None], (ROUTER_ROWS, LANES))
        layers.append(dict(
            norm_mix_g=norm_mix_g[l][None], w_in=w_in[l].astype(BF16), norm_v_g=norm_v_g[l][None],
            w_spatial=w_spatial[l].astype(BF16),
            b_spatial=jnp.broadcast_to(b_spatial[l][:, :, None], (A_GROUPS, CHUNK, LANES)),
            q_norm_g=q_norm_g[l][None], k_norm_g=k_norm_g[l][None], sink=sink[l],
            w_proj_a=w_proj_a[l].astype(BF16), w_proj_b=w_proj_b[l].astype(BF16), w_out=w_out[l].astype(BF16),
            norm_ffn_g=norm_ffn_g[l][None], w_router=w_router.astype(BF16), b_router=b_router,
            experts=(w_gate_e[l], w_up_e[l], w_down_e[l])))

    trunks = [x_prompt, x_sample]
    rows = [x.reshape(-1, D_MODEL) for x in trunks]
    order = sorted(range(len(trunks)), key=lambda n: -rows[n].shape[0])
    rope = _rope_tables(max(x.shape[1] for x in trunks))
    for p in layers:
        experts = p["experts"]
        for n in order:
            rows[n], experts = _layer(rows[n], p, trunks[n].shape[1], rope, experts)
    return tuple(r.reshape(x.shape) for r, x in zip(rows, trunks))
```

```python
import functools

import jax
import jax.numpy as jnp
from jax import lax
from jax.experimental import pallas as pl
from jax.experimental.pallas import tpu as pltpu
from jax.experimental.pallas import tpu_sc as plsc

F32 = jnp.float32
BF16 = jnp.bfloat16
I32 = jnp.int32
U32 = jnp.uint32

LANES = 128
SUBLANES = 8
VMEM_BYTES_V7X = 64 * 1024 * 1024
SC_CORES = 2
SC_SUBCORES = 16
SC_WORKERS = SC_CORES * SC_SUBCORES
SC_CHUNK_MAX = 128

D_MODEL = 1024
A_WIDTH = D_MODEL
A_GROUPS = 8
CHUNK = 128
HEAD_DIM = 128
N_Q_HEADS = D_MODEL // HEAD_DIM
N_KV_HEADS = 2
REP = N_Q_HEADS // N_KV_HEADS
WINDOW = 128
ROPE_THETA = 10000.0
ROPE_SPLIT = 64
Q_W = N_Q_HEADS * HEAD_DIM
KV_W = N_KV_HEADS * HEAD_DIM
IN_W = 2 * A_WIDTH + Q_W + 2 * KV_W + 2 * D_MODEL
COL_U = 0
COL_V = COL_U + A_WIDTH
COL_Q = COL_V + A_WIDTH
COL_K = COL_Q + Q_W
COL_VA = COL_K + KV_W
COL_GA = COL_VA + KV_W
COL_GB = COL_GA + D_MODEL
N_GROUPS = 4
EXPERTS_PER_GROUP = 8
N_EXPERTS = N_GROUPS * EXPERTS_PER_GROUP
TOP_K = 2
D_EXPERT = 512
EPS = 1e-6
NEG = -1e30

TM_IN = 512
TM_IN_SUB = 256
TQ = 512
DENSE_COLS = 256
ROUTER_ROWS = 64
assert EXPERTS_PER_GROUP == SUBLANES and SUBLANES + N_EXPERTS <= ROUTER_ROWS
TM_ROW = 1024
TM_DEST = 2048
MOE_ROWS_MAX = 1024
MOE_MIN_BLOCKS = 1
VMEM_RESERVE = 8 * 1024 * 1024
VMEM_LIMIT = VMEM_BYTES_V7X - VMEM_RESERVE


def _rms(x, g):
    return x * lax.rsqrt(jnp.mean(x * x, axis=-1, keepdims=True) + EPS) * g


ROW_LINES = D_MODEL // 2 // LANES
ROW3 = (ROW_LINES, LANES)
HIGH_HALF = 0xFFFF0000


def _as_rows3(a):
    return a.reshape((a.shape[0] // ROW_LINES,) + ROW3)


def _as_lines(a):
    return a.reshape((a.shape[0] * ROW_LINES, LANES))


def _store_rows3(lines_ref, val):
    rows = val.shape[0]
    bits = lax.bitcast_convert_type(val.astype(BF16).astype(F32), U32)
    half = ROW_LINES * LANES
    for s in range(ROW_LINES):
        lo = bits[:, s * LANES:(s + 1) * LANES] >> 16
        hi = bits[:, half + s * LANES:half + (s + 1) * LANES] & U32(HIGH_HALF)
        lines_ref[pl.ds(s, rows, stride=ROW_LINES), :] = lax.bitcast_convert_type(lo | hi, I32)


def _load_rows3(lines_ref):
    rows = lines_ref.shape[0] // ROW_LINES
    words = [lax.bitcast_convert_type(lines_ref[pl.ds(s, rows, stride=ROW_LINES), :], U32)
             for s in range(ROW_LINES)]
    lo = [lax.bitcast_convert_type(w << 16, F32) for w in words]
    hi = [lax.bitcast_convert_type(w & U32(HIGH_HALF), F32) for w in words]
    return jnp.concatenate(lo + hi, axis=1)


def _rows3_spec(rows, index_map):
    return pl.BlockSpec((rows * ROW_LINES, LANES), index_map)


def _const_spec(shape):
    nd = len(shape)
    return pl.BlockSpec(shape, lambda *_: (0,) * nd, pipeline_mode=pl.Buffered(1))


def _inproj_kernel(x_ref, gmix_ref, win_ref, gv_ref, ws_ref, bs_ref, gq_ref, gk_ref, cos_ref, sin_ref,
                   wpa_ref, ma_ref, sgb_ref, q_ref, k_ref, v_ref, h_scr, u_scr, vn_scr, a_scr):
    tm = x_ref.shape[0]

    def stages(rows):
        def proj(lo, width):
            return jnp.dot(h_scr[rows], win_ref[:, lo:lo + width], preferred_element_type=F32)

        def norm_rope(z, g):
            zn = _rms(z, g)
            return zn * cos_ref[rows] + pltpu.roll(zn, HEAD_DIM // 2, 1) * sin_ref[rows]

        def norm():
            h_scr[rows] = _rms(x_ref[rows], gmix_ref[...]).astype(BF16)

        def mix_v():
            vn_scr[rows] = _rms(jax.nn.gelu(proj(COL_V, A_WIDTH)), gv_ref[...]).astype(BF16)

        def mix_u():
            u_scr[rows] = jax.nn.gelu(proj(COL_U, A_WIDTH))

        def spatial():
            for c in range(rows.start, rows.stop, CHUNK):
                chunk = slice(c, c + CHUNK)
                for g in range(A_GROUPS):
                    cols = slice(g * LANES, (g + 1) * LANES)
                    mixed = jnp.dot(ws_ref[g], vn_scr[chunk, cols], preferred_element_type=F32) + bs_ref[g]
                    a_scr[chunk, cols] = (u_scr[chunk, cols] * mixed).astype(BF16)

        def gate_a():
            ya = jnp.dot(a_scr[rows], wpa_ref[...], preferred_element_type=F32)
            ma_ref[rows] = jax.nn.sigmoid(proj(COL_GA, D_MODEL)) * ya

        def gate_b():
            sgb_ref[rows] = jax.nn.sigmoid(proj(COL_GB, D_MODEL))

        def queries():
            qz = proj(COL_Q, Q_W)
            for hd in range(N_Q_HEADS):
                cols = slice(hd * HEAD_DIM, (hd + 1) * HEAD_DIM)
                q_ref[rows, cols] = norm_rope(qz[:, cols], gq_ref[...]).astype(BF16)

        def keys_values():
            kz = proj(COL_K, KV_W)
            for hd in range(N_KV_HEADS):
                cols = slice(hd * HEAD_DIM, (hd + 1) * HEAD_DIM)
                k_ref[rows, cols] = norm_rope(kz[:, cols], gk_ref[...]).astype(BF16)
            v_ref[rows] = proj(COL_VA, KV_W).astype(BF16)

        return [norm, mix_v, mix_u, spatial, gate_a, gate_b, queries, keys_values]

    subs = [stages(slice(r, r + TM_IN_SUB)) for r in range(0, tm, TM_IN_SUB)]
    n_stage = len(subs[0])
    for step in range(n_stage + len(subs) - 1):
        for n, sub in enumerate(subs):
            if 0 <= step - n < n_stage:
                sub[step - n]()


def _inproj(x, p, cos, sin, seq):
    t = x.shape[0]
    tm = TM_IN
    n_pos = seq // tm
    row = lambda w: pl.BlockSpec((tm, w), lambda i: (i, 0))
    pos = pl.BlockSpec((tm, HEAD_DIM), lambda i: (i % n_pos, 0))
    return pl.pallas_call(
        _inproj_kernel,
        grid=(t // tm,),
        in_specs=[row(D_MODEL), _const_spec((1, D_MODEL)), _const_spec((D_MODEL, IN_W)),
                  _const_spec((1, A_WIDTH)), _const_spec((A_GROUPS, CHUNK, CHUNK)),
                  _const_spec((A_GROUPS, CHUNK, LANES)), _const_spec((1, HEAD_DIM)),
                  _const_spec((1, HEAD_DIM)), pos, pos, _const_spec((A_WIDTH, D_MODEL))],
        out_specs=[row(D_MODEL), row(D_MODEL), row(Q_W), row(KV_W), row(KV_W)],
        out_shape=[jax.ShapeDtypeStruct((t, D_MODEL), F32), jax.ShapeDtypeStruct((t, D_MODEL), F32),
                   jax.ShapeDtypeStruct((t, Q_W), BF16), jax.ShapeDtypeStruct((t, KV_W), BF16),
                   jax.ShapeDtypeStruct((t, KV_W), BF16)],
        scratch_shapes=[pltpu.VMEM((tm, D_MODEL), BF16), pltpu.VMEM((tm, A_WIDTH), F32),
                        pltpu.VMEM((tm, A_WIDTH), BF16), pltpu.VMEM((tm, A_WIDTH), BF16)],
        compiler_params=pltpu.CompilerParams(dimension_semantics=("parallel",),
                                             vmem_limit_bytes=VMEM_LIMIT),
        name="inproj",
    )(x, p["norm_mix_g"], p["w_in"], p["norm_v_g"], p["w_spatial"], p["b_spatial"], p["q_norm_g"],
      p["k_norm_g"], cos, sin, p["w_proj_a"])


def _attn_kernel(sink_ref, x_ref, ma_ref, sgb_ref, q_ref, kp_ref, kc_ref, kn_ref, vp_ref, vc_ref, vn_ref,
                 wpb_ref, wout_ref, gffn_ref, wr_ref, br_ref,
                 x1_ref, hn_ref, route_ref, gate_ref, counts_ref, kcat, vcat, o_scr, s_scr, p_scr, sink_scr, m_scr,
                 *, tiles_per_seq, n_tiles):
    tq = x_ref.shape[0]
    blk = WINDOW
    i = pl.program_id(0)
    slot = i % 2

    @pl.when(i == 0)
    def _():
        o_scr[...] = jnp.zeros_like(o_scr)
        counts_ref[...] = jnp.zeros_like(counts_ref)

    pos_tile = jnp.minimum(i, n_tiles - 1) % tiles_per_seq
    has_prev = pos_tile > 0
    has_next = pos_tile < tiles_per_seq - 1

    kcat[0:blk] = kp_ref[...]
    kcat[blk:blk + tq] = kc_ref[...]
    kcat[blk + tq:] = kn_ref[...]
    vcat[0:blk] = vp_ref[...]
    vcat[blk:blk + tq] = vc_ref[...]
    vcat[blk + tq:] = vn_ref[...]

    qr = lax.broadcasted_iota(I32, (blk, blk), 0)
    kc = lax.broadcasted_iota(I32, (blk, blk), 1)
    scale = HEAD_DIM ** -0.5
    n_sub = tq // blk
    pairs =[(j, g) for j in range(n_sub) for g in range(N_KV_HEADS)]

    def keys(ref, j, g):
        return ref[j * blk:(j + 3) * blk, g * HEAD_DIM:(g + 1) * HEAD_DIM]

    def head_cols(g, r):
        hd = g * REP + r
        return slice(hd * HEAD_DIM, (hd + 1) * HEAD_DIM)

    for b, (j, g) in enumerate(pairs):
        rows = slice(j * blk, (j + 1) * blk)
        qs = jnp.concatenate([q_ref[rows, head_cols(g, r)] for r in range(REP)], axis=0)
        s_scr[b] = lax.dot_general(qs, keys(kcat, j, g), (((1,), (1,)), ((), ())),
                                   preferred_element_type=F32)
    log2e = 1.4426950408889634

    def softmax(b, r):
        j, g = pairs[b]
        hrows = slice(r * blk, (r + 1) * blk)
        z = s_scr[b, hrows, :] * (scale * log2e)
        lo_ok = kc >= (qr + jnp.where(has_prev, 0, blk) if j == 0 else qr)
        hi_ok = kc <= (qr - jnp.where(has_next, 0, blk) if j == n_sub - 1 else qr)
        z = jnp.concatenate([jnp.where(lo_ok, z[:, :blk], NEG), z[:, blk:2 * blk],
                             jnp.where(hi_ok, z[:, 2 * blk:], NEG)], axis=1)
        sink = sink_ref[g * REP + r] * log2e
        m = jnp.maximum(jnp.max(z, axis=-1, keepdims=True), sink)
        p_scr[b, hrows, :] = jnp.exp2(z - m).astype(BF16)
        sink_scr[b, hrows, :] = jnp.broadcast_to(jnp.exp2(sink - m), (blk, LANES))

    def values(b):
        j, g = pairs[b]
        rows = slice(j * blk, (j + 1) * blk)
        v_ext = jnp.concatenate([keys(vcat, j, g), jnp.ones((3 * blk, HEAD_DIM), BF16)], axis=1)
        acc = jnp.dot(p_scr[b], v_ext, preferred_element_type=F32)
        o = (acc[:, :HEAD_DIM] / (acc[:, HEAD_DIM:] + sink_scr[b])).astype(BF16)
        for r in range(REP):
            o_scr[slot, rows, head_cols(g, r)] = o[r * blk:(r + 1) * blk, :]

    def merged_cols(cols):
        yb = jnp.dot(o_scr[1 - slot], wpb_ref[:, cols], preferred_element_type=F32)
        m_scr[:, cols] = (ma_ref[:, cols] + sgb_ref[:, cols] * yb).astype(BF16)

    def x1_cols(cols):
        x1_ref[:, cols] = x_ref[:, cols] + jnp.dot(m_scr[...], wout_ref[:, cols], preferred_element_type=F32)

    col_chunks = [slice(c * DENSE_COLS, (c + 1) * DENSE_COLS) for c in range(D_MODEL // DENSE_COLS)]
    dense = [functools.partial(f, cols) for f in (merged_cols, x1_cols) for cols in col_chunks]
    units = [(b, r) for b in range(len(pairs)) for r in range(REP)]
    units_per_dense = len(units) // len(dense)
    for n, (b, r) in enumerate(units):
        softmax(b, r)
        if (n + 1) % units_per_dense == 0:
            dense[(n + 1) // units_per_dense - 1]()
        if r == REP - 1:
            values(b)

    hn = _rms(x1_ref[...], gffn_ref[...])
    _store_rows3(hn_ref, hn)
    def wide(a):
        return jnp.concatenate([a] * (tq // LANES), axis=1)

    logits = lax.dot_general(wr_ref[...], hn.astype(BF16), (((1,), (1,)), ((), ())),
                             preferred_element_type=F32) + wide(br_ref[...])
    sub = lax.broadcasted_iota(I32, (SUBLANES, tq), 0).astype(F32)
    ninf = -jnp.inf

    def cmax(a):
        return jnp.max(a, axis=0, keepdims=True)

    def csum(a):
        return jnp.sum(a, axis=0, keepdims=True)

    def first_row(mask):
        return jnp.min(jnp.where(mask, sub, float(SUBLANES)), axis=0, keepdims=True)

    def group_rows(g):
        return logits[(g + 1) * SUBLANES:(g + 2) * SUBLANES]

    gl = jnp.where(sub < N_GROUPS, logits[0:SUBLANES], ninf)
    gmax = cmax(gl)
    g_sel = first_row(gl == gmax)
    g_p = 1.0 / csum(jnp.exp(gl - gmax))
    el = group_rows(0)
    for g in range(1, N_GROUPS):
        el = jnp.where(g_sel == g, group_rows(g), el)
    ee = jnp.exp(el - cmax(el))
    eprob = ee / csum(ee)
    p1 = cmax(eprob)
    i1 = first_row(eprob == p1)
    eprob2 = jnp.where(sub == i1, -1.0, eprob)
    p2 = cmax(eprob2)
    i2 = first_row(eprob2 == p2)
    psum = p1 + p2
    w1 = g_p * p1 / psum
    w2 = g_p * p2 / psum
    e1 = g_sel * EXPERTS_PER_GROUP + i1
    e2 = g_sel * EXPERTS_PER_GROUP + i2

    erow = lax.broadcasted_iota(I32, (N_EXPERTS, tq), 0).astype(F32)
    oh1 = erow == e1
    oh2 = erow == e2
    cnt = (jnp.where(oh1, 1.0, 0.0) + jnp.where(oh2, 1.0, 0.0)) * jnp.where(i > 0, 1.0, 0.0)
    ri = lax.broadcasted_iota(I32, (tq, tq), 0)
    ci = lax.broadcasted_iota(I32, (tq, tq), 1)
    earlier = jnp.where(ri < ci, 1.0, 0.0).astype(BF16)
    base = wide(counts_ref[...]) + jnp.dot(cnt.astype(BF16), earlier, preferred_element_type=F32)
    r1 = csum(jnp.where(oh1, base, 0.0))
    r2 = csum(jnp.where(oh2, base, 0.0))
    counts_ref[...] = counts_ref[...] + jnp.sum(cnt, axis=1, keepdims=True)

    route = jnp.where(sub == 0.0, e1, jnp.where(sub == 1.0, e2,
                      jnp.where(sub == 2.0, r1, jnp.where(sub == 3.0, r2, 0.0))))
    route_ref[...] = route.astype(I32)
    gate_ref[...] = jnp.where(sub == 0.0, w1, jnp.where(sub == 1.0, w2, 0.0))


def _attn(x, ma, sgb, q, k, v, p, seq):
    t = x.shape[0]
    tq = TQ
    sub = tq // WINDOW
    last_blk = t // WINDOW - 1
    n_tiles = t // tq
    att = lambda i: jnp.minimum(i, n_tiles - 1)
    post = lambda i: jnp.maximum(i - 1, 0)
    att_row = lambda w: pl.BlockSpec((tq, w), lambda i: (att(i), 0))
    row = lambda w: pl.BlockSpec((tq, w), lambda i: (post(i), 0))
    prev = pl.BlockSpec((WINDOW, KV_W), lambda i: (jnp.maximum(att(i) * sub - 1, 0), 0))
    nxt = pl.BlockSpec((WINDOW, KV_W), lambda i: (jnp.minimum((att(i) + 1) * sub, last_blk), 0))
    return pl.pallas_call(
        functools.partial(_attn_kernel, tiles_per_seq=seq // tq, n_tiles=n_tiles),
        grid=(n_tiles + 1,),
        in_specs=[pl.BlockSpec(memory_space=pltpu.SMEM),
                  row(D_MODEL), row(D_MODEL), row(D_MODEL), att_row(Q_W),
                  prev, att_row(KV_W), nxt, prev, att_row(KV_W), nxt,
                  _const_spec((Q_W, D_MODEL)), _const_spec((D_MODEL, D_MODEL)), _const_spec((1, D_MODEL)),
                  _const_spec((ROUTER_ROWS, D_MODEL)), _const_spec((ROUTER_ROWS, LANES))],
        out_specs=[row(D_MODEL), _rows3_spec(tq, lambda i: (post(i), 0)),
                   pl.BlockSpec((SUBLANES, tq), lambda i: (0, post(i))),
                   pl.BlockSpec((SUBLANES, tq), lambda i: (0, post(i))),
                   pl.BlockSpec((N_EXPERTS, LANES), lambda i: (0, 0))],
        out_shape=[jax.ShapeDtypeStruct((t, D_MODEL), F32), jax.ShapeDtypeStruct((t * ROW_LINES, LANES), I32),
                   jax.ShapeDtypeStruct((SUBLANES, t), I32), jax.ShapeDtypeStruct((SUBLANES, t), F32),
                   jax.ShapeDtypeStruct((N_EXPERTS, LANES), F32)],
        scratch_shapes=[pltpu.VMEM((tq + 2 * WINDOW, KV_W), BF16), pltpu.VMEM((tq + 2 * WINDOW, KV_W), BF16),
                        pltpu.VMEM((2, tq, Q_W), BF16),
                        pltpu.VMEM((sub * N_KV_HEADS, REP * WINDOW, 3 * WINDOW), F32),
                        pltpu.VMEM((sub * N_KV_HEADS, REP * WINDOW, 3 * WINDOW), BF16),
                        pltpu.VMEM((sub * N_KV_HEADS, REP * WINDOW, LANES), F32),
                        pltpu.VMEM((tq, D_MODEL), BF16)],
        compiler_params=pltpu.CompilerParams(dimension_semantics=("arbitrary",),
                                             vmem_limit_bytes=VMEM_LIMIT),
        name="attn",
    )(p["sink"], x, ma, sgb, q, k, k, k, v, v, v, p["w_proj_b"], p["w_out"], p["norm_ffn_g"],
      p["w_router"], p["b_router"])


def _dest_kernel(route_ref, starts_ref, dest_ref):
    route = route_ref[...].astype(F32)
    td = route.shape[1]
    sub = lax.broadcasted_iota(I32, route.shape, 0)
    erow = lax.broadcasted_iota(I32, (N_EXPERTS, td), 0).astype(F32)
    starts = jnp.broadcast_to(starts_ref[:, 0:1], (N_EXPERTS, td))

    def slot(k):
        start = jnp.sum(jnp.where(erow == route[k:k + 1], starts, 0.0), axis=0, keepdims=True)
        return start + route[TOP_K + k:TOP_K + k + 1]

    dest_ref[...] = jnp.where(sub == 0, slot(0), jnp.where(sub == 1, slot(1), 0.0)).astype(I32)


def _dest(route, pad_starts):
    t = route.shape[1]
    td = min(TM_DEST, t)
    return pl.pallas_call(
        _dest_kernel,
        grid=(t // td,),
        in_specs=[pl.BlockSpec((SUBLANES, td), lambda i: (0, i)), _const_spec((N_EXPERTS, LANES))],
        out_specs=pl.BlockSpec((SUBLANES, td), lambda i: (0, i)),
        out_shape=jax.ShapeDtypeStruct((SUBLANES, t), I32),
        compiler_params=pltpu.CompilerParams(dimension_semantics=("parallel",)),
        name="dest",
    )(route, pad_starts)


def _sc_mesh():
    return plsc.VectorSubcoreMesh(core_axis_name="c", subcore_axis_name="s")


def _sc_worker():
    return lax.axis_index("s") * SC_CORES + lax.axis_index("c")


def _sc_chunk(t):
    return min(SC_CHUNK_MAX, t // (SC_WORKERS * SUBLANES))


def _dispatch(hn, dests, n_rows):
    t = hn.shape[0]
    chunk = dests[0].shape[1]
    per_worker = t // SC_WORKERS
    n_chunks = per_worker // chunk
    idx = pltpu.VMEM((n_chunks, chunk), I32)

    @functools.partial(
        pl.kernel, mesh=_sc_mesh(), out_type=jax.ShapeDtypeStruct((n_rows,) + ROW3, I32),
        scratch_types=[idx, idx, pltpu.VMEM((chunk,) + ROW3, I32), pltpu.SemaphoreType.DMA])
    def scatter_rows(hn_hbm, d0_hbm, d1_hbm, xs_hbm, i0_v, i1_v, rows_v, sem):
        w = _sc_worker()
        pltpu.sync_copy(d0_hbm.at[pl.ds(w * n_chunks, n_chunks)], i0_v)
        pltpu.sync_copy(d1_hbm.at[pl.ds(w * n_chunks, n_chunks)], i1_v)

        @pl.loop(0, n_chunks)
        def _(j):
            pltpu.sync_copy(hn_hbm.at[pl.ds(w * per_worker + j * chunk, chunk)], rows_v)
            copies = [pltpu.make_async_copy(rows_v, xs_hbm.at[i_v.at[j]], sem) for i_v in (i0_v, i1_v)]
            for cp in copies:
                cp.start()
            for cp in copies:
                cp.wait()

    return scatter_rows(hn, *dests)


def _moe_block_rows(t):
    rows = MOE_ROWS_MAX
    while rows > CHUNK and (t * TOP_K) // N_EXPERTS < MOE_MIN_BLOCKS * rows:
        rows //= 2
    return rows


def _moe_kernel(be_ref, nvalid_ref, nused_ref, xs_ref, wg_ref, wu_ref, wd_ref, yb_ref, *w16_refs):
    i = pl.program_id(0)
    used = i < nused_ref[0]
    weights = w16_refs if w16_refs else (wg_ref, wu_ref, wd_ref)

    @pl.when(jnp.logical_not(used))
    def _():
        yb_ref[...] = jnp.zeros_like(yb_ref)

    @pl.when(used)
    def _():
        if w16_refs:
            @pl.when(jnp.logical_or(i == 0, be_ref[i] != be_ref[jnp.maximum(i - 1, 0)]))
            def _():
                for src, dst in zip((wg_ref, wu_ref, wd_ref), w16_refs):
                    dst[0] = src[0].astype(BF16)

        row = lax.broadcasted_iota(I32, (xs_ref.shape[0] // ROW_LINES, 1), 0)
        x = jnp.where(row < nvalid_ref[i], _load_rows3(xs_ref), 0.0).astype(BF16)
        gate = jnp.dot(x, weights[0][0], preferred_element_type=F32)
        up = jnp.dot(x, weights[1][0], preferred_element_type=F32)
        hid = (jax.nn.silu(gate) * up).astype(BF16)
        _store_rows3(yb_ref, jnp.dot(hid, weights[2][0], preferred_element_type=F32))


def _moe(block_e, n_valid, n_used, xs, experts, block_rows):
    n_blocks = xs.shape[0] // (block_rows * ROW_LINES)
    cast = experts[0].dtype != BF16

    def rows(i, be, nv, nu):
        return (jnp.minimum(i, nu[0] - 1), 0)

    def expert(i, be, nv, nu):
        return (be[jnp.minimum(i, nu[0] - 1)], 0, 0)

    w_specs = [pl.BlockSpec((1,) + w.shape[1:], expert) for w in experts]
    out_specs = [_rows3_spec(block_rows, lambda i, be, nv, nu: (i, 0))]
    out_shape = [jax.ShapeDtypeStruct(xs.shape, I32)]
    if cast:
        out_specs += w_specs
        out_shape += [jax.ShapeDtypeStruct(w.shape, BF16) for w in experts]
    yb, *w16 = pl.pallas_call(
        _moe_kernel,
        grid_spec=pltpu.PrefetchScalarGridSpec(
            num_scalar_prefetch=3,
            grid=(n_blocks,),
            in_specs=[_rows3_spec(block_rows, rows)] + w_specs,
            out_specs=out_specs,
        ),
        out_shape=out_shape,
        compiler_params=pltpu.CompilerParams(dimension_semantics=("arbitrary",),
                                             vmem_limit_bytes=VMEM_LIMIT),
        name="moe",
    )(block_e, n_valid, n_used, xs, *experts)
    return yb, (tuple(w16) if cast else experts)


def _gather(yb, dests):
    chunk = dests[0].shape[1]
    t = dests[0].shape[0] * chunk
    per_worker = t // SC_WORKERS
    n_chunks = per_worker // chunk
    idx = pltpu.VMEM((n_chunks, chunk), I32)
    out = jax.ShapeDtypeStruct((t,) + ROW3, I32)

    @functools.partial(
        pl.kernel, mesh=_sc_mesh(), out_type=(out, out),
        scratch_types=[idx, idx, pltpu.VMEM((chunk,) + ROW3, I32), pltpu.SemaphoreType.DMA])
    def gather_rows(yb_hbm, d0_hbm, d1_hbm, y0_hbm, y1_hbm, i0_v, i1_v, rows_v, sem):
        w = _sc_worker()
        pltpu.sync_copy(d0_hbm.at[pl.ds(w * n_chunks, n_chunks)], i0_v)
        pltpu.sync_copy(d1_hbm.at[pl.ds(w * n_chunks, n_chunks)], i1_v)

        @pl.loop(0, n_chunks)
        def _(j):
            rows = pl.ds(w * per_worker + j * chunk, chunk)
            for i_v, y_hbm in ((i0_v, y0_hbm), (i1_v, y1_hbm)):
                pltpu.async_copy(yb_hbm.at[i_v.at[j]], rows_v, sem).wait()
                pltpu.sync_copy(rows_v, y_hbm.at[rows])

    return gather_rows(yb, *dests)


def _combine_kernel(x1_ref, gate_ref, y0_ref, y1_ref, out_ref):
    tm = x1_ref.shape[0]
    gate = jnp.concatenate([gate_ref[...], jnp.zeros((LANES - SUBLANES, tm), F32)], axis=0).T
    out_ref[...] = x1_ref[...] + (_load_rows3(y0_ref) * gate[:, 0:1] + _load_rows3(y1_ref) * gate[:, 1:2])


def _combine(x1, gate, y0, y1):
    t = x1.shape[0]
    tm = TM_ROW
    return pl.pallas_call(
        _combine_kernel,
        grid=(t // tm,),
        in_specs=[pl.BlockSpec((tm, D_MODEL), lambda i: (i, 0)),
                  pl.BlockSpec((SUBLANES, tm), lambda i: (0, i)),
                  _rows3_spec(tm, lambda i: (i, 0)), _rows3_spec(tm, lambda i: (i, 0))],
        out_specs=pl.BlockSpec((tm, D_MODEL), lambda i: (i, 0)),
        out_shape=jax.ShapeDtypeStruct((t, D_MODEL), F32),
        compiler_params=pltpu.CompilerParams(dimension_semantics=("parallel",)),
        name="combine",
    )(x1, gate, y0, y1)


def _rope_tables(seq):
    half = HEAD_DIM // 2
    inv_freq = ROPE_THETA ** (-jnp.arange(half, dtype=F32) / half)
    coarse = (jnp.arange(seq // ROPE_SPLIT) * ROPE_SPLIT).astype(F32)[:, None, None] * inv_freq
    fine = jnp.arange(ROPE_SPLIT).astype(F32)[None, :, None] * inv_freq
    cos = (jnp.cos(coarse) * jnp.cos(fine) - jnp.sin(coarse) * jnp.sin(fine)).reshape(seq, half)
    sin = (jnp.sin(coarse) * jnp.cos(fine) + jnp.cos(coarse) * jnp.sin(fine)).reshape(seq, half)
    return jnp.concatenate([cos, cos], axis=-1), jnp.concatenate([-sin, sin], axis=-1)


def _layer(x, p, seq, rope, experts):
    t = x.shape[0]
    ma, sgb, q, k, v = _inproj(x, p, *rope, seq)
    x1, hn, route, gate, counts_f = _attn(x, ma, sgb, q, k, v, p, seq)

    block_rows = _moe_block_rows(t)
    counts = counts_f[:, 0].astype(I32)
    min_blocks = 1 if experts[0].dtype != BF16 else 0
    padded = jnp.maximum((counts + block_rows - 1) // block_rows, min_blocks) * block_rows
    pad_ends = jnp.cumsum(padded)
    pad_starts = pad_ends - padded
    n_blocks = (t * TOP_K) // block_rows + N_EXPERTS
    block_start = jnp.arange(n_blocks, dtype=I32) * block_rows
    in_expert = jnp.logical_and(block_start[:, None] >= pad_starts[None, :],
                                block_start[:, None] < pad_ends[None, :]).astype(I32)
    block_e = jnp.minimum(jnp.sum((block_start[:, None] >= pad_ends[None, :]).astype(I32), axis=1), N_EXPERTS - 1)
    n_valid = jnp.sum(in_expert * jnp.clip(pad_starts + counts - block_start[:, None], 0, block_rows), axis=1)
    n_used = pad_ends[-1:] // block_rows
    starts_col = jnp.broadcast_to(pad_starts.astype(F32)[:, None], (N_EXPERTS, LANES))

    dest = _dest(route, starts_col)
    dests = [dest[k].reshape(t // _sc_chunk(t), _sc_chunk(t)) for k in range(TOP_K)]
    xs = _dispatch(_as_rows3(hn), dests, n_blocks * block_rows)
    yb, experts = _moe(block_e, n_valid, n_used, _as_lines(xs), experts, block_rows)
    y0, y1 = _gather(_as_rows3(yb), dests)
    return _combine(x1, gate, _as_lines(y0), _as_lines(y1)), experts


def kernel(x_prompt, x_sample, norm_mix_g, w_in, norm_v_g, w_spatial, b_spatial, q_norm_g, k_norm_g, sink,
           w_proj_a, w_proj_b, w_out, norm_ffn_g, w_router_group, b_router_group, w_router_expert,
           b_router_expert, w_gate_e, w_up_e, w_down_e):
    depth = w_in.shape[0]
    layers = []
    for l in range(depth):
        w_router = jnp.zeros((ROUTER_ROWS, D_MODEL), F32)
        w_router = w_router.at[:N_GROUPS].set(w_router_group[l].T)
        w_router = w_router.at[SUBLANES:SUBLANES + N_EXPERTS].set(w_router_expert[l].T)
        b_router = jnp.zeros((ROUTER_ROWS,), F32)
        b_router = b_router.at[:N_GROUPS].set(b_router_group[l])
        b_router = b_router.at[SUBLANES:SUBLANES + N_EXPERTS].set(b_router_expert[l])
        b_router = jnp.broadcast_to(b_router[:, None], (ROUTER_ROWS, LANES))
        layers.append(dict(
            norm_mix_g=norm_mix_g[l][None], w_in=w_in[l].astype(BF16), norm_v_g=norm_v_g[l][None],
            w_spatial=w_spatial[l].astype(BF16),
            b_spatial=jnp.broadcast_to(b_spatial[l][:, :, None], (A_GROUPS, CHUNK, LANES)),
            q_norm_g=q_norm_g[l][None], k_norm_g=k_norm_g[l][None], sink=sink[l],
            w_proj_a=w_proj_a[l].astype(BF16), w_proj_b=w_proj_b[l].astype(BF16), w_out=w_out[l].astype(BF16),
            norm_ffn_g=norm_ffn_g[l][None], w_router=w_router.astype(BF16), b_router=b_router,
            experts=(w_gate_e[l], w_up_e[l], w_down_e[l])))

    trunks = [x_prompt, x_sample]
    rows = [x.reshape(-1, D_MODEL) for x in trunks]
    order = sorted(range(len(trunks)), key=lambda n: -rows[n].shape[0])
    rope = _rope_tables(max(x.shape[1] for x in trunks))
    for p in layers:
        experts = p["experts"]
        for n in order:
            rows[n], experts = _layer(rows[n], p, trunks[n].shape[1], rope, experts)
    return tuple(r.reshape(x.shape) for r, x in zip(rows, trunks))
```

```python
import functools

import jax
import jax.numpy as jnp
from jax import lax
from jax.experimental import pallas as pl
from jax.experimental.pallas import tpu as pltpu
from jax.experimental.pallas import tpu_sc as plsc

F32 = jnp.float32
BF16 = jnp.bfloat16
I32 = jnp.int32
U32 = jnp.uint32

LANES = 128
SUBLANES = 8
VMEM_BYTES_V7X = 64 * 1024 * 1024
SC_CORES = 2
SC_SUBCORES = 16
SC_WORKERS = SC_CORES * SC_SUBCORES
SC_CHUNK_MAX = 128

D_MODEL = 1024
A_WIDTH = D_MODEL
A_GROUPS = 8
CHUNK = 128
HEAD_DIM = 128
N_Q_HEADS = D_MODEL // HEAD_DIM
N_KV_HEADS = 2
REP = N_Q_HEADS // N_KV_HEADS
WINDOW = 128
ROPE_THETA = 10000.0
ROPE_SPLIT = 64
Q_W = N_Q_HEADS * HEAD_DIM
KV_W = N_KV_HEADS * HEAD_DIM
IN_W = 2 * A_WIDTH + Q_W + 2 * KV_W + 2 * D_MODEL
COL_U = 0
COL_V = COL_U + A_WIDTH
COL_Q = COL_V + A_WIDTH
COL_K = COL_Q + Q_W
COL_VA = COL_K + KV_W
COL_GA = COL_VA + KV_W
COL_GB = COL_GA + D_MODEL
N_GROUPS = 4
EXPERTS_PER_GROUP = 8
N_EXPERTS = N_GROUPS * EXPERTS_PER_GROUP
TOP_K = 2
D_EXPERT = 512
EPS = 1e-6
NEG = -1e30

TM_IN = 512
TM_IN_SUB = 256
TQ = 512
DENSE_COLS = 256
ROUTER_ROWS = 64
assert EXPERTS_PER_GROUP == SUBLANES and SUBLANES + N_EXPERTS <= ROUTER_ROWS
TM_ROW = 1024
TM_DEST = 8192
MOE_ROWS_MAX = 1024
MOE_MIN_BLOCKS = 1
VMEM_RESERVE = 8 * 1024 * 1024
VMEM_LIMIT = VMEM_BYTES_V7X - VMEM_RESERVE


def _rms(x, g):
    return x * lax.rsqrt(jnp.mean(x * x, axis=-1, keepdims=True) + EPS) * g


ROW_LINES = D_MODEL // 2 // LANES
ROW3 = (ROW_LINES, LANES)
HIGH_HALF = 0xFFFF0000


def _as_rows3(a):
    return a.reshape((a.shape[0] // ROW_LINES,) + ROW3)


def _as_lines(a):
    return a.reshape((a.shape[0] * ROW_LINES, LANES))


def _store_rows3(lines_ref, val):
    rows = val.shape[0]
    bits = lax.bitcast_convert_type(val.astype(BF16).astype(F32), U32)
    half = ROW_LINES * LANES
    for s in range(ROW_LINES):
        lo = bits[:, s * LANES:(s + 1) * LANES] >> 16
        hi = bits[:, half + s * LANES:half + (s + 1) * LANES] & U32(HIGH_HALF)
        lines_ref[pl.ds(s, rows, stride=ROW_LINES), :] = lax.bitcast_convert_type(lo | hi, I32)


def _load_rows3(lines_ref):
    rows = lines_ref.shape[0] // ROW_LINES
    words = [lax.bitcast_convert_type(lines_ref[pl.ds(s, rows, stride=ROW_LINES), :], U32)
             for s in range(ROW_LINES)]
    lo = [lax.bitcast_convert_type(w << 16, F32) for w in words]
    hi = [lax.bitcast_convert_type(w & U32(HIGH_HALF), F32) for w in words]
    return jnp.concatenate(lo + hi, axis=1)


def _rows3_spec(rows, index_map):
    return pl.BlockSpec((rows * ROW_LINES, LANES), index_map)


def _const_spec(shape):
    nd = len(shape)
    return pl.BlockSpec(shape, lambda *_: (0,) * nd, pipeline_mode=pl.Buffered(1))


def _inproj_kernel(x_ref, gmix_ref, win_ref, gv_ref, ws_ref, bs_ref, gq_ref, gk_ref, cos_ref, sin_ref,
                   wpa_ref, ma_ref, sgb_ref, q_ref, k_ref, v_ref, h_scr, u_scr, vn_scr, a_scr):
    tm = x_ref.shape[0]

    def stages(rows):
        def proj(lo, width):
            return jnp.dot(h_scr[rows], win_ref[:, lo:lo + width], preferred_element_type=F32)

        def norm_rope(z, g):
            zn = _rms(z, g)
            return zn * cos_ref[rows] + pltpu.roll(zn, HEAD_DIM // 2, 1) * sin_ref[rows]

        def norm():
            h_scr[rows] = _rms(x_ref[rows], gmix_ref[...]).astype(BF16)

        def mix_v():
            vn_scr[rows] = _rms(jax.nn.gelu(proj(COL_V, A_WIDTH)), gv_ref[...]).astype(BF16)

        def mix_u():
            u_scr[rows] = jax.nn.gelu(proj(COL_U, A_WIDTH))

        def spatial():
            for c in range(rows.start, rows.stop, CHUNK):
                chunk = slice(c, c + CHUNK)
                for g in range(A_GROUPS):
                    cols = slice(g * LANES, (g + 1) * LANES)
                    mixed = jnp.dot(ws_ref[g], vn_scr[chunk, cols], preferred_element_type=F32) + bs_ref[g]
                    a_scr[chunk, cols] = (u_scr[chunk, cols] * mixed).astype(BF16)

        def gate_a():
            ya = jnp.dot(a_scr[rows], wpa_ref[...], preferred_element_type=F32)
            ma_ref[rows] = jax.nn.sigmoid(proj(COL_GA, D_MODEL)) * ya

        def gate_b():
            sgb_ref[rows] = jax.nn.sigmoid(proj(COL_GB, D_MODEL))

        def queries():
            qz = proj(COL_Q, Q_W)
            for hd in range(N_Q_HEADS):
                cols = slice(hd * HEAD_DIM, (hd + 1) * HEAD_DIM)
                q_ref[rows, cols] = norm_rope(qz[:, cols], gq_ref[...]).astype(BF16)

        def keys_values():
            kz = proj(COL_K, KV_W)
            for hd in range(N_KV_HEADS):
                cols = slice(hd * HEAD_DIM, (hd + 1) * HEAD_DIM)
                k_ref[rows, cols] = norm_rope(kz[:, cols], gk_ref[...]).astype(BF16)
            v_ref[rows] = proj(COL_VA, KV_W).astype(BF16)

        return [norm, mix_v, mix_u, spatial, gate_a, gate_b, queries, keys_values]

    subs = [stages(slice(r, r + TM_IN_SUB)) for r in range(0, tm, TM_IN_SUB)]
    n_stage = len(subs[0])
    for step in range(n_stage + len(subs) - 1):
        for n, sub in enumerate(subs):
            if 0 <= step - n < n_stage:
                sub[step - n]()


def _inproj(x, p, cos, sin, seq):
    t = x.shape[0]
    tm = TM_IN
    n_pos = seq // tm
    row = lambda w: pl.BlockSpec((tm, w), lambda i: (i, 0))
    pos = pl.BlockSpec((tm, HEAD_DIM), lambda i: (i % n_pos, 0))
    return pl.pallas_call(
        _inproj_kernel,
        grid=(t // tm,),
        in_specs=[row(D_MODEL), _const_spec((1, D_MODEL)), _const_spec((D_MODEL, IN_W)),
                  _const_spec((1, A_WIDTH)), _const_spec((A_GROUPS, CHUNK, CHUNK)),
                  _const_spec((A_GROUPS, CHUNK, LANES)), _const_spec((1, HEAD_DIM)),
                  _const_spec((1, HEAD_DIM)), pos, pos, _const_spec((A_WIDTH, D_MODEL))],
        out_specs=[row(D_MODEL), row(D_MODEL), row(Q_W), row(KV_W), row(KV_W)],
        out_shape=[jax.ShapeDtypeStruct((t, D_MODEL), F32), jax.ShapeDtypeStruct((t, D_MODEL), F32),
                   jax.ShapeDtypeStruct((t, Q_W), BF16), jax.ShapeDtypeStruct((t, KV_W), BF16),
                   jax.ShapeDtypeStruct((t, KV_W), BF16)],
        scratch_shapes=[pltpu.VMEM((tm, D_MODEL), BF16), pltpu.VMEM((tm, A_WIDTH), F32),
                        pltpu.VMEM((tm, A_WIDTH), BF16), pltpu.VMEM((tm, A_WIDTH), BF16)],
        compiler_params=pltpu.CompilerParams(dimension_semantics=("parallel",),
                                             vmem_limit_bytes=VMEM_LIMIT),
        name="inproj",
    )(x, p["norm_mix_g"], p["w_in"], p["norm_v_g"], p["w_spatial"], p["b_spatial"], p["q_norm_g"],
      p["k_norm_g"], cos, sin, p["w_proj_a"])


def _attn_kernel(sink_ref, x_ref, ma_ref, sgb_ref, q_ref, kp_ref, kc_ref, kn_ref, vp_ref, vc_ref, vn_ref,
                 wpb_ref, wout_ref, gffn_ref, wr_ref, br_ref,
                 x1_ref, hn_ref, route_ref, gate_ref, counts_ref, kcat, vcat, o_scr, s_scr, p_scr, sink_scr, m_scr,
                 *, tiles_per_seq, n_tiles):
    tq = x_ref.shape[0]
    blk = WINDOW
    i = pl.program_id(0)
    slot = i % 2

    @pl.when(i == 0)
    def _():
        o_scr[...] = jnp.zeros_like(o_scr)
        counts_ref[...] = jnp.zeros_like(counts_ref)

    pos_tile = jnp.minimum(i, n_tiles - 1) % tiles_per_seq
    has_prev = pos_tile > 0
    has_next = pos_tile < tiles_per_seq - 1

    kcat[0:blk] = kp_ref[...]
    kcat[blk:blk + tq] = kc_ref[...]
    kcat[blk + tq:] = kn_ref[...]
    vcat[0:blk] = vp_ref[...]
    vcat[blk:blk + tq] = vc_ref[...]
    vcat[blk + tq:] = vn_ref[...]

    qr = lax.broadcasted_iota(I32, (blk, blk), 0)
    kc = lax.broadcasted_iota(I32, (blk, blk), 1)
    scale = HEAD_DIM ** -0.5
    n_sub = tq // blk
    pairs =[(j, g) for j in range(n_sub) for g in range(N_KV_HEADS)]

    def keys(ref, j, g):
        return ref[j * blk:(j + 3) * blk, g * HEAD_DIM:(g + 1) * HEAD_DIM]

    def head_cols(g, r):
        hd = g * REP + r
        return slice(hd * HEAD_DIM, (hd + 1) * HEAD_DIM)

    for b, (j, g) in enumerate(pairs):
        rows = slice(j * blk, (j + 1) * blk)
        qs = jnp.concatenate([q_ref[rows, head_cols(g, r)] for r in range(REP)], axis=0)
        s_scr[b] = lax.dot_general(qs, keys(kcat, j, g), (((1,), (1,)), ((), ())),
                                   preferred_element_type=F32)
    log2e = 1.4426950408889634

    def softmax(b, r):
        j, g = pairs[b]
        hrows = slice(r * blk, (r + 1) * blk)
        z = s_scr[b, hrows, :] * (scale * log2e)
        lo_ok = kc >= (qr + jnp.where(has_prev, 0, blk) if j == 0 else qr)
        hi_ok = kc <= (qr - jnp.where(has_next, 0, blk) if j == n_sub - 1 else qr)
        z = jnp.concatenate([jnp.where(lo_ok, z[:, :blk], NEG), z[:, blk:2 * blk],
                             jnp.where(hi_ok, z[:, 2 * blk:], NEG)], axis=1)
        sink = sink_ref[g * REP + r] * log2e
        m = jnp.maximum(jnp.max(z, axis=-1, keepdims=True), sink)
        p_scr[b, hrows, :] = jnp.exp2(z - m).astype(BF16)
        sink_scr[b, hrows, :] = jnp.broadcast_to(jnp.exp2(sink - m), (blk, LANES))

    def values(b):
        j, g = pairs[b]
        rows = slice(j * blk, (j + 1) * blk)
        v_ext = jnp.concatenate([keys(vcat, j, g), jnp.ones((3 * blk, HEAD_DIM), BF16)], axis=1)
        acc = jnp.dot(p_scr[b], v_ext, preferred_element_type=F32)
        o = (acc[:, :HEAD_DIM] / (acc[:, HEAD_DIM:] + sink_scr[b])).astype(BF16)
        for r in range(REP):
            o_scr[slot, rows, head_cols(g, r)] = o[r * blk:(r + 1) * blk, :]

    def merged_cols(cols):
        yb = jnp.dot(o_scr[1 - slot], wpb_ref[:, cols], preferred_element_type=F32)
        m_scr[:, cols] = (ma_ref[:, cols] + sgb_ref[:, cols] * yb).astype(BF16)

    def x1_cols(cols):
        x1_ref[:, cols] = x_ref[:, cols] + jnp.dot(m_scr[...], wout_ref[:, cols], preferred_element_type=F32)

    col_chunks = [slice(c * DENSE_COLS, (c + 1) * DENSE_COLS) for c in range(D_MODEL // DENSE_COLS)]
    dense = [functools.partial(f, cols) for f in (merged_cols, x1_cols) for cols in col_chunks]
    units = [(b, r) for b in range(len(pairs)) for r in range(REP)]
    units_per_dense = len(units) // len(dense)
    for n, (b, r) in enumerate(units):
        softmax(b, r)
        if (n + 1) % units_per_dense == 0:
            dense[(n + 1) // units_per_dense - 1]()
        if r == REP - 1:
            values(b)

    hn = _rms(x1_ref[...], gffn_ref[...])
    _store_rows3(hn_ref, hn)
    def wide(a):
        return jnp.concatenate([a] * (tq // LANES), axis=1)

    logits = lax.dot_general(wr_ref[...], hn.astype(BF16), (((1,), (1,)), ((), ())),
                             preferred_element_type=F32) + wide(br_ref[...])
    sub = lax.broadcasted_iota(I32, (SUBLANES, tq), 0).astype(F32)
    ninf = -jnp.inf

    def cmax(a):
        return jnp.max(a, axis=0, keepdims=True)

    def csum(a):
        return jnp.sum(a, axis=0, keepdims=True)

    def first_row(mask):
        return jnp.min(jnp.where(mask, sub, float(SUBLANES)), axis=0, keepdims=True)

    def group_rows(g):
        return logits[(g + 1) * SUBLANES:(g + 2) * SUBLANES]

    gl = jnp.where(sub < N_GROUPS, logits[0:SUBLANES], ninf)
    gmax = cmax(gl)
    g_sel = first_row(gl == gmax)
    g_p = 1.0 / csum(jnp.exp(gl - gmax))
    el = group_rows(0)
    for g in range(1, N_GROUPS):
        el = jnp.where(g_sel == g, group_rows(g), el)
    ee = jnp.exp(el - cmax(el))
    eprob = ee / csum(ee)
    p1 = cmax(eprob)
    i1 = first_row(eprob == p1)
    eprob2 = jnp.where(sub == i1, -1.0, eprob)
    p2 = cmax(eprob2)
    i2 = first_row(eprob2 == p2)
    psum = p1 + p2
    w1 = g_p * p1 / psum
    w2 = g_p * p2 / psum
    e1 = g_sel * EXPERTS_PER_GROUP + i1
    e2 = g_sel * EXPERTS_PER_GROUP + i2

    erow = lax.broadcasted_iota(I32, (N_EXPERTS, tq), 0).astype(F32)
    oh1 = erow == e1
    oh2 = erow == e2
    cnt = (jnp.where(oh1, 1.0, 0.0) + jnp.where(oh2, 1.0, 0.0)) * jnp.where(i > 0, 1.0, 0.0)
    ri = lax.broadcasted_iota(I32, (tq, tq), 0)
    ci = lax.broadcasted_iota(I32, (tq, tq), 1)
    earlier = jnp.where(ri < ci, 1.0, 0.0).astype(BF16)
    base = wide(counts_ref[...]) + jnp.dot(cnt.astype(BF16), earlier, preferred_element_type=F32)
    r1 = csum(jnp.where(oh1, base, 0.0))
    r2 = csum(jnp.where(oh2, base, 0.0))
    counts_ref[...] = counts_ref[...] + jnp.sum(cnt, axis=1, keepdims=True)

    route = jnp.where(sub == 0.0, e1, jnp.where(sub == 1.0, e2,
                      jnp.where(sub == 2.0, r1, jnp.where(sub == 3.0, r2, 0.0))))
    route_ref[...] = route.astype(I32)
    gate_ref[...] = jnp.where(sub == 0.0, w1, jnp.where(sub == 1.0, w2, 0.0))


def _attn(x, ma, sgb, q, k, v, p, seq):
    t = x.shape[0]
    tq = TQ
    sub = tq // WINDOW
    last_blk = t // WINDOW - 1
    n_tiles = t // tq
    att = lambda i: jnp.minimum(i, n_tiles - 1)
    post = lambda i: jnp.maximum(i - 1, 0)
    att_row = lambda w: pl.BlockSpec((tq, w), lambda i: (att(i), 0))
    row = lambda w: pl.BlockSpec((tq, w), lambda i: (post(i), 0))
    prev = pl.BlockSpec((WINDOW, KV_W), lambda i: (jnp.maximum(att(i) * sub - 1, 0), 0))
    nxt = pl.BlockSpec((WINDOW, KV_W), lambda i: (jnp.minimum((att(i) + 1) * sub, last_blk), 0))
    return pl.pallas_call(
        functools.partial(_attn_kernel, tiles_per_seq=seq // tq, n_tiles=n_tiles),
        grid=(n_tiles + 1,),
        in_specs=[pl.BlockSpec(memory_space=pltpu.SMEM),
                  row(D_MODEL), row(D_MODEL), row(D_MODEL), att_row(Q_W),
                  prev, att_row(KV_W), nxt, prev, att_row(KV_W), nxt,
                  _const_spec((Q_W, D_MODEL)), _const_spec((D_MODEL, D_MODEL)), _const_spec((1, D_MODEL)),
                  _const_spec((ROUTER_ROWS, D_MODEL)), _const_spec((ROUTER_ROWS, LANES))],
        out_specs=[row(D_MODEL), _rows3_spec(tq, lambda i: (post(i), 0)),
                   pl.BlockSpec((SUBLANES, tq), lambda i: (0, post(i))),
                   pl.BlockSpec((SUBLANES, tq), lambda i: (0, post(i))),
                   pl.BlockSpec((N_EXPERTS, LANES), lambda i: (0, 0))],
        out_shape=[jax.ShapeDtypeStruct((t, D_MODEL), F32), jax.ShapeDtypeStruct((t * ROW_LINES, LANES), I32),
                   jax.ShapeDtypeStruct((SUBLANES, t), I32), jax.ShapeDtypeStruct((SUBLANES, t), F32),
                   jax.ShapeDtypeStruct((N_EXPERTS, LANES), F32)],
        scratch_shapes=[pltpu.VMEM((tq + 2 * WINDOW, KV_W), BF16), pltpu.VMEM((tq + 2 * WINDOW, KV_W), BF16),
                        pltpu.VMEM((2, tq, Q_W), BF16),
                        pltpu.VMEM((sub * N_KV_HEADS, REP * WINDOW, 3 * WINDOW), F32),
                        pltpu.VMEM((sub * N_KV_HEADS, REP * WINDOW, 3 * WINDOW), BF16),
                        pltpu.VMEM((sub * N_KV_HEADS, REP * WINDOW, LANES), F32),
                        pltpu.VMEM((tq, D_MODEL), BF16)],
        compiler_params=pltpu.CompilerParams(dimension_semantics=("arbitrary",),
                                             vmem_limit_bytes=VMEM_LIMIT),
        name="attn",
    )(p["sink"], x, ma, sgb, q, k, k, k, v, v, v, p["w_proj_b"], p["w_out"], p["norm_ffn_g"],
      p["w_router"], p["b_router"])


def _dest_kernel(route_ref, starts_ref, dest_ref):
    route = route_ref[...].astype(F32)
    td = route.shape[1]
    sub = lax.broadcasted_iota(I32, route.shape, 0)
    erow = lax.broadcasted_iota(I32, (N_EXPERTS, td), 0).astype(F32)
    starts = jnp.broadcast_to(starts_ref[:, 0:1], (N_EXPERTS, td))

    def slot(k):
        start = jnp.sum(jnp.where(erow == route[k:k + 1], starts, 0.0), axis=0, keepdims=True)
        return start + route[TOP_K + k:TOP_K + k + 1]

    dest_ref[...] = jnp.where(sub == 0, slot(0), jnp.where(sub == 1, slot(1), 0.0)).astype(I32)


def _dest(route, pad_starts):
    t = route.shape[1]
    td = min(TM_DEST, t)
    return pl.pallas_call(
        _dest_kernel,
        grid=(t // td,),
        in_specs=[pl.BlockSpec((SUBLANES, td), lambda i: (0, i)), _const_spec((N_EXPERTS, LANES))],
        out_specs=pl.BlockSpec((SUBLANES, td), lambda i: (0, i)),
        out_shape=jax.ShapeDtypeStruct((SUBLANES, t), I32),
        compiler_params=pltpu.CompilerParams(dimension_semantics=("parallel",)),
        name="dest",
    )(route, pad_starts)


def _sc_mesh():
    return plsc.VectorSubcoreMesh(core_axis_name="c", subcore_axis_name="s")


def _sc_worker():
    return lax.axis_index("s") * SC_CORES + lax.axis_index("c")


def _sc_chunk(t):
    return min(SC_CHUNK_MAX, t // (SC_WORKERS * SUBLANES))


def _dispatch(hn, dests, n_rows):
    t = hn.shape[0]
    chunk = dests[0].shape[1]
    per_worker = t // SC_WORKERS
    n_chunks = per_worker // chunk
    idx = pltpu.VMEM((n_chunks, chunk), I32)

    @functools.partial(
        pl.kernel, mesh=_sc_mesh(), out_type=jax.ShapeDtypeStruct((n_rows,) + ROW3, I32),
        scratch_types=[idx, idx, pltpu.VMEM((chunk,) + ROW3, I32), pltpu.SemaphoreType.DMA])
    def scatter_rows(hn_hbm, d0_hbm, d1_hbm, xs_hbm, i0_v, i1_v, rows_v, sem):
        w = _sc_worker()
        pltpu.sync_copy(d0_hbm.at[pl.ds(w * n_chunks, n_chunks)], i0_v)
        pltpu.sync_copy(d1_hbm.at[pl.ds(w * n_chunks, n_chunks)], i1_v)

        @pl.loop(0, n_chunks)
        def _(j):
            pltpu.sync_copy(hn_hbm.at[pl.ds(w * per_worker + j * chunk, chunk)], rows_v)
            copies = [pltpu.make_async_copy(rows_v, xs_hbm.at[i_v.at[j]], sem) for i_v in (i0_v, i1_v)]
            for cp in copies:
                cp.start()
            for cp in copies:
                cp.wait()

    return scatter_rows(hn, *dests)


def _moe_block_rows(t):
    rows = MOE_ROWS_MAX
    while rows > CHUNK and (t * TOP_K) // N_EXPERTS < MOE_MIN_BLOCKS * rows:
        rows //= 2
    return rows


def _moe_kernel(be_ref, nvalid_ref, nused_ref, xs_ref, wg_ref, wu_ref, wd_ref, yb_ref, *w16_refs):
    i = pl.program_id(0)
    used = i < nused_ref[0]
    weights = w16_refs if w16_refs else (wg_ref, wu_ref, wd_ref)

    @pl.when(jnp.logical_not(used))
    def _():
        yb_ref[...] = jnp.zeros_like(yb_ref)

    @pl.when(used)
    def _():
        if w16_refs:
            @pl.when(jnp.logical_or(i == 0, be_ref[i] != be_ref[jnp.maximum(i - 1, 0)]))
            def _():
                for src, dst in zip((wg_ref, wu_ref, wd_ref), w16_refs):
                    dst[0] = src[0].astype(BF16)

        row = lax.broadcasted_iota(I32, (xs_ref.shape[0] // ROW_LINES, 1), 0)
        x = jnp.where(row < nvalid_ref[i], _load_rows3(xs_ref), 0.0).astype(BF16)
        gate = jnp.dot(x, weights[0][0], preferred_element_type=F32)
        up = jnp.dot(x, weights[1][0], preferred_element_type=F32)
        hid = (jax.nn.silu(gate) * up).astype(BF16)
        _store_rows3(yb_ref, jnp.dot(hid, weights[2][0], preferred_element_type=F32))


def _moe(block_e, n_valid, n_used, xs, experts, block_rows):
    n_blocks = xs.shape[0] // (block_rows * ROW_LINES)
    cast = experts[0].dtype != BF16

    def rows(i, be, nv, nu):
        return (jnp.minimum(i, nu[0] - 1), 0)

    def expert(i, be, nv, nu):
        return (be[jnp.minimum(i, nu[0] - 1)], 0, 0)

    w_specs = [pl.BlockSpec((1,) + w.shape[1:], expert) for w in experts]
    out_specs = [_rows3_spec(block_rows, lambda i, be, nv, nu: (i, 0))]
    out_shape = [jax.ShapeDtypeStruct(xs.shape, I32)]
    if cast:
        out_specs += w_specs
        out_shape += [jax.ShapeDtypeStruct(w.shape, BF16) for w in experts]
    yb, *w16 = pl.pallas_call(
        _moe_kernel,
        grid_spec=pltpu.PrefetchScalarGridSpec(
            num_scalar_prefetch=3,
            grid=(n_blocks,),
            in_specs=[_rows3_spec(block_rows, rows)] + w_specs,
            out_specs=out_specs,
        ),
        out_shape=out_shape,
        compiler_params=pltpu.CompilerParams(dimension_semantics=("arbitrary",),
                                             vmem_limit_bytes=VMEM_LIMIT),
        name="moe",
    )(block_e, n_valid, n_used, xs, *experts)
    return yb, (tuple(w16) if cast else experts)


def _gather(yb, dests):
    chunk = dests[0].shape[1]
    t = dests[0].shape[0] * chunk
    per_worker = t // SC_WORKERS
    n_chunks = per_worker // chunk
    idx = pltpu.VMEM((n_chunks, chunk), I32)
    out = jax.ShapeDtypeStruct((t,) + ROW3, I32)

    @functools.partial(
        pl.kernel, mesh=_sc_mesh(), out_type=(out, out),
        scratch_types=[idx, idx, pltpu.VMEM((chunk,) + ROW3, I32), pltpu.SemaphoreType.DMA])
    def gather_rows(yb_hbm, d0_hbm, d1_hbm, y0_hbm, y1_hbm, i0_v, i1_v, rows_v, sem):
        w = _sc_worker()
        pltpu.sync_copy(d0_hbm.at[pl.ds(w * n_chunks, n_chunks)], i0_v)
        pltpu.sync_copy(d1_hbm.at[pl.ds(w * n_chunks, n_chunks)], i1_v)

        @pl.loop(0, n_chunks)
        def _(j):
            rows = pl.ds(w * per_worker + j * chunk, chunk)
            for i_v, y_hbm in ((i0_v, y0_hbm), (i1_v, y1_hbm)):
                pltpu.async_copy(yb_hbm.at[i_v.at[j]], rows_v, sem).wait()
                pltpu.sync_copy(rows_v, y_hbm.at[rows])

    return gather_rows(yb, *dests)


def _combine_kernel(x1_ref, gate_ref, y0_ref, y1_ref, out_ref):
    tm = x1_ref.shape[0]
    gate = jnp.concatenate([gate_ref[...], jnp.zeros((LANES - SUBLANES, tm), F32)], axis=0).T
    out_ref[...] = x1_ref[...] + (_load_rows3(y0_ref) * gate[:, 0:1] + _load_rows3(y1_ref) * gate[:, 1:2])


def _combine(x1, gate, y0, y1):
    t = x1.shape[0]
    tm = TM_ROW
    return pl.pallas_call(
        _combine_kernel,
        grid=(t // tm,),
        in_specs=[pl.BlockSpec((tm, D_MODEL), lambda i: (i, 0)),
                  pl.BlockSpec((SUBLANES, tm), lambda i: (0, i)),
                  _rows3_spec(tm, lambda i: (i, 0)), _rows3_spec(tm, lambda i: (i, 0))],
        out_specs=pl.BlockSpec((tm, D_MODEL), lambda i: (i, 0)),
        out_shape=jax.ShapeDtypeStruct((t, D_MODEL), F32),
        compiler_params=pltpu.CompilerParams(dimension_semantics=("parallel",)),
        name="combine",
    )(x1, gate, y0, y1)


def _rope_tables(seq):
    half = HEAD_DIM // 2
    inv_freq = ROPE_THETA ** (-jnp.arange(half, dtype=F32) / half)
    coarse = (jnp.arange(seq // ROPE_SPLIT) * ROPE_SPLIT).astype(F32)[:, None, None] * inv_freq
    fine = jnp.arange(ROPE_SPLIT).astype(F32)[None, :, None] * inv_freq
    cos = (jnp.cos(coarse) * jnp.cos(fine) - jnp.sin(coarse) * jnp.sin(fine)).reshape(seq, half)
    sin = (jnp.sin(coarse) * jnp.cos(fine) + jnp.cos(coarse) * jnp.sin(fine)).reshape(seq, half)
    return jnp.concatenate([cos, cos], axis=-1), jnp.concatenate([-sin, sin], axis=-1)


def _layer(x, p, seq, rope, experts):
    t = x.shape[0]
    ma, sgb, q, k, v = _inproj(x, p, *rope, seq)
    x1, hn, route, gate, counts_f = _attn(x, ma, sgb, q, k, v, p, seq)

    block_rows = _moe_block_rows(t)
    counts = counts_f[:, 0].astype(I32)
    min_blocks = 1 if experts[0].dtype != BF16 else 0
    padded = jnp.maximum((counts + block_rows - 1) // block_rows, min_blocks) * block_rows
    pad_ends = jnp.cumsum(padded)
    pad_starts = pad_ends - padded
    n_blocks = (t * TOP_K) // block_rows + N_EXPERTS
    block_start = jnp.arange(n_blocks, dtype=I32) * block_rows
    in_expert = jnp.logical_and(block_start[:, None] >= pad_starts[None, :],
                                block_start[:, None] < pad_ends[None, :]).astype(I32)
    block_e = jnp.minimum(jnp.sum((block_start[:, None] >= pad_ends[None, :]).astype(I32), axis=1), N_EXPERTS - 1)
    n_valid = jnp.sum(in_expert * jnp.clip(pad_starts + counts - block_start[:, None], 0, block_rows), axis=1)
    n_used = pad_ends[-1:] // block_rows
    starts_col = jnp.broadcast_to(pad_starts.astype(F32)[:, None], (N_EXPERTS, LANES))

    dest = _dest(route, starts_col)
    dests = [dest[k].reshape(t // _sc_chunk(t), _sc_chunk(t)) for k in range(TOP_K)]
    xs = _dispatch(_as_rows3(hn), dests, n_blocks * block_rows)
    yb, experts = _moe(block_e, n_valid, n_used, _as_lines(xs), experts, block_rows)
    y0, y1 = _gather(_as_rows3(yb), dests)
    return _combine(x1, gate, _as_lines(y0), _as_lines(y1)), experts


def kernel(x_prompt, x_sample, norm_mix_g, w_in, norm_v_g, w_spatial, b_spatial, q_norm_g, k_norm_g, sink,
           w_proj_a, w_proj_b, w_out, norm_ffn_g, w_router_group, b_router_group, w_router_expert,
           b_router_expert, w_gate_e, w_up_e, w_down_e):
    depth = w_in.shape[0]
    layers = []
    for l in range(depth):
        w_router = jnp.zeros((ROUTER_ROWS, D_MODEL), F32)
        w_router = w_router.at[:N_GROUPS].set(w_router_group[l].T)
        w_router = w_router.at[SUBLANES:SUBLANES + N_EXPERTS].set(w_router_expert[l].T)
        b_router = jnp.zeros((ROUTER_ROWS,), F32)
        b_router = b_router.at[:N_GROUPS].set(b_router_group[l])
        b_router = b_router.at[SUBLANES:SUBLANES + N_EXPERTS].set(b_router_expert[l])
        b_router = jnp.broadcast_to(b_router[:, None], (ROUTER_ROWS, LANES))
        layers.append(dict(
            norm_mix_g=norm_mix_g[l][None], w_in=w_in[l].astype(BF16), norm_v_g=norm_v_g[l][None],
            w_spatial=w_spatial[l].astype(BF16),
            b_spatial=jnp.broadcast_to(b_spatial[l][:, :, None], (A_GROUPS, CHUNK, LANES)),
            q_norm_g=q_norm_g[l][None], k_norm_g=k_norm_g[l][None], sink=sink[l],
            w_proj_a=w_proj_a[l].astype(BF16), w_proj_b=w_proj_b[l].astype(BF16), w_out=w_out[l].astype(BF16),
            norm_ffn_g=norm_ffn_g[l][None], w_router=w_router.astype(BF16), b_router=b_router,
            experts=(w_gate_e[l], w_up_e[l], w_down_e[l])))

    trunks = [x_prompt, x_sample]
    rows = [x.reshape(-1, D_MODEL) for x in trunks]
    order = sorted(range(len(trunks)), key=lambda n: -rows[n].shape[0])
    rope = _rope_tables(max(x.shape[1] for x in trunks))
    for p in layers:
        experts = p["experts"]
        for n in order:
            rows[n], experts = _layer(rows[n], p, trunks[n].shape[1], rope, experts)
    return tuple(r.reshape(x.shape) for r, x in zip(rows, trunks))
```

```python
import functools

import jax
import jax.numpy as jnp
from jax import lax
from jax.experimental import pallas as pl
from jax.experimental.pallas import tpu as pltpu
from jax.experimental.pallas import tpu_sc as plsc

F32 = jnp.float32
BF16 = jnp.bfloat16
I32 = jnp.int32
U32 = jnp.uint32

LANES = 128
SUBLANES = 8
VMEM_BYTES_V7X = 64 * 1024 * 1024
SC_CORES = 2
SC_SUBCORES = 16
SC_WORKERS = SC_CORES * SC_SUBCORES
SC_CHUNK_MAX = 128

D_MODEL = 1024
A_WIDTH = D_MODEL
A_GROUPS = 8
CHUNK = 128
HEAD_DIM = 128
N_Q_HEADS = D_MODEL // HEAD_DIM
N_KV_HEADS = 2
REP = N_Q_HEADS // N_KV_HEADS
WINDOW = 128
ROPE_THETA = 10000.0
ROPE_SPLIT = 64
Q_W = N_Q_HEADS * HEAD_DIM
KV_W = N_KV_HEADS * HEAD_DIM
IN_W = 2 * A_WIDTH + Q_W + 2 * KV_W + 2 * D_MODEL
COL_U = 0
COL_V = COL_U + A_WIDTH
COL_Q = COL_V + A_WIDTH
COL_K = COL_Q + Q_W
COL_VA = COL_K + KV_W
COL_GA = COL_VA + KV_W
COL_GB = COL_GA + D_MODEL
N_GROUPS = 4
EXPERTS_PER_GROUP = 8
N_EXPERTS = N_GROUPS * EXPERTS_PER_GROUP
TOP_K = 2
D_EXPERT = 512
EPS = 1e-6
NEG = -1e30

TM_IN = 512
TM_IN_SUB = 256
TQ = 512
DENSE_COLS = 256
ROUTER_ROWS = 64
assert EXPERTS_PER_GROUP == SUBLANES and SUBLANES + N_EXPERTS <= ROUTER_ROWS
TM_ROW = 1024
TM_DEST = 8192
MOE_ROWS_MAX = 1024
MOE_MIN_BLOCKS = 1
VMEM_RESERVE = 8 * 1024 * 1024
VMEM_LIMIT = VMEM_BYTES_V7X - VMEM_RESERVE


def _rms(x, g):
    return x * lax.rsqrt(jnp.mean(x * x, axis=-1, keepdims=True) + EPS) * g


ROW_LINES = D_MODEL // 2 // LANES
ROW3 = (ROW_LINES, LANES)
HIGH_HALF = 0xFFFF0000


def _as_rows3(a):
    return a.reshape((a.shape[0] // ROW_LINES,) + ROW3)


def _as_lines(a):
    return a.reshape((a.shape[0] * ROW_LINES, LANES))


def _store_rows3(lines_ref, val):
    rows = val.shape[0]
    bits = lax.bitcast_convert_type(val.astype(BF16).astype(F32), U32)
    half = ROW_LINES * LANES
    for s in range(ROW_LINES):
        lo = bits[:, s * LANES:(s + 1) * LANES] >> 16
        hi = bits[:, half + s * LANES:half + (s + 1) * LANES] & U32(HIGH_HALF)
        lines_ref[pl.ds(s, rows, stride=ROW_LINES), :] = lax.bitcast_convert_type(lo | hi, I32)


def _load_rows3(lines_ref):
    rows = lines_ref.shape[0] // ROW_LINES
    words = [lax.bitcast_convert_type(lines_ref[pl.ds(s, rows, stride=ROW_LINES), :], U32)
             for s in range(ROW_LINES)]
    lo = [lax.bitcast_convert_type(w << 16, F32) for w in words]
    hi = [lax.bitcast_convert_type(w & U32(HIGH_HALF), F32) for w in words]
    return jnp.concatenate(lo + hi, axis=1)


def _rows3_spec(rows, index_map):
    return pl.BlockSpec((rows * ROW_LINES, LANES), index_map)


def _const_spec(shape):
    nd = len(shape)
    return pl.BlockSpec(shape, lambda *_: (0,) * nd, pipeline_mode=pl.Buffered(1))


def _inproj_kernel(x_ref, gmix_ref, win_ref, gv_ref, ws_ref, bs_ref, gq_ref, gk_ref, cos_ref, sin_ref,
                   wpa_ref, ma_ref, sgb_ref, q_ref, k_ref, v_ref, h_scr, u_scr, vn_scr, a_scr):
    tm = x_ref.shape[0]

    def stages(rows):
        def proj(lo, width):
            return jnp.dot(h_scr[rows], win_ref[:, lo:lo + width], preferred_element_type=F32)

        def norm_rope(z, g):
            zn = _rms(z, g)
            return zn * cos_ref[rows] + pltpu.roll(zn, HEAD_DIM // 2, 1) * sin_ref[rows]

        def norm():
            h_scr[rows] = _rms(x_ref[rows], gmix_ref[...]).astype(BF16)

        def mix_v():
            vn_scr[rows] = _rms(jax.nn.gelu(proj(COL_V, A_WIDTH)), gv_ref[...]).astype(BF16)

        def mix_u():
            u_scr[rows] = jax.nn.gelu(proj(COL_U, A_WIDTH))

        def spatial():
            for c in range(rows.start, rows.stop, CHUNK):
                chunk = slice(c, c + CHUNK)
                for g in range(A_GROUPS):
                    cols = slice(g * LANES, (g + 1) * LANES)
                    mixed = jnp.dot(ws_ref[g], vn_scr[chunk, cols], preferred_element_type=F32) + bs_ref[g]
                    a_scr[chunk, cols] = (u_scr[chunk, cols] * mixed).astype(BF16)

        def gate_a():
            ya = jnp.dot(a_scr[rows], wpa_ref[...], preferred_element_type=F32)
            ma_ref[rows] = jax.nn.sigmoid(proj(COL_GA, D_MODEL)) * ya

        def gate_b():
            sgb_ref[rows] = jax.nn.sigmoid(proj(COL_GB, D_MODEL))

        def queries():
            qz = proj(COL_Q, Q_W)
            for hd in range(N_Q_HEADS):
                cols = slice(hd * HEAD_DIM, (hd + 1) * HEAD_DIM)
                q_ref[rows, cols] = norm_rope(qz[:, cols], gq_ref[...]).astype(BF16)

        def keys_values():
            kz = proj(COL_K, KV_W)
            for hd in range(N_KV_HEADS):
                cols = slice(hd * HEAD_DIM, (hd + 1) * HEAD_DIM)
                k_ref[rows, cols] = norm_rope(kz[:, cols], gk_ref[...]).astype(BF16)
            v_ref[rows] = proj(COL_VA, KV_W).astype(BF16)

        return [norm, mix_v, mix_u, spatial, gate_a, gate_b, queries, keys_values]

    subs = [stages(slice(r, r + TM_IN_SUB)) for r in range(0, tm, TM_IN_SUB)]
    n_stage = len(subs[0])
    for step in range(n_stage + len(subs) - 1):
        for n, sub in enumerate(subs):
            if 0 <= step - n < n_stage:
                sub[step - n]()


def _inproj(x, p, cos, sin, seq):
    t = x.shape[0]
    tm = TM_IN
    n_pos = seq // tm
    row = lambda w: pl.BlockSpec((tm, w), lambda i: (i, 0))
    pos = pl.BlockSpec((tm, HEAD_DIM), lambda i: (i % n_pos, 0))
    return pl.pallas_call(
        _inproj_kernel,
        grid=(t // tm,),
        in_specs=[row(D_MODEL), _const_spec((1, D_MODEL)), _const_spec((D_MODEL, IN_W)),
                  _const_spec((1, A_WIDTH)), _const_spec((A_GROUPS, CHUNK, CHUNK)),
                  _const_spec((A_GROUPS, CHUNK, LANES)), _const_spec((1, HEAD_DIM)),
                  _const_spec((1, HEAD_DIM)), pos, pos, _const_spec((A_WIDTH, D_MODEL))],
        out_specs=[row(D_MODEL), row(D_MODEL), row(Q_W), row(KV_W), row(KV_W)],
        out_shape=[jax.ShapeDtypeStruct((t, D_MODEL), F32), jax.ShapeDtypeStruct((t, D_MODEL), F32),
                   jax.ShapeDtypeStruct((t, Q_W), BF16), jax.ShapeDtypeStruct((t, KV_W), BF16),
                   jax.ShapeDtypeStruct((t, KV_W), BF16)],
        scratch_shapes=[pltpu.VMEM((tm, D_MODEL), BF16), pltpu.VMEM((tm, A_WIDTH), F32),
                        pltpu.VMEM((tm, A_WIDTH), BF16), pltpu.VMEM((tm, A_WIDTH), BF16)],
        compiler_params=pltpu.CompilerParams(dimension_semantics=("parallel",),
                                             vmem_limit_bytes=VMEM_LIMIT),
        name="inproj",
    )(x, p["norm_mix_g"], p["w_in"], p["norm_v_g"], p["w_spatial"], p["b_spatial"], p["q_norm_g"],
      p["k_norm_g"], cos, sin, p["w_proj_a"])


def _attn_kernel(sink_ref, x_ref, ma_ref, sgb_ref, q_ref, kp_ref, kc_ref, kn_ref, vp_ref, vc_ref, vn_ref,
                 wpb_ref, wout_ref, gffn_ref, wr_ref, br_ref,
                 x1_ref, hn_ref, route_ref, gate_ref, counts_ref, kcat, vcat, o_scr, s_scr, p_scr, sink_scr, m_scr,
                 *, tiles_per_seq, n_tiles):
    tq = x_ref.shape[0]
    blk = WINDOW
    i = pl.program_id(0)
    slot = i % 2

    @pl.when(i == 0)
    def _():
        o_scr[...] = jnp.zeros_like(o_scr)
        counts_ref[...] = jnp.zeros_like(counts_ref)

    pos_tile = jnp.minimum(i, n_tiles - 1) % tiles_per_seq
    has_prev = pos_tile > 0
    has_next = pos_tile < tiles_per_seq - 1

    kcat[0:blk] = kp_ref[...]
    kcat[blk:blk + tq] = kc_ref[...]
    kcat[blk + tq:] = kn_ref[...]
    vcat[0:blk] = vp_ref[...]
    vcat[blk:blk + tq] = vc_ref[...]
    vcat[blk + tq:] = vn_ref[...]

    qr = lax.broadcasted_iota(I32, (blk, blk), 0)
    kc = lax.broadcasted_iota(I32, (blk, blk), 1)
    scale = HEAD_DIM ** -0.5
    n_sub = tq // blk
    pairs =[(j, g) for j in range(n_sub) for g in range(N_KV_HEADS)]

    def keys(ref, j, g):
        return ref[j * blk:(j + 3) * blk, g * HEAD_DIM:(g + 1) * HEAD_DIM]

    def head_cols(g, r):
        hd = g * REP + r
        return slice(hd * HEAD_DIM, (hd + 1) * HEAD_DIM)

    for b, (j, g) in enumerate(pairs):
        rows = slice(j * blk, (j + 1) * blk)
        qs = jnp.concatenate([q_ref[rows, head_cols(g, r)] for r in range(REP)], axis=0)
        s_scr[b] = lax.dot_general(qs, keys(kcat, j, g), (((1,), (1,)), ((), ())),
                                   preferred_element_type=F32)
    log2e = 1.4426950408889634

    def softmax(b, r):
        j, g = pairs[b]
        hrows = slice(r * blk, (r + 1) * blk)
        z = s_scr[b, hrows, :] * (scale * log2e)
        lo_ok = kc >= (qr + jnp.where(has_prev, 0, blk) if j == 0 else qr)
        hi_ok = kc <= (qr - jnp.where(has_next, 0, blk) if j == n_sub - 1 else qr)
        z = jnp.concatenate([jnp.where(lo_ok, z[:, :blk], NEG), z[:, blk:2 * blk],
                             jnp.where(hi_ok, z[:, 2 * blk:], NEG)], axis=1)
        sink = sink_ref[g * REP + r] * log2e
        m = jnp.maximum(jnp.max(z, axis=-1, keepdims=True), sink)
        p_scr[b, hrows, :] = jnp.exp2(z - m).astype(BF16)
        sink_scr[b, hrows, :] = jnp.broadcast_to(jnp.exp2(sink - m), (blk, LANES))

    def values(b):
        j, g = pairs[b]
        rows = slice(j * blk, (j + 1) * blk)
        v_ext = jnp.concatenate([keys(vcat, j, g), jnp.ones((3 * blk, HEAD_DIM), BF16)], axis=1)
        acc = jnp.dot(p_scr[b], v_ext, preferred_element_type=F32)
        o = (acc[:, :HEAD_DIM] / (acc[:, HEAD_DIM:] + sink_scr[b])).astype(BF16)
        for r in range(REP):
            o_scr[slot, rows, head_cols(g, r)] = o[r * blk:(r + 1) * blk, :]

    def merged_cols(cols):
        yb = jnp.dot(o_scr[1 - slot], wpb_ref[:, cols], preferred_element_type=F32)
        m_scr[:, cols] = (ma_ref[:, cols] + sgb_ref[:, cols] * yb).astype(BF16)

    def x1_cols(cols):
        x1_ref[:, cols] = x_ref[:, cols] + jnp.dot(m_scr[...], wout_ref[:, cols], preferred_element_type=F32)

    col_chunks = [slice(c * DENSE_COLS, (c + 1) * DENSE_COLS) for c in range(D_MODEL // DENSE_COLS)]
    dense = [functools.partial(f, cols) for f in (merged_cols, x1_cols) for cols in col_chunks]
    units = [(b, r) for b in range(len(pairs)) for r in range(REP)]
    units_per_dense = len(units) // len(dense)
    for n, (b, r) in enumerate(units):
        softmax(b, r)
        if (n + 1) % units_per_dense == 0:
            dense[(n + 1) // units_per_dense - 1]()
        if r == REP - 1:
            values(b)

    hn = _rms(x1_ref[...], gffn_ref[...])
    _store_rows3(hn_ref, hn)
    def wide(a):
        return jnp.concatenate([a] * (tq // LANES), axis=1)

    logits = lax.dot_general(wr_ref[...], hn.astype(BF16), (((1,), (1,)), ((), ())),
                             preferred_element_type=F32) + wide(br_ref[...])
    sub = lax.broadcasted_iota(I32, (SUBLANES, tq), 0).astype(F32)
    ninf = -jnp.inf

    def cmax(a):
        return jnp.max(a, axis=0, keepdims=True)

    def csum(a):
        return jnp.sum(a, axis=0, keepdims=True)

    def first_row(mask):
        return jnp.min(jnp.where(mask, sub, float(SUBLANES)), axis=0, keepdims=True)

    def group_rows(g):
        return logits[(g + 1) * SUBLANES:(g + 2) * SUBLANES]

    gl = jnp.where(sub < N_GROUPS, logits[0:SUBLANES], ninf)
    gmax = cmax(gl)
    g_sel = first_row(gl == gmax)
    g_p = 1.0 / csum(jnp.exp(gl - gmax))
    el = group_rows(0)
    for g in range(1, N_GROUPS):
        el = jnp.where(g_sel == g, group_rows(g), el)
    ee = jnp.exp(el - cmax(el))
    eprob = ee / csum(ee)
    p1 = cmax(eprob)
    i1 = first_row(eprob == p1)
    eprob2 = jnp.where(sub == i1, -1.0, eprob)
    p2 = cmax(eprob2)
    i2 = first_row(eprob2 == p2)
    psum = p1 + p2
    w1 = g_p * p1 / psum
    w2 = g_p * p2 / psum
    e1 = g_sel * EXPERTS_PER_GROUP + i1
    e2 = g_sel * EXPERTS_PER_GROUP + i2

    erow = lax.broadcasted_iota(I32, (N_EXPERTS, tq), 0).astype(F32)
    oh1 = erow == e1
    oh2 = erow == e2
    cnt = (jnp.where(oh1, 1.0, 0.0) + jnp.where(oh2, 1.0, 0.0)) * jnp.where(i > 0, 1.0, 0.0)
    ri = lax.broadcasted_iota(I32, (tq, tq), 0)
    ci = lax.broadcasted_iota(I32, (tq, tq), 1)
    earlier = jnp.where(ri < ci, 1.0, 0.0).astype(BF16)
    base = wide(counts_ref[...]) + jnp.dot(cnt.astype(BF16), earlier, preferred_element_type=F32)
    r1 = csum(jnp.where(oh1, base, 0.0))
    r2 = csum(jnp.where(oh2, base, 0.0))
    counts_ref[...] = counts_ref[...] + jnp.sum(cnt, axis=1, keepdims=True)

    route = jnp.where(sub == 0.0, e1, jnp.where(sub == 1.0, e2,
                      jnp.where(sub == 2.0, r1, jnp.where(sub == 3.0, r2, 0.0))))
    route_ref[...] = route.astype(I32)
    gate_ref[...] = jnp.where(sub == 0.0, w1, jnp.where(sub == 1.0, w2, 0.0))


def _attn(x, ma, sgb, q, k, v, p, seq):
    t = x.shape[0]
    tq = TQ
    sub = tq // WINDOW
    last_blk = t // WINDOW - 1
    n_tiles = t // tq
    att = lambda i: jnp.minimum(i, n_tiles - 1)
    post = lambda i: jnp.maximum(i - 1, 0)
    att_row = lambda w: pl.BlockSpec((tq, w), lambda i: (att(i), 0))
    row = lambda w: pl.BlockSpec((tq, w), lambda i: (post(i), 0))
    prev = pl.BlockSpec((WINDOW, KV_W), lambda i: (jnp.maximum(att(i) * sub - 1, 0), 0))
    nxt = pl.BlockSpec((WINDOW, KV_W), lambda i: (jnp.minimum((att(i) + 1) * sub, last_blk), 0))
    return pl.pallas_call(
        functools.partial(_attn_kernel, tiles_per_seq=seq // tq, n_tiles=n_tiles),
        grid=(n_tiles + 1,),
        in_specs=[pl.BlockSpec(memory_space=pltpu.SMEM),
                  row(D_MODEL), row(D_MODEL), row(D_MODEL), att_row(Q_W),
                  prev, att_row(KV_W), nxt, prev, att_row(KV_W), nxt,
                  _const_spec((Q_W, D_MODEL)), _const_spec((D_MODEL, D_MODEL)), _const_spec((1, D_MODEL)),
                  _const_spec((ROUTER_ROWS, D_MODEL)), _const_spec((ROUTER_ROWS, LANES))],
        out_specs=[row(D_MODEL), _rows3_spec(tq, lambda i: (post(i), 0)),
                   pl.BlockSpec((SUBLANES, tq), lambda i: (0, post(i))),
                   pl.BlockSpec((SUBLANES, tq), lambda i: (0, post(i))),
                   pl.BlockSpec((N_EXPERTS, LANES), lambda i: (0, 0))],
        out_shape=[jax.ShapeDtypeStruct((t, D_MODEL), F32), jax.ShapeDtypeStruct((t * ROW_LINES, LANES), I32),
                   jax.ShapeDtypeStruct((SUBLANES, t), I32), jax.ShapeDtypeStruct((SUBLANES, t), F32),
                   jax.ShapeDtypeStruct((N_EXPERTS, LANES), F32)],
        scratch_shapes=[pltpu.VMEM((tq + 2 * WINDOW, KV_W), BF16), pltpu.VMEM((tq + 2 * WINDOW, KV_W), BF16),
                        pltpu.VMEM((2, tq, Q_W), BF16),
                        pltpu.VMEM((sub * N_KV_HEADS, REP * WINDOW, 3 * WINDOW), F32),
                        pltpu.VMEM((sub * N_KV_HEADS, REP * WINDOW, 3 * WINDOW), BF16),
                        pltpu.VMEM((sub * N_KV_HEADS, REP * WINDOW, LANES), F32),
                        pltpu.VMEM((tq, D_MODEL), BF16)],
        compiler_params=pltpu.CompilerParams(dimension_semantics=("arbitrary",),
                                             vmem_limit_bytes=VMEM_LIMIT),
        name="attn",
    )(p["sink"], x, ma, sgb, q, k, k, k, v, v, v, p["w_proj_b"], p["w_out"], p["norm_ffn_g"],
      p["w_router"], p["b_router"])


def _dest_kernel(route_ref, starts_ref, dest_ref):
    route = route_ref[...].astype(F32)
    td = route.shape[1]
    sub = lax.broadcasted_iota(I32, route.shape, 0)
    erow = lax.broadcasted_iota(I32, (N_EXPERTS, td), 0).astype(F32)
    starts = jnp.broadcast_to(starts_ref[:, 0:1], (N_EXPERTS, td))

    def slot(k):
        start = jnp.sum(jnp.where(erow == route[k:k + 1], starts, 0.0), axis=0, keepdims=True)
        return start + route[TOP_K + k:TOP_K + k + 1]

    dest_ref[...] = jnp.where(sub == 0, slot(0), jnp.where(sub == 1, slot(1), 0.0)).astype(I32)


def _dest(route, pad_starts):
    t = route.shape[1]
    td = min(TM_DEST, t)
    return pl.pallas_call(
        _dest_kernel,
        grid=(t // td,),
        in_specs=[pl.BlockSpec((SUBLANES, td), lambda i: (0, i)), _const_spec((N_EXPERTS, LANES))],
        out_specs=pl.BlockSpec((SUBLANES, td), lambda i: (0, i)),
        out_shape=jax.ShapeDtypeStruct((SUBLANES, t), I32),
        compiler_params=pltpu.CompilerParams(dimension_semantics=("parallel",)),
        name="dest",
    )(route, pad_starts)


def _sc_mesh():
    return plsc.VectorSubcoreMesh(core_axis_name="c", subcore_axis_name="s")


def _sc_worker():
    return lax.axis_index("s") * SC_CORES + lax.axis_index("c")


def _sc_chunk(t):
    return min(SC_CHUNK_MAX, t // (SC_WORKERS * SUBLANES))


def _dispatch(hn, dests, n_rows):
    t = hn.shape[0]
    chunk = dests[0].shape[1]
    per_worker = t // SC_WORKERS
    n_chunks = per_worker // chunk
    idx = pltpu.VMEM((n_chunks, chunk), I32)

    @functools.partial(
        pl.kernel, mesh=_sc_mesh(), out_type=jax.ShapeDtypeStruct((n_rows,) + ROW3, I32),
        scratch_types=[idx, idx, pltpu.VMEM((chunk,) + ROW3, I32), pltpu.SemaphoreType.DMA])
    def scatter_rows(hn_hbm, d0_hbm, d1_hbm, xs_hbm, i0_v, i1_v, rows_v, sem):
        w = _sc_worker()
        pltpu.sync_copy(d0_hbm.at[pl.ds(w * n_chunks, n_chunks)], i0_v)
        pltpu.sync_copy(d1_hbm.at[pl.ds(w * n_chunks, n_chunks)], i1_v)

        @pl.loop(0, n_chunks)
        def _(j):
            pltpu.sync_copy(hn_hbm.at[pl.ds(w * per_worker + j * chunk, chunk)], rows_v)
            copies = [pltpu.make_async_copy(rows_v, xs_hbm.at[i_v.at[j]], sem) for i_v in (i0_v, i1_v)]
            for cp in copies:
                cp.start()
            for cp in copies:
                cp.wait()

    return scatter_rows(hn, *dests)


def _moe_block_rows(t):
    rows = MOE_ROWS_MAX
    while rows > CHUNK and (t * TOP_K) // N_EXPERTS < MOE_MIN_BLOCKS * rows:
        rows //= 2
    return rows


def _moe_kernel(be_ref, nvalid_ref, nused_ref, xs_ref, *refs, cast):
    i = pl.program_id(0)
    used = i < nused_ref[0]
    if cast:
        even, odd, yb_ref, weights = refs[0:3], refs[3:6], refs[6], refs[7:10]
    else:
        weights, yb_ref = refs[0:3], refs[3]

    @pl.when(jnp.logical_not(used))
    def _():
        yb_ref[...] = jnp.zeros_like(yb_ref)

    @pl.when(used)
    def _():
        if cast:
            e = be_ref[i]
            changed = jnp.logical_or(i == 0, e != be_ref[jnp.maximum(i - 1, 0)])
            for parity, srcs in enumerate((even, odd)):
                @pl.when(jnp.logical_and(changed, e % 2 == parity))
                def _(srcs=srcs):
                    for src, dst in zip(srcs, weights):
                        dst[0] = src[0].astype(BF16)

        row = lax.broadcasted_iota(I32, (xs_ref.shape[0] // ROW_LINES, 1), 0)
        x = jnp.where(row < nvalid_ref[i], _load_rows3(xs_ref), 0.0).astype(BF16)
        gate = jnp.dot(x, weights[0][0], preferred_element_type=F32)
        up = jnp.dot(x, weights[1][0], preferred_element_type=F32)
        hid = (jax.nn.silu(gate) * up).astype(BF16)
        _store_rows3(yb_ref, jnp.dot(hid, weights[2][0], preferred_element_type=F32))


def _moe(block_e, n_valid, n_used, xs, experts, block_rows):
    n_blocks = xs.shape[0] // (block_rows * ROW_LINES)
    cast = experts[0].dtype != BF16

    def rows(i, be, nv, nu):
        return (jnp.minimum(i, nu[0] - 1), 0)

    def expert(i, be, nv, nu):
        return be[jnp.minimum(i, nu[0] - 1)]

    def w_specs(index):
        return [pl.BlockSpec((1,) + w.shape[1:], lambda *a: (index(*a), 0, 0)) for w in experts]

    current = w_specs(expert)
    out_specs = [_rows3_spec(block_rows, lambda i, be, nv, nu: (i, 0))]
    out_shape = [jax.ShapeDtypeStruct(xs.shape, I32)]
    if cast:
        in_w = (w_specs(lambda *a: jnp.minimum(expert(*a) + expert(*a) % 2, N_EXPERTS - 2))
                + w_specs(lambda *a: expert(*a) | 1))
        operands = experts + experts
        out_specs += current
        out_shape += [jax.ShapeDtypeStruct(w.shape, BF16) for w in experts]
    else:
        in_w, operands = current, experts
    yb, *w16 = pl.pallas_call(
        functools.partial(_moe_kernel, cast=cast),
        grid_spec=pltpu.PrefetchScalarGridSpec(
            num_scalar_prefetch=3,
            grid=(n_blocks,),
            in_specs=[_rows3_spec(block_rows, rows)] + in_w,
            out_specs=out_specs,
        ),
        out_shape=out_shape,
        compiler_params=pltpu.CompilerParams(dimension_semantics=("arbitrary",),
                                             vmem_limit_bytes=VMEM_LIMIT),
        name="moe",
    )(block_e, n_valid, n_used, xs, *operands)
    return yb, (tuple(w16) if cast else experts)


def _gather(yb, dests):
    chunk = dests[0].shape[1]
    t = dests[0].shape[0] * chunk
    per_worker = t // SC_WORKERS
    n_chunks = per_worker // chunk
    idx = pltpu.VMEM((n_chunks, chunk), I32)
    out = jax.ShapeDtypeStruct((t,) + ROW3, I32)

    @functools.partial(
        pl.kernel, mesh=_sc_mesh(), out_type=(out, out),
        scratch_types=[idx, idx, pltpu.VMEM((chunk,) + ROW3, I32), pltpu.SemaphoreType.DMA])
    def gather_rows(yb_hbm, d0_hbm, d1_hbm, y0_hbm, y1_hbm, i0_v, i1_v, rows_v, sem):
        w = _sc_worker()
        pltpu.sync_copy(d0_hbm.at[pl.ds(w * n_chunks, n_chunks)], i0_v)
        pltpu.sync_copy(d1_hbm.at[pl.ds(w * n_chunks, n_chunks)], i1_v)

        @pl.loop(0, n_chunks)
        def _(j):
            rows = pl.ds(w * per_worker + j * chunk, chunk)
            for i_v, y_hbm in ((i0_v, y0_hbm), (i1_v, y1_hbm)):
                pltpu.async_copy(yb_hbm.at[i_v.at[j]], rows_v, sem).wait()
                pltpu.sync_copy(rows_v, y_hbm.at[rows])

    return gather_rows(yb, *dests)


def _combine_kernel(x1_ref, gate_ref, y0_ref, y1_ref, out_ref):
    tm = x1_ref.shape[0]
    gate = jnp.concatenate([gate_ref[...], jnp.zeros((LANES - SUBLANES, tm), F32)], axis=0).T
    out_ref[...] = x1_ref[...] + (_load_rows3(y0_ref) * gate[:, 0:1] + _load_rows3(y1_ref) * gate[:, 1:2])


def _combine(x1, gate, y0, y1):
    t = x1.shape[0]
    tm = TM_ROW
    return pl.pallas_call(
        _combine_kernel,
        grid=(t // tm,),
        in_specs=[pl.BlockSpec((tm, D_MODEL), lambda i: (i, 0)),
                  pl.BlockSpec((SUBLANES, tm), lambda i: (0, i)),
                  _rows3_spec(tm, lambda i: (i, 0)), _rows3_spec(tm, lambda i: (i, 0))],
        out_specs=pl.BlockSpec((tm, D_MODEL), lambda i: (i, 0)),
        out_shape=jax.ShapeDtypeStruct((t, D_MODEL), F32),
        compiler_params=pltpu.CompilerParams(dimension_semantics=("parallel",)),
        name="combine",
    )(x1, gate, y0, y1)


def _rope_tables(seq):
    half = HEAD_DIM // 2
    inv_freq = ROPE_THETA ** (-jnp.arange(half, dtype=F32) / half)
    coarse = (jnp.arange(seq // ROPE_SPLIT) * ROPE_SPLIT).astype(F32)[:, None, None] * inv_freq
    fine = jnp.arange(ROPE_SPLIT).astype(F32)[None, :, None] * inv_freq
    cos = (jnp.cos(coarse) * jnp.cos(fine) - jnp.sin(coarse) * jnp.sin(fine)).reshape(seq, half)
    sin = (jnp.sin(coarse) * jnp.cos(fine) + jnp.cos(coarse) * jnp.sin(fine)).reshape(seq, half)
    return jnp.concatenate([cos, cos], axis=-1), jnp.concatenate([-sin, sin], axis=-1)


def _layer(x, p, seq, rope, experts):
    t = x.shape[0]
    ma, sgb, q, k, v = _inproj(x, p, *rope, seq)
    x1, hn, route, gate, counts_f = _attn(x, ma, sgb, q, k, v, p, seq)

    block_rows = _moe_block_rows(t)
    counts = counts_f[:, 0].astype(I32)
    min_blocks = 1 if experts[0].dtype != BF16 else 0
    padded = jnp.maximum((counts + block_rows - 1) // block_rows, min_blocks) * block_rows
    pad_ends = jnp.cumsum(padded)
    pad_starts = pad_ends - padded
    n_blocks = (t * TOP_K) // block_rows + N_EXPERTS
    block_start = jnp.arange(n_blocks, dtype=I32) * block_rows
    in_expert = jnp.logical_and(block_start[:, None] >= pad_starts[None, :],
                                block_start[:, None] < pad_ends[None, :]).astype(I32)
    block_e = jnp.minimum(jnp.sum((block_start[:, None] >= pad_ends[None, :]).astype(I32), axis=1), N_EXPERTS - 1)
    n_valid = jnp.sum(in_expert * jnp.clip(pad_starts + counts - block_start[:, None], 0, block_rows), axis=1)
    n_used = pad_ends[-1:] // block_rows
    starts_col = jnp.broadcast_to(pad_starts.astype(F32)[:, None], (N_EXPERTS, LANES))

    dest = _dest(route, starts_col)
    dests = [dest[k].reshape(t // _sc_chunk(t), _sc_chunk(t)) for k in range(TOP_K)]
    xs = _dispatch(_as_rows3(hn), dests, n_blocks * block_rows)
    yb, experts = _moe(block_e, n_valid, n_used, _as_lines(xs), experts, block_rows)
    y0, y1 = _gather(_as_rows3(yb), dests)
    return _combine(x1, gate, _as_lines(y0), _as_lines(y1)), experts


def kernel(x_prompt, x_sample, norm_mix_g, w_in, norm_v_g, w_spatial, b_spatial, q_norm_g, k_norm_g, sink,
           w_proj_a, w_proj_b, w_out, norm_ffn_g, w_router_group, b_router_group, w_router_expert,
           b_router_expert, w_gate_e, w_up_e, w_down_e):
    depth = w_in.shape[0]
    layers = []
    for l in range(depth):
        w_router = jnp.zeros((ROUTER_ROWS, D_MODEL), F32)
        w_router = w_router.at[:N_GROUPS].set(w_router_group[l].T)
        w_router = w_router.at[SUBLANES:SUBLANES + N_EXPERTS].set(w_router_expert[l].T)
        b_router = jnp.zeros((ROUTER_ROWS,), F32)
        b_router = b_router.at[:N_GROUPS].set(b_router_group[l])
        b_router = b_router.at[SUBLANES:SUBLANES + N_EXPERTS].set(b_router_expert[l])
        b_router = jnp.broadcast_to(b_router[:, None], (ROUTER_ROWS, LANES))
        layers.append(dict(
            norm_mix_g=norm_mix_g[l][None], w_in=w_in[l].astype(BF16), norm_v_g=norm_v_g[l][None],
            w_spatial=w_spatial[l].astype(BF16),
            b_spatial=jnp.broadcast_to(b_spatial[l][:, :, None], (A_GROUPS, CHUNK, LANES)),
            q_norm_g=q_norm_g[l][None], k_norm_g=k_norm_g[l][None], sink=sink[l],
            w_proj_a=w_proj_a[l].astype(BF16), w_proj_b=w_proj_b[l].astype(BF16), w_out=w_out[l].astype(BF16),
            norm_ffn_g=norm_ffn_g[l][None], w_router=w_router.astype(BF16), b_router=b_router,
            experts=(w_gate_e[l], w_up_e[l], w_down_e[l])))

    trunks = [x_prompt, x_sample]
    rows = [x.reshape(-1, D_MODEL) for x in trunks]
    order = sorted(range(len(trunks)), key=lambda n: -rows[n].shape[0])
    rope = _rope_tables(max(x.shape[1] for x in trunks))
    for p in layers:
        experts = p["experts"]
        for n in order:
            rows[n], experts = _layer(rows[n], p, trunks[n].shape[1], rope, experts)
    return tuple(r.reshape(x.shape) for r, x in zip(rows, trunks))
```

```python
import functools

import jax
import jax.numpy as jnp
from jax import lax
from jax.experimental import pallas as pl
from jax.experimental.pallas import tpu as pltpu
from jax.experimental.pallas import tpu_sc as plsc

F32 = jnp.float32
BF16 = jnp.bfloat16
I32 = jnp.int32
U32 = jnp.uint32

LANES = 128
SUBLANES = 8
VMEM_BYTES_V7X = 64 * 1024 * 1024
SC_CORES = 2
SC_SUBCORES = 16
SC_WORKERS = SC_CORES * SC_SUBCORES
SC_CHUNK_MAX = 128

D_MODEL = 1024
A_WIDTH = D_MODEL
A_GROUPS = 8
CHUNK = 128
HEAD_DIM = 128
N_Q_HEADS = D_MODEL // HEAD_DIM
N_KV_HEADS = 2
REP = N_Q_HEADS // N_KV_HEADS
WINDOW = 128
ROPE_THETA = 10000.0
ROPE_SPLIT = 64
Q_W = N_Q_HEADS * HEAD_DIM
KV_W = N_KV_HEADS * HEAD_DIM
IN_W = 2 * A_WIDTH + Q_W + 2 * KV_W + 2 * D_MODEL
COL_U = 0
COL_V = COL_U + A_WIDTH
COL_Q = COL_V + A_WIDTH
COL_K = COL_Q + Q_W
COL_VA = COL_K + KV_W
COL_GA = COL_VA + KV_W
COL_GB = COL_GA + D_MODEL
N_GROUPS = 4
EXPERTS_PER_GROUP = 8
N_EXPERTS = N_GROUPS * EXPERTS_PER_GROUP
TOP_K = 2
D_EXPERT = 512
EPS = 1e-6
NEG = -1e30

TM_IN = 512
TM_IN_SUB = 256
TQ = 512
DENSE_COLS = 256
ROUTER_ROWS = 64
assert EXPERTS_PER_GROUP == SUBLANES and SUBLANES + N_EXPERTS <= ROUTER_ROWS
TM_ROW = 1024
TM_DEST = 8192
MOE_ROWS_MAX = 1024
MOE_MIN_BLOCKS = 1
VMEM_RESERVE = 8 * 1024 * 1024
VMEM_LIMIT = VMEM_BYTES_V7X - VMEM_RESERVE


def _rms(x, g):
    return x * lax.rsqrt(jnp.mean(x * x, axis=-1, keepdims=True) + EPS) * g


ROW_LINES = D_MODEL // 2 // LANES
ROW3 = (ROW_LINES, LANES)
HIGH_HALF = 0xFFFF0000


def _as_rows3(a):
    return a.reshape((a.shape[0] // ROW_LINES,) + ROW3)


def _as_lines(a):
    return a.reshape((a.shape[0] * ROW_LINES, LANES))


def _store_rows3(lines_ref, val):
    rows = val.shape[0]
    bits = lax.bitcast_convert_type(val.astype(BF16).astype(F32), U32)
    half = ROW_LINES * LANES
    for s in range(ROW_LINES):
        lo = bits[:, s * LANES:(s + 1) * LANES] >> 16
        hi = bits[:, half + s * LANES:half + (s + 1) * LANES] & U32(HIGH_HALF)
        lines_ref[pl.ds(s, rows, stride=ROW_LINES), :] = lax.bitcast_convert_type(lo | hi, I32)


def _load_rows3(lines_ref):
    rows = lines_ref.shape[0] // ROW_LINES
    words = [lax.bitcast_convert_type(lines_ref[pl.ds(s, rows, stride=ROW_LINES), :], U32)
             for s in range(ROW_LINES)]
    lo = [lax.bitcast_convert_type(w << 16, F32) for w in words]
    hi = [lax.bitcast_convert_type(w & U32(HIGH_HALF), F32) for w in words]
    return jnp.concatenate(lo + hi, axis=1)


def _rows3_spec(rows, index_map):
    return pl.BlockSpec((rows * ROW_LINES, LANES), index_map)


def _const_spec(shape):
    nd = len(shape)
    return pl.BlockSpec(shape, lambda *_: (0,) * nd, pipeline_mode=pl.Buffered(1))


def _inproj_kernel(x_ref, gmix_ref, win_ref, gv_ref, ws_ref, bs_ref, gq_ref, gk_ref, cos_ref, sin_ref,
                   wpa_ref, ma_ref, sgb_ref, q_ref, k_ref, v_ref, h_scr, u_scr, vn_scr, a_scr):
    tm = x_ref.shape[0]

    def stages(rows):
        def proj(lo, width):
            return jnp.dot(h_scr[rows], win_ref[:, lo:lo + width], preferred_element_type=F32)

        def norm_rope(z, g):
            zn = _rms(z, g)
            return zn * cos_ref[rows] + pltpu.roll(zn, HEAD_DIM // 2, 1) * sin_ref[rows]

        def norm():
            h_scr[rows] = _rms(x_ref[rows], gmix_ref[...]).astype(BF16)

        def mix_v():
            vn_scr[rows] = _rms(jax.nn.gelu(proj(COL_V, A_WIDTH)), gv_ref[...]).astype(BF16)

        def mix_u():
            u_scr[rows] = jax.nn.gelu(proj(COL_U, A_WIDTH))

        def spatial():
            for c in range(rows.start, rows.stop, CHUNK):
                chunk = slice(c, c + CHUNK)
                for g in range(A_GROUPS):
                    cols = slice(g * LANES, (g + 1) * LANES)
                    mixed = jnp.dot(ws_ref[g], vn_scr[chunk, cols], preferred_element_type=F32) + bs_ref[g]
                    a_scr[chunk, cols] = (u_scr[chunk, cols] * mixed).astype(BF16)

        def gate_a():
            ya = jnp.dot(a_scr[rows], wpa_ref[...], preferred_element_type=F32)
            ma_ref[rows] = jax.nn.sigmoid(proj(COL_GA, D_MODEL)) * ya

        def gate_b():
            sgb_ref[rows] = jax.nn.sigmoid(proj(COL_GB, D_MODEL))

        def queries():
            qz = proj(COL_Q, Q_W)
            for hd in range(N_Q_HEADS):
                cols = slice(hd * HEAD_DIM, (hd + 1) * HEAD_DIM)
                q_ref[rows, cols] = norm_rope(qz[:, cols], gq_ref[...]).astype(BF16)

        def keys_values():
            kz = proj(COL_K, KV_W)
            for hd in range(N_KV_HEADS):
                cols = slice(hd * HEAD_DIM, (hd + 1) * HEAD_DIM)
                k_ref[rows, cols] = norm_rope(kz[:, cols], gk_ref[...]).astype(BF16)
            v_ref[rows] = proj(COL_VA, KV_W).astype(BF16)

        return [norm, mix_v, mix_u, spatial, gate_a, gate_b, queries, keys_values]

    subs = [stages(slice(r, r + TM_IN_SUB)) for r in range(0, tm, TM_IN_SUB)]
    n_stage = len(subs[0])
    for step in range(n_stage + len(subs) - 1):
        for n, sub in enumerate(subs):
            if 0 <= step - n < n_stage:
                sub[step - n]()


def _inproj(x, p, cos, sin, seq):
    t = x.shape[0]
    tm = TM_IN
    n_pos = seq // tm
    row = lambda w: pl.BlockSpec((tm, w), lambda i: (i, 0))
    pos = pl.BlockSpec((tm, HEAD_DIM), lambda i: (i % n_pos, 0))
    return pl.pallas_call(
        _inproj_kernel,
        grid=(t // tm,),
        in_specs=[row(D_MODEL), _const_spec((1, D_MODEL)), _const_spec((D_MODEL, IN_W)),
                  _const_spec((1, A_WIDTH)), _const_spec((A_GROUPS, CHUNK, CHUNK)),
                  _const_spec((A_GROUPS, CHUNK, LANES)), _const_spec((1, HEAD_DIM)),
                  _const_spec((1, HEAD_DIM)), pos, pos, _const_spec((A_WIDTH, D_MODEL))],
        out_specs=[row(D_MODEL), row(D_MODEL), row(Q_W), row(KV_W), row(KV_W)],
        out_shape=[jax.ShapeDtypeStruct((t, D_MODEL), F32), jax.ShapeDtypeStruct((t, D_MODEL), F32),
                   jax.ShapeDtypeStruct((t, Q_W), BF16), jax.ShapeDtypeStruct((t, KV_W), BF16),
                   jax.ShapeDtypeStruct((t, KV_W), BF16)],
        scratch_shapes=[pltpu.VMEM((tm, D_MODEL), BF16), pltpu.VMEM((tm, A_WIDTH), F32),
                        pltpu.VMEM((tm, A_WIDTH), BF16), pltpu.VMEM((tm, A_WIDTH), BF16)],
        compiler_params=pltpu.CompilerParams(dimension_semantics=("parallel",),
                                             vmem_limit_bytes=VMEM_LIMIT),
        name="inproj",
    )(x, p["norm_mix_g"], p["w_in"], p["norm_v_g"], p["w_spatial"], p["b_spatial"], p["q_norm_g"],
      p["k_norm_g"], cos, sin, p["w_proj_a"])


def _attn_kernel(sink_ref, x_ref, ma_ref, sgb_ref, q_ref, kp_ref, kc_ref, kn_ref, vp_ref, vc_ref, vn_ref,
                 wpb_ref, wout_ref, gffn_ref, wr_ref, br_ref,
                 x1_ref, hn_ref, route_ref, gate_ref, counts_ref, kcat, vcat, o_scr, s_scr, p_scr, sink_scr, m_scr,
                 *, tiles_per_seq, n_tiles):
    tq = x_ref.shape[0]
    blk = WINDOW
    i = pl.program_id(0)
    slot = i % 2

    @pl.when(i == 0)
    def _():
        o_scr[...] = jnp.zeros_like(o_scr)
        counts_ref[...] = jnp.zeros_like(counts_ref)

    pos_tile = jnp.minimum(i, n_tiles - 1) % tiles_per_seq
    has_prev = pos_tile > 0
    has_next = pos_tile < tiles_per_seq - 1

    kcat[0:blk] = kp_ref[...]
    kcat[blk:blk + tq] = kc_ref[...]
    kcat[blk + tq:] = kn_ref[...]
    vcat[0:blk] = vp_ref[...]
    vcat[blk:blk + tq] = vc_ref[...]
    vcat[blk + tq:] = vn_ref[...]

    qr = lax.broadcasted_iota(I32, (blk, blk), 0)
    kc = lax.broadcasted_iota(I32, (blk, blk), 1)
    scale = HEAD_DIM ** -0.5
    n_sub = tq // blk
    pairs = [(j, g) for j in range(n_sub) for g in range(N_KV_HEADS)]

    def keys(ref, j, g):
        return ref[j * blk:(j + 3) * blk, g * HEAD_DIM:(g + 1) * HEAD_DIM]

    def head_cols(g, r):
        hd = g * REP + r
        return slice(hd * HEAD_DIM, (hd + 1) * HEAD_DIM)

    for b, (j, g) in enumerate(pairs):
        rows = slice(j * blk, (j + 1) * blk)
        qs = jnp.concatenate([q_ref[rows, head_cols(g, r)] for r in range(REP)], axis=0)
        s_scr[b] = lax.dot_general(qs, keys(kcat, j, g), (((1,), (1,)), ((), ())),
                                   preferred_element_type=F32)
    log2e = 1.4426950408889634

    def softmax(b, r):
        j, g = pairs[b]
        hrows = slice(r * blk, (r + 1) * blk)
        z = s_scr[b, hrows, :] * (scale * log2e)
        lo_ok = kc >= (qr + jnp.where(has_prev, 0, blk) if j == 0 else qr)
        hi_ok = kc <= (qr - jnp.where(has_next, 0, blk) if j == n_sub - 1 else qr)
        z = jnp.concatenate([jnp.where(lo_ok, z[:, :blk], NEG), z[:, blk:2 * blk],
                             jnp.where(hi_ok, z[:, 2 * blk:], NEG)], axis=1)
        sink = sink_ref[g * REP + r] * log2e
        m = jnp.maximum(jnp.max(z, axis=-1, keepdims=True), sink)
        p_scr[b, hrows, :] = jnp.exp2(z - m).astype(BF16)
        sink_scr[b, hrows, :] = jnp.broadcast_to(jnp.exp2(sink - m), (blk, LANES))

    def values(b):
        j, g = pairs[b]
        rows = slice(j * blk, (j + 1) * blk)
        v_ext = jnp.concatenate([keys(vcat, j, g), jnp.ones((3 * blk, HEAD_DIM), BF16)], axis=1)
        acc = jnp.dot(p_scr[b], v_ext, preferred_element_type=F32)
        o = (acc[:, :HEAD_DIM] / (acc[:, HEAD_DIM:] + sink_scr[b])).astype(BF16)
        for r in range(REP):
            o_scr[slot, rows, head_cols(g, r)] = o[r * blk:(r + 1) * blk, :]

    def merged_cols(cols):
        yb = jnp.dot(o_scr[1 - slot], wpb_ref[:, cols], preferred_element_type=F32)
        m_scr[:, cols] = (ma_ref[:, cols] + sgb_ref[:, cols] * yb).astype(BF16)

    def x1_cols(cols):
        x1_ref[:, cols] = x_ref[:, cols] + jnp.dot(m_scr[...], wout_ref[:, cols], preferred_element_type=F32)

    col_chunks = [slice(c * DENSE_COLS, (c + 1) * DENSE_COLS) for c in range(D_MODEL // DENSE_COLS)]
    dense = [functools.partial(f, cols) for f in (merged_cols, x1_cols) for cols in col_chunks]
    units = [(b, r) for b in range(len(pairs)) for r in range(REP)]
    units_per_dense = len(units) // len(dense)
    for n, (b, r) in enumerate(units):
        softmax(b, r)
        if (n + 1) % units_per_dense == 0:
            dense[(n + 1) // units_per_dense - 1]()
        if r == REP - 1:
            values(b)

    hn = _rms(x1_ref[...], gffn_ref[...])
    _store_rows3(hn_ref, hn)
    def wide(a):
        return jnp.concatenate([a] * (tq // LANES), axis=1)

    logits = lax.dot_general(wr_ref[...], hn.astype(BF16), (((1,), (1,)), ((), ())),
                             preferred_element_type=F32) + wide(br_ref[...])
    sub = lax.broadcasted_iota(I32, (SUBLANES, tq), 0).astype(F32)
    ninf = -jnp.inf

    def cmax(a):
        return jnp.max(a, axis=0, keepdims=True)

    def csum(a):
        return jnp.sum(a, axis=0, keepdims=True)

    def first_row(mask):
        return jnp.min(jnp.where(mask, sub, float(SUBLANES)), axis=0, keepdims=True)

    def group_rows(g):
        return logits[(g + 1) * SUBLANES:(g + 2) * SUBLANES]

    gl = jnp.where(sub < N_GROUPS, logits[0:SUBLANES], ninf)
    gmax = cmax(gl)
    g_sel = first_row(gl == gmax)
    g_p = 1.0 / csum(jnp.exp(gl - gmax))
    el = group_rows(0)
    for g in range(1, N_GROUPS):
        el = jnp.where(g_sel == g, group_rows(g), el)
    ee = jnp.exp(el - cmax(el))
    eprob = ee / csum(ee)
    p1 = cmax(eprob)
    i1 = first_row(eprob == p1)
    eprob2 = jnp.where(sub == i1, -1.0, eprob)
    p2 = cmax(eprob2)
    i2 = first_row(eprob2 == p2)
    psum = p1 + p2
    w1 = g_p * p1 / psum
    w2 = g_p * p2 / psum
    e1 = g_sel * EXPERTS_PER_GROUP + i1
    e2 = g_sel * EXPERTS_PER_GROUP + i2

    erow = lax.broadcasted_iota(I32, (N_EXPERTS, tq), 0).astype(F32)
    oh1 = erow == e1
    oh2 = erow == e2
    cnt = (jnp.where(oh1, 1.0, 0.0) + jnp.where(oh2, 1.0, 0.0)) * jnp.where(i > 0, 1.0, 0.0)
    ri = lax.broadcasted_iota(I32, (tq, tq), 0)
    ci = lax.broadcasted_iota(I32, (tq, tq), 1)
    earlier = jnp.where(ri < ci, 1.0, 0.0).astype(BF16)
    base = wide(counts_ref[...]) + jnp.dot(cnt.astype(BF16), earlier, preferred_element_type=F32)
    r1 = csum(jnp.where(oh1, base, 0.0))
    r2 = csum(jnp.where(oh2, base, 0.0))
    counts_ref[...] = counts_ref[...] + jnp.sum(cnt, axis=1, keepdims=True)

    route = jnp.where(sub == 0.0, e1, jnp.where(sub == 1.0, e2,
                      jnp.where(sub == 2.0, r1, jnp.where(sub == 3.0, r2, 0.0))))
    route_ref[...] = route.astype(I32)
    gate_ref[...] = jnp.where(sub == 0.0, w1, jnp.where(sub == 1.0, w2, 0.0))


def _attn(x, ma, sgb, q, k, v, p, seq):
    t = x.shape[0]
    tq = TQ
    sub = tq // WINDOW
    last_blk = t // WINDOW - 1
    n_tiles = t // tq
    att = lambda i: jnp.minimum(i, n_tiles - 1)
    post = lambda i: jnp.maximum(i - 1, 0)
    att_row = lambda w: pl.BlockSpec((tq, w), lambda i: (att(i), 0))
    row = lambda w: pl.BlockSpec((tq, w), lambda i: (post(i), 0))
    prev = pl.BlockSpec((WINDOW, KV_W), lambda i: (jnp.maximum(att(i) * sub - 1, 0), 0))
    nxt = pl.BlockSpec((WINDOW, KV_W), lambda i: (jnp.minimum((att(i) + 1) * sub, last_blk), 0))
    return pl.pallas_call(
        functools.partial(_attn_kernel, tiles_per_seq=seq // tq, n_tiles=n_tiles),
        grid=(n_tiles + 1,),
        in_specs=[pl.BlockSpec(memory_space=pltpu.SMEM),
                  row(D_MODEL), row(D_MODEL), row(D_MODEL), att_row(Q_W),
                  prev, att_row(KV_W), nxt, prev, att_row(KV_W), nxt,
                  _const_spec((Q_W, D_MODEL)), _const_spec((D_MODEL, D_MODEL)), _const_spec((1, D_MODEL)),
                  _const_spec((ROUTER_ROWS, D_MODEL)), _const_spec((ROUTER_ROWS, LANES))],
        out_specs=[row(D_MODEL), _rows3_spec(tq, lambda i: (post(i), 0)),
                   pl.BlockSpec((SUBLANES, tq), lambda i: (0, post(i))),
                   pl.BlockSpec((SUBLANES, tq), lambda i: (0, post(i))),
                   pl.BlockSpec((N_EXPERTS, LANES), lambda i: (0, 0))],
        out_shape=[jax.ShapeDtypeStruct((t, D_MODEL), F32), jax.ShapeDtypeStruct((t * ROW_LINES, LANES), I32),
                   jax.ShapeDtypeStruct((SUBLANES, t), I32), jax.ShapeDtypeStruct((SUBLANES, t), F32),
                   jax.ShapeDtypeStruct((N_EXPERTS, LANES), F32)],
        scratch_shapes=[pltpu.VMEM((tq + 2 * WINDOW, KV_W), BF16), pltpu.VMEM((tq + 2 * WINDOW, KV_W), BF16),
                        pltpu.VMEM((2, tq, Q_W), BF16),
                        pltpu.VMEM((sub * N_KV_HEADS, REP * WINDOW, 3 * WINDOW), F32),
                        pltpu.VMEM((sub * N_KV_HEADS, REP * WINDOW, 3 * WINDOW), BF16),
                        pltpu.VMEM((sub * N_KV_HEADS, REP * WINDOW, LANES), F32),
                        pltpu.VMEM((tq, D_MODEL), BF16)],
        compiler_params=pltpu.CompilerParams(dimension_semantics=("arbitrary",),
                                             vmem_limit_bytes=VMEM_LIMIT),
        name="attn",
    )(p["sink"], x, ma, sgb, q, k, k, k, v, v, v, p["w_proj_b"], p["w_out"], p["norm_ffn_g"],
      p["w_router"], p["b_router"])


def _dest_kernel(route_ref, starts_ref, dest_ref):
    route = route_ref[...].astype(F32)
    td = route.shape[1]
    sub = lax.broadcasted_iota(I32, route.shape, 0)
    erow = lax.broadcasted_iota(I32, (N_EXPERTS, td), 0).astype(F32)
    starts = jnp.broadcast_to(starts_ref[:, 0:1], (N_EXPERTS, td))

    def slot(k):
        start = jnp.sum(jnp.where(erow == route[k:k + 1], starts, 0.0), axis=0, keepdims=True)
        return start + route[TOP_K + k:TOP_K + k + 1]

    dest_ref[...] = jnp.where(sub == 0, slot(0), jnp.where(sub == 1, slot(1), 0.0)).astype(I32)


def _dest(route, pad_starts):
    t = route.shape[1]
    td = min(TM_DEST, t)
    return pl.pallas_call(
        _dest_kernel,
        grid=(t // td,),
        in_specs=[pl.BlockSpec((SUBLANES, td), lambda i: (0, i)), _const_spec((N_EXPERTS, LANES))],
        out_specs=pl.BlockSpec((SUBLANES, td), lambda i: (0, i)),
        out_shape=jax.ShapeDtypeStruct((SUBLANES, t), I32),
        compiler_params=pltpu.CompilerParams(dimension_semantics=("parallel",)),
        name="dest",
    )(route, pad_starts)


def _sc_mesh():
    return plsc.VectorSubcoreMesh(core_axis_name="c", subcore_axis_name="s")


def _sc_worker():
    return lax.axis_index("s") * SC_CORES + lax.axis_index("c")


def _sc_chunk(t):
    return min(SC_CHUNK_MAX, t // (SC_WORKERS * SUBLANES))


def _dispatch(hn, dests, n_rows):
    t = hn.shape[0]
    chunk = dests[0].shape[1]
    per_worker = t // SC_WORKERS
    n_chunks = per_worker // chunk
    idx = pltpu.VMEM((n_chunks, chunk), I32)

    @functools.partial(
        pl.kernel, mesh=_sc_mesh(), out_type=jax.ShapeDtypeStruct((n_rows,) + ROW3, I32),
        scratch_types=[idx, idx, pltpu.VMEM((chunk,) + ROW3, I32), pltpu.SemaphoreType.DMA])
    def scatter_rows(hn_hbm, d0_hbm, d1_hbm, xs_hbm, i0_v, i1_v, rows_v, sem):
        w = _sc_worker()
        pltpu.sync_copy(d0_hbm.at[pl.ds(w * n_chunks, n_chunks)], i0_v)
        pltpu.sync_copy(d1_hbm.at[pl.ds(w * n_chunks, n_chunks)], i1_v)

        @pl.loop(0, n_chunks)
        def _(j):
            pltpu.sync_copy(hn_hbm.at[pl.ds(w * per_worker + j * chunk, chunk)], rows_v)
            copies = [pltpu.make_async_copy(rows_v, xs_hbm.at[i_v.at[j]], sem) for i_v in (i0_v, i1_v)]
            for cp in copies:
                cp.start()
            for cp in copies:
                cp.wait()

    return scatter_rows(hn, *dests)


def _moe_block_rows(t):
    rows = MOE_ROWS_MAX
    while rows > CHUNK and (t * TOP_K) // N_EXPERTS < MOE_MIN_BLOCKS * rows:
        rows //= 2
    return rows


def _moe_kernel(be_ref, nvalid_ref, nused_ref, xs_ref, wg_ref, wu_ref, wd_ref, yb_ref, *w16_refs):
    i = pl.program_id(0)
    used = i < nused_ref[0]
    weights = w16_refs if w16_refs else (wg_ref, wu_ref, wd_ref)

    @pl.when(jnp.logical_not(used))
    def _():
        yb_ref[...] = jnp.zeros_like(yb_ref)

    @pl.when(used)
    def _():
        if w16_refs:
            @pl.when(jnp.logical_or(i == 0, be_ref[i] != be_ref[jnp.maximum(i - 1, 0)]))
            def _():
                for src, dst in zip((wg_ref, wu_ref, wd_ref), w16_refs):
                    dst[0] = src[0].astype(BF16)

        row = lax.broadcasted_iota(I32, (xs_ref.shape[0] // ROW_LINES, 1), 0)
        x = jnp.where(row < nvalid_ref[i], _load_rows3(xs_ref), 0.0).astype(BF16)
        gate = jnp.dot(x, weights[0][0], preferred_element_type=F32)
        up = jnp.dot(x, weights[1][0], preferred_element_type=F32)
        hid = (jax.nn.silu(gate) * up).astype(BF16)
        _store_rows3(yb_ref, jnp.dot(hid, weights[2][0], preferred_element_type=F32))


def _moe(block_e, n_valid, n_used, xs, experts, block_rows):
    n_blocks = xs.shape[0] // (block_rows * ROW_LINES)
    cast = experts[0].dtype != BF16

    def rows(i, be, nv, nu):
        return (jnp.minimum(i, nu[0] - 1), 0)

    def expert(i, be, nv, nu):
        return (be[jnp.minimum(i, nu[0] - 1)], 0, 0)

    w_specs = [pl.BlockSpec((1,) + w.shape[1:], expert) for w in experts]
    out_specs = [_rows3_spec(block_rows, lambda i, be, nv, nu: (i, 0))]
    out_shape = [jax.ShapeDtypeStruct(xs.shape, I32)]
    if cast:
        out_specs += w_specs
        out_shape += [jax.ShapeDtypeStruct(w.shape, BF16) for w in experts]
    yb, *w16 = pl.pallas_call(
        _moe_kernel,
        grid_spec=pltpu.PrefetchScalarGridSpec(
            num_scalar_prefetch=3,
            grid=(n_blocks,),
            in_specs=[_rows3_spec(block_rows, rows)] + w_specs,
            out_specs=out_specs,
        ),
        out_shape=out_shape,
        compiler_params=pltpu.CompilerParams(dimension_semantics=("arbitrary",),
                                             vmem_limit_bytes=VMEM_LIMIT),
        name="moe",
    )(block_e, n_valid, n_used, xs, *experts)
    return yb, (tuple(w16) if cast else experts)


def _gather(yb, dests):
    chunk = dests[0].shape[1]
    t = dests[0].shape[0] * chunk
    per_worker = t // SC_WORKERS
    n_chunks = per_worker // chunk
    idx = pltpu.VMEM((n_chunks, chunk), I32)
    out = jax.ShapeDtypeStruct((t,) + ROW3, I32)

    @functools.partial(
        pl.kernel, mesh=_sc_mesh(), out_type=(out, out),
        scratch_types=[idx, idx, pltpu.VMEM((chunk,) + ROW3, I32), pltpu.SemaphoreType.DMA])
    def gather_rows(yb_hbm, d0_hbm, d1_hbm, y0_hbm, y1_hbm, i0_v, i1_v, rows_v, sem):
        w = _sc_worker()
        pltpu.sync_copy(d0_hbm.at[pl.ds(w * n_chunks, n_chunks)], i0_v)
        pltpu.sync_copy(d1_hbm.at[pl.ds(w * n_chunks, n_chunks)], i1_v)

        @pl.loop(0, n_chunks)
        def _(j):
            rows = pl.ds(w * per_worker + j * chunk, chunk)
            for i_v, y_hbm in ((i0_v, y0_hbm), (i1_v, y1_hbm)):
                pltpu.async_copy(yb_hbm.at[i_v.at[j]], rows_v, sem).wait()
                pltpu.sync_copy(rows_v, y_hbm.at[rows])

    return gather_rows(yb, *dests)


def _combine_kernel(x1_ref, gate_ref, y0_ref, y1_ref, out_ref):
    tm = x1_ref.shape[0]
    gate = jnp.concatenate([gate_ref[...], jnp.zeros((LANES - SUBLANES, tm), F32)], axis=0).T
    out_ref[...] = x1_ref[...] + (_load_rows3(y0_ref) * gate[:, 0:1] + _load_rows3(y1_ref) * gate[:, 1:2])


def _combine(x1, gate, y0, y1):
    t = x1.shape[0]
    tm = TM_ROW
    return pl.pallas_call(
        _combine_kernel,
        grid=(t // tm,),
        in_specs=[pl.BlockSpec((tm, D_MODEL), lambda i: (i, 0)),
                  pl.BlockSpec((SUBLANES, tm), lambda i: (0, i)),
                  _rows3_spec(tm, lambda i: (i, 0)), _rows3_spec(tm, lambda i: (i, 0))],
        out_specs=pl.BlockSpec((tm, D_MODEL), lambda i: (i, 0)),
        out_shape=jax.ShapeDtypeStruct((t, D_MODEL), F32),
        compiler_params=pltpu.CompilerParams(dimension_semantics=("parallel",)),
        name="combine",
    )(x1, gate, y0, y1)


def _rope_tables(seq):
    half = HEAD_DIM // 2
    inv_freq = ROPE_THETA ** (-jnp.arange(half, dtype=F32) / half)
    coarse = (jnp.arange(seq // ROPE_SPLIT) * ROPE_SPLIT).astype(F32)[:, None, None] * inv_freq
    fine = jnp.arange(ROPE_SPLIT).astype(F32)[None, :, None] * inv_freq
    cos = (jnp.cos(coarse) * jnp.cos(fine) - jnp.sin(coarse) * jnp.sin(fine)).reshape(seq, half)
    sin = (jnp.sin(coarse) * jnp.cos(fine) + jnp.cos(coarse) * jnp.sin(fine)).reshape(seq, half)
    return jnp.concatenate([cos, cos], axis=-1), jnp.concatenate([-sin, sin], axis=-1)


def _layer(x, p, seq, rope, experts):
    t = x.shape[0]
    ma, sgb, q, k, v = _inproj(x, p, *rope, seq)
    x1, hn, route, gate, counts_f = _attn(x, ma, sgb, q, k, v, p, seq)

    block_rows = _moe_block_rows(t)
    counts = counts_f[:, 0].astype(I32)
    min_blocks = 1 if experts[0].dtype != BF16 else 0
    padded = jnp.maximum((counts + block_rows - 1) // block_rows, min_blocks) * block_rows
    pad_ends = jnp.cumsum(padded)
    pad_starts = pad_ends - padded
    n_blocks = (t * TOP_K) // block_rows + N_EXPERTS
    block_start = jnp.arange(n_blocks, dtype=I32) * block_rows
    in_expert = jnp.logical_and(block_start[:, None] >= pad_starts[None, :],
                                block_start[:, None] < pad_ends[None, :]).astype(I32)
    block_e = jnp.minimum(jnp.sum((block_start[:, None] >= pad_ends[None, :]).astype(I32), axis=1), N_EXPERTS - 1)
    n_valid = jnp.sum(in_expert * jnp.clip(pad_starts + counts - block_start[:, None], 0, block_rows), axis=1)
    n_used = pad_ends[-1:] // block_rows
    starts_col = jnp.broadcast_to(pad_starts.astype(F32)[:, None], (N_EXPERTS, LANES))

    dest = _dest(route, starts_col)
    dests = [dest[k].reshape(t // _sc_chunk(t), _sc_chunk(t)) for k in range(TOP_K)]
    xs = _dispatch(_as_rows3(hn), dests, n_blocks * block_rows)
    yb, experts = _moe(block_e, n_valid, n_used, _as_lines(xs), experts, block_rows)
    y0, y1 = _gather(_as_rows3(yb), dests)
    return _combine(x1, gate, _as_lines(y0), _as_lines(y1)), experts


def kernel(x_prompt, x_sample, norm_mix_g, w_in, norm_v_g, w_spatial, b_spatial, q_norm_g, k_norm_g, sink,
           w_proj_a, w_proj_b, w_out, norm_ffn_g, w_router_group, b_router_group, w_router_expert,
           b_router_expert, w_gate_e, w_up_e, w_down_e):
    depth = w_in.shape[0]
    layers = []
    for l in range(depth):
        w_router = jnp.zeros((ROUTER_ROWS, D_MODEL), F32)
        w_router = w_router.at[:N_GROUPS].set(w_router_group[l].T)
        w_router = w_router.at[SUBLANES:SUBLANES + N_EXPERTS].set(w_router_expert[l].T)
        b_router = jnp.zeros((ROUTER_ROWS,), F32)
        b_router = b_router.at[:N_GROUPS].set(b_router_group[l])
        b_router = b_router.at[SUBLANES:SUBLANES + N_EXPERTS].set(b_router_expert[l])
        b_router = jnp.broadcast_to(b_router[:, None], (ROUTER_ROWS, LANES))
        layers.append(dict(
            norm_mix_g=norm_mix_g[l][None], w_in=w_in[l].astype(BF16), norm_v_g=norm_v_g[l][None],
            w_spatial=w_spatial[l].astype(BF16),
            b_spatial=jnp.broadcast_to(b_spatial[l][:, :, None], (A_GROUPS, CHUNK, LANES)),
            q_norm_g=q_norm_g[l][None], k_norm_g=k_norm_g[l][None], sink=sink[l],
            w_proj_a=w_proj_a[l].astype(BF16), w_proj_b=w_proj_b[l].astype(BF16), w_out=w_out[l].astype(BF16),
            norm_ffn_g=norm_ffn_g[l][None], w_router=w_router.astype(BF16), b_router=b_router,
            experts=(w_gate_e[l], w_up_e[l], w_down_e[l])))

    trunks = [x_prompt, x_sample]
    rows = [x.reshape(-1, D_MODEL) for x in trunks]
    order = sorted(range(len(trunks)), key=lambda n: -rows[n].shape[0])
    rope = _rope_tables(max(x.shape[1] for x in trunks))
    for p in layers:
        experts = p["experts"]
        for n in order:
            rows[n], experts = _layer(rows[n], p, trunks[n].shape[1], rope, experts)
    return tuple(r.reshape(x.shape) for r, x in zip(rows, trunks))
```

```python
import functools

import jax
import jax.numpy as jnp
from jax import lax
from jax.experimental import pallas as pl
from jax.experimental.pallas import tpu as pltpu
from jax.experimental.pallas import tpu_sc as plsc

F32 = jnp.float32
BF16 = jnp.bfloat16
I32 = jnp.int32
U32 = jnp.uint32

LANES = 128
SUBLANES = 8
VMEM_BYTES_V7X = 64 * 1024 * 1024
SC_CORES = 2
SC_SUBCORES = 16
SC_WORKERS = SC_CORES * SC_SUBCORES
SC_CHUNK_MAX = 128

D_MODEL = 1024
A_WIDTH = D_MODEL
A_GROUPS = 8
CHUNK = 128
HEAD_DIM = 128
N_Q_HEADS = D_MODEL // HEAD_DIM
N_KV_HEADS = 2
REP = N_Q_HEADS // N_KV_HEADS
WINDOW = 128
ROPE_THETA = 10000.0
ROPE_SPLIT = 64
Q_W = N_Q_HEADS * HEAD_DIM
KV_W = N_KV_HEADS * HEAD_DIM
IN_W = 2 * A_WIDTH + Q_W + 2 * KV_W + 2 * D_MODEL
COL_U = 0
COL_V = COL_U + A_WIDTH
COL_Q = COL_V + A_WIDTH
COL_K = COL_Q + Q_W
COL_VA = COL_K + KV_W
COL_GA = COL_VA + KV_W
COL_GB = COL_GA + D_MODEL
N_GROUPS = 4
EXPERTS_PER_GROUP = 8
N_EXPERTS = N_GROUPS * EXPERTS_PER_GROUP
TOP_K = 2
EPS = 1e-6
NEG = -1e30

TM_IN = 512
TM_IN_SUB = 256
TQ = 512
DENSE_COLS = 256
ROUTER_ROWS = 64
assert EXPERTS_PER_GROUP == SUBLANES and SUBLANES + N_EXPERTS <= ROUTER_ROWS
TM_ROW = 1024
TM_DEST = 8192
MOE_ROWS_MAX = 1024
MOE_MIN_BLOCKS = 1
VMEM_RESERVE = 8 * 1024 * 1024
VMEM_LIMIT = VMEM_BYTES_V7X - VMEM_RESERVE


def _rms(x, g):
    return x * lax.rsqrt(jnp.mean(x * x, axis=-1, keepdims=True) + EPS) * g


ROW_LINES = D_MODEL // 2 // LANES
ROW3 = (ROW_LINES, LANES)
HIGH_HALF = 0xFFFF0000


def _as_rows3(a):
    return a.reshape((a.shape[0] // ROW_LINES,) + ROW3)


def _as_lines(a):
    return a.reshape((a.shape[0] * ROW_LINES, LANES))


def _store_rows3(lines_ref, val):
    rows = val.shape[0]
    bits = lax.bitcast_convert_type(val.astype(BF16).astype(F32), U32)
    half = ROW_LINES * LANES
    for s in range(ROW_LINES):
        lo = bits[:, s * LANES:(s + 1) * LANES] >> 16
        hi = bits[:, half + s * LANES:half + (s + 1) * LANES] & U32(HIGH_HALF)
        lines_ref[pl.ds(s, rows, stride=ROW_LINES), :] = lax.bitcast_convert_type(lo | hi, I32)


def _load_rows3(lines_ref):
    rows = lines_ref.shape[0] // ROW_LINES
    words = [lax.bitcast_convert_type(lines_ref[pl.ds(s, rows, stride=ROW_LINES), :], U32)
             for s in range(ROW_LINES)]
    lo = [lax.bitcast_convert_type(w << 16, F32) for w in words]
    hi = [lax.bitcast_convert_type(w & U32(HIGH_HALF), F32) for w in words]
    return jnp.concatenate(lo + hi, axis=1)


def _rows3_spec(rows, index_map):
    return pl.BlockSpec((rows * ROW_LINES, LANES), index_map)


def _const_spec(shape):
    nd = len(shape)
    return pl.BlockSpec(shape, lambda *_: (0,) * nd, pipeline_mode=pl.Buffered(1))


def _inproj_kernel(x_ref, gmix_ref, win_ref, gv_ref, ws_ref, bs_ref, gq_ref, gk_ref, cos_ref, sin_ref,
                   wpa_ref, ma_ref, sgb_ref, q_ref, k_ref, v_ref, h_scr, u_scr, vn_scr, a_scr):
    tm = x_ref.shape[0]

    def stages(rows):
        def proj(lo, width):
            return jnp.dot(h_scr[rows], win_ref[:, lo:lo + width], preferred_element_type=F32)

        def norm_rope(z, g):
            zn = _rms(z, g)
            return zn * cos_ref[rows] + pltpu.roll(zn, HEAD_DIM // 2, 1) * sin_ref[rows]

        def norm():
            h_scr[rows] = _rms(x_ref[rows], gmix_ref[...]).astype(BF16)

        def mix_v():
            vn_scr[rows] = _rms(jax.nn.gelu(proj(COL_V, A_WIDTH)), gv_ref[...]).astype(BF16)

        def mix_u():
            u_scr[rows] = jax.nn.gelu(proj(COL_U, A_WIDTH))

        def spatial():
            for c in range(rows.start, rows.stop, CHUNK):
                chunk = slice(c, c + CHUNK)
                for g in range(A_GROUPS):
                    cols = slice(g * LANES, (g + 1) * LANES)
                    mixed = jnp.dot(ws_ref[g], vn_scr[chunk, cols], preferred_element_type=F32) + bs_ref[g]
                    a_scr[chunk, cols] = (u_scr[chunk, cols] * mixed).astype(BF16)

        def gate_a():
            ya = jnp.dot(a_scr[rows], wpa_ref[...], preferred_element_type=F32)
            ma_ref[rows] = jax.nn.sigmoid(proj(COL_GA, D_MODEL)) * ya

        def gate_b():
            sgb_ref[rows] = jax.nn.sigmoid(proj(COL_GB, D_MODEL))

        def queries():
            qz = proj(COL_Q, Q_W)
            for hd in range(N_Q_HEADS):
                cols = slice(hd * HEAD_DIM, (hd + 1) * HEAD_DIM)
                q_ref[rows, cols] = norm_rope(qz[:, cols], gq_ref[...]).astype(BF16)

        def keys_values():
            kz = proj(COL_K, KV_W)
            for hd in range(N_KV_HEADS):
                cols = slice(hd * HEAD_DIM, (hd + 1) * HEAD_DIM)
                k_ref[rows, cols] = norm_rope(kz[:, cols], gk_ref[...]).astype(BF16)
            v_ref[rows] = proj(COL_VA, KV_W).astype(BF16)

        return [norm, mix_v, queries, mix_u, keys_values, spatial, gate_a, gate_b]

    subs = [stages(slice(r, r + TM_IN_SUB)) for r in range(0, tm, TM_IN_SUB)]
    n_stage = len(subs[0])
    for step in range(n_stage + len(subs) - 1):
        for n, sub in enumerate(subs):
            if 0 <= step - n < n_stage:
                sub[step - n]()


def _inproj(x, p, cos, sin, seq):
    t = x.shape[0]
    tm = TM_IN
    n_pos = seq // tm
    row = lambda w: pl.BlockSpec((tm, w), lambda i: (i, 0))
    pos = pl.BlockSpec((tm, HEAD_DIM), lambda i: (i % n_pos, 0))
    return pl.pallas_call(
        _inproj_kernel,
        grid=(t // tm,),
        in_specs=[row(D_MODEL), _const_spec((1, D_MODEL)), _const_spec((D_MODEL, IN_W)),
                  _const_spec((1, A_WIDTH)), _const_spec((A_GROUPS, CHUNK, CHUNK)),
                  _const_spec((A_GROUPS, CHUNK, LANES)), _const_spec((1, HEAD_DIM)),
                  _const_spec((1, HEAD_DIM)), pos, pos, _const_spec((A_WIDTH, D_MODEL))],
        out_specs=[row(D_MODEL), row(D_MODEL), row(Q_W), row(KV_W), row(KV_W)],
        out_shape=[jax.ShapeDtypeStruct((t, D_MODEL), F32), jax.ShapeDtypeStruct((t, D_MODEL), F32),
                   jax.ShapeDtypeStruct((t, Q_W), BF16), jax.ShapeDtypeStruct((t, KV_W), BF16),
                   jax.ShapeDtypeStruct((t, KV_W), BF16)],
        scratch_shapes=[pltpu.VMEM((tm, D_MODEL), BF16), pltpu.VMEM((tm, A_WIDTH), F32),
                        pltpu.VMEM((tm, A_WIDTH), BF16), pltpu.VMEM((tm, A_WIDTH), BF16)],
        compiler_params=pltpu.CompilerParams(dimension_semantics=("parallel",),
                                             vmem_limit_bytes=VMEM_LIMIT),
        name="inproj",
    )(x, p["norm_mix_g"], p["w_in"], p["norm_v_g"], p["w_spatial"], p["b_spatial"], p["q_norm_g"],
      p["k_norm_g"], cos, sin, p["w_proj_a"])


def _attn_kernel(sink_ref, x_ref, ma_ref, sgb_ref, q_ref, kp_ref, kc_ref, kn_ref, vp_ref, vc_ref, vn_ref,
                 wpb_ref, wout_ref, gffn_ref, wr_ref, br_ref,
                 x1_ref, hn_ref, route_ref, gate_ref, counts_ref, kcat, vcat, o_scr, s_scr, p_scr, sink_scr, m_scr,
                 *, tiles_per_seq, n_tiles):
    tq = x_ref.shape[0]
    blk = WINDOW
    i = pl.program_id(0)
    slot = i % 2

    @pl.when(i == 0)
    def _():
        o_scr[...] = jnp.zeros_like(o_scr)
        counts_ref[...] = jnp.zeros_like(counts_ref)

    pos_tile = jnp.minimum(i, n_tiles - 1) % tiles_per_seq
    has_prev = pos_tile > 0
    has_next = pos_tile < tiles_per_seq - 1

    kcat[0:blk] = kp_ref[...]
    kcat[blk:blk + tq] = kc_ref[...]
    kcat[blk + tq:] = kn_ref[...]
    vcat[0:blk] = vp_ref[...]
    vcat[blk:blk + tq] = vc_ref[...]
    vcat[blk + tq:] = vn_ref[...]

    qr = lax.broadcasted_iota(I32, (blk, blk), 0)
    kc = lax.broadcasted_iota(I32, (blk, blk), 1)
    scale = HEAD_DIM ** -0.5
    n_sub = tq // blk
    pairs = [(j, g) for j in range(n_sub) for g in range(N_KV_HEADS)]

    def keys(ref, j, g):
        return ref[j * blk:(j + 3) * blk, g * HEAD_DIM:(g + 1) * HEAD_DIM]

    def head_cols(g, r):
        hd = g * REP + r
        return slice(hd * HEAD_DIM, (hd + 1) * HEAD_DIM)

    for b, (j, g) in enumerate(pairs):
        rows = slice(j * blk, (j + 1) * blk)
        qs = jnp.concatenate([q_ref[rows, head_cols(g, r)] for r in range(REP)], axis=0)
        s_scr[b] = lax.dot_general(qs, keys(kcat, j, g), (((1,), (1,)), ((), ())),
                                   preferred_element_type=F32)
    log2e = 1.4426950408889634

    def softmax(b, r):
        j, g = pairs[b]
        hrows = slice(r * blk, (r + 1) * blk)
        z = s_scr[b, hrows, :] * (scale * log2e)
        lo_ok = kc >= (qr + jnp.where(has_prev, 0, blk) if j == 0 else qr)
        hi_ok = kc <= (qr - jnp.where(has_next, 0, blk) if j == n_sub - 1 else qr)
        z = jnp.concatenate([jnp.where(lo_ok, z[:, :blk], NEG), z[:, blk:2 * blk],
                             jnp.where(hi_ok, z[:, 2 * blk:], NEG)], axis=1)
        sink = sink_ref[g * REP + r] * log2e
        m = jnp.maximum(jnp.max(z, axis=-1, keepdims=True), sink)
        p_scr[b, hrows, :] = jnp.exp2(z - m).astype(BF16)
        sink_scr[b, hrows, :] = jnp.broadcast_to(jnp.exp2(sink - m), (blk, LANES))

    def values(b):
        j, g = pairs[b]
        rows = slice(j * blk, (j + 1) * blk)
        v_ext = jnp.concatenate([keys(vcat, j, g), jnp.ones((3 * blk, HEAD_DIM), BF16)], axis=1)
        acc = jnp.dot(p_scr[b], v_ext, preferred_element_type=F32)
        o = (acc[:, :HEAD_DIM] / (acc[:, HEAD_DIM:] + sink_scr[b])).astype(BF16)
        for r in range(REP):
            o_scr[slot, rows, head_cols(g, r)] = o[r * blk:(r + 1) * blk, :]

    def merged_cols(cols):
        yb = jnp.dot(o_scr[1 - slot], wpb_ref[:, cols], preferred_element_type=F32)
        m_scr[:, cols] = (ma_ref[:, cols] + sgb_ref[:, cols] * yb).astype(BF16)

    def x1_cols(cols):
        x1_ref[:, cols] = x_ref[:, cols] + jnp.dot(m_scr[...], wout_ref[:, cols], preferred_element_type=F32)

    col_chunks = [slice(c * DENSE_COLS, (c + 1) * DENSE_COLS) for c in range(D_MODEL // DENSE_COLS)]
    dense = [functools.partial(f, cols) for f in (merged_cols, x1_cols) for cols in col_chunks]
    units = [(b, r) for b in range(len(pairs)) for r in range(REP)]
    units_per_dense = len(units) // len(dense)
    for n, (b, r) in enumerate(units):
        softmax(b, r)
        if (n + 1) % units_per_dense == 0:
            dense[(n + 1) // units_per_dense - 1]()
        if r == REP - 1:
            values(b)

    hn = _rms(x1_ref[...], gffn_ref[...])
    _store_rows3(hn_ref, hn)
    def wide(a):
        return jnp.concatenate([a] * (tq // LANES), axis=1)

    logits = lax.dot_general(wr_ref[...], hn.astype(BF16), (((1,), (1,)), ((), ())),
                             preferred_element_type=F32) + wide(br_ref[...])
    sub = lax.broadcasted_iota(I32, (SUBLANES, tq), 0).astype(F32)
    ninf = -jnp.inf

    def cmax(a):
        return jnp.max(a, axis=0, keepdims=True)

    def csum(a):
        return jnp.sum(a, axis=0, keepdims=True)

    def first_row(mask):
        return jnp.min(jnp.where(mask, sub, float(SUBLANES)), axis=0, keepdims=True)

    def group_rows(g):
        return logits[(g + 1) * SUBLANES:(g + 2) * SUBLANES]

    gl = jnp.where(sub < N_GROUPS, logits[0:SUBLANES], ninf)
    gmax = cmax(gl)
    g_sel = first_row(gl == gmax)
    g_p = 1.0 / csum(jnp.exp(gl - gmax))
    el = group_rows(0)
    for g in range(1, N_GROUPS):
        el = jnp.where(g_sel == g, group_rows(g), el)
    ee = jnp.exp(el - cmax(el))
    eprob = ee / csum(ee)
    p1 = cmax(eprob)
    i1 = first_row(eprob == p1)
    eprob2 = jnp.where(sub == i1, -1.0, eprob)
    p2 = cmax(eprob2)
    i2 = first_row(eprob2 == p2)
    psum = p1 + p2
    w1 = g_p * p1 / psum
    w2 = g_p * p2 / psum
    e1 = g_sel * EXPERTS_PER_GROUP + i1
    e2 = g_sel * EXPERTS_PER_GROUP + i2

    erow = lax.broadcasted_iota(I32, (N_EXPERTS, tq), 0).astype(F32)
    oh1 = erow == e1
    oh2 = erow == e2
    cnt = (jnp.where(oh1, 1.0, 0.0) + jnp.where(oh2, 1.0, 0.0)) * jnp.where(i > 0, 1.0, 0.0)
    ri = lax.broadcasted_iota(I32, (tq, tq), 0)
    ci = lax.broadcasted_iota(I32, (tq, tq), 1)
    earlier = jnp.where(ri < ci, 1.0, 0.0).astype(BF16)
    base = wide(counts_ref[...]) + jnp.dot(cnt.astype(BF16), earlier, preferred_element_type=F32)
    r1 = csum(jnp.where(oh1, base, 0.0))
    r2 = csum(jnp.where(oh2, base, 0.0))
    counts_ref[...] = counts_ref[...] + jnp.sum(cnt, axis=1, keepdims=True)

    route = jnp.where(sub == 0.0, e1, jnp.where(sub == 1.0, e2,
                      jnp.where(sub == 2.0, r1, jnp.where(sub == 3.0, r2, 0.0))))
    route_ref[...] = route.astype(I32)
    gate_ref[...] = jnp.where(sub == 0.0, w1, jnp.where(sub == 1.0, w2, 0.0))


def _attn(x, ma, sgb, q, k, v, p, seq):
    t = x.shape[0]
    tq = TQ
    sub = tq // WINDOW
    last_blk = t // WINDOW - 1
    n_tiles = t // tq
    att = lambda i: jnp.minimum(i, n_tiles - 1)
    post = lambda i: jnp.maximum(i - 1, 0)
    att_row = lambda w: pl.BlockSpec((tq, w), lambda i: (att(i), 0))
    row = lambda w: pl.BlockSpec((tq, w), lambda i: (post(i), 0))
    prev = pl.BlockSpec((WINDOW, KV_W), lambda i: (jnp.maximum(att(i) * sub - 1, 0), 0))
    nxt = pl.BlockSpec((WINDOW, KV_W), lambda i: (jnp.minimum((att(i) + 1) * sub, last_blk), 0))
    return pl.pallas_call(
        functools.partial(_attn_kernel, tiles_per_seq=seq // tq, n_tiles=n_tiles),
        grid=(n_tiles + 1,),
        in_specs=[pl.BlockSpec(memory_space=pltpu.SMEM),
                  row(D_MODEL), row(D_MODEL), row(D_MODEL), att_row(Q_W),
                  prev, att_row(KV_W), nxt, prev, att_row(KV_W), nxt,
                  _const_spec((Q_W, D_MODEL)), _const_spec((D_MODEL, D_MODEL)), _const_spec((1, D_MODEL)),
                  _const_spec((ROUTER_ROWS, D_MODEL)), _const_spec((ROUTER_ROWS, LANES))],
        out_specs=[row(D_MODEL), _rows3_spec(tq, lambda i: (post(i), 0)),
                   pl.BlockSpec((SUBLANES, tq), lambda i: (0, post(i))),
                   pl.BlockSpec((SUBLANES, tq), lambda i: (0, post(i))),
                   pl.BlockSpec((N_EXPERTS, LANES), lambda i: (0, 0))],
        out_shape=[jax.ShapeDtypeStruct((t, D_MODEL), F32), jax.ShapeDtypeStruct((t * ROW_LINES, LANES), I32),
                   jax.ShapeDtypeStruct((SUBLANES, t), I32), jax.ShapeDtypeStruct((SUBLANES, t), F32),
                   jax.ShapeDtypeStruct((N_EXPERTS, LANES), F32)],
        scratch_shapes=[pltpu.VMEM((tq + 2 * WINDOW, KV_W), BF16), pltpu.VMEM((tq + 2 * WINDOW, KV_W), BF16),
                        pltpu.VMEM((2, tq, Q_W), BF16),
                        pltpu.VMEM((sub * N_KV_HEADS, REP * WINDOW, 3 * WINDOW), F32),
                        pltpu.VMEM((sub * N_KV_HEADS, REP * WINDOW, 3 * WINDOW), BF16),
                        pltpu.VMEM((sub * N_KV_HEADS, REP * WINDOW, LANES), F32),
                        pltpu.VMEM((tq, D_MODEL), BF16)],
        compiler_params=pltpu.CompilerParams(dimension_semantics=("arbitrary",),
                                             vmem_limit_bytes=VMEM_LIMIT),
        name="attn",
    )(p["sink"], x, ma, sgb, q, k, k, k, v, v, v, p["w_proj_b"], p["w_out"], p["norm_ffn_g"],
      p["w_router"], p["b_router"])


def _dest_kernel(route_ref, starts_ref, dest_ref):
    route = route_ref[...].astype(F32)
    td = route.shape[1]
    sub = lax.broadcasted_iota(I32, route.shape, 0)
    erow = lax.broadcasted_iota(I32, (N_EXPERTS, td), 0).astype(F32)
    starts = jnp.broadcast_to(starts_ref[:, 0:1], (N_EXPERTS, td))

    def slot(k):
        start = jnp.sum(jnp.where(erow == route[k:k + 1], starts, 0.0), axis=0, keepdims=True)
        return start + route[TOP_K + k:TOP_K + k + 1]

    dest_ref[...] = jnp.where(sub == 0, slot(0), jnp.where(sub == 1, slot(1), 0.0)).astype(I32)


def _dest(route, pad_starts):
    t = route.shape[1]
    td = min(TM_DEST, t)
    return pl.pallas_call(
        _dest_kernel,
        grid=(t // td,),
        in_specs=[pl.BlockSpec((SUBLANES, td), lambda i: (0, i)), _const_spec((N_EXPERTS, LANES))],
        out_specs=pl.BlockSpec((SUBLANES, td), lambda i: (0, i)),
        out_shape=jax.ShapeDtypeStruct((SUBLANES, t), I32),
        compiler_params=pltpu.CompilerParams(dimension_semantics=("parallel",)),
        name="dest",
    )(route, pad_starts)


def _sc_mesh():
    return plsc.VectorSubcoreMesh(core_axis_name="c", subcore_axis_name="s")


def _sc_worker():
    return lax.axis_index("s") * SC_CORES + lax.axis_index("c")


def _sc_chunk(t):
    return min(SC_CHUNK_MAX, t // (SC_WORKERS * SUBLANES))


def _dispatch(hn, dests, n_rows):
    t = hn.shape[0]
    chunk = dests[0].shape[1]
    per_worker = t // SC_WORKERS
    n_chunks = per_worker // chunk
    idx = pltpu.VMEM((n_chunks, chunk), I32)

    @functools.partial(
        pl.kernel, mesh=_sc_mesh(), out_type=jax.ShapeDtypeStruct((n_rows,) + ROW3, I32),
        scratch_types=[idx, idx, pltpu.VMEM((chunk,) + ROW3, I32), pltpu.SemaphoreType.DMA])
    def scatter_rows(hn_hbm, d0_hbm, d1_hbm, xs_hbm, i0_v, i1_v, rows_v, sem):
        w = _sc_worker()
        pltpu.sync_copy(d0_hbm.at[pl.ds(w * n_chunks, n_chunks)], i0_v)
        pltpu.sync_copy(d1_hbm.at[pl.ds(w * n_chunks, n_chunks)], i1_v)

        @pl.loop(0, n_chunks)
        def _(j):
            pltpu.sync_copy(hn_hbm.at[pl.ds(w * per_worker + j * chunk, chunk)], rows_v)
            copies = [pltpu.make_async_copy(rows_v, xs_hbm.at[i_v.at[j]], sem) for i_v in (i0_v, i1_v)]
            for cp in copies:
                cp.start()
            for cp in copies:
                cp.wait()

    return scatter_rows(hn, *dests)


def _moe_block_rows(t):
    rows = MOE_ROWS_MAX
    while rows > CHUNK and (t * TOP_K) // N_EXPERTS < MOE_MIN_BLOCKS * rows:
        rows //= 2
    return rows


def _moe_kernel(be_ref, nvalid_ref, nused_ref, xs_ref, wg_ref, wu_ref, wd_ref, yb_ref, *w16_refs):
    i = pl.program_id(0)
    used = i < nused_ref[0]
    weights = w16_refs if w16_refs else (wg_ref, wu_ref, wd_ref)

    @pl.when(jnp.logical_not(used))
    def _():
        yb_ref[...] = jnp.zeros_like(yb_ref)

    @pl.when(used)
    def _():
        if w16_refs:
            @pl.when(jnp.logical_or(i == 0, be_ref[i] != be_ref[jnp.maximum(i - 1, 0)]))
            def _():
                for src, dst in zip((wg_ref, wu_ref, wd_ref), w16_refs):
                    dst[0] = src[0].astype(BF16)

        row = lax.broadcasted_iota(I32, (xs_ref.shape[0] // ROW_LINES, 1), 0)
        x = jnp.where(row < nvalid_ref[i], _load_rows3(xs_ref), 0.0).astype(BF16)
        gate = jnp.dot(x, weights[0][0], preferred_element_type=F32)
        up = jnp.dot(x, weights[1][0], preferred_element_type=F32)
        hid = (jax.nn.silu(gate) * up).astype(BF16)
        _store_rows3(yb_ref, jnp.dot(hid, weights[2][0], preferred_element_type=F32))


def _moe(block_e, n_valid, n_used, xs, experts, block_rows):
    n_blocks = xs.shape[0] // (block_rows * ROW_LINES)
    cast = experts[0].dtype != BF16

    def rows(i, be, nv, nu):
        return (jnp.minimum(i, nu[0] - 1), 0)

    def expert(i, be, nv, nu):
        return (be[jnp.minimum(i, nu[0] - 1)], 0, 0)

    w_specs = [pl.BlockSpec((1,) + w.shape[1:], expert) for w in experts]
    out_specs = [_rows3_spec(block_rows, lambda i, be, nv, nu: (i, 0))]
    out_shape = [jax.ShapeDtypeStruct(xs.shape, I32)]
    if cast:
        out_specs += w_specs
        out_shape += [jax.ShapeDtypeStruct(w.shape, BF16) for w in experts]
    yb, *w16 = pl.pallas_call(
        _moe_kernel,
        grid_spec=pltpu.PrefetchScalarGridSpec(
            num_scalar_prefetch=3,
            grid=(n_blocks,),
            in_specs=[_rows3_spec(block_rows, rows)] + w_specs,
            out_specs=out_specs,
        ),
        out_shape=out_shape,
        compiler_params=pltpu.CompilerParams(dimension_semantics=("arbitrary",),
                                             vmem_limit_bytes=VMEM_LIMIT),
        name="moe",
    )(block_e, n_valid, n_used, xs, *experts)
    return yb, (tuple(w16) if cast else experts)


def _gather(yb, dests):
    chunk = dests[0].shape[1]
    t = dests[0].shape[0] * chunk
    per_worker = t // SC_WORKERS
    n_chunks = per_worker // chunk
    idx = pltpu.VMEM((n_chunks, chunk), I32)
    out = jax.ShapeDtypeStruct((t,) + ROW3, I32)

    @functools.partial(
        pl.kernel, mesh=_sc_mesh(), out_type=(out, out),
        scratch_types=[idx, idx, pltpu.VMEM((chunk,) + ROW3, I32), pltpu.SemaphoreType.DMA])
    def gather_rows(yb_hbm, d0_hbm, d1_hbm, y0_hbm, y1_hbm, i0_v, i1_v, rows_v, sem):
        w = _sc_worker()
        pltpu.sync_copy(d0_hbm.at[pl.ds(w * n_chunks, n_chunks)], i0_v)
        pltpu.sync_copy(d1_hbm.at[pl.ds(w * n_chunks, n_chunks)], i1_v)

        @pl.loop(0, n_chunks)
        def _(j):
            rows = pl.ds(w * per_worker + j * chunk, chunk)
            for i_v, y_hbm in ((i0_v, y0_hbm), (i1_v, y1_hbm)):
                pltpu.async_copy(yb_hbm.at[i_v.at[j]], rows_v, sem).wait()
                pltpu.sync_copy(rows_v, y_hbm.at[rows])

    return gather_rows(yb, *dests)


def _combine_kernel(x1_ref, gate_ref, y0_ref, y1_ref, out_ref):
    tm = x1_ref.shape[0]
    gate = jnp.concatenate([gate_ref[...], jnp.zeros((LANES - SUBLANES, tm), F32)], axis=0).T
    out_ref[...] = x1_ref[...] + (_load_rows3(y0_ref) * gate[:, 0:1] + _load_rows3(y1_ref) * gate[:, 1:2])


def _combine(x1, gate, y0, y1):
    t = x1.shape[0]
    tm = TM_ROW
    return pl.pallas_call(
        _combine_kernel,
        grid=(t // tm,),
        in_specs=[pl.BlockSpec((tm, D_MODEL), lambda i: (i, 0)),
                  pl.BlockSpec((SUBLANES, tm), lambda i: (0, i)),
                  _rows3_spec(tm, lambda i: (i, 0)), _rows3_spec(tm, lambda i: (i, 0))],
        out_specs=pl.BlockSpec((tm, D_MODEL), lambda i: (i, 0)),
        out_shape=jax.ShapeDtypeStruct((t, D_MODEL), F32),
        compiler_params=pltpu.CompilerParams(dimension_semantics=("parallel",)),
        name="combine",
    )(x1, gate, y0, y1)


def _rope_tables(seq):
    half = HEAD_DIM // 2
    inv_freq = ROPE_THETA ** (-jnp.arange(half, dtype=F32) / half)
    coarse = (jnp.arange(seq // ROPE_SPLIT) * ROPE_SPLIT).astype(F32)[:, None, None] * inv_freq
    fine = jnp.arange(ROPE_SPLIT).astype(F32)[None, :, None] * inv_freq
    cos = (jnp.cos(coarse) * jnp.cos(fine) - jnp.sin(coarse) * jnp.sin(fine)).reshape(seq, half)
    sin = (jnp.sin(coarse) * jnp.cos(fine) + jnp.cos(coarse) * jnp.sin(fine)).reshape(seq, half)
    return jnp.concatenate([cos, cos], axis=-1), jnp.concatenate([-sin, sin], axis=-1)


def _layer(x, p, seq, rope, experts):
    t = x.shape[0]
    ma, sgb, q, k, v = _inproj(x, p, *rope, seq)
    x1, hn, route, gate, counts_f = _attn(x, ma, sgb, q, k, v, p, seq)

    block_rows = _moe_block_rows(t)
    counts = counts_f[:, 0].astype(I32)
    min_blocks = 1 if experts[0].dtype != BF16 else 0
    padded = jnp.maximum((counts + block_rows - 1) // block_rows, min_blocks) * block_rows
    pad_ends = jnp.cumsum(padded)
    pad_starts = pad_ends - padded
    n_blocks = (t * TOP_K) // block_rows + N_EXPERTS
    block_start = jnp.arange(n_blocks, dtype=I32) * block_rows
    in_expert = jnp.logical_and(block_start[:, None] >= pad_starts[None, :],
                                block_start[:, None] < pad_ends[None, :]).astype(I32)
    block_e = jnp.minimum(jnp.sum((block_start[:, None] >= pad_ends[None, :]).astype(I32), axis=1), N_EXPERTS - 1)
    n_valid = jnp.sum(in_expert * jnp.clip(pad_starts + counts - block_start[:, None], 0, block_rows), axis=1)
    n_used = pad_ends[-1:] // block_rows
    starts_col = jnp.broadcast_to(pad_starts.astype(F32)[:, None], (N_EXPERTS, LANES))

    dest = _dest(route, starts_col)
    dests = [dest[k].reshape(t // _sc_chunk(t), _sc_chunk(t)) for k in range(TOP_K)]
    xs = _dispatch(_as_rows3(hn), dests, n_blocks * block_rows)
    yb, experts = _moe(block_e, n_valid, n_used, _as_lines(xs), experts, block_rows)
    y0, y1 = _gather(_as_rows3(yb), dests)
    return _combine(x1, gate, _as_lines(y0), _as_lines(y1)), experts


def kernel(x_prompt, x_sample, norm_mix_g, w_in, norm_v_g, w_spatial, b_spatial, q_norm_g, k_norm_g, sink,
           w_proj_a, w_proj_b, w_out, norm_ffn_g, w_router_group, b_router_group, w_router_expert,
           b_router_expert, w_gate_e, w_up_e, w_down_e):
    depth = w_in.shape[0]
    layers = []
    for l in range(depth):
        w_router = jnp.zeros((ROUTER_ROWS, D_MODEL), F32)
        w_router = w_router.at[:N_GROUPS].set(w_router_group[l].T)
        w_router = w_router.at[SUBLANES:SUBLANES + N_EXPERTS].set(w_router_expert[l].T)
        b_router = jnp.zeros((ROUTER_ROWS,), F32)
        b_router = b_router.at[:N_GROUPS].set(b_router_group[l])
        b_router = b_router.at[SUBLANES:SUBLANES + N_EXPERTS].set(b_router_expert[l])
        b_router = jnp.broadcast_to(b_router[:, None], (ROUTER_ROWS, LANES))
        layers.append(dict(
            norm_mix_g=norm_mix_g[l][None], w_in=w_in[l].astype(BF16), norm_v_g=norm_v_g[l][None],
            w_spatial=w_spatial[l].astype(BF16),
            b_spatial=jnp.broadcast_to(b_spatial[l][:, :, None], (A_GROUPS, CHUNK, LANES)),
            q_norm_g=q_norm_g[l][None], k_norm_g=k_norm_g[l][None], sink=sink[l],
            w_proj_a=w_proj_a[l].astype(BF16), w_proj_b=w_proj_b[l].astype(BF16), w_out=w_out[l].astype(BF16),
            norm_ffn_g=norm_ffn_g[l][None], w_router=w_router.astype(BF16), b_router=b_router,
            experts=(w_gate_e[l], w_up_e[l], w_down_e[l])))

    trunks = [x_prompt, x_sample]
    rows = [x.reshape(-1, D_MODEL) for x in trunks]
    order = sorted(range(len(trunks)), key=lambda n: -rows[n].shape[0])
    rope = _rope_tables(max(x.shape[1] for x in trunks))
    for p in layers:
        experts = p["experts"]
        for n in order:
            rows[n], experts = _layer(rows[n], p, trunks[n].shape[1], rope, experts)
    return tuple(r.reshape(x.shape) for r, x in zip(rows, trunks))
```

```python
import functools

import jax
import jax.numpy as jnp
from jax import lax
from jax.experimental import pallas as pl
from jax.experimental.pallas import tpu as pltpu
from jax.experimental.pallas import tpu_sc as plsc

F32 = jnp.float32
BF16 = jnp.bfloat16
I32 = jnp.int32
U32 = jnp.uint32

LANES = 128
SUBLANES = 8
VMEM_BYTES_V7X = 64 * 1024 * 1024
SC_CORES = 2
SC_SUBCORES = 16
SC_WORKERS = SC_CORES * SC_SUBCORES
SC_CHUNK_MAX = 128

D_MODEL = 1024
A_WIDTH = D_MODEL
A_GROUPS = 8
CHUNK = 128
HEAD_DIM = 128
N_Q_HEADS = D_MODEL // HEAD_DIM
N_KV_HEADS = 2
REP = N_Q_HEADS // N_KV_HEADS
WINDOW = 128
ROPE_THETA = 10000.0
ROPE_SPLIT = 64
Q_W = N_Q_HEADS * HEAD_DIM
KV_W = N_KV_HEADS * HEAD_DIM
IN_W = 2 * A_WIDTH + Q_W + 2 * KV_W + 2 * D_MODEL
COL_U = 0
COL_V = COL_U + A_WIDTH
COL_Q = COL_V + A_WIDTH
COL_K = COL_Q + Q_W
COL_VA = COL_K + KV_W
COL_GA = COL_VA + KV_W
COL_GB = COL_GA + D_MODEL
N_GROUPS = 4
EXPERTS_PER_GROUP = 8
N_EXPERTS = N_GROUPS * EXPERTS_PER_GROUP
TOP_K = 2
EPS = 1e-6
NEG = -1e30

TM_IN = 512
TM_IN_SUB = 256
TQ = 512
DENSE_COLS = 256
ROUTER_ROWS = 64
assert EXPERTS_PER_GROUP == SUBLANES and SUBLANES + N_EXPERTS <= ROUTER_ROWS
TM_ROW = 1024
TM_DEST = 8192
MOE_ROWS_MAX = 1024
MOE_MIN_BLOCKS = 1
VMEM_RESERVE = 8 * 1024 * 1024
VMEM_LIMIT = VMEM_BYTES_V7X - VMEM_RESERVE


def _rms(x, g):
    return x * lax.rsqrt(jnp.mean(x * x, axis=-1, keepdims=True) + EPS) * g


ROW_LINES = D_MODEL // 2 // LANES
ROW3 = (ROW_LINES, LANES)
HIGH_HALF = 0xFFFF0000


def _as_rows3(a):
    return a.reshape((a.shape[0] // ROW_LINES,) + ROW3)


def _as_lines(a):
    return a.reshape((a.shape[0] * ROW_LINES, LANES))


def _store_rows3(lines_ref, val):
    rows = val.shape[0]
    bits = lax.bitcast_convert_type(val.astype(BF16).astype(F32), U32)
    half = ROW_LINES * LANES
    for s in range(ROW_LINES):
        lo = bits[:, s * LANES:(s + 1) * LANES] >> 16
        hi = bits[:, half + s * LANES:half + (s + 1) * LANES] & U32(HIGH_HALF)
        lines_ref[pl.ds(s, rows, stride=ROW_LINES), :] = lax.bitcast_convert_type(lo | hi, I32)


def _load_rows3(lines_ref):
    rows = lines_ref.shape[0] // ROW_LINES
    words = [lax.bitcast_convert_type(lines_ref[pl.ds(s, rows, stride=ROW_LINES), :], U32)
             for s in range(ROW_LINES)]
    lo = [lax.bitcast_convert_type(w << 16, F32) for w in words]
    hi = [lax.bitcast_convert_type(w & U32(HIGH_HALF), F32) for w in words]
    return jnp.concatenate(lo + hi, axis=1)


def _rows3_spec(rows, index_map):
    return pl.BlockSpec((rows * ROW_LINES, LANES), index_map)


def _const_spec(shape):
    nd = len(shape)
    return pl.BlockSpec(shape, lambda *_: (0,) * nd, pipeline_mode=pl.Buffered(1))


def _inproj_kernel(x_ref, gmix_ref, win_ref, gv_ref, ws_ref, bs_ref, gq_ref, gk_ref, cos_ref, sin_ref,
                   wpa_ref, ma_ref, sgb_ref, q_ref, k_ref, v_ref, h_scr, u_scr, vn_scr, a_scr):
    tm = x_ref.shape[0]

    def stages(rows):
        def proj(lo, width):
            return jnp.dot(h_scr[rows], win_ref[:, lo:lo + width], preferred_element_type=F32)

        def norm_rope(z, g):
            zn = _rms(z, g)
            return zn * cos_ref[rows] + pltpu.roll(zn, HEAD_DIM // 2, 1) * sin_ref[rows]

        def norm():
            h_scr[rows] = _rms(x_ref[rows], gmix_ref[...]).astype(BF16)

        def mix_v():
            vn_scr[rows] = _rms(jax.nn.gelu(proj(COL_V, A_WIDTH)), gv_ref[...]).astype(BF16)

        def mix_u():
            u_scr[rows] = jax.nn.gelu(proj(COL_U, A_WIDTH))

        def spatial():
            for c in range(rows.start, rows.stop, CHUNK):
                chunk = slice(c, c + CHUNK)
                for g in range(A_GROUPS):
                    cols = slice(g * LANES, (g + 1) * LANES)
                    mixed = jnp.dot(ws_ref[g], vn_scr[chunk, cols], preferred_element_type=F32) + bs_ref[g]
                    a_scr[chunk, cols] = (u_scr[chunk, cols] * mixed).astype(BF16)

        def gate_a():
            ya = jnp.dot(a_scr[rows], wpa_ref[...], preferred_element_type=F32)
            ma_ref[rows] = jax.nn.sigmoid(proj(COL_GA, D_MODEL)) * ya

        def gate_b():
            sgb_ref[rows] = jax.nn.sigmoid(proj(COL_GB, D_MODEL))

        def queries():
            qz = proj(COL_Q, Q_W)
            for hd in range(N_Q_HEADS):
                cols = slice(hd * HEAD_DIM, (hd + 1) * HEAD_DIM)
                q_ref[rows, cols] = norm_rope(qz[:, cols], gq_ref[...]).astype(BF16)

        def keys_values():
            kz = proj(COL_K, KV_W)
            for hd in range(N_KV_HEADS):
                cols = slice(hd * HEAD_DIM, (hd + 1) * HEAD_DIM)
                k_ref[rows, cols] = norm_rope(kz[:, cols], gk_ref[...]).astype(BF16)
            v_ref[rows] = proj(COL_VA, KV_W).astype(BF16)

        return [norm, mix_v, queries, mix_u, gate_b, keys_values, spatial, gate_a]

    subs = [stages(slice(r, r + TM_IN_SUB)) for r in range(0, tm, TM_IN_SUB)]
    n_stage = len(subs[0])
    for step in range(n_stage + len(subs) - 1):
        for n, sub in enumerate(subs):
            if 0 <= step - n < n_stage:
                sub[step - n]()


def _inproj(x, p, cos, sin, seq):
    t = x.shape[0]
    tm = TM_IN
    n_pos = seq // tm
    row = lambda w: pl.BlockSpec((tm, w), lambda i: (i, 0))
    pos = pl.BlockSpec((tm, HEAD_DIM), lambda i: (i % n_pos, 0))
    return pl.pallas_call(
        _inproj_kernel,
        grid=(t // tm,),
        in_specs=[row(D_MODEL), _const_spec((1, D_MODEL)), _const_spec((D_MODEL, IN_W)),
                  _const_spec((1, A_WIDTH)), _const_spec((A_GROUPS, CHUNK, CHUNK)),
                  _const_spec((A_GROUPS, CHUNK, LANES)), _const_spec((1, HEAD_DIM)),
                  _const_spec((1, HEAD_DIM)), pos, pos, _const_spec((A_WIDTH, D_MODEL))],
        out_specs=[row(D_MODEL), row(D_MODEL), row(Q_W), row(KV_W), row(KV_W)],
        out_shape=[jax.ShapeDtypeStruct((t, D_MODEL), F32), jax.ShapeDtypeStruct((t, D_MODEL), F32),
                   jax.ShapeDtypeStruct((t, Q_W), BF16), jax.ShapeDtypeStruct((t, KV_W), BF16),
                   jax.ShapeDtypeStruct((t, KV_W), BF16)],
        scratch_shapes=[pltpu.VMEM((tm, D_MODEL), BF16), pltpu.VMEM((tm, A_WIDTH), F32),
                        pltpu.VMEM((tm, A_WIDTH), BF16), pltpu.VMEM((tm, A_WIDTH), BF16)],
        compiler_params=pltpu.CompilerParams(dimension_semantics=("parallel",),
                                             vmem_limit_bytes=VMEM_LIMIT),
        name="inproj",
    )(x, p["norm_mix_g"], p["w_in"], p["norm_v_g"], p["w_spatial"], p["b_spatial"], p["q_norm_g"],
      p["k_norm_g"], cos, sin, p["w_proj_a"])


def _attn_kernel(sink_ref, x_ref, ma_ref, sgb_ref, q_ref, kp_ref, kc_ref, kn_ref, vp_ref, vc_ref, vn_ref,
                 wpb_ref, wout_ref, gffn_ref, wr_ref, br_ref,
                 x1_ref, hn_ref, route_ref, gate_ref, counts_ref, kcat, vcat, o_scr, s_scr, p_scr, sink_scr, m_scr,
                 *, tiles_per_seq, n_tiles):
    tq = x_ref.shape[0]
    blk = WINDOW
    i = pl.program_id(0)
    slot = i % 2

    @pl.when(i == 0)
    def _():
        o_scr[...] = jnp.zeros_like(o_scr)
        counts_ref[...] = jnp.zeros_like(counts_ref)

    pos_tile = jnp.minimum(i, n_tiles - 1) % tiles_per_seq
    has_prev = pos_tile > 0
    has_next = pos_tile < tiles_per_seq - 1

    kcat[0:blk] = kp_ref[...]
    kcat[blk:blk + tq] = kc_ref[...]
    kcat[blk + tq:] = kn_ref[...]
    vcat[0:blk] = vp_ref[...]
    vcat[blk:blk + tq] = vc_ref[...]
    vcat[blk + tq:] = vn_ref[...]

    qr = lax.broadcasted_iota(I32, (blk, blk), 0)
    kc = lax.broadcasted_iota(I32, (blk, blk), 1)
    scale = HEAD_DIM ** -0.5
    n_sub = tq // blk
    pairs = [(j, g) for j in range(n_sub) for g in range(N_KV_HEADS)]

    def keys(ref, j, g):
        return ref[j * blk:(j + 3) * blk, g * HEAD_DIM:(g + 1) * HEAD_DIM]

    def head_cols(g, r):
        hd = g * REP + r
        return slice(hd * HEAD_DIM, (hd + 1) * HEAD_DIM)

    for b, (j, g) in enumerate(pairs):
        rows = slice(j * blk, (j + 1) * blk)
        qs = jnp.concatenate([q_ref[rows, head_cols(g, r)] for r in range(REP)], axis=0)
        s_scr[b] = lax.dot_general(qs, keys(kcat, j, g), (((1,), (1,)), ((), ())),
                                   preferred_element_type=F32)
    log2e = 1.4426950408889634

    def softmax(b, r):
        j, g = pairs[b]
        hrows = slice(r * blk, (r + 1) * blk)
        z = s_scr[b, hrows, :] * (scale * log2e)
        lo_ok = kc >= (qr + jnp.where(has_prev, 0, blk) if j == 0 else qr)
        hi_ok = kc <= (qr - jnp.where(has_next, 0, blk) if j == n_sub - 1 else qr)
        z = jnp.concatenate([jnp.where(lo_ok, z[:, :blk], NEG), z[:, blk:2 * blk],
                             jnp.where(hi_ok, z[:, 2 * blk:], NEG)], axis=1)
        sink = sink_ref[g * REP + r] * log2e
        m = jnp.maximum(jnp.max(z, axis=-1, keepdims=True), sink)
        p_scr[b, hrows, :] = jnp.exp2(z - m).astype(BF16)
        sink_scr[b, hrows, :] = jnp.broadcast_to(jnp.exp2(sink - m), (blk, LANES))

    def values(b):
        j, g = pairs[b]
        rows = slice(j * blk, (j + 1) * blk)
        v_ext = jnp.concatenate([keys(vcat, j, g), jnp.ones((3 * blk, HEAD_DIM), BF16)], axis=1)
        acc = jnp.dot(p_scr[b], v_ext, preferred_element_type=F32)
        o = (acc[:, :HEAD_DIM] / (acc[:, HEAD_DIM:] + sink_scr[b])).astype(BF16)
        for r in range(REP):
            o_scr[slot, rows, head_cols(g, r)] = o[r * blk:(r + 1) * blk, :]

    def merged_cols(cols):
        yb = jnp.dot(o_scr[1 - slot], wpb_ref[:, cols], preferred_element_type=F32)
        m_scr[:, cols] = (ma_ref[:, cols] + sgb_ref[:, cols] * yb).astype(BF16)

    def x1_cols(cols):
        x1_ref[:, cols] = x_ref[:, cols] + jnp.dot(m_scr[...], wout_ref[:, cols], preferred_element_type=F32)

    col_chunks = [slice(c * DENSE_COLS, (c + 1) * DENSE_COLS) for c in range(D_MODEL // DENSE_COLS)]
    dense = [functools.partial(f, cols) for f in (merged_cols, x1_cols) for cols in col_chunks]
    units = [(b, r) for b in range(len(pairs)) for r in range(REP)]
    units_per_dense = len(units) // len(dense)
    for n, (b, r) in enumerate(units):
        softmax(b, r)
        if (n + 1) % units_per_dense == 0:
            dense[(n + 1) // units_per_dense - 1]()
        if r == REP - 1:
            values(b)

    hn = _rms(x1_ref[...], gffn_ref[...])
    _store_rows3(hn_ref, hn)
    def wide(a):
        return jnp.concatenate([a] * (tq // LANES), axis=1)

    logits = lax.dot_general(wr_ref[...], hn.astype(BF16), (((1,), (1,)), ((), ())),
                             preferred_element_type=F32) + wide(br_ref[...])
    sub = lax.broadcasted_iota(I32, (SUBLANES, tq), 0).astype(F32)
    ninf = -jnp.inf

    def cmax(a):
        return jnp.max(a, axis=0, keepdims=True)

    def csum(a):
        return jnp.sum(a, axis=0, keepdims=True)

    def first_row(mask):
        return jnp.min(jnp.where(mask, sub, float(SUBLANES)), axis=0, keepdims=True)

    def group_rows(g):
        return logits[(g + 1) * SUBLANES:(g + 2) * SUBLANES]

    gl = jnp.where(sub < N_GROUPS, logits[0:SUBLANES], ninf)
    gmax = cmax(gl)
    g_sel = first_row(gl == gmax)
    g_p = 1.0 / csum(jnp.exp(gl - gmax))
    el = group_rows(0)
    for g in range(1, N_GROUPS):
        el = jnp.where(g_sel == g, group_rows(g), el)
    ee = jnp.exp(el - cmax(el))
    eprob = ee / csum(ee)
    p1 = cmax(eprob)
    i1 = first_row(eprob == p1)
    eprob2 = jnp.where(sub == i1, -1.0, eprob)
    p2 = cmax(eprob2)
    i2 = first_row(eprob2 == p2)
    psum = p1 + p2
    w1 = g_p * p1 / psum
    w2 = g_p * p2 / psum
    e1 = g_sel * EXPERTS_PER_GROUP + i1
    e2 = g_sel * EXPERTS_PER_GROUP + i2

    erow = lax.broadcasted_iota(I32, (N_EXPERTS, tq), 0).astype(F32)
    oh1 = erow == e1
    oh2 = erow == e2
    cnt = (jnp.where(oh1, 1.0, 0.0) + jnp.where(oh2, 1.0, 0.0)) * jnp.where(i > 0, 1.0, 0.0)
    ri = lax.broadcasted_iota(I32, (tq, tq), 0)
    ci = lax.broadcasted_iota(I32, (tq, tq), 1)
    earlier = jnp.where(ri < ci, 1.0, 0.0).astype(BF16)
    base = wide(counts_ref[...]) + jnp.dot(cnt.astype(BF16), earlier, preferred_element_type=F32)
    r1 = csum(jnp.where(oh1, base, 0.0))
    r2 = csum(jnp.where(oh2, base, 0.0))
    counts_ref[...] = counts_ref[...] + jnp.sum(cnt, axis=1, keepdims=True)

    route = jnp.where(sub == 0.0, e1, jnp.where(sub == 1.0, e2,
                      jnp.where(sub == 2.0, r1, jnp.where(sub == 3.0, r2, 0.0))))
    route_ref[...] = route.astype(I32)
    gate_ref[...] = jnp.where(sub == 0.0, w1, jnp.where(sub == 1.0, w2, 0.0))


def _attn(x, ma, sgb, q, k, v, p, seq):
    t = x.shape[0]
    tq = TQ
    sub = tq // WINDOW
    last_blk = t // WINDOW - 1
    n_tiles = t // tq
    att = lambda i: jnp.minimum(i, n_tiles - 1)
    post = lambda i: jnp.maximum(i - 1, 0)
    att_row = lambda w: pl.BlockSpec((tq, w), lambda i: (att(i), 0))
    row = lambda w: pl.BlockSpec((tq, w), lambda i: (post(i), 0))
    prev = pl.BlockSpec((WINDOW, KV_W), lambda i: (jnp.maximum(att(i) * sub - 1, 0), 0))
    nxt = pl.BlockSpec((WINDOW, KV_W), lambda i: (jnp.minimum((att(i) + 1) * sub, last_blk), 0))
    return pl.pallas_call(
        functools.partial(_attn_kernel, tiles_per_seq=seq // tq, n_tiles=n_tiles),
        grid=(n_tiles + 1,),
        in_specs=[pl.BlockSpec(memory_space=pltpu.SMEM),
                  row(D_MODEL), row(D_MODEL), row(D_MODEL), att_row(Q_W),
                  prev, att_row(KV_W), nxt, prev, att_row(KV_W), nxt,
                  _const_spec((Q_W, D_MODEL)), _const_spec((D_MODEL, D_MODEL)), _const_spec((1, D_MODEL)),
                  _const_spec((ROUTER_ROWS, D_MODEL)), _const_spec((ROUTER_ROWS, LANES))],
        out_specs=[row(D_MODEL), _rows3_spec(tq, lambda i: (post(i), 0)),
                   pl.BlockSpec((SUBLANES, tq), lambda i: (0, post(i))),
                   pl.BlockSpec((SUBLANES, tq), lambda i: (0, post(i))),
                   pl.BlockSpec((N_EXPERTS, LANES), lambda i: (0, 0))],
        out_shape=[jax.ShapeDtypeStruct((t, D_MODEL), F32), jax.ShapeDtypeStruct((t * ROW_LINES, LANES), I32),
                   jax.ShapeDtypeStruct((SUBLANES, t), I32), jax.ShapeDtypeStruct((SUBLANES, t), F32),
                   jax.ShapeDtypeStruct((N_EXPERTS, LANES), F32)],
        scratch_shapes=[pltpu.VMEM((tq + 2 * WINDOW, KV_W), BF16), pltpu.VMEM((tq + 2 * WINDOW, KV_W), BF16),
                        pltpu.VMEM((2, tq, Q_W), BF16),
                        pltpu.VMEM((sub * N_KV_HEADS, REP * WINDOW, 3 * WINDOW), F32),
                        pltpu.VMEM((sub * N_KV_HEADS, REP * WINDOW, 3 * WINDOW), BF16),
                        pltpu.VMEM((sub * N_KV_HEADS, REP * WINDOW, LANES), F32),
                        pltpu.VMEM((tq, D_MODEL), BF16)],
        compiler_params=pltpu.CompilerParams(dimension_semantics=("arbitrary",),
                                             vmem_limit_bytes=VMEM_LIMIT),
        name="attn",
    )(p["sink"], x, ma, sgb, q, k, k, k, v, v, v, p["w_proj_b"], p["w_out"], p["norm_ffn_g"],
      p["w_router"], p["b_router"])


def _dest_kernel(route_ref, starts_ref, dest_ref):
    route = route_ref[...].astype(F32)
    td = route.shape[1]
    sub = lax.broadcasted_iota(I32, route.shape, 0)
    erow = lax.broadcasted_iota(I32, (N_EXPERTS, td), 0).astype(F32)
    starts = jnp.broadcast_to(starts_ref[:, 0:1], (N_EXPERTS, td))

    def slot(k):
        start = jnp.sum(jnp.where(erow == route[k:k + 1], starts, 0.0), axis=0, keepdims=True)
        return start + route[TOP_K + k:TOP_K + k + 1]

    dest_ref[...] = jnp.where(sub == 0, slot(0), jnp.where(sub == 1, slot(1), 0.0)).astype(I32)


def _dest(route, pad_starts):
    t = route.shape[1]
    td = min(TM_DEST, t)
    return pl.pallas_call(
        _dest_kernel,
        grid=(t // td,),
        in_specs=[pl.BlockSpec((SUBLANES, td), lambda i: (0, i)), _const_spec((N_EXPERTS, LANES))],
        out_specs=pl.BlockSpec((SUBLANES, td), lambda i: (0, i)),
        out_shape=jax.ShapeDtypeStruct((SUBLANES, t), I32),
        compiler_params=pltpu.CompilerParams(dimension_semantics=("parallel",)),
        name="dest",
    )(route, pad_starts)


def _sc_mesh():
    return plsc.VectorSubcoreMesh(core_axis_name="c", subcore_axis_name="s")


def _sc_worker():
    return lax.axis_index("s") * SC_CORES + lax.axis_index("c")


def _sc_chunk(t):
    return min(SC_CHUNK_MAX, t // (SC_WORKERS * SUBLANES))


def _dispatch(hn, dests, n_rows):
    t = hn.shape[0]
    chunk = dests[0].shape[1]
    per_worker = t // SC_WORKERS
    n_chunks = per_worker // chunk
    idx = pltpu.VMEM((n_chunks, chunk), I32)

    @functools.partial(
        pl.kernel, mesh=_sc_mesh(), out_type=jax.ShapeDtypeStruct((n_rows,) + ROW3, I32),
        scratch_types=[idx, idx, pltpu.VMEM((chunk,) + ROW3, I32), pltpu.SemaphoreType.DMA])
    def scatter_rows(hn_hbm, d0_hbm, d1_hbm, xs_hbm, i0_v, i1_v, rows_v, sem):
        w = _sc_worker()
        pltpu.sync_copy(d0_hbm.at[pl.ds(w * n_chunks, n_chunks)], i0_v)
        pltpu.sync_copy(d1_hbm.at[pl.ds(w * n_chunks, n_chunks)], i1_v)

        @pl.loop(0, n_chunks)
        def _(j):
            pltpu.sync_copy(hn_hbm.at[pl.ds(w * per_worker + j * chunk, chunk)], rows_v)
            copies = [pltpu.make_async_copy(rows_v, xs_hbm.at[i_v.at[j]], sem) for i_v in (i0_v, i1_v)]
            for cp in copies:
                cp.start()
            for cp in copies:
                cp.wait()

    return scatter_rows(hn, *dests)


def _moe_block_rows(t):
    rows = MOE_ROWS_MAX
    while rows > CHUNK and (t * TOP_K) // N_EXPERTS < MOE_MIN_BLOCKS * rows:
        rows //= 2
    return rows


def _moe_kernel(be_ref, nvalid_ref, nused_ref, xs_ref, wg_ref, wu_ref, wd_ref, yb_ref, *w16_refs):
    i = pl.program_id(0)
    used = i < nused_ref[0]
    weights = w16_refs if w16_refs else (wg_ref, wu_ref, wd_ref)

    @pl.when(jnp.logical_not(used))
    def _():
        yb_ref[...] = jnp.zeros_like(yb_ref)

    @pl.when(used)
    def _():
        if w16_refs:
            @pl.when(jnp.logical_or(i == 0, be_ref[i] != be_ref[jnp.maximum(i - 1, 0)]))
            def _():
                for src, dst in zip((wg_ref, wu_ref, wd_ref), w16_refs):
                    dst[0] = src[0].astype(BF16)

        row = lax.broadcasted_iota(I32, (xs_ref.shape[0] // ROW_LINES, 1), 0)
        x = jnp.where(row < nvalid_ref[i], _load_rows3(xs_ref), 0.0).astype(BF16)
        gate = jnp.dot(x, weights[0][0], preferred_element_type=F32)
        up = jnp.dot(x, weights[1][0], preferred_element_type=F32)
        hid = (jax.nn.silu(gate) * up).astype(BF16)
        _store_rows3(yb_ref, jnp.dot(hid, weights[2][0], preferred_element_type=F32))


def _moe(block_e, n_valid, n_used, xs, experts, block_rows):
    n_blocks = xs.shape[0] // (block_rows * ROW_LINES)
    cast = experts[0].dtype != BF16

    def rows(i, be, nv, nu):
        return (jnp.minimum(i, nu[0] - 1), 0)

    def expert(i, be, nv, nu):
        return (be[jnp.minimum(i, nu[0] - 1)], 0, 0)

    w_specs = [pl.BlockSpec((1,) + w.shape[1:], expert) for w in experts]
    out_specs = [_rows3_spec(block_rows, lambda i, be, nv, nu: (i, 0))]
    out_shape = [jax.ShapeDtypeStruct(xs.shape, I32)]
    if cast:
        out_specs += w_specs
        out_shape += [jax.ShapeDtypeStruct(w.shape, BF16) for w in experts]
    yb, *w16 = pl.pallas_call(
        _moe_kernel,
        grid_spec=pltpu.PrefetchScalarGridSpec(
            num_scalar_prefetch=3,
            grid=(n_blocks,),
            in_specs=[_rows3_spec(block_rows, rows)] + w_specs,
            out_specs=out_specs,
        ),
        out_shape=out_shape,
        compiler_params=pltpu.CompilerParams(dimension_semantics=("arbitrary",),
                                             vmem_limit_bytes=VMEM_LIMIT),
        name="moe",
    )(block_e, n_valid, n_used, xs, *experts)
    return yb, (tuple(w16) if cast else experts)


def _gather(yb, dests):
    chunk = dests[0].shape[1]
    t = dests[0].shape[0] * chunk
    per_worker = t // SC_WORKERS
    n_chunks = per_worker // chunk
    idx = pltpu.VMEM((n_chunks, chunk), I32)
    out = jax.ShapeDtypeStruct((t,) + ROW3, I32)

    @functools.partial(
        pl.kernel, mesh=_sc_mesh(), out_type=(out, out),
        scratch_types=[idx, idx, pltpu.VMEM((chunk,) + ROW3, I32), pltpu.SemaphoreType.DMA])
    def gather_rows(yb_hbm, d0_hbm, d1_hbm, y0_hbm, y1_hbm, i0_v, i1_v, rows_v, sem):
        w = _sc_worker()
        pltpu.sync_copy(d0_hbm.at[pl.ds(w * n_chunks, n_chunks)], i0_v)
        pltpu.sync_copy(d1_hbm.at[pl.ds(w * n_chunks, n_chunks)], i1_v)

        @pl.loop(0, n_chunks)
        def _(j):
            rows = pl.ds(w * per_worker + j * chunk, chunk)
            for i_v, y_hbm in ((i0_v, y0_hbm), (i1_v, y1_hbm)):
                pltpu.async_copy(yb_hbm.at[i_v.at[j]], rows_v, sem).wait()
                pltpu.sync_copy(rows_v, y_hbm.at[rows])

    return gather_rows(yb, *dests)


def _combine_kernel(x1_ref, gate_ref, y0_ref, y1_ref, out_ref):
    tm = x1_ref.shape[0]
    gate = jnp.concatenate([gate_ref[...], jnp.zeros((LANES - SUBLANES, tm), F32)], axis=0).T
    out_ref[...] = x1_ref[...] + (_load_rows3(y0_ref) * gate[:, 0:1] + _load_rows3(y1_ref) * gate[:, 1:2])


def _combine(x1, gate, y0, y1):
    t = x1.shape[0]
    tm = TM_ROW
    return pl.pallas_call(
        _combine_kernel,
        grid=(t // tm,),
        in_specs=[pl.BlockSpec((tm, D_MODEL), lambda i: (i, 0)),
                  pl.BlockSpec((SUBLANES, tm), lambda i: (0, i)),
                  _rows3_spec(tm, lambda i: (i, 0)), _rows3_spec(tm, lambda i: (i, 0))],
        out_specs=pl.BlockSpec((tm, D_MODEL), lambda i: (i, 0)),
        out_shape=jax.ShapeDtypeStruct((t, D_MODEL), F32),
        compiler_params=pltpu.CompilerParams(dimension_semantics=("parallel",)),
        name="combine",
    )(x1, gate, y0, y1)


def _rope_tables(seq):
    half = HEAD_DIM // 2
    inv_freq = ROPE_THETA ** (-jnp.arange(half, dtype=F32) / half)
    coarse = (jnp.arange(seq // ROPE_SPLIT) * ROPE_SPLIT).astype(F32)[:, None, None] * inv_freq
    fine = jnp.arange(ROPE_SPLIT).astype(F32)[None, :, None] * inv_freq
    cos = (jnp.cos(coarse) * jnp.cos(fine) - jnp.sin(coarse) * jnp.sin(fine)).reshape(seq, half)
    sin = (jnp.sin(coarse) * jnp.cos(fine) + jnp.cos(coarse) * jnp.sin(fine)).reshape(seq, half)
    return jnp.concatenate([cos, cos], axis=-1), jnp.concatenate([-sin, sin], axis=-1)


def _layer(x, p, seq, rope, experts):
    t = x.shape[0]
    ma, sgb, q, k, v = _inproj(x, p, *rope, seq)
    x1, hn, route, gate, counts_f = _attn(x, ma, sgb, q, k, v, p, seq)

    block_rows = _moe_block_rows(t)
    counts = counts_f[:, 0].astype(I32)
    min_blocks = 1 if experts[0].dtype != BF16 else 0
    padded = jnp.maximum((counts + block_rows - 1) // block_rows, min_blocks) * block_rows
    pad_ends = jnp.cumsum(padded)
    pad_starts = pad_ends - padded
    n_blocks = (t * TOP_K) // block_rows + N_EXPERTS
    block_start = jnp.arange(n_blocks, dtype=I32) * block_rows
    in_expert = jnp.logical_and(block_start[:, None] >= pad_starts[None, :],
                                block_start[:, None] < pad_ends[None, :]).astype(I32)
    block_e = jnp.minimum(jnp.sum((block_start[:, None] >= pad_ends[None, :]).astype(I32), axis=1), N_EXPERTS - 1)
    n_valid = jnp.sum(in_expert * jnp.clip(pad_starts + counts - block_start[:, None], 0, block_rows), axis=1)
    n_used = pad_ends[-1:] // block_rows
    starts_col = jnp.broadcast_to(pad_starts.astype(F32)[:, None], (N_EXPERTS, LANES))

    dest = _dest(route, starts_col)
    dests = [dest[k].reshape(t // _sc_chunk(t), _sc_chunk(t)) for k in range(TOP_K)]
    xs = _dispatch(_as_rows3(hn), dests, n_blocks * block_rows)
    yb, experts = _moe(block_e, n_valid, n_used, _as_lines(xs), experts, block_rows)
    y0, y1 = _gather(_as_rows3(yb), dests)
    return _combine(x1, gate, _as_lines(y0), _as_lines(y1)), experts


def kernel(x_prompt, x_sample, norm_mix_g, w_in, norm_v_g, w_spatial, b_spatial, q_norm_g, k_norm_g, sink,
           w_proj_a, w_proj_b, w_out, norm_ffn_g, w_router_group, b_router_group, w_router_expert,
           b_router_expert, w_gate_e, w_up_e, w_down_e):
    depth = w_in.shape[0]
    layers = []
    for l in range(depth):
        w_router = jnp.zeros((ROUTER_ROWS, D_MODEL), F32)
        w_router = w_router.at[:N_GROUPS].set(w_router_group[l].T)
        w_router = w_router.at[SUBLANES:SUBLANES + N_EXPERTS].set(w_router_expert[l].T)
        b_router = jnp.zeros((ROUTER_ROWS,), F32)
        b_router = b_router.at[:N_GROUPS].set(b_router_group[l])
        b_router = b_router.at[SUBLANES:SUBLANES + N_EXPERTS].set(b_router_expert[l])
        b_router = jnp.broadcast_to(b_router[:, None], (ROUTER_ROWS, LANES))
        layers.append(dict(
            norm_mix_g=norm_mix_g[l][None], w_in=w_in[l].astype(BF16), norm_v_g=norm_v_g[l][None],
            w_spatial=w_spatial[l].astype(BF16),
            b_spatial=jnp.broadcast_to(b_spatial[l][:, :, None], (A_GROUPS, CHUNK, LANES)),
            q_norm_g=q_norm_g[l][None], k_norm_g=k_norm_g[l][None], sink=sink[l],
            w_proj_a=w_proj_a[l].astype(BF16), w_proj_b=w_proj_b[l].astype(BF16), w_out=w_out[l].astype(BF16),
            norm_ffn_g=norm_ffn_g[l][None], w_router=w_router.astype(BF16), b_router=b_router,
            experts=(w_gate_e[l], w_up_e[l], w_down_e[l])))

    trunks = [x_prompt, x_sample]
    rows = [x.reshape(-1, D_MODEL) for x in trunks]
    order = sorted(range(len(trunks)), key=lambda n: -rows[n].shape[0])
    rope = _rope_tables(max(x.shape[1] for x in trunks))
    for p in layers:
        experts = p["experts"]
        for n in order:
            rows[n], experts = _layer(rows[n], p, trunks[n].shape[1], rope, experts)
    return tuple(r.reshape(x.shape) for r, x in zip(rows, trunks))
```

```python
import functools

import jax
import jax.numpy as jnp
from jax import lax
from jax.experimental import pallas as pl
from jax.experimental.pallas import tpu as pltpu
from jax.experimental.pallas import tpu_sc as plsc

F32 = jnp.float32
BF16 = jnp.bfloat16
I32 = jnp.int32
U32 = jnp.uint32

LANES = 128
SUBLANES = 8
VMEM_BYTES_V7X = 64 * 1024 * 1024
SC_CORES = 2
SC_SUBCORES = 16
SC_WORKERS = SC_CORES * SC_SUBCORES
SC_CHUNK_MAX = 128

D_MODEL = 1024
A_WIDTH = D_MODEL
A_GROUPS = 8
CHUNK = 128
HEAD_DIM = 128
N_Q_HEADS = D_MODEL // HEAD_DIM
N_KV_HEADS = 2
REP = N_Q_HEADS // N_KV_HEADS
WINDOW = 128
ROPE_THETA = 10000.0
ROPE_SPLIT = 64
Q_W = N_Q_HEADS * HEAD_DIM
KV_W = N_KV_HEADS * HEAD_DIM
IN_W = 2 * A_WIDTH + Q_W + 2 * KV_W + 2 * D_MODEL
COL_U = 0
COL_V = COL_U + A_WIDTH
COL_Q = COL_V + A_WIDTH
COL_K = COL_Q + Q_W
COL_VA = COL_K + KV_W
COL_GA = COL_VA + KV_W
COL_GB = COL_GA + D_MODEL
N_GROUPS = 4
EXPERTS_PER_GROUP = 8
N_EXPERTS = N_GROUPS * EXPERTS_PER_GROUP
TOP_K = 2
EPS = 1e-6
NEG = -1e30

TM_IN = 512
TM_IN_SUB = 256
TQ = 512
DENSE_COLS = 256
ROUTER_ROWS = 64
assert EXPERTS_PER_GROUP == SUBLANES and SUBLANES + N_EXPERTS <= ROUTER_ROWS
TM_ROW = 1024
TM_DEST = 8192
MOE_ROWS_MAX = 1024
MOE_MIN_BLOCKS = 1
VMEM_RESERVE = 8 * 1024 * 1024
VMEM_LIMIT = VMEM_BYTES_V7X - VMEM_RESERVE


def _rms(x, g):
    return x * lax.rsqrt(jnp.mean(x * x, axis=-1, keepdims=True) + EPS) * g


ROW_LINES = D_MODEL // 2 // LANES
ROW3 = (ROW_LINES, LANES)
HIGH_HALF = 0xFFFF0000


def _as_rows3(a):
    return a.reshape((a.shape[0] // ROW_LINES,) + ROW3)


def _as_lines(a):
    return a.reshape((a.shape[0] * ROW_LINES, LANES))


def _store_rows3(lines_ref, val):
    rows = val.shape[0]
    bits = lax.bitcast_convert_type(val.astype(BF16).astype(F32), U32)
    half = ROW_LINES * LANES
    for s in range(ROW_LINES):
        lo = bits[:, s * LANES:(s + 1) * LANES] >> 16
        hi = bits[:, half + s * LANES:half + (s + 1) * LANES] & U32(HIGH_HALF)
        lines_ref[pl.ds(s, rows, stride=ROW_LINES), :] = lax.bitcast_convert_type(lo | hi, I32)


def _load_rows3(lines_ref):
    rows = lines_ref.shape[0] // ROW_LINES
    words = [lax.bitcast_convert_type(lines_ref[pl.ds(s, rows, stride=ROW_LINES), :], U32)
             for s in range(ROW_LINES)]
    lo = [lax.bitcast_convert_type(w << 16, F32) for w in words]
    hi = [lax.bitcast_convert_type(w & U32(HIGH_HALF), F32) for w in words]
    return jnp.concatenate(lo + hi, axis=1)


def _rows3_spec(rows, index_map):
    return pl.BlockSpec((rows * ROW_LINES, LANES), index_map)


def _const_spec(shape):
    nd = len(shape)
    return pl.BlockSpec(shape, lambda *_: (0,) * nd, pipeline_mode=pl.Buffered(1))


def _inproj_kernel(x_ref, gmix_ref, win_ref, gv_ref, ws_ref, bs_ref, gq_ref, gk_ref, cos_ref, sin_ref,
                   wpa_ref, ma_ref, sgb_ref, q_ref, k_ref, v_ref, h_scr, u_scr, vn_scr, a_scr):
    tm = x_ref.shape[0]

    def stages(rows):
        def proj(lo, width):
            return jnp.dot(h_scr[rows], win_ref[:, lo:lo + width], preferred_element_type=F32)

        def norm_rope(z, g):
            zn = _rms(z, g)
            return zn * cos_ref[rows] + pltpu.roll(zn, HEAD_DIM // 2, 1) * sin_ref[rows]

        def norm():
            h_scr[rows] = _rms(x_ref[rows], gmix_ref[...]).astype(BF16)

        def mix_v():
            vn_scr[rows] = _rms(jax.nn.gelu(proj(COL_V, A_WIDTH)), gv_ref[...]).astype(BF16)

        def mix_u():
            u_scr[rows] = jax.nn.gelu(proj(COL_U, A_WIDTH))

        def spatial():
            for c in range(rows.start, rows.stop, CHUNK):
                chunk = slice(c, c + CHUNK)
                for g in range(A_GROUPS):
                    cols = slice(g * LANES, (g + 1) * LANES)
                    mixed = jnp.dot(ws_ref[g], vn_scr[chunk, cols], preferred_element_type=F32) + bs_ref[g]
                    a_scr[chunk, cols] = (u_scr[chunk, cols] * mixed).astype(BF16)

        def gate_a():
            ya = jnp.dot(a_scr[rows], wpa_ref[...], preferred_element_type=F32)
            ma_ref[rows] = jax.nn.sigmoid(proj(COL_GA, D_MODEL)) * ya

        def gate_b():
            sgb_ref[rows] = jax.nn.sigmoid(proj(COL_GB, D_MODEL))

        def queries():
            qz = proj(COL_Q, Q_W)
            for hd in range(N_Q_HEADS):
                cols = slice(hd * HEAD_DIM, (hd + 1) * HEAD_DIM)
                q_ref[rows, cols] = norm_rope(qz[:, cols], gq_ref[...]).astype(BF16)

        def keys_values():
            kz = proj(COL_K, KV_W)
            for hd in range(N_KV_HEADS):
                cols = slice(hd * HEAD_DIM, (hd + 1) * HEAD_DIM)
                k_ref[rows, cols] = norm_rope(kz[:, cols], gk_ref[...]).astype(BF16)
            v_ref[rows] = proj(COL_VA, KV_W).astype(BF16)

        return [norm, mix_v, queries, mix_u, gate_b, keys_values, spatial, gate_a]

    subs = [stages(slice(r, r + TM_IN_SUB)) for r in range(0, tm, TM_IN_SUB)]
    n_stage = len(subs[0])
    for step in range(n_stage + len(subs) - 1):
        for n, sub in enumerate(subs):
            if 0 <= step - n < n_stage:
                sub[step - n]()


def _inproj(x, p, cos, sin, seq):
    t = x.shape[0]
    tm = TM_IN
    n_pos = seq // tm
    row = lambda w: pl.BlockSpec((tm, w), lambda i: (i, 0))
    pos = pl.BlockSpec((tm, HEAD_DIM), lambda i: (i % n_pos, 0))
    return pl.pallas_call(
        _inproj_kernel,
        grid=(t // tm,),
        in_specs=[row(D_MODEL), _const_spec((1, D_MODEL)), _const_spec((D_MODEL, IN_W)),
                  _const_spec((1, A_WIDTH)), _const_spec((A_GROUPS, CHUNK, CHUNK)),
                  _const_spec((A_GROUPS, CHUNK, LANES)), _const_spec((1, HEAD_DIM)),
                  _const_spec((1, HEAD_DIM)), pos, pos, _const_spec((A_WIDTH, D_MODEL))],
        out_specs=[row(D_MODEL), row(D_MODEL), row(Q_W), row(KV_W), row(KV_W)],
        out_shape=[jax.ShapeDtypeStruct((t, D_MODEL), F32), jax.ShapeDtypeStruct((t, D_MODEL), F32),
                   jax.ShapeDtypeStruct((t, Q_W), BF16), jax.ShapeDtypeStruct((t, KV_W), BF16),
                   jax.ShapeDtypeStruct((t, KV_W), BF16)],
        scratch_shapes=[pltpu.VMEM((tm, D_MODEL), BF16), pltpu.VMEM((tm, A_WIDTH), F32),
                        pltpu.VMEM((tm, A_WIDTH), BF16), pltpu.VMEM((tm, A_WIDTH), BF16)],
        compiler_params=pltpu.CompilerParams(dimension_semantics=("parallel",),
                                             vmem_limit_bytes=VMEM_LIMIT),
        name="inproj",
    )(x, p["norm_mix_g"], p["w_in"], p["norm_v_g"], p["w_spatial"], p["b_spatial"], p["q_norm_g"],
      p["k_norm_g"], cos, sin, p["w_proj_a"])


def _attn_kernel(sink_ref, x_ref, ma_ref, sgb_ref, q_ref, kp_ref, kc_ref, kn_ref, vp_ref, vc_ref, vn_ref,
                 wpb_ref, wout_ref, gffn_ref, wr_ref, br_ref,
                 x1_ref, hn_ref, route_ref, gate_ref, counts_ref, kcat, vcat, o_scr, s_scr, p_scr, sink_scr, m_scr,
                 *, tiles_per_seq, n_tiles):
    tq = x_ref.shape[0]
    blk = WINDOW
    i = pl.program_id(0)
    slot = i % 2

    @pl.when(i == 0)
    def _():
        o_scr[...] = jnp.zeros_like(o_scr)
        counts_ref[...] = jnp.zeros_like(counts_ref)

    pos_tile = jnp.minimum(i, n_tiles - 1) % tiles_per_seq
    has_prev = pos_tile > 0
    has_next = pos_tile < tiles_per_seq - 1

    kcat[0:blk] = kp_ref[...]
    kcat[blk:blk + tq] = kc_ref[...]
    kcat[blk + tq:] = kn_ref[...]
    vcat[0:blk] = vp_ref[...]
    vcat[blk:blk + tq] = vc_ref[...]
    vcat[blk + tq:] = vn_ref[...]

    qr = lax.broadcasted_iota(I32, (blk, blk), 0)
    kc = lax.broadcasted_iota(I32, (blk, blk), 1)
    scale = HEAD_DIM ** -0.5
    n_sub = tq // blk
    pairs = [(j, g) for j in range(n_sub) for g in range(N_KV_HEADS)]

    def keys(ref, j, g):
        return ref[j * blk:(j + 3) * blk, g * HEAD_DIM:(g + 1) * HEAD_DIM]

    def head_cols(g, r):
        hd = g * REP + r
        return slice(hd * HEAD_DIM, (hd + 1) * HEAD_DIM)

    def scores(b):
        j, g = pairs[b]
        rows = slice(j * blk, (j + 1) * blk)
        qs = jnp.concatenate([q_ref[rows, head_cols(g, r)] for r in range(REP)], axis=0)
        s_scr[b] = lax.dot_general(qs, keys(kcat, j, g), (((1,), (1,)), ((), ())),
                                   preferred_element_type=F32)

    log2e = 1.4426950408889634

    def softmax(b, r):
        j, g = pairs[b]
        hrows = slice(r * blk, (r + 1) * blk)
        z = s_scr[b, hrows, :] * (scale * log2e)
        lo_ok = kc >= (qr + jnp.where(has_prev, 0, blk) if j == 0 else qr)
        hi_ok = kc <= (qr - jnp.where(has_next, 0, blk) if j == n_sub - 1 else qr)
        z = jnp.concatenate([jnp.where(lo_ok, z[:, :blk], NEG), z[:, blk:2 * blk],
                             jnp.where(hi_ok, z[:, 2 * blk:], NEG)], axis=1)
        sink = sink_ref[g * REP + r] * log2e
        m = jnp.maximum(jnp.max(z, axis=-1, keepdims=True), sink)
        p_scr[b, hrows, :] = jnp.exp2(z - m).astype(BF16)
        sink_scr[b, hrows, :] = jnp.broadcast_to(jnp.exp2(sink - m), (blk, LANES))

    def values(b):
        j, g = pairs[b]
        rows = slice(j * blk, (j + 1) * blk)
        v_ext = jnp.concatenate([keys(vcat, j, g), jnp.ones((3 * blk, HEAD_DIM), BF16)], axis=1)
        acc = jnp.dot(p_scr[b], v_ext, preferred_element_type=F32)
        o = (acc[:, :HEAD_DIM] / (acc[:, HEAD_DIM:] + sink_scr[b])).astype(BF16)
        for r in range(REP):
            o_scr[slot, rows, head_cols(g, r)] = o[r * blk:(r + 1) * blk, :]

    def merged_cols(cols):
        yb = jnp.dot(o_scr[1 - slot], wpb_ref[:, cols], preferred_element_type=F32)
        m_scr[:, cols] = (ma_ref[:, cols] + sgb_ref[:, cols] * yb).astype(BF16)

    def x1_cols(cols):
        x1_ref[:, cols] = x_ref[:, cols] + jnp.dot(m_scr[...], wout_ref[:, cols], preferred_element_type=F32)

    col_chunks = [slice(c * DENSE_COLS, (c + 1) * DENSE_COLS) for c in range(D_MODEL // DENSE_COLS)]
    dense = [functools.partial(f, cols) for f in (merged_cols, x1_cols) for cols in col_chunks]
    units = [(b, r) for b in range(len(pairs)) for r in range(REP)]
    units_per_dense = len(units) // len(dense)
    scores(0)
    for n, (b, r) in enumerate(units):
        if r == 0 and b + 1 < len(pairs):
            scores(b + 1)
        softmax(b, r)
        if (n + 1) % units_per_dense == 0:
            dense[(n + 1) // units_per_dense - 1]()
        if r == REP - 1:
            values(b)

    hn = _rms(x1_ref[...], gffn_ref[...])
    _store_rows3(hn_ref, hn)
    def wide(a):
        return jnp.concatenate([a] * (tq // LANES), axis=1)

    logits = lax.dot_general(wr_ref[...], hn.astype(BF16), (((1,), (1,)), ((), ())),
                             preferred_element_type=F32) + wide(br_ref[...])
    sub = lax.broadcasted_iota(I32, (SUBLANES, tq), 0).astype(F32)
    ninf = -jnp.inf

    def cmax(a):
        return jnp.max(a, axis=0, keepdims=True)

    def csum(a):
        return jnp.sum(a, axis=0, keepdims=True)

    def first_row(mask):
        return jnp.min(jnp.where(mask, sub, float(SUBLANES)), axis=0, keepdims=True)

    def group_rows(g):
        return logits[(g + 1) * SUBLANES:(g + 2) * SUBLANES]

    gl = jnp.where(sub < N_GROUPS, logits[0:SUBLANES], ninf)
    gmax = cmax(gl)
    g_sel = first_row(gl == gmax)
    g_p = 1.0 / csum(jnp.exp(gl - gmax))
    el = group_rows(0)
    for g in range(1, N_GROUPS):
        el = jnp.where(g_sel == g, group_rows(g), el)
    ee = jnp.exp(el - cmax(el))
    eprob = ee / csum(ee)
    p1 = cmax(eprob)
    i1 = first_row(eprob == p1)
    eprob2 = jnp.where(sub == i1, -1.0, eprob)
    p2 = cmax(eprob2)
    i2 = first_row(eprob2 == p2)
    psum = p1 + p2
    w1 = g_p * p1 / psum
    w2 = g_p * p2 / psum
    e1 = g_sel * EXPERTS_PER_GROUP + i1
    e2 = g_sel * EXPERTS_PER_GROUP + i2

    erow = lax.broadcasted_iota(I32, (N_EXPERTS, tq), 0).astype(F32)
    oh1 = erow == e1
    oh2 = erow == e2
    cnt = (jnp.where(oh1, 1.0, 0.0) + jnp.where(oh2, 1.0, 0.0)) * jnp.where(i > 0, 1.0, 0.0)
    ri = lax.broadcasted_iota(I32, (tq, tq), 0)
    ci = lax.broadcasted_iota(I32, (tq, tq), 1)
    earlier = jnp.where(ri < ci, 1.0, 0.0).astype(BF16)
    base = wide(counts_ref[...]) + jnp.dot(cnt.astype(BF16), earlier, preferred_element_type=F32)
    r1 = csum(jnp.where(oh1, base, 0.0))
    r2 = csum(jnp.where(oh2, base, 0.0))
    counts_ref[...] = counts_ref[...] + jnp.sum(cnt, axis=1, keepdims=True)

    route = jnp.where(sub == 0.0, e1, jnp.where(sub == 1.0, e2,
                      jnp.where(sub == 2.0, r1, jnp.where(sub == 3.0, r2, 0.0))))
    route_ref[...] = route.astype(I32)
    gate_ref[...] = jnp.where(sub == 0.0, w1, jnp.where(sub == 1.0, w2, 0.0))


def _attn(x, ma, sgb, q, k, v, p, seq):
    t = x.shape[0]
    tq = TQ
    sub = tq // WINDOW
    last_blk = t // WINDOW - 1
    n_tiles = t // tq
    att = lambda i: jnp.minimum(i, n_tiles - 1)
    post = lambda i: jnp.maximum(i - 1, 0)
    att_row = lambda w: pl.BlockSpec((tq, w), lambda i: (att(i), 0))
    row = lambda w: pl.BlockSpec((tq, w), lambda i: (post(i), 0))
    prev = pl.BlockSpec((WINDOW, KV_W), lambda i: (jnp.maximum(att(i) * sub - 1, 0), 0))
    nxt = pl.BlockSpec((WINDOW, KV_W), lambda i: (jnp.minimum((att(i) + 1) * sub, last_blk), 0))
    return pl.pallas_call(
        functools.partial(_attn_kernel, tiles_per_seq=seq // tq, n_tiles=n_tiles),
        grid=(n_tiles + 1,),
        in_specs=[pl.BlockSpec(memory_space=pltpu.SMEM),
                  row(D_MODEL), row(D_MODEL), row(D_MODEL), att_row(Q_W),
                  prev, att_row(KV_W), nxt, prev, att_row(KV_W), nxt,
                  _const_spec((Q_W, D_MODEL)), _const_spec((D_MODEL, D_MODEL)), _const_spec((1, D_MODEL)),
                  _const_spec((ROUTER_ROWS, D_MODEL)), _const_spec((ROUTER_ROWS, LANES))],
        out_specs=[row(D_MODEL), _rows3_spec(tq, lambda i: (post(i), 0)),
                   pl.BlockSpec((SUBLANES, tq), lambda i: (0, post(i))),
                   pl.BlockSpec((SUBLANES, tq), lambda i: (0, post(i))),
                   pl.BlockSpec((N_EXPERTS, LANES), lambda i: (0, 0))],
        out_shape=[jax.ShapeDtypeStruct((t, D_MODEL), F32), jax.ShapeDtypeStruct((t * ROW_LINES, LANES), I32),
                   jax.ShapeDtypeStruct((SUBLANES, t), I32), jax.ShapeDtypeStruct((SUBLANES, t), F32),
                   jax.ShapeDtypeStruct((N_EXPERTS, LANES), F32)],
        scratch_shapes=[pltpu.VMEM((tq + 2 * WINDOW, KV_W), BF16), pltpu.VMEM((tq + 2 * WINDOW, KV_W), BF16),
                        pltpu.VMEM((2, tq, Q_W), BF16),
                        pltpu.VMEM((sub * N_KV_HEADS, REP * WINDOW, 3 * WINDOW), F32),
                        pltpu.VMEM((sub * N_KV_HEADS, REP * WINDOW, 3 * WINDOW), BF16),
                        pltpu.VMEM((sub * N_KV_HEADS, REP * WINDOW, LANES), F32),
                        pltpu.VMEM((tq, D_MODEL), BF16)],
        compiler_params=pltpu.CompilerParams(dimension_semantics=("arbitrary",),
                                             vmem_limit_bytes=VMEM_LIMIT),
        name="attn",
    )(p["sink"], x, ma, sgb, q, k, k, k, v, v, v, p["w_proj_b"], p["w_out"], p["norm_ffn_g"],
      p["w_router"], p["b_router"])


def _dest_kernel(route_ref, starts_ref, dest_ref):
    route = route_ref[...].astype(F32)
    td = route.shape[1]
    sub = lax.broadcasted_iota(I32, route.shape, 0)
    erow = lax.broadcasted_iota(I32, (N_EXPERTS, td), 0).astype(F32)
    starts = jnp.broadcast_to(starts_ref[:, 0:1], (N_EXPERTS, td))

    def slot(k):
        start = jnp.sum(jnp.where(erow == route[k:k + 1], starts, 0.0), axis=0, keepdims=True)
        return start + route[TOP_K + k:TOP_K + k + 1]

    dest_ref[...] = jnp.where(sub == 0, slot(0), jnp.where(sub == 1, slot(1), 0.0)).astype(I32)


def _dest(route, pad_starts):
    t = route.shape[1]
    td = min(TM_DEST, t)
    return pl.pallas_call(
        _dest_kernel,
        grid=(t // td,),
        in_specs=[pl.BlockSpec((SUBLANES, td), lambda i: (0, i)), _const_spec((N_EXPERTS, LANES))],
        out_specs=pl.BlockSpec((SUBLANES, td), lambda i: (0, i)),
        out_shape=jax.ShapeDtypeStruct((SUBLANES, t), I32),
        compiler_params=pltpu.CompilerParams(dimension_semantics=("parallel",)),
        name="dest",
    )(route, pad_starts)


def _sc_mesh():
    return plsc.VectorSubcoreMesh(core_axis_name="c", subcore_axis_name="s")


def _sc_worker():
    return lax.axis_index("s") * SC_CORES + lax.axis_index("c")


def _sc_chunk(t):
    return min(SC_CHUNK_MAX, t // (SC_WORKERS * SUBLANES))


def _dispatch(hn, dests, n_rows):
    t = hn.shape[0]
    chunk = dests[0].shape[1]
    per_worker = t // SC_WORKERS
    n_chunks = per_worker // chunk
    idx = pltpu.VMEM((n_chunks, chunk), I32)

    @functools.partial(
        pl.kernel, mesh=_sc_mesh(), out_type=jax.ShapeDtypeStruct((n_rows,) + ROW3, I32),
        scratch_types=[idx, idx, pltpu.VMEM((chunk,) + ROW3, I32), pltpu.SemaphoreType.DMA])
    def scatter_rows(hn_hbm, d0_hbm, d1_hbm, xs_hbm, i0_v, i1_v, rows_v, sem):
        w = _sc_worker()
        pltpu.sync_copy(d0_hbm.at[pl.ds(w * n_chunks, n_chunks)], i0_v)
        pltpu.sync_copy(d1_hbm.at[pl.ds(w * n_chunks, n_chunks)], i1_v)

        @pl.loop(0, n_chunks)
        def _(j):
            pltpu.sync_copy(hn_hbm.at[pl.ds(w * per_worker + j * chunk, chunk)], rows_v)
            copies = [pltpu.make_async_copy(rows_v, xs_hbm.at[i_v.at[j]], sem) for i_v in (i0_v, i1_v)]
            for cp in copies:
                cp.start()
            for cp in copies:
                cp.wait()

    return scatter_rows(hn, *dests)


def _moe_block_rows(t):
    rows = MOE_ROWS_MAX
    while rows > CHUNK and (t * TOP_K) // N_EXPERTS < MOE_MIN_BLOCKS * rows:
        rows //= 2
    return rows


def _moe_kernel(be_ref, nvalid_ref, nused_ref, xs_ref, wg_ref, wu_ref, wd_ref, yb_ref, *w16_refs):
    i = pl.program_id(0)
    used = i < nused_ref[0]
    weights = w16_refs if w16_refs else (wg_ref, wu_ref, wd_ref)

    @pl.when(jnp.logical_not(used))
    def _():
        yb_ref[...] = jnp.zeros_like(yb_ref)

    @pl.when(used)
    def _():
        if w16_refs:
            @pl.when(jnp.logical_or(i == 0, be_ref[i] != be_ref[jnp.maximum(i - 1, 0)]))
            def _():
                for src, dst in zip((wg_ref, wu_ref, wd_ref), w16_refs):
                    dst[0] = src[0].astype(BF16)

        row = lax.broadcasted_iota(I32, (xs_ref.shape[0] // ROW_LINES, 1), 0)
        x = jnp.where(row < nvalid_ref[i], _load_rows3(xs_ref), 0.0).astype(BF16)
        gate = jnp.dot(x, weights[0][0], preferred_element_type=F32)
        up = jnp.dot(x, weights[1][0], preferred_element_type=F32)
        hid = (jax.nn.silu(gate) * up).astype(BF16)
        _store_rows3(yb_ref, jnp.dot(hid, weights[2][0], preferred_element_type=F32))


def _moe(block_e, n_valid, n_used, xs, experts, block_rows):
    n_blocks = xs.shape[0] // (block_rows * ROW_LINES)
    cast = experts[0].dtype != BF16

    def rows(i, be, nv, nu):
        return (jnp.minimum(i, nu[0] - 1), 0)

    def expert(i, be, nv, nu):
        return (be[jnp.minimum(i, nu[0] - 1)], 0, 0)

    w_specs = [pl.BlockSpec((1,) + w.shape[1:], expert) for w in experts]
    out_specs = [_rows3_spec(block_rows, lambda i, be, nv, nu: (i, 0))]
    out_shape = [jax.ShapeDtypeStruct(xs.shape, I32)]
    if cast:
        out_specs += w_specs
        out_shape += [jax.ShapeDtypeStruct(w.shape, BF16) for w in experts]
    yb, *w16 = pl.pallas_call(
        _moe_kernel,
        grid_spec=pltpu.PrefetchScalarGridSpec(
            num_scalar_prefetch=3,
            grid=(n_blocks,),
            in_specs=[_rows3_spec(block_rows, rows)] + w_specs,
            out_specs=out_specs,
        ),
        out_shape=out_shape,
        compiler_params=pltpu.CompilerParams(dimension_semantics=("arbitrary",),
                                             vmem_limit_bytes=VMEM_LIMIT),
        name="moe",
    )(block_e, n_valid, n_used, xs, *experts)
    return yb, (tuple(w16) if cast else experts)


def _gather(yb, dests):
    chunk = dests[0].shape[1]
    t = dests[0].shape[0] * chunk
    per_worker = t // SC_WORKERS
    n_chunks = per_worker // chunk
    idx = pltpu.VMEM((n_chunks, chunk), I32)
    out = jax.ShapeDtypeStruct((t,) + ROW3, I32)

    @functools.partial(
        pl.kernel, mesh=_sc_mesh(), out_type=(out, out),
        scratch_types=[idx, idx, pltpu.VMEM((chunk,) + ROW3, I32), pltpu.SemaphoreType.DMA])
    def gather_rows(yb_hbm, d0_hbm, d1_hbm, y0_hbm, y1_hbm, i0_v, i1_v, rows_v, sem):
        w = _sc_worker()
        pltpu.sync_copy(d0_hbm.at[pl.ds(w * n_chunks, n_chunks)], i0_v)
        pltpu.sync_copy(d1_hbm.at[pl.ds(w * n_chunks, n_chunks)], i1_v)

        @pl.loop(0, n_chunks)
        def _(j):
            rows = pl.ds(w * per_worker + j * chunk, chunk)
            for i_v, y_hbm in ((i0_v, y0_hbm), (i1_v, y1_hbm)):
                pltpu.async_copy(yb_hbm.at[i_v.at[j]], rows_v, sem).wait()
                pltpu.sync_copy(rows_v, y_hbm.at[rows])

    return gather_rows(yb, *dests)


def _combine_kernel(x1_ref, gate_ref, y0_ref, y1_ref, out_ref):
    tm = x1_ref.shape[0]
    gate = jnp.concatenate([gate_ref[...], jnp.zeros((LANES - SUBLANES, tm), F32)], axis=0).T
    out_ref[...] = x1_ref[...] + (_load_rows3(y0_ref) * gate[:, 0:1] + _load_rows3(y1_ref) * gate[:, 1:2])


def _combine(x1, gate, y0, y1):
    t = x1.shape[0]
    tm = TM_ROW
    return pl.pallas_call(
        _combine_kernel,
        grid=(t // tm,),
        in_specs=[pl.BlockSpec((tm, D_MODEL), lambda i: (i, 0)),
                  pl.BlockSpec((SUBLANES, tm), lambda i: (0, i)),
                  _rows3_spec(tm, lambda i: (i, 0)), _rows3_spec(tm, lambda i: (i, 0))],
        out_specs=pl.BlockSpec((tm, D_MODEL), lambda i: (i, 0)),
        out_shape=jax.ShapeDtypeStruct((t, D_MODEL), F32),
        compiler_params=pltpu.CompilerParams(dimension_semantics=("parallel",)),
        name="combine",
    )(x1, gate, y0, y1)


def _rope_tables(seq):
    half = HEAD_DIM // 2
    inv_freq = ROPE_THETA ** (-jnp.arange(half, dtype=F32) / half)
    coarse = (jnp.arange(seq // ROPE_SPLIT) * ROPE_SPLIT).astype(F32)[:, None, None] * inv_freq
    fine = jnp.arange(ROPE_SPLIT).astype(F32)[None, :, None] * inv_freq
    cos = (jnp.cos(coarse) * jnp.cos(fine) - jnp.sin(coarse) * jnp.sin(fine)).reshape(seq, half)
    sin = (jnp.sin(coarse) * jnp.cos(fine) + jnp.cos(coarse) * jnp.sin(fine)).reshape(seq, half)
    return jnp.concatenate([cos, cos], axis=-1), jnp.concatenate([-sin, sin], axis=-1)


def _layer(x, p, seq, rope, experts):
    t = x.shape[0]
    ma, sgb, q, k, v = _inproj(x, p, *rope, seq)
    x1, hn, route, gate, counts_f = _attn(x, ma, sgb, q, k, v, p, seq)

    block_rows = _moe_block_rows(t)
    counts = counts_f[:, 0].astype(I32)
    min_blocks = 1 if experts[0].dtype != BF16 else 0
    padded = jnp.maximum((counts + block_rows - 1) // block_rows, min_blocks) * block_rows
    pad_ends = jnp.cumsum(padded)
    pad_starts = pad_ends - padded
    n_blocks = (t * TOP_K) // block_rows + N_EXPERTS
    block_start = jnp.arange(n_blocks, dtype=I32) * block_rows
    in_expert = jnp.logical_and(block_start[:, None] >= pad_starts[None, :],
                                block_start[:, None] < pad_ends[None, :]).astype(I32)
    block_e = jnp.minimum(jnp.sum((block_start[:, None] >= pad_ends[None, :]).astype(I32), axis=1), N_EXPERTS - 1)
    n_valid = jnp.sum(in_expert * jnp.clip(pad_starts + counts - block_start[:, None], 0, block_rows), axis=1)
    n_used = pad_ends[-1:] // block_rows
    starts_col = jnp.broadcast_to(pad_starts.astype(F32)[:, None], (N_EXPERTS, LANES))

    dest = _dest(route, starts_col)
    dests = [dest[k].reshape(t // _sc_chunk(t), _sc_chunk(t)) for k in range(TOP_K)]
    xs = _dispatch(_as_rows3(hn), dests, n_blocks * block_rows)
    yb, experts = _moe(block_e, n_valid, n_used, _as_lines(xs), experts, block_rows)
    y0, y1 = _gather(_as_rows3(yb), dests)
    return _combine(x1, gate, _as_lines(y0), _as_lines(y1)), experts


def kernel(x_prompt, x_sample, norm_mix_g, w_in, norm_v_g, w_spatial, b_spatial, q_norm_g, k_norm_g, sink,
           w_proj_a, w_proj_b, w_out, norm_ffn_g, w_router_group, b_router_group, w_router_expert,
           b_router_expert, w_gate_e, w_up_e, w_down_e):
    depth = w_in.shape[0]
    layers = []
    for l in range(depth):
        w_router = jnp.zeros((ROUTER_ROWS, D_MODEL), F32)
        w_router = w_router.at[:N_GROUPS].set(w_router_group[l].T)
        w_router = w_router.at[SUBLANES:SUBLANES + N_EXPERTS].set(w_router_expert[l].T)
        b_router = jnp.zeros((ROUTER_ROWS,), F32)
        b_router = b_router.at[:N_GROUPS].set(b_router_group[l])
        b_router = b_router.at[SUBLANES:SUBLANES + N_EXPERTS].set(b_router_expert[l])
        b_router = jnp.broadcast_to(b_router[:, None], (ROUTER_ROWS, LANES))
        layers.append(dict(
            norm_mix_g=norm_mix_g[l][None], w_in=w_in[l].astype(BF16), norm_v_g=norm_v_g[l][None],
            w_spatial=w_spatial[l].astype(BF16),
            b_spatial=jnp.broadcast_to(b_spatial[l][:, :, None], (A_GROUPS, CHUNK, LANES)),
            q_norm_g=q_norm_g[l][None], k_norm_g=k_norm_g[l][None], sink=sink[l],
            w_proj_a=w_proj_a[l].astype(BF16), w_proj_b=w_proj_b[l].astype(BF16), w_out=w_out[l].astype(BF16),
            norm_ffn_g=norm_ffn_g[l][None], w_router=w_router.astype(BF16), b_router=b_router,
            experts=(w_gate_e[l], w_up_e[l], w_down_e[l])))

    trunks = [x_prompt, x_sample]
    rows = [x.reshape(-1, D_MODEL) for x in trunks]
    order = sorted(range(len(trunks)), key=lambda n: -rows[n].shape[0])
    rope = _rope_tables(max(x.shape[1] for x in trunks))
    for p in layers:
        experts = p["experts"]
        for n in order:
            rows[n], experts = _layer(rows[n], p, trunks[n].shape[1], rope, experts)
    return tuple(r.reshape(x.shape) for r, x in zip(rows, trunks))
```

```python
import functools

import jax
import jax.numpy as jnp
from jax import lax
from jax.experimental import pallas as pl
from jax.experimental.pallas import tpu as pltpu
from jax.experimental.pallas import tpu_sc as plsc

F32 = jnp.float32
BF16 = jnp.bfloat16
I32 = jnp.int32
U32 = jnp.uint32

LANES = 128
SUBLANES = 8
VMEM_BYTES_V7X = 64 * 1024 * 1024
SC_CORES = 2
SC_SUBCORES = 16
SC_WORKERS = SC_CORES * SC_SUBCORES
SC_CHUNK_MAX = 128

D_MODEL = 1024
A_WIDTH = D_MODEL
A_GROUPS = 8
CHUNK = 128
HEAD_DIM = 128
N_Q_HEADS = D_MODEL // HEAD_DIM
N_KV_HEADS = 2
REP = N_Q_HEADS // N_KV_HEADS
WINDOW = 128
ROPE_THETA = 10000.0
ROPE_SPLIT = 64
Q_W = N_Q_HEADS * HEAD_DIM
KV_W = N_KV_HEADS * HEAD_DIM
IN_W = 2 * A_WIDTH + Q_W + 2 * KV_W + 2 * D_MODEL
COL_U = 0
COL_V = COL_U + A_WIDTH
COL_Q = COL_V + A_WIDTH
COL_K = COL_Q + Q_W
COL_VA = COL_K + KV_W
COL_GA = COL_VA + KV_W
COL_GB = COL_GA + D_MODEL
N_GROUPS = 4
EXPERTS_PER_GROUP = 8
N_EXPERTS = N_GROUPS * EXPERTS_PER_GROUP
TOP_K = 2
EPS = 1e-6
NEG = -1e30

TM_IN = 512
TM_IN_SUB = 256
TQ = 512
DENSE_COLS = 256
ROUTER_ROWS = 64
assert EXPERTS_PER_GROUP == SUBLANES and SUBLANES + N_EXPERTS <= ROUTER_ROWS
TM_ROW = 1024
TM_DEST = 8192
MOE_ROWS_MAX = 1024
MOE_MIN_BLOCKS = 1
VMEM_RESERVE = 8 * 1024 * 1024
VMEM_LIMIT = VMEM_BYTES_V7X - VMEM_RESERVE


def _rms(x, g):
    return x * lax.rsqrt(jnp.mean(x * x, axis=-1, keepdims=True) + EPS) * g


ROW_LINES = D_MODEL // 2 // LANES
ROW3 = (ROW_LINES, LANES)
HIGH_HALF = 0xFFFF0000


def _as_rows3(a):
    return a.reshape((a.shape[0] // ROW_LINES,) + ROW3)


def _as_lines(a):
    return a.reshape((a.shape[0] * ROW_LINES, LANES))


def _store_rows3(lines_ref, val):
    rows = val.shape[0]
    bits = lax.bitcast_convert_type(val.astype(BF16).astype(F32), U32)
    half = ROW_LINES * LANES
    for s in range(ROW_LINES):
        lo = bits[:, s * LANES:(s + 1) * LANES] >> 16
        hi = bits[:, half + s * LANES:half + (s + 1) * LANES] & U32(HIGH_HALF)
        lines_ref[pl.ds(s, rows, stride=ROW_LINES), :] = lax.bitcast_convert_type(lo | hi, I32)


def _load_rows3(lines_ref):
    rows = lines_ref.shape[0] // ROW_LINES
    words = [lax.bitcast_convert_type(lines_ref[pl.ds(s, rows, stride=ROW_LINES), :], U32)
             for s in range(ROW_LINES)]
    lo = [lax.bitcast_convert_type(w << 16, F32) for w in words]
    hi = [lax.bitcast_convert_type(w & U32(HIGH_HALF), F32) for w in words]
    return jnp.concatenate(lo + hi, axis=1)


def _rows3_spec(rows, index_map):
    return pl.BlockSpec((rows * ROW_LINES, LANES), index_map)


def _const_spec(shape):
    nd = len(shape)
    return pl.BlockSpec(shape, lambda *_: (0,) * nd, pipeline_mode=pl.Buffered(1))


def _inproj_kernel(x_ref, gmix_ref, win_ref, gv_ref, ws_ref, bs_ref, gq_ref, gk_ref, cos_ref, sin_ref,
                   wpa_ref, ma_ref, sgb_ref, q_ref, k_ref, v_ref, h_scr, u_scr, vn_scr, a_scr):
    tm = x_ref.shape[0]

    def stages(rows):
        def proj(lo, width):
            return jnp.dot(h_scr[rows], win_ref[:, lo:lo + width], preferred_element_type=F32)

        def norm_rope(z, g):
            zn = _rms(z, g)
            return zn * cos_ref[rows] + pltpu.roll(zn, HEAD_DIM // 2, 1) * sin_ref[rows]

        def norm():
            h_scr[rows] = _rms(x_ref[rows], gmix_ref[...]).astype(BF16)

        def mix_v():
            vn_scr[rows] = _rms(jax.nn.gelu(proj(COL_V, A_WIDTH)), gv_ref[...]).astype(BF16)

        def mix_u():
            u_scr[rows] = jax.nn.gelu(proj(COL_U, A_WIDTH))

        def spatial():
            for c in range(rows.start, rows.stop, CHUNK):
                chunk = slice(c, c + CHUNK)
                for g in range(A_GROUPS):
                    cols = slice(g * LANES, (g + 1) * LANES)
                    mixed = jnp.dot(ws_ref[g], vn_scr[chunk, cols], preferred_element_type=F32) + bs_ref[g]
                    a_scr[chunk, cols] = (u_scr[chunk, cols] * mixed).astype(BF16)

        def gate_a():
            ya = jnp.dot(a_scr[rows], wpa_ref[...], preferred_element_type=F32)
            ma_ref[rows] = jax.nn.sigmoid(proj(COL_GA, D_MODEL)) * ya

        def gate_b():
            sgb_ref[rows] = jax.nn.sigmoid(proj(COL_GB, D_MODEL))

        def queries():
            qz = proj(COL_Q, Q_W)
            for hd in range(N_Q_HEADS):
                cols = slice(hd * HEAD_DIM, (hd + 1) * HEAD_DIM)
                q_ref[rows, cols] = norm_rope(qz[:, cols], gq_ref[...]).astype(BF16)

        def keys_values():
            kz = proj(COL_K, KV_W)
            for hd in range(N_KV_HEADS):
                cols = slice(hd * HEAD_DIM, (hd + 1) * HEAD_DIM)
                k_ref[rows, cols] = norm_rope(kz[:, cols], gk_ref[...]).astype(BF16)
            v_ref[rows] = proj(COL_VA, KV_W).astype(BF16)

        return [norm, mix_v, queries, mix_u, gate_b, keys_values, spatial, gate_a]

    subs = [stages(slice(r, r + TM_IN_SUB)) for r in range(0, tm, TM_IN_SUB)]
    n_stage = len(subs[0])
    for step in range(n_stage + len(subs) - 1):
        for n, sub in enumerate(subs):
            if 0 <= step - n < n_stage:
                sub[step - n]()


def _inproj(x, p, cos, sin, seq):
    t = x.shape[0]
    tm = TM_IN
    n_pos = seq // tm
    row = lambda w: pl.BlockSpec((tm, w), lambda i: (i, 0))
    pos = pl.BlockSpec((tm, HEAD_DIM), lambda i: (i % n_pos, 0))
    return pl.pallas_call(
        _inproj_kernel,
        grid=(t // tm,),
        in_specs=[row(D_MODEL), _const_spec((1, D_MODEL)), _const_spec((D_MODEL, IN_W)),
                  _const_spec((1, A_WIDTH)), _const_spec((A_GROUPS, CHUNK, CHUNK)),
                  _const_spec((A_GROUPS, CHUNK, LANES)), _const_spec((1, HEAD_DIM)),
                  _const_spec((1, HEAD_DIM)), pos, pos, _const_spec((A_WIDTH, D_MODEL))],
        out_specs=[row(D_MODEL), row(D_MODEL), row(Q_W), row(KV_W), row(KV_W)],
        out_shape=[jax.ShapeDtypeStruct((t, D_MODEL), F32), jax.ShapeDtypeStruct((t, D_MODEL), F32),
                   jax.ShapeDtypeStruct((t, Q_W), BF16), jax.ShapeDtypeStruct((t, KV_W), BF16),
                   jax.ShapeDtypeStruct((t, KV_W), BF16)],
        scratch_shapes=[pltpu.VMEM((tm, D_MODEL), BF16), pltpu.VMEM((tm, A_WIDTH), F32),
                        pltpu.VMEM((tm, A_WIDTH), BF16), pltpu.VMEM((tm, A_WIDTH), BF16)],
        compiler_params=pltpu.CompilerParams(dimension_semantics=("parallel",),
                                             vmem_limit_bytes=VMEM_LIMIT),
        name="inproj",
    )(x, p["norm_mix_g"], p["w_in"], p["norm_v_g"], p["w_spatial"], p["b_spatial"], p["q_norm_g"],
      p["k_norm_g"], cos, sin, p["w_proj_a"])


def _attn_kernel(sink_ref, x_ref, ma_ref, sgb_ref, q_ref, kp_ref, kc_ref, kn_ref, vp_ref, vc_ref, vn_ref,
                 wpb_ref, wout_ref, gffn_ref, wr_ref, br_ref,
                 x1_ref, hn_ref, route_ref, gate_ref, counts_ref, kcat, vcat, o_scr, s_scr, p_scr, sink_scr, m_scr,
                 *, tiles_per_seq, n_tiles):
    tq = x_ref.shape[0]
    blk = WINDOW
    i = pl.program_id(0)
    slot = i % 2

    @pl.when(i == 0)
    def _():
        o_scr[...] = jnp.zeros_like(o_scr)
        counts_ref[...] = jnp.zeros_like(counts_ref)

    pos_tile = jnp.minimum(i, n_tiles - 1) % tiles_per_seq
    has_prev = pos_tile > 0
    has_next = pos_tile < tiles_per_seq - 1

    kcat[0:blk] = kp_ref[...]
    kcat[blk:blk + tq] = kc_ref[...]
    kcat[blk + tq:] = kn_ref[...]
    vcat[0:blk] = vp_ref[...]
    vcat[blk:blk + tq] = vc_ref[...]
    vcat[blk + tq:] = vn_ref[...]

    qr = lax.broadcasted_iota(I32, (blk, blk), 0)
    kc = lax.broadcasted_iota(I32, (blk, blk), 1)
    scale = HEAD_DIM ** -0.5
    n_sub = tq // blk
    pairs = [(j, g) for j in range(n_sub) for g in range(N_KV_HEADS)]

    def keys(ref, j, g):
        return ref[j * blk:(j + 3) * blk, g * HEAD_DIM:(g + 1) * HEAD_DIM]

    def head_cols(g, r):
        hd = g * REP + r
        return slice(hd * HEAD_DIM, (hd + 1) * HEAD_DIM)

    def scores(b):
        j, g = pairs[b]
        rows = slice(j * blk, (j + 1) * blk)
        qs = jnp.concatenate([q_ref[rows, head_cols(g, r)] for r in range(REP)], axis=0)
        s_scr[b] = lax.dot_general(qs, keys(kcat, j, g), (((1,), (1,)), ((), ())),
                                   preferred_element_type=F32)

    log2e = 1.4426950408889634

    def softmax(b, r):
        j, g = pairs[b]
        hrows = slice(r * blk, (r + 1) * blk)
        z = s_scr[b, hrows, :] * (scale * log2e)
        lo_ok = kc >= (qr + jnp.where(has_prev, 0, blk) if j == 0 else qr)
        hi_ok = kc <= (qr - jnp.where(has_next, 0, blk) if j == n_sub - 1 else qr)
        z = jnp.concatenate([jnp.where(lo_ok, z[:, :blk], NEG), z[:, blk:2 * blk],
                             jnp.where(hi_ok, z[:, 2 * blk:], NEG)], axis=1)
        sink = sink_ref[g * REP + r] * log2e
        m = jnp.maximum(jnp.max(z, axis=-1, keepdims=True), sink)
        p_scr[b, hrows, :] = jnp.exp2(z - m).astype(BF16)
        sink_scr[b, hrows, :] = jnp.broadcast_to(jnp.exp2(sink - m), (blk, LANES))

    def values(b):
        j, g = pairs[b]
        rows = slice(j * blk, (j + 1) * blk)
        v_ext = jnp.concatenate([keys(vcat, j, g), jnp.ones((3 * blk, HEAD_DIM), BF16)], axis=1)
        acc = jnp.dot(p_scr[b], v_ext, preferred_element_type=F32)
        o = (acc[:, :HEAD_DIM] / (acc[:, HEAD_DIM:] + sink_scr[b])).astype(BF16)
        for r in range(REP):
            o_scr[slot, rows, head_cols(g, r)] = o[r * blk:(r + 1) * blk, :]

    def merged_cols(cols):
        yb = jnp.dot(o_scr[1 - slot], wpb_ref[:, cols], preferred_element_type=F32)
        m_scr[:, cols] = (ma_ref[:, cols] + sgb_ref[:, cols] * yb).astype(BF16)

    def x1_cols(cols):
        x1_ref[:, cols] = x_ref[:, cols] + jnp.dot(m_scr[...], wout_ref[:, cols], preferred_element_type=F32)

    col_chunks = [slice(c * DENSE_COLS, (c + 1) * DENSE_COLS) for c in range(D_MODEL // DENSE_COLS)]
    dense = [functools.partial(f, cols) for f in (merged_cols, x1_cols) for cols in col_chunks]
    units = [(b, r) for b in range(len(pairs)) for r in range(REP)]
    units_per_dense = len(units) // len(dense)
    scores(0)
    for n, (b, r) in enumerate(units):
        if r == 0 and b + 1 < len(pairs):
            scores(b + 1)
        if n % units_per_dense == 0:
            dense[n // units_per_dense]()
        softmax(b, r)
        if r == REP - 1:
            values(b)

    hn = _rms(x1_ref[...], gffn_ref[...])
    _store_rows3(hn_ref, hn)
    def wide(a):
        return jnp.concatenate([a] * (tq // LANES), axis=1)

    logits = lax.dot_general(wr_ref[...], hn.astype(BF16), (((1,), (1,)), ((), ())),
                             preferred_element_type=F32) + wide(br_ref[...])
    sub = lax.broadcasted_iota(I32, (SUBLANES, tq), 0).astype(F32)
    ninf = -jnp.inf

    def cmax(a):
        return jnp.max(a, axis=0, keepdims=True)

    def csum(a):
        return jnp.sum(a, axis=0, keepdims=True)

    def first_row(mask):
        return jnp.min(jnp.where(mask, sub, float(SUBLANES)), axis=0, keepdims=True)

    def group_rows(g):
        return logits[(g + 1) * SUBLANES:(g + 2) * SUBLANES]

    gl = jnp.where(sub < N_GROUPS, logits[0:SUBLANES], ninf)
    gmax = cmax(gl)
    g_sel = first_row(gl == gmax)
    g_p = 1.0 / csum(jnp.exp(gl - gmax))
    el = group_rows(0)
    for g in range(1, N_GROUPS):
        el = jnp.where(g_sel == g, group_rows(g), el)
    ee = jnp.exp(el - cmax(el))
    eprob = ee / csum(ee)
    p1 = cmax(eprob)
    i1 = first_row(eprob == p1)
    eprob2 = jnp.where(sub == i1, -1.0, eprob)
    p2 = cmax(eprob2)
    i2 = first_row(eprob2 == p2)
    psum = p1 + p2
    w1 = g_p * p1 / psum
    w2 = g_p * p2 / psum
    e1 = g_sel * EXPERTS_PER_GROUP + i1
    e2 = g_sel * EXPERTS_PER_GROUP + i2

    erow = lax.broadcasted_iota(I32, (N_EXPERTS, tq), 0).astype(F32)
    oh1 = erow == e1
    oh2 = erow == e2
    cnt = (jnp.where(oh1, 1.0, 0.0) + jnp.where(oh2, 1.0, 0.0)) * jnp.where(i > 0, 1.0, 0.0)
    ri = lax.broadcasted_iota(I32, (tq, tq), 0)
    ci = lax.broadcasted_iota(I32, (tq, tq), 1)
    earlier = jnp.where(ri < ci, 1.0, 0.0).astype(BF16)
    base = wide(counts_ref[...]) + jnp.dot(cnt.astype(BF16), earlier, preferred_element_type=F32)
    r1 = csum(jnp.where(oh1, base, 0.0))
    r2 = csum(jnp.where(oh2, base, 0.0))
    counts_ref[...] = counts_ref[...] + jnp.sum(cnt, axis=1, keepdims=True)

    route = jnp.where(sub == 0.0, e1, jnp.where(sub == 1.0, e2,
                      jnp.where(sub == 2.0, r1, jnp.where(sub == 3.0, r2, 0.0))))
    route_ref[...] = route.astype(I32)
    gate_ref[...] = jnp.where(sub == 0.0, w1, jnp.where(sub == 1.0, w2, 0.0))


def _attn(x, ma, sgb, q, k, v, p, seq):
    t = x.shape[0]
    tq = TQ
    sub = tq // WINDOW
    last_blk = t // WINDOW - 1
    n_tiles = t // tq
    att = lambda i: jnp.minimum(i, n_tiles - 1)
    post = lambda i: jnp.maximum(i - 1, 0)
    att_row = lambda w: pl.BlockSpec((tq, w), lambda i: (att(i), 0))
    row = lambda w: pl.BlockSpec((tq, w), lambda i: (post(i), 0))
    prev = pl.BlockSpec((WINDOW, KV_W), lambda i: (jnp.maximum(att(i) * sub - 1, 0), 0))
    nxt = pl.BlockSpec((WINDOW, KV_W), lambda i: (jnp.minimum((att(i) + 1) * sub, last_blk), 0))
    return pl.pallas_call(
        functools.partial(_attn_kernel, tiles_per_seq=seq // tq, n_tiles=n_tiles),
        grid=(n_tiles + 1,),
        in_specs=[pl.BlockSpec(memory_space=pltpu.SMEM),
                  row(D_MODEL), row(D_MODEL), row(D_MODEL), att_row(Q_W),
                  prev, att_row(KV_W), nxt, prev, att_row(KV_W), nxt,
                  _const_spec((Q_W, D_MODEL)), _const_spec((D_MODEL, D_MODEL)), _const_spec((1, D_MODEL)),
                  _const_spec((ROUTER_ROWS, D_MODEL)), _const_spec((ROUTER_ROWS, LANES))],
        out_specs=[row(D_MODEL), _rows3_spec(tq, lambda i: (post(i), 0)),
                   pl.BlockSpec((SUBLANES, tq), lambda i: (0, post(i))),
                   pl.BlockSpec((SUBLANES, tq), lambda i: (0, post(i))),
                   pl.BlockSpec((N_EXPERTS, LANES), lambda i: (0, 0))],
        out_shape=[jax.ShapeDtypeStruct((t, D_MODEL), F32), jax.ShapeDtypeStruct((t * ROW_LINES, LANES), I32),
                   jax.ShapeDtypeStruct((SUBLANES, t), I32), jax.ShapeDtypeStruct((SUBLANES, t), F32),
                   jax.ShapeDtypeStruct((N_EXPERTS, LANES), F32)],
        scratch_shapes=[pltpu.VMEM((tq + 2 * WINDOW, KV_W), BF16), pltpu.VMEM((tq + 2 * WINDOW, KV_W), BF16),
                        pltpu.VMEM((2, tq, Q_W), BF16),
                        pltpu.VMEM((sub * N_KV_HEADS, REP * WINDOW, 3 * WINDOW), F32),
                        pltpu.VMEM((sub * N_KV_HEADS, REP * WINDOW, 3 * WINDOW), BF16),
                        pltpu.VMEM((sub * N_KV_HEADS, REP * WINDOW, LANES), F32),
                        pltpu.VMEM((tq, D_MODEL), BF16)],
        compiler_params=pltpu.CompilerParams(dimension_semantics=("arbitrary",),
                                             vmem_limit_bytes=VMEM_LIMIT),
        name="attn",
    )(p["sink"], x, ma, sgb, q, k, k, k, v, v, v, p["w_proj_b"], p["w_out"], p["norm_ffn_g"],
      p["w_router"], p["b_router"])


def _dest_kernel(route_ref, starts_ref, dest_ref):
    route = route_ref[...].astype(F32)
    td = route.shape[1]
    sub = lax.broadcasted_iota(I32, route.shape, 0)
    erow = lax.broadcasted_iota(I32, (N_EXPERTS, td), 0).astype(F32)
    starts = jnp.broadcast_to(starts_ref[:, 0:1], (N_EXPERTS, td))

    def slot(k):
        start = jnp.sum(jnp.where(erow == route[k:k + 1], starts, 0.0), axis=0, keepdims=True)
        return start + route[TOP_K + k:TOP_K + k + 1]

    dest_ref[...] = jnp.where(sub == 0, slot(0), jnp.where(sub == 1, slot(1), 0.0)).astype(I32)


def _dest(route, pad_starts):
    t = route.shape[1]
    td = min(TM_DEST, t)
    return pl.pallas_call(
        _dest_kernel,
        grid=(t // td,),
        in_specs=[pl.BlockSpec((SUBLANES, td), lambda i: (0, i)), _const_spec((N_EXPERTS, LANES))],
        out_specs=pl.BlockSpec((SUBLANES, td), lambda i: (0, i)),
        out_shape=jax.ShapeDtypeStruct((SUBLANES, t), I32),
        compiler_params=pltpu.CompilerParams(dimension_semantics=("parallel",)),
        name="dest",
    )(route, pad_starts)


def _sc_mesh():
    return plsc.VectorSubcoreMesh(core_axis_name="c", subcore_axis_name="s")


def _sc_worker():
    return lax.axis_index("s") * SC_CORES + lax.axis_index("c")


def _sc_chunk(t):
    return min(SC_CHUNK_MAX, t // (SC_WORKERS * SUBLANES))


def _dispatch(hn, dests, n_rows):
    t = hn.shape[0]
    chunk = dests[0].shape[1]
    per_worker = t // SC_WORKERS
    n_chunks = per_worker // chunk
    idx = pltpu.VMEM((n_chunks, chunk), I32)

    @functools.partial(
        pl.kernel, mesh=_sc_mesh(), out_type=jax.ShapeDtypeStruct((n_rows,) + ROW3, I32),
        scratch_types=[idx, idx, pltpu.VMEM((chunk,) + ROW3, I32), pltpu.SemaphoreType.DMA])
    def scatter_rows(hn_hbm, d0_hbm, d1_hbm, xs_hbm, i0_v, i1_v, rows_v, sem):
        w = _sc_worker()
        pltpu.sync_copy(d0_hbm.at[pl.ds(w * n_chunks, n_chunks)], i0_v)
        pltpu.sync_copy(d1_hbm.at[pl.ds(w * n_chunks, n_chunks)], i1_v)

        @pl.loop(0, n_chunks)
        def _(j):
            pltpu.sync_copy(hn_hbm.at[pl.ds(w * per_worker + j * chunk, chunk)], rows_v)
            copies = [pltpu.make_async_copy(rows_v, xs_hbm.at[i_v.at[j]], sem) for i_v in (i0_v, i1_v)]
            for cp in copies:
                cp.start()
            for cp in copies:
                cp.wait()

    return scatter_rows(hn, *dests)


def _moe_block_rows(t):
    rows = MOE_ROWS_MAX
    while rows > CHUNK and (t * TOP_K) // N_EXPERTS < MOE_MIN_BLOCKS * rows:
        rows //= 2
    return rows


def _moe_kernel(be_ref, nvalid_ref, nused_ref, xs_ref, wg_ref, wu_ref, wd_ref, yb_ref, *w16_refs):
    i = pl.program_id(0)
    used = i < nused_ref[0]
    weights = w16_refs if w16_refs else (wg_ref, wu_ref, wd_ref)

    @pl.when(jnp.logical_not(used))
    def _():
        yb_ref[...] = jnp.zeros_like(yb_ref)

    @pl.when(used)
    def _():
        if w16_refs:
            @pl.when(jnp.logical_or(i == 0, be_ref[i] != be_ref[jnp.maximum(i - 1, 0)]))
            def _():
                for src, dst in zip((wg_ref, wu_ref, wd_ref), w16_refs):
                    dst[0] = src[0].astype(BF16)

        row = lax.broadcasted_iota(I32, (xs_ref.shape[0] // ROW_LINES, 1), 0)
        x = jnp.where(row < nvalid_ref[i], _load_rows3(xs_ref), 0.0).astype(BF16)
        gate = jnp.dot(x, weights[0][0], preferred_element_type=F32)
        up = jnp.dot(x, weights[1][0], preferred_element_type=F32)
        hid = (jax.nn.silu(gate) * up).astype(BF16)
        _store_rows3(yb_ref, jnp.dot(hid, weights[2][0], preferred_element_type=F32))


def _moe(block_e, n_valid, n_used, xs, experts, block_rows):
    n_blocks = xs.shape[0] // (block_rows * ROW_LINES)
    cast = experts[0].dtype != BF16

    def rows(i, be, nv, nu):
        return (jnp.minimum(i, nu[0] - 1), 0)

    def expert(i, be, nv, nu):
        return (be[jnp.minimum(i, nu[0] - 1)], 0, 0)

    w_specs = [pl.BlockSpec((1,) + w.shape[1:], expert) for w in experts]
    out_specs = [_rows3_spec(block_rows, lambda i, be, nv, nu: (i, 0))]
    out_shape = [jax.ShapeDtypeStruct(xs.shape, I32)]
    if cast:
        out_specs += w_specs
        out_shape += [jax.ShapeDtypeStruct(w.shape, BF16) for w in experts]
    yb, *w16 = pl.pallas_call(
        _moe_kernel,
        grid_spec=pltpu.PrefetchScalarGridSpec(
            num_scalar_prefetch=3,
            grid=(n_blocks,),
            in_specs=[_rows3_spec(block_rows, rows)] + w_specs,
            out_specs=out_specs,
        ),
        out_shape=out_shape,
        compiler_params=pltpu.CompilerParams(dimension_semantics=("arbitrary",),
                                             vmem_limit_bytes=VMEM_LIMIT),
        name="moe",
    )(block_e, n_valid, n_used, xs, *experts)
    return yb, (tuple(w16) if cast else experts)


def _gather(yb, dests):
    chunk = dests[0].shape[1]
    t = dests[0].shape[0] * chunk
    per_worker = t // SC_WORKERS
    n_chunks = per_worker // chunk
    idx = pltpu.VMEM((n_chunks, chunk), I32)
    out = jax.ShapeDtypeStruct((t,) + ROW3, I32)

    @functools.partial(
        pl.kernel, mesh=_sc_mesh(), out_type=(out, out),
        scratch_types=[idx, idx, pltpu.VMEM((chunk,) + ROW3, I32), pltpu.SemaphoreType.DMA])
    def gather_rows(yb_hbm, d0_hbm, d1_hbm, y0_hbm, y1_hbm, i0_v, i1_v, rows_v, sem):
        w = _sc_worker()
        pltpu.sync_copy(d0_hbm.at[pl.ds(w * n_chunks, n_chunks)], i0_v)
        pltpu.sync_copy(d1_hbm.at[pl.ds(w * n_chunks, n_chunks)], i1_v)

        @pl.loop(0, n_chunks)
        def _(j):
            rows = pl.ds(w * per_worker + j * chunk, chunk)
            for i_v, y_hbm in ((i0_v, y0_hbm), (i1_v, y1_hbm)):
                pltpu.async_copy(yb_hbm.at[i_v.at[j]], rows_v, sem).wait()
                pltpu.sync_copy(rows_v, y_hbm.at[rows])

    return gather_rows(yb, *dests)


def _combine_kernel(x1_ref, gate_ref, y0_ref, y1_ref, out_ref):
    tm = x1_ref.shape[0]
    gate = jnp.concatenate([gate_ref[...], jnp.zeros((LANES - SUBLANES, tm), F32)], axis=0).T
    out_ref[...] = x1_ref[...] + (_load_rows3(y0_ref) * gate[:, 0:1] + _load_rows3(y1_ref) * gate[:, 1:2])


def _combine(x1, gate, y0, y1):
    t = x1.shape[0]
    tm = TM_ROW
    return pl.pallas_call(
        _combine_kernel,
        grid=(t // tm,),
        in_specs=[pl.BlockSpec((tm, D_MODEL), lambda i: (i, 0)),
                  pl.BlockSpec((SUBLANES, tm), lambda i: (0, i)),
                  _rows3_spec(tm, lambda i: (i, 0)), _rows3_spec(tm, lambda i: (i, 0))],
        out_specs=pl.BlockSpec((tm, D_MODEL), lambda i: (i, 0)),
        out_shape=jax.ShapeDtypeStruct((t, D_MODEL), F32),
        compiler_params=pltpu.CompilerParams(dimension_semantics=("parallel",)),
        name="combine",
    )(x1, gate, y0, y1)


def _rope_tables(seq):
    half = HEAD_DIM // 2
    inv_freq = ROPE_THETA ** (-jnp.arange(half, dtype=F32) / half)
    coarse = (jnp.arange(seq // ROPE_SPLIT) * ROPE_SPLIT).astype(F32)[:, None, None] * inv_freq
    fine = jnp.arange(ROPE_SPLIT).astype(F32)[None, :, None] * inv_freq
    cos = (jnp.cos(coarse) * jnp.cos(fine) - jnp.sin(coarse) * jnp.sin(fine)).reshape(seq, half)
    sin = (jnp.sin(coarse) * jnp.cos(fine) + jnp.cos(coarse) * jnp.sin(fine)).reshape(seq, half)
    return jnp.concatenate([cos, cos], axis=-1), jnp.concatenate([-sin, sin], axis=-1)


def _layer(x, p, seq, rope, experts):
    t = x.shape[0]
    ma, sgb, q, k, v = _inproj(x, p, *rope, seq)
    x1, hn, route, gate, counts_f = _attn(x, ma, sgb, q, k, v, p, seq)

    block_rows = _moe_block_rows(t)
    counts = counts_f[:, 0].astype(I32)
    min_blocks = 1 if experts[0].dtype != BF16 else 0
    padded = jnp.maximum((counts + block_rows - 1) // block_rows, min_blocks) * block_rows
    pad_ends = jnp.cumsum(padded)
    pad_starts = pad_ends - padded
    n_blocks = (t * TOP_K) // block_rows + N_EXPERTS
    block_start = jnp.arange(n_blocks, dtype=I32) * block_rows
    in_expert = jnp.logical_and(block_start[:, None] >= pad_starts[None, :],
                                block_start[:, None] < pad_ends[None, :]).astype(I32)
    block_e = jnp.minimum(jnp.sum((block_start[:, None] >= pad_ends[None, :]).astype(I32), axis=1), N_EXPERTS - 1)
    n_valid = jnp.sum(in_expert * jnp.clip(pad_starts + counts - block_start[:, None], 0, block_rows), axis=1)
    n_used = pad_ends[-1:] // block_rows
    starts_col = jnp.broadcast_to(pad_starts.astype(F32)[:, None], (N_EXPERTS, LANES))

    dest = _dest(route, starts_col)
    dests = [dest[k].reshape(t // _sc_chunk(t), _sc_chunk(t)) for k in range(TOP_K)]
    xs = _dispatch(_as_rows3(hn), dests, n_blocks * block_rows)
    yb, experts = _moe(block_e, n_valid, n_used, _as_lines(xs), experts, block_rows)
    y0, y1 = _gather(_as_rows3(yb), dests)
    return _combine(x1, gate, _as_lines(y0), _as_lines(y1)), experts


def kernel(x_prompt, x_sample, norm_mix_g, w_in, norm_v_g, w_spatial, b_spatial, q_norm_g, k_norm_g, sink,
           w_proj_a, w_proj_b, w_out, norm_ffn_g, w_router_group, b_router_group, w_router_expert,
           b_router_expert, w_gate_e, w_up_e, w_down_e):
    depth = w_in.shape[0]
    layers = []
    for l in range(depth):
        w_router = jnp.zeros((ROUTER_ROWS, D_MODEL), F32)
        w_router = w_router.at[:N_GROUPS].set(w_router_group[l].T)
        w_router = w_router.at[SUBLANES:SUBLANES + N_EXPERTS].set(w_router_expert[l].T)
        b_router = jnp.zeros((ROUTER_ROWS,), F32)
        b_router = b_router.at[:N_GROUPS].set(b_router_group[l])
        b_router = b_router.at[SUBLANES:SUBLANES + N_EXPERTS].set(b_router_expert[l])
        b_router = jnp.broadcast_to(b_router[:, None], (ROUTER_ROWS, LANES))
        layers.append(dict(
            norm_mix_g=norm_mix_g[l][None], w_in=w_in[l].astype(BF16), norm_v_g=norm_v_g[l][None],
            w_spatial=w_spatial[l].astype(BF16),
            b_spatial=jnp.broadcast_to(b_spatial[l][:, :, None], (A_GROUPS, CHUNK, LANES)),
            q_norm_g=q_norm_g[l][None], k_norm_g=k_norm_g[l][None], sink=sink[l],
            w_proj_a=w_proj_a[l].astype(BF16), w_proj_b=w_proj_b[l].astype(BF16), w_out=w_out[l].astype(BF16),
            norm_ffn_g=norm_ffn_g[l][None], w_router=w_router.astype(BF16), b_router=b_router,
            experts=(w_gate_e[l], w_up_e[l], w_down_e[l])))

    trunks = [x_prompt, x_sample]
    rows = [x.reshape(-1, D_MODEL) for x in trunks]
    order = sorted(range(len(trunks)), key=lambda n: -rows[n].shape[0])
    rope = _rope_tables(max(x.shape[1] for x in trunks))
    for p in layers:
        experts = p["experts"]
        for n in order:
            rows[n], experts = _layer(rows[n], p, trunks[n].shape[1], rope, experts)
    return tuple(r.reshape(x.shape) for r, x in zip(rows, trunks))
```

```python
import functools

import jax
import jax.numpy as jnp
from jax import lax
from jax.experimental import pallas as pl
from jax.experimental.pallas import tpu as pltpu
from jax.experimental.pallas import tpu_sc as plsc

F32 = jnp.float32
BF16 = jnp.bfloat16
I32 = jnp.int32
U32 = jnp.uint32

LANES = 128
SUBLANES = 8
VMEM_BYTES_V7X = 64 * 1024 * 1024
SC_CORES = 2
SC_SUBCORES = 16
SC_WORKERS = SC_CORES * SC_SUBCORES
SC_CHUNK_MAX = 128

D_MODEL = 1024
A_WIDTH = D_MODEL
A_GROUPS = 8
CHUNK = 128
HEAD_DIM = 128
N_Q_HEADS = D_MODEL // HEAD_DIM
N_KV_HEADS = 2
REP = N_Q_HEADS // N_KV_HEADS
WINDOW = 128
ROPE_THETA = 10000.0
ROPE_SPLIT = 64
Q_W = N_Q_HEADS * HEAD_DIM
KV_W = N_KV_HEADS * HEAD_DIM
IN_W = 2 * A_WIDTH + Q_W + 2 * KV_W + 2 * D_MODEL
COL_U = 0
COL_V = COL_U + A_WIDTH
COL_Q = COL_V + A_WIDTH
COL_K = COL_Q + Q_W
COL_VA = COL_K + KV_W
COL_GA = COL_VA + KV_W
COL_GB = COL_GA + D_MODEL
N_GROUPS = 4
EXPERTS_PER_GROUP = 8
N_EXPERTS = N_GROUPS * EXPERTS_PER_GROUP
TOP_K = 2
EPS = 1e-6
NEG = -1e30

TM_IN = 512
TM_IN_SUB = 256
TQ = 512
DENSE_COLS = 256
ROUTER_ROWS = 64
assert EXPERTS_PER_GROUP == SUBLANES and SUBLANES + N_EXPERTS <= ROUTER_ROWS
TM_ROW = 1024
TM_DEST = 8192
MOE_ROWS_MAX = 1024
MOE_MIN_BLOCKS = 1
VMEM_RESERVE = 8 * 1024 * 1024
VMEM_LIMIT = VMEM_BYTES_V7X - VMEM_RESERVE


def _rms(x, g):
    return x * lax.rsqrt(jnp.mean(x * x, axis=-1, keepdims=True) + EPS) * g


ROW_LINES = D_MODEL // 2 // LANES
ROW3 = (ROW_LINES, LANES)
HIGH_HALF = 0xFFFF0000


def _as_rows3(a):
    return a.reshape((a.shape[0] // ROW_LINES,) + ROW3)


def _as_lines(a):
    return a.reshape((a.shape[0] * ROW_LINES, LANES))


def _store_rows3(lines_ref, val):
    rows = val.shape[0]
    bits = lax.bitcast_convert_type(val.astype(BF16).astype(F32), U32)
    half = ROW_LINES * LANES
    for s in range(ROW_LINES):
        lo = bits[:, s * LANES:(s + 1) * LANES] >> 16
        hi = bits[:, half + s * LANES:half + (s + 1) * LANES] & U32(HIGH_HALF)
        lines_ref[pl.ds(s, rows, stride=ROW_LINES), :] = lax.bitcast_convert_type(lo | hi, I32)


def _load_rows3(lines_ref):
    rows = lines_ref.shape[0] // ROW_LINES
    words = [lax.bitcast_convert_type(lines_ref[pl.ds(s, rows, stride=ROW_LINES), :], U32)
             for s in range(ROW_LINES)]
    lo = [lax.bitcast_convert_type(w << 16, F32) for w in words]
    hi = [lax.bitcast_convert_type(w & U32(HIGH_HALF), F32) for w in words]
    return jnp.concatenate(lo + hi, axis=1)


def _rows3_spec(rows, index_map):
    return pl.BlockSpec((rows * ROW_LINES, LANES), index_map)


def _const_spec(shape):
    nd = len(shape)
    return pl.BlockSpec(shape, lambda *_: (0,) * nd, pipeline_mode=pl.Buffered(1))


def _inproj_kernel(x_ref, gmix_ref, win_ref, gv_ref, ws_ref, bs_ref, gq_ref, gk_ref, cos_ref, sin_ref,
                   wpa_ref, ma_ref, sgb_ref, q_ref, k_ref, v_ref, h_scr, u_scr, vn_scr, a_scr):
    tm = x_ref.shape[0]

    def stages(rows):
        def proj(lo, width):
            return jnp.dot(h_scr[rows], win_ref[:, lo:lo + width], preferred_element_type=F32)

        def norm_rope(z, g):
            zn = _rms(z, g)
            return zn * cos_ref[rows] + pltpu.roll(zn, HEAD_DIM // 2, 1) * sin_ref[rows]

        def norm():
            h_scr[rows] = _rms(x_ref[rows], gmix_ref[...]).astype(BF16)

        def mix_v():
            vn_scr[rows] = _rms(jax.nn.gelu(proj(COL_V, A_WIDTH)), gv_ref[...]).astype(BF16)

        def mix_u():
            u_scr[rows] = jax.nn.gelu(proj(COL_U, A_WIDTH))

        def spatial():
            for c in range(rows.start, rows.stop, CHUNK):
                chunk = slice(c, c + CHUNK)
                for g in range(A_GROUPS):
                    cols = slice(g * LANES, (g + 1) * LANES)
                    mixed = jnp.dot(ws_ref[g], vn_scr[chunk, cols], preferred_element_type=F32) + bs_ref[g]
                    a_scr[chunk, cols] = (u_scr[chunk, cols] * mixed).astype(BF16)

        def gate_a():
            ya = jnp.dot(a_scr[rows], wpa_ref[...], preferred_element_type=F32)
            ma_ref[rows] = jax.nn.sigmoid(proj(COL_GA, D_MODEL)) * ya

        def gate_b():
            sgb_ref[rows] = jax.nn.sigmoid(proj(COL_GB, D_MODEL))

        def queries():
            qz = proj(COL_Q, Q_W)
            for hd in range(N_Q_HEADS):
                cols = slice(hd * HEAD_DIM, (hd + 1) * HEAD_DIM)
                q_ref[rows, cols] = norm_rope(qz[:, cols], gq_ref[...]).astype(BF16)

        def keys_values():
            kz = proj(COL_K, KV_W)
            for hd in range(N_KV_HEADS):
                cols = slice(hd * HEAD_DIM, (hd + 1) * HEAD_DIM)
                k_ref[rows, cols] = norm_rope(kz[:, cols], gk_ref[...]).astype(BF16)
            v_ref[rows] = proj(COL_VA, KV_W).astype(BF16)

        return [norm, mix_v, queries, mix_u, gate_b, keys_values, spatial, gate_a]

    subs = [stages(slice(r, r + TM_IN_SUB)) for r in range(0, tm, TM_IN_SUB)]
    n_stage = len(subs[0])
    for step in range(n_stage + len(subs) - 1):
        for n, sub in enumerate(subs):
            if 0 <= step - n < n_stage:
                sub[step - n]()


def _inproj(x, p, cos, sin, seq):
    t = x.shape[0]
    tm = TM_IN
    n_pos = seq // tm
    row = lambda w: pl.BlockSpec((tm, w), lambda i: (i, 0))
    pos = pl.BlockSpec((tm, HEAD_DIM), lambda i: (i % n_pos, 0))
    return pl.pallas_call(
        _inproj_kernel,
        grid=(t // tm,),
        in_specs=[row(D_MODEL), _const_spec((1, D_MODEL)), _const_spec((D_MODEL, IN_W)),
                  _const_spec((1, A_WIDTH)), _const_spec((A_GROUPS, CHUNK, CHUNK)),
                  _const_spec((A_GROUPS, CHUNK, LANES)), _const_spec((1, HEAD_DIM)),
                  _const_spec((1, HEAD_DIM)), pos, pos, _const_spec((A_WIDTH, D_MODEL))],
        out_specs=[row(D_MODEL), row(D_MODEL), row(Q_W), row(KV_W), row(KV_W)],
        out_shape=[jax.ShapeDtypeStruct((t, D_MODEL), F32), jax.ShapeDtypeStruct((t, D_MODEL), F32),
                   jax.ShapeDtypeStruct((t, Q_W), BF16), jax.ShapeDtypeStruct((t, KV_W), BF16),
                   jax.ShapeDtypeStruct((t, KV_W), BF16)],
        scratch_shapes=[pltpu.VMEM((tm, D_MODEL), BF16), pltpu.VMEM((tm, A_WIDTH), F32),
                        pltpu.VMEM((tm, A_WIDTH), BF16), pltpu.VMEM((tm, A_WIDTH), BF16)],
        compiler_params=pltpu.CompilerParams(dimension_semantics=("parallel",),
                                             vmem_limit_bytes=VMEM_LIMIT),
        name="inproj",
    )(x, p["norm_mix_g"], p["w_in"], p["norm_v_g"], p["w_spatial"], p["b_spatial"], p["q_norm_g"],
      p["k_norm_g"], cos, sin, p["w_proj_a"])


def _attn_kernel(sink_ref, x_ref, ma_ref, sgb_ref, q_ref, kp_ref, kc_ref, kn_ref, vp_ref, vc_ref, vn_ref,
                 wpb_ref, wout_ref, gffn_ref, wr_ref, br_ref,
                 x1_ref, hn_ref, route_ref, gate_ref, counts_ref, kcat, vcat, o_scr, s_scr, p_scr, sink_scr, m_scr,
                 *, tiles_per_seq, n_tiles):
    tq = x_ref.shape[0]
    blk = WINDOW
    i = pl.program_id(0)
    slot = i % 2

    @pl.when(i == 0)
    def _():
        o_scr[...] = jnp.zeros_like(o_scr)
        counts_ref[...] = jnp.zeros_like(counts_ref)

    pos_tile = jnp.minimum(i, n_tiles - 1) % tiles_per_seq
    has_prev = pos_tile > 0
    has_next = pos_tile < tiles_per_seq - 1

    kcat[0:blk] = kp_ref[...]
    kcat[blk:blk + tq] = kc_ref[...]
    kcat[blk + tq:] = kn_ref[...]
    vcat[0:blk] = vp_ref[...]
    vcat[blk:blk + tq] = vc_ref[...]
    vcat[blk + tq:] = vn_ref[...]

    qr = lax.broadcasted_iota(I32, (blk, blk), 0)
    kc = lax.broadcasted_iota(I32, (blk, blk), 1)
    scale = HEAD_DIM ** -0.5
    n_sub = tq // blk
    pairs = [(j, g) for j in range(n_sub) for g in range(N_KV_HEADS)]

    def keys(ref, j, g):
        return ref[j * blk:(j + 3) * blk, g * HEAD_DIM:(g + 1) * HEAD_DIM]

    def head_cols(g, r):
        hd = g * REP + r
        return slice(hd * HEAD_DIM, (hd + 1) * HEAD_DIM)

    def scores(b):
        j, g = pairs[b]
        rows = slice(j * blk, (j + 1) * blk)
        qs = jnp.concatenate([q_ref[rows, head_cols(g, r)] for r in range(REP)], axis=0)
        s_scr[b] = lax.dot_general(qs, keys(kcat, j, g), (((1,), (1,)), ((), ())),
                                   preferred_element_type=F32)

    log2e = 1.4426950408889634

    def softmax(b, r):
        j, g = pairs[b]
        hrows = slice(r * blk, (r + 1) * blk)
        z = s_scr[b, hrows, :] * (scale * log2e)
        lo_ok = kc >= (qr + jnp.where(has_prev, 0, blk) if j == 0 else qr)
        hi_ok = kc <= (qr - jnp.where(has_next, 0, blk) if j == n_sub - 1 else qr)
        z = jnp.concatenate([jnp.where(lo_ok, z[:, :blk], NEG), z[:, blk:2 * blk],
                             jnp.where(hi_ok, z[:, 2 * blk:], NEG)], axis=1)
        sink = sink_ref[g * REP + r] * log2e
        m = jnp.maximum(jnp.max(z, axis=-1, keepdims=True), sink)
        p_scr[b, hrows, :] = jnp.exp2(z - m).astype(BF16)
        sink_scr[b, hrows, :] = jnp.broadcast_to(jnp.exp2(sink - m), (blk, LANES))

    def values(b):
        j, g = pairs[b]
        rows = slice(j * blk, (j + 1) * blk)
        v_ext = jnp.concatenate([keys(vcat, j, g), jnp.ones((3 * blk, HEAD_DIM), BF16)], axis=1)
        acc = jnp.dot(p_scr[b], v_ext, preferred_element_type=F32)
        o = (acc[:, :HEAD_DIM] / (acc[:, HEAD_DIM:] + sink_scr[b])).astype(BF16)
        for r in range(REP):
            o_scr[slot, rows, head_cols(g, r)] = o[r * blk:(r + 1) * blk, :]

    def merged_cols(cols):
        yb = jnp.dot(o_scr[1 - slot], wpb_ref[:, cols], preferred_element_type=F32)
        m_scr[:, cols] = (ma_ref[:, cols] + sgb_ref[:, cols] * yb).astype(BF16)

    def x1_cols(cols):
        x1_ref[:, cols] = x_ref[:, cols] + jnp.dot(m_scr[...], wout_ref[:, cols], preferred_element_type=F32)

    col_chunks = [slice(c * DENSE_COLS, (c + 1) * DENSE_COLS) for c in range(D_MODEL // DENSE_COLS)]
    dense = [functools.partial(f, cols) for f in (merged_cols, x1_cols) for cols in col_chunks]
    units = [(b, r) for b in range(len(pairs)) for r in range(REP)]
    units_per_dense = len(units) // len(dense)
    scores(0)
    for n, (b, r) in enumerate(units):
        if r == 0 and b + 1 < len(pairs):
            scores(b + 1)
        softmax(b, r)
        if (n + 1) % units_per_dense == 0:
            dense[(n + 1) // units_per_dense - 1]()
        if r == REP - 1:
            values(b)

    hn = _rms(x1_ref[...], gffn_ref[...])
    _store_rows3(hn_ref, hn)
    def wide(a):
        return jnp.concatenate([a] * (tq // LANES), axis=1)

    logits = lax.dot_general(wr_ref[...], hn.astype(BF16), (((1,), (1,)), ((), ())),
                             preferred_element_type=F32) + wide(br_ref[...])
    sub = lax.broadcasted_iota(I32, (SUBLANES, tq), 0).astype(F32)
    ninf = -jnp.inf

    def cmax(a):
        return jnp.max(a, axis=0, keepdims=True)

    def csum(a):
        return jnp.sum(a, axis=0, keepdims=True)

    def first_row(mask):
        return jnp.min(jnp.where(mask, sub, float(SUBLANES)), axis=0, keepdims=True)

    def group_rows(g):
        return logits[(g + 1) * SUBLANES:(g + 2) * SUBLANES]

    gl = jnp.where(sub < N_GROUPS, logits[0:SUBLANES], ninf)
    gmax = cmax(gl)
    g_sel = first_row(gl == gmax)
    g_p = 1.0 / csum(jnp.exp(gl - gmax))
    el = group_rows(0)
    for g in range(1, N_GROUPS):
        el = jnp.where(g_sel == g, group_rows(g), el)
    ee = jnp.exp(el - cmax(el))
    eprob = ee / csum(ee)
    p1 = cmax(eprob)
    i1 = first_row(eprob == p1)
    eprob2 = jnp.where(sub == i1, -1.0, eprob)
    p2 = cmax(eprob2)
    i2 = first_row(eprob2 == p2)
    psum = p1 + p2
    w1 = g_p * p1 / psum
    w2 = g_p * p2 / psum
    e1 = g_sel * EXPERTS_PER_GROUP + i1
    e2 = g_sel * EXPERTS_PER_GROUP + i2

    erow = lax.broadcasted_iota(I32, (N_EXPERTS, tq), 0).astype(F32)
    oh1 = erow == e1
    oh2 = erow == e2
    cnt = (jnp.where(oh1, 1.0, 0.0) + jnp.where(oh2, 1.0, 0.0)) * jnp.where(i > 0, 1.0, 0.0)
    ri = lax.broadcasted_iota(I32, (tq, tq), 0)
    ci = lax.broadcasted_iota(I32, (tq, tq), 1)
    earlier = jnp.where(ri < ci, 1.0, 0.0).astype(BF16)
    base = wide(counts_ref[...]) + jnp.dot(cnt.astype(BF16), earlier, preferred_element_type=F32)
    r1 = csum(jnp.where(oh1, base, 0.0))
    r2 = csum(jnp.where(oh2, base, 0.0))
    counts_ref[...] = counts_ref[...] + jnp.sum(cnt, axis=1, keepdims=True)

    route = jnp.where(sub == 0.0, e1, jnp.where(sub == 1.0, e2,
                      jnp.where(sub == 2.0, r1, jnp.where(sub == 3.0, r2, 0.0))))
    route_ref[...] = route.astype(I32)
    gate_ref[...] = jnp.where(sub == 0.0, w1, jnp.where(sub == 1.0, w2, 0.0))


def _attn(x, ma, sgb, q, k, v, p, seq):
    t = x.shape[0]
    tq = TQ
    sub = tq // WINDOW
    last_blk = t // WINDOW - 1
    n_tiles = t // tq
    att = lambda i: jnp.minimum(i, n_tiles - 1)
    post = lambda i: jnp.maximum(i - 1, 0)
    att_row = lambda w: pl.BlockSpec((tq, w), lambda i: (att(i), 0))
    row = lambda w: pl.BlockSpec((tq, w), lambda i: (post(i), 0))
    prev = pl.BlockSpec((WINDOW, KV_W), lambda i: (jnp.maximum(att(i) * sub - 1, 0), 0))
    nxt = pl.BlockSpec((WINDOW, KV_W), lambda i: (jnp.minimum((att(i) + 1) * sub, last_blk), 0))
    return pl.pallas_call(
        functools.partial(_attn_kernel, tiles_per_seq=seq // tq, n_tiles=n_tiles),
        grid=(n_tiles + 1,),
        in_specs=[pl.BlockSpec(memory_space=pltpu.SMEM),
                  row(D_MODEL), row(D_MODEL), row(D_MODEL), att_row(Q_W),
                  prev, att_row(KV_W), nxt, prev, att_row(KV_W), nxt,
                  _const_spec((Q_W, D_MODEL)), _const_spec((D_MODEL, D_MODEL)), _const_spec((1, D_MODEL)),
                  _const_spec((ROUTER_ROWS, D_MODEL)), _const_spec((ROUTER_ROWS, LANES))],
        out_specs=[row(D_MODEL), _rows3_spec(tq, lambda i: (post(i), 0)),
                   pl.BlockSpec((SUBLANES, tq), lambda i: (0, post(i))),
                   pl.BlockSpec((SUBLANES, tq), lambda i: (0, post(i))),
                   pl.BlockSpec((N_EXPERTS, LANES), lambda i: (0, 0))],
        out_shape=[jax.ShapeDtypeStruct((t, D_MODEL), F32), jax.ShapeDtypeStruct((t * ROW_LINES, LANES), I32),
                   jax.ShapeDtypeStruct((SUBLANES, t), I32), jax.ShapeDtypeStruct((SUBLANES, t), F32),
                   jax.ShapeDtypeStruct((N_EXPERTS, LANES), F32)],
        scratch_shapes=[pltpu.VMEM((tq + 2 * WINDOW, KV_W), BF16), pltpu.VMEM((tq + 2 * WINDOW, KV_W), BF16),
                        pltpu.VMEM((2, tq, Q_W), BF16),
                        pltpu.VMEM((sub * N_KV_HEADS, REP * WINDOW, 3 * WINDOW), F32),
                        pltpu.VMEM((sub * N_KV_HEADS, REP * WINDOW, 3 * WINDOW), BF16),
                        pltpu.VMEM((sub * N_KV_HEADS, REP * WINDOW, LANES), F32),
                        pltpu.VMEM((tq, D_MODEL), BF16)],
        compiler_params=pltpu.CompilerParams(dimension_semantics=("arbitrary",),
                                             vmem_limit_bytes=VMEM_LIMIT),
        name="attn",
    )(p["sink"], x, ma, sgb, q, k, k, k, v, v, v, p["w_proj_b"], p["w_out"], p["norm_ffn_g"],
      p["w_router"], p["b_router"])


def _dest_kernel(route_ref, starts_ref, dest_ref):
    route = route_ref[...].astype(F32)
    td = route.shape[1]
    sub = lax.broadcasted_iota(I32, route.shape, 0)
    erow = lax.broadcasted_iota(I32, (N_EXPERTS, td), 0).astype(F32)
    starts = jnp.broadcast_to(starts_ref[:, 0:1], (N_EXPERTS, td))

    def slot(k):
        start = jnp.sum(jnp.where(erow == route[k:k + 1], starts, 0.0), axis=0, keepdims=True)
        return start + route[TOP_K + k:TOP_K + k + 1]

    dest_ref[...] = jnp.where(sub == 0, slot(0), jnp.where(sub == 1, slot(1), 0.0)).astype(I32)


def _dest(route, pad_starts):
    t = route.shape[1]
    td = min(TM_DEST, t)
    return pl.pallas_call(
        _dest_kernel,
        grid=(t // td,),
        in_specs=[pl.BlockSpec((SUBLANES, td), lambda i: (0, i)), _const_spec((N_EXPERTS, LANES))],
        out_specs=pl.BlockSpec((SUBLANES, td), lambda i: (0, i)),
        out_shape=jax.ShapeDtypeStruct((SUBLANES, t), I32),
        compiler_params=pltpu.CompilerParams(dimension_semantics=("parallel",)),
        name="dest",
    )(route, pad_starts)


def _sc_mesh():
    return plsc.VectorSubcoreMesh(core_axis_name="c", subcore_axis_name="s")


def _sc_worker():
    return lax.axis_index("s") * SC_CORES + lax.axis_index("c")


def _sc_chunk(t):
    return min(SC_CHUNK_MAX, t // (SC_WORKERS * SUBLANES))


def _dispatch(hn, dests, n_rows):
    t = hn.shape[0]
    chunk = dests[0].shape[1]
    per_worker = t // SC_WORKERS
    n_chunks = per_worker // chunk
    idx = pltpu.VMEM((n_chunks, chunk), I32)

    @functools.partial(
        pl.kernel, mesh=_sc_mesh(), out_type=jax.ShapeDtypeStruct((n_rows,) + ROW3, I32),
        scratch_types=[idx, idx, pltpu.VMEM((chunk,) + ROW3, I32), pltpu.SemaphoreType.DMA])
    def scatter_rows(hn_hbm, d0_hbm, d1_hbm, xs_hbm, i0_v, i1_v, rows_v, sem):
        w = _sc_worker()
        pltpu.sync_copy(d0_hbm.at[pl.ds(w * n_chunks, n_chunks)], i0_v)
        pltpu.sync_copy(d1_hbm.at[pl.ds(w * n_chunks, n_chunks)], i1_v)

        @pl.loop(0, n_chunks)
        def _(j):
            pltpu.sync_copy(hn_hbm.at[pl.ds(w * per_worker + j * chunk, chunk)], rows_v)
            copies = [pltpu.make_async_copy(rows_v, xs_hbm.at[i_v.at[j]], sem) for i_v in (i0_v, i1_v)]
            for cp in copies:
                cp.start()
            for cp in copies:
                cp.wait()

    return scatter_rows(hn, *dests)


def _moe_block_rows(t):
    rows = MOE_ROWS_MAX
    while rows > CHUNK and (t * TOP_K) // N_EXPERTS < MOE_MIN_BLOCKS * rows:
        rows //= 2
    return rows


def _moe_kernel(be_ref, nvalid_ref, nused_ref, xs_ref, wg_ref, wu_ref, wd_ref, yb_ref, *w16_refs):
    i = pl.program_id(0)
    used = i < nused_ref[0]
    weights = w16_refs if w16_refs else (wg_ref, wu_ref, wd_ref)

    @pl.when(jnp.logical_not(used))
    def _():
        yb_ref[...] = jnp.zeros_like(yb_ref)

    @pl.when(used)
    def _():
        if w16_refs:
            @pl.when(jnp.logical_or(i == 0, be_ref[i] != be_ref[jnp.maximum(i - 1, 0)]))
            def _():
                for src, dst in zip((wg_ref, wu_ref, wd_ref), w16_refs):
                    dst[0] = src[0].astype(BF16)

        row = lax.broadcasted_iota(I32, (xs_ref.shape[0] // ROW_LINES, 1), 0)
        x = jnp.where(row < nvalid_ref[i], _load_rows3(xs_ref), 0.0).astype(BF16)
        half = weights[0].shape[2] // 2
        hid = []
        for cols in (slice(0, half), slice(half, 2 * half)):
            gate = jnp.dot(x, weights[0][0, :, cols], preferred_element_type=F32)
            up = jnp.dot(x, weights[1][0, :, cols], preferred_element_type=F32)
            hid.append((jax.nn.silu(gate) * up).astype(BF16))
        hid = jnp.concatenate(hid, axis=1)
        _store_rows3(yb_ref, jnp.dot(hid, weights[2][0], preferred_element_type=F32))


def _moe(block_e, n_valid, n_used, xs, experts, block_rows):
    n_blocks = xs.shape[0] // (block_rows * ROW_LINES)
    cast = experts[0].dtype != BF16

    def rows(i, be, nv, nu):
        return (jnp.minimum(i, nu[0] - 1), 0)

    def expert(i, be, nv, nu):
        return (be[jnp.minimum(i, nu[0] - 1)], 0, 0)

    w_specs = [pl.BlockSpec((1,) + w.shape[1:], expert) for w in experts]
    out_specs = [_rows3_spec(block_rows, lambda i, be, nv, nu: (i, 0))]
    out_shape = [jax.ShapeDtypeStruct(xs.shape, I32)]
    if cast:
        out_specs += w_specs
        out_shape += [jax.ShapeDtypeStruct(w.shape, BF16) for w in experts]
    yb, *w16 = pl.pallas_call(
        _moe_kernel,
        grid_spec=pltpu.PrefetchScalarGridSpec(
            num_scalar_prefetch=3,
            grid=(n_blocks,),
            in_specs=[_rows3_spec(block_rows, rows)] + w_specs,
            out_specs=out_specs,
        ),
        out_shape=out_shape,
        compiler_params=pltpu.CompilerParams(dimension_semantics=("arbitrary",),
                                             vmem_limit_bytes=VMEM_LIMIT),
        name="moe",
    )(block_e, n_valid, n_used, xs, *experts)
    return yb, (tuple(w16) if cast else experts)


def _gather(yb, dests):
    chunk = dests[0].shape[1]
    t = dests[0].shape[0] * chunk
    per_worker = t // SC_WORKERS
    n_chunks = per_worker // chunk
    idx = pltpu.VMEM((n_chunks, chunk), I32)
    out = jax.ShapeDtypeStruct((t,) + ROW3, I32)

    @functools.partial(
        pl.kernel, mesh=_sc_mesh(), out_type=(out, out),
        scratch_types=[idx, idx, pltpu.VMEM((chunk,) + ROW3, I32), pltpu.SemaphoreType.DMA])
    def gather_rows(yb_hbm, d0_hbm, d1_hbm, y0_hbm, y1_hbm, i0_v, i1_v, rows_v, sem):
        w = _sc_worker()
        pltpu.sync_copy(d0_hbm.at[pl.ds(w * n_chunks, n_chunks)], i0_v)
        pltpu.sync_copy(d1_hbm.at[pl.ds(w * n_chunks, n_chunks)], i1_v)

        @pl.loop(0, n_chunks)
        def _(j):
            rows = pl.ds(w * per_worker + j * chunk, chunk)
            for i_v, y_hbm in ((i0_v, y0_hbm), (i1_v, y1_hbm)):
                pltpu.async_copy(yb_hbm.at[i_v.at[j]], rows_v, sem).wait()
                pltpu.sync_copy(rows_v, y_hbm.at[rows])

    return gather_rows(yb, *dests)


def _combine_kernel(x1_ref, gate_ref, y0_ref, y1_ref, out_ref):
    tm = x1_ref.shape[0]
    gate = jnp.concatenate([gate_ref[...], jnp.zeros((LANES - SUBLANES, tm), F32)], axis=0).T
    out_ref[...] = x1_ref[...] + (_load_rows3(y0_ref) * gate[:, 0:1] + _load_rows3(y1_ref) * gate[:, 1:2])


def _combine(x1, gate, y0, y1):
    t = x1.shape[0]
    tm = TM_ROW
    return pl.pallas_call(
        _combine_kernel,
        grid=(t // tm,),
        in_specs=[pl.BlockSpec((tm, D_MODEL), lambda i: (i, 0)),
                  pl.BlockSpec((SUBLANES, tm), lambda i: (0, i)),
                  _rows3_spec(tm, lambda i: (i, 0)), _rows3_spec(tm, lambda i: (i, 0))],
        out_specs=pl.BlockSpec((tm, D_MODEL), lambda i: (i, 0)),
        out_shape=jax.ShapeDtypeStruct((t, D_MODEL), F32),
        compiler_params=pltpu.CompilerParams(dimension_semantics=("parallel",)),
        name="combine",
    )(x1, gate, y0, y1)


def _rope_tables(seq):
    half = HEAD_DIM // 2
    inv_freq = ROPE_THETA ** (-jnp.arange(half, dtype=F32) / half)
    coarse = (jnp.arange(seq // ROPE_SPLIT) * ROPE_SPLIT).astype(F32)[:, None, None] * inv_freq
    fine = jnp.arange(ROPE_SPLIT).astype(F32)[None, :, None] * inv_freq
    cos = (jnp.cos(coarse) * jnp.cos(fine) - jnp.sin(coarse) * jnp.sin(fine)).reshape(seq, half)
    sin = (jnp.sin(coarse) * jnp.cos(fine) + jnp.cos(coarse) * jnp.sin(fine)).reshape(seq, half)
    return jnp.concatenate([cos, cos], axis=-1), jnp.concatenate([-sin, sin], axis=-1)


def _layer(x, p, seq, rope, experts):
    t = x.shape[0]
    ma, sgb, q, k, v = _inproj(x, p, *rope, seq)
    x1, hn, route, gate, counts_f = _attn(x, ma, sgb, q, k, v, p, seq)

    block_rows = _moe_block_rows(t)
    counts = counts_f[:, 0].astype(I32)
    min_blocks = 1 if experts[0].dtype != BF16 else 0
    padded = jnp.maximum((counts + block_rows - 1) // block_rows, min_blocks) * block_rows
    pad_ends = jnp.cumsum(padded)
    pad_starts = pad_ends - padded
    n_blocks = (t * TOP_K) // block_rows + N_EXPERTS
    block_start = jnp.arange(n_blocks, dtype=I32) * block_rows
    in_expert = jnp.logical_and(block_start[:, None] >= pad_starts[None, :],
                                block_start[:, None] < pad_ends[None, :]).astype(I32)
    block_e = jnp.minimum(jnp.sum((block_start[:, None] >= pad_ends[None, :]).astype(I32), axis=1), N_EXPERTS - 1)
    n_valid = jnp.sum(in_expert * jnp.clip(pad_starts + counts - block_start[:, None], 0, block_rows), axis=1)
    n_used = pad_ends[-1:] // block_rows
    starts_col = jnp.broadcast_to(pad_starts.astype(F32)[:, None], (N_EXPERTS, LANES))

    dest = _dest(route, starts_col)
    dests = [dest[k].reshape(t // _sc_chunk(t), _sc_chunk(t)) for k in range(TOP_K)]
    xs = _dispatch(_as_rows3(hn), dests, n_blocks * block_rows)
    yb, experts = _moe(block_e, n_valid, n_used, _as_lines(xs), experts, block_rows)
    y0, y1 = _gather(_as_rows3(yb), dests)
    return _combine(x1, gate, _as_lines(y0), _as_lines(y1)), experts


def kernel(x_prompt, x_sample, norm_mix_g, w_in, norm_v_g, w_spatial, b_spatial, q_norm_g, k_norm_g, sink,
           w_proj_a, w_proj_b, w_out, norm_ffn_g, w_router_group, b_router_group, w_router_expert,
           b_router_expert, w_gate_e, w_up_e, w_down_e):
    depth = w_in.shape[0]
    layers = []
    for l in range(depth):
        w_router = jnp.zeros((ROUTER_ROWS, D_MODEL), F32)
        w_router = w_router.at[:N_GROUPS].set(w_router_group[l].T)
        w_router = w_router.at[SUBLANES:SUBLANES + N_EXPERTS].set(w_router_expert[l].T)
        b_router = jnp.zeros((ROUTER_ROWS,), F32)
        b_router = b_router.at[:N_GROUPS].set(b_router_group[l])
        b_router = b_router.at[SUBLANES:SUBLANES + N_EXPERTS].set(b_router_expert[l])
        b_router = jnp.broadcast_to(b_router[:, None], (ROUTER_ROWS, LANES))
        layers.append(dict(
            norm_mix_g=norm_mix_g[l][None], w_in=w_in[l].astype(BF16), norm_v_g=norm_v_g[l][None],
            w_spatial=w_spatial[l].astype(BF16),
            b_spatial=jnp.broadcast_to(b_spatial[l][:, :, None], (A_GROUPS, CHUNK, LANES)),
            q_norm_g=q_norm_g[l][None], k_norm_g=k_norm_g[l][None], sink=sink[l],
            w_proj_a=w_proj_a[l].astype(BF16), w_proj_b=w_proj_b[l].astype(BF16), w_out=w_out[l].astype(BF16),
            norm_ffn_g=norm_ffn_g[l][None], w_router=w_router.astype(BF16), b_router=b_router,
            experts=(w_gate_e[l], w_up_e[l], w_down_e[l])))

    trunks = [x_prompt, x_sample]
    rows = [x.reshape(-1, D_MODEL) for x in trunks]
    order = sorted(range(len(trunks)), key=lambda n: -rows[n].shape[0])
    rope = _rope_tables(max(x.shape[1] for x in trunks))
    for p in layers:
        experts = p["experts"]
        for n in order:
            rows[n], experts = _layer(rows[n], p, trunks[n].shape[1], rope, experts)
    return tuple(r.reshape(x.shape) for r, x in zip(rows, trunks))
```

```python
import functools

import jax
import jax.numpy as jnp
from jax import lax
from jax.experimental import pallas as pl
from jax.experimental.pallas import tpu as pltpu
from jax.experimental.pallas import tpu_sc as plsc

F32 = jnp.float32
BF16 = jnp.bfloat16
I32 = jnp.int32
U32 = jnp.uint32

LANES = 128
SUBLANES = 8
VMEM_BYTES_V7X = 64 * 1024 * 1024
SC_CORES = 2
SC_SUBCORES = 16
SC_WORKERS = SC_CORES * SC_SUBCORES
SC_CHUNK_MAX = 128

D_MODEL = 1024
A_WIDTH = D_MODEL
A_GROUPS = 8
CHUNK = 128
HEAD_DIM = 128
N_Q_HEADS = D_MODEL // HEAD_DIM
N_KV_HEADS = 2
REP = N_Q_HEADS // N_KV_HEADS
WINDOW = 128
ROPE_THETA = 10000.0
ROPE_SPLIT = 64
Q_W = N_Q_HEADS * HEAD_DIM
KV_W = N_KV_HEADS * HEAD_DIM
IN_W = 2 * A_WIDTH + Q_W + 2 * KV_W + 2 * D_MODEL
COL_U = 0
COL_V = COL_U + A_WIDTH
COL_Q = COL_V + A_WIDTH
COL_K = COL_Q + Q_W
COL_VA = COL_K + KV_W
COL_GA = COL_VA + KV_W
COL_GB = COL_GA + D_MODEL
N_GROUPS = 4
EXPERTS_PER_GROUP = 8
N_EXPERTS = N_GROUPS * EXPERTS_PER_GROUP
TOP_K = 2
EPS = 1e-6
NEG = -1e30

TM_IN = 512
TM_IN_SUB = 256
TQ = 512
DENSE_COLS = 256
ROUTER_ROWS = 64
assert EXPERTS_PER_GROUP == SUBLANES and SUBLANES + N_EXPERTS <= ROUTER_ROWS
TM_ROW = 1024
TM_DEST = 8192
MOE_ROWS_MAX = 1024
MOE_MIN_BLOCKS = 1
VMEM_RESERVE = 8 * 1024 * 1024
VMEM_LIMIT = VMEM_BYTES_V7X - VMEM_RESERVE


def _rms(x, g):
    return x * lax.rsqrt(jnp.mean(x * x, axis=-1, keepdims=True) + EPS) * g


ROW_LINES = D_MODEL // 2 // LANES
ROW3 = (ROW_LINES, LANES)
HIGH_HALF = 0xFFFF0000


def _as_rows3(a):
    return a.reshape((a.shape[0] // ROW_LINES,) + ROW3)


def _as_lines(a):
    return a.reshape((a.shape[0] * ROW_LINES, LANES))


def _store_rows3(lines_ref, val):
    rows = val.shape[0]
    bits = lax.bitcast_convert_type(val.astype(BF16).astype(F32), U32)
    half = ROW_LINES * LANES
    for s in range(ROW_LINES):
        lo = bits[:, s * LANES:(s + 1) * LANES] >> 16
        hi = bits[:, half + s * LANES:half + (s + 1) * LANES] & U32(HIGH_HALF)
        lines_ref[pl.ds(s, rows, stride=ROW_LINES), :] = lax.bitcast_convert_type(lo | hi, I32)


def _load_rows3(lines_ref):
    rows = lines_ref.shape[0] // ROW_LINES
    words = [lax.bitcast_convert_type(lines_ref[pl.ds(s, rows, stride=ROW_LINES), :], U32)
             for s in range(ROW_LINES)]
    lo = [lax.bitcast_convert_type(w << 16, F32) for w in words]
    hi = [lax.bitcast_convert_type(w & U32(HIGH_HALF), F32) for w in words]
    return jnp.concatenate(lo + hi, axis=1)


def _rows3_spec(rows, index_map):
    return pl.BlockSpec((rows * ROW_LINES, LANES), index_map)


def _const_spec(shape):
    nd = len(shape)
    return pl.BlockSpec(shape, lambda *_: (0,) * nd, pipeline_mode=pl.Buffered(1))


def _inproj_kernel(x_ref, gmix_ref, win_ref, gv_ref, ws_ref, bs_ref, gq_ref, gk_ref, cos_ref, sin_ref,
                   wpa_ref, ma_ref, sgb_ref, q_ref, k_ref, v_ref, h_scr, u_scr, vn_scr, a_scr):
    tm = x_ref.shape[0]

    def stages(rows):
        def proj(lo, width):
            return jnp.dot(h_scr[rows], win_ref[:, lo:lo + width], preferred_element_type=F32)

        def norm_rope(z, g):
            zn = _rms(z, g)
            return zn * cos_ref[rows] + pltpu.roll(zn, HEAD_DIM // 2, 1) * sin_ref[rows]

        def norm():
            h_scr[rows] = _rms(x_ref[rows], gmix_ref[...]).astype(BF16)

        def mix_v():
            vn_scr[rows] = _rms(jax.nn.gelu(proj(COL_V, A_WIDTH)), gv_ref[...]).astype(BF16)

        def mix_u():
            u_scr[rows] = jax.nn.gelu(proj(COL_U, A_WIDTH))

        def spatial():
            for c in range(rows.start, rows.stop, CHUNK):
                chunk = slice(c, c + CHUNK)
                for g in range(A_GROUPS):
                    cols = slice(g * LANES, (g + 1) * LANES)
                    mixed = jnp.dot(ws_ref[g], vn_scr[chunk, cols], preferred_element_type=F32) + bs_ref[g]
                    a_scr[chunk, cols] = (u_scr[chunk, cols] * mixed).astype(BF16)

        def gate_a():
            ya = jnp.dot(a_scr[rows], wpa_ref[...], preferred_element_type=F32)
            ma_ref[rows] = jax.nn.sigmoid(proj(COL_GA, D_MODEL)) * ya

        def gate_b():
            sgb_ref[rows] = jax.nn.sigmoid(proj(COL_GB, D_MODEL))

        def queries():
            qz = proj(COL_Q, Q_W)
            for hd in range(N_Q_HEADS):
                cols = slice(hd * HEAD_DIM, (hd + 1) * HEAD_DIM)
                q_ref[rows, cols] = norm_rope(qz[:, cols], gq_ref[...]).astype(BF16)

        def keys_values():
            kz = proj(COL_K, KV_W)
            for hd in range(N_KV_HEADS):
                cols = slice(hd * HEAD_DIM, (hd + 1) * HEAD_DIM)
                k_ref[rows, cols] = norm_rope(kz[:, cols], gk_ref[...]).astype(BF16)
            v_ref[rows] = proj(COL_VA, KV_W).astype(BF16)

        return [norm, mix_v, queries, mix_u, gate_b, keys_values, spatial, gate_a]

    subs = [stages(slice(r, r + TM_IN_SUB)) for r in range(0, tm, TM_IN_SUB)]
    n_stage = len(subs[0])
    for step in range(n_stage + len(subs) - 1):
        for n, sub in enumerate(subs):
            if 0 <= step - n < n_stage:
                sub[step - n]()


def _inproj(x, p, cos, sin, seq):
    t = x.shape[0]
    tm = TM_IN
    n_pos = seq // tm
    row = lambda w: pl.BlockSpec((tm, w), lambda i: (i, 0))
    pos = pl.BlockSpec((tm, HEAD_DIM), lambda i: (i % n_pos, 0))
    return pl.pallas_call(
        _inproj_kernel,
        grid=(t // tm,),
        in_specs=[row(D_MODEL), _const_spec((1, D_MODEL)), _const_spec((D_MODEL, IN_W)),
                  _const_spec((1, A_WIDTH)), _const_spec((A_GROUPS, CHUNK, CHUNK)),
                  _const_spec((A_GROUPS, CHUNK, LANES)), _const_spec((1, HEAD_DIM)),
                  _const_spec((1, HEAD_DIM)), pos, pos, _const_spec((A_WIDTH, D_MODEL))],
        out_specs=[row(D_MODEL), row(D_MODEL), row(Q_W), row(KV_W), row(KV_W)],
        out_shape=[jax.ShapeDtypeStruct((t, D_MODEL), F32), jax.ShapeDtypeStruct((t, D_MODEL), F32),
                   jax.ShapeDtypeStruct((t, Q_W), BF16), jax.ShapeDtypeStruct((t, KV_W), BF16),
                   jax.ShapeDtypeStruct((t, KV_W), BF16)],
        scratch_shapes=[pltpu.VMEM((tm, D_MODEL), BF16), pltpu.VMEM((tm, A_WIDTH), F32),
                        pltpu.VMEM((tm, A_WIDTH), BF16), pltpu.VMEM((tm, A_WIDTH), BF16)],
        compiler_params=pltpu.CompilerParams(dimension_semantics=("parallel",),
                                             vmem_limit_bytes=VMEM_LIMIT),
        name="inproj",
    )(x, p["norm_mix_g"], p["w_in"], p["norm_v_g"], p["w_spatial"], p["b_spatial"], p["q_norm_g"],
      p["k_norm_g"], cos, sin, p["w_proj_a"])


def _attn_kernel(sink_ref, x_ref, ma_ref, sgb_ref, q_ref, kp_ref, kc_ref, kn_ref, vp_ref, vc_ref, vn_ref,
                 wpb_ref, wout_ref, gffn_ref, wr_ref, br_ref,
                 x1_ref, hn_ref, route_ref, gate_ref, counts_ref, kcat, vcat, o_scr, s_scr, p_scr, sink_scr, m_scr,
                 *, tiles_per_seq, n_tiles):
    tq = x_ref.shape[0]
    blk = WINDOW
    i = pl.program_id(0)
    slot = i % 2

    @pl.when(i == 0)
    def _():
        o_scr[...] = jnp.zeros_like(o_scr)
        counts_ref[...] = jnp.zeros_like(counts_ref)

    pos_tile = jnp.minimum(i, n_tiles - 1) % tiles_per_seq
    has_prev = pos_tile > 0
    has_next = pos_tile < tiles_per_seq - 1

    kcat[0:blk] = kp_ref[...]
    kcat[blk:blk + tq] = kc_ref[...]
    kcat[blk + tq:] = kn_ref[...]
    vcat[0:blk] = vp_ref[...]
    vcat[blk:blk + tq] = vc_ref[...]
    vcat[blk + tq:] = vn_ref[...]

    qr = lax.broadcasted_iota(I32, (blk, blk), 0)
    kc = lax.broadcasted_iota(I32, (blk, blk), 1)
    scale = HEAD_DIM ** -0.5
    n_sub = tq // blk
    pairs = [(j, g) for j in range(n_sub) for g in range(N_KV_HEADS)]

    def keys(ref, j, g):
        return ref[j * blk:(j + 3) * blk, g * HEAD_DIM:(g + 1) * HEAD_DIM]

    def head_cols(g, r):
        hd = g * REP + r
        return slice(hd * HEAD_DIM, (hd + 1) * HEAD_DIM)

    def scores(b):
        j, g = pairs[b]
        rows = slice(j * blk, (j + 1) * blk)
        qs = jnp.concatenate([q_ref[rows, head_cols(g, r)] for r in range(REP)], axis=0)
        s_scr[b] = lax.dot_general(qs, keys(kcat, j, g), (((1,), (1,)), ((), ())),
                                   preferred_element_type=F32)

    log2e = 1.4426950408889634

    def softmax(b, r):
        j, g = pairs[b]
        hrows = slice(r * blk, (r + 1) * blk)
        z = s_scr[b, hrows, :] * (scale * log2e)
        lo_ok = kc >= (qr + jnp.where(has_prev, 0, blk) if j == 0 else qr)
        hi_ok = kc <= (qr - jnp.where(has_next, 0, blk) if j == n_sub - 1 else qr)
        z = jnp.concatenate([jnp.where(lo_ok, z[:, :blk], NEG), z[:, blk:2 * blk],
                             jnp.where(hi_ok, z[:, 2 * blk:], NEG)], axis=1)
        sink = sink_ref[g * REP + r] * log2e
        m = jnp.maximum(jnp.max(z, axis=-1, keepdims=True), sink)
        p_scr[b, hrows, :] = jnp.exp2(z - m).astype(BF16)
        sink_scr[b, hrows, :] = jnp.broadcast_to(jnp.exp2(sink - m), (blk, LANES))

    def values(b):
        j, g = pairs[b]
        rows = slice(j * blk, (j + 1) * blk)
        v_ext = jnp.concatenate([keys(vcat, j, g), jnp.ones((3 * blk, HEAD_DIM), BF16)], axis=1)
        acc = jnp.dot(p_scr[b], v_ext, preferred_element_type=F32)
        o = (acc[:, :HEAD_DIM] / (acc[:, HEAD_DIM:] + sink_scr[b])).astype(BF16)
        for r in range(REP):
            o_scr[slot, rows, head_cols(g, r)] = o[r * blk:(r + 1) * blk, :]

    def merged_cols(cols):
        yb = jnp.dot(o_scr[1 - slot], wpb_ref[:, cols], preferred_element_type=F32)
        m_scr[:, cols] = (ma_ref[:, cols] + sgb_ref[:, cols] * yb).astype(BF16)

    def x1_cols(cols):
        x1_ref[:, cols] = x_ref[:, cols] + jnp.dot(m_scr[...], wout_ref[:, cols], preferred_element_type=F32)

    col_chunks = [slice(c * DENSE_COLS, (c + 1) * DENSE_COLS) for c in range(D_MODEL // DENSE_COLS)]
    dense = [functools.partial(f, cols) for f in (merged_cols, x1_cols) for cols in col_chunks]
    units = [(b, r) for b in range(len(pairs)) for r in range(REP)]
    units_per_dense = len(units) // len(dense)
    scores(0)
    for n, (b, r) in enumerate(units):
        if r == 0 and b + 1 < len(pairs):
            scores(b + 1)
        softmax(b, r)
        if (n + 1) % units_per_dense == 0:
            dense[(n + 1) // units_per_dense - 1]()
        if r == 0 and b > 0:
            values(b - 1)
    values(len(pairs) - 1)

    hn = _rms(x1_ref[...], gffn_ref[...])
    _store_rows3(hn_ref, hn)
    def wide(a):
        return jnp.concatenate([a] * (tq // LANES), axis=1)

    logits = lax.dot_general(wr_ref[...], hn.astype(BF16), (((1,), (1,)), ((), ())),
                             preferred_element_type=F32) + wide(br_ref[...])
    sub = lax.broadcasted_iota(I32, (SUBLANES, tq), 0).astype(F32)
    ninf = -jnp.inf

    def cmax(a):
        return jnp.max(a, axis=0, keepdims=True)

    def csum(a):
        return jnp.sum(a, axis=0, keepdims=True)

    def first_row(mask):
        return jnp.min(jnp.where(mask, sub, float(SUBLANES)), axis=0, keepdims=True)

    def group_rows(g):
        return logits[(g + 1) * SUBLANES:(g + 2) * SUBLANES]

    gl = jnp.where(sub < N_GROUPS, logits[0:SUBLANES], ninf)
    gmax = cmax(gl)
    g_sel = first_row(gl == gmax)
    g_p = 1.0 / csum(jnp.exp(gl - gmax))
    el = group_rows(0)
    for g in range(1, N_GROUPS):
        el = jnp.where(g_sel == g, group_rows(g), el)
    ee = jnp.exp(el - cmax(el))
    eprob = ee / csum(ee)
    p1 = cmax(eprob)
    i1 = first_row(eprob == p1)
    eprob2 = jnp.where(sub == i1, -1.0, eprob)
    p2 = cmax(eprob2)
    i2 = first_row(eprob2 == p2)
    psum = p1 + p2
    w1 = g_p * p1 / psum
    w2 = g_p * p2 / psum
    e1 = g_sel * EXPERTS_PER_GROUP + i1
    e2 = g_sel * EXPERTS_PER_GROUP + i2

    erow = lax.broadcasted_iota(I32, (N_EXPERTS, tq), 0).astype(F32)
    oh1 = erow == e1
    oh2 = erow == e2
    cnt = (jnp.where(oh1, 1.0, 0.0) + jnp.where(oh2, 1.0, 0.0)) * jnp.where(i > 0, 1.0, 0.0)
    ri = lax.broadcasted_iota(I32, (tq, tq), 0)
    ci = lax.broadcasted_iota(I32, (tq, tq), 1)
    earlier = jnp.where(ri < ci, 1.0, 0.0).astype(BF16)
    base = wide(counts_ref[...]) + jnp.dot(cnt.astype(BF16), earlier, preferred_element_type=F32)
    r1 = csum(jnp.where(oh1, base, 0.0))
    r2 = csum(jnp.where(oh2, base, 0.0))
    counts_ref[...] = counts_ref[...] + jnp.sum(cnt, axis=1, keepdims=True)

    route = jnp.where(sub == 0.0, e1, jnp.where(sub == 1.0, e2,
                      jnp.where(sub == 2.0, r1, jnp.where(sub == 3.0, r2, 0.0))))
    route_ref[...] = route.astype(I32)
    gate_ref[...] = jnp.where(sub == 0.0, w1, jnp.where(sub == 1.0, w2, 0.0))


def _attn(x, ma, sgb, q, k, v, p, seq):
    t = x.shape[0]
    tq = TQ
    sub = tq // WINDOW
    last_blk = t // WINDOW - 1
    n_tiles = t // tq
    att = lambda i: jnp.minimum(i, n_tiles - 1)
    post = lambda i: jnp.maximum(i - 1, 0)
    att_row = lambda w: pl.BlockSpec((tq, w), lambda i: (att(i), 0))
    row = lambda w: pl.BlockSpec((tq, w), lambda i: (post(i), 0))
    prev = pl.BlockSpec((WINDOW, KV_W), lambda i: (jnp.maximum(att(i) * sub - 1, 0), 0))
    nxt = pl.BlockSpec((WINDOW, KV_W), lambda i: (jnp.minimum((att(i) + 1) * sub, last_blk), 0))
    return pl.pallas_call(
        functools.partial(_attn_kernel, tiles_per_seq=seq // tq, n_tiles=n_tiles),
        grid=(n_tiles + 1,),
        in_specs=[pl.BlockSpec(memory_space=pltpu.SMEM),
                  row(D_MODEL), row(D_MODEL), row(D_MODEL), att_row(Q_W),
                  prev, att_row(KV_W), nxt, prev, att_row(KV_W), nxt,
                  _const_spec((Q_W, D_MODEL)), _const_spec((D_MODEL, D_MODEL)), _const_spec((1, D_MODEL)),
                  _const_spec((ROUTER_ROWS, D_MODEL)), _const_spec((ROUTER_ROWS, LANES))],
        out_specs=[row(D_MODEL), _rows3_spec(tq, lambda i: (post(i), 0)),
                   pl.BlockSpec((SUBLANES, tq), lambda i: (0, post(i))),
                   pl.BlockSpec((SUBLANES, tq), lambda i: (0, post(i))),
                   pl.BlockSpec((N_EXPERTS, LANES), lambda i: (0, 0))],
        out_shape=[jax.ShapeDtypeStruct((t, D_MODEL), F32), jax.ShapeDtypeStruct((t * ROW_LINES, LANES), I32),
                   jax.ShapeDtypeStruct((SUBLANES, t), I32), jax.ShapeDtypeStruct((SUBLANES, t), F32),
                   jax.ShapeDtypeStruct((N_EXPERTS, LANES), F32)],
        scratch_shapes=[pltpu.VMEM((tq + 2 * WINDOW, KV_W), BF16), pltpu.VMEM((tq + 2 * WINDOW, KV_W), BF16),
                        pltpu.VMEM((2, tq, Q_W), BF16),
                        pltpu.VMEM((sub * N_KV_HEADS, REP * WINDOW, 3 * WINDOW), F32),
                        pltpu.VMEM((sub * N_KV_HEADS, REP * WINDOW, 3 * WINDOW), BF16),
                        pltpu.VMEM((sub * N_KV_HEADS, REP * WINDOW, LANES), F32),
                        pltpu.VMEM((tq, D_MODEL), BF16)],
        compiler_params=pltpu.CompilerParams(dimension_semantics=("arbitrary",),
                                             vmem_limit_bytes=VMEM_LIMIT),
        name="attn",
    )(p["sink"], x, ma, sgb, q, k, k, k, v, v, v, p["w_proj_b"], p["w_out"], p["norm_ffn_g"],
      p["w_router"], p["b_router"])


def _dest_kernel(route_ref, starts_ref, dest_ref):
    route = route_ref[...].astype(F32)
    td = route.shape[1]
    sub = lax.broadcasted_iota(I32, route.shape, 0)
    erow = lax.broadcasted_iota(I32, (N_EXPERTS, td), 0).astype(F32)
    starts = jnp.broadcast_to(starts_ref[:, 0:1], (N_EXPERTS, td))

    def slot(k):
        start = jnp.sum(jnp.where(erow == route[k:k + 1], starts, 0.0), axis=0, keepdims=True)
        return start + route[TOP_K + k:TOP_K + k + 1]

    dest_ref[...] = jnp.where(sub == 0, slot(0), jnp.where(sub == 1, slot(1), 0.0)).astype(I32)


def _dest(route, pad_starts):
    t = route.shape[1]
    td = min(TM_DEST, t)
    return pl.pallas_call(
        _dest_kernel,
        grid=(t // td,),
        in_specs=[pl.BlockSpec((SUBLANES, td), lambda i: (0, i)), _const_spec((N_EXPERTS, LANES))],
        out_specs=pl.BlockSpec((SUBLANES, td), lambda i: (0, i)),
        out_shape=jax.ShapeDtypeStruct((SUBLANES, t), I32),
        compiler_params=pltpu.CompilerParams(dimension_semantics=("parallel",)),
        name="dest",
    )(route, pad_starts)


def _sc_mesh():
    return plsc.VectorSubcoreMesh(core_axis_name="c", subcore_axis_name="s")


def _sc_worker():
    return lax.axis_index("s") * SC_CORES + lax.axis_index("c")


def _sc_chunk(t):
    return min(SC_CHUNK_MAX, t // (SC_WORKERS * SUBLANES))


def _dispatch(hn, dests, n_rows):
    t = hn.shape[0]
    chunk = dests[0].shape[1]
    per_worker = t // SC_WORKERS
    n_chunks = per_worker // chunk
    idx = pltpu.VMEM((n_chunks, chunk), I32)

    @functools.partial(
        pl.kernel, mesh=_sc_mesh(), out_type=jax.ShapeDtypeStruct((n_rows,) + ROW3, I32),
        scratch_types=[idx, idx, pltpu.VMEM((chunk,) + ROW3, I32), pltpu.SemaphoreType.DMA])
    def scatter_rows(hn_hbm, d0_hbm, d1_hbm, xs_hbm, i0_v, i1_v, rows_v, sem):
        w = _sc_worker()
        pltpu.sync_copy(d0_hbm.at[pl.ds(w * n_chunks, n_chunks)], i0_v)
        pltpu.sync_copy(d1_hbm.at[pl.ds(w * n_chunks, n_chunks)], i1_v)

        @pl.loop(0, n_chunks)
        def _(j):
            pltpu.sync_copy(hn_hbm.at[pl.ds(w * per_worker + j * chunk, chunk)], rows_v)
            copies = [pltpu.make_async_copy(rows_v, xs_hbm.at[i_v.at[j]], sem) for i_v in (i0_v, i1_v)]
            for cp in copies:
                cp.start()
            for cp in copies:
                cp.wait()

    return scatter_rows(hn, *dests)


def _moe_block_rows(t):
    rows = MOE_ROWS_MAX
    while rows > CHUNK and (t * TOP_K) // N_EXPERTS < MOE_MIN_BLOCKS * rows:
        rows //= 2
    return rows


def _moe_kernel(be_ref, nvalid_ref, nused_ref, xs_ref, wg_ref, wu_ref, wd_ref, yb_ref, *w16_refs):
    i = pl.program_id(0)
    used = i < nused_ref[0]
    weights = w16_refs if w16_refs else (wg_ref, wu_ref, wd_ref)

    @pl.when(jnp.logical_not(used))
    def _():
        yb_ref[...] = jnp.zeros_like(yb_ref)

    @pl.when(used)
    def _():
        if w16_refs:
            @pl.when(jnp.logical_or(i == 0, be_ref[i] != be_ref[jnp.maximum(i - 1, 0)]))
            def _():
                for src, dst in zip((wg_ref, wu_ref, wd_ref), w16_refs):
                    dst[0] = src[0].astype(BF16)

        row = lax.broadcasted_iota(I32, (xs_ref.shape[0] // ROW_LINES, 1), 0)
        x = jnp.where(row < nvalid_ref[i], _load_rows3(xs_ref), 0.0).astype(BF16)
        half = weights[0].shape[2] // 2
        hid = []
        for cols in (slice(0, half), slice(half, 2 * half)):
            gate = jnp.dot(x, weights[0][0, :, cols], preferred_element_type=F32)
            up = jnp.dot(x, weights[1][0, :, cols], preferred_element_type=F32)
            hid.append((jax.nn.silu(gate) * up).astype(BF16))
        hid = jnp.concatenate(hid, axis=1)
        _store_rows3(yb_ref, jnp.dot(hid, weights[2][0], preferred_element_type=F32))


def _moe(block_e, n_valid, n_used, xs, experts, block_rows):
    n_blocks = xs.shape[0] // (block_rows * ROW_LINES)
    cast = experts[0].dtype != BF16

    def rows(i, be, nv, nu):
        return (jnp.minimum(i, nu[0] - 1), 0)

    def expert(i, be, nv, nu):
        return (be[jnp.minimum(i, nu[0] - 1)], 0, 0)

    w_specs = [pl.BlockSpec((1,) + w.shape[1:], expert) for w in experts]
    out_specs = [_rows3_spec(block_rows, lambda i, be, nv, nu: (i, 0))]
    out_shape = [jax.ShapeDtypeStruct(xs.shape, I32)]
    if cast:
        out_specs += w_specs
        out_shape += [jax.ShapeDtypeStruct(w.shape, BF16) for w in experts]
    yb, *w16 = pl.pallas_call(
        _moe_kernel,
        grid_spec=pltpu.PrefetchScalarGridSpec(
            num_scalar_prefetch=3,
            grid=(n_blocks,),
            in_specs=[_rows3_spec(block_rows, rows)] + w_specs,
            out_specs=out_specs,
        ),
        out_shape=out_shape,
        compiler_params=pltpu.CompilerParams(dimension_semantics=("arbitrary",),
                                             vmem_limit_bytes=VMEM_LIMIT),
        name="moe",
    )(block_e, n_valid, n_used, xs, *experts)
    return yb, (tuple(w16) if cast else experts)


def _gather(yb, dests):
    chunk = dests[0].shape[1]
    t = dests[0].shape[0] * chunk
    per_worker = t // SC_WORKERS
    n_chunks = per_worker // chunk
    idx = pltpu.VMEM((n_chunks, chunk), I32)
    out = jax.ShapeDtypeStruct((t,) + ROW3, I32)

    @functools.partial(
        pl.kernel, mesh=_sc_mesh(), out_type=(out, out),
        scratch_types=[idx, idx, pltpu.VMEM((chunk,) + ROW3, I32), pltpu.SemaphoreType.DMA])
    def gather_rows(yb_hbm, d0_hbm, d1_hbm, y0_hbm, y1_hbm, i0_v, i1_v, rows_v, sem):
        w = _sc_worker()
        pltpu.sync_copy(d0_hbm.at[pl.ds(w * n_chunks, n_chunks)], i0_v)
        pltpu.sync_copy(d1_hbm.at[pl.ds(w * n_chunks, n_chunks)], i1_v)

        @pl.loop(0, n_chunks)
        def _(j):
            rows = pl.ds(w * per_worker + j * chunk, chunk)
            for i_v, y_hbm in ((i0_v, y0_hbm), (i1_v, y1_hbm)):
                pltpu.async_copy(yb_hbm.at[i_v.at[j]], rows_v, sem).wait()
                pltpu.sync_copy(rows_v, y_hbm.at[rows])

    return gather_rows(yb, *dests)


def _combine_kernel(x1_ref, gate_ref, y0_ref, y1_ref, out_ref):
    tm = x1_ref.shape[0]
    gate = jnp.concatenate([gate_ref[...], jnp.zeros((LANES - SUBLANES, tm), F32)], axis=0).T
    out_ref[...] = x1_ref[...] + (_load_rows3(y0_ref) * gate[:, 0:1] + _load_rows3(y1_ref) * gate[:, 1:2])


def _combine(x1, gate, y0, y1):
    t = x1.shape[0]
    tm = TM_ROW
    return pl.pallas_call(
        _combine_kernel,
        grid=(t // tm,),
        in_specs=[pl.BlockSpec((tm, D_MODEL), lambda i: (i, 0)),
                  pl.BlockSpec((SUBLANES, tm), lambda i: (0, i)),
                  _rows3_spec(tm, lambda i: (i, 0)), _rows3_spec(tm, lambda i: (i, 0))],
        out_specs=pl.BlockSpec((tm, D_MODEL), lambda i: (i, 0)),
        out_shape=jax.ShapeDtypeStruct((t, D_MODEL), F32),
        compiler_params=pltpu.CompilerParams(dimension_semantics=("parallel",)),
        name="combine",
    )(x1, gate, y0, y1)


def _rope_tables(seq):
    half = HEAD_DIM // 2
    inv_freq = ROPE_THETA ** (-jnp.arange(half, dtype=F32) / half)
    coarse = (jnp.arange(seq // ROPE_SPLIT) * ROPE_SPLIT).astype(F32)[:, None, None] * inv_freq
    fine = jnp.arange(ROPE_SPLIT).astype(F32)[None, :, None] * inv_freq
    cos = (jnp.cos(coarse) * jnp.cos(fine) - jnp.sin(coarse) * jnp.sin(fine)).reshape(seq, half)
    sin = (jnp.sin(coarse) * jnp.cos(fine) + jnp.cos(coarse) * jnp.sin(fine)).reshape(seq, half)
    return jnp.concatenate([cos, cos], axis=-1), jnp.concatenate([-sin, sin], axis=-1)


def _layer(x, p, seq, rope, experts):
    t = x.shape[0]
    ma, sgb, q, k, v = _inproj(x, p, *rope, seq)
    x1, hn, route, gate, counts_f = _attn(x, ma, sgb, q, k, v, p, seq)

    block_rows = _moe_block_rows(t)
    counts = counts_f[:, 0].astype(I32)
    min_blocks = 1 if experts[0].dtype != BF16 else 0
    padded = jnp.maximum((counts + block_rows - 1) // block_rows, min_blocks) * block_rows
    pad_ends = jnp.cumsum(padded)
    pad_starts = pad_ends - padded
    n_blocks = (t * TOP_K) // block_rows + N_EXPERTS
    block_start = jnp.arange(n_blocks, dtype=I32) * block_rows
    in_expert = jnp.logical_and(block_start[:, None] >= pad_starts[None, :],
                                block_start[:, None] < pad_ends[None, :]).astype(I32)
    block_e = jnp.minimum(jnp.sum((block_start[:, None] >= pad_ends[None, :]).astype(I32), axis=1), N_EXPERTS - 1)
    n_valid = jnp.sum(in_expert * jnp.clip(pad_starts + counts - block_start[:, None], 0, block_rows), axis=1)
    n_used = pad_ends[-1:] // block_rows
    starts_col = jnp.broadcast_to(pad_starts.astype(F32)[:, None], (N_EXPERTS, LANES))

    dest = _dest(route, starts_col)
    dests = [dest[k].reshape(t // _sc_chunk(t), _sc_chunk(t)) for k in range(TOP_K)]
    xs = _dispatch(_as_rows3(hn), dests, n_blocks * block_rows)
    yb, experts = _moe(block_e, n_valid, n_used, _as_lines(xs), experts, block_rows)
    y0, y1 = _gather(_as_rows3(yb), dests)
    return _combine(x1, gate, _as_lines(y0), _as_lines(y1)), experts


def kernel(x_prompt, x_sample, norm_mix_g, w_in, norm_v_g, w_spatial, b_spatial, q_norm_g, k_norm_g, sink,
           w_proj_a, w_proj_b, w_out, norm_ffn_g, w_router_group, b_router_group, w_router_expert,
           b_router_expert, w_gate_e, w_up_e, w_down_e):
    depth = w_in.shape[0]
    layers = []
    for l in range(depth):
        w_router = jnp.zeros((ROUTER_ROWS, D_MODEL), F32)
        w_router = w_router.at[:N_GROUPS].set(w_router_group[l].T)
        w_router = w_router.at[SUBLANES:SUBLANES + N_EXPERTS].set(w_router_expert[l].T)
        b_router = jnp.zeros((ROUTER_ROWS,), F32)
        b_router = b_router.at[:N_GROUPS].set(b_router_group[l])
        b_router = b_router.at[SUBLANES:SUBLANES + N_EXPERTS].set(b_router_expert[l])
        b_router = jnp.broadcast_to(b_router[:, None], (ROUTER_ROWS, LANES))
        layers.append(dict(
            norm_mix_g=norm_mix_g[l][None], w_in=w_in[l].astype(BF16), norm_v_g=norm_v_g[l][None],
            w_spatial=w_spatial[l].astype(BF16),
            b_spatial=jnp.broadcast_to(b_spatial[l][:, :, None], (A_GROUPS, CHUNK, LANES)),
            q_norm_g=q_norm_g[l][None], k_norm_g=k_norm_g[l][None], sink=sink[l],
            w_proj_a=w_proj_a[l].astype(BF16), w_proj_b=w_proj_b[l].astype(BF16), w_out=w_out[l].astype(BF16),
            norm_ffn_g=norm_ffn_g[l][None], w_router=w_router.astype(BF16), b_router=b_router,
            experts=(w_gate_e[l], w_up_e[l], w_down_e[l])))

    trunks = [x_prompt, x_sample]
    rows = [x.reshape(-1, D_MODEL) for x in trunks]
    order = sorted(range(len(trunks)), key=lambda n: -rows[n].shape[0])
    rope = _rope_tables(max(x.shape[1] for x in trunks))
    for p in layers:
        experts = p["experts"]
        for n in order:
            rows[n], experts = _layer(rows[n], p, trunks[n].shape[1], rope, experts)
    return tuple(r.reshape(x.shape) for r, x in zip(rows, trunks))
```

```python
import functools

import jax
import jax.numpy as jnp
from jax import lax
from jax.experimental import pallas as pl
from jax.experimental.pallas import tpu as pltpu
from jax.experimental.pallas import tpu_sc as plsc

F32 = jnp.float32
BF16 = jnp.bfloat16
I32 = jnp.int32
U32 = jnp.uint32

LANES = 128
SUBLANES = 8
VMEM_BYTES_V7X = 64 * 1024 * 1024
SC_CORES = 2
SC_SUBCORES = 16
SC_WORKERS = SC_CORES * SC_SUBCORES
SC_CHUNK_MAX = 128

D_MODEL = 1024
A_WIDTH = D_MODEL
A_GROUPS = 8
CHUNK = 128
HEAD_DIM = 128
N_Q_HEADS = D_MODEL // HEAD_DIM
N_KV_HEADS = 2
REP = N_Q_HEADS // N_KV_HEADS
WINDOW = 128
ROPE_THETA = 10000.0
ROPE_SPLIT = 64
Q_W = N_Q_HEADS * HEAD_DIM
KV_W = N_KV_HEADS * HEAD_DIM
IN_W = 2 * A_WIDTH + Q_W + 2 * KV_W + 2 * D_MODEL
COL_U = 0
COL_V = COL_U + A_WIDTH
COL_Q = COL_V + A_WIDTH
COL_K = COL_Q + Q_W
COL_VA = COL_K + KV_W
COL_GA = COL_VA + KV_W
COL_GB = COL_GA + D_MODEL
N_GROUPS = 4
EXPERTS_PER_GROUP = 8
N_EXPERTS = N_GROUPS * EXPERTS_PER_GROUP
TOP_K = 2
EPS = 1e-6
NEG = -1e30

TM_IN = 512
TM_IN_SUB = 256
TQ = 512
DENSE_COLS = 256
ROUTER_ROWS = 64
assert EXPERTS_PER_GROUP == SUBLANES and SUBLANES + N_EXPERTS <= ROUTER_ROWS
TM_ROW = 1024
TM_DEST = 8192
MOE_ROWS_MAX = 1024
MOE_MIN_BLOCKS = 1
VMEM_RESERVE = 8 * 1024 * 1024
VMEM_LIMIT = VMEM_BYTES_V7X - VMEM_RESERVE


def _rms(x, g):
    return x * lax.rsqrt(jnp.mean(x * x, axis=-1, keepdims=True) + EPS) * g


ROW_LINES = D_MODEL // 2 // LANES
ROW3 = (ROW_LINES, LANES)
HIGH_HALF = 0xFFFF0000


def _as_rows3(a):
    return a.reshape((a.shape[0] // ROW_LINES,) + ROW3)


def _as_lines(a):
    return a.reshape((a.shape[0] * ROW_LINES, LANES))


def _store_rows3(lines_ref, val):
    rows = val.shape[0]
    bits = lax.bitcast_convert_type(val.astype(BF16).astype(F32), U32)
    half = ROW_LINES * LANES
    for s in range(ROW_LINES):
        lo = bits[:, s * LANES:(s + 1) * LANES] >> 16
        hi = bits[:, half + s * LANES:half + (s + 1) * LANES] & U32(HIGH_HALF)
        lines_ref[pl.ds(s, rows, stride=ROW_LINES), :] = lax.bitcast_convert_type(lo | hi, I32)


def _load_rows3(lines_ref):
    rows = lines_ref.shape[0] // ROW_LINES
    words = [lax.bitcast_convert_type(lines_ref[pl.ds(s, rows, stride=ROW_LINES), :], U32)
             for s in range(ROW_LINES)]
    lo = [lax.bitcast_convert_type(w << 16, F32) for w in words]
    hi = [lax.bitcast_convert_type(w & U32(HIGH_HALF), F32) for w in words]
    return jnp.concatenate(lo + hi, axis=1)


def _rows3_spec(rows, index_map):
    return pl.BlockSpec((rows * ROW_LINES, LANES), index_map)


def _const_spec(shape):
    nd = len(shape)
    return pl.BlockSpec(shape, lambda *_: (0,) * nd, pipeline_mode=pl.Buffered(1))


def _inproj_kernel(x_ref, gmix_ref, win_ref, gv_ref, ws_ref, bs_ref, gq_ref, gk_ref, cos_ref, sin_ref,
                   wpa_ref, ma_ref, sgb_ref, q_ref, k_ref, v_ref, h_scr, u_scr, vn_scr, a_scr):
    tm = x_ref.shape[0]

    def stages(rows):
        def proj(lo, width):
            return jnp.dot(h_scr[rows], win_ref[:, lo:lo + width], preferred_element_type=F32)

        def norm_rope(z, g):
            zn = _rms(z, g)
            return zn * cos_ref[rows] + pltpu.roll(zn, HEAD_DIM // 2, 1) * sin_ref[rows]

        def norm():
            h_scr[rows] = _rms(x_ref[rows], gmix_ref[...]).astype(BF16)

        def mix_v():
            vn_scr[rows] = _rms(jax.nn.gelu(proj(COL_V, A_WIDTH)), gv_ref[...]).astype(BF16)

        def mix_u():
            u_scr[rows] = jax.nn.gelu(proj(COL_U, A_WIDTH))

        def spatial():
            for c in range(rows.start, rows.stop, CHUNK):
                chunk = slice(c, c + CHUNK)
                for g in range(A_GROUPS):
                    cols = slice(g * LANES, (g + 1) * LANES)
                    mixed = jnp.dot(ws_ref[g], vn_scr[chunk, cols], preferred_element_type=F32) + bs_ref[g]
                    a_scr[chunk, cols] = (u_scr[chunk, cols] * mixed).astype(BF16)

        def gate_a():
            for lo in range(0, D_MODEL, D_MODEL // 2):
                cols = slice(lo, lo + D_MODEL // 2)
                ya = jnp.dot(a_scr[rows], wpa_ref[:, cols], preferred_element_type=F32)
                ma_ref[rows, cols] = jax.nn.sigmoid(proj(COL_GA + lo, D_MODEL // 2)) * ya

        def gate_b():
            for lo in range(0, D_MODEL, D_MODEL // 2):
                sgb_ref[rows, lo:lo + D_MODEL // 2] = jax.nn.sigmoid(proj(COL_GB + lo, D_MODEL // 2))

        def queries():
            qz = proj(COL_Q, Q_W)
            for hd in range(N_Q_HEADS):
                cols = slice(hd * HEAD_DIM, (hd + 1) * HEAD_DIM)
                q_ref[rows, cols] = norm_rope(qz[:, cols], gq_ref[...]).astype(BF16)

        def keys_values():
            kz = proj(COL_K, KV_W)
            for hd in range(N_KV_HEADS):
                cols = slice(hd * HEAD_DIM, (hd + 1) * HEAD_DIM)
                k_ref[rows, cols] = norm_rope(kz[:, cols], gk_ref[...]).astype(BF16)
            v_ref[rows] = proj(COL_VA, KV_W).astype(BF16)

        return [norm, mix_v, queries, mix_u, gate_b, keys_values, spatial, gate_a]

    subs = [stages(slice(r, r + TM_IN_SUB)) for r in range(0, tm, TM_IN_SUB)]
    n_stage = len(subs[0])
    for step in range(n_stage + len(subs) - 1):
        for n, sub in enumerate(subs):
            if 0 <= step - n < n_stage:
                sub[step - n]()


def _inproj(x, p, cos, sin, seq):
    t = x.shape[0]
    tm = TM_IN
    n_pos = seq // tm
    row = lambda w: pl.BlockSpec((tm, w), lambda i: (i, 0))
    pos = pl.BlockSpec((tm, HEAD_DIM), lambda i: (i % n_pos, 0))
    return pl.pallas_call(
        _inproj_kernel,
        grid=(t // tm,),
        in_specs=[row(D_MODEL), _const_spec((1, D_MODEL)), _const_spec((D_MODEL, IN_W)),
                  _const_spec((1, A_WIDTH)), _const_spec((A_GROUPS, CHUNK, CHUNK)),
                  _const_spec((A_GROUPS, CHUNK, LANES)), _const_spec((1, HEAD_DIM)),
                  _const_spec((1, HEAD_DIM)), pos, pos, _const_spec((A_WIDTH, D_MODEL))],
        out_specs=[row(D_MODEL), row(D_MODEL), row(Q_W), row(KV_W), row(KV_W)],
        out_shape=[jax.ShapeDtypeStruct((t, D_MODEL), F32), jax.ShapeDtypeStruct((t, D_MODEL), F32),
                   jax.ShapeDtypeStruct((t, Q_W), BF16), jax.ShapeDtypeStruct((t, KV_W), BF16),
                   jax.ShapeDtypeStruct((t, KV_W), BF16)],
        scratch_shapes=[pltpu.VMEM((tm, D_MODEL), BF16), pltpu.VMEM((tm, A_WIDTH), F32),
                        pltpu.VMEM((tm, A_WIDTH), BF16), pltpu.VMEM((tm, A_WIDTH), BF16)],
        compiler_params=pltpu.CompilerParams(dimension_semantics=("parallel",),
                                             vmem_limit_bytes=VMEM_LIMIT),
        name="inproj",
    )(x, p["norm_mix_g"], p["w_in"], p["norm_v_g"], p["w_spatial"], p["b_spatial"], p["q_norm_g"],
      p["k_norm_g"], cos, sin, p["w_proj_a"])


def _attn_kernel(sink_ref, x_ref, ma_ref, sgb_ref, q_ref, kp_ref, kc_ref, kn_ref, vp_ref, vc_ref, vn_ref,
                 wpb_ref, wout_ref, gffn_ref, wr_ref, br_ref,
                 x1_ref, hn_ref, route_ref, gate_ref, counts_ref, kcat, vcat, o_scr, s_scr, p_scr, sink_scr, m_scr,
                 *, tiles_per_seq, n_tiles):
    tq = x_ref.shape[0]
    blk = WINDOW
    i = pl.program_id(0)
    slot = i % 2

    @pl.when(i == 0)
    def _():
        o_scr[...] = jnp.zeros_like(o_scr)
        counts_ref[...] = jnp.zeros_like(counts_ref)

    pos_tile = jnp.minimum(i, n_tiles - 1) % tiles_per_seq
    has_prev = pos_tile > 0
    has_next = pos_tile < tiles_per_seq - 1

    kcat[0:blk] = kp_ref[...]
    kcat[blk:blk + tq] = kc_ref[...]
    kcat[blk + tq:] = kn_ref[...]
    vcat[0:blk] = vp_ref[...]
    vcat[blk:blk + tq] = vc_ref[...]
    vcat[blk + tq:] = vn_ref[...]

    qr = lax.broadcasted_iota(I32, (blk, blk), 0)
    kc = lax.broadcasted_iota(I32, (blk, blk), 1)
    scale = HEAD_DIM ** -0.5
    n_sub = tq // blk
    pairs = [(j, g) for j in range(n_sub) for g in range(N_KV_HEADS)]

    def keys(ref, j, g):
        return ref[j * blk:(j + 3) * blk, g * HEAD_DIM:(g + 1) * HEAD_DIM]

    def head_cols(g, r):
        hd = g * REP + r
        return slice(hd * HEAD_DIM, (hd + 1) * HEAD_DIM)

    def scores(b):
        j, g = pairs[b]
        rows = slice(j * blk, (j + 1) * blk)
        qs = jnp.concatenate([q_ref[rows, head_cols(g, r)] for r in range(REP)], axis=0)
        s_scr[b] = lax.dot_general(qs, keys(kcat, j, g), (((1,), (1,)), ((), ())),
                                   preferred_element_type=F32)

    log2e = 1.4426950408889634

    def softmax(b, r):
        j, g = pairs[b]
        hrows = slice(r * blk, (r + 1) * blk)
        z = s_scr[b, hrows, :] * (scale * log2e)
        lo_ok = kc >= (qr + jnp.where(has_prev, 0, blk) if j == 0 else qr)
        hi_ok = kc <= (qr - jnp.where(has_next, 0, blk) if j == n_sub - 1 else qr)
        z = jnp.concatenate([jnp.where(lo_ok, z[:, :blk], NEG), z[:, blk:2 * blk],
                             jnp.where(hi_ok, z[:, 2 * blk:], NEG)], axis=1)
        sink = sink_ref[g * REP + r] * log2e
        m = jnp.maximum(jnp.max(z, axis=-1, keepdims=True), sink)
        p_scr[b, hrows, :] = jnp.exp2(z - m).astype(BF16)
        sink_scr[b, hrows, :] = jnp.broadcast_to(jnp.exp2(sink - m), (blk, LANES))

    def values(b):
        j, g = pairs[b]
        rows = slice(j * blk, (j + 1) * blk)
        v_ext = jnp.concatenate([keys(vcat, j, g), jnp.ones((3 * blk, HEAD_DIM), BF16)], axis=1)
        acc = jnp.dot(p_scr[b], v_ext, preferred_element_type=F32)
        o = (acc[:, :HEAD_DIM] / (acc[:, HEAD_DIM:] + sink_scr[b])).astype(BF16)
        for r in range(REP):
            o_scr[slot, rows, head_cols(g, r)] = o[r * blk:(r + 1) * blk, :]

    def merged_cols(cols):
        yb = jnp.dot(o_scr[1 - slot], wpb_ref[:, cols], preferred_element_type=F32)
        m_scr[:, cols] = (ma_ref[:, cols] + sgb_ref[:, cols] * yb).astype(BF16)

    def x1_cols(cols):
        x1_ref[:, cols] = x_ref[:, cols] + jnp.dot(m_scr[...], wout_ref[:, cols], preferred_element_type=F32)

    col_chunks = [slice(c * DENSE_COLS, (c + 1) * DENSE_COLS) for c in range(D_MODEL // DENSE_COLS)]
    dense = [functools.partial(f, cols) for f in (merged_cols, x1_cols) for cols in col_chunks]
    units = [(b, r) for b in range(len(pairs)) for r in range(REP)]
    units_per_dense = len(units) // len(dense)
    scores(0)
    for n, (b, r) in enumerate(units):
        if r == 0 and b + 1 < len(pairs):
            scores(b + 1)
        softmax(b, r)
        if (n + 1) % units_per_dense == 0:
            dense[(n + 1) // units_per_dense - 1]()
        if r == 0 and b > 0:
            values(b - 1)
    values(len(pairs) - 1)

    hn = _rms(x1_ref[...], gffn_ref[...])
    _store_rows3(hn_ref, hn)
    def wide(a):
        return jnp.concatenate([a] * (tq // LANES), axis=1)

    logits = lax.dot_general(wr_ref[...], hn.astype(BF16), (((1,), (1,)), ((), ())),
                             preferred_element_type=F32) + wide(br_ref[...])
    sub = lax.broadcasted_iota(I32, (SUBLANES, tq), 0).astype(F32)
    ninf = -jnp.inf

    def cmax(a):
        return jnp.max(a, axis=0, keepdims=True)

    def csum(a):
        return jnp.sum(a, axis=0, keepdims=True)

    def first_row(mask):
        return jnp.min(jnp.where(mask, sub, float(SUBLANES)), axis=0, keepdims=True)

    def group_rows(g):
        return logits[(g + 1) * SUBLANES:(g + 2) * SUBLANES]

    gl = jnp.where(sub < N_GROUPS, logits[0:SUBLANES], ninf)
    gmax = cmax(gl)
    g_sel = first_row(gl == gmax)
    g_p = 1.0 / csum(jnp.exp(gl - gmax))
    el = group_rows(0)
    for g in range(1, N_GROUPS):
        el = jnp.where(g_sel == g, group_rows(g), el)
    ee = jnp.exp(el - cmax(el))
    eprob = ee / csum(ee)
    p1 = cmax(eprob)
    i1 = first_row(eprob == p1)
    eprob2 = jnp.where(sub == i1, -1.0, eprob)
    p2 = cmax(eprob2)
    i2 = first_row(eprob2 == p2)
    psum = p1 + p2
    w1 = g_p * p1 / psum
    w2 = g_p * p2 / psum
    e1 = g_sel * EXPERTS_PER_GROUP + i1
    e2 = g_sel * EXPERTS_PER_GROUP + i2

    erow = lax.broadcasted_iota(I32, (N_EXPERTS, tq), 0).astype(F32)
    oh1 = erow == e1
    oh2 = erow == e2
    cnt = (jnp.where(oh1, 1.0, 0.0) + jnp.where(oh2, 1.0, 0.0)) * jnp.where(i > 0, 1.0, 0.0)
    ri = lax.broadcasted_iota(I32, (tq, tq), 0)
    ci = lax.broadcasted_iota(I32, (tq, tq), 1)
    earlier = jnp.where(ri < ci, 1.0, 0.0).astype(BF16)
    base = wide(counts_ref[...]) + jnp.dot(cnt.astype(BF16), earlier, preferred_element_type=F32)
    r1 = csum(jnp.where(oh1, base, 0.0))
    r2 = csum(jnp.where(oh2, base, 0.0))
    counts_ref[...] = counts_ref[...] + jnp.sum(cnt, axis=1, keepdims=True)

    route = jnp.where(sub == 0.0, e1, jnp.where(sub == 1.0, e2,
                      jnp.where(sub == 2.0, r1, jnp.where(sub == 3.0, r2, 0.0))))
    route_ref[...] = route.astype(I32)
    gate_ref[...] = jnp.where(sub == 0.0, w1, jnp.where(sub == 1.0, w2, 0.0))


def _attn(x, ma, sgb, q, k, v, p, seq):
    t = x.shape[0]
    tq = TQ
    sub = tq // WINDOW
    last_blk = t // WINDOW - 1
    n_tiles = t // tq
    att = lambda i: jnp.minimum(i, n_tiles - 1)
    post = lambda i: jnp.maximum(i - 1, 0)
    att_row = lambda w: pl.BlockSpec((tq, w), lambda i: (att(i), 0))
    row = lambda w: pl.BlockSpec((tq, w), lambda i: (post(i), 0))
    prev = pl.BlockSpec((WINDOW, KV_W), lambda i: (jnp.maximum(att(i) * sub - 1, 0), 0))
    nxt = pl.BlockSpec((WINDOW, KV_W), lambda i: (jnp.minimum((att(i) + 1) * sub, last_blk), 0))
    return pl.pallas_call(
        functools.partial(_attn_kernel, tiles_per_seq=seq // tq, n_tiles=n_tiles),
        grid=(n_tiles + 1,),
        in_specs=[pl.BlockSpec(memory_space=pltpu.SMEM),
                  row(D_MODEL), row(D_MODEL), row(D_MODEL), att_row(Q_W),
                  prev, att_row(KV_W), nxt, prev, att_row(KV_W), nxt,
                  _const_spec((Q_W, D_MODEL)), _const_spec((D_MODEL, D_MODEL)), _const_spec((1, D_MODEL)),
                  _const_spec((ROUTER_ROWS, D_MODEL)), _const_spec((ROUTER_ROWS, LANES))],
        out_specs=[row(D_MODEL), _rows3_spec(tq, lambda i: (post(i), 0)),
                   pl.BlockSpec((SUBLANES, tq), lambda i: (0, post(i))),
                   pl.BlockSpec((SUBLANES, tq), lambda i: (0, post(i))),
                   pl.BlockSpec((N_EXPERTS, LANES), lambda i: (0, 0))],
        out_shape=[jax.ShapeDtypeStruct((t, D_MODEL), F32), jax.ShapeDtypeStruct((t * ROW_LINES, LANES), I32),
                   jax.ShapeDtypeStruct((SUBLANES, t), I32), jax.ShapeDtypeStruct((SUBLANES, t), F32),
                   jax.ShapeDtypeStruct((N_EXPERTS, LANES), F32)],
        scratch_shapes=[pltpu.VMEM((tq + 2 * WINDOW, KV_W), BF16), pltpu.VMEM((tq + 2 * WINDOW, KV_W), BF16),
                        pltpu.VMEM((2, tq, Q_W), BF16),
                        pltpu.VMEM((sub * N_KV_HEADS, REP * WINDOW, 3 * WINDOW), F32),
                        pltpu.VMEM((sub * N_KV_HEADS, REP * WINDOW, 3 * WINDOW), BF16),
                        pltpu.VMEM((sub * N_KV_HEADS, REP * WINDOW, LANES), F32),
                        pltpu.VMEM((tq, D_MODEL), BF16)],
        compiler_params=pltpu.CompilerParams(dimension_semantics=("arbitrary",),
                                             vmem_limit_bytes=VMEM_LIMIT),
        name="attn",
    )(p["sink"], x, ma, sgb, q, k, k, k, v, v, v, p["w_proj_b"], p["w_out"], p["norm_ffn_g"],
      p["w_router"], p["b_router"])


def _dest_kernel(route_ref, starts_ref, dest_ref):
    route = route_ref[...].astype(F32)
    td = route.shape[1]
    sub = lax.broadcasted_iota(I32, route.shape, 0)
    erow = lax.broadcasted_iota(I32, (N_EXPERTS, td), 0).astype(F32)
    starts = jnp.broadcast_to(starts_ref[:, 0:1], (N_EXPERTS, td))

    def slot(k):
        start = jnp.sum(jnp.where(erow == route[k:k + 1], starts, 0.0), axis=0, keepdims=True)
        return start + route[TOP_K + k:TOP_K + k + 1]

    dest_ref[...] = jnp.where(sub == 0, slot(0), jnp.where(sub == 1, slot(1), 0.0)).astype(I32)


def _dest(route, pad_starts):
    t = route.shape[1]
    td = min(TM_DEST, t)
    return pl.pallas_call(
        _dest_kernel,
        grid=(t // td,),
        in_specs=[pl.BlockSpec((SUBLANES, td), lambda i: (0, i)), _const_spec((N_EXPERTS, LANES))],
        out_specs=pl.BlockSpec((SUBLANES, td), lambda i: (0, i)),
        out_shape=jax.ShapeDtypeStruct((SUBLANES, t), I32),
        compiler_params=pltpu.CompilerParams(dimension_semantics=("parallel",)),
        name="dest",
    )(route, pad_starts)


def _sc_mesh():
    return plsc.VectorSubcoreMesh(core_axis_name="c", subcore_axis_name="s")


def _sc_worker():
    return lax.axis_index("s") * SC_CORES + lax.axis_index("c")


def _sc_chunk(t):
    return min(SC_CHUNK_MAX, t // (SC_WORKERS * SUBLANES))


def _dispatch(hn, dests, n_rows):
    t = hn.shape[0]
    chunk = dests[0].shape[1]
    per_worker = t // SC_WORKERS
    n_chunks = per_worker // chunk
    idx = pltpu.VMEM((n_chunks, chunk), I32)

    @functools.partial(
        pl.kernel, mesh=_sc_mesh(), out_type=jax.ShapeDtypeStruct((n_rows,) + ROW3, I32),
        scratch_types=[idx, idx, pltpu.VMEM((chunk,) + ROW3, I32), pltpu.SemaphoreType.DMA])
    def scatter_rows(hn_hbm, d0_hbm, d1_hbm, xs_hbm, i0_v, i1_v, rows_v, sem):
        w = _sc_worker()
        pltpu.sync_copy(d0_hbm.at[pl.ds(w * n_chunks, n_chunks)], i0_v)
        pltpu.sync_copy(d1_hbm.at[pl.ds(w * n_chunks, n_chunks)], i1_v)

        @pl.loop(0, n_chunks)
        def _(j):
            pltpu.sync_copy(hn_hbm.at[pl.ds(w * per_worker + j * chunk, chunk)], rows_v)
            copies = [pltpu.make_async_copy(rows_v, xs_hbm.at[i_v.at[j]], sem) for i_v in (i0_v, i1_v)]
            for cp in copies:
                cp.start()
            for cp in copies:
                cp.wait()

    return scatter_rows(hn, *dests)


def _moe_block_rows(t):
    rows = MOE_ROWS_MAX
    while rows > CHUNK and (t * TOP_K) // N_EXPERTS < MOE_MIN_BLOCKS * rows:
        rows //= 2
    return rows


def _moe_kernel(be_ref, nvalid_ref, nused_ref, xs_ref, wg_ref, wu_ref, wd_ref, yb_ref, *w16_refs):
    i = pl.program_id(0)
    used = i < nused_ref[0]
    weights = w16_refs if w16_refs else (wg_ref, wu_ref, wd_ref)

    @pl.when(jnp.logical_not(used))
    def _():
        yb_ref[...] = jnp.zeros_like(yb_ref)

    @pl.when(used)
    def _():
        if w16_refs:
            @pl.when(jnp.logical_or(i == 0, be_ref[i] != be_ref[jnp.maximum(i - 1, 0)]))
            def _():
                for src, dst in zip((wg_ref, wu_ref, wd_ref), w16_refs):
                    dst[0] = src[0].astype(BF16)

        row = lax.broadcasted_iota(I32, (xs_ref.shape[0] // ROW_LINES, 1), 0)
        x = jnp.where(row < nvalid_ref[i], _load_rows3(xs_ref), 0.0).astype(BF16)
        half = weights[0].shape[2] // 2
        hid = []
        for cols in (slice(0, half), slice(half, 2 * half)):
            gate = jnp.dot(x, weights[0][0, :, cols], preferred_element_type=F32)
            up = jnp.dot(x, weights[1][0, :, cols], preferred_element_type=F32)
            hid.append((jax.nn.silu(gate) * up).astype(BF16))
        hid = jnp.concatenate(hid, axis=1)
        rows = hid.shape[0] // 2
        for r0 in (0, rows):
            y = jnp.dot(hid[r0:r0 + rows], weights[2][0], preferred_element_type=F32)
            _store_rows3(yb_ref.at[pl.ds(r0 * ROW_LINES, rows * ROW_LINES)], y)


def _moe(block_e, n_valid, n_used, xs, experts, block_rows):
    n_blocks = xs.shape[0] // (block_rows * ROW_LINES)
    cast = experts[0].dtype != BF16

    def rows(i, be, nv, nu):
        return (jnp.minimum(i, nu[0] - 1), 0)

    def expert(i, be, nv, nu):
        return (be[jnp.minimum(i, nu[0] - 1)], 0, 0)

    w_specs = [pl.BlockSpec((1,) + w.shape[1:], expert) for w in experts]
    out_specs = [_rows3_spec(block_rows, lambda i, be, nv, nu: (i, 0))]
    out_shape = [jax.ShapeDtypeStruct(xs.shape, I32)]
    if cast:
        out_specs += w_specs
        out_shape += [jax.ShapeDtypeStruct(w.shape, BF16) for w in experts]
    yb, *w16 = pl.pallas_call(
        _moe_kernel,
        grid_spec=pltpu.PrefetchScalarGridSpec(
            num_scalar_prefetch=3,
            grid=(n_blocks,),
            in_specs=[_rows3_spec(block_rows, rows)] + w_specs,
            out_specs=out_specs,
        ),
        out_shape=out_shape,
        compiler_params=pltpu.CompilerParams(dimension_semantics=("arbitrary",),
                                             vmem_limit_bytes=VMEM_LIMIT),
        name="moe",
    )(block_e, n_valid, n_used, xs, *experts)
    return yb, (tuple(w16) if cast else experts)


def _gather(yb, dests):
    chunk = dests[0].shape[1]
    t = dests[0].shape[0] * chunk
    per_worker = t // SC_WORKERS
    n_chunks = per_worker // chunk
    idx = pltpu.VMEM((n_chunks, chunk), I32)
    out = jax.ShapeDtypeStruct((t,) + ROW3, I32)

    @functools.partial(
        pl.kernel, mesh=_sc_mesh(), out_type=(out, out),
        scratch_types=[idx, idx, pltpu.VMEM((chunk,) + ROW3, I32), pltpu.SemaphoreType.DMA])
    def gather_rows(yb_hbm, d0_hbm, d1_hbm, y0_hbm, y1_hbm, i0_v, i1_v, rows_v, sem):
        w = _sc_worker()
        pltpu.sync_copy(d0_hbm.at[pl.ds(w * n_chunks, n_chunks)], i0_v)
        pltpu.sync_copy(d1_hbm.at[pl.ds(w * n_chunks, n_chunks)], i1_v)

        @pl.loop(0, n_chunks)
        def _(j):
            rows = pl.ds(w * per_worker + j * chunk, chunk)
            for i_v, y_hbm in ((i0_v, y0_hbm), (i1_v, y1_hbm)):
                pltpu.async_copy(yb_hbm.at[i_v.at[j]], rows_v, sem).wait()
                pltpu.sync_copy(rows_v, y_hbm.at[rows])

    return gather_rows(yb, *dests)


def _combine_kernel(x1_ref, gate_ref, y0_ref, y1_ref, out_ref):
    tm = x1_ref.shape[0]
    gate = jnp.concatenate([gate_ref[...], jnp.zeros((LANES - SUBLANES, tm), F32)], axis=0).T
    out_ref[...] = x1_ref[...] + (_load_rows3(y0_ref) * gate[:, 0:1] + _load_rows3(y1_ref) * gate[:, 1:2])


def _combine(x1, gate, y0, y1):
    t = x1.shape[0]
    tm = TM_ROW
    return pl.pallas_call(
        _combine_kernel,
        grid=(t // tm,),
        in_specs=[pl.BlockSpec((tm, D_MODEL), lambda i: (i, 0)),
                  pl.BlockSpec((SUBLANES, tm), lambda i: (0, i)),
                  _rows3_spec(tm, lambda i: (i, 0)), _rows3_spec(tm, lambda i: (i, 0))],
        out_specs=pl.BlockSpec((tm, D_MODEL), lambda i: (i, 0)),
        out_shape=jax.ShapeDtypeStruct((t, D_MODEL), F32),
        compiler_params=pltpu.CompilerParams(dimension_semantics=("parallel",)),
        name="combine",
    )(x1, gate, y0, y1)


def _rope_tables(seq):
    half = HEAD_DIM // 2
    inv_freq = ROPE_THETA ** (-jnp.arange(half, dtype=F32) / half)
    coarse = (jnp.arange(seq // ROPE_SPLIT) * ROPE_SPLIT).astype(F32)[:, None, None] * inv_freq
    fine = jnp.arange(ROPE_SPLIT).astype(F32)[None, :, None] * inv_freq
    cos = (jnp.cos(coarse) * jnp.cos(fine) - jnp.sin(coarse) * jnp.sin(fine)).reshape(seq, half)
    sin = (jnp.sin(coarse) * jnp.cos(fine) + jnp.cos(coarse) * jnp.sin(fine)).reshape(seq, half)
    return jnp.concatenate([cos, cos], axis=-1), jnp.concatenate([-sin, sin], axis=-1)


def _layer(x, p, seq, rope, experts):
    t = x.shape[0]
    ma, sgb, q, k, v = _inproj(x, p, *rope, seq)
    x1, hn, route, gate, counts_f = _attn(x, ma, sgb, q, k, v, p, seq)

    block_rows = _moe_block_rows(t)
    counts = counts_f[:, 0].astype(I32)
    min_blocks = 1 if experts[0].dtype != BF16 else 0
    padded = jnp.maximum((counts + block_rows - 1) // block_rows, min_blocks) * block_rows
    pad_ends = jnp.cumsum(padded)
    pad_starts = pad_ends - padded
    n_blocks = (t * TOP_K) // block_rows + N_EXPERTS
    block_start = jnp.arange(n_blocks, dtype=I32) * block_rows
    in_expert = jnp.logical_and(block_start[:, None] >= pad_starts[None, :],
                                block_start[:, None] < pad_ends[None, :]).astype(I32)
    block_e = jnp.minimum(jnp.sum((block_start[:, None] >= pad_ends[None, :]).astype(I32), axis=1), N_EXPERTS - 1)
    n_valid = jnp.sum(in_expert * jnp.clip(pad_starts + counts - block_start[:, None], 0, block_rows), axis=1)
    n_used = pad_ends[-1:] // block_rows
    starts_col = jnp.broadcast_to(pad_starts.astype(F32)[:, None], (N_EXPERTS, LANES))

    dest = _dest(route, starts_col)
    dests = [dest[k].reshape(t // _sc_chunk(t), _sc_chunk(t)) for k in range(TOP_K)]
    xs = _dispatch(_as_rows3(hn), dests, n_blocks * block_rows)
    yb, experts = _moe(block_e, n_valid, n_used, _as_lines(xs), experts, block_rows)
    y0, y1 = _gather(_as_rows3(yb), dests)
    return _combine(x1, gate, _as_lines(y0), _as_lines(y1)), experts


def kernel(x_prompt, x_sample, norm_mix_g, w_in, norm_v_g, w_spatial, b_spatial, q_norm_g, k_norm_g, sink,
           w_proj_a, w_proj_b, w_out, norm_ffn_g, w_router_group, b_router_group, w_router_expert,
           b_router_expert, w_gate_e, w_up_e, w_down_e):
    depth = w_in.shape[0]
    layers = []
    for l in range(depth):
        w_router = jnp.zeros((ROUTER_ROWS, D_MODEL), F32)
        w_router = w_router.at[:N_GROUPS].set(w_router_group[l].T)
        w_router = w_router.at[SUBLANES:SUBLANES + N_EXPERTS].set(w_router_expert[l].T)
        b_router = jnp.zeros((ROUTER_ROWS,), F32)
        b_router = b_router.at[:N_GROUPS].set(b_router_group[l])
        b_router = b_router.at[SUBLANES:SUBLANES + N_EXPERTS].set(b_router_expert[l])
        b_router = jnp.broadcast_to(b_router[:, None], (ROUTER_ROWS, LANES))
        layers.append(dict(
            norm_mix_g=norm_mix_g[l][None], w_in=w_in[l].astype(BF16), norm_v_g=norm_v_g[l][None],
            w_spatial=w_spatial[l].astype(BF16),
            b_spatial=jnp.broadcast_to(b_spatial[l][:, :, None], (A_GROUPS, CHUNK, LANES)),
            q_norm_g=q_norm_g[l][None], k_norm_g=k_norm_g[l][None], sink=sink[l],
            w_proj_a=w_proj_a[l].astype(BF16), w_proj_b=w_proj_b[l].astype(BF16), w_out=w_out[l].astype(BF16),
            norm_ffn_g=norm_ffn_g[l][None], w_router=w_router.astype(BF16), b_router=b_router,
            experts=(w_gate_e[l], w_up_e[l], w_down_e[l])))

    trunks = [x_prompt, x_sample]
    rows = [x.reshape(-1, D_MODEL) for x in trunks]
    order = sorted(range(len(trunks)), key=lambda n: -rows[n].shape[0])
    rope = _rope_tables(max(x.shape[1] for x in trunks))
    for p in layers:
        experts = p["experts"]
        for n in order:
            rows[n], experts = _layer(rows[n], p, trunks[n].shape[1], rope, experts)
    return tuple(r.reshape(x.shape) for r, x in zip(rows, trunks))
```

```python
import functools

import jax
import jax.numpy as jnp
from jax import lax
from jax.experimental import pallas as pl
from jax.experimental.pallas import tpu as pltpu
from jax.experimental.pallas import tpu_sc as plsc

F32 = jnp.float32
BF16 = jnp.bfloat16
I32 = jnp.int32
U32 = jnp.uint32

LANES = 128
SUBLANES = 8
VMEM_BYTES_V7X = 64 * 1024 * 1024
SC_CORES = 2
SC_SUBCORES = 16
SC_WORKERS = SC_CORES * SC_SUBCORES
SC_CHUNK_MAX = 128

D_MODEL = 1024
A_WIDTH = D_MODEL
A_GROUPS = 8
CHUNK = 128
HEAD_DIM = 128
N_Q_HEADS = D_MODEL // HEAD_DIM
N_KV_HEADS = 2
REP = N_Q_HEADS // N_KV_HEADS
WINDOW = 128
ROPE_THETA = 10000.0
ROPE_SPLIT = 64
Q_W = N_Q_HEADS * HEAD_DIM
KV_W = N_KV_HEADS * HEAD_DIM
IN_W = 2 * A_WIDTH + Q_W + 2 * KV_W + 2 * D_MODEL
COL_U = 0
COL_V = COL_U + A_WIDTH
COL_Q = COL_V + A_WIDTH
COL_K = COL_Q + Q_W
COL_VA = COL_K + KV_W
COL_GA = COL_VA + KV_W
COL_GB = COL_GA + D_MODEL
N_GROUPS = 4
EXPERTS_PER_GROUP = 8
N_EXPERTS = N_GROUPS * EXPERTS_PER_GROUP
TOP_K = 2
EPS = 1e-6
NEG = -1e30

TM_IN = 512
TM_IN_SUB = 256
TQ = 512
DENSE_COLS = 256
ROUTER_ROWS = 64
assert EXPERTS_PER_GROUP == SUBLANES and SUBLANES + N_EXPERTS <= ROUTER_ROWS
TM_ROW = 1024
TM_DEST = 8192
MOE_ROWS_MAX = 1024
MOE_MIN_BLOCKS = 1
VMEM_RESERVE = 8 * 1024 * 1024
VMEM_LIMIT = VMEM_BYTES_V7X - VMEM_RESERVE


def _rms(x, g):
    return x * lax.rsqrt(jnp.mean(x * x, axis=-1, keepdims=True) + EPS) * g


ROW_LINES = D_MODEL // 2 // LANES
ROW3 = (ROW_LINES, LANES)
HIGH_HALF = 0xFFFF0000


def _as_rows3(a):
    return a.reshape((a.shape[0] // ROW_LINES,) + ROW3)


def _as_lines(a):
    return a.reshape((a.shape[0] * ROW_LINES, LANES))


def _store_rows3(lines_ref, val):
    rows = val.shape[0]
    bits = lax.bitcast_convert_type(val.astype(BF16).astype(F32), U32)
    half = ROW_LINES * LANES
    for s in range(ROW_LINES):
        lo = bits[:, s * LANES:(s + 1) * LANES] >> 16
        hi = bits[:, half + s * LANES:half + (s + 1) * LANES] & U32(HIGH_HALF)
        lines_ref[pl.ds(s, rows, stride=ROW_LINES), :] = lax.bitcast_convert_type(lo | hi, I32)


def _load_rows3(lines_ref):
    rows = lines_ref.shape[0] // ROW_LINES
    words = [lax.bitcast_convert_type(lines_ref[pl.ds(s, rows, stride=ROW_LINES), :], U32)
             for s in range(ROW_LINES)]
    lo = [lax.bitcast_convert_type(w << 16, F32) for w in words]
    hi = [lax.bitcast_convert_type(w & U32(HIGH_HALF), F32) for w in words]
    return jnp.concatenate(lo + hi, axis=1)


def _rows3_spec(rows, index_map):
    return pl.BlockSpec((rows * ROW_LINES, LANES), index_map)


def _const_spec(shape):
    nd = len(shape)
    return pl.BlockSpec(shape, lambda *_: (0,) * nd, pipeline_mode=pl.Buffered(1))


def _inproj_kernel(x_ref, gmix_ref, win_ref, gv_ref, ws_ref, bs_ref, gq_ref, gk_ref, cos_ref, sin_ref,
                   wpa_ref, ma_ref, sgb_ref, q_ref, k_ref, v_ref, h_scr, u_scr, vn_scr, a_scr):
    tm = x_ref.shape[0]

    def stages(rows):
        def proj(lo, width):
            return jnp.dot(h_scr[rows], win_ref[:, lo:lo + width], preferred_element_type=F32)

        def norm_rope(z, g):
            zn = _rms(z, g)
            return zn * cos_ref[rows] + pltpu.roll(zn, HEAD_DIM // 2, 1) * sin_ref[rows]

        def norm():
            h_scr[rows] = _rms(x_ref[rows], gmix_ref[...]).astype(BF16)

        def mix_v():
            vn_scr[rows] = _rms(jax.nn.gelu(proj(COL_V, A_WIDTH)), gv_ref[...]).astype(BF16)

        def mix_u():
            u_scr[rows] = jax.nn.gelu(proj(COL_U, A_WIDTH))

        def spatial():
            for c in range(rows.start, rows.stop, CHUNK):
                chunk = slice(c, c + CHUNK)
                for g in range(A_GROUPS):
                    cols = slice(g * LANES, (g + 1) * LANES)
                    mixed = jnp.dot(ws_ref[g], vn_scr[chunk, cols], preferred_element_type=F32) + bs_ref[g]
                    a_scr[chunk, cols] = (u_scr[chunk, cols] * mixed).astype(BF16)

        def gate_a():
            for lo in range(0, D_MODEL, D_MODEL // 2):
                cols = slice(lo, lo + D_MODEL // 2)
                ya = jnp.dot(a_scr[rows], wpa_ref[:, cols], preferred_element_type=F32)
                ma_ref[rows, cols] = jax.nn.sigmoid(proj(COL_GA + lo, D_MODEL // 2)) * ya

        def gate_b():
            for lo in range(0, D_MODEL, D_MODEL // 2):
                sgb_ref[rows, lo:lo + D_MODEL // 2] = jax.nn.sigmoid(proj(COL_GB + lo, D_MODEL // 2))

        def queries():
            qz = proj(COL_Q, Q_W)
            for hd in range(N_Q_HEADS):
                cols = slice(hd * HEAD_DIM, (hd + 1) * HEAD_DIM)
                q_ref[rows, cols] = norm_rope(qz[:, cols], gq_ref[...]).astype(BF16)

        def keys_values():
            kz = proj(COL_K, KV_W)
            for hd in range(N_KV_HEADS):
                cols = slice(hd * HEAD_DIM, (hd + 1) * HEAD_DIM)
                k_ref[rows, cols] = norm_rope(kz[:, cols], gk_ref[...]).astype(BF16)
            v_ref[rows] = proj(COL_VA, KV_W).astype(BF16)

        return [norm, mix_v, queries, mix_u, gate_b, keys_values, spatial, gate_a]

    subs = [stages(slice(r, r + TM_IN_SUB)) for r in range(0, tm, TM_IN_SUB)]
    n_stage = len(subs[0])
    for step in range(n_stage + len(subs) - 1):
        for n, sub in enumerate(subs):
            if 0 <= step - n < n_stage:
                sub[step - n]()


def _inproj(x, p, cos, sin, seq):
    t = x.shape[0]
    tm = TM_IN
    n_pos = seq // tm
    row = lambda w: pl.BlockSpec((tm, w), lambda i: (i, 0))
    pos = pl.BlockSpec((tm, HEAD_DIM), lambda i: (i % n_pos, 0))
    return pl.pallas_call(
        _inproj_kernel,
        grid=(t // tm,),
        in_specs=[row(D_MODEL), _const_spec((1, D_MODEL)), _const_spec((D_MODEL, IN_W)),
                  _const_spec((1, A_WIDTH)), _const_spec((A_GROUPS, CHUNK, CHUNK)),
                  _const_spec((A_GROUPS, CHUNK, LANES)), _const_spec((1, HEAD_DIM)),
                  _const_spec((1, HEAD_DIM)), pos, pos, _const_spec((A_WIDTH, D_MODEL))],
        out_specs=[row(D_MODEL), row(D_MODEL), row(Q_W), row(KV_W), row(KV_W)],
        out_shape=[jax.ShapeDtypeStruct((t, D_MODEL), F32), jax.ShapeDtypeStruct((t, D_MODEL), F32),
                   jax.ShapeDtypeStruct((t, Q_W), BF16), jax.ShapeDtypeStruct((t, KV_W), BF16),
                   jax.ShapeDtypeStruct((t, KV_W), BF16)],
        scratch_shapes=[pltpu.VMEM((tm, D_MODEL), BF16), pltpu.VMEM((tm, A_WIDTH), F32),
                        pltpu.VMEM((tm, A_WIDTH), BF16), pltpu.VMEM((tm, A_WIDTH), BF16)],
        compiler_params=pltpu.CompilerParams(dimension_semantics=("parallel",),
                                             vmem_limit_bytes=VMEM_LIMIT),
        name="inproj",
    )(x, p["norm_mix_g"], p["w_in"], p["norm_v_g"], p["w_spatial"], p["b_spatial"], p["q_norm_g"],
      p["k_norm_g"], cos, sin, p["w_proj_a"])


def _attn_kernel(sink_ref, x_ref, ma_ref, sgb_ref, q_ref, kp_ref, kc_ref, kn_ref, vp_ref, vc_ref, vn_ref,
                 wpb_ref, wout_ref, gffn_ref, wr_ref, br_ref,
                 x1_ref, hn_ref, route_ref, gate_ref, counts_ref, kcat, vcat, o_scr, s_scr, p_scr, sink_scr, m_scr,
                 *, tiles_per_seq, n_tiles):
    tq = x_ref.shape[0]
    blk = WINDOW
    i = pl.program_id(0)
    slot = i % 2

    @pl.when(i == 0)
    def _():
        o_scr[...] = jnp.zeros_like(o_scr)
        counts_ref[...] = jnp.zeros_like(counts_ref)

    pos_tile = jnp.minimum(i, n_tiles - 1) % tiles_per_seq
    has_prev = pos_tile > 0
    has_next = pos_tile < tiles_per_seq - 1

    kcat[0:blk] = kp_ref[...]
    kcat[blk:blk + tq] = kc_ref[...]
    kcat[blk + tq:] = kn_ref[...]
    vcat[0:blk] = vp_ref[...]
    vcat[blk:blk + tq] = vc_ref[...]
    vcat[blk + tq:] = vn_ref[...]

    qr = lax.broadcasted_iota(I32, (blk, blk), 0)
    kc = lax.broadcasted_iota(I32, (blk, blk), 1)
    scale = HEAD_DIM ** -0.5
    n_sub = tq // blk
    pairs = [(j, g) for j in range(n_sub) for g in range(N_KV_HEADS)]

    def keys(ref, j, g):
        return ref[j * blk:(j + 3) * blk, g * HEAD_DIM:(g + 1) * HEAD_DIM]

    def head_cols(g, r):
        hd = g * REP + r
        return slice(hd * HEAD_DIM, (hd + 1) * HEAD_DIM)

    def scores(b):
        j, g = pairs[b]
        rows = slice(j * blk, (j + 1) * blk)
        qs = jnp.concatenate([q_ref[rows, head_cols(g, r)] for r in range(REP)], axis=0)
        s_scr[b] = lax.dot_general(qs, keys(kcat, j, g), (((1,), (1,)), ((), ())),
                                   preferred_element_type=F32)

    log2e = 1.4426950408889634

    def softmax(b, r):
        j, g = pairs[b]
        hrows = slice(r * blk, (r + 1) * blk)
        z = s_scr[b, hrows, :] * (scale * log2e)
        lo_ok = kc >= (qr + jnp.where(has_prev, 0, blk) if j == 0 else qr)
        hi_ok = kc <= (qr - jnp.where(has_next, 0, blk) if j == n_sub - 1 else qr)
        z = jnp.concatenate([jnp.where(lo_ok, z[:, :blk], NEG), z[:, blk:2 * blk],
                             jnp.where(hi_ok, z[:, 2 * blk:], NEG)], axis=1)
        sink = sink_ref[g * REP + r] * log2e
        m = jnp.maximum(jnp.max(z, axis=-1, keepdims=True), sink)
        p_scr[b, hrows, :] = jnp.exp2(z - m).astype(BF16)
        sink_scr[b, hrows, :] = jnp.broadcast_to(jnp.exp2(sink - m), (blk, LANES))

    def values(b):
        j, g = pairs[b]
        rows = slice(j * blk, (j + 1) * blk)
        v_ext = jnp.concatenate([keys(vcat, j, g), jnp.ones((3 * blk, HEAD_DIM), BF16)], axis=1)
        acc = jnp.dot(p_scr[b], v_ext, preferred_element_type=F32)
        o = (acc[:, :HEAD_DIM] / (acc[:, HEAD_DIM:] + sink_scr[b])).astype(BF16)
        for r in range(REP):
            o_scr[slot, rows, head_cols(g, r)] = o[r * blk:(r + 1) * blk, :]

    def merged_cols(cols):
        yb = jnp.dot(o_scr[1 - slot], wpb_ref[:, cols], preferred_element_type=F32)
        m_scr[:, cols] = (ma_ref[:, cols] + sgb_ref[:, cols] * yb).astype(BF16)

    def x1_cols(cols):
        x1_ref[:, cols] = x_ref[:, cols] + jnp.dot(m_scr[...], wout_ref[:, cols], preferred_element_type=F32)

    col_chunks = [slice(c * DENSE_COLS, (c + 1) * DENSE_COLS) for c in range(D_MODEL // DENSE_COLS)]
    dense = [functools.partial(f, cols) for f in (merged_cols, x1_cols) for cols in col_chunks]
    units = [(b, r) for b in range(len(pairs)) for r in range(REP)]
    units_per_dense = len(units) // len(dense)
    scores(0)
    for n, (b, r) in enumerate(units):
        if r == 0 and b + 1 < len(pairs):
            scores(b + 1)
        softmax(b, r)
        if (n + 1) % units_per_dense == 0:
            dense[(n + 1) // units_per_dense - 1]()
        if r == 0 and b > 0:
            values(b - 1)
    values(len(pairs) - 1)

    hn = _rms(x1_ref[...], gffn_ref[...])
    _store_rows3(hn_ref, hn)
    def wide(a):
        return jnp.concatenate([a] * (tq // LANES), axis=1)

    logits = lax.dot_general(wr_ref[...], hn.astype(BF16), (((1,), (1,)), ((), ())),
                             preferred_element_type=F32) + wide(br_ref[...])
    sub = lax.broadcasted_iota(I32, (SUBLANES, tq), 0).astype(F32)
    ninf = -jnp.inf

    def cmax(a):
        return jnp.max(a, axis=0, keepdims=True)

    def csum(a):
        return jnp.sum(a, axis=0, keepdims=True)

    def first_row(mask):
        return jnp.min(jnp.where(mask, sub, float(SUBLANES)), axis=0, keepdims=True)

    def group_rows(g):
        return logits[(g + 1) * SUBLANES:(g + 2) * SUBLANES]

    gl = jnp.where(sub < N_GROUPS, logits[0:SUBLANES], ninf)
    gmax = cmax(gl)
    g_sel = first_row(gl == gmax)
    g_p = 1.0 / csum(jnp.exp(gl - gmax))
    el = group_rows(0)
    for g in range(1, N_GROUPS):
        el = jnp.where(g_sel == g, group_rows(g), el)
    ee = jnp.exp(el - cmax(el))
    eprob = ee / csum(ee)
    p1 = cmax(eprob)
    i1 = first_row(eprob == p1)
    eprob2 = jnp.where(sub == i1, -1.0, eprob)
    p2 = cmax(eprob2)
    i2 = first_row(eprob2 == p2)
    psum = p1 + p2
    w1 = g_p * p1 / psum
    w2 = g_p * p2 / psum
    e1 = g_sel * EXPERTS_PER_GROUP + i1
    e2 = g_sel * EXPERTS_PER_GROUP + i2

    erow = lax.broadcasted_iota(I32, (N_EXPERTS, tq), 0).astype(F32)
    oh1 = erow == e1
    oh2 = erow == e2
    cnt = (jnp.where(oh1, 1.0, 0.0) + jnp.where(oh2, 1.0, 0.0)) * jnp.where(i > 0, 1.0, 0.0)
    ri = lax.broadcasted_iota(I32, (tq, tq), 0)
    ci = lax.broadcasted_iota(I32, (tq, tq), 1)
    earlier = jnp.where(ri < ci, 1.0, 0.0).astype(BF16)
    base = wide(counts_ref[...]) + jnp.dot(cnt.astype(BF16), earlier, preferred_element_type=F32)
    r1 = csum(jnp.where(oh1, base, 0.0))
    r2 = csum(jnp.where(oh2, base, 0.0))
    counts_ref[...] = counts_ref[...] + jnp.sum(cnt, axis=1, keepdims=True)

    route = jnp.where(sub == 0.0, e1, jnp.where(sub == 1.0, e2,
                      jnp.where(sub == 2.0, r1, jnp.where(sub == 3.0, r2, 0.0))))
    route_ref[...] = route.astype(I32)
    gate_ref[...] = jnp.where(sub == 0.0, w1, jnp.where(sub == 1.0, w2, 0.0))


def _attn(x, ma, sgb, q, k, v, p, seq):
    t = x.shape[0]
    tq = TQ
    sub = tq // WINDOW
    last_blk = t // WINDOW - 1
    n_tiles = t // tq
    att = lambda i: jnp.minimum(i, n_tiles - 1)
    post = lambda i: jnp.maximum(i - 1, 0)
    att_row = lambda w: pl.BlockSpec((tq, w), lambda i: (att(i), 0))
    row = lambda w: pl.BlockSpec((tq, w), lambda i: (post(i), 0))
    prev = pl.BlockSpec((WINDOW, KV_W), lambda i: (jnp.maximum(att(i) * sub - 1, 0), 0))
    nxt = pl.BlockSpec((WINDOW, KV_W), lambda i: (jnp.minimum((att(i) + 1) * sub, last_blk), 0))
    return pl.pallas_call(
        functools.partial(_attn_kernel, tiles_per_seq=seq // tq, n_tiles=n_tiles),
        grid=(n_tiles + 1,),
        in_specs=[pl.BlockSpec(memory_space=pltpu.SMEM),
                  row(D_MODEL), row(D_MODEL), row(D_MODEL), att_row(Q_W),
                  prev, att_row(KV_W), nxt, prev, att_row(KV_W), nxt,
                  _const_spec((Q_W, D_MODEL)), _const_spec((D_MODEL, D_MODEL)), _const_spec((1, D_MODEL)),
                  _const_spec((ROUTER_ROWS, D_MODEL)), _const_spec((ROUTER_ROWS, LANES))],
        out_specs=[row(D_MODEL), _rows3_spec(tq, lambda i: (post(i), 0)),
                   pl.BlockSpec((SUBLANES, tq), lambda i: (0, post(i))),
                   pl.BlockSpec((SUBLANES, tq), lambda i: (0, post(i))),
                   pl.BlockSpec((N_EXPERTS, LANES), lambda i: (0, 0))],
        out_shape=[jax.ShapeDtypeStruct((t, D_MODEL), F32), jax.ShapeDtypeStruct((t * ROW_LINES, LANES), I32),
                   jax.ShapeDtypeStruct((SUBLANES, t), I32), jax.ShapeDtypeStruct((SUBLANES, t), F32),
                   jax.ShapeDtypeStruct((N_EXPERTS, LANES), F32)],
        scratch_shapes=[pltpu.VMEM((tq + 2 * WINDOW, KV_W), BF16), pltpu.VMEM((tq + 2 * WINDOW, KV_W), BF16),
                        pltpu.VMEM((2, tq, Q_W), BF16),
                        pltpu.VMEM((sub * N_KV_HEADS, REP * WINDOW, 3 * WINDOW), F32),
                        pltpu.VMEM((sub * N_KV_HEADS, REP * WINDOW, 3 * WINDOW), BF16),
                        pltpu.VMEM((sub * N_KV_HEADS, REP * WINDOW, LANES), F32),
                        pltpu.VMEM((tq, D_MODEL), BF16)],
        compiler_params=pltpu.CompilerParams(dimension_semantics=("arbitrary",),
                                             vmem_limit_bytes=VMEM_LIMIT),
        name="attn",
    )(p["sink"], x, ma, sgb, q, k, k, k, v, v, v, p["w_proj_b"], p["w_out"], p["norm_ffn_g"],
      p["w_router"], p["b_router"])


def _dest_kernel(route_ref, starts_ref, dest_ref):
    route = route_ref[...].astype(F32)
    td = route.shape[1]
    sub = lax.broadcasted_iota(I32, route.shape, 0)
    erow = lax.broadcasted_iota(I32, (N_EXPERTS, td), 0).astype(F32)
    starts = jnp.broadcast_to(starts_ref[:, 0:1], (N_EXPERTS, td))

    def slot(k):
        start = jnp.sum(jnp.where(erow == route[k:k + 1], starts, 0.0), axis=0, keepdims=True)
        return start + route[TOP_K + k:TOP_K + k + 1]

    dest_ref[...] = jnp.where(sub == 0, slot(0), jnp.where(sub == 1, slot(1), 0.0)).astype(I32)


def _dest(route, pad_starts):
    t = route.shape[1]
    td = min(TM_DEST, t)
    return pl.pallas_call(
        _dest_kernel,
        grid=(t // td,),
        in_specs=[pl.BlockSpec((SUBLANES, td), lambda i: (0, i)), _const_spec((N_EXPERTS, LANES))],
        out_specs=pl.BlockSpec((SUBLANES, td), lambda i: (0, i)),
        out_shape=jax.ShapeDtypeStruct((SUBLANES, t), I32),
        compiler_params=pltpu.CompilerParams(dimension_semantics=("parallel",)),
        name="dest",
    )(route, pad_starts)


def _sc_mesh():
    return plsc.VectorSubcoreMesh(core_axis_name="c", subcore_axis_name="s")


def _sc_worker():
    return lax.axis_index("s") * SC_CORES + lax.axis_index("c")


def _sc_chunk(t):
    return min(SC_CHUNK_MAX, t // (SC_WORKERS * SUBLANES))


def _dispatch(hn, dests, n_rows):
    t = hn.shape[0]
    chunk = dests[0].shape[1]
    per_worker = t // SC_WORKERS
    n_chunks = per_worker // chunk
    idx = pltpu.VMEM((n_chunks, chunk), I32)

    @functools.partial(
        pl.kernel, mesh=_sc_mesh(), out_type=jax.ShapeDtypeStruct((n_rows,) + ROW3, I32),
        scratch_types=[idx, idx, pltpu.VMEM((chunk,) + ROW3, I32), pltpu.SemaphoreType.DMA])
    def scatter_rows(hn_hbm, d0_hbm, d1_hbm, xs_hbm, i0_v, i1_v, rows_v, sem):
        w = _sc_worker()
        pltpu.sync_copy(d0_hbm.at[pl.ds(w * n_chunks, n_chunks)], i0_v)
        pltpu.sync_copy(d1_hbm.at[pl.ds(w * n_chunks, n_chunks)], i1_v)

        @pl.loop(0, n_chunks)
        def _(j):
            pltpu.sync_copy(hn_hbm.at[pl.ds(w * per_worker + j * chunk, chunk)], rows_v)
            copies = [pltpu.make_async_copy(rows_v, xs_hbm.at[i_v.at[j]], sem) for i_v in (i0_v, i1_v)]
            for cp in copies:
                cp.start()
            for cp in copies:
                cp.wait()

    return scatter_rows(hn, *dests)


def _moe_block_rows(t):
    rows = MOE_ROWS_MAX
    while rows > CHUNK and (t * TOP_K) // N_EXPERTS < MOE_MIN_BLOCKS * rows:
        rows //= 2
    return rows


def _moe_kernel(be_ref, nvalid_ref, nused_ref, xs_ref, wg_ref, wu_ref, wd_ref, yb_ref, *w16_refs_and_acc):
    *w16_refs, acc_scr = w16_refs_and_acc
    i = pl.program_id(0)
    used = i < nused_ref[0]
    weights = w16_refs if w16_refs else (wg_ref, wu_ref, wd_ref)

    @pl.when(jnp.logical_not(used))
    def _():
        yb_ref[...] = jnp.zeros_like(yb_ref)

    @pl.when(used)
    def _():
        if w16_refs:
            @pl.when(jnp.logical_or(i == 0, be_ref[i] != be_ref[jnp.maximum(i - 1, 0)]))
            def _():
                for src, dst in zip((wg_ref, wu_ref, wd_ref), w16_refs):
                    dst[0] = src[0].astype(BF16)

        row = lax.broadcasted_iota(I32, (xs_ref.shape[0] // ROW_LINES, 1), 0)
        x = jnp.where(row < nvalid_ref[i], _load_rows3(xs_ref), 0.0).astype(BF16)
        half = weights[0].shape[2] // 2
        rows = x.shape[0] // 2

        def hidden(cols):
            gate = jnp.dot(x, weights[0][0, :, cols], preferred_element_type=F32)
            up = jnp.dot(x, weights[1][0, :, cols], preferred_element_type=F32)
            return (jax.nn.silu(gate) * up).astype(BF16)

        lo, hi = slice(0, half), slice(half, 2 * half)
        hid_lo = hidden(lo)
        hid_hi = hidden(hi)
        for r0 in (0, rows):
            acc_scr[r0:r0 + rows] = jnp.dot(hid_lo[r0:r0 + rows], weights[2][0, lo, :], preferred_element_type=F32)
        for r0 in (0, rows):
            y = acc_scr[r0:r0 + rows] + jnp.dot(hid_hi[r0:r0 + rows], weights[2][0, hi, :],
                                                preferred_element_type=F32)
            _store_rows3(yb_ref.at[pl.ds(r0 * ROW_LINES, rows * ROW_LINES)], y)


def _moe(block_e, n_valid, n_used, xs, experts, block_rows):
    n_blocks = xs.shape[0] // (block_rows * ROW_LINES)
    cast = experts[0].dtype != BF16

    def rows(i, be, nv, nu):
        return (jnp.minimum(i, nu[0] - 1), 0)

    def expert(i, be, nv, nu):
        return (be[jnp.minimum(i, nu[0] - 1)], 0, 0)

    w_specs = [pl.BlockSpec((1,) + w.shape[1:], expert) for w in experts]
    out_specs = [_rows3_spec(block_rows, lambda i, be, nv, nu: (i, 0))]
    out_shape = [jax.ShapeDtypeStruct(xs.shape, I32)]
    if cast:
        out_specs += w_specs
        out_shape += [jax.ShapeDtypeStruct(w.shape, BF16) for w in experts]
    yb, *w16 = pl.pallas_call(
        _moe_kernel,
        grid_spec=pltpu.PrefetchScalarGridSpec(
            num_scalar_prefetch=3,
            grid=(n_blocks,),
            in_specs=[_rows3_spec(block_rows, rows)] + w_specs,
            out_specs=out_specs,
            scratch_shapes=[pltpu.VMEM((block_rows, D_MODEL), F32)],
        ),
        out_shape=out_shape,
        compiler_params=pltpu.CompilerParams(dimension_semantics=("arbitrary",),
                                             vmem_limit_bytes=VMEM_LIMIT),
        name="moe",
    )(block_e, n_valid, n_used, xs, *experts)
    return yb, (tuple(w16) if cast else experts)


def _gather(yb, dests):
    chunk = dests[0].shape[1]
    t = dests[0].shape[0] * chunk
    per_worker = t // SC_WORKERS
    n_chunks = per_worker // chunk
    idx = pltpu.VMEM((n_chunks, chunk), I32)
    out = jax.ShapeDtypeStruct((t,) + ROW3, I32)

    @functools.partial(
        pl.kernel, mesh=_sc_mesh(), out_type=(out, out),
        scratch_types=[idx, idx, pltpu.VMEM((chunk,) + ROW3, I32), pltpu.SemaphoreType.DMA])
    def gather_rows(yb_hbm, d0_hbm, d1_hbm, y0_hbm, y1_hbm, i0_v, i1_v, rows_v, sem):
        w = _sc_worker()
        pltpu.sync_copy(d0_hbm.at[pl.ds(w * n_chunks, n_chunks)], i0_v)
        pltpu.sync_copy(d1_hbm.at[pl.ds(w * n_chunks, n_chunks)], i1_v)

        @pl.loop(0, n_chunks)
        def _(j):
            rows = pl.ds(w * per_worker + j * chunk, chunk)
            for i_v, y_hbm in ((i0_v, y0_hbm), (i1_v, y1_hbm)):
                pltpu.async_copy(yb_hbm.at[i_v.at[j]], rows_v, sem).wait()
                pltpu.sync_copy(rows_v, y_hbm.at[rows])

    return gather_rows(yb, *dests)


def _combine_kernel(x1_ref, gate_ref, y0_ref, y1_ref, out_ref):
    tm = x1_ref.shape[0]
    gate = jnp.concatenate([gate_ref[...], jnp.zeros((LANES - SUBLANES, tm), F32)], axis=0).T
    out_ref[...] = x1_ref[...] + (_load_rows3(y0_ref) * gate[:, 0:1] + _load_rows3(y1_ref) * gate[:, 1:2])


def _combine(x1, gate, y0, y1):
    t = x1.shape[0]
    tm = TM_ROW
    return pl.pallas_call(
        _combine_kernel,
        grid=(t // tm,),
        in_specs=[pl.BlockSpec((tm, D_MODEL), lambda i: (i, 0)),
                  pl.BlockSpec((SUBLANES, tm), lambda i: (0, i)),
                  _rows3_spec(tm, lambda i: (i, 0)), _rows3_spec(tm, lambda i: (i, 0))],
        out_specs=pl.BlockSpec((tm, D_MODEL), lambda i: (i, 0)),
        out_shape=jax.ShapeDtypeStruct((t, D_MODEL), F32),
        compiler_params=pltpu.CompilerParams(dimension_semantics=("parallel",)),
        name="combine",
    )(x1, gate, y0, y1)


def _rope_tables(seq):
    half = HEAD_DIM // 2
    inv_freq = ROPE_THETA ** (-jnp.arange(half, dtype=F32) / half)
    coarse = (jnp.arange(seq // ROPE_SPLIT) * ROPE_SPLIT).astype(F32)[:, None, None] * inv_freq
    fine = jnp.arange(ROPE_SPLIT).astype(F32)[None, :, None] * inv_freq
    cos = (jnp.cos(coarse) * jnp.cos(fine) - jnp.sin(coarse) * jnp.sin(fine)).reshape(seq, half)
    sin = (jnp.sin(coarse) * jnp.cos(fine) + jnp.cos(coarse) * jnp.sin(fine)).reshape(seq, half)
    return jnp.concatenate([cos, cos], axis=-1), jnp.concatenate([-sin, sin], axis=-1)


def _layer(x, p, seq, rope, experts):
    t = x.shape[0]
    ma, sgb, q, k, v = _inproj(x, p, *rope, seq)
    x1, hn, route, gate, counts_f = _attn(x, ma, sgb, q, k, v, p, seq)

    block_rows = _moe_block_rows(t)
    counts = counts_f[:, 0].astype(I32)
    min_blocks = 1 if experts[0].dtype != BF16 else 0
    padded = jnp.maximum((counts + block_rows - 1) // block_rows, min_blocks) * block_rows
    pad_ends = jnp.cumsum(padded)
    pad_starts = pad_ends - padded
    n_blocks = (t * TOP_K) // block_rows + N_EXPERTS
    block_start = jnp.arange(n_blocks, dtype=I32) * block_rows
    in_expert = jnp.logical_and(block_start[:, None] >= pad_starts[None, :],
                                block_start[:, None] < pad_ends[None, :]).astype(I32)
    block_e = jnp.minimum(jnp.sum((block_start[:, None] >= pad_ends[None, :]).astype(I32), axis=1), N_EXPERTS - 1)
    n_valid = jnp.sum(in_expert * jnp.clip(pad_starts + counts - block_start[:, None], 0, block_rows), axis=1)
    n_used = pad_ends[-1:] // block_rows
    starts_col = jnp.broadcast_to(pad_starts.astype(F32)[:, None], (N_EXPERTS, LANES))

    dest = _dest(route, starts_col)
    dests = [dest[k].reshape(t // _sc_chunk(t), _sc_chunk(t)) for k in range(TOP_K)]
    xs = _dispatch(_as_rows3(hn), dests, n_blocks * block_rows)
    yb, experts = _moe(block_e, n_valid, n_used, _as_lines(xs), experts, block_rows)
    y0, y1 = _gather(_as_rows3(yb), dests)
    return _combine(x1, gate, _as_lines(y0), _as_lines(y1)), experts


def kernel(x_prompt, x_sample, norm_mix_g, w_in, norm_v_g, w_spatial, b_spatial, q_norm_g, k_norm_g, sink,
           w_proj_a, w_proj_b, w_out, norm_ffn_g, w_router_group, b_router_group, w_router_expert,
           b_router_expert, w_gate_e, w_up_e, w_down_e):
    depth = w_in.shape[0]
    layers = []
    for l in range(depth):
        w_router = jnp.zeros((ROUTER_ROWS, D_MODEL), F32)
        w_router = w_router.at[:N_GROUPS].set(w_router_group[l].T)
        w_router = w_router.at[SUBLANES:SUBLANES + N_EXPERTS].set(w_router_expert[l].T)
        b_router = jnp.zeros((ROUTER_ROWS,), F32)
        b_router = b_router.at[:N_GROUPS].set(b_router_group[l])
        b_router = b_router.at[SUBLANES:SUBLANES + N_EXPERTS].set(b_router_expert[l])
        b_router = jnp.broadcast_to(b_router[:, None], (ROUTER_ROWS, LANES))
        layers.append(dict(
            norm_mix_g=norm_mix_g[l][None], w_in=w_in[l].astype(BF16), norm_v_g=norm_v_g[l][None],
            w_spatial=w_spatial[l].astype(BF16),
            b_spatial=jnp.broadcast_to(b_spatial[l][:, :, None], (A_GROUPS, CHUNK, LANES)),
            q_norm_g=q_norm_g[l][None], k_norm_g=k_norm_g[l][None], sink=sink[l],
            w_proj_a=w_proj_a[l].astype(BF16), w_proj_b=w_proj_b[l].astype(BF16), w_out=w_out[l].astype(BF16),
            norm_ffn_g=norm_ffn_g[l][None], w_router=w_router.astype(BF16), b_router=b_router,
            experts=(w_gate_e[l], w_up_e[l], w_down_e[l])))

    trunks = [x_prompt, x_sample]
    rows = [x.reshape(-1, D_MODEL) for x in trunks]
    order = sorted(range(len(trunks)), key=lambda n: -rows[n].shape[0])
    rope = _rope_tables(max(x.shape[1] for x in trunks))
    for p in layers:
        experts = p["experts"]
        for n in order:
            rows[n], experts = _layer(rows[n], p, trunks[n].shape[1], rope, experts)
    return tuple(r.reshape(x.shape) for r, x in zip(rows, trunks))
```
